```python
import math
import jax
import jax.numpy as jnp
from jax import lax
import numpy as np

D_MODEL = 2048
BATCH = 8
SEQ = 2048
DEPTH = 4

N_MIXERS = 2
N_A_LAYERS = (DEPTH + 1) // 2
N_B_LAYERS = DEPTH // 2
EPS = 1e-6
NEG = -1e30

A_HEAD_DIM = 128
A_HEADS_PER_GROUP = 4
A_PATTERNS = ((128, 1), (512, 4), (2048, 16))
A_N_GROUPS = len(A_PATTERNS)
A_HEADS = A_HEADS_PER_GROUP * A_N_GROUPS
A_QKV_W = A_HEADS * A_HEAD_DIM
A_OUT_W = A_HEADS_PER_GROUP * A_HEAD_DIM
QBLK = 128

CHUNK = 128
B_GROUPS = 12
B_GROUP_W = 128
B_W = B_GROUPS * B_GROUP_W

MEM_LEN = 256
MEM_HEADS = 4
MEM_HEAD_DIM = 128
MEM_W = MEM_HEADS * MEM_HEAD_DIM

A_IN = 3 * A_QKV_W + MEM_W
A_OUT_IN = A_OUT_W + MEM_W
B_IN = 2 * B_W + MEM_W
B_OUT_IN = B_W + MEM_W

FF = 5632
CONV_W = 3

kernel_name = "hybrid_dilated_sgu_memory_encoder"


def rmsnorm(x, g):
    xf = x.astype(jnp.float32)
    y = xf * lax.rsqrt(jnp.mean(xf * xf, axis=-1, keepdims=True) + EPS)
    return (y * g.astype(jnp.float32)).astype(x.dtype)


def alibi_slopes():
    return (2.0 ** (-8.0 * (np.arange(A_HEADS) + 1) / A_HEADS)).astype(np.float32)


def dilated_window_attention(q, k, v, dilation, n_side, slopes):
    B, S, H, E = q.shape
    L = S // dilation
    nblk = -(-L // QBLK)
    Lp = nblk * QBLK
    W = QBLK + 2 * n_side
    qs = q.reshape(B, L, dilation, H, E)
    ks = k.reshape(B, L, dilation, H, E)
    vs = v.reshape(B, L, dilation, H, E)
    qb = jnp.pad(qs, ((0, 0), (0, Lp - L), (0, 0), (0, 0), (0, 0)))
    qb = qb.reshape(B, nblk, QBLK, dilation, H, E)
    pad_k = ((0, 0), (n_side, n_side + Lp - L), (0, 0), (0, 0), (0, 0))
    kp = jnp.pad(ks, pad_k)
    vp = jnp.pad(vs, pad_k)
    idx = np.arange(nblk)[:, None] * QBLK + np.arange(W)[None, :]
    kb = kp[:, idx]
    vb = vp[:, idx]
    s = jnp.einsum('bnqrhe,bnkrhe->bnrhqk', qb.astype(jnp.float32),
                   kb.astype(jnp.float32)) * (E ** -0.5)
    rel = np.arange(W)[None, :] - n_side - np.arange(QBLK)[:, None]
    band = np.abs(rel) <= n_side
    jk = np.arange(nblk)[:, None] * QBLK - n_side + np.arange(W)[None, :]
    valid = (jk >= 0) & (jk < L)
    mask = band[None, :, :] & valid[:, None, :]
    dist = (np.abs(rel) * dilation).astype(np.float32)
    alibi = -slopes[:, None, None] * dist[None]
    s = s + alibi[None, None, None]
    s = jnp.where(mask[None, :, None, None], s, NEG)
    lse = jax.nn.logsumexp(s, axis=-1)
    p = jnp.exp(s - lse[..., None])
    o = jnp.einsum('bnrhqk,bnkrhe->bnqrhe', p, vb.astype(jnp.float32))
    o = o.reshape(B, Lp, dilation, H, E)[:, :L].reshape(B, S, H, E)
    lse = lse.transpose(0, 1, 4, 2, 3).reshape(B, Lp, dilation, H)[:, :L].reshape(B, S, H)
    return o, lse


def mixer_a(proj):
    B, S, _ = proj.shape
    qkv = proj.reshape(B, S, 3, A_N_GROUPS, A_HEADS_PER_GROUP, A_HEAD_DIM)
    slopes_all = jnp.asarray(alibi_slopes())
    outs, lses = [], []
    for g, (window, dilation) in enumerate(A_PATTERNS):
        n_side = (window // 2) // dilation
        sl = slopes_all[g * A_HEADS_PER_GROUP:(g + 1) * A_HEADS_PER_GROUP]
        o, l = dilated_window_attention(qkv[:, :, 0, g], qkv[:, :, 1, g],
                                        qkv[:, :, 2, g], dilation, n_side, sl)
        outs.append(o)
        lses.append(l)
    outs = jnp.stack(outs)
    wts = jax.nn.softmax(jnp.stack(lses), axis=0)
    comb = jnp.sum(wts[..., None] * outs, axis=0)
    return comb.reshape(B, S, A_OUT_W).astype(proj.dtype)


def mixer_b(proj_uv, v_norm_g, w_s, s_bias):
    B, S, _ = proj_uv.shape
    uv = jax.nn.gelu(proj_uv, approximate=False)
    u, v = uv[..., :B_W], uv[..., B_W:]
    v = rmsnorm(v, v_norm_g)
    vc = v.reshape(B, S // CHUNK, CHUNK, B_GROUPS, B_GROUP_W)
    mixed = jnp.einsum('gpq,bcqge->bcpge', w_s, vc) + s_bias.T[None, None, :, :, None]
    return u * mixed.reshape(B, S, B_W)


def memory_cross_attention(qm, mem_n, w_kv):
    B, S, _ = qm.shape
    M = mem_n.shape[1]
    kv = jnp.einsum('bmd,df->bmf', mem_n, w_kv).reshape(B, M, 2, MEM_HEADS, MEM_HEAD_DIM)
    q = qm.reshape(B, S, MEM_HEADS, MEM_HEAD_DIM).astype(jnp.float32)
    k = kv[:, :, 0].astype(jnp.float32)
    v = kv[:, :, 1].astype(jnp.float32)
    s = jnp.einsum('bshe,bmhe->bhsm', q, k) * (MEM_HEAD_DIM ** -0.5)
    p = jax.nn.softmax(s, axis=-1)
    o = jnp.einsum('bhsm,bmhe->bshe', p, v)
    return o.reshape(B, S, MEM_W).astype(qm.dtype)


def conv_ffn(h, w_up, conv_w, conv_b, w_down):
    a = jnp.einsum('bsd,df->bsf', h, w_up)
    ap = jnp.pad(a, ((0, 0), (1, 1), (0, 0)))
    a = ap[:, :-2] * conv_w[0] + ap[:, 1:-1] * conv_w[1] + ap[:, 2:] * conv_w[2] + conv_b
    gate, val = a[..., :FF], a[..., FF:]
    return jnp.einsum('bsf,fd->bsd', jax.nn.gelu(gate, approximate=False) * val, w_down)


def _fwd_setup_inputs(seed: int = 0) -> dict:
    key = jax.random.key(seed)
    ks = jax.random.split(key, 20)

    def nrm(k, shape, scale):
        return jax.random.normal(k, shape, jnp.float32) * scale

    D = D_MODEL
    return {
        "x": nrm(ks[0], (BATCH, SEQ, D), 1.0),
        "mem": nrm(ks[1], (BATCH, MEM_LEN, D), 1.0),
        "mix_norm_g": 1.0 + nrm(ks[2], (DEPTH, D), 0.02),
        "ffn_norm_g": 1.0 + nrm(ks[3], (DEPTH, D), 0.02),
        "mem_norm_g": 1.0 + nrm(ks[4], (DEPTH, D), 0.02),
        "w_mem_kv": nrm(ks[5], (DEPTH, D, 2 * MEM_W), D ** -0.5),
        "a_w_in": nrm(ks[6], (N_A_LAYERS, D, A_IN), D ** -0.5),
        "a_w_out": nrm(ks[7], (N_A_LAYERS, A_OUT_IN, D), A_OUT_IN ** -0.5),
        "b_w_in": nrm(ks[8], (N_B_LAYERS, D, B_IN), D ** -0.5),
        "b_v_norm_g": 1.0 + nrm(ks[9], (N_B_LAYERS, B_W), 0.02),
        "b_w_s": nrm(ks[10], (N_B_LAYERS, B_GROUPS, CHUNK, CHUNK), CHUNK ** -0.5),
        "b_s_bias": 1.0 + nrm(ks[11], (N_B_LAYERS, B_GROUPS, CHUNK), 0.02),
        "b_w_out": nrm(ks[12], (N_B_LAYERS, B_OUT_IN, D), B_OUT_IN ** -0.5),
        "ffn_w_up": nrm(ks[13], (DEPTH, D, 2 * FF), D ** -0.5),
        "ffn_conv_w": nrm(ks[14], (DEPTH, CONV_W, 2 * FF), CONV_W ** -0.5),
        "ffn_conv_b": nrm(ks[15], (DEPTH, 2 * FF), 0.02),
        "ffn_w_down": nrm(ks[16], (DEPTH, FF, D), FF ** -0.5),
        "final_norm_g": 1.0 + nrm(ks[17], (D,), 0.02),
    }


def _fwd_reference(x, mem, mix_norm_g, ffn_norm_g, mem_norm_g, w_mem_kv, a_w_in, a_w_out,
              b_w_in, b_v_norm_g, b_w_s, b_s_bias, b_w_out, ffn_w_up, ffn_conv_w,
              ffn_conv_b, ffn_w_down, final_norm_g):
    for i in range(DEPTH):
        h = rmsnorm(x, mix_norm_g[i])
        mem_n = rmsnorm(mem, mem_norm_g[i])
        j = i // N_MIXERS
        if i % N_MIXERS == 0:
            proj = jnp.einsum('bsd,df->bsf', h, a_w_in[j])
            tok = mixer_a(proj[..., :3 * A_QKV_W])
            mem_out = memory_cross_attention(proj[..., 3 * A_QKV_W:], mem_n, w_mem_kv[i])
            w_out = a_w_out[j]
        else:
            proj = jnp.einsum('bsd,df->bsf', h, b_w_in[j])
            tok = mixer_b(proj[..., :2 * B_W], b_v_norm_g[j], b_w_s[j], b_s_bias[j])
            mem_out = memory_cross_attention(proj[..., 2 * B_W:], mem_n, w_mem_kv[i])
            w_out = b_w_out[j]
        cat = jnp.concatenate([tok, mem_out], axis=-1)
        x = x + jnp.einsum('bsf,fd->bsd', cat, w_out)
        h = rmsnorm(x, ffn_norm_g[i])
        x = x + conv_ffn(h, ffn_w_up[i], ffn_conv_w[i], ffn_conv_b[i], ffn_w_down[i])
    return rmsnorm(x, final_norm_g)


import jax as _jax
import jax.numpy as _jnp

TWIN_FORMAT = 'train_step'
FWD_PARAMS = ['x', 'mem', 'mix_norm_g', 'ffn_norm_g', 'mem_norm_g', 'w_mem_kv', 'a_w_in', 'a_w_out', 'b_w_in', 'b_v_norm_g', 'b_w_s', 'b_s_bias', 'b_w_out', 'ffn_w_up', 'ffn_conv_w', 'ffn_conv_b', 'ffn_w_down', 'final_norm_g']
TWIN_WEIGHTS = ['mix_norm_g', 'ffn_norm_g', 'mem_norm_g', 'w_mem_kv', 'a_w_in', 'a_w_out', 'b_w_in', 'b_v_norm_g', 'b_w_s', 'b_s_bias', 'b_w_out', 'ffn_w_up', 'ffn_conv_w', 'ffn_conv_b', 'ffn_w_down', 'final_norm_g']
TWIN_DIFF_INPUT = 'x'
TWIN_INPUTS = ['x', 'mem', 'mix_norm_g', 'ffn_norm_g', 'mem_norm_g', 'w_mem_kv', 'a_w_in', 'a_w_out', 'b_w_in', 'b_v_norm_g', 'b_w_s', 'b_s_bias', 'b_w_out', 'ffn_w_up', 'ffn_conv_w', 'ffn_conv_b', 'ffn_w_down', 'final_norm_g', 'loss_target', 'm_mix_norm_g', 'm_ffn_norm_g', 'm_mem_norm_g', 'm_w_mem_kv', 'm_a_w_in', 'm_a_w_out', 'm_b_w_in', 'm_b_v_norm_g', 'm_b_w_s', 'm_b_s_bias', 'm_b_w_out', 'm_ffn_w_up', 'm_ffn_conv_w', 'm_ffn_conv_b', 'm_ffn_w_down', 'm_final_norm_g', 'v_mix_norm_g', 'v_ffn_norm_g', 'v_mem_norm_g', 'v_w_mem_kv', 'v_a_w_in', 'v_a_w_out', 'v_b_w_in', 'v_b_v_norm_g', 'v_b_w_s', 'v_b_s_bias', 'v_b_w_out', 'v_ffn_w_up', 'v_ffn_conv_w', 'v_ffn_conv_b', 'v_ffn_w_down', 'v_final_norm_g']
TWIN_OUTPUTS = ['loss', 'grad_x', 'grad_mix_norm_g', 'grad_ffn_norm_g', 'grad_mem_norm_g', 'grad_w_mem_kv', 'grad_a_w_in', 'grad_a_w_out', 'grad_b_w_in', 'grad_b_v_norm_g', 'grad_b_w_s', 'grad_b_s_bias', 'grad_b_w_out', 'grad_ffn_w_up', 'grad_ffn_conv_w', 'grad_ffn_conv_b', 'grad_ffn_w_down', 'grad_final_norm_g', 'delta_mix_norm_g', 'delta_ffn_norm_g', 'delta_mem_norm_g', 'delta_w_mem_kv', 'delta_a_w_in', 'delta_a_w_out', 'delta_b_w_in', 'delta_b_v_norm_g', 'delta_b_w_s', 'delta_b_s_bias', 'delta_b_w_out', 'delta_ffn_w_up', 'delta_ffn_conv_w', 'delta_ffn_conv_b', 'delta_ffn_w_down', 'delta_final_norm_g', 'new_m_mix_norm_g', 'new_m_ffn_norm_g', 'new_m_mem_norm_g', 'new_m_w_mem_kv', 'new_m_a_w_in', 'new_m_a_w_out', 'new_m_b_w_in', 'new_m_b_v_norm_g', 'new_m_b_w_s', 'new_m_b_s_bias', 'new_m_b_w_out', 'new_m_ffn_w_up', 'new_m_ffn_conv_w', 'new_m_ffn_conv_b', 'new_m_ffn_w_down', 'new_m_final_norm_g', 'new_v_mix_norm_g', 'new_v_ffn_norm_g', 'new_v_mem_norm_g', 'new_v_w_mem_kv', 'new_v_a_w_in', 'new_v_a_w_out', 'new_v_b_w_in', 'new_v_b_v_norm_g', 'new_v_b_w_s', 'new_v_b_s_bias', 'new_v_b_w_out', 'new_v_ffn_w_up', 'new_v_ffn_conv_w', 'new_v_ffn_conv_b', 'new_v_ffn_w_down', 'new_v_final_norm_g']
TWIN_LEAF_KINDS = {'loss': 'loss', 'grad_x': 'grad_x', 'grad_mix_norm_g': 'grad_w', 'grad_ffn_norm_g': 'grad_w', 'grad_mem_norm_g': 'grad_w', 'grad_w_mem_kv': 'grad_w', 'grad_a_w_in': 'grad_w', 'grad_a_w_out': 'grad_w', 'grad_b_w_in': 'grad_w', 'grad_b_v_norm_g': 'grad_w', 'grad_b_w_s': 'grad_w', 'grad_b_s_bias': 'grad_w', 'grad_b_w_out': 'grad_w', 'grad_ffn_w_up': 'grad_w', 'grad_ffn_conv_w': 'grad_w', 'grad_ffn_conv_b': 'grad_w', 'grad_ffn_w_down': 'grad_w', 'grad_final_norm_g': 'grad_w', 'delta_mix_norm_g': 'delta_w', 'delta_ffn_norm_g': 'delta_w', 'delta_mem_norm_g': 'delta_w', 'delta_w_mem_kv': 'delta_w', 'delta_a_w_in': 'delta_w', 'delta_a_w_out': 'delta_w', 'delta_b_w_in': 'delta_w', 'delta_b_v_norm_g': 'delta_w', 'delta_b_w_s': 'delta_w', 'delta_b_s_bias': 'delta_w', 'delta_b_w_out': 'delta_w', 'delta_ffn_w_up': 'delta_w', 'delta_ffn_conv_w': 'delta_w', 'delta_ffn_conv_b': 'delta_w', 'delta_ffn_w_down': 'delta_w', 'delta_final_norm_g': 'delta_w', 'new_m_mix_norm_g': 'new_m', 'new_m_ffn_norm_g': 'new_m', 'new_m_mem_norm_g': 'new_m', 'new_m_w_mem_kv': 'new_m', 'new_m_a_w_in': 'new_m', 'new_m_a_w_out': 'new_m', 'new_m_b_w_in': 'new_m', 'new_m_b_v_norm_g': 'new_m', 'new_m_b_w_s': 'new_m', 'new_m_b_s_bias': 'new_m', 'new_m_b_w_out': 'new_m', 'new_m_ffn_w_up': 'new_m', 'new_m_ffn_conv_w': 'new_m', 'new_m_ffn_conv_b': 'new_m', 'new_m_ffn_w_down': 'new_m', 'new_m_final_norm_g': 'new_m', 'new_v_mix_norm_g': 'new_v', 'new_v_ffn_norm_g': 'new_v', 'new_v_mem_norm_g': 'new_v', 'new_v_w_mem_kv': 'new_v', 'new_v_a_w_in': 'new_v', 'new_v_a_w_out': 'new_v', 'new_v_b_w_in': 'new_v', 'new_v_b_v_norm_g': 'new_v', 'new_v_b_w_s': 'new_v', 'new_v_b_s_bias': 'new_v', 'new_v_b_w_out': 'new_v', 'new_v_ffn_w_up': 'new_v', 'new_v_ffn_conv_w': 'new_v', 'new_v_ffn_conv_b': 'new_v', 'new_v_ffn_w_down': 'new_v', 'new_v_final_norm_g': 'new_v'}


def _forward(args):
    return _fwd_reference(*[args[k] for k in FWD_PARAMS])


def _output_shape():
    out = _jax.eval_shape(lambda: _forward(_fwd_setup_inputs(0)))
    return out.shape, out.dtype

N_MICROBATCH = 1
ADAM_LR = 0.001
ADAM_B1 = 0.9
ADAM_B2 = 0.999
ADAM_EPS = 1e-08
ADAM_WD = 0.01
ADAM_STEP = 10
PER_EXAMPLE_BATCH_AXIS = {'x': 0, 'mem': 0, 'loss_target': 0}
SHARED_INPUTS = []
_WEIGHT_DTYPES = {'mix_norm_g': _jnp.float32, 'ffn_norm_g': _jnp.float32, 'mem_norm_g': _jnp.float32, 'w_mem_kv': _jnp.float32, 'a_w_in': _jnp.float32, 'a_w_out': _jnp.float32, 'b_w_in': _jnp.float32, 'b_v_norm_g': _jnp.float32, 'b_w_s': _jnp.float32, 'b_s_bias': _jnp.float32, 'b_w_out': _jnp.float32, 'ffn_w_up': _jnp.float32, 'ffn_conv_w': _jnp.float32, 'ffn_conv_b': _jnp.float32, 'ffn_w_down': _jnp.float32, 'final_norm_g': _jnp.float32}
MOMENT_SCALE = {'mix_norm_g': 3.731425e-02, 'ffn_norm_g': 4.808294e-02, 'mem_norm_g': 6.416110e-03, 'w_mem_kv': 8.964001e-03, 'a_w_in': 1.416011e-02, 'a_w_out': 1.471830e-02, 'b_w_in': 3.595423e-02, 'b_v_norm_g': 3.112166e-02, 'b_w_s': 3.045105e-02, 'b_s_bias': 3.045764e-02, 'b_w_out': 3.777253e-02, 'ffn_w_up': 2.046791e-02, 'ffn_conv_w': 2.050190e-02, 'ffn_conv_b': 1.997073e-02, 'ffn_w_down': 3.340815e-02, 'final_norm_g': 8.013631e+00}


def _to_microbatches(a, axis):
    t = _jnp.moveaxis(a, axis, 0)
    t = t.reshape((N_MICROBATCH, t.shape[0] // N_MICROBATCH) + t.shape[1:])
    return _jnp.moveaxis(t, 1, axis + 1)


def setup_inputs(seed: int = 0) -> dict:
    inp = _fwd_setup_inputs(seed)
    key = _jax.random.fold_in(_jax.random.key(seed), 7919)
    shape, _ = _output_shape()
    out = dict(inp)
    out["loss_target"] = _jax.random.normal(_jax.random.fold_in(key, 0), shape, _jnp.float32)
    for i, name in enumerate(TWIN_WEIGHTS):
        w = inp[name].astype(_jnp.float32)
        if MOMENT_SCALE is None:
            s = _jnp.sqrt(_jnp.mean(_jnp.square(w)) + 1e-30)
        else:
            s = MOMENT_SCALE[name]
        km, kv = _jax.random.split(_jax.random.fold_in(key, i + 1))
        out[name] = w
        out["m_" + name] = s * _jax.random.normal(km, w.shape, _jnp.float32)
        out["v_" + name] = (s * s) * _jax.random.uniform(kv, w.shape, _jnp.float32, 0.5, 1.5)
    if N_MICROBATCH > 1:
        for name, axis in PER_EXAMPLE_BATCH_AXIS.items():
            out[name] = _to_microbatches(out[name], axis)
    return {'x': out['x'], 'mem': out['mem'], 'mix_norm_g': out['mix_norm_g'], 'ffn_norm_g': out['ffn_norm_g'], 'mem_norm_g': out['mem_norm_g'], 'w_mem_kv': out['w_mem_kv'], 'a_w_in': out['a_w_in'], 'a_w_out': out['a_w_out'], 'b_w_in': out['b_w_in'], 'b_v_norm_g': out['b_v_norm_g'], 'b_w_s': out['b_w_s'], 'b_s_bias': out['b_s_bias'], 'b_w_out': out['b_w_out'], 'ffn_w_up': out['ffn_w_up'], 'ffn_conv_w': out['ffn_conv_w'], 'ffn_conv_b': out['ffn_conv_b'], 'ffn_w_down': out['ffn_w_down'], 'final_norm_g': out['final_norm_g'], 'loss_target': out['loss_target'], 'm_mix_norm_g': out['m_mix_norm_g'], 'm_ffn_norm_g': out['m_ffn_norm_g'], 'm_mem_norm_g': out['m_mem_norm_g'], 'm_w_mem_kv': out['m_w_mem_kv'], 'm_a_w_in': out['m_a_w_in'], 'm_a_w_out': out['m_a_w_out'], 'm_b_w_in': out['m_b_w_in'], 'm_b_v_norm_g': out['m_b_v_norm_g'], 'm_b_w_s': out['m_b_w_s'], 'm_b_s_bias': out['m_b_s_bias'], 'm_b_w_out': out['m_b_w_out'], 'm_ffn_w_up': out['m_ffn_w_up'], 'm_ffn_conv_w': out['m_ffn_conv_w'], 'm_ffn_conv_b': out['m_ffn_conv_b'], 'm_ffn_w_down': out['m_ffn_w_down'], 'm_final_norm_g': out['m_final_norm_g'], 'v_mix_norm_g': out['v_mix_norm_g'], 'v_ffn_norm_g': out['v_ffn_norm_g'], 'v_mem_norm_g': out['v_mem_norm_g'], 'v_w_mem_kv': out['v_w_mem_kv'], 'v_a_w_in': out['v_a_w_in'], 'v_a_w_out': out['v_a_w_out'], 'v_b_w_in': out['v_b_w_in'], 'v_b_v_norm_g': out['v_b_v_norm_g'], 'v_b_w_s': out['v_b_w_s'], 'v_b_s_bias': out['v_b_s_bias'], 'v_b_w_out': out['v_b_w_out'], 'v_ffn_w_up': out['v_ffn_w_up'], 'v_ffn_conv_w': out['v_ffn_conv_w'], 'v_ffn_conv_b': out['v_ffn_conv_b'], 'v_ffn_w_down': out['v_ffn_w_down'], 'v_final_norm_g': out['v_final_norm_g']}


def _loss(weights, diff, rest, loss_target):
    with _jax.named_scope("forward"):
        args = {**rest, TWIN_DIFF_INPUT: diff, **{k: w.astype(_WEIGHT_DTYPES[k]) for k, w in weights.items()}}
        y = _forward(args)
    with _jax.named_scope("loss_head"):
        err = _jnp.square(y.astype(_jnp.float32) - loss_target)
        return 0.5 * _jnp.sum(_jnp.mean(err, axis=-1)) if err.ndim else 0.5 * err


def _adamw(w, g, m, v):
    m = ADAM_B1 * m + (1.0 - ADAM_B1) * g
    v = ADAM_B2 * v + (1.0 - ADAM_B2) * _jnp.square(g)
    m_hat = m / (1.0 - ADAM_B1 ** ADAM_STEP)
    v_hat = v / (1.0 - ADAM_B2 ** ADAM_STEP)
    delta = -ADAM_LR * (m_hat / (_jnp.sqrt(v_hat) + ADAM_EPS) + ADAM_WD * w)
    return delta, m, v


def reference(x, mem, mix_norm_g, ffn_norm_g, mem_norm_g, w_mem_kv, a_w_in, a_w_out, b_w_in, b_v_norm_g, b_w_s, b_s_bias, b_w_out, ffn_w_up, ffn_conv_w, ffn_conv_b, ffn_w_down, final_norm_g, loss_target, m_mix_norm_g, m_ffn_norm_g, m_mem_norm_g, m_w_mem_kv, m_a_w_in, m_a_w_out, m_b_w_in, m_b_v_norm_g, m_b_w_s, m_b_s_bias, m_b_w_out, m_ffn_w_up, m_ffn_conv_w, m_ffn_conv_b, m_ffn_w_down, m_final_norm_g, v_mix_norm_g, v_ffn_norm_g, v_mem_norm_g, v_w_mem_kv, v_a_w_in, v_a_w_out, v_b_w_in, v_b_v_norm_g, v_b_w_s, v_b_s_bias, v_b_w_out, v_ffn_w_up, v_ffn_conv_w, v_ffn_conv_b, v_ffn_w_down, v_final_norm_g):
    given = dict(x=x, mem=mem, mix_norm_g=mix_norm_g, ffn_norm_g=ffn_norm_g, mem_norm_g=mem_norm_g, w_mem_kv=w_mem_kv, a_w_in=a_w_in, a_w_out=a_w_out, b_w_in=b_w_in, b_v_norm_g=b_v_norm_g, b_w_s=b_w_s, b_s_bias=b_s_bias, b_w_out=b_w_out, ffn_w_up=ffn_w_up, ffn_conv_w=ffn_conv_w, ffn_conv_b=ffn_conv_b, ffn_w_down=ffn_w_down, final_norm_g=final_norm_g, loss_target=loss_target, m_mix_norm_g=m_mix_norm_g, m_ffn_norm_g=m_ffn_norm_g, m_mem_norm_g=m_mem_norm_g, m_w_mem_kv=m_w_mem_kv, m_a_w_in=m_a_w_in, m_a_w_out=m_a_w_out, m_b_w_in=m_b_w_in, m_b_v_norm_g=m_b_v_norm_g, m_b_w_s=m_b_w_s, m_b_s_bias=m_b_s_bias, m_b_w_out=m_b_w_out, m_ffn_w_up=m_ffn_w_up, m_ffn_conv_w=m_ffn_conv_w, m_ffn_conv_b=m_ffn_conv_b, m_ffn_w_down=m_ffn_w_down, m_final_norm_g=m_final_norm_g, v_mix_norm_g=v_mix_norm_g, v_ffn_norm_g=v_ffn_norm_g, v_mem_norm_g=v_mem_norm_g, v_w_mem_kv=v_w_mem_kv, v_a_w_in=v_a_w_in, v_a_w_out=v_a_w_out, v_b_w_in=v_b_w_in, v_b_v_norm_g=v_b_v_norm_g, v_b_w_s=v_b_w_s, v_b_s_bias=v_b_s_bias, v_b_w_out=v_b_w_out, v_ffn_w_up=v_ffn_w_up, v_ffn_conv_w=v_ffn_conv_w, v_ffn_conv_b=v_ffn_conv_b, v_ffn_w_down=v_ffn_w_down, v_final_norm_g=v_final_norm_g)
    weights = {n: given[n] for n in TWIN_WEIGHTS}
    shared = {n: given[n] for n in SHARED_INPUTS}
    per_example = {n: given[n] for n in ['x', 'mem']}
    grad_fn = _jax.value_and_grad(_loss, argnums=(0, 1))

    def one_microbatch(ex, loss_target):
        ex = dict(ex)
        diff = ex.pop(TWIN_DIFF_INPUT)
        return grad_fn(weights, diff, {**shared, **ex}, loss_target)

    if N_MICROBATCH == 1:
        loss, (grad_w, grad_x) = one_microbatch(per_example, given["loss_target"])
    else:
        def body(carry, xs):
            loss_sum, grad_sum = carry
            l_k, (gw_k, gx_k) = one_microbatch(xs[0], xs[1])
            with _jax.named_scope("update"):
                return (loss_sum + l_k, _jax.tree.map(_jnp.add, grad_sum, gw_k)), gx_k

        init = (_jnp.zeros((), _jnp.float32), _jax.tree.map(_jnp.zeros_like, weights))
        (loss, grad_w), grad_x = _jax.lax.scan(body, init, (per_example, given["loss_target"]))
    with _jax.named_scope("update"):
        delta_w, new_m, new_v = {}, {}, {}
        for n in TWIN_WEIGHTS:
            delta_w[n], new_m[n], new_v[n] = _adamw(weights[n], grad_w[n], given["m_" + n], given["v_" + n])
    return (loss, grad_x, *[grad_w[n] for n in TWIN_WEIGHTS], *[delta_w[n] for n in TWIN_WEIGHTS],
            *[new_m[n] for n in TWIN_WEIGHTS], *[new_v[n] for n in TWIN_WEIGHTS])
```

```python
import functools
import math

import numpy as np
import jax
import jax.numpy as jnp
from jax import lax
from jax.experimental import pallas as pl
from jax.experimental.pallas import tpu as pltpu

F32 = jnp.float32
BF16 = jnp.bfloat16
MESH = pl.DeviceIdType.MESH

D_MODEL = 2048
SEQ = 2048
DEPTH = 4
EPS = 1e-6
NEG = -1e30
HEAD = 128
A_PATTERNS = ((128, 1), (512, 4), (2048, 16))
A_QKV_W = 1536
A_OUT_W = 512
A_IN = 5120
QBLK = 128
N_SIDE = 64
CHUNK = 128
B_GROUPS = 12
B_W = 1536
B_IN = 3584
MEM_LEN = 256
MEM_HEADS = 4
MEM_W = 512
FF = 5632
ADAM_LR, ADAM_B1, ADAM_B2, ADAM_EPS, ADAM_WD, ADAM_STEP = 0.001, 0.9, 0.999, 1e-08, 0.01, 10
N_CHIPS = 4

LANES = 128
V7X_VMEM_LIMIT = 56 * 1024 * 1024


def _cp(*sem):
    return pltpu.CompilerParams(dimension_semantics=sem, vmem_limit_bytes=V7X_VMEM_LIMIT)


def _pick(dim, prefs):
    for p in prefs:
        if dim % p == 0:
            return p
    raise ValueError(f"no tile for {dim} in {prefs}")


def _gelu_parts(x):
    cdf = 0.5 * (1.0 + lax.erf(x * (1.0 / math.sqrt(2.0))))
    pdf = jnp.exp(-0.5 * x * x) * (1.0 / math.sqrt(2.0 * math.pi))
    return x * cdf, cdf + x * pdf


def _gelu(x):
    return 0.5 * x * (1.0 + lax.erf(x * (1.0 / math.sqrt(2.0))))


TM_PREFS = (1024, 512, 256, 128)
TN_PREFS = (1408, 1280, 1024, 896, 512, 256, 128)
TK_PREFS = (2048, 1408, 1280, 1024, 896, 512, 256, 128)


def _mm_body(nk, dims, has_res):
    def body(*refs):
        if has_res:
            a_ref, b_ref, r_ref, o_ref = refs[:4]
        else:
            a_ref, b_ref, o_ref = refs[:3]
            r_ref = None
        part = lax.dot_general(a_ref[...].astype(BF16), b_ref[...].astype(BF16), dims,
                               preferred_element_type=F32)
        if nk == 1:
            if has_res:
                part = part + r_ref[...]
            o_ref[...] = part.astype(o_ref.dtype)
            return
        acc_ref = refs[-1]
        k = pl.program_id(2)

        @pl.when(k == 0)
        def _():
            acc_ref[...] = part

        @pl.when(k > 0)
        def _():
            acc_ref[...] += part

        @pl.when(k == nk - 1)
        def _():
            tot = acc_ref[...]
            if has_res:
                tot = tot + r_ref[...]
            o_ref[...] = tot.astype(o_ref.dtype)
    return body


def mm_nn(a, w, out_dtype, res=None, name="mm_nn"):
    m, kw = a.shape
    ns_, kw2, nsz = w.shape
    assert kw == kw2
    n = ns_ * nsz
    tm, tn, tk = _pick(m, TM_PREFS), _pick(nsz, TN_PREFS), _pick(kw, TK_PREFS)
    nb, nk = nsz // tn, kw // tk
    in_specs = [pl.BlockSpec((tm, tk), lambda i, j, k: (i, k)),
                pl.BlockSpec((None, tk, tn), lambda i, j, k: (j // nb, k, j % nb))]
    args = [a, w]
    if res is not None:
        in_specs.append(pl.BlockSpec((tm, tn), lambda i, j, k: (i, j)))
        args.append(res)
    return pl.pallas_call(
        _mm_body(nk, (((1,), (0,)), ((), ())), res is not None),
        out_shape=jax.ShapeDtypeStruct((m, n), out_dtype),
        grid=(m // tm, n // tn, nk), in_specs=in_specs,
        out_specs=pl.BlockSpec((tm, tn), lambda i, j, k: (i, j)),
        scratch_shapes=[pltpu.VMEM((tm, tn), F32)] if nk > 1 else [],
        compiler_params=_cp("parallel", "parallel", "arbitrary"), name=name)(*args)


def mm_nt(g, w, out_dtype, name="mm_nt"):
    m, n = g.shape
    ns_, kw, nsz = w.shape
    assert n == ns_ * nsz
    tm, tn, tk = _pick(m, TM_PREFS), _pick(kw, TN_PREFS), _pick(nsz, TK_PREFS)
    nb, nk = nsz // tk, n // tk
    return pl.pallas_call(
        _mm_body(nk, (((1,), (1,)), ((), ())), False),
        out_shape=jax.ShapeDtypeStruct((m, kw), out_dtype),
        grid=(m // tm, kw // tn, nk),
        in_specs=[pl.BlockSpec((tm, tk), lambda i, j, k: (i, k)),
                  pl.BlockSpec((None, tn, tk), lambda i, j, k: (k // nb, j, k % nb))],
        out_specs=pl.BlockSpec((tm, tn), lambda i, j, k: (i, j)),
        scratch_shapes=[pltpu.VMEM((tm, tn), F32)] if nk > 1 else [],
        compiler_params=_cp("parallel", "parallel", "arbitrary"), name=name)(g, w)


def mm_tn(a, g, n_shards, out_dtype, name="mm_tn"):
    t, kw = a.shape
    t2, n = g.shape
    assert t == t2
    nsz = n // n_shards
    tm, tn, tk = _pick(kw, TM_PREFS), _pick(nsz, TN_PREFS), _pick(t, TK_PREFS)
    nb, nk = nsz // tn, t // tk
    return pl.pallas_call(
        _mm_body(nk, (((0,), (0,)), ((), ())), False),
        out_shape=jax.ShapeDtypeStruct((n_shards, kw, nsz), out_dtype),
        grid=(kw // tm, n // tn, nk),
        in_specs=[pl.BlockSpec((tk, tm), lambda i, j, k: (k, i)),
                  pl.BlockSpec((tk, tn), lambda i, j, k: (k, j))],
        out_specs=pl.BlockSpec((None, tm, tn), lambda i, j, k: (j // nb, i, j % nb)),
        scratch_shapes=[pltpu.VMEM((tm, tn), F32)] if nk > 1 else [],
        compiler_params=_cp("parallel", "parallel", "arbitrary"), name=name)(a, g)


ROW_TILE = 256


def _rms_stats(x):
    r = lax.rsqrt(jnp.mean(x * x, axis=-1, keepdims=True) + EPS)
    return r, x * r


def _rms_back(xh, r, g, dh):
    u = dh * g
    return r * (u - xh * jnp.mean(u * xh, axis=-1, keepdims=True))


def rms_fwd(x, g, out_dtype, name="rms_fwd"):
    rows, d = x.shape
    tr = _pick(rows, (ROW_TILE, 128))

    def body(x_ref, g_ref, o_ref):
        _, xh = _rms_stats(x_ref[...])
        o_ref[...] = (xh * g_ref[...]).astype(o_ref.dtype)

    return pl.pallas_call(
        body, out_shape=jax.ShapeDtypeStruct((rows, d), out_dtype), grid=(rows // tr,),
        in_specs=[pl.BlockSpec((tr, d), lambda i: (i, 0)), pl.BlockSpec((1, d), lambda i: (0, 0))],
        out_specs=pl.BlockSpec((tr, d), lambda i: (i, 0)),
        compiler_params=_cp("parallel"), name=name)(x, g.reshape(1, d))


def rms_bwd(x, g, dh, dres=None, name="rms_bwd"):
    rows, d = x.shape
    tr = _pick(rows, (ROW_TILE, 128))
    has_res = dres is not None

    def body(*refs):
        if has_res:
            x_ref, g_ref, dh_ref, dres_ref, dx_ref, dg_ref = refs
        else:
            x_ref, g_ref, dh_ref, dx_ref, dg_ref = refs
        r, xh = _rms_stats(x_ref[...])
        dh_ = dh_ref[...].astype(F32)
        part = jnp.sum(dh_ * xh, axis=0, keepdims=True)

        @pl.when(pl.program_id(0) == 0)
        def _():
            dg_ref[...] = part

        @pl.when(pl.program_id(0) > 0)
        def _():
            dg_ref[...] += part

        dx = _rms_back(xh, r, g_ref[...], dh_)
        if has_res:
            dx = dx + dres_ref[...]
        dx_ref[...] = dx

    row_spec = pl.BlockSpec((tr, d), lambda i: (i, 0))
    vec_spec = pl.BlockSpec((1, d), lambda i: (0, 0))
    args = [x, g.reshape(1, d), dh] + ([dres] if has_res else [])
    return pl.pallas_call(
        body, out_shape=(jax.ShapeDtypeStruct((rows, d), F32), jax.ShapeDtypeStruct((1, d), F32)),
        grid=(rows // tr,), in_specs=[row_spec, vec_spec, row_spec] + ([row_spec] if has_res else []),
        out_specs=(row_spec, vec_spec), compiler_params=_cp("arbitrary"), name=name)(*args)


def final_loss(x, g, target, name="final_loss"):
    rows, d = x.shape
    tr = _pick(rows, (ROW_TILE, 128))

    def body(x_ref, g_ref, t_ref, loss_ref, dx_ref, dg_ref):
        r, xh = _rms_stats(x_ref[...])
        gain = g_ref[...]
        err = xh * gain - t_ref[...]
        sq = jnp.sum(jnp.sum(err * err, axis=1, keepdims=True), axis=0, keepdims=True) * (0.5 / d)
        dy = err * (1.0 / d)
        part = jnp.sum(dy * xh, axis=0, keepdims=True)

        @pl.when(pl.program_id(0) == 0)
        def _():
            dg_ref[...] = part
            loss_ref[...] = jnp.broadcast_to(sq, loss_ref.shape)

        @pl.when(pl.program_id(0) > 0)
        def _():
            dg_ref[...] += part
            loss_ref[...] += jnp.broadcast_to(sq, loss_ref.shape)

        dx_ref[...] = _rms_back(xh, r, gain, dy)

    row_spec = pl.BlockSpec((tr, d), lambda i: (i, 0))
    vec_spec = pl.BlockSpec((1, d), lambda i: (0, 0))
    return pl.pallas_call(
        body, out_shape=(jax.ShapeDtypeStruct((1, LANES), F32), jax.ShapeDtypeStruct((rows, d), F32),
                         jax.ShapeDtypeStruct((1, d), F32)),
        grid=(rows // tr,), in_specs=[row_spec, vec_spec, row_spec],
        out_specs=(pl.BlockSpec((1, LANES), lambda i: (0, 0)), row_spec, vec_spec),
        compiler_params=_cp("arbitrary"), name=name)(x, g.reshape(1, d), target)


def _alibi_slopes():
    return (2.0 ** (-8.0 * (np.arange(12) + 1) / 12)).astype(np.float32)


def _band_scores(q, k, q0, start, wk, slope):
    s = lax.dot_general(q, k, (((1,), (1,)), ((), ())), preferred_element_type=F32) * (HEAD ** -0.5)
    qpos = q0 + lax.broadcasted_iota(jnp.int32, (QBLK, wk), 0)
    kpos = start + lax.broadcasted_iota(jnp.int32, (QBLK, wk), 1)
    rel = jnp.abs(qpos - kpos)
    return jnp.where(rel <= N_SIDE, s - slope * rel.astype(F32), NEG)


def _attn_geometry(seq, dilation):
    length = seq // dilation
    return length, length // QBLK, min(2 * QBLK, length)


def _attn_window(n, length, wk):
    q0 = pl.multiple_of(n * QBLK, QBLK)
    start = pl.multiple_of(jnp.clip(n * QBLK - N_SIDE, 0, length - wk), N_SIDE)
    return q0, start


def attn_fwd(proj, group, name):
    seq = proj.shape[0]
    dilation = A_PATTERNS[group][1]
    length, nblk, wk = _attn_geometry(seq, dilation)
    cols = A_IN // HEAD
    pv = proj.reshape(length, dilation * A_IN)

    def body(slope_ref, q_ref, k_ref, v_ref, o_ref, lse_ref):
        slope = slope_ref[group * 4 + pl.program_id(1)] * float(dilation)

        def blk(n, carry):
            q0, start = _attn_window(n, length, wk)
            q = q_ref[pl.ds(q0, QBLK), :].astype(BF16)
            k = k_ref[pl.ds(start, wk), :].astype(BF16)
            v = v_ref[pl.ds(start, wk), :].astype(BF16)
            s = _band_scores(q, k, q0, start, wk, slope)
            m = jnp.max(s, axis=-1, keepdims=True)
            p = jnp.exp(s - m)
            l = jnp.sum(p, axis=-1, keepdims=True)
            o = jnp.dot(p.astype(BF16), v, preferred_element_type=F32) / l
            o_ref[pl.ds(q0, QBLK), :] = o
            lse_ref[pl.ds(q0, QBLK), :] = jnp.broadcast_to(m + jnp.log(l), (QBLK, HEAD))
            return carry

        lax.fori_loop(0, nblk, blk, 0)

    def part(p):
        return pl.BlockSpec((length, HEAD), lambda r, h: (0, r * cols + p * 12 + group * 4 + h))

    out_spec = pl.BlockSpec((length, HEAD), lambda r, h: (0, r * 4 + h))
    o, lse = pl.pallas_call(
        body, out_shape=(jax.ShapeDtypeStruct((length, dilation * A_OUT_W), F32),) * 2,
        grid=(dilation, 4),
        in_specs=[pl.BlockSpec(memory_space=pltpu.SMEM), part(0), part(1), part(2)],
        out_specs=(out_spec, out_spec), compiler_params=_cp("parallel", "parallel"), name=name,
    )(jnp.asarray(_alibi_slopes()), pv, pv, pv)
    return o.reshape(seq, A_OUT_W), lse.reshape(seq, A_OUT_W)


def attn_combine(os_, lses, name="attn_combine"):
    seq = os_[0].shape[0]
    tr = ROW_TILE

    def body(o0, o1, o2, l0, l1, l2, c_ref, lse_ref):
        a, b, c = l0[...], l1[...], l2[...]
        m = jnp.maximum(jnp.maximum(a, b), c)
        ea, eb, ec = jnp.exp(a - m), jnp.exp(b - m), jnp.exp(c - m)
        den = ea + eb + ec
        c_ref[...] = (ea * o0[...] + eb * o1[...] + ec * o2[...]) / den
        lse_ref[...] = m + jnp.log(den)

    spec = pl.BlockSpec((tr, A_OUT_W), lambda i: (i, 0))
    return pl.pallas_call(
        body, out_shape=(jax.ShapeDtypeStruct((seq, A_OUT_W), F32),) * 2, grid=(seq // tr,),
        in_specs=[spec] * 6, out_specs=(spec, spec), compiler_params=_cp("parallel"), name=name)(*os_, *lses)


def attn_bwd(proj, dcat, comb, lse, group, name):
    seq = proj.shape[0]
    dilation = A_PATTERNS[group][1]
    length, nblk, wk = _attn_geometry(seq, dilation)
    cols = A_IN // HEAD
    pv = proj.reshape(length, dilation * A_IN)
    view = lambda a: a.reshape(length, dilation * A_OUT_W)
    scale = HEAD ** -0.5

    def body(slope_ref, q_ref, k_ref, v_ref, do_ref, c_ref, lse_ref, dq_ref, dk_ref, dv_ref, dk_acc, dv_acc):
        slope = slope_ref[group * 4 + pl.program_id(1)] * float(dilation)
        dk_acc[...] = jnp.zeros_like(dk_acc)
        dv_acc[...] = jnp.zeros_like(dv_acc)

        def blk(n, carry):
            q0, start = _attn_window(n, length, wk)
            rows = pl.ds(q0, QBLK)
            keys = pl.ds(start, wk)
            q = q_ref[rows, :].astype(BF16)
            k = k_ref[keys, :].astype(BF16)
            v = v_ref[keys, :].astype(BF16)
            do = do_ref[rows, :]
            s = _band_scores(q, k, q0, start, wk, slope)
            p = jnp.exp(s - lse_ref[rows, :][:, :1])
            delta = jnp.sum(do * c_ref[rows, :], axis=-1, keepdims=True)
            do16 = do.astype(BF16)
            dp = lax.dot_general(do16, v, (((1,), (1,)), ((), ())), preferred_element_type=F32)
            ds = (p * (dp - delta) * scale).astype(BF16)
            p16 = p.astype(BF16)
            dq_ref[rows, :] = jnp.dot(ds, k, preferred_element_type=F32).astype(dq_ref.dtype)
            dk_acc[keys, :] += lax.dot_general(ds, q, (((0,), (0,)), ((), ())), preferred_element_type=F32)
            dv_acc[keys, :] += lax.dot_general(p16, do16, (((0,), (0,)), ((), ())), preferred_element_type=F32)
            return carry

        lax.fori_loop(0, nblk, blk, 0)
        dk_ref[...] = dk_acc[...].astype(dk_ref.dtype)
        dv_ref[...] = dv_acc[...].astype(dv_ref.dtype)

    def part(p):
        return pl.BlockSpec((length, HEAD), lambda r, h: (0, r * cols + p * 12 + group * 4 + h))

    hs = pl.BlockSpec((length, HEAD), lambda r, h: (0, r * 4 + h))
    do_cols = dcat.shape[1] // HEAD
    do_spec = pl.BlockSpec((length, HEAD), lambda r, h: (0, r * do_cols + h))
    outs = pl.pallas_call(
        body, out_shape=(jax.ShapeDtypeStruct((length, dilation * A_OUT_W), BF16),) * 3,
        grid=(dilation, 4),
        in_specs=[pl.BlockSpec(memory_space=pltpu.SMEM), part(0), part(1), part(2), do_spec, hs, hs],
        out_specs=(hs, hs, hs),
        scratch_shapes=[pltpu.VMEM((length, HEAD), F32), pltpu.VMEM((length, HEAD), F32)],
        compiler_params=_cp("parallel", "parallel"), name=name,
    )(jnp.asarray(_alibi_slopes()), pv, pv, pv, dcat.reshape(length, dilation * dcat.shape[1]), view(comb), view(lse))
    return tuple(o.reshape(seq, A_OUT_W) for o in outs)


MEM_ROW_TILE = 512


def _mem_probs(q, k):
    s = lax.dot_general(q, k, (((1,), (1,)), ((), ())), preferred_element_type=F32) * (HEAD ** -0.5)
    p = jnp.exp(s - jnp.max(s, axis=-1, keepdims=True))
    return p / jnp.sum(p, axis=-1, keepdims=True)


def mem_fwd(proj, q_col, kv, name="mem_fwd"):
    seq = proj.shape[0]
    qb = q_col // HEAD

    def body(q_ref, k_ref, v_ref, o_ref):
        p = _mem_probs(q_ref[...].astype(BF16), k_ref[...].astype(BF16))
        o_ref[...] = jnp.dot(p.astype(BF16), v_ref[...].astype(BF16), preferred_element_type=F32).astype(o_ref.dtype)

    return pl.pallas_call(
        body, out_shape=jax.ShapeDtypeStruct((seq, MEM_W), BF16), grid=(MEM_HEADS, seq // MEM_ROW_TILE),
        in_specs=[pl.BlockSpec((MEM_ROW_TILE, HEAD), lambda h, i: (i, qb + h)),
                  pl.BlockSpec((MEM_LEN, HEAD), lambda h, i: (0, h)),
                  pl.BlockSpec((MEM_LEN, HEAD), lambda h, i: (0, MEM_HEADS + h))],
        out_specs=pl.BlockSpec((MEM_ROW_TILE, HEAD), lambda h, i: (i, h)),
        compiler_params=_cp("parallel", "parallel"), name=name)(proj, kv, kv)


def mem_bwd(proj, q_col, kv, dcat, do_col, name="mem_bwd"):
    seq = proj.shape[0]
    qb, ob = q_col // HEAD, do_col // HEAD
    scale = HEAD ** -0.5

    def body(q_ref, k_ref, v_ref, do_ref, dq_ref, dk_ref, dv_ref):
        q = q_ref[...].astype(BF16)
        k = k_ref[...].astype(BF16)
        v = v_ref[...].astype(BF16)
        do = do_ref[...].astype(BF16)
        p = _mem_probs(q, k)
        dp = lax.dot_general(do, v, (((1,), (1,)), ((), ())), preferred_element_type=F32)
        ds = (p * (dp - jnp.sum(dp * p, axis=-1, keepdims=True)) * scale).astype(BF16)
        dq_ref[...] = jnp.dot(ds, k, preferred_element_type=F32).astype(dq_ref.dtype)
        dk = lax.dot_general(ds, q, (((0,), (0,)), ((), ())), preferred_element_type=F32)
        dv = lax.dot_general(p.astype(BF16), do, (((0,), (0,)), ((), ())), preferred_element_type=F32)

        @pl.when(pl.program_id(1) == 0)
        def _():
            dk_ref[...] = dk
            dv_ref[...] = dv

        @pl.when(pl.program_id(1) > 0)
        def _():
            dk_ref[...] += dk
            dv_ref[...] += dv

    dq, dk, dv = pl.pallas_call(
        body, out_shape=(jax.ShapeDtypeStruct((seq, MEM_W), BF16), jax.ShapeDtypeStruct((MEM_LEN, MEM_W), F32),
                         jax.ShapeDtypeStruct((MEM_LEN, MEM_W), F32)),
        grid=(MEM_HEADS, seq // MEM_ROW_TILE),
        in_specs=[pl.BlockSpec((MEM_ROW_TILE, HEAD), lambda h, i: (i, qb + h)),
                  pl.BlockSpec((MEM_LEN, HEAD), lambda h, i: (0, h)),
                  pl.BlockSpec((MEM_LEN, HEAD), lambda h, i: (0, MEM_HEADS + h)),
                  pl.BlockSpec((MEM_ROW_TILE, HEAD), lambda h, i: (i, ob + h))],
        out_specs=(pl.BlockSpec((MEM_ROW_TILE, HEAD), lambda h, i: (i, h)),
                   pl.BlockSpec((MEM_LEN, HEAD), lambda h, i: (0, h)),
                   pl.BlockSpec((MEM_LEN, HEAD), lambda h, i: (0, h))),
        compiler_params=_cp("parallel", "arbitrary"), name=name)(proj, kv, kv, dcat)
    return dq, jnp.concatenate([dk, dv], axis=1)


def _sgu_front(x, gain):
    uv, duv = _gelu_parts(x)
    u, v = uv[:, :B_W], uv[:, B_W:]
    r, vh = _rms_stats(v)
    return u, duv, r, vh, vh * gain


def sgu_fwd(proj, gain, w_s, bias_b, name="sgu_fwd"):
    seq = proj.shape[0]

    def body(x_ref, gain_ref, ws_ref, bias_ref, o_ref):
        u, _, _, _, vn = _sgu_front(x_ref[...], gain_ref[...])
        for g in range(B_GROUPS):
            cs = slice(g * CHUNK, (g + 1) * CHUNK)
            mixed = jnp.dot(ws_ref[g].astype(BF16), vn[:, cs].astype(BF16), preferred_element_type=F32) + bias_ref[g]
            o_ref[:, cs] = (u[:, cs] * mixed).astype(o_ref.dtype)

    full = lambda shape: pl.BlockSpec(shape, lambda c: (0,) * len(shape))
    return pl.pallas_call(
        body, out_shape=jax.ShapeDtypeStruct((seq, B_W), BF16), grid=(seq // CHUNK,),
        in_specs=[pl.BlockSpec((CHUNK, 2 * B_W), lambda c: (c, 0)), full((1, B_W)),
                  full((B_GROUPS, CHUNK, CHUNK)), full((B_GROUPS, CHUNK, CHUNK))],
        out_specs=pl.BlockSpec((CHUNK, B_W), lambda c: (c, 0)),
        compiler_params=_cp("parallel"), name=name)(proj, gain.reshape(1, B_W), w_s, bias_b)


def sgu_bwd(proj, gain, w_s, w_s_t, bias_b, dcat, name="sgu_bwd"):
    seq = proj.shape[0]

    def body(x_ref, gain_ref, ws_ref, wst_ref, bias_ref, do_ref, dx_ref, dws_ref, dmix_ref, dgain_ref, dvn_ref):
        first = pl.program_id(0) == 0
        gain = gain_ref[...]
        u, duv, r, vh, vn = _sgu_front(x_ref[...], gain)
        do = do_ref[...]
        for g in range(B_GROUPS):
            cs = slice(g * CHUNK, (g + 1) * CHUNK)
            vg = vn[:, cs].astype(BF16)
            mixed = jnp.dot(ws_ref[g].astype(BF16), vg, preferred_element_type=F32) + bias_ref[g]
            dx_ref[:, cs] = (do[:, cs] * mixed * duv[:, cs]).astype(dx_ref.dtype)
            dmixed = do[:, cs] * u[:, cs]
            dm16 = dmixed.astype(BF16)
            dws = lax.dot_general(dm16, vg, (((1,), (1,)), ((), ())), preferred_element_type=F32)
            dvn_ref[:, cs] = jnp.dot(wst_ref[g].astype(BF16), dm16, preferred_element_type=F32)

            @pl.when(first)
            def _():
                dws_ref[g] = dws
                dmix_ref[g] = dmixed

            @pl.when(jnp.logical_not(first))
            def _():
                dws_ref[g] += dws
                dmix_ref[g] += dmixed

        dvn = dvn_ref[...]
        dgain = jnp.sum(dvn * vh, axis=0, keepdims=True)

        @pl.when(first)
        def _():
            dgain_ref[...] = dgain

        @pl.when(jnp.logical_not(first))
        def _():
            dgain_ref[...] += dgain

        dv = _rms_back(vh, r, gain, dvn)
        dx_ref[:, B_W:] = (dv * duv[:, B_W:]).astype(dx_ref.dtype)

    full = lambda shape: pl.BlockSpec(shape, lambda c: (0,) * len(shape))
    mats = full((B_GROUPS, CHUNK, CHUNK))
    return pl.pallas_call(
        body, out_shape=(jax.ShapeDtypeStruct((seq, 2 * B_W), BF16), jax.ShapeDtypeStruct((B_GROUPS, CHUNK, CHUNK), F32),
                         jax.ShapeDtypeStruct((B_GROUPS, CHUNK, CHUNK), F32), jax.ShapeDtypeStruct((1, B_W), F32)),
        grid=(seq // CHUNK,),
        in_specs=[pl.BlockSpec((CHUNK, 2 * B_W), lambda c: (c, 0)), full((1, B_W)), mats, mats, mats,
                  pl.BlockSpec((CHUNK, B_W), lambda c: (c, 0))],
        out_specs=(pl.BlockSpec((CHUNK, 2 * B_W), lambda c: (c, 0)), mats, mats, full((1, B_W))),
        scratch_shapes=[pltpu.VMEM((CHUNK, B_W), F32)],
        compiler_params=_cp("arbitrary"), name=name)(proj, gain.reshape(1, B_W), w_s, w_s_t, bias_b, dcat)


FFN_COLS = 256


def _shift_prev(a):
    rows = lax.broadcasted_iota(jnp.int32, a.shape, 0)
    return jnp.where(rows == 0, 0.0, pltpu.roll(a, 1, 0))


def _shift_next(a):
    n = a.shape[0]
    rows = lax.broadcasted_iota(jnp.int32, a.shape, 0)
    return jnp.where(rows == n - 1, 0.0, pltpu.roll(a, n - 1, 0))


def _conv3(a, w, b):
    return _shift_prev(a) * w[0:1] + a * w[1:2] + _shift_next(a) * w[2:3] + b


def ffn_act_fwd(a, conv_w, conv_b, name="ffn_act_fwd"):
    seq = a.shape[0]
    nb = FF // FFN_COLS

    def body(ag_ref, av_ref, wg_ref, wv_ref, bg_ref, bv_ref, o_ref):
        gate = _conv3(ag_ref[...], wg_ref[...], bg_ref[...])
        val = _conv3(av_ref[...], wv_ref[...], bv_ref[...])
        o_ref[...] = (_gelu(gate) * val).astype(o_ref.dtype)

    col = lambda rows, off: pl.BlockSpec((rows, FFN_COLS), lambda j: (0, j + off))
    cb = conv_b.reshape(1, 2 * FF)
    return pl.pallas_call(
        body, out_shape=jax.ShapeDtypeStruct((seq, FF), BF16), grid=(nb,),
        in_specs=[col(seq, 0), col(seq, nb), col(3, 0), col(3, nb), col(1, 0), col(1, nb)],
        out_specs=col(seq, 0), compiler_params=_cp("parallel"), name=name)(a, a, conv_w, conv_w, cb, cb)


def ffn_act_bwd(a, conv_w, conv_b, dact, name="ffn_act_bwd"):
    seq = a.shape[0]
    nb = FF // FFN_COLS

    def back(a_, w, dc, da_ref, dw_ref, db_ref):
        db_ref[...] = jnp.sum(dc, axis=0, keepdims=True)
        dw_ref[0:1, :] = jnp.sum(dc * _shift_prev(a_), axis=0, keepdims=True)
        dw_ref[1:2, :] = jnp.sum(dc * a_, axis=0, keepdims=True)
        dw_ref[2:3, :] = jnp.sum(dc * _shift_next(a_), axis=0, keepdims=True)
        da_ref[...] = (_shift_next(dc) * w[0:1] + dc * w[1:2] + _shift_prev(dc) * w[2:3]).astype(da_ref.dtype)

    def body(ag_ref, av_ref, wg_ref, wv_ref, bg_ref, bv_ref, d_ref, dag_ref, dav_ref, dwg_ref, dwv_ref, dbg_ref, dbv_ref):
        ag, av, wg, wv = ag_ref[...], av_ref[...], wg_ref[...], wv_ref[...]
        gate = _conv3(ag, wg, bg_ref[...])
        val = _conv3(av, wv, bv_ref[...])
        act, dact_dgate = _gelu_parts(gate)
        d = d_ref[...].astype(F32)
        back(ag, wg, d * val * dact_dgate, dag_ref, dwg_ref, dbg_ref)
        back(av, wv, d * act, dav_ref, dwv_ref, dbv_ref)

    col = lambda rows, off: pl.BlockSpec((rows, FFN_COLS), lambda j: (0, j + off))
    cb = conv_b.reshape(1, 2 * FF)
    dag, dav, dwg, dwv, dbg, dbv = pl.pallas_call(
        body, out_shape=(jax.ShapeDtypeStruct((seq, FF), BF16),) * 2 + (jax.ShapeDtypeStruct((3, FF), F32),) * 2
        + (jax.ShapeDtypeStruct((1, FF), F32),) * 2, grid=(nb,),
        in_specs=[col(seq, 0), col(seq, nb), col(3, 0), col(3, nb), col(1, 0), col(1, nb), col(seq, 0)],
        out_specs=(col(seq, 0), col(seq, 0), col(3, 0), col(3, 0), col(1, 0), col(1, 0)),
        compiler_params=_cp("parallel"), name=name)(a, a, conv_w, conv_w, cb, cb, dact)
    cat = lambda p, q: jnp.concatenate([p, q], axis=1)
    return cat(dag, dav), cat(dwg, dwv), cat(dbg, dbv)


def _adam_math(w, g, m, v):
    m = ADAM_B1 * m + (1.0 - ADAM_B1) * g
    v = ADAM_B2 * v + (1.0 - ADAM_B2) * (g * g)
    m_hat = m / (1.0 - ADAM_B1 ** ADAM_STEP)
    v_hat = v / (1.0 - ADAM_B2 ** ADAM_STEP)
    return -ADAM_LR * (m_hat / (jnp.sqrt(v_hat) + ADAM_EPS) + ADAM_WD * w), m, v


def _row_tile(rows, cols):
    return _pick(rows, (256, 128, 64)) if cols <= 1024 else _pick(rows, (128, 64))


def adamw_layer(w_all, m_all, v_all, layer, g, prev, name):
    n, rows, cols = w_all.shape
    tr = _row_tile(rows, cols)

    def body(w_ref, m_ref, v_ref, g_ref, *rest):
        go_ref, d_ref, mo_ref, vo_ref = rest[-4:]
        g_ = g_ref[...]
        d, m_, v_ = _adam_math(w_ref[...], g_, m_ref[...], v_ref[...])
        go_ref[...] = g_
        d_ref[...] = d
        mo_ref[...] = m_
        vo_ref[...] = v_

    lay = pl.BlockSpec((None, tr, cols), lambda i: (layer, i, 0))
    in_specs = [lay, lay, lay, pl.BlockSpec((tr, cols), lambda i: (i, 0))]
    args = [w_all, m_all, v_all, g]
    aliases = {}
    if prev is not None:
        in_specs += [pl.BlockSpec(memory_space=pl.ANY)] * 4
        args += list(prev)
        aliases = {4 + k: k for k in range(4)}
    return pl.pallas_call(
        body, out_shape=(jax.ShapeDtypeStruct(w_all.shape, F32),) * 4, grid=(rows // tr,),
        in_specs=in_specs, out_specs=(lay,) * 4, input_output_aliases=aliases,
        compiler_params=_cp("parallel"), name=name)(*args)


def adamw_flat(w, g, m, v, name="adamw_small"):
    rows, cols = w.shape
    tr = _pick(rows, (128, 8))

    def body(w_ref, g_ref, m_ref, v_ref, d_ref, mo_ref, vo_ref):
        d_ref[...], mo_ref[...], vo_ref[...] = _adam_math(w_ref[...], g_ref[...], m_ref[...], v_ref[...])

    spec = pl.BlockSpec((tr, cols), lambda i: (i, 0))
    return pl.pallas_call(
        body, out_shape=(jax.ShapeDtypeStruct(w.shape, F32),) * 3, grid=(rows // tr,),
        in_specs=[spec] * 4, out_specs=(spec,) * 3, compiler_params=_cp("parallel"), name=name)(w, g, m, v)


def pair_sum(dw, got, core, name):
    _, rows, cols = dw.shape
    half = rows // 2
    tr = _row_tile(half, cols)
    nrb = half // tr

    def body(c_ref, a_ref, b_ref, o_ref):
        o_ref[...] = (a_ref[...].astype(F32) + b_ref[...].astype(F32)).astype(o_ref.dtype)

    return pl.pallas_call(
        body, out_shape=jax.ShapeDtypeStruct((N_CHIPS, half, cols), BF16),
        grid_spec=pltpu.PrefetchScalarGridSpec(
            num_scalar_prefetch=1, grid=(N_CHIPS, nrb),
            in_specs=[pl.BlockSpec((None, tr, cols), lambda s, i, c_ref: (s, c_ref[0] * nrb + i, 0)),
                      pl.BlockSpec((None, tr, cols), lambda s, i, c_ref: (s, i, 0))],
            out_specs=pl.BlockSpec((None, tr, cols), lambda s, i, c_ref: (s, i, 0))),
        compiler_params=_cp("parallel", "parallel"), name=name)(core, dw, got)


def chip_sum(parts, name):
    _, half, cols = parts.shape
    tr = _row_tile(half, cols)

    def body(p_ref, o_ref):
        o_ref[...] = ((p_ref[0].astype(F32) + p_ref[1].astype(F32)) + p_ref[2].astype(F32)) + p_ref[3].astype(F32)

    return pl.pallas_call(
        body, out_shape=jax.ShapeDtypeStruct((half, cols), F32), grid=(half // tr,),
        in_specs=[pl.BlockSpec((N_CHIPS, tr, cols), lambda i: (0, i, 0))],
        out_specs=pl.BlockSpec((tr, cols), lambda i: (i, 0)),
        compiler_params=_cp("parallel"), name=name)(parts)


ANY = pl.BlockSpec(memory_space=pl.ANY)


def _place():
    x, y, c = lax.axis_index("x"), lax.axis_index("y"), lax.axis_index("c")
    others = [(1 - x, y), (x, 1 - y), (1 - x, 1 - y)]
    return x, y, c, 2 * x + y, others


def _remote(src, dst, send_sem, recv_sem, dev):
    return pltpu.make_async_remote_copy(src_ref=src, dst_ref=dst, send_sem=send_sem, recv_sem=recv_sem,
                                        device_id=dev, device_id_type=MESH)


def gather_weights(shards, name):
    n = len(shards)

    def body(*refs):
        ins, outs = refs[:n], refs[n:2 * n]
        send_sems, recv_sems, local_sems = refs[2 * n:]
        x, y, c, me, others = _place()
        sends = []
        local = []
        for w in range(n):
            half = ins[w].shape[0] // 2
            mine = pl.ds(c * half, half)
            cp = pltpu.make_async_copy(ins[w], outs[w].at[me], local_sems.at[w])
            cp.start()
            local.append(cp)
            for k, (ox, oy) in enumerate(others):
                cp = _remote(ins[w].at[mine], outs[w].at[me, mine], send_sems.at[6 * w + k], recv_sems.at[6 * w + k], (ox, oy, c))
                cp.start()
                sends.append(cp)
        for w in range(n):
            half = ins[w].shape[0] // 2
            mine = pl.ds(c * half, half)
            for k, (ox, oy) in enumerate(others):
                landed = outs[w].at[2 * ox + oy, mine]
                _remote(landed, landed, send_sems.at[6 * w + k], recv_sems.at[6 * w + k], (ox, oy, c)).wait_recv()
                cp = _remote(landed, landed, send_sems.at[6 * w + 3 + k], recv_sems.at[6 * w + 3 + k], (x, y, 1 - c))
                cp.start()
                sends.append(cp)
        for w in range(n):
            half = ins[w].shape[0] // 2
            theirs = pl.ds((1 - c) * half, half)
            for k, (ox, oy) in enumerate(others):
                passed = outs[w].at[2 * ox + oy, theirs]
                _remote(passed, passed, send_sems.at[6 * w + 3 + k], recv_sems.at[6 * w + 3 + k], (x, y, 1 - c)).wait_recv()
        for cp in sends:
            cp.wait_send()
        for cp in local:
            cp.wait()

    return pl.pallas_call(
        body, out_shape=tuple(jax.ShapeDtypeStruct((N_CHIPS,) + s.shape, s.dtype) for s in shards),
        in_specs=[ANY] * n, out_specs=(ANY,) * n,
        scratch_shapes=[pltpu.SemaphoreType.DMA((6 * n,)), pltpu.SemaphoreType.DMA((6 * n,)), pltpu.SemaphoreType.DMA((n,))],
        name=name)(*shards)


def swap_halves(grads, name):
    n = len(grads)

    def body(*refs):
        ins, outs = refs[:n], refs[n:2 * n]
        send_sems, recv_sems = refs[2 * n:]
        x, y, c, _, _ = _place()
        copies = []
        for w in range(n):
            half = ins[w].shape[1] // 2
            cp = _remote(ins[w].at[:, pl.ds((1 - c) * half, half)], outs[w], send_sems.at[w], recv_sems.at[w], (x, y, 1 - c))
            cp.start()
            copies.append(cp)
        for cp in copies:
            cp.wait()

    return pl.pallas_call(
        body, out_shape=tuple(jax.ShapeDtypeStruct((N_CHIPS, g.shape[1] // 2, g.shape[2]), g.dtype) for g in grads),
        in_specs=[ANY] * n, out_specs=(ANY,) * n,
        scratch_shapes=[pltpu.SemaphoreType.DMA((n,)), pltpu.SemaphoreType.DMA((n,))], name=name)(*grads)


def exchange_chip_sums(sums, name):
    n = len(sums)

    def body(*refs):
        ins, outs = refs[:n], refs[n:2 * n]
        send_sems, recv_sems, local_sems = refs[2 * n:]
        x, y, c, me, others = _place()
        copies = []
        for w in range(n):
            cp = pltpu.make_async_copy(ins[w].at[me], outs[w].at[me], local_sems.at[w])
            cp.start()
            copies.append(cp)
            for k, (ox, oy) in enumerate(others):
                cp = _remote(ins[w].at[2 * ox + oy], outs[w].at[me], send_sems.at[3 * w + k], recv_sems.at[3 * w + k], (ox, oy, c))
                cp.start()
                copies.append(cp)
        for w in range(n):
            for k, (ox, oy) in enumerate(others):
                landed = outs[w].at[2 * ox + oy]
                _remote(landed, landed, send_sems.at[3 * w + k], recv_sems.at[3 * w + k], (ox, oy, c)).wait_recv()
        for w in range(n):
            copies[4 * w].wait()
            for k in range(3):
                copies[4 * w + 1 + k].wait_send()

    return pl.pallas_call(
        body, out_shape=tuple(jax.ShapeDtypeStruct(s.shape, s.dtype) for s in sums),
        in_specs=[ANY] * n, out_specs=(ANY,) * n,
        scratch_shapes=[pltpu.SemaphoreType.DMA((3 * n,)), pltpu.SemaphoreType.DMA((3 * n,)), pltpu.SemaphoreType.DMA((n,))],
        name=name)(*sums)


def join_halves(halves, name):
    n = len(halves)

    def body(*refs):
        ins, outs = refs[:n], refs[n:2 * n]
        send_sems, recv_sems, local_sems = refs[2 * n:]
        x, y, c, _, _ = _place()
        copies = []
        for w in range(n):
            half = ins[w].shape[0]
            mine = pl.ds(c * half, half)
            loc = pltpu.make_async_copy(ins[w], outs[w].at[mine], local_sems.at[w])
            loc.start()
            cp = _remote(ins[w], outs[w].at[mine], send_sems.at[w], recv_sems.at[w], (x, y, 1 - c))
            cp.start()
            copies.append((loc, cp))
        for w, (loc, cp) in enumerate(copies):
            half = ins[w].shape[0]
            theirs = outs[w].at[pl.ds((1 - c) * half, half)]
            _remote(theirs, theirs, send_sems.at[w], recv_sems.at[w], (x, y, 1 - c)).wait_recv()
            cp.wait_send()
            loc.wait()

    return pl.pallas_call(
        body, out_shape=tuple(jax.ShapeDtypeStruct((2 * h.shape[0], h.shape[1]), h.dtype) for h in halves),
        in_specs=[ANY] * n, out_specs=(ANY,) * n,
        scratch_shapes=[pltpu.SemaphoreType.DMA((n,)), pltpu.SemaphoreType.DMA((n,)), pltpu.SemaphoreType.DMA((n,))],
        name=name)(*halves)


VM = pl.BlockSpec(memory_space=pltpu.VMEM)


def small_allgather(buf, name="small_allgather"):
    def body(in_ref, out_ref, send_sems, recv_sems):
        x, y, c, me, others = _place()
        out_ref[me] = in_ref[...]
        copies = []
        for k, (ox, oy) in enumerate(others):
            cp = _remote(in_ref, out_ref.at[me], send_sems.at[k], recv_sems.at[k], (ox, oy, c))
            cp.start()
            copies.append(cp)
        for k, (ox, oy) in enumerate(others):
            landed = out_ref.at[2 * ox + oy]
            _remote(landed, landed, send_sems.at[k], recv_sems.at[k], (ox, oy, c)).wait_recv()
        for cp in copies:
            cp.wait_send()

    return pl.pallas_call(
        body, out_shape=jax.ShapeDtypeStruct((N_CHIPS,) + buf.shape, buf.dtype), in_specs=[VM], out_specs=VM,
        scratch_shapes=[pltpu.SemaphoreType.DMA((3,)), pltpu.SemaphoreType.DMA((3,))],
        compiler_params=pltpu.CompilerParams(vmem_limit_bytes=V7X_VMEM_LIMIT), name=name)(buf)


def small_allreduce(buf, name="small_allreduce"):
    def body(in_ref, out_ref, sib_ref, slot_ref, send_sems, recv_sems):
        x, y, c, me, others = _place()
        cp = _remote(in_ref, sib_ref, send_sems.at[3], recv_sems.at[3], (x, y, 1 - c))
        cp.start()
        cp.wait()
        slot_ref[me] = in_ref[...] + sib_ref[...]
        copies = []
        for k, (ox, oy) in enumerate(others):
            cp = _remote(slot_ref.at[me], slot_ref.at[me], send_sems.at[k], recv_sems.at[k], (ox, oy, c))
            cp.start()
            copies.append(cp)
        for k, (ox, oy) in enumerate(others):
            landed = slot_ref.at[2 * ox + oy]
            _remote(landed, landed, send_sems.at[k], recv_sems.at[k], (ox, oy, c)).wait_recv()
        for cp in copies:
            cp.wait_send()
        out_ref[...] = ((slot_ref[0] + slot_ref[1]) + slot_ref[2]) + slot_ref[3]

    return pl.pallas_call(
        body, out_shape=jax.ShapeDtypeStruct(buf.shape, buf.dtype), in_specs=[VM], out_specs=VM,
        scratch_shapes=[pltpu.VMEM(buf.shape, buf.dtype), pltpu.VMEM((N_CHIPS,) + buf.shape, buf.dtype),
                        pltpu.SemaphoreType.DMA((4,)), pltpu.SemaphoreType.DMA((4,))],
        compiler_params=pltpu.CompilerParams(vmem_limit_bytes=V7X_VMEM_LIMIT), name=name)(buf)


def _pack_rows(arrays, row_multiple):
    flat = jnp.concatenate([a.reshape(-1) for a in arrays])
    rows = -(-flat.shape[0] // (LANES * row_multiple)) * row_multiple
    return jnp.pad(flat, (0, rows * LANES - flat.shape[0])).reshape(rows, LANES)


def _unpack_rows(buf, shapes):
    flat = buf.reshape(-1)
    out, at = [], 0
    for s in shapes:
        n = math.prod(s)
        out.append(flat[at:at + n].reshape(s))
        at += n
    return out


def _layer_weights(i):
    j = i // 2
    mixer = "a" if i % 2 == 0 else "b"
    return [("w_mem_kv", i), (mixer + "_w_in", j), (mixer + "_w_out", j), ("ffn_w_up", i), ("ffn_w_down", i)]


def _layer_fwd(i, x, mem, w, small):
    is_a = i % 2 == 0
    j = i // 2
    wkv, win, wout, wup, wdn = w
    wkv = wkv.reshape(1, D_MODEL, 2 * MEM_W)
    wdn = wdn.reshape(1, FF, D_MODEL)
    h1 = rms_fwd(x, small["mix_norm_g"][i], BF16, name=f"mix_norm{i}")
    mem_n = rms_fwd(mem, small["mem_norm_g"][i], BF16, name=f"mem_norm{i}")
    kv = mm_nn(mem_n, wkv, F32, name=f"mem_kv{i}")
    proj = mm_nn(h1, win, F32, name=f"in_proj{i}")
    saved = dict(x0=x, h1=h1, mem_n=mem_n, kv=kv, proj=proj)
    if is_a:
        outs, lses = zip(*[attn_fwd(proj, g, name=f"attn_fwd{i}_{g}") for g in range(3)])
        comb, lse = attn_combine(outs, lses, name=f"attn_combine{i}")
        mem_out = mem_fwd(proj, 3 * A_QKV_W, kv, name=f"mem_fwd{i}")
        cat = jnp.concatenate([comb.astype(BF16), mem_out], axis=1)
        saved.update(comb=comb, lse=lse)
    else:
        wout = wout.reshape(1, B_W + MEM_W, D_MODEL)
        tok = sgu_fwd(proj, small["b_v_norm_g"][j], small["b_w_s"][j], small["bias_b"][j], name=f"sgu_fwd{i}")
        mem_out = mem_fwd(proj, 2 * B_W, kv, name=f"mem_fwd{i}")
        cat = jnp.concatenate([tok, mem_out], axis=1)
    x1 = mm_nn(cat, wout, F32, res=x, name=f"out_proj{i}")
    h2 = rms_fwd(x1, small["ffn_norm_g"][i], BF16, name=f"ffn_norm{i}")
    a = mm_nn(h2, wup, F32, name=f"ffn_up{i}")
    act = ffn_act_fwd(a, small["ffn_conv_w"][i], small["ffn_conv_b"][i], name=f"ffn_act{i}")
    x2 = mm_nn(act, wdn, F32, res=x1, name=f"ffn_down{i}")
    saved.update(cat=cat, x1=x1, h2=h2, a=a, act=act)
    return x2, saved


def _layer_bwd(i, dx2, mem, w, small, sv):
    is_a = i % 2 == 0
    j = i // 2
    wkv, win, wout, wup, wdn = w
    wkv = wkv.reshape(1, D_MODEL, 2 * MEM_W)
    wdn = wdn.reshape(1, FF, D_MODEL)
    sg = {}
    dact = mm_nt(dx2, wdn, F32, name=f"d_act{i}")
    d_wdn = mm_tn(sv["act"], dx2, 1, BF16, name=f"d_wdown{i}").reshape(N_CHIPS, FF // N_CHIPS, D_MODEL)
    da, sg["ffn_conv_w"], sg["ffn_conv_b"] = ffn_act_bwd(sv["a"], small["ffn_conv_w"][i], small["ffn_conv_b"][i], dact,
                                                          name=f"ffn_act_bwd{i}")
    d_wup = mm_tn(sv["h2"], da, N_CHIPS, BF16, name=f"d_wup{i}")
    dh2 = mm_nt(da, wup, F32, name=f"d_h2_{i}")
    dx1, sg["ffn_norm_g"] = rms_bwd(sv["x1"], small["ffn_norm_g"][i], dh2, dres=dx2, name=f"ffn_norm_bwd{i}")
    proj, kv = sv["proj"], sv["kv"]
    if is_a:
        dcat = mm_nt(dx1, wout, F32, name=f"d_cat{i}")
        d_wout = mm_tn(sv["cat"], dx1, N_CHIPS, BF16, name=f"d_wout{i}")
        dqm, dkv = mem_bwd(proj, 3 * A_QKV_W, kv, dcat, A_OUT_W, name=f"mem_bwd{i}")
        parts = [attn_bwd(proj, dcat, sv["comb"], sv["lse"], g, name=f"attn_bwd{i}_{g}") for g in range(3)]
        dproj = jnp.concatenate([parts[g][p] for p in range(3) for g in range(3)] + [dqm], axis=1)
    else:
        dcat = mm_nt(dx1, wout.reshape(1, B_W + MEM_W, D_MODEL), F32, name=f"d_cat{i}")
        d_wout = mm_tn(sv["cat"], dx1, 1, BF16, name=f"d_wout{i}").reshape(N_CHIPS, (B_W + MEM_W) // N_CHIPS, D_MODEL)
        dqm, dkv = mem_bwd(proj, 2 * B_W, kv, dcat, B_W, name=f"mem_bwd{i}")
        w_s = small["b_w_s"][j]
        duv, sg["b_w_s"], dmix, sg["b_v_norm_g"] = sgu_bwd(proj, small["b_v_norm_g"][j], w_s, jnp.swapaxes(w_s, 1, 2),
                                                           small["bias_b"][j], dcat, name=f"sgu_bwd{i}")
        sg["b_s_bias"] = jnp.sum(dmix, axis=-1)
        dproj = jnp.concatenate([duv, dqm], axis=1)
    d_wkv = mm_tn(sv["mem_n"], dkv, 1, BF16, name=f"d_wkv{i}").reshape(N_CHIPS, D_MODEL // N_CHIPS, 2 * MEM_W)
    dmem_n = mm_nt(dkv, wkv, F32, name=f"d_mem_n{i}")
    _, sg["mem_norm_g"] = rms_bwd(mem, small["mem_norm_g"][i], dmem_n, name=f"mem_norm_bwd{i}")
    d_win = mm_tn(sv["h1"], dproj, N_CHIPS, BF16, name=f"d_win{i}")
    dh1 = mm_nt(dproj, win, F32, name=f"d_h1_{i}")
    dx0, sg["mix_norm_g"] = rms_bwd(sv["x0"], small["mix_norm_g"][i], dh1, dres=dx1, name=f"mix_norm_bwd{i}")
    return dx0, [d_wkv, d_win, d_wout, d_wup, d_wdn], sg


def _reduce_scatter(grads, core, tag):
    got = swap_halves(grads, name=f"swap_halves{tag}")
    sums = [pair_sum(g, o, core, name=f"pair_sum{tag}_{k}") for k, (g, o) in enumerate(zip(grads, got))]
    parts = exchange_chip_sums(sums, name=f"exchange{tag}")
    halves = [chip_sum(p, name=f"chip_sum{tag}_{k}") for k, p in enumerate(parts)]
    return join_halves(halves, name=f"join_halves{tag}")


SMALL_SHARDED = ("b_v_norm_g", "ffn_conv_w")
SMALL_FULL_SHAPES = dict(mix_norm_g=(D_MODEL,), ffn_norm_g=(D_MODEL,), mem_norm_g=(D_MODEL,), b_v_norm_g=(B_W,),
                         b_w_s=(B_GROUPS, CHUNK, CHUNK), b_s_bias=(B_GROUPS, CHUNK), ffn_conv_w=(3, 2 * FF),
                         ffn_conv_b=(2 * FF,))
BIG = ("w_mem_kv", "a_w_in", "a_w_out", "b_w_in", "b_w_out", "ffn_w_up", "ffn_w_down")
WEIGHT_ORDER = ("mix_norm_g", "ffn_norm_g", "mem_norm_g", "w_mem_kv", "a_w_in", "a_w_out", "b_w_in", "b_v_norm_g", "b_w_s",
                "b_s_bias", "b_w_out", "ffn_w_up", "ffn_conv_w", "ffn_conv_b", "ffn_w_down", "final_norm_g")


def kernel(x, mem, mix_norm_g, ffn_norm_g, mem_norm_g, w_mem_kv, a_w_in, a_w_out, b_w_in, b_v_norm_g, b_w_s, b_s_bias, b_w_out, ffn_w_up, ffn_conv_w, ffn_conv_b, ffn_w_down, final_norm_g, loss_target, m_mix_norm_g, m_ffn_norm_g, m_mem_norm_g, m_w_mem_kv, m_a_w_in, m_a_w_out, m_b_w_in, m_b_v_norm_g, m_b_w_s, m_b_s_bias, m_b_w_out, m_ffn_w_up, m_ffn_conv_w, m_ffn_conv_b, m_ffn_w_down, m_final_norm_g, v_mix_norm_g, v_ffn_norm_g, v_mem_norm_g, v_w_mem_kv, v_a_w_in, v_a_w_out, v_b_w_in, v_b_v_norm_g, v_b_w_s, v_b_s_bias, v_b_w_out, v_ffn_w_up, v_ffn_conv_w, v_ffn_conv_b, v_ffn_w_down, v_final_norm_g):
    weights = dict(mix_norm_g=mix_norm_g, ffn_norm_g=ffn_norm_g, mem_norm_g=mem_norm_g, w_mem_kv=w_mem_kv, a_w_in=a_w_in,
                   a_w_out=a_w_out, b_w_in=b_w_in, b_v_norm_g=b_v_norm_g, b_w_s=b_w_s, b_s_bias=b_s_bias, b_w_out=b_w_out,
                   ffn_w_up=ffn_w_up, ffn_conv_w=ffn_conv_w, ffn_conv_b=ffn_conv_b, ffn_w_down=ffn_w_down,
                   final_norm_g=final_norm_g)
    mom1 = dict(mix_norm_g=m_mix_norm_g, ffn_norm_g=m_ffn_norm_g, mem_norm_g=m_mem_norm_g, w_mem_kv=m_w_mem_kv,
                a_w_in=m_a_w_in, a_w_out=m_a_w_out, b_w_in=m_b_w_in, b_v_norm_g=m_b_v_norm_g, b_w_s=m_b_w_s,
                b_s_bias=m_b_s_bias, b_w_out=m_b_w_out, ffn_w_up=m_ffn_w_up, ffn_conv_w=m_ffn_conv_w,
                ffn_conv_b=m_ffn_conv_b, ffn_w_down=m_ffn_w_down, final_norm_g=m_final_norm_g)
    mom2 = dict(mix_norm_g=v_mix_norm_g, ffn_norm_g=v_ffn_norm_g, mem_norm_g=v_mem_norm_g, w_mem_kv=v_w_mem_kv,
                a_w_in=v_a_w_in, a_w_out=v_a_w_out, b_w_in=v_b_w_in, b_v_norm_g=v_b_v_norm_g, b_w_s=v_b_w_s,
                b_s_bias=v_b_s_bias, b_w_out=v_b_w_out, ffn_w_up=v_ffn_w_up, ffn_conv_w=v_ffn_conv_w,
                ffn_conv_b=v_ffn_conv_b, ffn_w_down=v_ffn_w_down, final_norm_g=v_final_norm_g)
    chip = 2 * lax.axis_index("x") + lax.axis_index("y")
    core = lax.axis_index("c").astype(jnp.int32).reshape(1)
    x0, mem0, target = x[0], mem[0], loss_target[0]
    depth = DEPTH

    n_cw, n_vg = ffn_conv_w.size, b_v_norm_g.size
    gathered = small_allgather(_pack_rows([ffn_conv_w, b_v_norm_g], 8)).reshape(N_CHIPS, -1)
    conv_w_full = gathered[:, :n_cw].reshape(N_CHIPS, DEPTH, 3, 2 * FF // N_CHIPS).transpose(1, 2, 0, 3).reshape(DEPTH, 3, 2 * FF)
    vgain_full = gathered[:, n_cw:n_cw + n_vg].reshape(N_CHIPS, 2, B_W // N_CHIPS).transpose(1, 0, 2).reshape(2, B_W)
    small = dict(mix_norm_g=mix_norm_g, ffn_norm_g=ffn_norm_g, mem_norm_g=mem_norm_g, b_w_s=b_w_s, ffn_conv_b=ffn_conv_b,
                 ffn_conv_w=conv_w_full, b_v_norm_g=vgain_full,
                 bias_b=jnp.broadcast_to(b_s_bias[..., None], b_s_bias.shape + (CHUNK,)))

    gathered_w, saved = [], []
    h = x0
    for i in range(depth):
        shards = [weights[n][l].astype(BF16) for n, l in _layer_weights(i)]
        gathered_w.append(gather_weights(shards, name=f"gather_weights{i}"))
        h, sv = _layer_fwd(i, h, mem0, gathered_w[i], small)
        saved.append(sv)
    loss_row, dh, d_final = final_loss(h, final_norm_g, target)
    loss = lax.psum(loss_row[0, 0], ("x", "y", "c"))

    names = [n for n in WEIGHT_ORDER if n not in BIG]
    small_g = {n: [None] * weights[n].shape[0] for n in names if n != "final_norm_g"}
    big_out = {n: None for n in BIG}
    for i in reversed(range(depth)):
        dh, big_g, sg = _layer_bwd(i, dh, mem0, gathered_w[i], small, saved[i])
        for n, g in sg.items():
            small_g[n][i if len(small_g[n]) == depth else i // 2] = g.reshape(SMALL_FULL_SHAPES[n])
        reduced = _reduce_scatter(big_g, core, i)
        for (n, l), g in zip(_layer_weights(i), reduced):
            big_out[n] = adamw_layer(weights[n], mom1[n], mom2[n], l, g, big_out[n], name=f"adamw_{n}{l}")

    full_g = {n: (d_final.reshape(-1) if n == "final_norm_g" else jnp.stack(small_g[n])) for n in names}
    shapes = [full_g[n].shape for n in names]
    summed = dict(zip(names, _unpack_rows(small_allreduce(_pack_rows([full_g[n] for n in names], 8)), shapes)))
    for n in SMALL_SHARDED:
        width = weights[n].shape[-1]
        summed[n] = lax.dynamic_slice_in_dim(summed[n], chip * width, width, axis=summed[n].ndim - 1)
    own_shapes = [weights[n].shape for n in names]
    pack = lambda d: _pack_rows([d[n] for n in names], 128)
    small_out = [_unpack_rows(b, own_shapes) for b in adamw_flat(pack(weights), pack(summed), pack(mom1), pack(mom2))]
    outs = {}
    for k, n in enumerate(names):
        outs[n] = (summed[n], small_out[0][k], small_out[1][k], small_out[2][k])
    outs.update(big_out)
    return (loss, dh[None], *[outs[n][0] for n in WEIGHT_ORDER], *[outs[n][1] for n in WEIGHT_ORDER],
            *[outs[n][2] for n in WEIGHT_ORDER], *[outs[n][3] for n in WEIGHT_ORDER])
```

```python
import functools
import math

import numpy as np
import jax
import jax.numpy as jnp
from jax import lax
from jax.experimental import pallas as pl
from jax.experimental.pallas import tpu as pltpu

F32 = jnp.float32
BF16 = jnp.bfloat16
MESH = pl.DeviceIdType.MESH

D_MODEL = 2048
SEQ = 2048
DEPTH = 4
EPS = 1e-6
NEG = -1e30
HEAD = 128
A_PATTERNS = ((128, 1), (512, 4), (2048, 16))
A_QKV_W = 1536
A_OUT_W = 512
A_IN = 5120
QBLK = 128
N_SIDE = 64
CHUNK = 128
B_GROUPS = 12
B_W = 1536
B_IN = 3584
MEM_LEN = 256
MEM_HEADS = 4
MEM_W = 512
FF = 5632
ADAM_LR, ADAM_B1, ADAM_B2, ADAM_EPS, ADAM_WD, ADAM_STEP = 0.001, 0.9, 0.999, 1e-08, 0.01, 10
N_CHIPS = 4

LANES = 128
V7X_VMEM_LIMIT = 56 * 1024 * 1024


def _cp(*sem):
    return pltpu.CompilerParams(dimension_semantics=sem, vmem_limit_bytes=V7X_VMEM_LIMIT)


def _pick(dim, prefs):
    for p in prefs:
        if dim % p == 0:
            return p
    raise ValueError(f"no tile for {dim} in {prefs}")


def _gelu_parts(x):
    cdf = 0.5 * (1.0 + lax.erf(x * (1.0 / math.sqrt(2.0))))
    pdf = jnp.exp(-0.5 * x * x) * (1.0 / math.sqrt(2.0 * math.pi))
    return x * cdf, cdf + x * pdf


def _gelu(x):
    return 0.5 * x * (1.0 + lax.erf(x * (1.0 / math.sqrt(2.0))))


TM_PREFS = (1024, 512, 256, 128)
TN_PREFS = (1408, 1280, 1024, 896, 512, 256, 128)
TK_PREFS = (2048, 1408, 1280, 1024, 896, 512, 256, 128)


def _mm_body(nk, dims, has_res):
    def body(*refs):
        if has_res:
            a_ref, b_ref, r_ref, o_ref = refs[:4]
        else:
            a_ref, b_ref, o_ref = refs[:3]
            r_ref = None
        part = lax.dot_general(a_ref[...].astype(BF16), b_ref[...].astype(BF16), dims,
                               preferred_element_type=F32)
        if nk == 1:
            if has_res:
                part = part + r_ref[...]
            o_ref[...] = part.astype(o_ref.dtype)
            return
        acc_ref = refs[-1]
        k = pl.program_id(2)

        @pl.when(k == 0)
        def _():
            acc_ref[...] = part

        @pl.when(k > 0)
        def _():
            acc_ref[...] += part

        @pl.when(k == nk - 1)
        def _():
            tot = acc_ref[...]
            if has_res:
                tot = tot + r_ref[...]
            o_ref[...] = tot.astype(o_ref.dtype)
    return body


def mm_nn(a, w, out_dtype, res=None, name="mm_nn"):
    m, kw = a.shape
    ns_, kw2, nsz = w.shape
    assert kw == kw2
    n = ns_ * nsz
    tm, tn, tk = _pick(m, TM_PREFS), _pick(nsz, TN_PREFS), _pick(kw, TK_PREFS)
    nb, nk = nsz // tn, kw // tk
    in_specs = [pl.BlockSpec((tm, tk), lambda i, j, k: (i, k)),
                pl.BlockSpec((None, tk, tn), lambda i, j, k: (j // nb, k, j % nb))]
    args = [a, w]
    if res is not None:
        in_specs.append(pl.BlockSpec((tm, tn), lambda i, j, k: (i, j)))
        args.append(res)
    return pl.pallas_call(
        _mm_body(nk, (((1,), (0,)), ((), ())), res is not None),
        out_shape=jax.ShapeDtypeStruct((m, n), out_dtype),
        grid=(m // tm, n // tn, nk), in_specs=in_specs,
        out_specs=pl.BlockSpec((tm, tn), lambda i, j, k: (i, j)),
        scratch_shapes=[pltpu.VMEM((tm, tn), F32)] if nk > 1 else [],
        compiler_params=_cp("parallel", "parallel", "arbitrary"), name=name)(*args)


def mm_nt(g, w, out_dtype, name="mm_nt"):
    m, n = g.shape
    ns_, kw, nsz = w.shape
    assert n == ns_ * nsz
    tm, tn, tk = _pick(m, TM_PREFS), _pick(kw, TN_PREFS), _pick(nsz, TK_PREFS)
    nb, nk = nsz // tk, n // tk
    return pl.pallas_call(
        _mm_body(nk, (((1,), (1,)), ((), ())), False),
        out_shape=jax.ShapeDtypeStruct((m, kw), out_dtype),
        grid=(m // tm, kw // tn, nk),
        in_specs=[pl.BlockSpec((tm, tk), lambda i, j, k: (i, k)),
                  pl.BlockSpec((None, tn, tk), lambda i, j, k: (k // nb, j, k % nb))],
        out_specs=pl.BlockSpec((tm, tn), lambda i, j, k: (i, j)),
        scratch_shapes=[pltpu.VMEM((tm, tn), F32)] if nk > 1 else [],
        compiler_params=_cp("parallel", "parallel", "arbitrary"), name=name)(g, w)


def mm_tn(a, g, n_shards, out_dtype, name="mm_tn"):
    t, kw = a.shape
    t2, n = g.shape
    assert t == t2
    nsz = n // n_shards
    tm, tn, tk = _pick(kw, TM_PREFS), _pick(nsz, TN_PREFS), _pick(t, TK_PREFS)
    nb, nk = nsz // tn, t // tk
    return pl.pallas_call(
        _mm_body(nk, (((0,), (0,)), ((), ())), False),
        out_shape=jax.ShapeDtypeStruct((n_shards, kw, nsz), out_dtype),
        grid=(kw // tm, n // tn, nk),
        in_specs=[pl.BlockSpec((tk, tm), lambda i, j, k: (k, i)),
                  pl.BlockSpec((tk, tn), lambda i, j, k: (k, j))],
        out_specs=pl.BlockSpec((None, tm, tn), lambda i, j, k: (j // nb, i, j % nb)),
        scratch_shapes=[pltpu.VMEM((tm, tn), F32)] if nk > 1 else [],
        compiler_params=_cp("parallel", "parallel", "arbitrary"), name=name)(a, g)


ROW_TILE = 256


def _rms_stats(x):
    r = lax.rsqrt(jnp.mean(x * x, axis=-1, keepdims=True) + EPS)
    return r, x * r


def _rms_back(xh, r, g, dh):
    u = dh * g
    return r * (u - xh * jnp.mean(u * xh, axis=-1, keepdims=True))


def rms_fwd(x, g, out_dtype, name="rms_fwd"):
    rows, d = x.shape
    tr = _pick(rows, (ROW_TILE, 128))

    def body(x_ref, g_ref, o_ref):
        _, xh = _rms_stats(x_ref[...])
        o_ref[...] = (xh * g_ref[...]).astype(o_ref.dtype)

    return pl.pallas_call(
        body, out_shape=jax.ShapeDtypeStruct((rows, d), out_dtype), grid=(rows // tr,),
        in_specs=[pl.BlockSpec((tr, d), lambda i: (i, 0)), pl.BlockSpec((1, d), lambda i: (0, 0))],
        out_specs=pl.BlockSpec((tr, d), lambda i: (i, 0)),
        compiler_params=_cp("parallel"), name=name)(x, g.reshape(1, d))


def rms_bwd(x, g, dh, dres=None, name="rms_bwd"):
    rows, d = x.shape
    tr = _pick(rows, (ROW_TILE, 128))
    has_res = dres is not None

    def body(*refs):
        if has_res:
            x_ref, g_ref, dh_ref, dres_ref, dx_ref, dg_ref = refs
        else:
            x_ref, g_ref, dh_ref, dx_ref, dg_ref = refs
        r, xh = _rms_stats(x_ref[...])
        dh_ = dh_ref[...].astype(F32)
        part = jnp.sum(dh_ * xh, axis=0, keepdims=True)

        @pl.when(pl.program_id(0) == 0)
        def _():
            dg_ref[...] = part

        @pl.when(pl.program_id(0) > 0)
        def _():
            dg_ref[...] += part

        dx = _rms_back(xh, r, g_ref[...], dh_)
        if has_res:
            dx = dx + dres_ref[...]
        dx_ref[...] = dx

    row_spec = pl.BlockSpec((tr, d), lambda i: (i, 0))
    vec_spec = pl.BlockSpec((1, d), lambda i: (0, 0))
    args = [x, g.reshape(1, d), dh] + ([dres] if has_res else [])
    return pl.pallas_call(
        body, out_shape=(jax.ShapeDtypeStruct((rows, d), F32), jax.ShapeDtypeStruct((1, d), F32)),
        grid=(rows // tr,), in_specs=[row_spec, vec_spec, row_spec] + ([row_spec] if has_res else []),
        out_specs=(row_spec, vec_spec), compiler_params=_cp("arbitrary"), name=name)(*args)


def final_loss(x, g, target, name="final_loss"):
    rows, d = x.shape
    tr = _pick(rows, (ROW_TILE, 128))

    def body(x_ref, g_ref, t_ref, loss_ref, dx_ref, dg_ref):
        r, xh = _rms_stats(x_ref[...])
        gain = g_ref[...]
        err = xh * gain - t_ref[...]
        sq = jnp.sum(jnp.sum(err * err, axis=1, keepdims=True), axis=0, keepdims=True) * (0.5 / d)
        dy = err * (1.0 / d)
        part = jnp.sum(dy * xh, axis=0, keepdims=True)

        @pl.when(pl.program_id(0) == 0)
        def _():
            dg_ref[...] = part
            loss_ref[...] = jnp.broadcast_to(sq, loss_ref.shape)

        @pl.when(pl.program_id(0) > 0)
        def _():
            dg_ref[...] += part
            loss_ref[...] += jnp.broadcast_to(sq, loss_ref.shape)

        dx_ref[...] = _rms_back(xh, r, gain, dy)

    row_spec = pl.BlockSpec((tr, d), lambda i: (i, 0))
    vec_spec = pl.BlockSpec((1, d), lambda i: (0, 0))
    return pl.pallas_call(
        body, out_shape=(jax.ShapeDtypeStruct((1, LANES), F32), jax.ShapeDtypeStruct((rows, d), F32),
                         jax.ShapeDtypeStruct((1, d), F32)),
        grid=(rows // tr,), in_specs=[row_spec, vec_spec, row_spec],
        out_specs=(pl.BlockSpec((1, LANES), lambda i: (0, 0)), row_spec, vec_spec),
        compiler_params=_cp("arbitrary"), name=name)(x, g.reshape(1, d), target)


def _alibi_slopes():
    return (2.0 ** (-8.0 * (np.arange(12) + 1) / 12)).astype(np.float32)


def _band_scores(q, k, q0, start, wk, slope):
    s = lax.dot_general(q, k, (((1,), (1,)), ((), ())), preferred_element_type=F32) * (HEAD ** -0.5)
    qpos = q0 + lax.broadcasted_iota(jnp.int32, (QBLK, wk), 0)
    kpos = start + lax.broadcasted_iota(jnp.int32, (QBLK, wk), 1)
    rel = jnp.abs(qpos - kpos)
    return jnp.where(rel <= N_SIDE, s - slope * rel.astype(F32), NEG)


def _attn_geometry(seq, dilation):
    length = seq // dilation
    return length, length // QBLK, min(2 * QBLK, length)


def _attn_window(n, length, wk):
    q0 = pl.multiple_of(n * QBLK, QBLK)
    start = pl.multiple_of(jnp.clip(n * QBLK - N_SIDE, 0, length - wk), N_SIDE)
    return q0, start


def attn_fwd(proj, group, name):
    seq = proj.shape[0]
    dilation = A_PATTERNS[group][1]
    length, nblk, wk = _attn_geometry(seq, dilation)
    cols = A_IN // HEAD
    pv = proj.reshape(length, dilation * A_IN)

    def body(slope_ref, q_ref, k_ref, v_ref, o_ref, lse_ref):
        slope = slope_ref[group * 4 + pl.program_id(1)] * float(dilation)

        def blk(n, carry):
            q0, start = _attn_window(n, length, wk)
            q = q_ref[pl.ds(q0, QBLK), :].astype(BF16)
            k = k_ref[pl.ds(start, wk), :].astype(BF16)
            v = v_ref[pl.ds(start, wk), :].astype(BF16)
            s = _band_scores(q, k, q0, start, wk, slope)
            m = jnp.max(s, axis=-1, keepdims=True)
            p = jnp.exp(s - m)
            l = jnp.sum(p, axis=-1, keepdims=True)
            o = jnp.dot(p.astype(BF16), v, preferred_element_type=F32) / l
            o_ref[pl.ds(q0, QBLK), :] = o
            lse_ref[pl.ds(q0, QBLK), :] = jnp.broadcast_to(m + jnp.log(l), (QBLK, HEAD))
            return carry

        lax.fori_loop(0, nblk, blk, 0)

    def part(p):
        return pl.BlockSpec((length, HEAD), lambda r, h: (0, r * cols + p * 12 + group * 4 + h))

    out_spec = pl.BlockSpec((length, HEAD), lambda r, h: (0, r * 4 + h))
    o, lse = pl.pallas_call(
        body, out_shape=(jax.ShapeDtypeStruct((length, dilation * A_OUT_W), F32),) * 2,
        grid=(dilation, 4),
        in_specs=[pl.BlockSpec(memory_space=pltpu.SMEM), part(0), part(1), part(2)],
        out_specs=(out_spec, out_spec), compiler_params=_cp("parallel", "parallel"), name=name,
    )(jnp.asarray(_alibi_slopes()), pv, pv, pv)
    return o.reshape(seq, A_OUT_W), lse.reshape(seq, A_OUT_W)


def attn_combine(os_, lses, name="attn_combine"):
    seq = os_[0].shape[0]
    tr = ROW_TILE

    def body(o0, o1, o2, l0, l1, l2, c_ref, lse_ref):
        a, b, c = l0[...], l1[...], l2[...]
        m = jnp.maximum(jnp.maximum(a, b), c)
        ea, eb, ec = jnp.exp(a - m), jnp.exp(b - m), jnp.exp(c - m)
        den = ea + eb + ec
        c_ref[...] = (ea * o0[...] + eb * o1[...] + ec * o2[...]) / den
        lse_ref[...] = m + jnp.log(den)

    spec = pl.BlockSpec((tr, A_OUT_W), lambda i: (i, 0))
    return pl.pallas_call(
        body, out_shape=(jax.ShapeDtypeStruct((seq, A_OUT_W), F32),) * 2, grid=(seq // tr,),
        in_specs=[spec] * 6, out_specs=(spec, spec), compiler_params=_cp("parallel"), name=name)(*os_, *lses)


def attn_bwd(proj, dcat, comb, lse, group, name):
    seq = proj.shape[0]
    dilation = A_PATTERNS[group][1]
    length, nblk, wk = _attn_geometry(seq, dilation)
    cols = A_IN // HEAD
    pv = proj.reshape(length, dilation * A_IN)
    view = lambda a: a.reshape(length, dilation * A_OUT_W)
    scale = HEAD ** -0.5

    def body(slope_ref, q_ref, k_ref, v_ref, do_ref, c_ref, lse_ref, dq_ref, dk_ref, dv_ref, dk_acc, dv_acc):
        slope = slope_ref[group * 4 + pl.program_id(1)] * float(dilation)
        dk_acc[...] = jnp.zeros_like(dk_acc)
        dv_acc[...] = jnp.zeros_like(dv_acc)

        def blk(n, carry):
            q0, start = _attn_window(n, length, wk)
            rows = pl.ds(q0, QBLK)
            keys = pl.ds(start, wk)
            q = q_ref[rows, :].astype(BF16)
            k = k_ref[keys, :].astype(BF16)
            v = v_ref[keys, :].astype(BF16)
            do = do_ref[rows, :]
            s = _band_scores(q, k, q0, start, wk, slope)
            p = jnp.exp(s - lse_ref[rows, :][:, :1])
            delta = jnp.sum(do * c_ref[rows, :], axis=-1, keepdims=True)
            do16 = do.astype(BF16)
            dp = lax.dot_general(do16, v, (((1,), (1,)), ((), ())), preferred_element_type=F32)
            ds = (p * (dp - delta) * scale).astype(BF16)
            p16 = p.astype(BF16)
            dq_ref[rows, :] = jnp.dot(ds, k, preferred_element_type=F32).astype(dq_ref.dtype)
            dk_acc[keys, :] += lax.dot_general(ds, q, (((0,), (0,)), ((), ())), preferred_element_type=F32)
            dv_acc[keys, :] += lax.dot_general(p16, do16, (((0,), (0,)), ((), ())), preferred_element_type=F32)
            return carry

        lax.fori_loop(0, nblk, blk, 0)
        dk_ref[...] = dk_acc[...].astype(dk_ref.dtype)
        dv_ref[...] = dv_acc[...].astype(dv_ref.dtype)

    def part(p):
        return pl.BlockSpec((length, HEAD), lambda r, h: (0, r * cols + p * 12 + group * 4 + h))

    hs = pl.BlockSpec((length, HEAD), lambda r, h: (0, r * 4 + h))
    do_cols = dcat.shape[1] // HEAD
    do_spec = pl.BlockSpec((length, HEAD), lambda r, h: (0, r * do_cols + h))
    outs = pl.pallas_call(
        body, out_shape=(jax.ShapeDtypeStruct((length, dilation * A_OUT_W), BF16),) * 3,
        grid=(dilation, 4),
        in_specs=[pl.BlockSpec(memory_space=pltpu.SMEM), part(0), part(1), part(2), do_spec, hs, hs],
        out_specs=(hs, hs, hs),
        scratch_shapes=[pltpu.VMEM((length, HEAD), F32), pltpu.VMEM((length, HEAD), F32)],
        compiler_params=_cp("parallel", "parallel"), name=name,
    )(jnp.asarray(_alibi_slopes()), pv, pv, pv, dcat.reshape(length, dilation * dcat.shape[1]), view(comb), view(lse))
    return tuple(o.reshape(seq, A_OUT_W) for o in outs)


MEM_ROW_TILE = 512


def _mem_probs(q, k):
    s = lax.dot_general(q, k, (((1,), (1,)), ((), ())), preferred_element_type=F32) * (HEAD ** -0.5)
    p = jnp.exp(s - jnp.max(s, axis=-1, keepdims=True))
    return p / jnp.sum(p, axis=-1, keepdims=True)


def mem_fwd(proj, q_col, kv, name="mem_fwd"):
    seq = proj.shape[0]
    qb = q_col // HEAD

    def body(q_ref, k_ref, v_ref, o_ref):
        p = _mem_probs(q_ref[...].astype(BF16), k_ref[...].astype(BF16))
        o_ref[...] = jnp.dot(p.astype(BF16), v_ref[...].astype(BF16), preferred_element_type=F32).astype(o_ref.dtype)

    return pl.pallas_call(
        body, out_shape=jax.ShapeDtypeStruct((seq, MEM_W), BF16), grid=(MEM_HEADS, seq // MEM_ROW_TILE),
        in_specs=[pl.BlockSpec((MEM_ROW_TILE, HEAD), lambda h, i: (i, qb + h)),
                  pl.BlockSpec((MEM_LEN, HEAD), lambda h, i: (0, h)),
                  pl.BlockSpec((MEM_LEN, HEAD), lambda h, i: (0, MEM_HEADS + h))],
        out_specs=pl.BlockSpec((MEM_ROW_TILE, HEAD), lambda h, i: (i, h)),
        compiler_params=_cp("parallel", "parallel"), name=name)(proj, kv, kv)


def mem_bwd(proj, q_col, kv, dcat, do_col, name="mem_bwd"):
    seq = proj.shape[0]
    qb, ob = q_col // HEAD, do_col // HEAD
    scale = HEAD ** -0.5

    def body(q_ref, k_ref, v_ref, do_ref, dq_ref, dk_ref, dv_ref):
        q = q_ref[...].astype(BF16)
        k = k_ref[...].astype(BF16)
        v = v_ref[...].astype(BF16)
        do = do_ref[...].astype(BF16)
        p = _mem_probs(q, k)
        dp = lax.dot_general(do, v, (((1,), (1,)), ((), ())), preferred_element_type=F32)
        ds = (p * (dp - jnp.sum(dp * p, axis=-1, keepdims=True)) * scale).astype(BF16)
        dq_ref[...] = jnp.dot(ds, k, preferred_element_type=F32).astype(dq_ref.dtype)
        dk = lax.dot_general(ds, q, (((0,), (0,)), ((), ())), preferred_element_type=F32)
        dv = lax.dot_general(p.astype(BF16), do, (((0,), (0,)), ((), ())), preferred_element_type=F32)

        @pl.when(pl.program_id(1) == 0)
        def _():
            dk_ref[...] = dk
            dv_ref[...] = dv

        @pl.when(pl.program_id(1) > 0)
        def _():
            dk_ref[...] += dk
            dv_ref[...] += dv

    dq, dk, dv = pl.pallas_call(
        body, out_shape=(jax.ShapeDtypeStruct((seq, MEM_W), BF16), jax.ShapeDtypeStruct((MEM_LEN, MEM_W), F32),
                         jax.ShapeDtypeStruct((MEM_LEN, MEM_W), F32)),
        grid=(MEM_HEADS, seq // MEM_ROW_TILE),
        in_specs=[pl.BlockSpec((MEM_ROW_TILE, HEAD), lambda h, i: (i, qb + h)),
                  pl.BlockSpec((MEM_LEN, HEAD), lambda h, i: (0, h)),
                  pl.BlockSpec((MEM_LEN, HEAD), lambda h, i: (0, MEM_HEADS + h)),
                  pl.BlockSpec((MEM_ROW_TILE, HEAD), lambda h, i: (i, ob + h))],
        out_specs=(pl.BlockSpec((MEM_ROW_TILE, HEAD), lambda h, i: (i, h)),
                   pl.BlockSpec((MEM_LEN, HEAD), lambda h, i: (0, h)),
                   pl.BlockSpec((MEM_LEN, HEAD), lambda h, i: (0, h))),
        compiler_params=_cp("parallel", "arbitrary"), name=name)(proj, kv, kv, dcat)
    return dq, jnp.concatenate([dk, dv], axis=1)


def _sgu_front(x, gain):
    uv, duv = _gelu_parts(x)
    u, v = uv[:, :B_W], uv[:, B_W:]
    r, vh = _rms_stats(v)
    return u, duv, r, vh, vh * gain


def sgu_fwd(proj, gain, w_s, bias_b, name="sgu_fwd"):
    seq = proj.shape[0]

    def body(x_ref, gain_ref, ws_ref, bias_ref, o_ref):
        u, _, _, _, vn = _sgu_front(x_ref[...], gain_ref[...])
        for g in range(B_GROUPS):
            cs = slice(g * CHUNK, (g + 1) * CHUNK)
            mixed = jnp.dot(ws_ref[g].astype(BF16), vn[:, cs].astype(BF16), preferred_element_type=F32) + bias_ref[g]
            o_ref[:, cs] = (u[:, cs] * mixed).astype(o_ref.dtype)

    full = lambda shape: pl.BlockSpec(shape, lambda c: (0,) * len(shape))
    return pl.pallas_call(
        body, out_shape=jax.ShapeDtypeStruct((seq, B_W), BF16), grid=(seq // CHUNK,),
        in_specs=[pl.BlockSpec((CHUNK, 2 * B_W), lambda c: (c, 0)), full((1, B_W)),
                  full((B_GROUPS, CHUNK, CHUNK)), full((B_GROUPS, CHUNK, CHUNK))],
        out_specs=pl.BlockSpec((CHUNK, B_W), lambda c: (c, 0)),
        compiler_params=_cp("parallel"), name=name)(proj, gain.reshape(1, B_W), w_s, bias_b)


def sgu_bwd(proj, gain, w_s, w_s_t, bias_b, dcat, name="sgu_bwd"):
    seq = proj.shape[0]

    def body(x_ref, gain_ref, ws_ref, wst_ref, bias_ref, do_ref, dx_ref, dws_ref, dmix_ref, dgain_ref, dvn_ref):
        first = pl.program_id(0) == 0
        gain = gain_ref[...]
        u, duv, r, vh, vn = _sgu_front(x_ref[...], gain)
        do = do_ref[...]
        for g in range(B_GROUPS):
            cs = slice(g * CHUNK, (g + 1) * CHUNK)
            vg = vn[:, cs].astype(BF16)
            mixed = jnp.dot(ws_ref[g].astype(BF16), vg, preferred_element_type=F32) + bias_ref[g]
            dx_ref[:, cs] = (do[:, cs] * mixed * duv[:, cs]).astype(dx_ref.dtype)
            dmixed = do[:, cs] * u[:, cs]
            dm16 = dmixed.astype(BF16)
            dws = lax.dot_general(dm16, vg, (((1,), (1,)), ((), ())), preferred_element_type=F32)
            dvn_ref[:, cs] = jnp.dot(wst_ref[g].astype(BF16), dm16, preferred_element_type=F32)

            @pl.when(first)
            def _():
                dws_ref[g] = dws
                dmix_ref[g] = dmixed

            @pl.when(jnp.logical_not(first))
            def _():
                dws_ref[g] += dws
                dmix_ref[g] += dmixed

        dvn = dvn_ref[...]
        dgain = jnp.sum(dvn * vh, axis=0, keepdims=True)

        @pl.when(first)
        def _():
            dgain_ref[...] = dgain

        @pl.when(jnp.logical_not(first))
        def _():
            dgain_ref[...] += dgain

        dv = _rms_back(vh, r, gain, dvn)
        dx_ref[:, B_W:] = (dv * duv[:, B_W:]).astype(dx_ref.dtype)

    full = lambda shape: pl.BlockSpec(shape, lambda c: (0,) * len(shape))
    mats = full((B_GROUPS, CHUNK, CHUNK))
    return pl.pallas_call(
        body, out_shape=(jax.ShapeDtypeStruct((seq, 2 * B_W), BF16), jax.ShapeDtypeStruct((B_GROUPS, CHUNK, CHUNK), F32),
                         jax.ShapeDtypeStruct((B_GROUPS, CHUNK, CHUNK), F32), jax.ShapeDtypeStruct((1, B_W), F32)),
        grid=(seq // CHUNK,),
        in_specs=[pl.BlockSpec((CHUNK, 2 * B_W), lambda c: (c, 0)), full((1, B_W)), mats, mats, mats,
                  pl.BlockSpec((CHUNK, B_W), lambda c: (c, 0))],
        out_specs=(pl.BlockSpec((CHUNK, 2 * B_W), lambda c: (c, 0)), mats, mats, full((1, B_W))),
        scratch_shapes=[pltpu.VMEM((CHUNK, B_W), F32)],
        compiler_params=_cp("arbitrary"), name=name)(proj, gain.reshape(1, B_W), w_s, w_s_t, bias_b, dcat)


FFN_COLS = 256


def _shift_prev(a):
    rows = lax.broadcasted_iota(jnp.int32, a.shape, 0)
    return jnp.where(rows == 0, 0.0, pltpu.roll(a, 1, 0))


def _shift_next(a):
    n = a.shape[0]
    rows = lax.broadcasted_iota(jnp.int32, a.shape, 0)
    return jnp.where(rows == n - 1, 0.0, pltpu.roll(a, n - 1, 0))


def _conv3(a, w, b):
    return _shift_prev(a) * w[0:1] + a * w[1:2] + _shift_next(a) * w[2:3] + b


def ffn_act_fwd(a, conv_w, conv_b, name="ffn_act_fwd"):
    seq = a.shape[0]
    nb = FF // FFN_COLS

    def body(ag_ref, av_ref, wg_ref, wv_ref, bg_ref, bv_ref, o_ref):
        gate = _conv3(ag_ref[...], wg_ref[...], bg_ref[...])
        val = _conv3(av_ref[...], wv_ref[...], bv_ref[...])
        o_ref[...] = (_gelu(gate) * val).astype(o_ref.dtype)

    col = lambda rows, off: pl.BlockSpec((rows, FFN_COLS), lambda j: (0, j + off))
    cb = conv_b.reshape(1, 2 * FF)
    return pl.pallas_call(
        body, out_shape=jax.ShapeDtypeStruct((seq, FF), BF16), grid=(nb,),
        in_specs=[col(seq, 0), col(seq, nb), col(3, 0), col(3, nb), col(1, 0), col(1, nb)],
        out_specs=col(seq, 0), compiler_params=_cp("parallel"), name=name)(a, a, conv_w, conv_w, cb, cb)


def ffn_act_bwd(a, conv_w, conv_b, dact, name="ffn_act_bwd"):
    seq = a.shape[0]
    nb = FF // FFN_COLS

    def back(a_, w, dc, da_ref, dw_ref, db_ref):
        db_ref[...] = jnp.sum(dc, axis=0, keepdims=True)
        dw_ref[0:1, :] = jnp.sum(dc * _shift_prev(a_), axis=0, keepdims=True)
        dw_ref[1:2, :] = jnp.sum(dc * a_, axis=0, keepdims=True)
        dw_ref[2:3, :] = jnp.sum(dc * _shift_next(a_), axis=0, keepdims=True)
        da_ref[...] = (_shift_next(dc) * w[0:1] + dc * w[1:2] + _shift_prev(dc) * w[2:3]).astype(da_ref.dtype)

    def body(ag_ref, av_ref, wg_ref, wv_ref, bg_ref, bv_ref, d_ref, dag_ref, dav_ref, dwg_ref, dwv_ref, dbg_ref, dbv_ref):
        ag, av, wg, wv = ag_ref[...], av_ref[...], wg_ref[...], wv_ref[...]
        gate = _conv3(ag, wg, bg_ref[...])
        val = _conv3(av, wv, bv_ref[...])
        act, dact_dgate = _gelu_parts(gate)
        d = d_ref[...].astype(F32)
        back(ag, wg, d * val * dact_dgate, dag_ref, dwg_ref, dbg_ref)
        back(av, wv, d * act, dav_ref, dwv_ref, dbv_ref)

    col = lambda rows, off: pl.BlockSpec((rows, FFN_COLS), lambda j: (0, j + off))
    cb = conv_b.reshape(1, 2 * FF)
    dag, dav, dwg, dwv, dbg, dbv = pl.pallas_call(
        body, out_shape=(jax.ShapeDtypeStruct((seq, FF), BF16),) * 2 + (jax.ShapeDtypeStruct((3, FF), F32),) * 2
        + (jax.ShapeDtypeStruct((1, FF), F32),) * 2, grid=(nb,),
        in_specs=[col(seq, 0), col(seq, nb), col(3, 0), col(3, nb), col(1, 0), col(1, nb), col(seq, 0)],
        out_specs=(col(seq, 0), col(seq, 0), col(3, 0), col(3, 0), col(1, 0), col(1, 0)),
        compiler_params=_cp("parallel"), name=name)(a, a, conv_w, conv_w, cb, cb, dact)
    cat = lambda p, q: jnp.concatenate([p, q], axis=1)
    return cat(dag, dav), cat(dwg, dwv), cat(dbg, dbv)


def _adam_math(w, g, m, v):
    m = ADAM_B1 * m + (1.0 - ADAM_B1) * g
    v = ADAM_B2 * v + (1.0 - ADAM_B2) * (g * g)
    m_hat = m / (1.0 - ADAM_B1 ** ADAM_STEP)
    v_hat = v / (1.0 - ADAM_B2 ** ADAM_STEP)
    return -ADAM_LR * (m_hat / (jnp.sqrt(v_hat) + ADAM_EPS) + ADAM_WD * w), m, v


def _row_tile(rows, cols):
    return _pick(rows, (256, 128, 64)) if cols <= 1024 else _pick(rows, (128, 64))


def adamw_layer(w_all, m_all, v_all, layer, g, prev, name):
    n, rows, cols = w_all.shape
    tr = _row_tile(rows, cols)

    def body(w_ref, m_ref, v_ref, g_ref, *rest):
        go_ref, d_ref, mo_ref, vo_ref = rest[-4:]
        g_ = g_ref[...]
        d, m_, v_ = _adam_math(w_ref[...], g_, m_ref[...], v_ref[...])
        go_ref[...] = g_
        d_ref[...] = d
        mo_ref[...] = m_
        vo_ref[...] = v_

    lay = pl.BlockSpec((None, tr, cols), lambda i: (layer, i, 0))
    in_specs = [lay, lay, lay, pl.BlockSpec((tr, cols), lambda i: (i, 0))]
    args = [w_all, m_all, v_all, g]
    aliases = {}
    if prev is not None:
        in_specs += [pl.BlockSpec(memory_space=pl.ANY)] * 4
        args += list(prev)
        aliases = {4 + k: k for k in range(4)}
    return pl.pallas_call(
        body, out_shape=(jax.ShapeDtypeStruct(w_all.shape, F32),) * 4, grid=(rows // tr,),
        in_specs=in_specs, out_specs=(lay,) * 4, input_output_aliases=aliases,
        compiler_params=_cp("parallel"), name=name)(*args)


def adamw_flat(w, g, m, v, name="adamw_small"):
    rows, cols = w.shape
    tr = _pick(rows, (128, 8))

    def body(w_ref, g_ref, m_ref, v_ref, d_ref, mo_ref, vo_ref):
        d_ref[...], mo_ref[...], vo_ref[...] = _adam_math(w_ref[...], g_ref[...], m_ref[...], v_ref[...])

    spec = pl.BlockSpec((tr, cols), lambda i: (i, 0))
    return pl.pallas_call(
        body, out_shape=(jax.ShapeDtypeStruct(w.shape, F32),) * 3, grid=(rows // tr,),
        in_specs=[spec] * 4, out_specs=(spec,) * 3, compiler_params=_cp("parallel"), name=name)(w, g, m, v)


def pair_sum(dw, got, core, name):
    _, rows, cols = dw.shape
    half = rows // 2
    tr = _row_tile(half, cols)
    nrb = half // tr

    def body(c_ref, a_ref, b_ref, o_ref):
        o_ref[...] = (a_ref[...].astype(F32) + b_ref[...].astype(F32)).astype(o_ref.dtype)

    return pl.pallas_call(
        body, out_shape=jax.ShapeDtypeStruct((N_CHIPS, half, cols), BF16),
        grid_spec=pltpu.PrefetchScalarGridSpec(
            num_scalar_prefetch=1, grid=(N_CHIPS, nrb),
            in_specs=[pl.BlockSpec((None, tr, cols), lambda s, i, c_ref: (s, c_ref[0] * nrb + i, 0)),
                      pl.BlockSpec((None, tr, cols), lambda s, i, c_ref: (s, i, 0))],
            out_specs=pl.BlockSpec((None, tr, cols), lambda s, i, c_ref: (s, i, 0))),
        compiler_params=_cp("parallel", "parallel"), name=name)(core, dw, got)


def chip_sum(own, parts, place, name):
    _, half, cols = parts.shape
    tr = _row_tile(half, cols)
    nrb = half // tr

    def body(p_ref, own_ref, a_ref, b_ref, c_ref, o_ref):
        o_ref[...] = ((own_ref[...].astype(F32) + a_ref[...].astype(F32)) + b_ref[...].astype(F32)) + c_ref[...].astype(F32)

    def slot(k):
        return pl.BlockSpec((None, tr, cols), lambda i, p: (jnp.bitwise_xor(p[0], k), i, 0))

    return pl.pallas_call(
        body, out_shape=jax.ShapeDtypeStruct((2 * half, cols), F32),
        grid_spec=pltpu.PrefetchScalarGridSpec(
            num_scalar_prefetch=1, grid=(nrb,), in_specs=[slot(0), slot(1), slot(2), slot(3)],
            out_specs=pl.BlockSpec((tr, cols), lambda i, p: (p[1] * nrb + i, 0))),
        compiler_params=_cp("parallel"), name=name)(place, own, parts, parts, parts)


def cast_to_slot(w_all, layer, place, name):
    _, rows, cols = w_all.shape
    tr = _row_tile(rows, cols)

    def body(p_ref, w_ref, o_ref):
        o_ref[...] = w_ref[...].astype(o_ref.dtype)

    return pl.pallas_call(
        body, out_shape=jax.ShapeDtypeStruct((N_CHIPS, rows, cols), BF16),
        grid_spec=pltpu.PrefetchScalarGridSpec(
            num_scalar_prefetch=1, grid=(rows // tr,),
            in_specs=[pl.BlockSpec((None, tr, cols), lambda i, p: (layer, i, 0))],
            out_specs=pl.BlockSpec((None, tr, cols), lambda i, p: (p[0], i, 0))),
        compiler_params=_cp("parallel"), name=name)(place, w_all)


ANY = pl.BlockSpec(memory_space=pl.ANY)


def _place():
    x, y, c = lax.axis_index("x"), lax.axis_index("y"), lax.axis_index("c")
    others = [(1 - x, y), (x, 1 - y), (1 - x, 1 - y)]
    return x, y, c, 2 * x + y, others


def _remote(src, dst, send_sem, recv_sem, dev):
    return pltpu.make_async_remote_copy(src_ref=src, dst_ref=dst, send_sem=send_sem, recv_sem=recv_sem,
                                        device_id=dev, device_id_type=MESH)


def gather_weights(bufs, name):
    n = len(bufs)

    def body(*refs):
        ins, outs = refs[:n], refs[n:2 * n]
        send_sems, recv_sems = refs[2 * n:]
        x, y, c, me, others = _place()
        sends = []
        for w in range(n):
            half = ins[w].shape[1] // 2
            mine = pl.ds(c * half, half)
            for k, (ox, oy) in enumerate(others):
                cp = _remote(ins[w].at[me, mine], outs[w].at[me, mine], send_sems.at[6 * w + k], recv_sems.at[6 * w + k], (ox, oy, c))
                cp.start()
                sends.append(cp)
        for w in range(n):
            half = ins[w].shape[1] // 2
            mine = pl.ds(c * half, half)
            for k, (ox, oy) in enumerate(others):
                landed = outs[w].at[2 * ox + oy, mine]
                _remote(landed, landed, send_sems.at[6 * w + k], recv_sems.at[6 * w + k], (ox, oy, c)).wait_recv()
                cp = _remote(landed, landed, send_sems.at[6 * w + 3 + k], recv_sems.at[6 * w + 3 + k], (x, y, 1 - c))
                cp.start()
                sends.append(cp)
        for w in range(n):
            half = ins[w].shape[1] // 2
            theirs = pl.ds((1 - c) * half, half)
            for k, (ox, oy) in enumerate(others):
                passed = outs[w].at[2 * ox + oy, theirs]
                _remote(passed, passed, send_sems.at[6 * w + 3 + k], recv_sems.at[6 * w + 3 + k], (x, y, 1 - c)).wait_recv()
        for cp in sends:
            cp.wait_send()

    return pl.pallas_call(
        body, out_shape=tuple(jax.ShapeDtypeStruct(b.shape, b.dtype) for b in bufs),
        in_specs=[ANY] * n, out_specs=(ANY,) * n, input_output_aliases={w: w for w in range(n)},
        scratch_shapes=[pltpu.SemaphoreType.DMA((6 * n,)), pltpu.SemaphoreType.DMA((6 * n,))],
        name=name)(*bufs)


def swap_halves(grads, name):
    n = len(grads)

    def body(*refs):
        ins, outs = refs[:n], refs[n:2 * n]
        send_sems, recv_sems = refs[2 * n:]
        x, y, c, _, _ = _place()
        copies = []
        for w in range(n):
            half = ins[w].shape[1] // 2
            cp = _remote(ins[w].at[:, pl.ds((1 - c) * half, half)], outs[w], send_sems.at[w], recv_sems.at[w], (x, y, 1 - c))
            cp.start()
            copies.append(cp)
        for cp in copies:
            cp.wait()

    return pl.pallas_call(
        body, out_shape=tuple(jax.ShapeDtypeStruct((N_CHIPS, g.shape[1] // 2, g.shape[2]), g.dtype) for g in grads),
        in_specs=[ANY] * n, out_specs=(ANY,) * n,
        scratch_shapes=[pltpu.SemaphoreType.DMA((n,)), pltpu.SemaphoreType.DMA((n,))], name=name)(*grads)


def exchange_chip_sums(sums, name):
    n = len(sums)

    def body(*refs):
        ins, outs = refs[:n], refs[n:2 * n]
        send_sems, recv_sems = refs[2 * n:]
        x, y, c, me, others = _place()
        copies = []
        for w in range(n):
            for k, (ox, oy) in enumerate(others):
                cp = _remote(ins[w].at[2 * ox + oy], outs[w].at[me], send_sems.at[3 * w + k], recv_sems.at[3 * w + k], (ox, oy, c))
                cp.start()
                copies.append(cp)
        for w in range(n):
            for k, (ox, oy) in enumerate(others):
                landed = outs[w].at[2 * ox + oy]
                _remote(landed, landed, send_sems.at[3 * w + k], recv_sems.at[3 * w + k], (ox, oy, c)).wait_recv()
        for cp in copies:
            cp.wait_send()

    return pl.pallas_call(
        body, out_shape=tuple(jax.ShapeDtypeStruct(s.shape, s.dtype) for s in sums),
        in_specs=[ANY] * n, out_specs=(ANY,) * n,
        scratch_shapes=[pltpu.SemaphoreType.DMA((3 * n,)), pltpu.SemaphoreType.DMA((3 * n,))],
        name=name)(*sums)


def join_halves(grads, name):
    n = len(grads)

    def body(*refs):
        ins, outs = refs[:n], refs[n:2 * n]
        send_sems, recv_sems = refs[2 * n:]
        x, y, c, _, _ = _place()
        copies = []
        for w in range(n):
            half = ins[w].shape[0] // 2
            mine = pl.ds(c * half, half)
            cp = _remote(ins[w].at[mine], outs[w].at[mine], send_sems.at[w], recv_sems.at[w], (x, y, 1 - c))
            cp.start()
            copies.append(cp)
        for w, cp in enumerate(copies):
            half = ins[w].shape[0] // 2
            theirs = outs[w].at[pl.ds((1 - c) * half, half)]
            _remote(theirs, theirs, send_sems.at[w], recv_sems.at[w], (x, y, 1 - c)).wait_recv()
            cp.wait_send()

    return pl.pallas_call(
        body, out_shape=tuple(jax.ShapeDtypeStruct(g.shape, g.dtype) for g in grads),
        in_specs=[ANY] * n, out_specs=(ANY,) * n, input_output_aliases={w: w for w in range(n)},
        scratch_shapes=[pltpu.SemaphoreType.DMA((n,)), pltpu.SemaphoreType.DMA((n,))],
        name=name)(*grads)


VM = pl.BlockSpec(memory_space=pltpu.VMEM)


def small_allgather(buf, name="small_allgather"):
    def body(in_ref, out_ref, send_sems, recv_sems):
        x, y, c, me, others = _place()
        out_ref[me] = in_ref[...]
        copies = []
        for k, (ox, oy) in enumerate(others):
            cp = _remote(in_ref, out_ref.at[me], send_sems.at[k], recv_sems.at[k], (ox, oy, c))
            cp.start()
            copies.append(cp)
        for k, (ox, oy) in enumerate(others):
            landed = out_ref.at[2 * ox + oy]
            _remote(landed, landed, send_sems.at[k], recv_sems.at[k], (ox, oy, c)).wait_recv()
        for cp in copies:
            cp.wait_send()

    return pl.pallas_call(
        body, out_shape=jax.ShapeDtypeStruct((N_CHIPS,) + buf.shape, buf.dtype), in_specs=[VM], out_specs=VM,
        scratch_shapes=[pltpu.SemaphoreType.DMA((3,)), pltpu.SemaphoreType.DMA((3,))],
        compiler_params=pltpu.CompilerParams(vmem_limit_bytes=V7X_VMEM_LIMIT), name=name)(buf)


def small_allreduce(buf, name="small_allreduce"):
    def body(in_ref, out_ref, sib_ref, slot_ref, send_sems, recv_sems):
        x, y, c, me, others = _place()
        cp = _remote(in_ref, sib_ref, send_sems.at[3], recv_sems.at[3], (x, y, 1 - c))
        cp.start()
        cp.wait()
        slot_ref[me] = in_ref[...] + sib_ref[...]
        copies = []
        for k, (ox, oy) in enumerate(others):
            cp = _remote(slot_ref.at[me], slot_ref.at[me], send_sems.at[k], recv_sems.at[k], (ox, oy, c))
            cp.start()
            copies.append(cp)
        for k, (ox, oy) in enumerate(others):
            landed = slot_ref.at[2 * ox + oy]
            _remote(landed, landed, send_sems.at[k], recv_sems.at[k], (ox, oy, c)).wait_recv()
        for cp in copies:
            cp.wait_send()
        out_ref[...] = ((slot_ref[0] + slot_ref[1]) + slot_ref[2]) + slot_ref[3]

    return pl.pallas_call(
        body, out_shape=jax.ShapeDtypeStruct(buf.shape, buf.dtype), in_specs=[VM], out_specs=VM,
        scratch_shapes=[pltpu.VMEM(buf.shape, buf.dtype), pltpu.VMEM((N_CHIPS,) + buf.shape, buf.dtype),
                        pltpu.SemaphoreType.DMA((4,)), pltpu.SemaphoreType.DMA((4,))],
        compiler_params=pltpu.CompilerParams(vmem_limit_bytes=V7X_VMEM_LIMIT), name=name)(buf)


def _pack_rows(arrays, row_multiple):
    flat = jnp.concatenate([a.reshape(-1) for a in arrays])
    rows = -(-flat.shape[0] // (LANES * row_multiple)) * row_multiple
    return jnp.pad(flat, (0, rows * LANES - flat.shape[0])).reshape(rows, LANES)


def _unpack_rows(buf, shapes):
    flat = buf.reshape(-1)
    out, at = [], 0
    for s in shapes:
        n = math.prod(s)
        out.append(flat[at:at + n].reshape(s))
        at += n
    return out


def _layer_weights(i):
    j = i // 2
    mixer = "a" if i % 2 == 0 else "b"
    return [("w_mem_kv", i), (mixer + "_w_in", j), (mixer + "_w_out", j), ("ffn_w_up", i), ("ffn_w_down", i)]


def _layer_fwd(i, x, mem, w, small):
    is_a = i % 2 == 0
    j = i // 2
    wkv, win, wout, wup, wdn = w
    wkv = wkv.reshape(1, D_MODEL, 2 * MEM_W)
    wdn = wdn.reshape(1, FF, D_MODEL)
    h1 = rms_fwd(x, small["mix_norm_g"][i], BF16, name=f"mix_norm{i}")
    mem_n = rms_fwd(mem, small["mem_norm_g"][i], BF16, name=f"mem_norm{i}")
    kv = mm_nn(mem_n, wkv, F32, name=f"mem_kv{i}")
    proj = mm_nn(h1, win, F32, name=f"in_proj{i}")
    saved = dict(x0=x, h1=h1, mem_n=mem_n, kv=kv, proj=proj)
    if is_a:
        outs, lses = zip(*[attn_fwd(proj, g, name=f"attn_fwd{i}_{g}") for g in range(3)])
        comb, lse = attn_combine(outs, lses, name=f"attn_combine{i}")
        mem_out = mem_fwd(proj, 3 * A_QKV_W, kv, name=f"mem_fwd{i}")
        cat = jnp.concatenate([comb.astype(BF16), mem_out], axis=1)
        saved.update(comb=comb, lse=lse)
    else:
        wout = wout.reshape(1, B_W + MEM_W, D_MODEL)
        tok = sgu_fwd(proj, small["b_v_norm_g"][j], small["b_w_s"][j], small["bias_b"][j], name=f"sgu_fwd{i}")
        mem_out = mem_fwd(proj, 2 * B_W, kv, name=f"mem_fwd{i}")
        cat = jnp.concatenate([tok, mem_out], axis=1)
    x1 = mm_nn(cat, wout, F32, res=x, name=f"out_proj{i}")
    h2 = rms_fwd(x1, small["ffn_norm_g"][i], BF16, name=f"ffn_norm{i}")
    a = mm_nn(h2, wup, F32, name=f"ffn_up{i}")
    act = ffn_act_fwd(a, small["ffn_conv_w"][i], small["ffn_conv_b"][i], name=f"ffn_act{i}")
    x2 = mm_nn(act, wdn, F32, res=x1, name=f"ffn_down{i}")
    saved.update(cat=cat, x1=x1, h2=h2, a=a, act=act)
    return x2, saved


def _layer_bwd(i, dx2, mem, w, small, sv):
    is_a = i % 2 == 0
    j = i // 2
    wkv, win, wout, wup, wdn = w
    wkv = wkv.reshape(1, D_MODEL, 2 * MEM_W)
    wdn = wdn.reshape(1, FF, D_MODEL)
    sg = {}
    dact = mm_nt(dx2, wdn, F32, name=f"d_act{i}")
    d_wdn = mm_tn(sv["act"], dx2, 1, BF16, name=f"d_wdown{i}").reshape(N_CHIPS, FF // N_CHIPS, D_MODEL)
    da, sg["ffn_conv_w"], sg["ffn_conv_b"] = ffn_act_bwd(sv["a"], small["ffn_conv_w"][i], small["ffn_conv_b"][i], dact,
                                                          name=f"ffn_act_bwd{i}")
    d_wup = mm_tn(sv["h2"], da, N_CHIPS, BF16, name=f"d_wup{i}")
    dh2 = mm_nt(da, wup, F32, name=f"d_h2_{i}")
    dx1, sg["ffn_norm_g"] = rms_bwd(sv["x1"], small["ffn_norm_g"][i], dh2, dres=dx2, name=f"ffn_norm_bwd{i}")
    proj, kv = sv["proj"], sv["kv"]
    if is_a:
        dcat = mm_nt(dx1, wout, F32, name=f"d_cat{i}")
        d_wout = mm_tn(sv["cat"], dx1, N_CHIPS, BF16, name=f"d_wout{i}")
        dqm, dkv = mem_bwd(proj, 3 * A_QKV_W, kv, dcat, A_OUT_W, name=f"mem_bwd{i}")
        parts = [attn_bwd(proj, dcat, sv["comb"], sv["lse"], g, name=f"attn_bwd{i}_{g}") for g in range(3)]
        dproj = jnp.concatenate([parts[g][p] for p in range(3) for g in range(3)] + [dqm], axis=1)
    else:
        dcat = mm_nt(dx1, wout.reshape(1, B_W + MEM_W, D_MODEL), F32, name=f"d_cat{i}")
        d_wout = mm_tn(sv["cat"], dx1, 1, BF16, name=f"d_wout{i}").reshape(N_CHIPS, (B_W + MEM_W) // N_CHIPS, D_MODEL)
        dqm, dkv = mem_bwd(proj, 2 * B_W, kv, dcat, B_W, name=f"mem_bwd{i}")
        w_s = small["b_w_s"][j]
        duv, sg["b_w_s"], dmix, sg["b_v_norm_g"] = sgu_bwd(proj, small["b_v_norm_g"][j], w_s, jnp.swapaxes(w_s, 1, 2),
                                                           small["bias_b"][j], dcat, name=f"sgu_bwd{i}")
        sg["b_s_bias"] = jnp.sum(dmix, axis=-1)
        dproj = jnp.concatenate([duv, dqm], axis=1)
    d_wkv = mm_tn(sv["mem_n"], dkv, 1, BF16, name=f"d_wkv{i}").reshape(N_CHIPS, D_MODEL // N_CHIPS, 2 * MEM_W)
    dmem_n = mm_nt(dkv, wkv, F32, name=f"d_mem_n{i}")
    _, sg["mem_norm_g"] = rms_bwd(mem, small["mem_norm_g"][i], dmem_n, name=f"mem_norm_bwd{i}")
    d_win = mm_tn(sv["h1"], dproj, N_CHIPS, BF16, name=f"d_win{i}")
    dh1 = mm_nt(dproj, win, F32, name=f"d_h1_{i}")
    dx0, sg["mix_norm_g"] = rms_bwd(sv["x0"], small["mix_norm_g"][i], dh1, dres=dx1, name=f"mix_norm_bwd{i}")
    return dx0, [d_wkv, d_win, d_wout, d_wup, d_wdn], sg


def _reduce_scatter(grads, place, tag):
    got = swap_halves(grads, name=f"swap_halves{tag}")
    sums = [pair_sum(g, o, place[1:], name=f"pair_sum{tag}_{k}") for k, (g, o) in enumerate(zip(grads, got))]
    parts = exchange_chip_sums(sums, name=f"exchange{tag}")
    halves = [chip_sum(s, p, place, name=f"chip_sum{tag}_{k}") for k, (s, p) in enumerate(zip(sums, parts))]
    return join_halves(halves, name=f"join_halves{tag}")


SMALL_SHARDED = ("b_v_norm_g", "ffn_conv_w")
SMALL_FULL_SHAPES = dict(mix_norm_g=(D_MODEL,), ffn_norm_g=(D_MODEL,), mem_norm_g=(D_MODEL,), b_v_norm_g=(B_W,),
                         b_w_s=(B_GROUPS, CHUNK, CHUNK), b_s_bias=(B_GROUPS, CHUNK), ffn_conv_w=(3, 2 * FF),
                         ffn_conv_b=(2 * FF,))
BIG = ("w_mem_kv", "a_w_in", "a_w_out", "b_w_in", "b_w_out", "ffn_w_up", "ffn_w_down")
WEIGHT_ORDER = ("mix_norm_g", "ffn_norm_g", "mem_norm_g", "w_mem_kv", "a_w_in", "a_w_out", "b_w_in", "b_v_norm_g", "b_w_s",
                "b_s_bias", "b_w_out", "ffn_w_up", "ffn_conv_w", "ffn_conv_b", "ffn_w_down", "final_norm_g")


def kernel(x, mem, mix_norm_g, ffn_norm_g, mem_norm_g, w_mem_kv, a_w_in, a_w_out, b_w_in, b_v_norm_g, b_w_s, b_s_bias, b_w_out, ffn_w_up, ffn_conv_w, ffn_conv_b, ffn_w_down, final_norm_g, loss_target, m_mix_norm_g, m_ffn_norm_g, m_mem_norm_g, m_w_mem_kv, m_a_w_in, m_a_w_out, m_b_w_in, m_b_v_norm_g, m_b_w_s, m_b_s_bias, m_b_w_out, m_ffn_w_up, m_ffn_conv_w, m_ffn_conv_b, m_ffn_w_down, m_final_norm_g, v_mix_norm_g, v_ffn_norm_g, v_mem_norm_g, v_w_mem_kv, v_a_w_in, v_a_w_out, v_b_w_in, v_b_v_norm_g, v_b_w_s, v_b_s_bias, v_b_w_out, v_ffn_w_up, v_ffn_conv_w, v_ffn_conv_b, v_ffn_w_down, v_final_norm_g):
    weights = dict(mix_norm_g=mix_norm_g, ffn_norm_g=ffn_norm_g, mem_norm_g=mem_norm_g, w_mem_kv=w_mem_kv, a_w_in=a_w_in,
                   a_w_out=a_w_out, b_w_in=b_w_in, b_v_norm_g=b_v_norm_g, b_w_s=b_w_s, b_s_bias=b_s_bias, b_w_out=b_w_out,
                   ffn_w_up=ffn_w_up, ffn_conv_w=ffn_conv_w, ffn_conv_b=ffn_conv_b, ffn_w_down=ffn_w_down,
                   final_norm_g=final_norm_g)
    mom1 = dict(mix_norm_g=m_mix_norm_g, ffn_norm_g=m_ffn_norm_g, mem_norm_g=m_mem_norm_g, w_mem_kv=m_w_mem_kv,
                a_w_in=m_a_w_in, a_w_out=m_a_w_out, b_w_in=m_b_w_in, b_v_norm_g=m_b_v_norm_g, b_w_s=m_b_w_s,
                b_s_bias=m_b_s_bias, b_w_out=m_b_w_out, ffn_w_up=m_ffn_w_up, ffn_conv_w=m_ffn_conv_w,
                ffn_conv_b=m_ffn_conv_b, ffn_w_down=m_ffn_w_down, final_norm_g=m_final_norm_g)
    mom2 = dict(mix_norm_g=v_mix_norm_g, ffn_norm_g=v_ffn_norm_g, mem_norm_g=v_mem_norm_g, w_mem_kv=v_w_mem_kv,
                a_w_in=v_a_w_in, a_w_out=v_a_w_out, b_w_in=v_b_w_in, b_v_norm_g=v_b_v_norm_g, b_w_s=v_b_w_s,
                b_s_bias=v_b_s_bias, b_w_out=v_b_w_out, ffn_w_up=v_ffn_w_up, ffn_conv_w=v_ffn_conv_w,
                ffn_conv_b=v_ffn_conv_b, ffn_w_down=v_ffn_w_down, final_norm_g=v_final_norm_g)
    chip = 2 * lax.axis_index("x") + lax.axis_index("y")
    place = jnp.stack([chip, lax.axis_index("c")]).astype(jnp.int32)
    x0, mem0, target = x[0], mem[0], loss_target[0]
    depth = DEPTH

    n_cw, n_vg = ffn_conv_w.size, b_v_norm_g.size
    gathered = small_allgather(_pack_rows([ffn_conv_w, b_v_norm_g], 8)).reshape(N_CHIPS, -1)
    conv_w_full = gathered[:, :n_cw].reshape(N_CHIPS, DEPTH, 3, 2 * FF // N_CHIPS).transpose(1, 2, 0, 3).reshape(DEPTH, 3, 2 * FF)
    vgain_full = gathered[:, n_cw:n_cw + n_vg].reshape(N_CHIPS, 2, B_W // N_CHIPS).transpose(1, 0, 2).reshape(2, B_W)
    small = dict(mix_norm_g=mix_norm_g, ffn_norm_g=ffn_norm_g, mem_norm_g=mem_norm_g, b_w_s=b_w_s, ffn_conv_b=ffn_conv_b,
                 ffn_conv_w=conv_w_full, b_v_norm_g=vgain_full,
                 bias_b=jnp.broadcast_to(b_s_bias[..., None], b_s_bias.shape + (CHUNK,)))

    gathered_w, saved = [], []
    h = x0
    for i in range(depth):
        bufs = [cast_to_slot(weights[n], l, place, name=f"cast_{n}{l}") for n, l in _layer_weights(i)]
        gathered_w.append(gather_weights(bufs, name=f"gather_weights{i}"))
        h, sv = _layer_fwd(i, h, mem0, gathered_w[i], small)
        saved.append(sv)
    loss_row, dh, d_final = final_loss(h, final_norm_g, target)
    loss = lax.psum(loss_row[0, 0], ("x", "y", "c"))

    names = [n for n in WEIGHT_ORDER if n not in BIG]
    small_g = {n: [None] * weights[n].shape[0] for n in names if n != "final_norm_g"}
    big_out = {n: None for n in BIG}
    for i in reversed(range(depth)):
        dh, big_g, sg = _layer_bwd(i, dh, mem0, gathered_w[i], small, saved[i])
        for n, g in sg.items():
            small_g[n][i if len(small_g[n]) == depth else i // 2] = g.reshape(SMALL_FULL_SHAPES[n])
        reduced = _reduce_scatter(big_g, place, i)
        for (n, l), g in zip(_layer_weights(i), reduced):
            big_out[n] = adamw_layer(weights[n], mom1[n], mom2[n], l, g, big_out[n], name=f"adamw_{n}{l}")

    full_g = {n: (d_final.reshape(-1) if n == "final_norm_g" else jnp.stack(small_g[n])) for n in names}
    shapes = [full_g[n].shape for n in names]
    summed = dict(zip(names, _unpack_rows(small_allreduce(_pack_rows([full_g[n] for n in names], 8)), shapes)))
    for n in SMALL_SHARDED:
        width = weights[n].shape[-1]
        summed[n] = lax.dynamic_slice_in_dim(summed[n], chip * width, width, axis=summed[n].ndim - 1)
    own_shapes = [weights[n].shape for n in names]
    pack = lambda d: _pack_rows([d[n] for n in names], 128)
    small_out = [_unpack_rows(b, own_shapes) for b in adamw_flat(pack(weights), pack(summed), pack(mom1), pack(mom2))]
    outs = {}
    for k, n in enumerate(names):
        outs[n] = (summed[n], small_out[0][k], small_out[1][k], small_out[2][k])
    outs.update(big_out)
    return (loss, dh[None], *[outs[n][0] for n in WEIGHT_ORDER], *[outs[n][1] for n in WEIGHT_ORDER],
            *[outs[n][2] for n in WEIGHT_ORDER], *[outs[n][3] for n in WEIGHT_ORDER])
```

```python
import functools
import math

import numpy as np
import jax
import jax.numpy as jnp
from jax import lax
from jax.experimental import pallas as pl
from jax.experimental.pallas import tpu as pltpu

F32 = jnp.float32
BF16 = jnp.bfloat16
MESH = pl.DeviceIdType.MESH

D_MODEL = 2048
SEQ = 2048
DEPTH = 4
EPS = 1e-6
NEG = -1e30
HEAD = 128
A_PATTERNS = ((128, 1), (512, 4), (2048, 16))
A_QKV_W = 1536
A_OUT_W = 512
A_IN = 5120
QBLK = 128
N_SIDE = 64
CHUNK = 128
B_GROUPS = 12
B_W = 1536
B_IN = 3584
MEM_LEN = 256
MEM_HEADS = 4
MEM_W = 512
FF = 5632
ADAM_LR, ADAM_B1, ADAM_B2, ADAM_EPS, ADAM_WD, ADAM_STEP = 0.001, 0.9, 0.999, 1e-08, 0.01, 10
N_CHIPS = 4

LANES = 128
V7X_VMEM_LIMIT = 56 * 1024 * 1024


def _cp(*sem):
    return pltpu.CompilerParams(dimension_semantics=sem, vmem_limit_bytes=V7X_VMEM_LIMIT)


def _pick(dim, prefs):
    for p in prefs:
        if dim % p == 0:
            return p
    raise ValueError(f"no tile for {dim} in {prefs}")


def _gelu_parts(x):
    cdf = 0.5 * (1.0 + lax.erf(x * (1.0 / math.sqrt(2.0))))
    pdf = jnp.exp(-0.5 * x * x) * (1.0 / math.sqrt(2.0 * math.pi))
    return x * cdf, cdf + x * pdf


def _gelu(x):
    return 0.5 * x * (1.0 + lax.erf(x * (1.0 / math.sqrt(2.0))))


TM_PREFS = (1024, 512, 256, 128)
TN_PREFS = (1408, 1280, 1024, 896, 512, 256, 128)
TK_PREFS = (2048, 1408, 1280, 1024, 896, 512, 256, 128)


def _mm_body(nk, dims, has_res):
    def body(*refs):
        if has_res:
            a_ref, b_ref, r_ref, o_ref = refs[:4]
        else:
            a_ref, b_ref, o_ref = refs[:3]
            r_ref = None
        part = lax.dot_general(a_ref[...].astype(BF16), b_ref[...].astype(BF16), dims,
                               preferred_element_type=F32)
        if nk == 1:
            if has_res:
                part = part + r_ref[...]
            o_ref[...] = part.astype(o_ref.dtype)
            return
        acc_ref = refs[-1]
        k = pl.program_id(2)

        @pl.when(k == 0)
        def _():
            acc_ref[...] = part

        @pl.when(k > 0)
        def _():
            acc_ref[...] += part

        @pl.when(k == nk - 1)
        def _():
            tot = acc_ref[...]
            if has_res:
                tot = tot + r_ref[...]
            o_ref[...] = tot.astype(o_ref.dtype)
    return body


def mm_nn(a, w, out_dtype, res=None, name="mm_nn"):
    m, kw = a.shape
    ns_, kw2, nsz = w.shape
    assert kw == kw2
    n = ns_ * nsz
    tm, tn, tk = _pick(m, TM_PREFS), _pick(nsz, TN_PREFS), _pick(kw, TK_PREFS)
    nb, nk = nsz // tn, kw // tk
    in_specs = [pl.BlockSpec((tm, tk), lambda i, j, k: (i, k)),
                pl.BlockSpec((None, tk, tn), lambda i, j, k: (j // nb, k, j % nb))]
    args = [a, w]
    if res is not None:
        in_specs.append(pl.BlockSpec((tm, tn), lambda i, j, k: (i, j)))
        args.append(res)
    return pl.pallas_call(
        _mm_body(nk, (((1,), (0,)), ((), ())), res is not None),
        out_shape=jax.ShapeDtypeStruct((m, n), out_dtype),
        grid=(m // tm, n // tn, nk), in_specs=in_specs,
        out_specs=pl.BlockSpec((tm, tn), lambda i, j, k: (i, j)),
        scratch_shapes=[pltpu.VMEM((tm, tn), F32)] if nk > 1 else [],
        compiler_params=_cp("parallel", "parallel", "arbitrary"), name=name)(*args)


def mm_nt(g, w, out_dtype, name="mm_nt", after=None):
    m, n = g.shape
    ns_, kw, nsz = w.shape
    assert n == ns_ * nsz
    tm, tn, tk = _pick(m, TM_PREFS), _pick(kw, TN_PREFS), _pick(nsz, TK_PREFS)
    nb, nk = nsz // tk, n // tk
    body = _mm_body(nk, (((1,), (1,)), ((), ())), False)
    tied = [] if after is None else [after]
    return pl.pallas_call(
        (lambda g_ref, w_ref, *rest: body(g_ref, w_ref, *rest[len(tied):])),
        out_shape=jax.ShapeDtypeStruct((m, kw), out_dtype),
        grid=(m // tm, kw // tn, nk),
        in_specs=[pl.BlockSpec((tm, tk), lambda i, j, k: (i, k)),
                  pl.BlockSpec((None, tn, tk), lambda i, j, k: (k // nb, j, k % nb))] + [ANY] * len(tied),
        out_specs=pl.BlockSpec((tm, tn), lambda i, j, k: (i, j)),
        scratch_shapes=[pltpu.VMEM((tm, tn), F32)] if nk > 1 else [],
        compiler_params=_cp("parallel", "parallel", "arbitrary"), name=name)(g, w, *tied)


def mm_tn(a, g, n_shards, out_dtype, name="mm_tn"):
    t, kw = a.shape
    t2, n = g.shape
    assert t == t2
    nsz = n // n_shards
    tm, tn, tk = _pick(kw, TM_PREFS), _pick(nsz, TN_PREFS), _pick(t, TK_PREFS)
    nb, nk = nsz // tn, t // tk
    return pl.pallas_call(
        _mm_body(nk, (((0,), (0,)), ((), ())), False),
        out_shape=jax.ShapeDtypeStruct((n_shards, kw, nsz), out_dtype),
        grid=(kw // tm, n // tn, nk),
        in_specs=[pl.BlockSpec((tk, tm), lambda i, j, k: (k, i)),
                  pl.BlockSpec((tk, tn), lambda i, j, k: (k, j))],
        out_specs=pl.BlockSpec((None, tm, tn), lambda i, j, k: (j // nb, i, j % nb)),
        scratch_shapes=[pltpu.VMEM((tm, tn), F32)] if nk > 1 else [],
        compiler_params=_cp("parallel", "parallel", "arbitrary"), name=name)(a, g)


ROW_TILE = 256


def _rms_stats(x):
    r = lax.rsqrt(jnp.mean(x * x, axis=-1, keepdims=True) + EPS)
    return r, x * r


def _rms_back(xh, r, g, dh):
    u = dh * g
    return r * (u - xh * jnp.mean(u * xh, axis=-1, keepdims=True))


def rms_fwd(x, g, out_dtype, name="rms_fwd", after=None):
    rows, d = x.shape
    tr = _pick(rows, (ROW_TILE, 128))
    tied = [] if after is None else [after]

    def body(x_ref, g_ref, *rest):
        o_ref = rest[-1]
        _, xh = _rms_stats(x_ref[...])
        o_ref[...] = (xh * g_ref[...]).astype(o_ref.dtype)

    return pl.pallas_call(
        body, out_shape=jax.ShapeDtypeStruct((rows, d), out_dtype), grid=(rows // tr,),
        in_specs=[pl.BlockSpec((tr, d), lambda i: (i, 0)), pl.BlockSpec((1, d), lambda i: (0, 0))] + [ANY] * len(tied),
        out_specs=pl.BlockSpec((tr, d), lambda i: (i, 0)),
        compiler_params=_cp("parallel"), name=name)(x, g.reshape(1, d), *tied)


def rms_bwd(x, g, dh, dres=None, name="rms_bwd"):
    rows, d = x.shape
    tr = _pick(rows, (ROW_TILE, 128))
    has_res = dres is not None

    def body(*refs):
        if has_res:
            x_ref, g_ref, dh_ref, dres_ref, dx_ref, dg_ref = refs
        else:
            x_ref, g_ref, dh_ref, dx_ref, dg_ref = refs
        r, xh = _rms_stats(x_ref[...])
        dh_ = dh_ref[...].astype(F32)
        part = jnp.sum(dh_ * xh, axis=0, keepdims=True)

        @pl.when(pl.program_id(0) == 0)
        def _():
            dg_ref[...] = part

        @pl.when(pl.program_id(0) > 0)
        def _():
            dg_ref[...] += part

        dx = _rms_back(xh, r, g_ref[...], dh_)
        if has_res:
            dx = dx + dres_ref[...]
        dx_ref[...] = dx

    row_spec = pl.BlockSpec((tr, d), lambda i: (i, 0))
    vec_spec = pl.BlockSpec((1, d), lambda i: (0, 0))
    args = [x, g.reshape(1, d), dh] + ([dres] if has_res else [])
    return pl.pallas_call(
        body, out_shape=(jax.ShapeDtypeStruct((rows, d), F32), jax.ShapeDtypeStruct((1, d), F32)),
        grid=(rows // tr,), in_specs=[row_spec, vec_spec, row_spec] + ([row_spec] if has_res else []),
        out_specs=(row_spec, vec_spec), compiler_params=_cp("arbitrary"), name=name)(*args)


def final_loss(x, g, target, name="final_loss"):
    rows, d = x.shape
    tr = _pick(rows, (ROW_TILE, 128))

    def body(x_ref, g_ref, t_ref, loss_ref, dx_ref, dg_ref):
        r, xh = _rms_stats(x_ref[...])
        gain = g_ref[...]
        err = xh * gain - t_ref[...]
        sq = jnp.sum(jnp.sum(err * err, axis=1, keepdims=True), axis=0, keepdims=True) * (0.5 / d)
        dy = err * (1.0 / d)
        part = jnp.sum(dy * xh, axis=0, keepdims=True)

        @pl.when(pl.program_id(0) == 0)
        def _():
            dg_ref[...] = part
            loss_ref[...] = jnp.broadcast_to(sq, loss_ref.shape)

        @pl.when(pl.program_id(0) > 0)
        def _():
            dg_ref[...] += part
            loss_ref[...] += jnp.broadcast_to(sq, loss_ref.shape)

        dx_ref[...] = _rms_back(xh, r, gain, dy)

    row_spec = pl.BlockSpec((tr, d), lambda i: (i, 0))
    vec_spec = pl.BlockSpec((1, d), lambda i: (0, 0))
    return pl.pallas_call(
        body, out_shape=(jax.ShapeDtypeStruct((1, LANES), F32), jax.ShapeDtypeStruct((rows, d), F32),
                         jax.ShapeDtypeStruct((1, d), F32)),
        grid=(rows // tr,), in_specs=[row_spec, vec_spec, row_spec],
        out_specs=(pl.BlockSpec((1, LANES), lambda i: (0, 0)), row_spec, vec_spec),
        compiler_params=_cp("arbitrary"), name=name)(x, g.reshape(1, d), target)


def _alibi_slopes():
    return (2.0 ** (-8.0 * (np.arange(12) + 1) / 12)).astype(np.float32)


def _band_scores(q, k, q0, start, wk, slope):
    s = lax.dot_general(q, k, (((1,), (1,)), ((), ())), preferred_element_type=F32) * (HEAD ** -0.5)
    qpos = q0 + lax.broadcasted_iota(jnp.int32, (QBLK, wk), 0)
    kpos = start + lax.broadcasted_iota(jnp.int32, (QBLK, wk), 1)
    rel = jnp.abs(qpos - kpos)
    return jnp.where(rel <= N_SIDE, s - slope * rel.astype(F32), NEG)


def _attn_geometry(seq, dilation):
    length = seq // dilation
    return length, length // QBLK, min(2 * QBLK, length)


def _attn_window(n, length, wk):
    q0 = pl.multiple_of(n * QBLK, QBLK)
    start = pl.multiple_of(jnp.clip(n * QBLK - N_SIDE, 0, length - wk), N_SIDE)
    return q0, start


def attn_fwd(proj, group, name):
    seq = proj.shape[0]
    dilation = A_PATTERNS[group][1]
    length, nblk, wk = _attn_geometry(seq, dilation)
    cols = A_IN // HEAD
    pv = proj.reshape(length, dilation * A_IN)

    def body(slope_ref, q_ref, k_ref, v_ref, o_ref, lse_ref):
        slope = slope_ref[group * 4 + pl.program_id(1)] * float(dilation)

        def blk(n, carry):
            q0, start = _attn_window(n, length, wk)
            q = q_ref[pl.ds(q0, QBLK), :].astype(BF16)
            k = k_ref[pl.ds(start, wk), :].astype(BF16)
            v = v_ref[pl.ds(start, wk), :].astype(BF16)
            s = _band_scores(q, k, q0, start, wk, slope)
            m = jnp.max(s, axis=-1, keepdims=True)
            p = jnp.exp(s - m)
            l = jnp.sum(p, axis=-1, keepdims=True)
            o = jnp.dot(p.astype(BF16), v, preferred_element_type=F32) / l
            o_ref[pl.ds(q0, QBLK), :] = o
            lse_ref[pl.ds(q0, QBLK), :] = jnp.broadcast_to(m + jnp.log(l), (QBLK, HEAD))
            return carry

        lax.fori_loop(0, nblk, blk, 0)

    def part(p):
        return pl.BlockSpec((length, HEAD), lambda r, h: (0, r * cols + p * 12 + group * 4 + h))

    out_spec = pl.BlockSpec((length, HEAD), lambda r, h: (0, r * 4 + h))
    o, lse = pl.pallas_call(
        body, out_shape=(jax.ShapeDtypeStruct((length, dilation * A_OUT_W), F32),) * 2,
        grid=(dilation, 4),
        in_specs=[pl.BlockSpec(memory_space=pltpu.SMEM), part(0), part(1), part(2)],
        out_specs=(out_spec, out_spec), compiler_params=_cp("parallel", "parallel"), name=name,
    )(jnp.asarray(_alibi_slopes()), pv, pv, pv)
    return o.reshape(seq, A_OUT_W), lse.reshape(seq, A_OUT_W)


def attn_combine(os_, lses, name="attn_combine"):
    seq = os_[0].shape[0]
    tr = ROW_TILE

    def body(o0, o1, o2, l0, l1, l2, c_ref, lse_ref):
        a, b, c = l0[...], l1[...], l2[...]
        m = jnp.maximum(jnp.maximum(a, b), c)
        ea, eb, ec = jnp.exp(a - m), jnp.exp(b - m), jnp.exp(c - m)
        den = ea + eb + ec
        c_ref[...] = (ea * o0[...] + eb * o1[...] + ec * o2[...]) / den
        lse_ref[...] = m + jnp.log(den)

    spec = pl.BlockSpec((tr, A_OUT_W), lambda i: (i, 0))
    return pl.pallas_call(
        body, out_shape=(jax.ShapeDtypeStruct((seq, A_OUT_W), F32),) * 2, grid=(seq // tr,),
        in_specs=[spec] * 6, out_specs=(spec, spec), compiler_params=_cp("parallel"), name=name)(*os_, *lses)


def attn_bwd(proj, dcat, comb, lse, group, name):
    seq = proj.shape[0]
    dilation = A_PATTERNS[group][1]
    length, nblk, wk = _attn_geometry(seq, dilation)
    cols = A_IN // HEAD
    pv = proj.reshape(length, dilation * A_IN)
    view = lambda a: a.reshape(length, dilation * A_OUT_W)
    scale = HEAD ** -0.5

    def body(slope_ref, q_ref, k_ref, v_ref, do_ref, c_ref, lse_ref, dq_ref, dk_ref, dv_ref, dk_acc, dv_acc):
        slope = slope_ref[group * 4 + pl.program_id(1)] * float(dilation)
        dk_acc[...] = jnp.zeros_like(dk_acc)
        dv_acc[...] = jnp.zeros_like(dv_acc)

        def blk(n, carry):
            q0, start = _attn_window(n, length, wk)
            rows = pl.ds(q0, QBLK)
            keys = pl.ds(start, wk)
            q = q_ref[rows, :].astype(BF16)
            k = k_ref[keys, :].astype(BF16)
            v = v_ref[keys, :].astype(BF16)
            do = do_ref[rows, :]
            s = _band_scores(q, k, q0, start, wk, slope)
            p = jnp.exp(s - lse_ref[rows, :][:, :1])
            delta = jnp.sum(do * c_ref[rows, :], axis=-1, keepdims=True)
            do16 = do.astype(BF16)
            dp = lax.dot_general(do16, v, (((1,), (1,)), ((), ())), preferred_element_type=F32)
            ds = (p * (dp - delta) * scale).astype(BF16)
            p16 = p.astype(BF16)
            dq_ref[rows, :] = jnp.dot(ds, k, preferred_element_type=F32).astype(dq_ref.dtype)
            dk_acc[keys, :] += lax.dot_general(ds, q, (((0,), (0,)), ((), ())), preferred_element_type=F32)
            dv_acc[keys, :] += lax.dot_general(p16, do16, (((0,), (0,)), ((), ())), preferred_element_type=F32)
            return carry

        lax.fori_loop(0, nblk, blk, 0)
        dk_ref[...] = dk_acc[...].astype(dk_ref.dtype)
        dv_ref[...] = dv_acc[...].astype(dv_ref.dtype)

    def part(p):
        return pl.BlockSpec((length, HEAD), lambda r, h: (0, r * cols + p * 12 + group * 4 + h))

    hs = pl.BlockSpec((length, HEAD), lambda r, h: (0, r * 4 + h))
    do_cols = dcat.shape[1] // HEAD
    do_spec = pl.BlockSpec((length, HEAD), lambda r, h: (0, r * do_cols + h))
    outs = pl.pallas_call(
        body, out_shape=(jax.ShapeDtypeStruct((length, dilation * A_OUT_W), BF16),) * 3,
        grid=(dilation, 4),
        in_specs=[pl.BlockSpec(memory_space=pltpu.SMEM), part(0), part(1), part(2), do_spec, hs, hs],
        out_specs=(hs, hs, hs),
        scratch_shapes=[pltpu.VMEM((length, HEAD), F32), pltpu.VMEM((length, HEAD), F32)],
        compiler_params=_cp("parallel", "parallel"), name=name,
    )(jnp.asarray(_alibi_slopes()), pv, pv, pv, dcat.reshape(length, dilation * dcat.shape[1]), view(comb), view(lse))
    return tuple(o.reshape(seq, A_OUT_W) for o in outs)


MEM_ROW_TILE = 512


def _mem_probs(q, k):
    s = lax.dot_general(q, k, (((1,), (1,)), ((), ())), preferred_element_type=F32) * (HEAD ** -0.5)
    p = jnp.exp(s - jnp.max(s, axis=-1, keepdims=True))
    return p / jnp.sum(p, axis=-1, keepdims=True)


def mem_fwd(proj, q_col, kv, name="mem_fwd"):
    seq = proj.shape[0]
    qb = q_col // HEAD

    def body(q_ref, k_ref, v_ref, o_ref):
        p = _mem_probs(q_ref[...].astype(BF16), k_ref[...].astype(BF16))
        o_ref[...] = jnp.dot(p.astype(BF16), v_ref[...].astype(BF16), preferred_element_type=F32).astype(o_ref.dtype)

    return pl.pallas_call(
        body, out_shape=jax.ShapeDtypeStruct((seq, MEM_W), BF16), grid=(MEM_HEADS, seq // MEM_ROW_TILE),
        in_specs=[pl.BlockSpec((MEM_ROW_TILE, HEAD), lambda h, i: (i, qb + h)),
                  pl.BlockSpec((MEM_LEN, HEAD), lambda h, i: (0, h)),
                  pl.BlockSpec((MEM_LEN, HEAD), lambda h, i: (0, MEM_HEADS + h))],
        out_specs=pl.BlockSpec((MEM_ROW_TILE, HEAD), lambda h, i: (i, h)),
        compiler_params=_cp("parallel", "parallel"), name=name)(proj, kv, kv)


def mem_bwd(proj, q_col, kv, dcat, do_col, name="mem_bwd"):
    seq = proj.shape[0]
    qb, ob = q_col // HEAD, do_col // HEAD
    scale = HEAD ** -0.5

    def body(q_ref, k_ref, v_ref, do_ref, dq_ref, dk_ref, dv_ref):
        q = q_ref[...].astype(BF16)
        k = k_ref[...].astype(BF16)
        v = v_ref[...].astype(BF16)
        do = do_ref[...].astype(BF16)
        p = _mem_probs(q, k)
        dp = lax.dot_general(do, v, (((1,), (1,)), ((), ())), preferred_element_type=F32)
        ds = (p * (dp - jnp.sum(dp * p, axis=-1, keepdims=True)) * scale).astype(BF16)
        dq_ref[...] = jnp.dot(ds, k, preferred_element_type=F32).astype(dq_ref.dtype)
        dk = lax.dot_general(ds, q, (((0,), (0,)), ((), ())), preferred_element_type=F32)
        dv = lax.dot_general(p.astype(BF16), do, (((0,), (0,)), ((), ())), preferred_element_type=F32)

        @pl.when(pl.program_id(1) == 0)
        def _():
            dk_ref[...] = dk
            dv_ref[...] = dv

        @pl.when(pl.program_id(1) > 0)
        def _():
            dk_ref[...] += dk
            dv_ref[...] += dv

    dq, dk, dv = pl.pallas_call(
        body, out_shape=(jax.ShapeDtypeStruct((seq, MEM_W), BF16), jax.ShapeDtypeStruct((MEM_LEN, MEM_W), F32),
                         jax.ShapeDtypeStruct((MEM_LEN, MEM_W), F32)),
        grid=(MEM_HEADS, seq // MEM_ROW_TILE),
        in_specs=[pl.BlockSpec((MEM_ROW_TILE, HEAD), lambda h, i: (i, qb + h)),
                  pl.BlockSpec((MEM_LEN, HEAD), lambda h, i: (0, h)),
                  pl.BlockSpec((MEM_LEN, HEAD), lambda h, i: (0, MEM_HEADS + h)),
                  pl.BlockSpec((MEM_ROW_TILE, HEAD), lambda h, i: (i, ob + h))],
        out_specs=(pl.BlockSpec((MEM_ROW_TILE, HEAD), lambda h, i: (i, h)),
                   pl.BlockSpec((MEM_LEN, HEAD), lambda h, i: (0, h)),
                   pl.BlockSpec((MEM_LEN, HEAD), lambda h, i: (0, h))),
        compiler_params=_cp("parallel", "arbitrary"), name=name)(proj, kv, kv, dcat)
    return dq, jnp.concatenate([dk, dv], axis=1)


def _sgu_front(x, gain):
    uv, duv = _gelu_parts(x)
    u, v = uv[:, :B_W], uv[:, B_W:]
    r, vh = _rms_stats(v)
    return u, duv, r, vh, vh * gain


def sgu_fwd(proj, gain, w_s, bias_b, name="sgu_fwd"):
    seq = proj.shape[0]

    def body(x_ref, gain_ref, ws_ref, bias_ref, o_ref):
        u, _, _, _, vn = _sgu_front(x_ref[...], gain_ref[...])
        for g in range(B_GROUPS):
            cs = slice(g * CHUNK, (g + 1) * CHUNK)
            mixed = jnp.dot(ws_ref[g].astype(BF16), vn[:, cs].astype(BF16), preferred_element_type=F32) + bias_ref[g]
            o_ref[:, cs] = (u[:, cs] * mixed).astype(o_ref.dtype)

    full = lambda shape: pl.BlockSpec(shape, lambda c: (0,) * len(shape))
    return pl.pallas_call(
        body, out_shape=jax.ShapeDtypeStruct((seq, B_W), BF16), grid=(seq // CHUNK,),
        in_specs=[pl.BlockSpec((CHUNK, 2 * B_W), lambda c: (c, 0)), full((1, B_W)),
                  full((B_GROUPS, CHUNK, CHUNK)), full((B_GROUPS, CHUNK, CHUNK))],
        out_specs=pl.BlockSpec((CHUNK, B_W), lambda c: (c, 0)),
        compiler_params=_cp("parallel"), name=name)(proj, gain.reshape(1, B_W), w_s, bias_b)


def sgu_bwd(proj, gain, w_s, w_s_t, bias_b, dcat, name="sgu_bwd"):
    seq = proj.shape[0]

    def body(x_ref, gain_ref, ws_ref, wst_ref, bias_ref, do_ref, dx_ref, dws_ref, dmix_ref, dgain_ref, dvn_ref):
        first = pl.program_id(0) == 0
        gain = gain_ref[...]
        u, duv, r, vh, vn = _sgu_front(x_ref[...], gain)
        do = do_ref[...]
        for g in range(B_GROUPS):
            cs = slice(g * CHUNK, (g + 1) * CHUNK)
            vg = vn[:, cs].astype(BF16)
            mixed = jnp.dot(ws_ref[g].astype(BF16), vg, preferred_element_type=F32) + bias_ref[g]
            dx_ref[:, cs] = (do[:, cs] * mixed * duv[:, cs]).astype(dx_ref.dtype)
            dmixed = do[:, cs] * u[:, cs]
            dm16 = dmixed.astype(BF16)
            dws = lax.dot_general(dm16, vg, (((1,), (1,)), ((), ())), preferred_element_type=F32)
            dvn_ref[:, cs] = jnp.dot(wst_ref[g].astype(BF16), dm16, preferred_element_type=F32)

            @pl.when(first)
            def _():
                dws_ref[g] = dws
                dmix_ref[g] = dmixed

            @pl.when(jnp.logical_not(first))
            def _():
                dws_ref[g] += dws
                dmix_ref[g] += dmixed

        dvn = dvn_ref[...]
        dgain = jnp.sum(dvn * vh, axis=0, keepdims=True)

        @pl.when(first)
        def _():
            dgain_ref[...] = dgain

        @pl.when(jnp.logical_not(first))
        def _():
            dgain_ref[...] += dgain

        dv = _rms_back(vh, r, gain, dvn)
        dx_ref[:, B_W:] = (dv * duv[:, B_W:]).astype(dx_ref.dtype)

    full = lambda shape: pl.BlockSpec(shape, lambda c: (0,) * len(shape))
    mats = full((B_GROUPS, CHUNK, CHUNK))
    return pl.pallas_call(
        body, out_shape=(jax.ShapeDtypeStruct((seq, 2 * B_W), BF16), jax.ShapeDtypeStruct((B_GROUPS, CHUNK, CHUNK), F32),
                         jax.ShapeDtypeStruct((B_GROUPS, CHUNK, CHUNK), F32), jax.ShapeDtypeStruct((1, B_W), F32)),
        grid=(seq // CHUNK,),
        in_specs=[pl.BlockSpec((CHUNK, 2 * B_W), lambda c: (c, 0)), full((1, B_W)), mats, mats, mats,
                  pl.BlockSpec((CHUNK, B_W), lambda c: (c, 0))],
        out_specs=(pl.BlockSpec((CHUNK, 2 * B_W), lambda c: (c, 0)), mats, mats, full((1, B_W))),
        scratch_shapes=[pltpu.VMEM((CHUNK, B_W), F32)],
        compiler_params=_cp("arbitrary"), name=name)(proj, gain.reshape(1, B_W), w_s, w_s_t, bias_b, dcat)


FFN_COLS = 256


def _shift_prev(a):
    rows = lax.broadcasted_iota(jnp.int32, a.shape, 0)
    return jnp.where(rows == 0, 0.0, pltpu.roll(a, 1, 0))


def _shift_next(a):
    n = a.shape[0]
    rows = lax.broadcasted_iota(jnp.int32, a.shape, 0)
    return jnp.where(rows == n - 1, 0.0, pltpu.roll(a, n - 1, 0))


def _conv3(a, w, b):
    return _shift_prev(a) * w[0:1] + a * w[1:2] + _shift_next(a) * w[2:3] + b


def ffn_act_fwd(a, conv_w, conv_b, name="ffn_act_fwd"):
    seq = a.shape[0]
    nb = FF // FFN_COLS

    def body(ag_ref, av_ref, wg_ref, wv_ref, bg_ref, bv_ref, o_ref):
        gate = _conv3(ag_ref[...], wg_ref[...], bg_ref[...])
        val = _conv3(av_ref[...], wv_ref[...], bv_ref[...])
        o_ref[...] = (_gelu(gate) * val).astype(o_ref.dtype)

    col = lambda rows, off: pl.BlockSpec((rows, FFN_COLS), lambda j: (0, j + off))
    cb = conv_b.reshape(1, 2 * FF)
    return pl.pallas_call(
        body, out_shape=jax.ShapeDtypeStruct((seq, FF), BF16), grid=(nb,),
        in_specs=[col(seq, 0), col(seq, nb), col(3, 0), col(3, nb), col(1, 0), col(1, nb)],
        out_specs=col(seq, 0), compiler_params=_cp("parallel"), name=name)(a, a, conv_w, conv_w, cb, cb)


def ffn_act_bwd(a, conv_w, conv_b, dact, name="ffn_act_bwd"):
    seq = a.shape[0]
    nb = FF // FFN_COLS

    def back(a_, w, dc, da_ref, dw_ref, db_ref):
        db_ref[...] = jnp.sum(dc, axis=0, keepdims=True)
        dw_ref[0:1, :] = jnp.sum(dc * _shift_prev(a_), axis=0, keepdims=True)
        dw_ref[1:2, :] = jnp.sum(dc * a_, axis=0, keepdims=True)
        dw_ref[2:3, :] = jnp.sum(dc * _shift_next(a_), axis=0, keepdims=True)
        da_ref[...] = (_shift_next(dc) * w[0:1] + dc * w[1:2] + _shift_prev(dc) * w[2:3]).astype(da_ref.dtype)

    def body(ag_ref, av_ref, wg_ref, wv_ref, bg_ref, bv_ref, d_ref, dag_ref, dav_ref, dwg_ref, dwv_ref, dbg_ref, dbv_ref):
        ag, av, wg, wv = ag_ref[...], av_ref[...], wg_ref[...], wv_ref[...]
        gate = _conv3(ag, wg, bg_ref[...])
        val = _conv3(av, wv, bv_ref[...])
        act, dact_dgate = _gelu_parts(gate)
        d = d_ref[...].astype(F32)
        back(ag, wg, d * val * dact_dgate, dag_ref, dwg_ref, dbg_ref)
        back(av, wv, d * act, dav_ref, dwv_ref, dbv_ref)

    col = lambda rows, off: pl.BlockSpec((rows, FFN_COLS), lambda j: (0, j + off))
    cb = conv_b.reshape(1, 2 * FF)
    dag, dav, dwg, dwv, dbg, dbv = pl.pallas_call(
        body, out_shape=(jax.ShapeDtypeStruct((seq, FF), BF16),) * 2 + (jax.ShapeDtypeStruct((3, FF), F32),) * 2
        + (jax.ShapeDtypeStruct((1, FF), F32),) * 2, grid=(nb,),
        in_specs=[col(seq, 0), col(seq, nb), col(3, 0), col(3, nb), col(1, 0), col(1, nb), col(seq, 0)],
        out_specs=(col(seq, 0), col(seq, 0), col(3, 0), col(3, 0), col(1, 0), col(1, 0)),
        compiler_params=_cp("parallel"), name=name)(a, a, conv_w, conv_w, cb, cb, dact)
    cat = lambda p, q: jnp.concatenate([p, q], axis=1)
    return cat(dag, dav), cat(dwg, dwv), cat(dbg, dbv)


def _adam_math(w, g, m, v):
    m = ADAM_B1 * m + (1.0 - ADAM_B1) * g
    v = ADAM_B2 * v + (1.0 - ADAM_B2) * (g * g)
    m_hat = m / (1.0 - ADAM_B1 ** ADAM_STEP)
    v_hat = v / (1.0 - ADAM_B2 ** ADAM_STEP)
    return -ADAM_LR * (m_hat / (jnp.sqrt(v_hat) + ADAM_EPS) + ADAM_WD * w), m, v


def _row_tile(rows, cols):
    return _pick(rows, (256, 128, 64)) if cols <= 1024 else _pick(rows, (128, 64))


def adamw_layer(w_all, m_all, v_all, layer, g, prev, name):
    n, rows, cols = w_all.shape
    tr = _row_tile(rows, cols)

    def body(w_ref, m_ref, v_ref, g_ref, *rest):
        go_ref, d_ref, mo_ref, vo_ref = rest[-4:]
        g_ = g_ref[...]
        d, m_, v_ = _adam_math(w_ref[...], g_, m_ref[...], v_ref[...])
        go_ref[...] = g_
        d_ref[...] = d
        mo_ref[...] = m_
        vo_ref[...] = v_

    lay = pl.BlockSpec((None, tr, cols), lambda i: (layer, i, 0))
    in_specs = [lay, lay, lay, pl.BlockSpec((tr, cols), lambda i: (i, 0))]
    args = [w_all, m_all, v_all, g]
    aliases = {}
    if prev is not None:
        in_specs += [pl.BlockSpec(memory_space=pl.ANY)] * 4
        args += list(prev)
        aliases = {4 + k: k for k in range(4)}
    return pl.pallas_call(
        body, out_shape=(jax.ShapeDtypeStruct(w_all.shape, F32),) * 4, grid=(rows // tr,),
        in_specs=in_specs, out_specs=(lay,) * 4, input_output_aliases=aliases,
        compiler_params=_cp("parallel"), name=name)(*args)


def adamw_flat(w, g, m, v, name="adamw_small"):
    rows, cols = w.shape
    tr = _pick(rows, (128, 8))

    def body(w_ref, g_ref, m_ref, v_ref, d_ref, mo_ref, vo_ref):
        d_ref[...], mo_ref[...], vo_ref[...] = _adam_math(w_ref[...], g_ref[...], m_ref[...], v_ref[...])

    spec = pl.BlockSpec((tr, cols), lambda i: (i, 0))
    return pl.pallas_call(
        body, out_shape=(jax.ShapeDtypeStruct(w.shape, F32),) * 3, grid=(rows // tr,),
        in_specs=[spec] * 4, out_specs=(spec,) * 3, compiler_params=_cp("parallel"), name=name)(w, g, m, v)


def pair_sum(dw, got, core, name):
    _, rows, cols = dw.shape
    half = rows // 2
    tr = _row_tile(half, cols)
    nrb = half // tr

    def body(c_ref, a_ref, b_ref, o_ref):
        o_ref[...] = (a_ref[...].astype(F32) + b_ref[...].astype(F32)).astype(o_ref.dtype)

    return pl.pallas_call(
        body, out_shape=jax.ShapeDtypeStruct((N_CHIPS, half, cols), BF16),
        grid_spec=pltpu.PrefetchScalarGridSpec(
            num_scalar_prefetch=1, grid=(N_CHIPS, nrb),
            in_specs=[pl.BlockSpec((None, tr, cols), lambda s, i, c_ref: (s, c_ref[0] * nrb + i, 0)),
                      pl.BlockSpec((None, tr, cols), lambda s, i, c_ref: (s, i, 0))],
            out_specs=pl.BlockSpec((None, tr, cols), lambda s, i, c_ref: (s, i, 0))),
        compiler_params=_cp("parallel", "parallel"), name=name)(core, dw, got)


def chip_sum(own, parts, place, name):
    _, half, cols = parts.shape
    tr = _row_tile(half, cols)
    nrb = half // tr

    def body(p_ref, own_ref, a_ref, b_ref, c_ref, o_ref):
        o_ref[...] = ((own_ref[...].astype(F32) + a_ref[...].astype(F32)) + b_ref[...].astype(F32)) + c_ref[...].astype(F32)

    def slot(k):
        return pl.BlockSpec((None, tr, cols), lambda i, p: (jnp.bitwise_xor(p[0], k), i, 0))

    return pl.pallas_call(
        body, out_shape=jax.ShapeDtypeStruct((2 * half, cols), F32),
        grid_spec=pltpu.PrefetchScalarGridSpec(
            num_scalar_prefetch=1, grid=(nrb,), in_specs=[slot(0), slot(1), slot(2), slot(3)],
            out_specs=pl.BlockSpec((tr, cols), lambda i, p: (p[1] * nrb + i, 0))),
        compiler_params=_cp("parallel"), name=name)(place, own, parts, parts, parts)


def cast_to_slot(w_all, layer, place, name):
    _, rows, cols = w_all.shape
    tr = _row_tile(rows, cols)

    def body(p_ref, w_ref, o_ref):
        o_ref[...] = w_ref[...].astype(o_ref.dtype)

    return pl.pallas_call(
        body, out_shape=jax.ShapeDtypeStruct((N_CHIPS, rows, cols), BF16),
        grid_spec=pltpu.PrefetchScalarGridSpec(
            num_scalar_prefetch=1, grid=(rows // tr,),
            in_specs=[pl.BlockSpec((None, tr, cols), lambda i, p: (layer, i, 0))],
            out_specs=pl.BlockSpec((None, tr, cols), lambda i, p: (p[0], i, 0))),
        compiler_params=_cp("parallel"), name=name)(place, w_all)


ANY = pl.BlockSpec(memory_space=pl.ANY)


def _place():
    x, y, c = lax.axis_index("x"), lax.axis_index("y"), lax.axis_index("c")
    others = [(1 - x, y), (x, 1 - y), (1 - x, 1 - y)]
    return x, y, c, 2 * x + y, others


def _remote(src, dst, send_sem, recv_sem, dev):
    return pltpu.make_async_remote_copy(src_ref=src, dst_ref=dst, send_sem=send_sem, recv_sem=recv_sem,
                                        device_id=dev, device_id_type=MESH)


HBM = pl.BlockSpec(memory_space=pltpu.HBM)
SEM = pl.BlockSpec(memory_space=pltpu.SEMAPHORE)
EFFECT = pltpu.SideEffectType.DATAFLOW_SIDE_EFFECTING
TOKEN = jax.ShapeDtypeStruct((8, LANES), F32)


def _in_hbm(a):
    return pltpu.with_memory_space_constraint(a, pltpu.HBM)


def _gather_copies(bufs, send_sems, recv_sems):
    x, y, c, me, others = _place()
    out = []
    for w, buf in enumerate(bufs):
        half = buf.shape[1] // 2
        mine = pl.ds(c * half, half)
        for k, (ox, oy) in enumerate(others):
            sems = send_sems.at[3 * w + k], recv_sems.at[3 * w + k]
            out.append((_remote(buf.at[me, mine], buf.at[me, mine], *sems, (ox, oy, c)),
                        _remote(buf.at[me, mine], buf.at[2 * ox + oy, mine], *sems, (ox, oy, c))))
    return out


def gather_start(bufs, name):
    n = len(bufs)

    def body(*refs):
        ins, (send_sems, recv_sems), token = refs[:n], refs[n:n + 2], refs[-1]
        for sent, _ in _gather_copies(ins, send_sems, recv_sems):
            sent.start()
        token[...] = jnp.zeros_like(token)

    outs = pl.pallas_call(
        body, name=name,
        out_shape=(pltpu.SemaphoreType.DMA((3 * n,)), pltpu.SemaphoreType.DMA((3 * n,)),
                   *[pltpu.HBM(b.shape, b.dtype) for b in bufs], TOKEN),
        in_specs=[HBM] * n, out_specs=(SEM, SEM, *[HBM] * n, VM),
        input_output_aliases={w: 2 + w for w in range(n)},
        compiler_params=pltpu.CompilerParams(has_side_effects=EFFECT))(*[_in_hbm(b) for b in bufs])
    return outs[0], outs[1], list(outs[2:2 + n]), outs[-1]


def gather_wait(send_sems, recv_sems, bufs, after, name):
    n = len(bufs)

    def body(*refs):
        ins, (send_ref, recv_ref) = refs[:n], refs[n:n + 2]
        for sent, landed in _gather_copies(ins, send_ref, recv_ref):
            sent.wait_send()
            landed.wait_recv()

    return list(pl.pallas_call(
        body, name=name, out_shape=tuple(pltpu.HBM(b.shape, b.dtype) for b in bufs),
        in_specs=[HBM] * n + [SEM, SEM, ANY], out_specs=(HBM,) * n,
        input_output_aliases={w: w for w in range(n)},
        compiler_params=pltpu.CompilerParams(has_side_effects=EFFECT))(*bufs, send_sems, recv_sems, after))


def forward_to_sibling(bufs, name):
    n = len(bufs)

    def body(*refs):
        ins, outs = refs[:n], refs[n:2 * n]
        send_sems, recv_sems = refs[2 * n:]
        x, y, c, me, others = _place()
        sends = []
        for w in range(n):
            half = ins[w].shape[1] // 2
            mine = pl.ds(c * half, half)
            for k, (ox, oy) in enumerate(others):
                cp = _remote(ins[w].at[2 * ox + oy, mine], outs[w].at[2 * ox + oy, mine], send_sems.at[3 * w + k],
                             recv_sems.at[3 * w + k], (x, y, 1 - c))
                cp.start()
                sends.append(cp)
        for w in range(n):
            half = ins[w].shape[1] // 2
            theirs = pl.ds((1 - c) * half, half)
            for k, (ox, oy) in enumerate(others):
                passed = outs[w].at[2 * ox + oy, theirs]
                _remote(passed, passed, send_sems.at[3 * w + k], recv_sems.at[3 * w + k], (x, y, 1 - c)).wait_recv()
        for cp in sends:
            cp.wait_send()

    return list(pl.pallas_call(
        body, out_shape=tuple(jax.ShapeDtypeStruct(b.shape, b.dtype) for b in bufs),
        in_specs=[ANY] * n, out_specs=(ANY,) * n, input_output_aliases={w: w for w in range(n)},
        scratch_shapes=[pltpu.SemaphoreType.DMA((3 * n,)), pltpu.SemaphoreType.DMA((3 * n,))],
        name=name)(*bufs))


def swap_halves(grads, name):
    n = len(grads)

    def body(*refs):
        ins, outs = refs[:n], refs[n:2 * n]
        send_sems, recv_sems = refs[2 * n:]
        x, y, c, _, _ = _place()
        copies = []
        for w in range(n):
            half = ins[w].shape[1] // 2
            cp = _remote(ins[w].at[:, pl.ds((1 - c) * half, half)], outs[w], send_sems.at[w], recv_sems.at[w], (x, y, 1 - c))
            cp.start()
            copies.append(cp)
        for cp in copies:
            cp.wait()

    return pl.pallas_call(
        body, out_shape=tuple(jax.ShapeDtypeStruct((N_CHIPS, g.shape[1] // 2, g.shape[2]), g.dtype) for g in grads),
        in_specs=[ANY] * n, out_specs=(ANY,) * n,
        scratch_shapes=[pltpu.SemaphoreType.DMA((n,)), pltpu.SemaphoreType.DMA((n,))], name=name)(*grads)


def _exchange_copies(sums, lands, send_sems, recv_sems):
    x, y, c, me, others = _place()
    out = []
    for w, (src, land) in enumerate(zip(sums, lands)):
        for k, (ox, oy) in enumerate(others):
            sems = send_sems.at[3 * w + k], recv_sems.at[3 * w + k]
            out.append((_remote(src.at[2 * ox + oy], land.at[me], *sems, (ox, oy, c)),
                        _remote(src.at[2 * ox + oy], land.at[2 * ox + oy], *sems, (ox, oy, c))))
    return out


def exchange_start(sums, name):
    n = len(sums)

    def body(*refs):
        ins, lands, (send_sems, recv_sems), token = refs[:n], refs[n:2 * n], refs[2 * n:2 * n + 2], refs[-1]
        for sent, _ in _exchange_copies(ins, lands, send_sems, recv_sems):
            sent.start()
        token[...] = jnp.zeros_like(token)

    zones = [_in_hbm(lax.empty(s.shape, s.dtype)) for s in sums]
    outs = pl.pallas_call(
        body, name=name,
        out_shape=(pltpu.SemaphoreType.DMA((3 * n,)), pltpu.SemaphoreType.DMA((3 * n,)),
                   *[pltpu.HBM(s.shape, s.dtype) for s in sums] * 2, TOKEN),
        in_specs=[HBM] * (2 * n), out_specs=(SEM, SEM, *[HBM] * (2 * n), VM),
        input_output_aliases={w: 2 + w for w in range(2 * n)},
        compiler_params=pltpu.CompilerParams(has_side_effects=EFFECT))(*[_in_hbm(s) for s in sums], *zones)
    return outs[0], outs[1], list(outs[2:2 + n]), list(outs[2 + n:2 + 2 * n]), outs[-1]


def exchange_wait(send_sems, recv_sems, sums, lands, after, name):
    n = len(sums)

    def body(*refs):
        ins, zones, (send_ref, recv_ref) = refs[:n], refs[n:2 * n], refs[2 * n:2 * n + 2]
        for sent, landed in _exchange_copies(ins, zones, send_ref, recv_ref):
            sent.wait_send()
            landed.wait_recv()

    outs = pl.pallas_call(
        body, name=name, out_shape=tuple(pltpu.HBM(s.shape, s.dtype) for s in sums) * 2,
        in_specs=[HBM] * (2 * n) + [SEM, SEM, ANY], out_specs=(HBM,) * (2 * n),
        input_output_aliases={w: w for w in range(2 * n)},
        compiler_params=pltpu.CompilerParams(has_side_effects=EFFECT))(*sums, *lands, send_sems, recv_sems, after)
    return list(outs[:n]), list(outs[n:])


def join_halves(grads, name):
    n = len(grads)

    def body(*refs):
        ins, outs = refs[:n], refs[n:2 * n]
        send_sems, recv_sems = refs[2 * n:]
        x, y, c, _, _ = _place()
        copies = []
        for w in range(n):
            half = ins[w].shape[0] // 2
            mine = pl.ds(c * half, half)
            cp = _remote(ins[w].at[mine], outs[w].at[mine], send_sems.at[w], recv_sems.at[w], (x, y, 1 - c))
            cp.start()
            copies.append(cp)
        for w, cp in enumerate(copies):
            half = ins[w].shape[0] // 2
            theirs = outs[w].at[pl.ds((1 - c) * half, half)]
            _remote(theirs, theirs, send_sems.at[w], recv_sems.at[w], (x, y, 1 - c)).wait_recv()
            cp.wait_send()

    return pl.pallas_call(
        body, out_shape=tuple(jax.ShapeDtypeStruct(g.shape, g.dtype) for g in grads),
        in_specs=[ANY] * n, out_specs=(ANY,) * n, input_output_aliases={w: w for w in range(n)},
        scratch_shapes=[pltpu.SemaphoreType.DMA((n,)), pltpu.SemaphoreType.DMA((n,))],
        name=name)(*grads)


VM = pl.BlockSpec(memory_space=pltpu.VMEM)


def small_allgather(buf, name="small_allgather"):
    def body(in_ref, out_ref, send_sems, recv_sems):
        x, y, c, me, others = _place()
        out_ref[me] = in_ref[...]
        copies = []
        for k, (ox, oy) in enumerate(others):
            cp = _remote(in_ref, out_ref.at[me], send_sems.at[k], recv_sems.at[k], (ox, oy, c))
            cp.start()
            copies.append(cp)
        for k, (ox, oy) in enumerate(others):
            landed = out_ref.at[2 * ox + oy]
            _remote(landed, landed, send_sems.at[k], recv_sems.at[k], (ox, oy, c)).wait_recv()
        for cp in copies:
            cp.wait_send()

    return pl.pallas_call(
        body, out_shape=jax.ShapeDtypeStruct((N_CHIPS,) + buf.shape, buf.dtype), in_specs=[VM], out_specs=VM,
        scratch_shapes=[pltpu.SemaphoreType.DMA((3,)), pltpu.SemaphoreType.DMA((3,))],
        compiler_params=pltpu.CompilerParams(vmem_limit_bytes=V7X_VMEM_LIMIT), name=name)(buf)


def small_allreduce(buf, name="small_allreduce"):
    def body(in_ref, out_ref, sib_ref, slot_ref, send_sems, recv_sems):
        x, y, c, me, others = _place()
        cp = _remote(in_ref, sib_ref, send_sems.at[3], recv_sems.at[3], (x, y, 1 - c))
        cp.start()
        cp.wait()
        slot_ref[me] = in_ref[...] + sib_ref[...]
        copies = []
        for k, (ox, oy) in enumerate(others):
            cp = _remote(slot_ref.at[me], slot_ref.at[me], send_sems.at[k], recv_sems.at[k], (ox, oy, c))
            cp.start()
            copies.append(cp)
        for k, (ox, oy) in enumerate(others):
            landed = slot_ref.at[2 * ox + oy]
            _remote(landed, landed, send_sems.at[k], recv_sems.at[k], (ox, oy, c)).wait_recv()
        for cp in copies:
            cp.wait_send()
        out_ref[...] = ((slot_ref[0] + slot_ref[1]) + slot_ref[2]) + slot_ref[3]

    return pl.pallas_call(
        body, out_shape=jax.ShapeDtypeStruct(buf.shape, buf.dtype), in_specs=[VM], out_specs=VM,
        scratch_shapes=[pltpu.VMEM(buf.shape, buf.dtype), pltpu.VMEM((N_CHIPS,) + buf.shape, buf.dtype),
                        pltpu.SemaphoreType.DMA((4,)), pltpu.SemaphoreType.DMA((4,))],
        compiler_params=pltpu.CompilerParams(vmem_limit_bytes=V7X_VMEM_LIMIT), name=name)(buf)


def _pack_rows(arrays, row_multiple):
    flat = jnp.concatenate([a.reshape(-1) for a in arrays])
    rows = -(-flat.shape[0] // (LANES * row_multiple)) * row_multiple
    return jnp.pad(flat, (0, rows * LANES - flat.shape[0])).reshape(rows, LANES)


def _unpack_rows(buf, shapes):
    flat = buf.reshape(-1)
    out, at = [], 0
    for s in shapes:
        n = math.prod(s)
        out.append(flat[at:at + n].reshape(s))
        at += n
    return out


def _mixer_weights(i):
    j = i // 2
    mixer = "a" if i % 2 == 0 else "b"
    return [("w_mem_kv", i), (mixer + "_w_in", j), (mixer + "_w_out", j)]


def _ffn_weights(i):
    return [("ffn_w_up", i), ("ffn_w_down", i)]


def _mixer_fwd(i, x, mem, w, small, after):
    is_a = i % 2 == 0
    j = i // 2
    wkv, win, wout = w
    wkv = wkv.reshape(1, D_MODEL, 2 * MEM_W)
    h1 = rms_fwd(x, small["mix_norm_g"][i], BF16, name=f"mix_norm{i}", after=after)
    mem_n = rms_fwd(mem, small["mem_norm_g"][i], BF16, name=f"mem_norm{i}")
    kv = mm_nn(mem_n, wkv, F32, name=f"mem_kv{i}")
    proj = mm_nn(h1, win, F32, name=f"in_proj{i}")
    saved = dict(x0=x, h1=h1, mem_n=mem_n, kv=kv, proj=proj)
    if is_a:
        outs, lses = zip(*[attn_fwd(proj, g, name=f"attn_fwd{i}_{g}") for g in range(3)])
        comb, lse = attn_combine(outs, lses, name=f"attn_combine{i}")
        mem_out = mem_fwd(proj, 3 * A_QKV_W, kv, name=f"mem_fwd{i}")
        cat = jnp.concatenate([comb.astype(BF16), mem_out], axis=1)
        saved.update(comb=comb, lse=lse)
    else:
        wout = wout.reshape(1, B_W + MEM_W, D_MODEL)
        tok = sgu_fwd(proj, small["b_v_norm_g"][j], small["b_w_s"][j], small["bias_b"][j], name=f"sgu_fwd{i}")
        mem_out = mem_fwd(proj, 2 * B_W, kv, name=f"mem_fwd{i}")
        cat = jnp.concatenate([tok, mem_out], axis=1)
    x1 = mm_nn(cat, wout, F32, res=x, name=f"out_proj{i}")
    saved.update(cat=cat)
    return x1, saved


def _ffn_fwd(i, x1, w, small, after):
    wup, wdn = w
    h2 = rms_fwd(x1, small["ffn_norm_g"][i], BF16, name=f"ffn_norm{i}", after=after)
    a = mm_nn(h2, wup, F32, name=f"ffn_up{i}")
    act = ffn_act_fwd(a, small["ffn_conv_w"][i], small["ffn_conv_b"][i], name=f"ffn_act{i}")
    x2 = mm_nn(act, wdn.reshape(1, FF, D_MODEL), F32, res=x1, name=f"ffn_down{i}")
    return x2, dict(x1=x1, h2=h2, a=a, act=act)


def _ffn_bwd(i, dx2, w, small, sv, after):
    wup, wdn = w
    sg = {}
    dact = mm_nt(dx2, wdn.reshape(1, FF, D_MODEL), F32, name=f"d_act{i}", after=after)
    d_wdn = mm_tn(sv["act"], dx2, 1, BF16, name=f"d_wdown{i}").reshape(N_CHIPS, FF // N_CHIPS, D_MODEL)
    da, sg["ffn_conv_w"], sg["ffn_conv_b"] = ffn_act_bwd(sv["a"], small["ffn_conv_w"][i], small["ffn_conv_b"][i], dact,
                                                          name=f"ffn_act_bwd{i}")
    d_wup = mm_tn(sv["h2"], da, N_CHIPS, BF16, name=f"d_wup{i}")
    dh2 = mm_nt(da, wup, F32, name=f"d_h2_{i}")
    dx1, sg["ffn_norm_g"] = rms_bwd(sv["x1"], small["ffn_norm_g"][i], dh2, dres=dx2, name=f"ffn_norm_bwd{i}")
    return dx1, [d_wup, d_wdn], sg


def _mixer_bwd(i, dx1, mem, w, small, sv, after):
    is_a = i % 2 == 0
    j = i // 2
    wkv, win, wout = w
    wkv = wkv.reshape(1, D_MODEL, 2 * MEM_W)
    sg = {}
    proj, kv = sv["proj"], sv["kv"]
    if is_a:
        dcat = mm_nt(dx1, wout, F32, name=f"d_cat{i}", after=after)
        d_wout = mm_tn(sv["cat"], dx1, N_CHIPS, BF16, name=f"d_wout{i}")
        dqm, dkv = mem_bwd(proj, 3 * A_QKV_W, kv, dcat, A_OUT_W, name=f"mem_bwd{i}")
        parts = [attn_bwd(proj, dcat, sv["comb"], sv["lse"], g, name=f"attn_bwd{i}_{g}") for g in range(3)]
        dproj = jnp.concatenate([parts[g][p] for p in range(3) for g in range(3)] + [dqm], axis=1)
    else:
        dcat = mm_nt(dx1, wout.reshape(1, B_W + MEM_W, D_MODEL), F32, name=f"d_cat{i}", after=after)
        d_wout = mm_tn(sv["cat"], dx1, 1, BF16, name=f"d_wout{i}").reshape(N_CHIPS, (B_W + MEM_W) // N_CHIPS, D_MODEL)
        dqm, dkv = mem_bwd(proj, 2 * B_W, kv, dcat, B_W, name=f"mem_bwd{i}")
        w_s = small["b_w_s"][j]
        duv, sg["b_w_s"], dmix, sg["b_v_norm_g"] = sgu_bwd(proj, small["b_v_norm_g"][j], w_s, jnp.swapaxes(w_s, 1, 2),
                                                           small["bias_b"][j], dcat, name=f"sgu_bwd{i}")
        sg["b_s_bias"] = jnp.sum(dmix, axis=-1)
        dproj = jnp.concatenate([duv, dqm], axis=1)
    d_wkv = mm_tn(sv["mem_n"], dkv, 1, BF16, name=f"d_wkv{i}").reshape(N_CHIPS, D_MODEL // N_CHIPS, 2 * MEM_W)
    dmem_n = mm_nt(dkv, wkv, F32, name=f"d_mem_n{i}")
    _, sg["mem_norm_g"] = rms_bwd(mem, small["mem_norm_g"][i], dmem_n, name=f"mem_norm_bwd{i}")
    d_win = mm_tn(sv["h1"], dproj, N_CHIPS, BF16, name=f"d_win{i}")
    dh1 = mm_nt(dproj, win, F32, name=f"d_h1_{i}")
    dx0, sg["mix_norm_g"] = rms_bwd(sv["x0"], small["mix_norm_g"][i], dh1, dres=dx1, name=f"mix_norm_bwd{i}")
    return dx0, [d_wkv, d_win, d_wout], sg


def _reduce_start(grads, place, tag):
    got = swap_halves(grads, name=f"swap_halves{tag}")
    sums = [pair_sum(g, o, place[1:], name=f"pair_sum{tag}_{k}") for k, (g, o) in enumerate(zip(grads, got))]
    send_sems, recv_sems, sums, lands, token = exchange_start(sums, name=f"exchange_start{tag}")
    return (send_sems, recv_sems, sums, lands), token


def _reduce_finish(started, place, after, tag):
    sums, parts = exchange_wait(*started, after, name=f"exchange_wait{tag}")
    halves = [chip_sum(s, p, place, name=f"chip_sum{tag}_{k}") for k, (s, p) in enumerate(zip(sums, parts))]
    return join_halves(halves, name=f"join_halves{tag}")


SMALL_SHARDED = ("b_v_norm_g", "ffn_conv_w")
SMALL_FULL_SHAPES = dict(mix_norm_g=(D_MODEL,), ffn_norm_g=(D_MODEL,), mem_norm_g=(D_MODEL,), b_v_norm_g=(B_W,),
                         b_w_s=(B_GROUPS, CHUNK, CHUNK), b_s_bias=(B_GROUPS, CHUNK), ffn_conv_w=(3, 2 * FF),
                         ffn_conv_b=(2 * FF,))
BIG = ("w_mem_kv", "a_w_in", "a_w_out", "b_w_in", "b_w_out", "ffn_w_up", "ffn_w_down")
WEIGHT_ORDER = ("mix_norm_g", "ffn_norm_g", "mem_norm_g", "w_mem_kv", "a_w_in", "a_w_out", "b_w_in", "b_v_norm_g", "b_w_s",
                "b_s_bias", "b_w_out", "ffn_w_up", "ffn_conv_w", "ffn_conv_b", "ffn_w_down", "final_norm_g")


def kernel(x, mem, mix_norm_g, ffn_norm_g, mem_norm_g, w_mem_kv, a_w_in, a_w_out, b_w_in, b_v_norm_g, b_w_s, b_s_bias, b_w_out, ffn_w_up, ffn_conv_w, ffn_conv_b, ffn_w_down, final_norm_g, loss_target, m_mix_norm_g, m_ffn_norm_g, m_mem_norm_g, m_w_mem_kv, m_a_w_in, m_a_w_out, m_b_w_in, m_b_v_norm_g, m_b_w_s, m_b_s_bias, m_b_w_out, m_ffn_w_up, m_ffn_conv_w, m_ffn_conv_b, m_ffn_w_down, m_final_norm_g, v_mix_norm_g, v_ffn_norm_g, v_mem_norm_g, v_w_mem_kv, v_a_w_in, v_a_w_out, v_b_w_in, v_b_v_norm_g, v_b_w_s, v_b_s_bias, v_b_w_out, v_ffn_w_up, v_ffn_conv_w, v_ffn_conv_b, v_ffn_w_down, v_final_norm_g):
    weights = dict(mix_norm_g=mix_norm_g, ffn_norm_g=ffn_norm_g, mem_norm_g=mem_norm_g, w_mem_kv=w_mem_kv, a_w_in=a_w_in,
                   a_w_out=a_w_out, b_w_in=b_w_in, b_v_norm_g=b_v_norm_g, b_w_s=b_w_s, b_s_bias=b_s_bias, b_w_out=b_w_out,
                   ffn_w_up=ffn_w_up, ffn_conv_w=ffn_conv_w, ffn_conv_b=ffn_conv_b, ffn_w_down=ffn_w_down,
                   final_norm_g=final_norm_g)
    mom1 = dict(mix_norm_g=m_mix_norm_g, ffn_norm_g=m_ffn_norm_g, mem_norm_g=m_mem_norm_g, w_mem_kv=m_w_mem_kv,
                a_w_in=m_a_w_in, a_w_out=m_a_w_out, b_w_in=m_b_w_in, b_v_norm_g=m_b_v_norm_g, b_w_s=m_b_w_s,
                b_s_bias=m_b_s_bias, b_w_out=m_b_w_out, ffn_w_up=m_ffn_w_up, ffn_conv_w=m_ffn_conv_w,
                ffn_conv_b=m_ffn_conv_b, ffn_w_down=m_ffn_w_down, final_norm_g=m_final_norm_g)
    mom2 = dict(mix_norm_g=v_mix_norm_g, ffn_norm_g=v_ffn_norm_g, mem_norm_g=v_mem_norm_g, w_mem_kv=v_w_mem_kv,
                a_w_in=v_a_w_in, a_w_out=v_a_w_out, b_w_in=v_b_w_in, b_v_norm_g=v_b_v_norm_g, b_w_s=v_b_w_s,
                b_s_bias=v_b_s_bias, b_w_out=v_b_w_out, ffn_w_up=v_ffn_w_up, ffn_conv_w=v_ffn_conv_w,
                ffn_conv_b=v_ffn_conv_b, ffn_w_down=v_ffn_w_down, final_norm_g=v_final_norm_g)
    chip = 2 * lax.axis_index("x") + lax.axis_index("y")
    place = jnp.stack([chip, lax.axis_index("c")]).astype(jnp.int32)
    x0, mem0, target = x[0], mem[0], loss_target[0]
    depth = DEPTH

    n_cw, n_vg = ffn_conv_w.size, b_v_norm_g.size
    gathered = small_allgather(_pack_rows([ffn_conv_w, b_v_norm_g], 8)).reshape(N_CHIPS, -1)
    conv_w_full = gathered[:, :n_cw].reshape(N_CHIPS, DEPTH, 3, 2 * FF // N_CHIPS).transpose(1, 2, 0, 3).reshape(DEPTH, 3, 2 * FF)
    vgain_full = gathered[:, n_cw:n_cw + n_vg].reshape(N_CHIPS, 2, B_W // N_CHIPS).transpose(1, 0, 2).reshape(2, B_W)
    small = dict(mix_norm_g=mix_norm_g, ffn_norm_g=ffn_norm_g, mem_norm_g=mem_norm_g, b_w_s=b_w_s, ffn_conv_b=ffn_conv_b,
                 ffn_conv_w=conv_w_full, b_v_norm_g=vgain_full,
                 bias_b=jnp.broadcast_to(b_s_bias[..., None], b_s_bias.shape + (CHUNK,)))

    def start_gather(group, tag):
        bufs = [cast_to_slot(weights[n], l, place, name=f"cast_{n}{l}") for n, l in group]
        return gather_start(bufs, name=f"gather_start_{tag}")

    def finish_gather(started, after, tag):
        send_sems, recv_sems, bufs, _ = started
        return forward_to_sibling(gather_wait(send_sems, recv_sems, bufs, after, name=f"gather_wait_{tag}"),
                                  name=f"forward_{tag}")

    w_mix, w_ffn, saved_mix, saved_ffn = [], [], [], []
    flying_mix = start_gather(_mixer_weights(0), "m0")
    flying_ffn = start_gather(_ffn_weights(0), "f0")
    h, tie = x0, flying_ffn[3]
    for i in range(depth):
        w_mix.append(finish_gather(flying_mix, tie if i == 0 else h, f"m{i}"))
        if i + 1 < depth:
            flying_mix = start_gather(_mixer_weights(i + 1), f"m{i + 1}")
            tie = flying_mix[3]
        x1, sv = _mixer_fwd(i, h, mem0, w_mix[i], small, tie)
        saved_mix.append(sv)
        w_ffn.append(finish_gather(flying_ffn, x1, f"f{i}"))
        if i + 1 < depth:
            flying_ffn = start_gather(_ffn_weights(i + 1), f"f{i + 1}")
            tie = flying_ffn[3]
        h, sv = _ffn_fwd(i, x1, w_ffn[i], small, tie)
        saved_ffn.append(sv)
    loss_row, dh, d_final = final_loss(h, final_norm_g, target)
    loss = lax.psum(loss_row[0, 0], ("x", "y", "c"))

    names = [n for n in WEIGHT_ORDER if n not in BIG]
    small_g = {n: [None] * weights[n].shape[0] for n in names if n != "final_norm_g"}
    big_out = {n: None for n in BIG}

    def keep_small(i, sg):
        for n, g in sg.items():
            small_g[n][i if len(small_g[n]) == depth else i // 2] = g.reshape(SMALL_FULL_SHAPES[n])

    def finish_reduce(started, group, after, tag):
        for (n, l), g in zip(group, _reduce_finish(started, place, after, tag)):
            big_out[n] = adamw_layer(weights[n], mom1[n], mom2[n], l, g, big_out[n], name=f"adamw_{n}{l}")

    flying, tie = None, None
    for i in reversed(range(depth)):
        dh, big_g, sg = _ffn_bwd(i, dh, w_ffn[i], small, saved_ffn[i], tie)
        keep_small(i, sg)
        started, tie = _reduce_start(big_g, place, f"f{i}")
        if flying is not None:
            finish_reduce(*flying, dh, f"m{i + 1}")
        flying = (started, _ffn_weights(i))
        dh, big_g, sg = _mixer_bwd(i, dh, mem0, w_mix[i], small, saved_mix[i], tie)
        keep_small(i, sg)
        started, tie = _reduce_start(big_g, place, f"m{i}")
        finish_reduce(*flying, dh, f"f{i}")
        flying = (started, _mixer_weights(i))
    finish_reduce(*flying, tie, "m0")

    full_g = {n: (d_final.reshape(-1) if n == "final_norm_g" else jnp.stack(small_g[n])) for n in names}
    shapes = [full_g[n].shape for n in names]
    summed = dict(zip(names, _unpack_rows(small_allreduce(_pack_rows([full_g[n] for n in names], 8)), shapes)))
    for n in SMALL_SHARDED:
        width = weights[n].shape[-1]
        summed[n] = lax.dynamic_slice_in_dim(summed[n], chip * width, width, axis=summed[n].ndim - 1)
    own_shapes = [weights[n].shape for n in names]
    pack = lambda d: _pack_rows([d[n] for n in names], 128)
    small_out = [_unpack_rows(b, own_shapes) for b in adamw_flat(pack(weights), pack(summed), pack(mom1), pack(mom2))]
    outs = {}
    for k, n in enumerate(names):
        outs[n] = (summed[n], small_out[0][k], small_out[1][k], small_out[2][k])
    outs.update(big_out)
    return (loss, dh[None], *[outs[n][0] for n in WEIGHT_ORDER], *[outs[n][1] for n in WEIGHT_ORDER],
            *[outs[n][2] for n in WEIGHT_ORDER], *[outs[n][3] for n in WEIGHT_ORDER])
```

```python
import functools
import math

import numpy as np
import jax
import jax.numpy as jnp
from jax import lax
from jax.experimental import pallas as pl
from jax.experimental.pallas import tpu as pltpu

F32 = jnp.float32
BF16 = jnp.bfloat16
MESH = pl.DeviceIdType.MESH

D_MODEL = 2048
SEQ = 2048
DEPTH = 4
EPS = 1e-6
NEG = -1e30
HEAD = 128
A_PATTERNS = ((128, 1), (512, 4), (2048, 16))
A_QKV_W = 1536
A_OUT_W = 512
A_IN = 5120
QBLK = 128
N_SIDE = 64
CHUNK = 128
B_GROUPS = 12
B_W = 1536
B_IN = 3584
MEM_LEN = 256
MEM_HEADS = 4
MEM_W = 512
FF = 5632
ADAM_LR, ADAM_B1, ADAM_B2, ADAM_EPS, ADAM_WD, ADAM_STEP = 0.001, 0.9, 0.999, 1e-08, 0.01, 10
N_CHIPS = 4

LANES = 128
V7X_VMEM_LIMIT = 56 * 1024 * 1024


def _cp(*sem):
    return pltpu.CompilerParams(dimension_semantics=sem, vmem_limit_bytes=V7X_VMEM_LIMIT)


def _pick(dim, prefs):
    for p in prefs:
        if dim % p == 0:
            return p
    raise ValueError(f"no tile for {dim} in {prefs}")


def _gelu_parts(x):
    cdf = 0.5 * (1.0 + lax.erf(x * (1.0 / math.sqrt(2.0))))
    pdf = jnp.exp(-0.5 * x * x) * (1.0 / math.sqrt(2.0 * math.pi))
    return x * cdf, cdf + x * pdf


def _gelu(x):
    return 0.5 * x * (1.0 + lax.erf(x * (1.0 / math.sqrt(2.0))))


TM_PREFS = (1024, 512, 256, 128)
TN_PREFS = (1408, 1280, 1024, 896, 512, 256, 128)
TK_PREFS = (2048, 1408, 1280, 1024, 896, 512, 256, 128)


def _mm_body(nk, dims, has_res):
    def body(*refs):
        if has_res:
            a_ref, b_ref, r_ref, o_ref = refs[:4]
        else:
            a_ref, b_ref, o_ref = refs[:3]
            r_ref = None
        part = lax.dot_general(a_ref[...].astype(BF16), b_ref[...].astype(BF16), dims,
                               preferred_element_type=F32)
        if nk == 1:
            if has_res:
                part = part + r_ref[...]
            o_ref[...] = part.astype(o_ref.dtype)
            return
        acc_ref = refs[-1]
        k = pl.program_id(2)

        @pl.when(k == 0)
        def _():
            acc_ref[...] = part

        @pl.when(k > 0)
        def _():
            acc_ref[...] += part

        @pl.when(k == nk - 1)
        def _():
            tot = acc_ref[...]
            if has_res:
                tot = tot + r_ref[...]
            o_ref[...] = tot.astype(o_ref.dtype)
    return body


def mm_nn(a, w, out_dtype, res=None, name="mm_nn"):
    m, kw = a.shape
    ns_, kw2, nsz = w.shape
    assert kw == kw2
    n = ns_ * nsz
    tm, tn, tk = _pick(m, TM_PREFS), _pick(nsz, TN_PREFS), _pick(kw, TK_PREFS)
    nb, nk = nsz // tn, kw // tk
    in_specs = [pl.BlockSpec((tm, tk), lambda i, j, k: (i, k)),
                pl.BlockSpec((None, tk, tn), lambda i, j, k: (j // nb, k, j % nb))]
    args = [a, w]
    if res is not None:
        in_specs.append(pl.BlockSpec((tm, tn), lambda i, j, k: (i, j)))
        args.append(res)
    return pl.pallas_call(
        _mm_body(nk, (((1,), (0,)), ((), ())), res is not None),
        out_shape=jax.ShapeDtypeStruct((m, n), out_dtype),
        grid=(m // tm, n // tn, nk), in_specs=in_specs,
        out_specs=pl.BlockSpec((tm, tn), lambda i, j, k: (i, j)),
        scratch_shapes=[pltpu.VMEM((tm, tn), F32)] if nk > 1 else [],
        compiler_params=_cp("parallel", "parallel", "arbitrary"), name=name)(*args)


def mm_nt(g, w, out_dtype, name="mm_nt", after=None):
    m, n = g.shape
    ns_, kw, nsz = w.shape
    assert n == ns_ * nsz
    tm, tn, tk = _pick(m, TM_PREFS), _pick(kw, TN_PREFS), _pick(nsz, TK_PREFS)
    nb, nk = nsz // tk, n // tk
    body = _mm_body(nk, (((1,), (1,)), ((), ())), False)
    tied = [] if after is None else [after]
    return pl.pallas_call(
        (lambda g_ref, w_ref, *rest: body(g_ref, w_ref, *rest[len(tied):])),
        out_shape=jax.ShapeDtypeStruct((m, kw), out_dtype),
        grid=(m // tm, kw // tn, nk),
        in_specs=[pl.BlockSpec((tm, tk), lambda i, j, k: (i, k)),
                  pl.BlockSpec((None, tn, tk), lambda i, j, k: (k // nb, j, k % nb))] + [ANY] * len(tied),
        out_specs=pl.BlockSpec((tm, tn), lambda i, j, k: (i, j)),
        scratch_shapes=[pltpu.VMEM((tm, tn), F32)] if nk > 1 else [],
        compiler_params=_cp("parallel", "parallel", "arbitrary"), name=name)(g, w, *tied)


def mm_tn(a, g, n_shards, out_dtype, name="mm_tn"):
    t, kw = a.shape
    t2, n = g.shape
    assert t == t2
    nsz = n // n_shards
    tm, tn, tk = _pick(kw, TM_PREFS), _pick(nsz, TN_PREFS), _pick(t, TK_PREFS)
    nb, nk = nsz // tn, t // tk
    return pl.pallas_call(
        _mm_body(nk, (((0,), (0,)), ((), ())), False),
        out_shape=jax.ShapeDtypeStruct((n_shards, kw, nsz), out_dtype),
        grid=(kw // tm, n // tn, nk),
        in_specs=[pl.BlockSpec((tk, tm), lambda i, j, k: (k, i)),
                  pl.BlockSpec((tk, tn), lambda i, j, k: (k, j))],
        out_specs=pl.BlockSpec((None, tm, tn), lambda i, j, k: (j // nb, i, j % nb)),
        scratch_shapes=[pltpu.VMEM((tm, tn), F32)] if nk > 1 else [],
        compiler_params=_cp("parallel", "parallel", "arbitrary"), name=name)(a, g)


ROW_TILE = 256


def _rms_stats(x):
    r = lax.rsqrt(jnp.mean(x * x, axis=-1, keepdims=True) + EPS)
    return r, x * r


def _rms_back(xh, r, g, dh):
    u = dh * g
    return r * (u - xh * jnp.mean(u * xh, axis=-1, keepdims=True))


def rms_fwd(x, g, out_dtype, name="rms_fwd", after=None):
    rows, d = x.shape
    tr = _pick(rows, (ROW_TILE, 128))
    tied = [] if after is None else [after]

    def body(x_ref, g_ref, *rest):
        o_ref = rest[-1]
        _, xh = _rms_stats(x_ref[...])
        o_ref[...] = (xh * g_ref[...]).astype(o_ref.dtype)

    return pl.pallas_call(
        body, out_shape=jax.ShapeDtypeStruct((rows, d), out_dtype), grid=(rows // tr,),
        in_specs=[pl.BlockSpec((tr, d), lambda i: (i, 0)), pl.BlockSpec((1, d), lambda i: (0, 0))] + [ANY] * len(tied),
        out_specs=pl.BlockSpec((tr, d), lambda i: (i, 0)),
        compiler_params=_cp("parallel"), name=name)(x, g.reshape(1, d), *tied)


def rms_bwd(x, g, dh, dres=None, name="rms_bwd"):
    rows, d = x.shape
    tr = _pick(rows, (ROW_TILE, 128))
    has_res = dres is not None

    def body(*refs):
        if has_res:
            x_ref, g_ref, dh_ref, dres_ref, dx_ref, dg_ref = refs
        else:
            x_ref, g_ref, dh_ref, dx_ref, dg_ref = refs
        r, xh = _rms_stats(x_ref[...])
        dh_ = dh_ref[...].astype(F32)
        part = jnp.sum(dh_ * xh, axis=0, keepdims=True)

        @pl.when(pl.program_id(0) == 0)
        def _():
            dg_ref[...] = part

        @pl.when(pl.program_id(0) > 0)
        def _():
            dg_ref[...] += part

        dx = _rms_back(xh, r, g_ref[...], dh_)
        if has_res:
            dx = dx + dres_ref[...]
        dx_ref[...] = dx

    row_spec = pl.BlockSpec((tr, d), lambda i: (i, 0))
    vec_spec = pl.BlockSpec((1, d), lambda i: (0, 0))
    args = [x, g.reshape(1, d), dh] + ([dres] if has_res else [])
    return pl.pallas_call(
        body, out_shape=(jax.ShapeDtypeStruct((rows, d), F32), jax.ShapeDtypeStruct((1, d), F32)),
        grid=(rows // tr,), in_specs=[row_spec, vec_spec, row_spec] + ([row_spec] if has_res else []),
        out_specs=(row_spec, vec_spec), compiler_params=_cp("arbitrary"), name=name)(*args)


def final_loss(x, g, target, name="final_loss"):
    rows, d = x.shape
    tr = _pick(rows, (ROW_TILE, 128))

    def body(x_ref, g_ref, t_ref, loss_ref, dx_ref, dg_ref):
        r, xh = _rms_stats(x_ref[...])
        gain = g_ref[...]
        err = xh * gain - t_ref[...]
        sq = jnp.sum(jnp.sum(err * err, axis=1, keepdims=True), axis=0, keepdims=True) * (0.5 / d)
        dy = err * (1.0 / d)
        part = jnp.sum(dy * xh, axis=0, keepdims=True)

        @pl.when(pl.program_id(0) == 0)
        def _():
            dg_ref[...] = part
            loss_ref[...] = jnp.broadcast_to(sq, loss_ref.shape)

        @pl.when(pl.program_id(0) > 0)
        def _():
            dg_ref[...] += part
            loss_ref[...] += jnp.broadcast_to(sq, loss_ref.shape)

        dx_ref[...] = _rms_back(xh, r, gain, dy)

    row_spec = pl.BlockSpec((tr, d), lambda i: (i, 0))
    vec_spec = pl.BlockSpec((1, d), lambda i: (0, 0))
    return pl.pallas_call(
        body, out_shape=(jax.ShapeDtypeStruct((1, LANES), F32), jax.ShapeDtypeStruct((rows, d), F32),
                         jax.ShapeDtypeStruct((1, d), F32)),
        grid=(rows // tr,), in_specs=[row_spec, vec_spec, row_spec],
        out_specs=(pl.BlockSpec((1, LANES), lambda i: (0, 0)), row_spec, vec_spec),
        compiler_params=_cp("arbitrary"), name=name)(x, g.reshape(1, d), target)


def _alibi_slopes():
    return (2.0 ** (-8.0 * (np.arange(12) + 1) / 12)).astype(np.float32)


def _band_scores(q, k, q0, start, wk, slope):
    s = lax.dot_general(q, k, (((1,), (1,)), ((), ())), preferred_element_type=F32) * (HEAD ** -0.5)
    qpos = q0 + lax.broadcasted_iota(jnp.int32, (QBLK, wk), 0)
    kpos = start + lax.broadcasted_iota(jnp.int32, (QBLK, wk), 1)
    rel = jnp.abs(qpos - kpos)
    return jnp.where(rel <= N_SIDE, s - slope * rel.astype(F32), NEG)


def _attn_geometry(seq, dilation):
    length = seq // dilation
    return length, length // QBLK, min(2 * QBLK, length)


def _attn_window(n, length, wk):
    q0 = pl.multiple_of(n * QBLK, QBLK)
    start = pl.multiple_of(jnp.clip(n * QBLK - N_SIDE, 0, length - wk), N_SIDE)
    return q0, start


def attn_fwd(proj, group, name):
    seq = proj.shape[0]
    dilation = A_PATTERNS[group][1]
    length, nblk, wk = _attn_geometry(seq, dilation)
    cols = A_IN // HEAD
    pv = proj.reshape(length, dilation * A_IN)

    def body(slope_ref, q_ref, k_ref, v_ref, o_ref, lse_ref):
        slope = slope_ref[group * 4 + pl.program_id(1)] * float(dilation)

        def blk(n, carry):
            q0, start = _attn_window(n, length, wk)
            q = q_ref[pl.ds(q0, QBLK), :].astype(BF16)
            k = k_ref[pl.ds(start, wk), :].astype(BF16)
            v = v_ref[pl.ds(start, wk), :].astype(BF16)
            s = _band_scores(q, k, q0, start, wk, slope)
            m = jnp.max(s, axis=-1, keepdims=True)
            p = jnp.exp(s - m)
            l = jnp.sum(p, axis=-1, keepdims=True)
            o = jnp.dot(p.astype(BF16), v, preferred_element_type=F32) / l
            o_ref[pl.ds(q0, QBLK), :] = o
            lse_ref[pl.ds(q0, QBLK), :] = jnp.broadcast_to(m + jnp.log(l), (QBLK, HEAD))
            return carry

        lax.fori_loop(0, nblk, blk, 0)

    def part(p):
        return pl.BlockSpec((length, HEAD), lambda r, h: (0, r * cols + p * 12 + group * 4 + h))

    out_spec = pl.BlockSpec((length, HEAD), lambda r, h: (0, r * 4 + h))
    o, lse = pl.pallas_call(
        body, out_shape=(jax.ShapeDtypeStruct((length, dilation * A_OUT_W), F32),) * 2,
        grid=(dilation, 4),
        in_specs=[pl.BlockSpec(memory_space=pltpu.SMEM), part(0), part(1), part(2)],
        out_specs=(out_spec, out_spec), compiler_params=_cp("parallel", "parallel"), name=name,
    )(jnp.asarray(_alibi_slopes()), pv, pv, pv)
    return o.reshape(seq, A_OUT_W), lse.reshape(seq, A_OUT_W)


def attn_combine(os_, lses, name="attn_combine"):
    seq = os_[0].shape[0]
    tr = ROW_TILE

    def body(o0, o1, o2, l0, l1, l2, c_ref, lse_ref):
        a, b, c = l0[...], l1[...], l2[...]
        m = jnp.maximum(jnp.maximum(a, b), c)
        ea, eb, ec = jnp.exp(a - m), jnp.exp(b - m), jnp.exp(c - m)
        den = ea + eb + ec
        c_ref[...] = (ea * o0[...] + eb * o1[...] + ec * o2[...]) / den
        lse_ref[...] = m + jnp.log(den)

    spec = pl.BlockSpec((tr, A_OUT_W), lambda i: (i, 0))
    return pl.pallas_call(
        body, out_shape=(jax.ShapeDtypeStruct((seq, A_OUT_W), F32),) * 2, grid=(seq // tr,),
        in_specs=[spec] * 6, out_specs=(spec, spec), compiler_params=_cp("parallel"), name=name)(*os_, *lses)


def attn_bwd(proj, dcat, comb, lse, group, name):
    seq = proj.shape[0]
    dilation = A_PATTERNS[group][1]
    length, nblk, wk = _attn_geometry(seq, dilation)
    cols = A_IN // HEAD
    pv = proj.reshape(length, dilation * A_IN)
    view = lambda a: a.reshape(length, dilation * A_OUT_W)
    scale = HEAD ** -0.5

    def body(slope_ref, q_ref, k_ref, v_ref, do_ref, c_ref, lse_ref, dq_ref, dk_ref, dv_ref, dk_acc, dv_acc):
        slope = slope_ref[group * 4 + pl.program_id(1)] * float(dilation)
        dk_acc[...] = jnp.zeros_like(dk_acc)
        dv_acc[...] = jnp.zeros_like(dv_acc)

        def blk(n, carry):
            q0, start = _attn_window(n, length, wk)
            rows = pl.ds(q0, QBLK)
            keys = pl.ds(start, wk)
            q = q_ref[rows, :].astype(BF16)
            k = k_ref[keys, :].astype(BF16)
            v = v_ref[keys, :].astype(BF16)
            do = do_ref[rows, :]
            s = _band_scores(q, k, q0, start, wk, slope)
            p = jnp.exp(s - lse_ref[rows, :][:, :1])
            delta = jnp.sum(do * c_ref[rows, :], axis=-1, keepdims=True)
            do16 = do.astype(BF16)
            dp = lax.dot_general(do16, v, (((1,), (1,)), ((), ())), preferred_element_type=F32)
            ds = (p * (dp - delta) * scale).astype(BF16)
            p16 = p.astype(BF16)
            dq_ref[rows, :] = jnp.dot(ds, k, preferred_element_type=F32).astype(dq_ref.dtype)
            dk_acc[keys, :] += lax.dot_general(ds, q, (((0,), (0,)), ((), ())), preferred_element_type=F32)
            dv_acc[keys, :] += lax.dot_general(p16, do16, (((0,), (0,)), ((), ())), preferred_element_type=F32)
            return carry

        lax.fori_loop(0, nblk, blk, 0)
        dk_ref[...] = dk_acc[...].astype(dk_ref.dtype)
        dv_ref[...] = dv_acc[...].astype(dv_ref.dtype)

    def part(p):
        return pl.BlockSpec((length, HEAD), lambda r, h: (0, r * cols + p * 12 + group * 4 + h))

    hs = pl.BlockSpec((length, HEAD), lambda r, h: (0, r * 4 + h))
    do_cols = dcat.shape[1] // HEAD
    do_spec = pl.BlockSpec((length, HEAD), lambda r, h: (0, r * do_cols + h))
    outs = pl.pallas_call(
        body, out_shape=(jax.ShapeDtypeStruct((length, dilation * A_OUT_W), BF16),) * 3,
        grid=(dilation, 4),
        in_specs=[pl.BlockSpec(memory_space=pltpu.SMEM), part(0), part(1), part(2), do_spec, hs, hs],
        out_specs=(hs, hs, hs),
        scratch_shapes=[pltpu.VMEM((length, HEAD), F32), pltpu.VMEM((length, HEAD), F32)],
        compiler_params=_cp("parallel", "parallel"), name=name,
    )(jnp.asarray(_alibi_slopes()), pv, pv, pv, dcat.reshape(length, dilation * dcat.shape[1]), view(comb), view(lse))
    return tuple(o.reshape(seq, A_OUT_W) for o in outs)


MEM_ROW_TILE = 512


def _mem_probs(q, k):
    s = lax.dot_general(q, k, (((1,), (1,)), ((), ())), preferred_element_type=F32) * (HEAD ** -0.5)
    p = jnp.exp(s - jnp.max(s, axis=-1, keepdims=True))
    return p / jnp.sum(p, axis=-1, keepdims=True)


def mem_fwd(proj, q_col, kv, name="mem_fwd"):
    seq = proj.shape[0]
    qb = q_col // HEAD

    def body(q_ref, k_ref, v_ref, o_ref):
        p = _mem_probs(q_ref[...].astype(BF16), k_ref[...].astype(BF16))
        o_ref[...] = jnp.dot(p.astype(BF16), v_ref[...].astype(BF16), preferred_element_type=F32).astype(o_ref.dtype)

    return pl.pallas_call(
        body, out_shape=jax.ShapeDtypeStruct((seq, MEM_W), BF16), grid=(MEM_HEADS, seq // MEM_ROW_TILE),
        in_specs=[pl.BlockSpec((MEM_ROW_TILE, HEAD), lambda h, i: (i, qb + h)),
                  pl.BlockSpec((MEM_LEN, HEAD), lambda h, i: (0, h)),
                  pl.BlockSpec((MEM_LEN, HEAD), lambda h, i: (0, MEM_HEADS + h))],
        out_specs=pl.BlockSpec((MEM_ROW_TILE, HEAD), lambda h, i: (i, h)),
        compiler_params=_cp("parallel", "parallel"), name=name)(proj, kv, kv)


def mem_bwd(proj, q_col, kv, dcat, do_col, name="mem_bwd"):
    seq = proj.shape[0]
    qb, ob = q_col // HEAD, do_col // HEAD
    scale = HEAD ** -0.5

    def body(q_ref, k_ref, v_ref, do_ref, dq_ref, dk_ref, dv_ref):
        q = q_ref[...].astype(BF16)
        k = k_ref[...].astype(BF16)
        v = v_ref[...].astype(BF16)
        do = do_ref[...].astype(BF16)
        p = _mem_probs(q, k)
        dp = lax.dot_general(do, v, (((1,), (1,)), ((), ())), preferred_element_type=F32)
        ds = (p * (dp - jnp.sum(dp * p, axis=-1, keepdims=True)) * scale).astype(BF16)
        dq_ref[...] = jnp.dot(ds, k, preferred_element_type=F32).astype(dq_ref.dtype)
        dk = lax.dot_general(ds, q, (((0,), (0,)), ((), ())), preferred_element_type=F32)
        dv = lax.dot_general(p.astype(BF16), do, (((0,), (0,)), ((), ())), preferred_element_type=F32)

        @pl.when(pl.program_id(1) == 0)
        def _():
            dk_ref[...] = dk
            dv_ref[...] = dv

        @pl.when(pl.program_id(1) > 0)
        def _():
            dk_ref[...] += dk
            dv_ref[...] += dv

    dq, dk, dv = pl.pallas_call(
        body, out_shape=(jax.ShapeDtypeStruct((seq, MEM_W), BF16), jax.ShapeDtypeStruct((MEM_LEN, MEM_W), F32),
                         jax.ShapeDtypeStruct((MEM_LEN, MEM_W), F32)),
        grid=(MEM_HEADS, seq // MEM_ROW_TILE),
        in_specs=[pl.BlockSpec((MEM_ROW_TILE, HEAD), lambda h, i: (i, qb + h)),
                  pl.BlockSpec((MEM_LEN, HEAD), lambda h, i: (0, h)),
                  pl.BlockSpec((MEM_LEN, HEAD), lambda h, i: (0, MEM_HEADS + h)),
                  pl.BlockSpec((MEM_ROW_TILE, HEAD), lambda h, i: (i, ob + h))],
        out_specs=(pl.BlockSpec((MEM_ROW_TILE, HEAD), lambda h, i: (i, h)),
                   pl.BlockSpec((MEM_LEN, HEAD), lambda h, i: (0, h)),
                   pl.BlockSpec((MEM_LEN, HEAD), lambda h, i: (0, h))),
        compiler_params=_cp("parallel", "arbitrary"), name=name)(proj, kv, kv, dcat)
    return dq, jnp.concatenate([dk, dv], axis=1)


def _sgu_front(x, gain):
    uv, duv = _gelu_parts(x)
    u, v = uv[:, :B_W], uv[:, B_W:]
    r, vh = _rms_stats(v)
    return u, duv, r, vh, vh * gain


def sgu_fwd(proj, gain, w_s, bias_b, name="sgu_fwd"):
    seq = proj.shape[0]

    def body(x_ref, gain_ref, ws_ref, bias_ref, o_ref):
        u, _, _, _, vn = _sgu_front(x_ref[...], gain_ref[...])
        for g in range(B_GROUPS):
            cs = slice(g * CHUNK, (g + 1) * CHUNK)
            mixed = jnp.dot(ws_ref[g].astype(BF16), vn[:, cs].astype(BF16), preferred_element_type=F32) + bias_ref[g]
            o_ref[:, cs] = (u[:, cs] * mixed).astype(o_ref.dtype)

    full = lambda shape: pl.BlockSpec(shape, lambda c: (0,) * len(shape))
    return pl.pallas_call(
        body, out_shape=jax.ShapeDtypeStruct((seq, B_W), BF16), grid=(seq // CHUNK,),
        in_specs=[pl.BlockSpec((CHUNK, 2 * B_W), lambda c: (c, 0)), full((1, B_W)),
                  full((B_GROUPS, CHUNK, CHUNK)), full((B_GROUPS, CHUNK, CHUNK))],
        out_specs=pl.BlockSpec((CHUNK, B_W), lambda c: (c, 0)),
        compiler_params=_cp("parallel"), name=name)(proj, gain.reshape(1, B_W), w_s, bias_b)


def sgu_bwd(proj, gain, w_s, w_s_t, bias_b, dcat, name="sgu_bwd"):
    seq = proj.shape[0]

    def body(x_ref, gain_ref, ws_ref, wst_ref, bias_ref, do_ref, dx_ref, dws_ref, dmix_ref, dgain_ref, dvn_ref):
        first = pl.program_id(0) == 0
        gain = gain_ref[...]
        u, duv, r, vh, vn = _sgu_front(x_ref[...], gain)
        do = do_ref[...]
        for g in range(B_GROUPS):
            cs = slice(g * CHUNK, (g + 1) * CHUNK)
            vg = vn[:, cs].astype(BF16)
            mixed = jnp.dot(ws_ref[g].astype(BF16), vg, preferred_element_type=F32) + bias_ref[g]
            dx_ref[:, cs] = (do[:, cs] * mixed * duv[:, cs]).astype(dx_ref.dtype)
            dmixed = do[:, cs] * u[:, cs]
            dm16 = dmixed.astype(BF16)
            dws = lax.dot_general(dm16, vg, (((1,), (1,)), ((), ())), preferred_element_type=F32)
            dvn_ref[:, cs] = jnp.dot(wst_ref[g].astype(BF16), dm16, preferred_element_type=F32)

            @pl.when(first)
            def _():
                dws_ref[g] = dws
                dmix_ref[g] = dmixed

            @pl.when(jnp.logical_not(first))
            def _():
                dws_ref[g] += dws
                dmix_ref[g] += dmixed

        dvn = dvn_ref[...]
        dgain = jnp.sum(dvn * vh, axis=0, keepdims=True)

        @pl.when(first)
        def _():
            dgain_ref[...] = dgain

        @pl.when(jnp.logical_not(first))
        def _():
            dgain_ref[...] += dgain

        dv = _rms_back(vh, r, gain, dvn)
        dx_ref[:, B_W:] = (dv * duv[:, B_W:]).astype(dx_ref.dtype)

    full = lambda shape: pl.BlockSpec(shape, lambda c: (0,) * len(shape))
    mats = full((B_GROUPS, CHUNK, CHUNK))
    return pl.pallas_call(
        body, out_shape=(jax.ShapeDtypeStruct((seq, 2 * B_W), BF16), jax.ShapeDtypeStruct((B_GROUPS, CHUNK, CHUNK), F32),
                         jax.ShapeDtypeStruct((B_GROUPS, CHUNK, CHUNK), F32), jax.ShapeDtypeStruct((1, B_W), F32)),
        grid=(seq // CHUNK,),
        in_specs=[pl.BlockSpec((CHUNK, 2 * B_W), lambda c: (c, 0)), full((1, B_W)), mats, mats, mats,
                  pl.BlockSpec((CHUNK, B_W), lambda c: (c, 0))],
        out_specs=(pl.BlockSpec((CHUNK, 2 * B_W), lambda c: (c, 0)), mats, mats, full((1, B_W))),
        scratch_shapes=[pltpu.VMEM((CHUNK, B_W), F32)],
        compiler_params=_cp("arbitrary"), name=name)(proj, gain.reshape(1, B_W), w_s, w_s_t, bias_b, dcat)


FFN_COLS = 128
FFN_ROWS = 32
SUBLANES = 8


def _window(ref, r0, first, last):
    cols = ref.shape[1]
    pad = jnp.zeros((SUBLANES, cols), F32)
    if first:
        return jnp.concatenate([pad, ref[pl.ds(0, FFN_ROWS + SUBLANES), :]], axis=0)
    if last:
        return jnp.concatenate([ref[pl.ds(r0 - SUBLANES, FFN_ROWS + SUBLANES), :], pad], axis=0)
    return ref[pl.ds(pl.multiple_of(r0 - SUBLANES, SUBLANES), FFN_ROWS + 2 * SUBLANES), :]


def _taps(win):
    mid = slice(SUBLANES, SUBLANES + FFN_ROWS)
    return pltpu.roll(win, 1, 0)[mid], win[mid], pltpu.roll(win, win.shape[0] - 1, 0)[mid]


def _row_steps(seq, step, carry):
    n = seq // FFN_ROWS
    carry = step(0, True, False, carry)
    carry = lax.fori_loop(1, n - 1, lambda i, c: step(pl.multiple_of(i * FFN_ROWS, FFN_ROWS), False, False, c), carry)
    return step(seq - FFN_ROWS, False, True, carry)


def _conv3(taps, w, b):
    prev, cur, nxt = taps
    return prev * w[0:1] + cur * w[1:2] + nxt * w[2:3] + b


def _fold(x):
    return jnp.sum(x.reshape(FFN_ROWS // SUBLANES, SUBLANES, x.shape[1]), axis=0)


def ffn_act_fwd(a, conv_w, conv_b, name="ffn_act_fwd"):
    seq = a.shape[0]
    nb = FF // FFN_COLS

    def body(ag_ref, av_ref, wg_ref, wv_ref, bg_ref, bv_ref, o_ref):
        wg, wv, bg, bv = wg_ref[...], wv_ref[...], bg_ref[...], bv_ref[...]

        def step(r0, first, last, carry):
            gate = _conv3(_taps(_window(ag_ref, r0, first, last)), wg, bg)
            val = _conv3(_taps(_window(av_ref, r0, first, last)), wv, bv)
            o_ref[pl.ds(r0, FFN_ROWS), :] = (_gelu(gate) * val).astype(o_ref.dtype)
            return carry

        _row_steps(seq, step, 0)

    col = lambda rows, off: pl.BlockSpec((rows, FFN_COLS), lambda j: (0, j + off))
    cb = conv_b.reshape(1, 2 * FF)
    return pl.pallas_call(
        body, out_shape=jax.ShapeDtypeStruct((seq, FF), BF16), grid=(nb,),
        in_specs=[col(seq, 0), col(seq, nb), col(3, 0), col(3, nb), col(1, 0), col(1, nb)],
        out_specs=col(seq, 0), compiler_params=_cp("parallel"), name=name)(a, a, conv_w, conv_w, cb, cb)


def ffn_act_bwd(a, conv_w, conv_b, dact, name="ffn_act_bwd"):
    seq = a.shape[0]
    nb = FF // FFN_COLS

    def body(ag_ref, av_ref, wg_ref, wv_ref, bg_ref, bv_ref, d_ref, dag_ref, dav_ref, dwg_ref, dwv_ref, dbg_ref, dbv_ref,
             dcg_ref, dcv_ref):
        wg, wv, bg, bv = wg_ref[...], wv_ref[...], bg_ref[...], bv_ref[...]

        def conv_grads(r0, first, last, sums):
            g_taps = _taps(_window(ag_ref, r0, first, last))
            v_taps = _taps(_window(av_ref, r0, first, last))
            act, dact_dgate = _gelu_parts(_conv3(g_taps, wg, bg))
            d = d_ref[pl.ds(r0, FFN_ROWS), :].astype(F32)
            dcg = d * _conv3(v_taps, wv, bv) * dact_dgate
            dcv = d * act
            dcg_ref[pl.ds(r0, FFN_ROWS), :] = dcg
            dcv_ref[pl.ds(r0, FFN_ROWS), :] = dcv
            new = [_fold(dcg)] + [_fold(dcg * t) for t in g_taps] + [_fold(dcv)] + [_fold(dcv * t) for t in v_taps]
            return tuple(s + n for s, n in zip(sums, new))

        zero = jnp.zeros((SUBLANES, FFN_COLS), F32)
        sums = _row_steps(seq, conv_grads, (zero,) * 8)
        total = [jnp.sum(s, axis=0, keepdims=True) for s in sums]
        dbg_ref[...] = total[0]
        dbv_ref[...] = total[4]
        for k in range(3):
            dwg_ref[k:k + 1, :] = total[1 + k]
            dwv_ref[k:k + 1, :] = total[5 + k]

        def conv_transpose(r0, first, last, carry):
            for dc_ref, w, da_ref in ((dcg_ref, wg, dag_ref), (dcv_ref, wv, dav_ref)):
                prev, cur, nxt = _taps(_window(dc_ref, r0, first, last))
                da_ref[pl.ds(r0, FFN_ROWS), :] = (nxt * w[0:1] + cur * w[1:2] + prev * w[2:3]).astype(da_ref.dtype)
            return carry

        _row_steps(seq, conv_transpose, 0)

    col = lambda rows, off: pl.BlockSpec((rows, FFN_COLS), lambda j: (0, j + off))
    cb = conv_b.reshape(1, 2 * FF)
    dag, dav, dwg, dwv, dbg, dbv = pl.pallas_call(
        body, out_shape=(jax.ShapeDtypeStruct((seq, FF), BF16),) * 2 + (jax.ShapeDtypeStruct((3, FF), F32),) * 2
        + (jax.ShapeDtypeStruct((1, FF), F32),) * 2, grid=(nb,),
        in_specs=[col(seq, 0), col(seq, nb), col(3, 0), col(3, nb), col(1, 0), col(1, nb), col(seq, 0)],
        out_specs=(col(seq, 0), col(seq, 0), col(3, 0), col(3, 0), col(1, 0), col(1, 0)),
        scratch_shapes=[pltpu.VMEM((seq, FFN_COLS), F32), pltpu.VMEM((seq, FFN_COLS), F32)],
        compiler_params=_cp("parallel"), name=name)(a, a, conv_w, conv_w, cb, cb, dact)
    cat = lambda p, q: jnp.concatenate([p, q], axis=1)
    return cat(dag, dav), cat(dwg, dwv), cat(dbg, dbv)


def _adam_math(w, g, m, v):
    m = ADAM_B1 * m + (1.0 - ADAM_B1) * g
    v = ADAM_B2 * v + (1.0 - ADAM_B2) * (g * g)
    m_hat = m / (1.0 - ADAM_B1 ** ADAM_STEP)
    v_hat = v / (1.0 - ADAM_B2 ** ADAM_STEP)
    return -ADAM_LR * (m_hat / (jnp.sqrt(v_hat) + ADAM_EPS) + ADAM_WD * w), m, v


def _row_tile(rows, cols):
    return _pick(rows, (256, 128, 64)) if cols <= 1024 else _pick(rows, (128, 64))


def adamw_layer(w_all, m_all, v_all, layer, g, prev, name):
    n, rows, cols = w_all.shape
    tr = _row_tile(rows, cols)

    def body(w_ref, m_ref, v_ref, g_ref, *rest):
        go_ref, d_ref, mo_ref, vo_ref = rest[-4:]
        g_ = g_ref[...]
        d, m_, v_ = _adam_math(w_ref[...], g_, m_ref[...], v_ref[...])
        go_ref[...] = g_
        d_ref[...] = d
        mo_ref[...] = m_
        vo_ref[...] = v_

    lay = pl.BlockSpec((None, tr, cols), lambda i: (layer, i, 0))
    in_specs = [lay, lay, lay, pl.BlockSpec((tr, cols), lambda i: (i, 0))]
    args = [w_all, m_all, v_all, g]
    aliases = {}
    if prev is not None:
        in_specs += [pl.BlockSpec(memory_space=pl.ANY)] * 4
        args += list(prev)
        aliases = {4 + k: k for k in range(4)}
    return pl.pallas_call(
        body, out_shape=(jax.ShapeDtypeStruct(w_all.shape, F32),) * 4, grid=(rows // tr,),
        in_specs=in_specs, out_specs=(lay,) * 4, input_output_aliases=aliases,
        compiler_params=_cp("parallel"), name=name)(*args)


def adamw_flat(w, g, m, v, name="adamw_small"):
    rows, cols = w.shape
    tr = _pick(rows, (128, 8))

    def body(w_ref, g_ref, m_ref, v_ref, d_ref, mo_ref, vo_ref):
        d_ref[...], mo_ref[...], vo_ref[...] = _adam_math(w_ref[...], g_ref[...], m_ref[...], v_ref[...])

    spec = pl.BlockSpec((tr, cols), lambda i: (i, 0))
    return pl.pallas_call(
        body, out_shape=(jax.ShapeDtypeStruct(w.shape, F32),) * 3, grid=(rows // tr,),
        in_specs=[spec] * 4, out_specs=(spec,) * 3, compiler_params=_cp("parallel"), name=name)(w, g, m, v)


def pair_sum(dw, got, core, name):
    _, rows, cols = dw.shape
    half = rows // 2
    tr = _row_tile(half, cols)
    nrb = half // tr

    def body(c_ref, a_ref, b_ref, o_ref):
        o_ref[...] = (a_ref[...].astype(F32) + b_ref[...].astype(F32)).astype(o_ref.dtype)

    return pl.pallas_call(
        body, out_shape=jax.ShapeDtypeStruct((N_CHIPS, half, cols), BF16),
        grid_spec=pltpu.PrefetchScalarGridSpec(
            num_scalar_prefetch=1, grid=(N_CHIPS, nrb),
            in_specs=[pl.BlockSpec((None, tr, cols), lambda s, i, c_ref: (s, c_ref[0] * nrb + i, 0)),
                      pl.BlockSpec((None, tr, cols), lambda s, i, c_ref: (s, i, 0))],
            out_specs=pl.BlockSpec((None, tr, cols), lambda s, i, c_ref: (s, i, 0))),
        compiler_params=_cp("parallel", "parallel"), name=name)(core, dw, got)


def chip_sum(own, parts, place, name):
    _, half, cols = parts.shape
    tr = _row_tile(half, cols)
    nrb = half // tr

    def body(p_ref, own_ref, a_ref, b_ref, c_ref, o_ref):
        o_ref[...] = ((own_ref[...].astype(F32) + a_ref[...].astype(F32)) + b_ref[...].astype(F32)) + c_ref[...].astype(F32)

    def slot(k):
        return pl.BlockSpec((None, tr, cols), lambda i, p: (jnp.bitwise_xor(p[0], k), i, 0))

    return pl.pallas_call(
        body, out_shape=jax.ShapeDtypeStruct((2 * half, cols), F32),
        grid_spec=pltpu.PrefetchScalarGridSpec(
            num_scalar_prefetch=1, grid=(nrb,), in_specs=[slot(0), slot(1), slot(2), slot(3)],
            out_specs=pl.BlockSpec((tr, cols), lambda i, p: (p[1] * nrb + i, 0))),
        compiler_params=_cp("parallel"), name=name)(place, own, parts, parts, parts)


def cast_to_slot(w_all, layer, place, name, after=None):
    _, rows, cols = w_all.shape
    tr = _row_tile(rows, cols)
    tied = [] if after is None else [after]

    def body(p_ref, w_ref, *rest):
        o_ref = rest[-1]
        o_ref[...] = w_ref[...].astype(o_ref.dtype)

    return pl.pallas_call(
        body, out_shape=jax.ShapeDtypeStruct((N_CHIPS, rows, cols), BF16),
        grid_spec=pltpu.PrefetchScalarGridSpec(
            num_scalar_prefetch=1, grid=(rows // tr,),
            in_specs=[pl.BlockSpec((None, tr, cols), lambda i, p: (layer, i, 0))] + [ANY] * len(tied),
            out_specs=pl.BlockSpec((None, tr, cols), lambda i, p: (p[0], i, 0))),
        compiler_params=_cp("parallel"), name=name)(place, w_all, *tied)


ANY = pl.BlockSpec(memory_space=pl.ANY)


def _place():
    x, y, c = lax.axis_index("x"), lax.axis_index("y"), lax.axis_index("c")
    others = [(1 - x, y), (x, 1 - y), (1 - x, 1 - y)]
    return x, y, c, 2 * x + y, others


def _remote(src, dst, send_sem, recv_sem, dev):
    return pltpu.make_async_remote_copy(src_ref=src, dst_ref=dst, send_sem=send_sem, recv_sem=recv_sem,
                                        device_id=dev, device_id_type=MESH)


HBM = pl.BlockSpec(memory_space=pltpu.HBM)
SEM = pl.BlockSpec(memory_space=pltpu.SEMAPHORE)
EFFECT = pltpu.SideEffectType.DATAFLOW_SIDE_EFFECTING
TOKEN = jax.ShapeDtypeStruct((8, LANES), F32)


def _in_hbm(a):
    return pltpu.with_memory_space_constraint(a, pltpu.HBM)


def _gather_copies(bufs, send_sems, recv_sems):
    x, y, c, me, others = _place()
    out = []
    for w, buf in enumerate(bufs):
        half = buf.shape[1] // 2
        mine = pl.ds(c * half, half)
        for k, (ox, oy) in enumerate(others):
            sems = send_sems.at[3 * w + k], recv_sems.at[3 * w + k]
            out.append((_remote(buf.at[me, mine], buf.at[me, mine], *sems, (ox, oy, c)),
                        _remote(buf.at[me, mine], buf.at[2 * ox + oy, mine], *sems, (ox, oy, c))))
    return out


def gather_start(bufs, name):
    n = len(bufs)

    def body(*refs):
        ins, (send_sems, recv_sems), token = refs[:n], refs[n:n + 2], refs[-1]
        for sent, _ in _gather_copies(ins, send_sems, recv_sems):
            sent.start()
        token[...] = jnp.zeros_like(token)

    outs = pl.pallas_call(
        body, name=name,
        out_shape=(pltpu.SemaphoreType.DMA((3 * n,)), pltpu.SemaphoreType.DMA((3 * n,)),
                   *[pltpu.HBM(b.shape, b.dtype) for b in bufs], TOKEN),
        in_specs=[HBM] * n, out_specs=(SEM, SEM, *[HBM] * n, VM),
        input_output_aliases={w: 2 + w for w in range(n)},
        compiler_params=pltpu.CompilerParams(has_side_effects=EFFECT))(*[_in_hbm(b) for b in bufs])
    return outs[0], outs[1], list(outs[2:2 + n]), outs[-1]


def gather_wait(send_sems, recv_sems, bufs, after, name):
    n = len(bufs)

    def body(*refs):
        ins, (send_ref, recv_ref) = refs[:n], refs[n:n + 2]
        for sent, landed in _gather_copies(ins, send_ref, recv_ref):
            sent.wait_send()
            landed.wait_recv()

    return list(pl.pallas_call(
        body, name=name, out_shape=tuple(pltpu.HBM(b.shape, b.dtype) for b in bufs),
        in_specs=[HBM] * n + [SEM, SEM, ANY], out_specs=(HBM,) * n,
        input_output_aliases={w: w for w in range(n)},
        compiler_params=pltpu.CompilerParams(has_side_effects=EFFECT))(*bufs, send_sems, recv_sems, after))


def forward_to_sibling(bufs, name):
    n = len(bufs)

    def body(*refs):
        ins, outs = refs[:n], refs[n:2 * n]
        send_sems, recv_sems = refs[2 * n:]
        x, y, c, me, others = _place()
        sends = []
        for w in range(n):
            half = ins[w].shape[1] // 2
            mine = pl.ds(c * half, half)
            for k, (ox, oy) in enumerate(others):
                cp = _remote(ins[w].at[2 * ox + oy, mine], outs[w].at[2 * ox + oy, mine], send_sems.at[3 * w + k],
                             recv_sems.at[3 * w + k], (x, y, 1 - c))
                cp.start()
                sends.append(cp)
        for w in range(n):
            half = ins[w].shape[1] // 2
            theirs = pl.ds((1 - c) * half, half)
            for k, (ox, oy) in enumerate(others):
                passed = outs[w].at[2 * ox + oy, theirs]
                _remote(passed, passed, send_sems.at[3 * w + k], recv_sems.at[3 * w + k], (x, y, 1 - c)).wait_recv()
        for cp in sends:
            cp.wait_send()

    return list(pl.pallas_call(
        body, out_shape=tuple(jax.ShapeDtypeStruct(b.shape, b.dtype) for b in bufs),
        in_specs=[ANY] * n, out_specs=(ANY,) * n, input_output_aliases={w: w for w in range(n)},
        scratch_shapes=[pltpu.SemaphoreType.DMA((3 * n,)), pltpu.SemaphoreType.DMA((3 * n,))],
        name=name)(*bufs))


def swap_halves(grads, name):
    n = len(grads)

    def body(*refs):
        ins, outs = refs[:n], refs[n:2 * n]
        send_sems, recv_sems = refs[2 * n:]
        x, y, c, _, _ = _place()
        copies = []
        for w in range(n):
            half = ins[w].shape[1] // 2
            cp = _remote(ins[w].at[:, pl.ds((1 - c) * half, half)], outs[w], send_sems.at[w], recv_sems.at[w], (x, y, 1 - c))
            cp.start()
            copies.append(cp)
        for cp in copies:
            cp.wait()

    return pl.pallas_call(
        body, out_shape=tuple(jax.ShapeDtypeStruct((N_CHIPS, g.shape[1] // 2, g.shape[2]), g.dtype) for g in grads),
        in_specs=[ANY] * n, out_specs=(ANY,) * n,
        scratch_shapes=[pltpu.SemaphoreType.DMA((n,)), pltpu.SemaphoreType.DMA((n,))], name=name)(*grads)


def _swap_copies(grads, lands, send_sems, recv_sems):
    x, y, c, _, _ = _place()
    out = []
    for w, (g, land) in enumerate(zip(grads, lands)):
        half = g.shape[1] // 2
        out.append(_remote(g.at[:, pl.ds((1 - c) * half, half)], land, send_sems.at[w], recv_sems.at[w], (x, y, 1 - c)))
    return out


def swap_start(grads, name):
    n = len(grads)

    def body(*refs):
        ins, lands, (send_sems, recv_sems), token = refs[:n], refs[n:2 * n], refs[2 * n:2 * n + 2], refs[-1]
        for cp in _swap_copies(ins, lands, send_sems, recv_sems):
            cp.start()
        token[...] = jnp.zeros_like(token)

    shapes = [(N_CHIPS, g.shape[1] // 2, g.shape[2]) for g in grads]
    zones = [_in_hbm(lax.empty(s, g.dtype)) for s, g in zip(shapes, grads)]
    outs = pl.pallas_call(
        body, name=name,
        out_shape=(pltpu.SemaphoreType.DMA((n,)), pltpu.SemaphoreType.DMA((n,)),
                   *[pltpu.HBM(g.shape, g.dtype) for g in grads], *[pltpu.HBM(s, g.dtype) for s, g in zip(shapes, grads)],
                   TOKEN),
        in_specs=[HBM] * (2 * n), out_specs=(SEM, SEM, *[HBM] * (2 * n), VM),
        input_output_aliases={w: 2 + w for w in range(2 * n)},
        compiler_params=pltpu.CompilerParams(has_side_effects=EFFECT))(*[_in_hbm(g) for g in grads], *zones)
    return outs[0], outs[1], list(outs[2:2 + n]), list(outs[2 + n:2 + 2 * n]), outs[-1]


def swap_wait(send_sems, recv_sems, grads, lands, after, name):
    n = len(grads)

    def body(*refs):
        ins, zones, (send_ref, recv_ref) = refs[:n], refs[n:2 * n], refs[2 * n:2 * n + 2]
        for cp in _swap_copies(ins, zones, send_ref, recv_ref):
            cp.wait_send()
            cp.wait_recv()

    outs = pl.pallas_call(
        body, name=name, out_shape=tuple(pltpu.HBM(a.shape, a.dtype) for a in list(grads) + list(lands)),
        in_specs=[HBM] * (2 * n) + [SEM, SEM, ANY], out_specs=(HBM,) * (2 * n),
        input_output_aliases={w: w for w in range(2 * n)},
        compiler_params=pltpu.CompilerParams(has_side_effects=EFFECT))(*grads, *lands, send_sems, recv_sems, after)
    return list(outs[:n]), list(outs[n:])


def _exchange_copies(sums, lands, send_sems, recv_sems):
    x, y, c, me, others = _place()
    out = []
    for w, (src, land) in enumerate(zip(sums, lands)):
        for k, (ox, oy) in enumerate(others):
            sems = send_sems.at[3 * w + k], recv_sems.at[3 * w + k]
            out.append((_remote(src.at[2 * ox + oy], land.at[me], *sems, (ox, oy, c)),
                        _remote(src.at[2 * ox + oy], land.at[2 * ox + oy], *sems, (ox, oy, c))))
    return out


def exchange_start(sums, name):
    n = len(sums)

    def body(*refs):
        ins, lands, (send_sems, recv_sems), token = refs[:n], refs[n:2 * n], refs[2 * n:2 * n + 2], refs[-1]
        for sent, _ in _exchange_copies(ins, lands, send_sems, recv_sems):
            sent.start()
        token[...] = jnp.zeros_like(token)

    zones = [_in_hbm(lax.empty(s.shape, s.dtype)) for s in sums]
    outs = pl.pallas_call(
        body, name=name,
        out_shape=(pltpu.SemaphoreType.DMA((3 * n,)), pltpu.SemaphoreType.DMA((3 * n,)),
                   *[pltpu.HBM(s.shape, s.dtype) for s in sums] * 2, TOKEN),
        in_specs=[HBM] * (2 * n), out_specs=(SEM, SEM, *[HBM] * (2 * n), VM),
        input_output_aliases={w: 2 + w for w in range(2 * n)},
        compiler_params=pltpu.CompilerParams(has_side_effects=EFFECT))(*[_in_hbm(s) for s in sums], *zones)
    return outs[0], outs[1], list(outs[2:2 + n]), list(outs[2 + n:2 + 2 * n]), outs[-1]


def exchange_wait(send_sems, recv_sems, sums, lands, after, name):
    n = len(sums)

    def body(*refs):
        ins, zones, (send_ref, recv_ref) = refs[:n], refs[n:2 * n], refs[2 * n:2 * n + 2]
        for sent, landed in _exchange_copies(ins, zones, send_ref, recv_ref):
            sent.wait_send()
            landed.wait_recv()

    outs = pl.pallas_call(
        body, name=name, out_shape=tuple(pltpu.HBM(s.shape, s.dtype) for s in sums) * 2,
        in_specs=[HBM] * (2 * n) + [SEM, SEM, ANY], out_specs=(HBM,) * (2 * n),
        input_output_aliases={w: w for w in range(2 * n)},
        compiler_params=pltpu.CompilerParams(has_side_effects=EFFECT))(*sums, *lands, send_sems, recv_sems, after)
    return list(outs[:n]), list(outs[n:])


def join_halves(grads, name):
    n = len(grads)

    def body(*refs):
        ins, outs = refs[:n], refs[n:2 * n]
        send_sems, recv_sems = refs[2 * n:]
        x, y, c, _, _ = _place()
        copies = []
        for w in range(n):
            half = ins[w].shape[0] // 2
            mine = pl.ds(c * half, half)
            cp = _remote(ins[w].at[mine], outs[w].at[mine], send_sems.at[w], recv_sems.at[w], (x, y, 1 - c))
            cp.start()
            copies.append(cp)
        for w, cp in enumerate(copies):
            half = ins[w].shape[0] // 2
            theirs = outs[w].at[pl.ds((1 - c) * half, half)]
            _remote(theirs, theirs, send_sems.at[w], recv_sems.at[w], (x, y, 1 - c)).wait_recv()
            cp.wait_send()

    return pl.pallas_call(
        body, out_shape=tuple(jax.ShapeDtypeStruct(g.shape, g.dtype) for g in grads),
        in_specs=[ANY] * n, out_specs=(ANY,) * n, input_output_aliases={w: w for w in range(n)},
        scratch_shapes=[pltpu.SemaphoreType.DMA((n,)), pltpu.SemaphoreType.DMA((n,))],
        name=name)(*grads)


VM = pl.BlockSpec(memory_space=pltpu.VMEM)


def small_allgather(buf, name="small_allgather"):
    def body(in_ref, out_ref, send_sems, recv_sems):
        x, y, c, me, others = _place()
        out_ref[me] = in_ref[...]
        copies = []
        for k, (ox, oy) in enumerate(others):
            cp = _remote(in_ref, out_ref.at[me], send_sems.at[k], recv_sems.at[k], (ox, oy, c))
            cp.start()
            copies.append(cp)
        for k, (ox, oy) in enumerate(others):
            landed = out_ref.at[2 * ox + oy]
            _remote(landed, landed, send_sems.at[k], recv_sems.at[k], (ox, oy, c)).wait_recv()
        for cp in copies:
            cp.wait_send()

    return pl.pallas_call(
        body, out_shape=jax.ShapeDtypeStruct((N_CHIPS,) + buf.shape, buf.dtype), in_specs=[VM], out_specs=VM,
        scratch_shapes=[pltpu.SemaphoreType.DMA((3,)), pltpu.SemaphoreType.DMA((3,))],
        compiler_params=pltpu.CompilerParams(vmem_limit_bytes=V7X_VMEM_LIMIT), name=name)(buf)


def small_allreduce(buf, name="small_allreduce"):
    def body(in_ref, out_ref, sib_ref, slot_ref, send_sems, recv_sems):
        x, y, c, me, others = _place()
        cp = _remote(in_ref, sib_ref, send_sems.at[3], recv_sems.at[3], (x, y, 1 - c))
        cp.start()
        cp.wait()
        slot_ref[me] = in_ref[...] + sib_ref[...]
        copies = []
        for k, (ox, oy) in enumerate(others):
            cp = _remote(slot_ref.at[me], slot_ref.at[me], send_sems.at[k], recv_sems.at[k], (ox, oy, c))
            cp.start()
            copies.append(cp)
        for k, (ox, oy) in enumerate(others):
            landed = slot_ref.at[2 * ox + oy]
            _remote(landed, landed, send_sems.at[k], recv_sems.at[k], (ox, oy, c)).wait_recv()
        for cp in copies:
            cp.wait_send()
        out_ref[...] = ((slot_ref[0] + slot_ref[1]) + slot_ref[2]) + slot_ref[3]

    return pl.pallas_call(
        body, out_shape=jax.ShapeDtypeStruct(buf.shape, buf.dtype), in_specs=[VM], out_specs=VM,
        scratch_shapes=[pltpu.VMEM(buf.shape, buf.dtype), pltpu.VMEM((N_CHIPS,) + buf.shape, buf.dtype),
                        pltpu.SemaphoreType.DMA((4,)), pltpu.SemaphoreType.DMA((4,))],
        compiler_params=pltpu.CompilerParams(vmem_limit_bytes=V7X_VMEM_LIMIT), name=name)(buf)


def _pack_rows(arrays, row_multiple):
    flat = jnp.concatenate([a.reshape(-1) for a in arrays])
    rows = -(-flat.shape[0] // (LANES * row_multiple)) * row_multiple
    return jnp.pad(flat, (0, rows * LANES - flat.shape[0])).reshape(rows, LANES)


def _unpack_rows(buf, shapes):
    flat = buf.reshape(-1)
    out, at = [], 0
    for s in shapes:
        n = math.prod(s)
        out.append(flat[at:at + n].reshape(s))
        at += n
    return out


def _mixer_weights(i):
    j = i // 2
    mixer = "a" if i % 2 == 0 else "b"
    return [("w_mem_kv", i), (mixer + "_w_in", j), (mixer + "_w_out", j)]


def _ffn_weights(i):
    return [("ffn_w_up", i), ("ffn_w_down", i)]


def _mixer_fwd(i, x, mem, w, small, after):
    is_a = i % 2 == 0
    j = i // 2
    wkv, win, wout = w
    wkv = wkv.reshape(1, D_MODEL, 2 * MEM_W)
    h1 = rms_fwd(x, small["mix_norm_g"][i], BF16, name=f"mix_norm{i}", after=after)
    mem_n = rms_fwd(mem, small["mem_norm_g"][i], BF16, name=f"mem_norm{i}")
    kv = mm_nn(mem_n, wkv, F32, name=f"mem_kv{i}")
    proj = mm_nn(h1, win, F32, name=f"in_proj{i}")
    saved = dict(x0=x, h1=h1, mem_n=mem_n, kv=kv, proj=proj)
    if is_a:
        outs, lses = zip(*[attn_fwd(proj, g, name=f"attn_fwd{i}_{g}") for g in range(3)])
        comb, lse = attn_combine(outs, lses, name=f"attn_combine{i}")
        mem_out = mem_fwd(proj, 3 * A_QKV_W, kv, name=f"mem_fwd{i}")
        cat = jnp.concatenate([comb.astype(BF16), mem_out], axis=1)
        saved.update(comb=comb, lse=lse)
    else:
        wout = wout.reshape(1, B_W + MEM_W, D_MODEL)
        tok = sgu_fwd(proj, small["b_v_norm_g"][j], small["b_w_s"][j], small["bias_b"][j], name=f"sgu_fwd{i}")
        mem_out = mem_fwd(proj, 2 * B_W, kv, name=f"mem_fwd{i}")
        cat = jnp.concatenate([tok, mem_out], axis=1)
    x1 = mm_nn(cat, wout, F32, res=x, name=f"out_proj{i}")
    saved.update(cat=cat)
    return x1, saved


def _ffn_fwd(i, x1, w, small, after):
    wup, wdn = w
    h2 = rms_fwd(x1, small["ffn_norm_g"][i], BF16, name=f"ffn_norm{i}", after=after)
    a = mm_nn(h2, wup, F32, name=f"ffn_up{i}")
    act = ffn_act_fwd(a, small["ffn_conv_w"][i], small["ffn_conv_b"][i], name=f"ffn_act{i}")
    x2 = mm_nn(act, wdn.reshape(1, FF, D_MODEL), F32, res=x1, name=f"ffn_down{i}")
    return x2, dict(x1=x1, h2=h2, a=a, act=act)


def _ffn_bwd(i, dx2, w, small, sv, after):
    wup, wdn = w
    sg = {}
    dact = mm_nt(dx2, wdn.reshape(1, FF, D_MODEL), F32, name=f"d_act{i}", after=after)
    d_wdn = mm_tn(sv["act"], dx2, 1, BF16, name=f"d_wdown{i}").reshape(N_CHIPS, FF // N_CHIPS, D_MODEL)
    da, sg["ffn_conv_w"], sg["ffn_conv_b"] = ffn_act_bwd(sv["a"], small["ffn_conv_w"][i], small["ffn_conv_b"][i], dact,
                                                          name=f"ffn_act_bwd{i}")
    d_wup = mm_tn(sv["h2"], da, N_CHIPS, BF16, name=f"d_wup{i}")
    dh2 = mm_nt(da, wup, F32, name=f"d_h2_{i}")
    dx1, sg["ffn_norm_g"] = rms_bwd(sv["x1"], small["ffn_norm_g"][i], dh2, dres=dx2, name=f"ffn_norm_bwd{i}")
    return dx1, [d_wup, d_wdn], sg


def _mixer_bwd(i, dx1, mem, w, small, sv, after):
    is_a = i % 2 == 0
    j = i // 2
    wkv, win, wout = w
    wkv = wkv.reshape(1, D_MODEL, 2 * MEM_W)
    sg = {}
    proj, kv = sv["proj"], sv["kv"]
    if is_a:
        dcat = mm_nt(dx1, wout, F32, name=f"d_cat{i}", after=after)
        d_wout = mm_tn(sv["cat"], dx1, N_CHIPS, BF16, name=f"d_wout{i}")
        dqm, dkv = mem_bwd(proj, 3 * A_QKV_W, kv, dcat, A_OUT_W, name=f"mem_bwd{i}")
        parts = [attn_bwd(proj, dcat, sv["comb"], sv["lse"], g, name=f"attn_bwd{i}_{g}") for g in range(3)]
        dproj = jnp.concatenate([parts[g][p] for p in range(3) for g in range(3)] + [dqm], axis=1)
    else:
        dcat = mm_nt(dx1, wout.reshape(1, B_W + MEM_W, D_MODEL), F32, name=f"d_cat{i}", after=after)
        d_wout = mm_tn(sv["cat"], dx1, 1, BF16, name=f"d_wout{i}").reshape(N_CHIPS, (B_W + MEM_W) // N_CHIPS, D_MODEL)
        dqm, dkv = mem_bwd(proj, 2 * B_W, kv, dcat, B_W, name=f"mem_bwd{i}")
        w_s = small["b_w_s"][j]
        duv, sg["b_w_s"], dmix, sg["b_v_norm_g"] = sgu_bwd(proj, small["b_v_norm_g"][j], w_s, jnp.swapaxes(w_s, 1, 2),
                                                           small["bias_b"][j], dcat, name=f"sgu_bwd{i}")
        sg["b_s_bias"] = jnp.sum(dmix, axis=-1)
        dproj = jnp.concatenate([duv, dqm], axis=1)
    d_wkv = mm_tn(sv["mem_n"], dkv, 1, BF16, name=f"d_wkv{i}").reshape(N_CHIPS, D_MODEL // N_CHIPS, 2 * MEM_W)
    dmem_n = mm_nt(dkv, wkv, F32, name=f"d_mem_n{i}")
    _, sg["mem_norm_g"] = rms_bwd(mem, small["mem_norm_g"][i], dmem_n, name=f"mem_norm_bwd{i}")
    d_win = mm_tn(sv["h1"], dproj, N_CHIPS, BF16, name=f"d_win{i}")
    dh1 = mm_nt(dproj, win, F32, name=f"d_h1_{i}")
    dx0, sg["mix_norm_g"] = rms_bwd(sv["x0"], small["mix_norm_g"][i], dh1, dres=dx1, name=f"mix_norm_bwd{i}")
    return dx0, [d_wkv, d_win, d_wout], sg


def _exchange_begin(grads, got, place, tag):
    sums = [pair_sum(g, o, place[1:], name=f"pair_sum{tag}_{k}") for k, (g, o) in enumerate(zip(grads, got))]
    send_sems, recv_sems, sums, lands, token = exchange_start(sums, name=f"exchange_start{tag}")
    return (send_sems, recv_sems, sums, lands), token


def _reduce_finish(started, place, after, tag):
    sums, parts = exchange_wait(*started, after, name=f"exchange_wait{tag}")
    halves = [chip_sum(s, p, place, name=f"chip_sum{tag}_{k}") for k, (s, p) in enumerate(zip(sums, parts))]
    return join_halves(halves, name=f"join_halves{tag}")


SMALL_SHARDED = ("b_v_norm_g", "ffn_conv_w")
SMALL_FULL_SHAPES = dict(mix_norm_g=(D_MODEL,), ffn_norm_g=(D_MODEL,), mem_norm_g=(D_MODEL,), b_v_norm_g=(B_W,),
                         b_w_s=(B_GROUPS, CHUNK, CHUNK), b_s_bias=(B_GROUPS, CHUNK), ffn_conv_w=(3, 2 * FF),
                         ffn_conv_b=(2 * FF,))
BIG = ("w_mem_kv", "a_w_in", "a_w_out", "b_w_in", "b_w_out", "ffn_w_up", "ffn_w_down")
WEIGHT_ORDER = ("mix_norm_g", "ffn_norm_g", "mem_norm_g", "w_mem_kv", "a_w_in", "a_w_out", "b_w_in", "b_v_norm_g", "b_w_s",
                "b_s_bias", "b_w_out", "ffn_w_up", "ffn_conv_w", "ffn_conv_b", "ffn_w_down", "final_norm_g")


def kernel(x, mem, mix_norm_g, ffn_norm_g, mem_norm_g, w_mem_kv, a_w_in, a_w_out, b_w_in, b_v_norm_g, b_w_s, b_s_bias, b_w_out, ffn_w_up, ffn_conv_w, ffn_conv_b, ffn_w_down, final_norm_g, loss_target, m_mix_norm_g, m_ffn_norm_g, m_mem_norm_g, m_w_mem_kv, m_a_w_in, m_a_w_out, m_b_w_in, m_b_v_norm_g, m_b_w_s, m_b_s_bias, m_b_w_out, m_ffn_w_up, m_ffn_conv_w, m_ffn_conv_b, m_ffn_w_down, m_final_norm_g, v_mix_norm_g, v_ffn_norm_g, v_mem_norm_g, v_w_mem_kv, v_a_w_in, v_a_w_out, v_b_w_in, v_b_v_norm_g, v_b_w_s, v_b_s_bias, v_b_w_out, v_ffn_w_up, v_ffn_conv_w, v_ffn_conv_b, v_ffn_w_down, v_final_norm_g):
    weights = dict(mix_norm_g=mix_norm_g, ffn_norm_g=ffn_norm_g, mem_norm_g=mem_norm_g, w_mem_kv=w_mem_kv, a_w_in=a_w_in,
                   a_w_out=a_w_out, b_w_in=b_w_in, b_v_norm_g=b_v_norm_g, b_w_s=b_w_s, b_s_bias=b_s_bias, b_w_out=b_w_out,
                   ffn_w_up=ffn_w_up, ffn_conv_w=ffn_conv_w, ffn_conv_b=ffn_conv_b, ffn_w_down=ffn_w_down,
                   final_norm_g=final_norm_g)
    mom1 = dict(mix_norm_g=m_mix_norm_g, ffn_norm_g=m_ffn_norm_g, mem_norm_g=m_mem_norm_g, w_mem_kv=m_w_mem_kv,
                a_w_in=m_a_w_in, a_w_out=m_a_w_out, b_w_in=m_b_w_in, b_v_norm_g=m_b_v_norm_g, b_w_s=m_b_w_s,
                b_s_bias=m_b_s_bias, b_w_out=m_b_w_out, ffn_w_up=m_ffn_w_up, ffn_conv_w=m_ffn_conv_w,
                ffn_conv_b=m_ffn_conv_b, ffn_w_down=m_ffn_w_down, final_norm_g=m_final_norm_g)
    mom2 = dict(mix_norm_g=v_mix_norm_g, ffn_norm_g=v_ffn_norm_g, mem_norm_g=v_mem_norm_g, w_mem_kv=v_w_mem_kv,
                a_w_in=v_a_w_in, a_w_out=v_a_w_out, b_w_in=v_b_w_in, b_v_norm_g=v_b_v_norm_g, b_w_s=v_b_w_s,
                b_s_bias=v_b_s_bias, b_w_out=v_b_w_out, ffn_w_up=v_ffn_w_up, ffn_conv_w=v_ffn_conv_w,
                ffn_conv_b=v_ffn_conv_b, ffn_w_down=v_ffn_w_down, final_norm_g=v_final_norm_g)
    chip = 2 * lax.axis_index("x") + lax.axis_index("y")
    place = jnp.stack([chip, lax.axis_index("c")]).astype(jnp.int32)
    x0, mem0, target = x[0], mem[0], loss_target[0]
    depth = DEPTH

    n_cw, n_vg = ffn_conv_w.size, b_v_norm_g.size
    gathered = small_allgather(_pack_rows([ffn_conv_w, b_v_norm_g], 8)).reshape(N_CHIPS, -1)
    conv_w_full = gathered[:, :n_cw].reshape(N_CHIPS, DEPTH, 3, 2 * FF // N_CHIPS).transpose(1, 2, 0, 3).reshape(DEPTH, 3, 2 * FF)
    vgain_full = gathered[:, n_cw:n_cw + n_vg].reshape(N_CHIPS, 2, B_W // N_CHIPS).transpose(1, 0, 2).reshape(2, B_W)
    small = dict(mix_norm_g=mix_norm_g, ffn_norm_g=ffn_norm_g, mem_norm_g=mem_norm_g, b_w_s=b_w_s, ffn_conv_b=ffn_conv_b,
                 ffn_conv_w=conv_w_full, b_v_norm_g=vgain_full,
                 bias_b=jnp.broadcast_to(b_s_bias[..., None], b_s_bias.shape + (CHUNK,)))

    def start_gather(group, tag, after=None):
        bufs = [cast_to_slot(weights[n], l, place, name=f"cast_{n}{l}", after=after) for n, l in group]
        return gather_start(bufs, name=f"gather_start_{tag}")

    def finish_gather(started, after, tag):
        send_sems, recv_sems, bufs, _ = started
        return forward_to_sibling(gather_wait(send_sems, recv_sems, bufs, after, name=f"gather_wait_{tag}"),
                                  name=f"forward_{tag}")

    w_mix, w_ffn, saved_mix, saved_ffn = [], [], [], []
    flying_mix = start_gather(_mixer_weights(0), "m0", after=gathered)
    flying_ffn = start_gather(_ffn_weights(0), "f0", after=flying_mix[3])
    h, tie = x0, flying_ffn[3]
    for i in range(depth):
        w_mix.append(finish_gather(flying_mix, tie if i == 0 else h, f"m{i}"))
        if i + 1 < depth:
            flying_mix = start_gather(_mixer_weights(i + 1), f"m{i + 1}")
            tie = flying_mix[3]
        x1, sv = _mixer_fwd(i, h, mem0, w_mix[i], small, tie)
        saved_mix.append(sv)
        w_ffn.append(finish_gather(flying_ffn, x1, f"f{i}"))
        if i + 1 < depth:
            flying_ffn = start_gather(_ffn_weights(i + 1), f"f{i + 1}")
            tie = flying_ffn[3]
        h, sv = _ffn_fwd(i, x1, w_ffn[i], small, tie)
        saved_ffn.append(sv)
    loss_row, dh, d_final = final_loss(h, final_norm_g, target)
    loss = lax.psum(loss_row[0, 0], ("x", "y", "c"))

    names = [n for n in WEIGHT_ORDER if n not in BIG]
    small_g = {n: [None] * weights[n].shape[0] for n in names if n != "final_norm_g"}
    big_out = {n: None for n in BIG}

    def keep_small(i, sg):
        for n, g in sg.items():
            small_g[n][i if len(small_g[n]) == depth else i // 2] = g.reshape(SMALL_FULL_SHAPES[n])

    def finish_reduce(started, group, after, tag):
        for (n, l), g in zip(group, _reduce_finish(started, place, after, tag)):
            big_out[n] = adamw_layer(weights[n], mom1[n], mom2[n], l, g, big_out[n], name=f"adamw_{n}{l}")

    half_layers = 2 * depth
    swapping, exchanging, tie = None, [], None
    for k in range(half_layers):
        i = depth - 1 - k // 2
        if k % 2 == 0:
            dh, big_g, sg = _ffn_bwd(i, dh, w_ffn[i], small, saved_ffn[i], tie)
            group, tag = _ffn_weights(i), f"f{i}"
        else:
            dh, big_g, sg = _mixer_bwd(i, dh, mem0, w_mix[i], small, saved_mix[i], tie)
            group, tag = _mixer_weights(i), f"m{i}"
        keep_small(i, sg)
        started_now, swap_now = [], None
        if k < half_layers - 2:
            *swap_now, tie = swap_start(big_g, name=f"swap_start_{tag}")
            swap_now = (swap_now, group, tag)
        else:
            started, tie = _exchange_begin(big_g, swap_halves(big_g, name=f"swap_halves_{tag}"), place, tag)
            started_now.append((started, group, tag))
        if swapping is not None:
            swap_args, old_group, old_tag = swapping
            grads, got = swap_wait(*swap_args, dh, name=f"swap_wait_{old_tag}")
            started, tie = _exchange_begin(grads, got, place, old_tag)
            started_now.append((started, old_group, old_tag))
        for started, old_group, old_tag in exchanging:
            finish_reduce(started, old_group, dh, old_tag)
        swapping, exchanging = swap_now, started_now
    for started, old_group, old_tag in exchanging:
        finish_reduce(started, old_group, tie, old_tag)

    full_g = {n: (d_final.reshape(-1) if n == "final_norm_g" else jnp.stack(small_g[n])) for n in names}
    shapes = [full_g[n].shape for n in names]
    summed = dict(zip(names, _unpack_rows(small_allreduce(_pack_rows([full_g[n] for n in names], 8)), shapes)))
    for n in SMALL_SHARDED:
        width = weights[n].shape[-1]
        summed[n] = lax.dynamic_slice_in_dim(summed[n], chip * width, width, axis=summed[n].ndim - 1)
    own_shapes = [weights[n].shape for n in names]
    pack = lambda d: _pack_rows([d[n] for n in names], 128)
    small_out = [_unpack_rows(b, own_shapes) for b in adamw_flat(pack(weights), pack(summed), pack(mom1), pack(mom2))]
    outs = {}
    for k, n in enumerate(names):
        outs[n] = (summed[n], small_out[0][k], small_out[1][k], small_out[2][k])
    outs.update(big_out)
    return (loss, dh[None], *[outs[n][0] for n in WEIGHT_ORDER], *[outs[n][1] for n in WEIGHT_ORDER],
            *[outs[n][2] for n in WEIGHT_ORDER], *[outs[n][3] for n in WEIGHT_ORDER])
```

```python
import functools
import math

import numpy as np
import jax
import jax.numpy as jnp
from jax import lax
from jax.experimental import pallas as pl
from jax.experimental.pallas import tpu as pltpu

F32 = jnp.float32
BF16 = jnp.bfloat16
MESH = pl.DeviceIdType.MESH

D_MODEL = 2048
SEQ = 2048
DEPTH = 4
EPS = 1e-6
NEG = -1e30
HEAD = 128
A_PATTERNS = ((128, 1), (512, 4), (2048, 16))
A_QKV_W = 1536
A_OUT_W = 512
A_IN = 5120
QBLK = 128
N_SIDE = 64
CHUNK = 128
B_GROUPS = 12
B_W = 1536
B_IN = 3584
MEM_LEN = 256
MEM_HEADS = 4
MEM_W = 512
FF = 5632
ADAM_LR, ADAM_B1, ADAM_B2, ADAM_EPS, ADAM_WD, ADAM_STEP = 0.001, 0.9, 0.999, 1e-08, 0.01, 10
N_CHIPS = 4

LANES = 128
V7X_VMEM_LIMIT = 56 * 1024 * 1024


def _cp(*sem):
    return pltpu.CompilerParams(dimension_semantics=sem, vmem_limit_bytes=V7X_VMEM_LIMIT)


def _pick(dim, prefs):
    for p in prefs:
        if dim % p == 0:
            return p
    raise ValueError(f"no tile for {dim} in {prefs}")


def _gelu_parts(x):
    cdf = 0.5 * (1.0 + lax.erf(x * (1.0 / math.sqrt(2.0))))
    pdf = jnp.exp(-0.5 * x * x) * (1.0 / math.sqrt(2.0 * math.pi))
    return x * cdf, cdf + x * pdf


def _gelu(x):
    return 0.5 * x * (1.0 + lax.erf(x * (1.0 / math.sqrt(2.0))))


TM_PREFS = (1024, 512, 256, 128)
TN_PREFS = (1408, 1280, 1024, 896, 512, 256, 128)
TK_PREFS = (2816, 2048, 1408, 1280, 1024, 896, 512, 256, 128)


def _mm_body(nk, dims, has_res):
    def body(*refs):
        if has_res:
            a_ref, b_ref, r_ref, o_ref = refs[:4]
        else:
            a_ref, b_ref, o_ref = refs[:3]
            r_ref = None
        part = lax.dot_general(a_ref[...].astype(BF16), b_ref[...].astype(BF16), dims,
                               preferred_element_type=F32)
        if nk == 1:
            if has_res:
                part = part + r_ref[...]
            o_ref[...] = part.astype(o_ref.dtype)
            return
        acc_ref = refs[-1]
        k = pl.program_id(2)

        @pl.when(k == 0)
        def _():
            acc_ref[...] = part

        @pl.when(k > 0)
        def _():
            acc_ref[...] += part

        @pl.when(k == nk - 1)
        def _():
            tot = acc_ref[...]
            if has_res:
                tot = tot + r_ref[...]
            o_ref[...] = tot.astype(o_ref.dtype)
    return body


def mm_nn(a, w, out_dtype, res=None, name="mm_nn"):
    m, kw = a.shape
    ns_, kw2, nsz = w.shape
    assert kw == kw2
    n = ns_ * nsz
    tm, tn, tk = _pick(m, TM_PREFS), _pick(nsz, TN_PREFS), _pick(kw, TK_PREFS)
    nb, nk = nsz // tn, kw // tk
    in_specs = [pl.BlockSpec((tm, tk), lambda i, j, k: (i, k)),
                pl.BlockSpec((None, tk, tn), lambda i, j, k: (j // nb, k, j % nb))]
    args = [a, w]
    if res is not None:
        in_specs.append(pl.BlockSpec((tm, tn), lambda i, j, k: (i, j)))
        args.append(res)
    return pl.pallas_call(
        _mm_body(nk, (((1,), (0,)), ((), ())), res is not None),
        out_shape=jax.ShapeDtypeStruct((m, n), out_dtype),
        grid=(m // tm, n // tn, nk), in_specs=in_specs,
        out_specs=pl.BlockSpec((tm, tn), lambda i, j, k: (i, j)),
        scratch_shapes=[pltpu.VMEM((tm, tn), F32)] if nk > 1 else [],
        compiler_params=_cp("parallel", "parallel", "arbitrary"), name=name)(*args)


def mm_nt(g, w, out_dtype, name="mm_nt", after=None):
    m, n = g.shape
    ns_, kw, nsz = w.shape
    assert n == ns_ * nsz
    tm, tn, tk = _pick(m, TM_PREFS), _pick(kw, TN_PREFS), _pick(nsz, TK_PREFS)
    nb, nk = nsz // tk, n // tk
    body = _mm_body(nk, (((1,), (1,)), ((), ())), False)
    tied = [] if after is None else [after]
    return pl.pallas_call(
        (lambda g_ref, w_ref, *rest: body(g_ref, w_ref, *rest[len(tied):])),
        out_shape=jax.ShapeDtypeStruct((m, kw), out_dtype),
        grid=(m // tm, kw // tn, nk),
        in_specs=[pl.BlockSpec((tm, tk), lambda i, j, k: (i, k)),
                  pl.BlockSpec((None, tn, tk), lambda i, j, k: (k // nb, j, k % nb))] + [ANY] * len(tied),
        out_specs=pl.BlockSpec((tm, tn), lambda i, j, k: (i, j)),
        scratch_shapes=[pltpu.VMEM((tm, tn), F32)] if nk > 1 else [],
        compiler_params=_cp("parallel", "parallel", "arbitrary"), name=name)(g, w, *tied)


def mm_tn(a, g, n_shards, out_dtype, name="mm_tn"):
    t, kw = a.shape
    t2, n = g.shape
    assert t == t2
    nsz = n // n_shards
    tm, tn, tk = _pick(kw, TM_PREFS), _pick(nsz, TN_PREFS), _pick(t, TK_PREFS)
    nb, nk = nsz // tn, t // tk
    return pl.pallas_call(
        _mm_body(nk, (((0,), (0,)), ((), ())), False),
        out_shape=jax.ShapeDtypeStruct((n_shards, kw, nsz), out_dtype),
        grid=(kw // tm, n // tn, nk),
        in_specs=[pl.BlockSpec((tk, tm), lambda i, j, k: (k, i)),
                  pl.BlockSpec((tk, tn), lambda i, j, k: (k, j))],
        out_specs=pl.BlockSpec((None, tm, tn), lambda i, j, k: (j // nb, i, j % nb)),
        scratch_shapes=[pltpu.VMEM((tm, tn), F32)] if nk > 1 else [],
        compiler_params=_cp("parallel", "parallel", "arbitrary"), name=name)(a, g)


ROW_TILE = 256


def _rms_stats(x):
    r = lax.rsqrt(jnp.mean(x * x, axis=-1, keepdims=True) + EPS)
    return r, x * r


def _rms_back(xh, r, g, dh):
    u = dh * g
    return r * (u - xh * jnp.mean(u * xh, axis=-1, keepdims=True))


def rms_fwd(x, g, out_dtype, name="rms_fwd", after=None):
    rows, d = x.shape
    tr = _pick(rows, (ROW_TILE, 128))
    tied = [] if after is None else [after]

    def body(x_ref, g_ref, *rest):
        o_ref = rest[-1]
        _, xh = _rms_stats(x_ref[...])
        o_ref[...] = (xh * g_ref[...]).astype(o_ref.dtype)

    return pl.pallas_call(
        body, out_shape=jax.ShapeDtypeStruct((rows, d), out_dtype), grid=(rows // tr,),
        in_specs=[pl.BlockSpec((tr, d), lambda i: (i, 0)), pl.BlockSpec((1, d), lambda i: (0, 0))] + [ANY] * len(tied),
        out_specs=pl.BlockSpec((tr, d), lambda i: (i, 0)),
        compiler_params=_cp("parallel"), name=name)(x, g.reshape(1, d), *tied)


def rms_bwd(x, g, dh, dres=None, name="rms_bwd"):
    rows, d = x.shape
    tr = _pick(rows, (ROW_TILE, 128))
    has_res = dres is not None

    def body(*refs):
        if has_res:
            x_ref, g_ref, dh_ref, dres_ref, dx_ref, dg_ref = refs
        else:
            x_ref, g_ref, dh_ref, dx_ref, dg_ref = refs
        r, xh = _rms_stats(x_ref[...])
        dh_ = dh_ref[...].astype(F32)
        part = jnp.sum(dh_ * xh, axis=0, keepdims=True)

        @pl.when(pl.program_id(0) == 0)
        def _():
            dg_ref[...] = part

        @pl.when(pl.program_id(0) > 0)
        def _():
            dg_ref[...] += part

        dx = _rms_back(xh, r, g_ref[...], dh_)
        if has_res:
            dx = dx + dres_ref[...]
        dx_ref[...] = dx

    row_spec = pl.BlockSpec((tr, d), lambda i: (i, 0))
    vec_spec = pl.BlockSpec((1, d), lambda i: (0, 0))
    args = [x, g.reshape(1, d), dh] + ([dres] if has_res else [])
    return pl.pallas_call(
        body, out_shape=(jax.ShapeDtypeStruct((rows, d), F32), jax.ShapeDtypeStruct((1, d), F32)),
        grid=(rows // tr,), in_specs=[row_spec, vec_spec, row_spec] + ([row_spec] if has_res else []),
        out_specs=(row_spec, vec_spec), compiler_params=_cp("arbitrary"), name=name)(*args)


def final_loss(x, g, target, name="final_loss"):
    rows, d = x.shape
    tr = _pick(rows, (ROW_TILE, 128))

    def body(x_ref, g_ref, t_ref, loss_ref, dx_ref, dg_ref):
        r, xh = _rms_stats(x_ref[...])
        gain = g_ref[...]
        err = xh * gain - t_ref[...]
        sq = jnp.sum(jnp.sum(err * err, axis=1, keepdims=True), axis=0, keepdims=True) * (0.5 / d)
        dy = err * (1.0 / d)
        part = jnp.sum(dy * xh, axis=0, keepdims=True)

        @pl.when(pl.program_id(0) == 0)
        def _():
            dg_ref[...] = part
            loss_ref[...] = jnp.broadcast_to(sq, loss_ref.shape)

        @pl.when(pl.program_id(0) > 0)
        def _():
            dg_ref[...] += part
            loss_ref[...] += jnp.broadcast_to(sq, loss_ref.shape)

        dx_ref[...] = _rms_back(xh, r, gain, dy)

    row_spec = pl.BlockSpec((tr, d), lambda i: (i, 0))
    vec_spec = pl.BlockSpec((1, d), lambda i: (0, 0))
    return pl.pallas_call(
        body, out_shape=(jax.ShapeDtypeStruct((1, LANES), F32), jax.ShapeDtypeStruct((rows, d), F32),
                         jax.ShapeDtypeStruct((1, d), F32)),
        grid=(rows // tr,), in_specs=[row_spec, vec_spec, row_spec],
        out_specs=(pl.BlockSpec((1, LANES), lambda i: (0, 0)), row_spec, vec_spec),
        compiler_params=_cp("arbitrary"), name=name)(x, g.reshape(1, d), target)


def _alibi_slopes():
    return (2.0 ** (-8.0 * (np.arange(12) + 1) / 12)).astype(np.float32)


def _band_scores(q, k, q0, start, wk, slope):
    s = lax.dot_general(q, k, (((1,), (1,)), ((), ())), preferred_element_type=F32) * (HEAD ** -0.5)
    qpos = q0 + lax.broadcasted_iota(jnp.int32, (QBLK, wk), 0)
    kpos = start + lax.broadcasted_iota(jnp.int32, (QBLK, wk), 1)
    rel = jnp.abs(qpos - kpos)
    return jnp.where(rel <= N_SIDE, s - slope * rel.astype(F32), NEG)


def _attn_geometry(seq, dilation):
    length = seq // dilation
    return length, length // QBLK, min(2 * QBLK, length)


def _attn_window(n, length, wk):
    q0 = pl.multiple_of(n * QBLK, QBLK)
    start = pl.multiple_of(jnp.clip(n * QBLK - N_SIDE, 0, length - wk), N_SIDE)
    return q0, start


def attn_fwd(proj, group, name):
    seq = proj.shape[0]
    dilation = A_PATTERNS[group][1]
    length, nblk, wk = _attn_geometry(seq, dilation)
    cols = A_IN // HEAD
    pv = proj.reshape(length, dilation * A_IN)

    def body(slope_ref, q_ref, k_ref, v_ref, o_ref, lse_ref):
        slope = slope_ref[group * 4 + pl.program_id(1)] * float(dilation)

        def blk(n, carry):
            q0, start = _attn_window(n, length, wk)
            q = q_ref[pl.ds(q0, QBLK), :].astype(BF16)
            k = k_ref[pl.ds(start, wk), :].astype(BF16)
            v = v_ref[pl.ds(start, wk), :].astype(BF16)
            s = _band_scores(q, k, q0, start, wk, slope)
            m = jnp.max(s, axis=-1, keepdims=True)
            p = jnp.exp(s - m)
            l = jnp.sum(p, axis=-1, keepdims=True)
            o = jnp.dot(p.astype(BF16), v, preferred_element_type=F32) / l
            o_ref[pl.ds(q0, QBLK), :] = o
            lse_ref[pl.ds(q0, QBLK), :] = jnp.broadcast_to(m + jnp.log(l), (QBLK, HEAD))
            return carry

        lax.fori_loop(0, nblk, blk, 0)

    def part(p):
        return pl.BlockSpec((length, HEAD), lambda r, h: (0, r * cols + p * 12 + group * 4 + h))

    out_spec = pl.BlockSpec((length, HEAD), lambda r, h: (0, r * 4 + h))
    o, lse = pl.pallas_call(
        body, out_shape=(jax.ShapeDtypeStruct((length, dilation * A_OUT_W), F32),) * 2,
        grid=(dilation, 4),
        in_specs=[pl.BlockSpec(memory_space=pltpu.SMEM), part(0), part(1), part(2)],
        out_specs=(out_spec, out_spec), compiler_params=_cp("parallel", "parallel"), name=name,
    )(jnp.asarray(_alibi_slopes()), pv, pv, pv)
    return o.reshape(seq, A_OUT_W), lse.reshape(seq, A_OUT_W)


def attn_combine(os_, lses, name="attn_combine"):
    seq = os_[0].shape[0]
    tr = ROW_TILE

    def body(o0, o1, o2, l0, l1, l2, c_ref, lse_ref):
        a, b, c = l0[...], l1[...], l2[...]
        m = jnp.maximum(jnp.maximum(a, b), c)
        ea, eb, ec = jnp.exp(a - m), jnp.exp(b - m), jnp.exp(c - m)
        den = ea + eb + ec
        c_ref[...] = (ea * o0[...] + eb * o1[...] + ec * o2[...]) / den
        lse_ref[...] = m + jnp.log(den)

    spec = pl.BlockSpec((tr, A_OUT_W), lambda i: (i, 0))
    return pl.pallas_call(
        body, out_shape=(jax.ShapeDtypeStruct((seq, A_OUT_W), F32),) * 2, grid=(seq // tr,),
        in_specs=[spec] * 6, out_specs=(spec, spec), compiler_params=_cp("parallel"), name=name)(*os_, *lses)


def attn_bwd(proj, dcat, comb, lse, group, name):
    seq = proj.shape[0]
    dilation = A_PATTERNS[group][1]
    length, nblk, wk = _attn_geometry(seq, dilation)
    cols = A_IN // HEAD
    pv = proj.reshape(length, dilation * A_IN)
    view = lambda a: a.reshape(length, dilation * A_OUT_W)
    scale = HEAD ** -0.5

    def body(slope_ref, q_ref, k_ref, v_ref, do_ref, c_ref, lse_ref, dq_ref, dk_ref, dv_ref, dk_acc, dv_acc):
        slope = slope_ref[group * 4 + pl.program_id(1)] * float(dilation)
        dk_acc[...] = jnp.zeros_like(dk_acc)
        dv_acc[...] = jnp.zeros_like(dv_acc)

        def blk(n, carry):
            q0, start = _attn_window(n, length, wk)
            rows = pl.ds(q0, QBLK)
            keys = pl.ds(start, wk)
            q = q_ref[rows, :].astype(BF16)
            k = k_ref[keys, :].astype(BF16)
            v = v_ref[keys, :].astype(BF16)
            do = do_ref[rows, :]
            s = _band_scores(q, k, q0, start, wk, slope)
            p = jnp.exp(s - lse_ref[rows, :][:, :1])
            delta = jnp.sum(do * c_ref[rows, :], axis=-1, keepdims=True)
            do16 = do.astype(BF16)
            dp = lax.dot_general(do16, v, (((1,), (1,)), ((), ())), preferred_element_type=F32)
            ds = (p * (dp - delta) * scale).astype(BF16)
            p16 = p.astype(BF16)
            dq_ref[rows, :] = jnp.dot(ds, k, preferred_element_type=F32).astype(dq_ref.dtype)
            dk_acc[keys, :] += lax.dot_general(ds, q, (((0,), (0,)), ((), ())), preferred_element_type=F32)
            dv_acc[keys, :] += lax.dot_general(p16, do16, (((0,), (0,)), ((), ())), preferred_element_type=F32)
            return carry

        lax.fori_loop(0, nblk, blk, 0)
        dk_ref[...] = dk_acc[...].astype(dk_ref.dtype)
        dv_ref[...] = dv_acc[...].astype(dv_ref.dtype)

    def part(p):
        return pl.BlockSpec((length, HEAD), lambda r, h: (0, r * cols + p * 12 + group * 4 + h))

    hs = pl.BlockSpec((length, HEAD), lambda r, h: (0, r * 4 + h))
    do_cols = dcat.shape[1] // HEAD
    do_spec = pl.BlockSpec((length, HEAD), lambda r, h: (0, r * do_cols + h))
    outs = pl.pallas_call(
        body, out_shape=(jax.ShapeDtypeStruct((length, dilation * A_OUT_W), BF16),) * 3,
        grid=(dilation, 4),
        in_specs=[pl.BlockSpec(memory_space=pltpu.SMEM), part(0), part(1), part(2), do_spec, hs, hs],
        out_specs=(hs, hs, hs),
        scratch_shapes=[pltpu.VMEM((length, HEAD), F32), pltpu.VMEM((length, HEAD), F32)],
        compiler_params=_cp("parallel", "parallel"), name=name,
    )(jnp.asarray(_alibi_slopes()), pv, pv, pv, dcat.reshape(length, dilation * dcat.shape[1]), view(comb), view(lse))
    return tuple(o.reshape(seq, A_OUT_W) for o in outs)


MEM_ROW_TILE = 512


def _mem_probs(q, k):
    s = lax.dot_general(q, k, (((1,), (1,)), ((), ())), preferred_element_type=F32) * (HEAD ** -0.5)
    p = jnp.exp(s - jnp.max(s, axis=-1, keepdims=True))
    return p / jnp.sum(p, axis=-1, keepdims=True)


def mem_fwd(proj, q_col, kv, name="mem_fwd"):
    seq = proj.shape[0]
    qb = q_col // HEAD

    def body(q_ref, k_ref, v_ref, o_ref):
        p = _mem_probs(q_ref[...].astype(BF16), k_ref[...].astype(BF16))
        o_ref[...] = jnp.dot(p.astype(BF16), v_ref[...].astype(BF16), preferred_element_type=F32).astype(o_ref.dtype)

    return pl.pallas_call(
        body, out_shape=jax.ShapeDtypeStruct((seq, MEM_W), BF16), grid=(MEM_HEADS, seq // MEM_ROW_TILE),
        in_specs=[pl.BlockSpec((MEM_ROW_TILE, HEAD), lambda h, i: (i, qb + h)),
                  pl.BlockSpec((MEM_LEN, HEAD), lambda h, i: (0, h)),
                  pl.BlockSpec((MEM_LEN, HEAD), lambda h, i: (0, MEM_HEADS + h))],
        out_specs=pl.BlockSpec((MEM_ROW_TILE, HEAD), lambda h, i: (i, h)),
        compiler_params=_cp("parallel", "parallel"), name=name)(proj, kv, kv)


def mem_bwd(proj, q_col, kv, dcat, do_col, name="mem_bwd"):
    seq = proj.shape[0]
    qb, ob = q_col // HEAD, do_col // HEAD
    scale = HEAD ** -0.5

    def body(q_ref, k_ref, v_ref, do_ref, dq_ref, dk_ref, dv_ref):
        q = q_ref[...].astype(BF16)
        k = k_ref[...].astype(BF16)
        v = v_ref[...].astype(BF16)
        do = do_ref[...].astype(BF16)
        p = _mem_probs(q, k)
        dp = lax.dot_general(do, v, (((1,), (1,)), ((), ())), preferred_element_type=F32)
        ds = (p * (dp - jnp.sum(dp * p, axis=-1, keepdims=True)) * scale).astype(BF16)
        dq_ref[...] = jnp.dot(ds, k, preferred_element_type=F32).astype(dq_ref.dtype)
        dk = lax.dot_general(ds, q, (((0,), (0,)), ((), ())), preferred_element_type=F32)
        dv = lax.dot_general(p.astype(BF16), do, (((0,), (0,)), ((), ())), preferred_element_type=F32)

        @pl.when(pl.program_id(1) == 0)
        def _():
            dk_ref[...] = dk
            dv_ref[...] = dv

        @pl.when(pl.program_id(1) > 0)
        def _():
            dk_ref[...] += dk
            dv_ref[...] += dv

    dq, dk, dv = pl.pallas_call(
        body, out_shape=(jax.ShapeDtypeStruct((seq, MEM_W), BF16), jax.ShapeDtypeStruct((MEM_LEN, MEM_W), F32),
                         jax.ShapeDtypeStruct((MEM_LEN, MEM_W), F32)),
        grid=(MEM_HEADS, seq // MEM_ROW_TILE),
        in_specs=[pl.BlockSpec((MEM_ROW_TILE, HEAD), lambda h, i: (i, qb + h)),
                  pl.BlockSpec((MEM_LEN, HEAD), lambda h, i: (0, h)),
                  pl.BlockSpec((MEM_LEN, HEAD), lambda h, i: (0, MEM_HEADS + h)),
                  pl.BlockSpec((MEM_ROW_TILE, HEAD), lambda h, i: (i, ob + h))],
        out_specs=(pl.BlockSpec((MEM_ROW_TILE, HEAD), lambda h, i: (i, h)),
                   pl.BlockSpec((MEM_LEN, HEAD), lambda h, i: (0, h)),
                   pl.BlockSpec((MEM_LEN, HEAD), lambda h, i: (0, h))),
        compiler_params=_cp("parallel", "arbitrary"), name=name)(proj, kv, kv, dcat)
    return dq, jnp.concatenate([dk, dv], axis=1)


def _sgu_front(x, gain):
    uv, duv = _gelu_parts(x)
    u, v = uv[:, :B_W], uv[:, B_W:]
    r, vh = _rms_stats(v)
    return u, duv, r, vh, vh * gain


def sgu_fwd(proj, gain, w_s, bias_b, name="sgu_fwd"):
    seq = proj.shape[0]

    def body(x_ref, gain_ref, ws_ref, bias_ref, o_ref):
        u, _, _, _, vn = _sgu_front(x_ref[...], gain_ref[...])
        for g in range(B_GROUPS):
            cs = slice(g * CHUNK, (g + 1) * CHUNK)
            mixed = jnp.dot(ws_ref[g].astype(BF16), vn[:, cs].astype(BF16), preferred_element_type=F32) + bias_ref[g]
            o_ref[:, cs] = (u[:, cs] * mixed).astype(o_ref.dtype)

    full = lambda shape: pl.BlockSpec(shape, lambda c: (0,) * len(shape))
    return pl.pallas_call(
        body, out_shape=jax.ShapeDtypeStruct((seq, B_W), BF16), grid=(seq // CHUNK,),
        in_specs=[pl.BlockSpec((CHUNK, 2 * B_W), lambda c: (c, 0)), full((1, B_W)),
                  full((B_GROUPS, CHUNK, CHUNK)), full((B_GROUPS, CHUNK, CHUNK))],
        out_specs=pl.BlockSpec((CHUNK, B_W), lambda c: (c, 0)),
        compiler_params=_cp("parallel"), name=name)(proj, gain.reshape(1, B_W), w_s, bias_b)


def sgu_bwd(proj, gain, w_s, w_s_t, bias_b, dcat, name="sgu_bwd"):
    seq = proj.shape[0]

    def body(x_ref, gain_ref, ws_ref, wst_ref, bias_ref, do_ref, dx_ref, dws_ref, dmix_ref, dgain_ref, dvn_ref):
        first = pl.program_id(0) == 0
        gain = gain_ref[...]
        u, duv, r, vh, vn = _sgu_front(x_ref[...], gain)
        do = do_ref[...]
        for g in range(B_GROUPS):
            cs = slice(g * CHUNK, (g + 1) * CHUNK)
            vg = vn[:, cs].astype(BF16)
            mixed = jnp.dot(ws_ref[g].astype(BF16), vg, preferred_element_type=F32) + bias_ref[g]
            dx_ref[:, cs] = (do[:, cs] * mixed * duv[:, cs]).astype(dx_ref.dtype)
            dmixed = do[:, cs] * u[:, cs]
            dm16 = dmixed.astype(BF16)
            dws = lax.dot_general(dm16, vg, (((1,), (1,)), ((), ())), preferred_element_type=F32)
            dvn_ref[:, cs] = jnp.dot(wst_ref[g].astype(BF16), dm16, preferred_element_type=F32)

            @pl.when(first)
            def _():
                dws_ref[g] = dws
                dmix_ref[g] = dmixed

            @pl.when(jnp.logical_not(first))
            def _():
                dws_ref[g] += dws
                dmix_ref[g] += dmixed

        dvn = dvn_ref[...]
        dgain = jnp.sum(dvn * vh, axis=0, keepdims=True)

        @pl.when(first)
        def _():
            dgain_ref[...] = dgain

        @pl.when(jnp.logical_not(first))
        def _():
            dgain_ref[...] += dgain

        dv = _rms_back(vh, r, gain, dvn)
        dx_ref[:, B_W:] = (dv * duv[:, B_W:]).astype(dx_ref.dtype)

    full = lambda shape: pl.BlockSpec(shape, lambda c: (0,) * len(shape))
    mats = full((B_GROUPS, CHUNK, CHUNK))
    return pl.pallas_call(
        body, out_shape=(jax.ShapeDtypeStruct((seq, 2 * B_W), BF16), jax.ShapeDtypeStruct((B_GROUPS, CHUNK, CHUNK), F32),
                         jax.ShapeDtypeStruct((B_GROUPS, CHUNK, CHUNK), F32), jax.ShapeDtypeStruct((1, B_W), F32)),
        grid=(seq // CHUNK,),
        in_specs=[pl.BlockSpec((CHUNK, 2 * B_W), lambda c: (c, 0)), full((1, B_W)), mats, mats, mats,
                  pl.BlockSpec((CHUNK, B_W), lambda c: (c, 0))],
        out_specs=(pl.BlockSpec((CHUNK, 2 * B_W), lambda c: (c, 0)), mats, mats, full((1, B_W))),
        scratch_shapes=[pltpu.VMEM((CHUNK, B_W), F32)],
        compiler_params=_cp("arbitrary"), name=name)(proj, gain.reshape(1, B_W), w_s, w_s_t, bias_b, dcat)


FFN_COLS = 128
FFN_ROWS = 32
SUBLANES = 8


def _window(ref, r0, first, last):
    cols = ref.shape[1]
    pad = jnp.zeros((SUBLANES, cols), F32)
    if first:
        return jnp.concatenate([pad, ref[pl.ds(0, FFN_ROWS + SUBLANES), :]], axis=0)
    if last:
        return jnp.concatenate([ref[pl.ds(r0 - SUBLANES, FFN_ROWS + SUBLANES), :], pad], axis=0)
    return ref[pl.ds(pl.multiple_of(r0 - SUBLANES, SUBLANES), FFN_ROWS + 2 * SUBLANES), :]


def _taps(win):
    mid = slice(SUBLANES, SUBLANES + FFN_ROWS)
    return pltpu.roll(win, 1, 0)[mid], win[mid], pltpu.roll(win, win.shape[0] - 1, 0)[mid]


def _row_steps(seq, step, carry):
    n = seq // FFN_ROWS
    carry = step(0, True, False, carry)
    carry = lax.fori_loop(1, n - 1, lambda i, c: step(pl.multiple_of(i * FFN_ROWS, FFN_ROWS), False, False, c), carry)
    return step(seq - FFN_ROWS, False, True, carry)


def _conv3(taps, w, b):
    prev, cur, nxt = taps
    return prev * w[0:1] + cur * w[1:2] + nxt * w[2:3] + b


def _fold(x):
    return jnp.sum(x.reshape(FFN_ROWS // SUBLANES, SUBLANES, x.shape[1]), axis=0)


FFN_FWD_COLS = 256


def _taps_whole(a):
    n = a.shape[0]
    rows = lax.broadcasted_iota(jnp.int32, a.shape, 0)
    return (jnp.where(rows == 0, 0.0, pltpu.roll(a, 1, 0)), a, jnp.where(rows == n - 1, 0.0, pltpu.roll(a, n - 1, 0)))


def ffn_act_fwd(a, conv_w, conv_b, name="ffn_act_fwd"):
    seq = a.shape[0]
    nb = FF // FFN_FWD_COLS

    def body(ag_ref, av_ref, wg_ref, wv_ref, bg_ref, bv_ref, o_ref):
        gate = _conv3(_taps_whole(ag_ref[...]), wg_ref[...], bg_ref[...])
        val = _conv3(_taps_whole(av_ref[...]), wv_ref[...], bv_ref[...])
        o_ref[...] = (_gelu(gate) * val).astype(o_ref.dtype)

    col = lambda rows, off: pl.BlockSpec((rows, FFN_FWD_COLS), lambda j: (0, j + off))
    cb = conv_b.reshape(1, 2 * FF)
    return pl.pallas_call(
        body, out_shape=jax.ShapeDtypeStruct((seq, FF), BF16), grid=(nb,),
        in_specs=[col(seq, 0), col(seq, nb), col(3, 0), col(3, nb), col(1, 0), col(1, nb)],
        out_specs=col(seq, 0), compiler_params=_cp("parallel"), name=name)(a, a, conv_w, conv_w, cb, cb)


def ffn_act_bwd(a, conv_w, conv_b, dact, name="ffn_act_bwd"):
    seq = a.shape[0]
    nb = FF // FFN_COLS

    def body(ag_ref, av_ref, wg_ref, wv_ref, bg_ref, bv_ref, d_ref, dag_ref, dav_ref, dwg_ref, dwv_ref, dbg_ref, dbv_ref,
             dcg_ref, dcv_ref):
        wg, wv, bg, bv = wg_ref[...], wv_ref[...], bg_ref[...], bv_ref[...]

        def conv_grads(r0, first, last, sums):
            g_taps = _taps(_window(ag_ref, r0, first, last))
            v_taps = _taps(_window(av_ref, r0, first, last))
            act, dact_dgate = _gelu_parts(_conv3(g_taps, wg, bg))
            d = d_ref[pl.ds(r0, FFN_ROWS), :].astype(F32)
            dcg = d * _conv3(v_taps, wv, bv) * dact_dgate
            dcv = d * act
            dcg_ref[pl.ds(r0, FFN_ROWS), :] = dcg
            dcv_ref[pl.ds(r0, FFN_ROWS), :] = dcv
            new = [_fold(dcg)] + [_fold(dcg * t) for t in g_taps] + [_fold(dcv)] + [_fold(dcv * t) for t in v_taps]
            return tuple(s + n for s, n in zip(sums, new))

        zero = jnp.zeros((SUBLANES, FFN_COLS), F32)
        sums = _row_steps(seq, conv_grads, (zero,) * 8)
        total = [jnp.sum(s, axis=0, keepdims=True) for s in sums]
        dbg_ref[...] = total[0]
        dbv_ref[...] = total[4]
        for k in range(3):
            dwg_ref[k:k + 1, :] = total[1 + k]
            dwv_ref[k:k + 1, :] = total[5 + k]

        def conv_transpose(r0, first, last, carry):
            for dc_ref, w, da_ref in ((dcg_ref, wg, dag_ref), (dcv_ref, wv, dav_ref)):
                prev, cur, nxt = _taps(_window(dc_ref, r0, first, last))
                da_ref[pl.ds(r0, FFN_ROWS), :] = (nxt * w[0:1] + cur * w[1:2] + prev * w[2:3]).astype(da_ref.dtype)
            return carry

        _row_steps(seq, conv_transpose, 0)

    col = lambda rows, off: pl.BlockSpec((rows, FFN_COLS), lambda j: (0, j + off))
    cb = conv_b.reshape(1, 2 * FF)
    dag, dav, dwg, dwv, dbg, dbv = pl.pallas_call(
        body, out_shape=(jax.ShapeDtypeStruct((seq, FF), BF16),) * 2 + (jax.ShapeDtypeStruct((3, FF), F32),) * 2
        + (jax.ShapeDtypeStruct((1, FF), F32),) * 2, grid=(nb,),
        in_specs=[col(seq, 0), col(seq, nb), col(3, 0), col(3, nb), col(1, 0), col(1, nb), col(seq, 0)],
        out_specs=(col(seq, 0), col(seq, 0), col(3, 0), col(3, 0), col(1, 0), col(1, 0)),
        scratch_shapes=[pltpu.VMEM((seq, FFN_COLS), F32), pltpu.VMEM((seq, FFN_COLS), F32)],
        compiler_params=_cp("parallel"), name=name)(a, a, conv_w, conv_w, cb, cb, dact)
    cat = lambda p, q: jnp.concatenate([p, q], axis=1)
    return cat(dag, dav), cat(dwg, dwv), cat(dbg, dbv)


def _adam_math(w, g, m, v):
    m = ADAM_B1 * m + (1.0 - ADAM_B1) * g
    v = ADAM_B2 * v + (1.0 - ADAM_B2) * (g * g)
    m_hat = m / (1.0 - ADAM_B1 ** ADAM_STEP)
    v_hat = v / (1.0 - ADAM_B2 ** ADAM_STEP)
    return -ADAM_LR * (m_hat / (jnp.sqrt(v_hat) + ADAM_EPS) + ADAM_WD * w), m, v


def _row_tile(rows, cols):
    return _pick(rows, (256, 128, 64)) if cols <= 1024 else _pick(rows, (128, 64))


def adamw_layer(w_all, m_all, v_all, layer, g, prev, name):
    n, rows, cols = w_all.shape
    tr = _row_tile(rows, cols)

    def body(w_ref, m_ref, v_ref, g_ref, *rest):
        go_ref, d_ref, mo_ref, vo_ref = rest[-4:]
        g_ = g_ref[...]
        d, m_, v_ = _adam_math(w_ref[...], g_, m_ref[...], v_ref[...])
        go_ref[...] = g_
        d_ref[...] = d
        mo_ref[...] = m_
        vo_ref[...] = v_

    lay = pl.BlockSpec((None, tr, cols), lambda i: (layer, i, 0))
    in_specs = [lay, lay, lay, pl.BlockSpec((tr, cols), lambda i: (i, 0))]
    args = [w_all, m_all, v_all, g]
    aliases = {}
    if prev is not None:
        in_specs += [pl.BlockSpec(memory_space=pl.ANY)] * 4
        args += list(prev)
        aliases = {4 + k: k for k in range(4)}
    return pl.pallas_call(
        body, out_shape=(jax.ShapeDtypeStruct(w_all.shape, F32),) * 4, grid=(rows // tr,),
        in_specs=in_specs, out_specs=(lay,) * 4, input_output_aliases=aliases,
        compiler_params=_cp("parallel"), name=name)(*args)


def adamw_flat(w, g, m, v, name="adamw_small"):
    rows, cols = w.shape
    tr = _pick(rows, (128, 8))

    def body(w_ref, g_ref, m_ref, v_ref, d_ref, mo_ref, vo_ref):
        d_ref[...], mo_ref[...], vo_ref[...] = _adam_math(w_ref[...], g_ref[...], m_ref[...], v_ref[...])

    spec = pl.BlockSpec((tr, cols), lambda i: (i, 0))
    return pl.pallas_call(
        body, out_shape=(jax.ShapeDtypeStruct(w.shape, F32),) * 3, grid=(rows // tr,),
        in_specs=[spec] * 4, out_specs=(spec,) * 3, compiler_params=_cp("parallel"), name=name)(w, g, m, v)


def pair_sum(dw, got, core, name):
    _, rows, cols = dw.shape
    half = rows // 2
    tr = _row_tile(half, cols)
    nrb = half // tr

    def body(c_ref, a_ref, b_ref, o_ref):
        o_ref[...] = (a_ref[...].astype(F32) + b_ref[...].astype(F32)).astype(o_ref.dtype)

    return pl.pallas_call(
        body, out_shape=jax.ShapeDtypeStruct((N_CHIPS, half, cols), BF16),
        grid_spec=pltpu.PrefetchScalarGridSpec(
            num_scalar_prefetch=1, grid=(N_CHIPS, nrb),
            in_specs=[pl.BlockSpec((None, tr, cols), lambda s, i, c_ref: (s, c_ref[0] * nrb + i, 0)),
                      pl.BlockSpec((None, tr, cols), lambda s, i, c_ref: (s, i, 0))],
            out_specs=pl.BlockSpec((None, tr, cols), lambda s, i, c_ref: (s, i, 0))),
        compiler_params=_cp("parallel", "parallel"), name=name)(core, dw, got)


def chip_sum(own, parts, place, name):
    _, half, cols = parts.shape
    tr = _row_tile(half, cols)
    nrb = half // tr

    def body(p_ref, own_ref, a_ref, b_ref, c_ref, o_ref):
        o_ref[...] = ((own_ref[...].astype(F32) + a_ref[...].astype(F32)) + b_ref[...].astype(F32)) + c_ref[...].astype(F32)

    def slot(k):
        return pl.BlockSpec((None, tr, cols), lambda i, p: (jnp.bitwise_xor(p[0], k), i, 0))

    return pl.pallas_call(
        body, out_shape=jax.ShapeDtypeStruct((2 * half, cols), F32),
        grid_spec=pltpu.PrefetchScalarGridSpec(
            num_scalar_prefetch=1, grid=(nrb,), in_specs=[slot(0), slot(1), slot(2), slot(3)],
            out_specs=pl.BlockSpec((tr, cols), lambda i, p: (p[1] * nrb + i, 0))),
        compiler_params=_cp("parallel"), name=name)(place, own, parts, parts, parts)


def cast_to_slot(w_all, layer, place, name, after=None):
    _, rows, cols = w_all.shape
    tr = _row_tile(rows, cols)
    tied = [] if after is None else [after]

    def body(p_ref, w_ref, *rest):
        o_ref = rest[-1]
        o_ref[...] = w_ref[...].astype(o_ref.dtype)

    return pl.pallas_call(
        body, out_shape=jax.ShapeDtypeStruct((N_CHIPS, rows, cols), BF16),
        grid_spec=pltpu.PrefetchScalarGridSpec(
            num_scalar_prefetch=1, grid=(rows // tr,),
            in_specs=[pl.BlockSpec((None, tr, cols), lambda i, p: (layer, i, 0))] + [ANY] * len(tied),
            out_specs=pl.BlockSpec((None, tr, cols), lambda i, p: (p[0], i, 0))),
        compiler_params=_cp("parallel"), name=name)(place, w_all, *tied)


ANY = pl.BlockSpec(memory_space=pl.ANY)


def _place():
    x, y, c = lax.axis_index("x"), lax.axis_index("y"), lax.axis_index("c")
    others = [(1 - x, y), (x, 1 - y), (1 - x, 1 - y)]
    return x, y, c, 2 * x + y, others


def _remote(src, dst, send_sem, recv_sem, dev):
    return pltpu.make_async_remote_copy(src_ref=src, dst_ref=dst, send_sem=send_sem, recv_sem=recv_sem,
                                        device_id=dev, device_id_type=MESH)


HBM = pl.BlockSpec(memory_space=pltpu.HBM)
SEM = pl.BlockSpec(memory_space=pltpu.SEMAPHORE)
EFFECT = pltpu.SideEffectType.DATAFLOW_SIDE_EFFECTING
TOKEN = jax.ShapeDtypeStruct((8, LANES), F32)


def _in_hbm(a):
    return pltpu.with_memory_space_constraint(a, pltpu.HBM)


def _gather_copies(bufs, send_sems, recv_sems):
    x, y, c, me, others = _place()
    out = []
    for w, buf in enumerate(bufs):
        half = buf.shape[1] // 2
        mine = pl.ds(c * half, half)
        for k, (ox, oy) in enumerate(others):
            sems = send_sems.at[3 * w + k], recv_sems.at[3 * w + k]
            out.append((_remote(buf.at[me, mine], buf.at[me, mine], *sems, (ox, oy, c)),
                        _remote(buf.at[me, mine], buf.at[2 * ox + oy, mine], *sems, (ox, oy, c))))
    return out


def gather_start(bufs, name):
    n = len(bufs)

    def body(*refs):
        ins, (send_sems, recv_sems), token = refs[:n], refs[n:n + 2], refs[-1]
        for sent, _ in _gather_copies(ins, send_sems, recv_sems):
            sent.start()
        token[...] = jnp.zeros_like(token)

    outs = pl.pallas_call(
        body, name=name,
        out_shape=(pltpu.SemaphoreType.DMA((3 * n,)), pltpu.SemaphoreType.DMA((3 * n,)),
                   *[pltpu.HBM(b.shape, b.dtype) for b in bufs], TOKEN),
        in_specs=[HBM] * n, out_specs=(SEM, SEM, *[HBM] * n, VM),
        input_output_aliases={w: 2 + w for w in range(n)},
        compiler_params=pltpu.CompilerParams(has_side_effects=EFFECT))(*[_in_hbm(b) for b in bufs])
    return outs[0], outs[1], list(outs[2:2 + n]), outs[-1]


def gather_wait(send_sems, recv_sems, bufs, after, name):
    n = len(bufs)

    def body(*refs):
        ins, (send_ref, recv_ref) = refs[:n], refs[n:n + 2]
        for sent, landed in _gather_copies(ins, send_ref, recv_ref):
            sent.wait_send()
            landed.wait_recv()

    return list(pl.pallas_call(
        body, name=name, out_shape=tuple(pltpu.HBM(b.shape, b.dtype) for b in bufs),
        in_specs=[HBM] * n + [SEM, SEM, ANY], out_specs=(HBM,) * n,
        input_output_aliases={w: w for w in range(n)},
        compiler_params=pltpu.CompilerParams(has_side_effects=EFFECT))(*bufs, send_sems, recv_sems, after))


def forward_to_sibling(bufs, name):
    n = len(bufs)

    def body(*refs):
        ins, outs = refs[:n], refs[n:2 * n]
        send_sems, recv_sems = refs[2 * n:]
        x, y, c, me, others = _place()
        sends = []
        for w in range(n):
            half = ins[w].shape[1] // 2
            mine = pl.ds(c * half, half)
            for k, (ox, oy) in enumerate(others):
                cp = _remote(ins[w].at[2 * ox + oy, mine], outs[w].at[2 * ox + oy, mine], send_sems.at[3 * w + k],
                             recv_sems.at[3 * w + k], (x, y, 1 - c))
                cp.start()
                sends.append(cp)
        for w in range(n):
            half = ins[w].shape[1] // 2
            theirs = pl.ds((1 - c) * half, half)
            for k, (ox, oy) in enumerate(others):
                passed = outs[w].at[2 * ox + oy, theirs]
                _remote(passed, passed, send_sems.at[3 * w + k], recv_sems.at[3 * w + k], (x, y, 1 - c)).wait_recv()
        for cp in sends:
            cp.wait_send()

    return list(pl.pallas_call(
        body, out_shape=tuple(jax.ShapeDtypeStruct(b.shape, b.dtype) for b in bufs),
        in_specs=[ANY] * n, out_specs=(ANY,) * n, input_output_aliases={w: w for w in range(n)},
        scratch_shapes=[pltpu.SemaphoreType.DMA((3 * n,)), pltpu.SemaphoreType.DMA((3 * n,))],
        name=name)(*bufs))


def swap_halves(grads, name):
    n = len(grads)

    def body(*refs):
        ins, outs = refs[:n], refs[n:2 * n]
        send_sems, recv_sems = refs[2 * n:]
        x, y, c, _, _ = _place()
        copies = []
        for w in range(n):
            half = ins[w].shape[1] // 2
            cp = _remote(ins[w].at[:, pl.ds((1 - c) * half, half)], outs[w], send_sems.at[w], recv_sems.at[w], (x, y, 1 - c))
            cp.start()
            copies.append(cp)
        for cp in copies:
            cp.wait()

    return pl.pallas_call(
        body, out_shape=tuple(jax.ShapeDtypeStruct((N_CHIPS, g.shape[1] // 2, g.shape[2]), g.dtype) for g in grads),
        in_specs=[ANY] * n, out_specs=(ANY,) * n,
        scratch_shapes=[pltpu.SemaphoreType.DMA((n,)), pltpu.SemaphoreType.DMA((n,))], name=name)(*grads)


def _swap_copies(grads, lands, send_sems, recv_sems):
    x, y, c, _, _ = _place()
    out = []
    for w, (g, land) in enumerate(zip(grads, lands)):
        half = g.shape[1] // 2
        out.append(_remote(g.at[:, pl.ds((1 - c) * half, half)], land, send_sems.at[w], recv_sems.at[w], (x, y, 1 - c)))
    return out


def swap_start(grads, name):
    n = len(grads)

    def body(*refs):
        ins, lands, (send_sems, recv_sems), token = refs[:n], refs[n:2 * n], refs[2 * n:2 * n + 2], refs[-1]
        for cp in _swap_copies(ins, lands, send_sems, recv_sems):
            cp.start()
        token[...] = jnp.zeros_like(token)

    shapes = [(N_CHIPS, g.shape[1] // 2, g.shape[2]) for g in grads]
    zones = [_in_hbm(lax.empty(s, g.dtype)) for s, g in zip(shapes, grads)]
    outs = pl.pallas_call(
        body, name=name,
        out_shape=(pltpu.SemaphoreType.DMA((n,)), pltpu.SemaphoreType.DMA((n,)),
                   *[pltpu.HBM(g.shape, g.dtype) for g in grads], *[pltpu.HBM(s, g.dtype) for s, g in zip(shapes, grads)],
                   TOKEN),
        in_specs=[HBM] * (2 * n), out_specs=(SEM, SEM, *[HBM] * (2 * n), VM),
        input_output_aliases={w: 2 + w for w in range(2 * n)},
        compiler_params=pltpu.CompilerParams(has_side_effects=EFFECT))(*[_in_hbm(g) for g in grads], *zones)
    return outs[0], outs[1], list(outs[2:2 + n]), list(outs[2 + n:2 + 2 * n]), outs[-1]


def swap_wait(send_sems, recv_sems, grads, lands, after, name):
    n = len(grads)

    def body(*refs):
        ins, zones, (send_ref, recv_ref) = refs[:n], refs[n:2 * n], refs[2 * n:2 * n + 2]
        for cp in _swap_copies(ins, zones, send_ref, recv_ref):
            cp.wait_send()
            cp.wait_recv()

    outs = pl.pallas_call(
        body, name=name, out_shape=tuple(pltpu.HBM(a.shape, a.dtype) for a in list(grads) + list(lands)),
        in_specs=[HBM] * (2 * n) + [SEM, SEM, ANY], out_specs=(HBM,) * (2 * n),
        input_output_aliases={w: w for w in range(2 * n)},
        compiler_params=pltpu.CompilerParams(has_side_effects=EFFECT))(*grads, *lands, send_sems, recv_sems, after)
    return list(outs[:n]), list(outs[n:])


def _exchange_copies(sums, lands, send_sems, recv_sems):
    x, y, c, me, others = _place()
    out = []
    for w, (src, land) in enumerate(zip(sums, lands)):
        for k, (ox, oy) in enumerate(others):
            sems = send_sems.at[3 * w + k], recv_sems.at[3 * w + k]
            out.append((_remote(src.at[2 * ox + oy], land.at[me], *sems, (ox, oy, c)),
                        _remote(src.at[2 * ox + oy], land.at[2 * ox + oy], *sems, (ox, oy, c))))
    return out


def exchange_start(sums, name):
    n = len(sums)

    def body(*refs):
        ins, lands, (send_sems, recv_sems), token = refs[:n], refs[n:2 * n], refs[2 * n:2 * n + 2], refs[-1]
        for sent, _ in _exchange_copies(ins, lands, send_sems, recv_sems):
            sent.start()
        token[...] = jnp.zeros_like(token)

    zones = [_in_hbm(lax.empty(s.shape, s.dtype)) for s in sums]
    outs = pl.pallas_call(
        body, name=name,
        out_shape=(pltpu.SemaphoreType.DMA((3 * n,)), pltpu.SemaphoreType.DMA((3 * n,)),
                   *[pltpu.HBM(s.shape, s.dtype) for s in sums] * 2, TOKEN),
        in_specs=[HBM] * (2 * n), out_specs=(SEM, SEM, *[HBM] * (2 * n), VM),
        input_output_aliases={w: 2 + w for w in range(2 * n)},
        compiler_params=pltpu.CompilerParams(has_side_effects=EFFECT))(*[_in_hbm(s) for s in sums], *zones)
    return outs[0], outs[1], list(outs[2:2 + n]), list(outs[2 + n:2 + 2 * n]), outs[-1]


def exchange_wait(send_sems, recv_sems, sums, lands, after, name):
    n = len(sums)

    def body(*refs):
        ins, zones, (send_ref, recv_ref) = refs[:n], refs[n:2 * n], refs[2 * n:2 * n + 2]
        for sent, landed in _exchange_copies(ins, zones, send_ref, recv_ref):
            sent.wait_send()
            landed.wait_recv()

    outs = pl.pallas_call(
        body, name=name, out_shape=tuple(pltpu.HBM(s.shape, s.dtype) for s in sums) * 2,
        in_specs=[HBM] * (2 * n) + [SEM, SEM, ANY], out_specs=(HBM,) * (2 * n),
        input_output_aliases={w: w for w in range(2 * n)},
        compiler_params=pltpu.CompilerParams(has_side_effects=EFFECT))(*sums, *lands, send_sems, recv_sems, after)
    return list(outs[:n]), list(outs[n:])


def join_halves(grads, name):
    n = len(grads)

    def body(*refs):
        ins, outs = refs[:n], refs[n:2 * n]
        send_sems, recv_sems = refs[2 * n:]
        x, y, c, _, _ = _place()
        copies = []
        for w in range(n):
            half = ins[w].shape[0] // 2
            mine = pl.ds(c * half, half)
            cp = _remote(ins[w].at[mine], outs[w].at[mine], send_sems.at[w], recv_sems.at[w], (x, y, 1 - c))
            cp.start()
            copies.append(cp)
        for w, cp in enumerate(copies):
            half = ins[w].shape[0] // 2
            theirs = outs[w].at[pl.ds((1 - c) * half, half)]
            _remote(theirs, theirs, send_sems.at[w], recv_sems.at[w], (x, y, 1 - c)).wait_recv()
            cp.wait_send()

    return pl.pallas_call(
        body, out_shape=tuple(jax.ShapeDtypeStruct(g.shape, g.dtype) for g in grads),
        in_specs=[ANY] * n, out_specs=(ANY,) * n, input_output_aliases={w: w for w in range(n)},
        scratch_shapes=[pltpu.SemaphoreType.DMA((n,)), pltpu.SemaphoreType.DMA((n,))],
        name=name)(*grads)


VM = pl.BlockSpec(memory_space=pltpu.VMEM)


def small_allgather(buf, name="small_allgather"):
    def body(in_ref, out_ref, send_sems, recv_sems):
        x, y, c, me, others = _place()
        out_ref[me] = in_ref[...]
        copies = []
        for k, (ox, oy) in enumerate(others):
            cp = _remote(in_ref, out_ref.at[me], send_sems.at[k], recv_sems.at[k], (ox, oy, c))
            cp.start()
            copies.append(cp)
        for k, (ox, oy) in enumerate(others):
            landed = out_ref.at[2 * ox + oy]
            _remote(landed, landed, send_sems.at[k], recv_sems.at[k], (ox, oy, c)).wait_recv()
        for cp in copies:
            cp.wait_send()

    return pl.pallas_call(
        body, out_shape=jax.ShapeDtypeStruct((N_CHIPS,) + buf.shape, buf.dtype), in_specs=[VM], out_specs=VM,
        scratch_shapes=[pltpu.SemaphoreType.DMA((3,)), pltpu.SemaphoreType.DMA((3,))],
        compiler_params=pltpu.CompilerParams(vmem_limit_bytes=V7X_VMEM_LIMIT), name=name)(buf)


def small_allreduce(buf, name="small_allreduce"):
    def body(in_ref, out_ref, sib_ref, slot_ref, send_sems, recv_sems):
        x, y, c, me, others = _place()
        cp = _remote(in_ref, sib_ref, send_sems.at[3], recv_sems.at[3], (x, y, 1 - c))
        cp.start()
        cp.wait()
        slot_ref[me] = in_ref[...] + sib_ref[...]
        copies = []
        for k, (ox, oy) in enumerate(others):
            cp = _remote(slot_ref.at[me], slot_ref.at[me], send_sems.at[k], recv_sems.at[k], (ox, oy, c))
            cp.start()
            copies.append(cp)
        for k, (ox, oy) in enumerate(others):
            landed = slot_ref.at[2 * ox + oy]
            _remote(landed, landed, send_sems.at[k], recv_sems.at[k], (ox, oy, c)).wait_recv()
        for cp in copies:
            cp.wait_send()
        out_ref[...] = ((slot_ref[0] + slot_ref[1]) + slot_ref[2]) + slot_ref[3]

    return pl.pallas_call(
        body, out_shape=jax.ShapeDtypeStruct(buf.shape, buf.dtype), in_specs=[VM], out_specs=VM,
        scratch_shapes=[pltpu.VMEM(buf.shape, buf.dtype), pltpu.VMEM((N_CHIPS,) + buf.shape, buf.dtype),
                        pltpu.SemaphoreType.DMA((4,)), pltpu.SemaphoreType.DMA((4,))],
        compiler_params=pltpu.CompilerParams(vmem_limit_bytes=V7X_VMEM_LIMIT), name=name)(buf)


def _pack_rows(arrays, row_multiple):
    flat = jnp.concatenate([a.reshape(-1) for a in arrays])
    rows = -(-flat.shape[0] // (LANES * row_multiple)) * row_multiple
    return jnp.pad(flat, (0, rows * LANES - flat.shape[0])).reshape(rows, LANES)


def _unpack_rows(buf, shapes):
    flat = buf.reshape(-1)
    out, at = [], 0
    for s in shapes:
        n = math.prod(s)
        out.append(flat[at:at + n].reshape(s))
        at += n
    return out


def _mixer_weights(i):
    j = i // 2
    mixer = "a" if i % 2 == 0 else "b"
    return [("w_mem_kv", i), (mixer + "_w_in", j), (mixer + "_w_out", j)]


def _ffn_weights(i):
    return [("ffn_w_up", i), ("ffn_w_down", i)]


def _mixer_fwd(i, x, mem, w, small, after):
    is_a = i % 2 == 0
    j = i // 2
    wkv, win, wout = w
    wkv = wkv.reshape(1, D_MODEL, 2 * MEM_W)
    h1 = rms_fwd(x, small["mix_norm_g"][i], BF16, name=f"mix_norm{i}", after=after)
    mem_n = rms_fwd(mem, small["mem_norm_g"][i], BF16, name=f"mem_norm{i}")
    kv = mm_nn(mem_n, wkv, F32, name=f"mem_kv{i}")
    proj = mm_nn(h1, win, F32, name=f"in_proj{i}")
    saved = dict(x0=x, h1=h1, mem_n=mem_n, kv=kv, proj=proj)
    if is_a:
        outs, lses = zip(*[attn_fwd(proj, g, name=f"attn_fwd{i}_{g}") for g in range(3)])
        comb, lse = attn_combine(outs, lses, name=f"attn_combine{i}")
        mem_out = mem_fwd(proj, 3 * A_QKV_W, kv, name=f"mem_fwd{i}")
        cat = jnp.concatenate([comb.astype(BF16), mem_out], axis=1)
        saved.update(comb=comb, lse=lse)
    else:
        wout = wout.reshape(1, B_W + MEM_W, D_MODEL)
        tok = sgu_fwd(proj, small["b_v_norm_g"][j], small["b_w_s"][j], small["bias_b"][j], name=f"sgu_fwd{i}")
        mem_out = mem_fwd(proj, 2 * B_W, kv, name=f"mem_fwd{i}")
        cat = jnp.concatenate([tok, mem_out], axis=1)
    x1 = mm_nn(cat, wout, F32, res=x, name=f"out_proj{i}")
    saved.update(cat=cat)
    return x1, saved


def _ffn_fwd(i, x1, w, small, after):
    wup, wdn = w
    h2 = rms_fwd(x1, small["ffn_norm_g"][i], BF16, name=f"ffn_norm{i}", after=after)
    a = mm_nn(h2, wup, F32, name=f"ffn_up{i}")
    act = ffn_act_fwd(a, small["ffn_conv_w"][i], small["ffn_conv_b"][i], name=f"ffn_act{i}")
    x2 = mm_nn(act, wdn.reshape(1, FF, D_MODEL), F32, res=x1, name=f"ffn_down{i}")
    return x2, dict(x1=x1, h2=h2, a=a, act=act)


def _ffn_bwd(i, dx2, w, small, sv, after):
    wup, wdn = w
    sg = {}
    dact = mm_nt(dx2, wdn.reshape(1, FF, D_MODEL), F32, name=f"d_act{i}", after=after)
    d_wdn = mm_tn(sv["act"], dx2, 1, BF16, name=f"d_wdown{i}").reshape(N_CHIPS, FF // N_CHIPS, D_MODEL)
    da, sg["ffn_conv_w"], sg["ffn_conv_b"] = ffn_act_bwd(sv["a"], small["ffn_conv_w"][i], small["ffn_conv_b"][i], dact,
                                                          name=f"ffn_act_bwd{i}")
    d_wup = mm_tn(sv["h2"], da, N_CHIPS, BF16, name=f"d_wup{i}")
    dh2 = mm_nt(da, wup, F32, name=f"d_h2_{i}")
    dx1, sg["ffn_norm_g"] = rms_bwd(sv["x1"], small["ffn_norm_g"][i], dh2, dres=dx2, name=f"ffn_norm_bwd{i}")
    return dx1, [d_wup, d_wdn], sg


def _mixer_bwd(i, dx1, mem, w, small, sv, after):
    is_a = i % 2 == 0
    j = i // 2
    wkv, win, wout = w
    wkv = wkv.reshape(1, D_MODEL, 2 * MEM_W)
    sg = {}
    proj, kv = sv["proj"], sv["kv"]
    if is_a:
        dcat = mm_nt(dx1, wout, F32, name=f"d_cat{i}", after=after)
        d_wout = mm_tn(sv["cat"], dx1, N_CHIPS, BF16, name=f"d_wout{i}")
        dqm, dkv = mem_bwd(proj, 3 * A_QKV_W, kv, dcat, A_OUT_W, name=f"mem_bwd{i}")
        parts = [attn_bwd(proj, dcat, sv["comb"], sv["lse"], g, name=f"attn_bwd{i}_{g}") for g in range(3)]
        dproj = jnp.concatenate([parts[g][p] for p in range(3) for g in range(3)] + [dqm], axis=1)
    else:
        dcat = mm_nt(dx1, wout.reshape(1, B_W + MEM_W, D_MODEL), F32, name=f"d_cat{i}", after=after)
        d_wout = mm_tn(sv["cat"], dx1, 1, BF16, name=f"d_wout{i}").reshape(N_CHIPS, (B_W + MEM_W) // N_CHIPS, D_MODEL)
        dqm, dkv = mem_bwd(proj, 2 * B_W, kv, dcat, B_W, name=f"mem_bwd{i}")
        w_s = small["b_w_s"][j]
        duv, sg["b_w_s"], dmix, sg["b_v_norm_g"] = sgu_bwd(proj, small["b_v_norm_g"][j], w_s, jnp.swapaxes(w_s, 1, 2),
                                                           small["bias_b"][j], dcat, name=f"sgu_bwd{i}")
        sg["b_s_bias"] = jnp.sum(dmix, axis=-1)
        dproj = jnp.concatenate([duv, dqm], axis=1)
    d_wkv = mm_tn(sv["mem_n"], dkv, 1, BF16, name=f"d_wkv{i}").reshape(N_CHIPS, D_MODEL // N_CHIPS, 2 * MEM_W)
    dmem_n = mm_nt(dkv, wkv, F32, name=f"d_mem_n{i}")
    _, sg["mem_norm_g"] = rms_bwd(mem, small["mem_norm_g"][i], dmem_n, name=f"mem_norm_bwd{i}")
    d_win = mm_tn(sv["h1"], dproj, N_CHIPS, BF16, name=f"d_win{i}")
    dh1 = mm_nt(dproj, win, F32, name=f"d_h1_{i}")
    dx0, sg["mix_norm_g"] = rms_bwd(sv["x0"], small["mix_norm_g"][i], dh1, dres=dx1, name=f"mix_norm_bwd{i}")
    return dx0, [d_wkv, d_win, d_wout], sg


def _exchange_begin(grads, got, place, tag):
    sums = [pair_sum(g, o, place[1:], name=f"pair_sum{tag}_{k}") for k, (g, o) in enumerate(zip(grads, got))]
    send_sems, recv_sems, sums, lands, token = exchange_start(sums, name=f"exchange_start{tag}")
    return (send_sems, recv_sems, sums, lands), token


def _reduce_finish(started, place, after, tag):
    sums, parts = exchange_wait(*started, after, name=f"exchange_wait{tag}")
    halves = [chip_sum(s, p, place, name=f"chip_sum{tag}_{k}") for k, (s, p) in enumerate(zip(sums, parts))]
    return join_halves(halves, name=f"join_halves{tag}")


SMALL_SHARDED = ("b_v_norm_g", "ffn_conv_w")
SMALL_FULL_SHAPES = dict(mix_norm_g=(D_MODEL,), ffn_norm_g=(D_MODEL,), mem_norm_g=(D_MODEL,), b_v_norm_g=(B_W,),
                         b_w_s=(B_GROUPS, CHUNK, CHUNK), b_s_bias=(B_GROUPS, CHUNK), ffn_conv_w=(3, 2 * FF),
                         ffn_conv_b=(2 * FF,))
BIG = ("w_mem_kv", "a_w_in", "a_w_out", "b_w_in", "b_w_out", "ffn_w_up", "ffn_w_down")
WEIGHT_ORDER = ("mix_norm_g", "ffn_norm_g", "mem_norm_g", "w_mem_kv", "a_w_in", "a_w_out", "b_w_in", "b_v_norm_g", "b_w_s",
                "b_s_bias", "b_w_out", "ffn_w_up", "ffn_conv_w", "ffn_conv_b", "ffn_w_down", "final_norm_g")


def kernel(x, mem, mix_norm_g, ffn_norm_g, mem_norm_g, w_mem_kv, a_w_in, a_w_out, b_w_in, b_v_norm_g, b_w_s, b_s_bias, b_w_out, ffn_w_up, ffn_conv_w, ffn_conv_b, ffn_w_down, final_norm_g, loss_target, m_mix_norm_g, m_ffn_norm_g, m_mem_norm_g, m_w_mem_kv, m_a_w_in, m_a_w_out, m_b_w_in, m_b_v_norm_g, m_b_w_s, m_b_s_bias, m_b_w_out, m_ffn_w_up, m_ffn_conv_w, m_ffn_conv_b, m_ffn_w_down, m_final_norm_g, v_mix_norm_g, v_ffn_norm_g, v_mem_norm_g, v_w_mem_kv, v_a_w_in, v_a_w_out, v_b_w_in, v_b_v_norm_g, v_b_w_s, v_b_s_bias, v_b_w_out, v_ffn_w_up, v_ffn_conv_w, v_ffn_conv_b, v_ffn_w_down, v_final_norm_g):
    weights = dict(mix_norm_g=mix_norm_g, ffn_norm_g=ffn_norm_g, mem_norm_g=mem_norm_g, w_mem_kv=w_mem_kv, a_w_in=a_w_in,
                   a_w_out=a_w_out, b_w_in=b_w_in, b_v_norm_g=b_v_norm_g, b_w_s=b_w_s, b_s_bias=b_s_bias, b_w_out=b_w_out,
                   ffn_w_up=ffn_w_up, ffn_conv_w=ffn_conv_w, ffn_conv_b=ffn_conv_b, ffn_w_down=ffn_w_down,
                   final_norm_g=final_norm_g)
    mom1 = dict(mix_norm_g=m_mix_norm_g, ffn_norm_g=m_ffn_norm_g, mem_norm_g=m_mem_norm_g, w_mem_kv=m_w_mem_kv,
                a_w_in=m_a_w_in, a_w_out=m_a_w_out, b_w_in=m_b_w_in, b_v_norm_g=m_b_v_norm_g, b_w_s=m_b_w_s,
                b_s_bias=m_b_s_bias, b_w_out=m_b_w_out, ffn_w_up=m_ffn_w_up, ffn_conv_w=m_ffn_conv_w,
                ffn_conv_b=m_ffn_conv_b, ffn_w_down=m_ffn_w_down, final_norm_g=m_final_norm_g)
    mom2 = dict(mix_norm_g=v_mix_norm_g, ffn_norm_g=v_ffn_norm_g, mem_norm_g=v_mem_norm_g, w_mem_kv=v_w_mem_kv,
                a_w_in=v_a_w_in, a_w_out=v_a_w_out, b_w_in=v_b_w_in, b_v_norm_g=v_b_v_norm_g, b_w_s=v_b_w_s,
                b_s_bias=v_b_s_bias, b_w_out=v_b_w_out, ffn_w_up=v_ffn_w_up, ffn_conv_w=v_ffn_conv_w,
                ffn_conv_b=v_ffn_conv_b, ffn_w_down=v_ffn_w_down, final_norm_g=v_final_norm_g)
    chip = 2 * lax.axis_index("x") + lax.axis_index("y")
    place = jnp.stack([chip, lax.axis_index("c")]).astype(jnp.int32)
    x0, mem0, target = x[0], mem[0], loss_target[0]
    depth = DEPTH

    n_cw, n_vg = ffn_conv_w.size, b_v_norm_g.size
    gathered = small_allgather(_pack_rows([ffn_conv_w, b_v_norm_g], 8)).reshape(N_CHIPS, -1)
    conv_w_full = gathered[:, :n_cw].reshape(N_CHIPS, DEPTH, 3, 2 * FF // N_CHIPS).transpose(1, 2, 0, 3).reshape(DEPTH, 3, 2 * FF)
    vgain_full = gathered[:, n_cw:n_cw + n_vg].reshape(N_CHIPS, 2, B_W // N_CHIPS).transpose(1, 0, 2).reshape(2, B_W)
    small = dict(mix_norm_g=mix_norm_g, ffn_norm_g=ffn_norm_g, mem_norm_g=mem_norm_g, b_w_s=b_w_s, ffn_conv_b=ffn_conv_b,
                 ffn_conv_w=conv_w_full, b_v_norm_g=vgain_full,
                 bias_b=jnp.broadcast_to(b_s_bias[..., None], b_s_bias.shape + (CHUNK,)))

    def start_gather(group, tag, after=None):
        bufs = [cast_to_slot(weights[n], l, place, name=f"cast_{n}{l}", after=after) for n, l in group]
        return gather_start(bufs, name=f"gather_start_{tag}")

    def finish_gather(started, after, tag):
        send_sems, recv_sems, bufs, _ = started
        return forward_to_sibling(gather_wait(send_sems, recv_sems, bufs, after, name=f"gather_wait_{tag}"),
                                  name=f"forward_{tag}")

    w_mix, w_ffn, saved_mix, saved_ffn = [], [], [], []
    flying_mix = start_gather(_mixer_weights(0), "m0", after=gathered)
    flying_ffn = start_gather(_ffn_weights(0), "f0", after=flying_mix[3])
    h, tie = x0, flying_ffn[3]
    for i in range(depth):
        w_mix.append(finish_gather(flying_mix, tie if i == 0 else h, f"m{i}"))
        if i + 1 < depth:
            flying_mix = start_gather(_mixer_weights(i + 1), f"m{i + 1}", after=w_mix[i][0])
            tie = flying_mix[3]
        x1, sv = _mixer_fwd(i, h, mem0, w_mix[i], small, tie)
        saved_mix.append(sv)
        w_ffn.append(finish_gather(flying_ffn, x1, f"f{i}"))
        if i + 1 < depth:
            flying_ffn = start_gather(_ffn_weights(i + 1), f"f{i + 1}", after=w_ffn[i][0])
            tie = flying_ffn[3]
        h, sv = _ffn_fwd(i, x1, w_ffn[i], small, tie)
        saved_ffn.append(sv)
    loss_row, dh, d_final = final_loss(h, final_norm_g, target)
    loss = lax.psum(loss_row[0, 0], ("x", "y", "c"))

    names = [n for n in WEIGHT_ORDER if n not in BIG]
    small_g = {n: [None] * weights[n].shape[0] for n in names if n != "final_norm_g"}
    big_out = {n: None for n in BIG}

    def keep_small(i, sg):
        for n, g in sg.items():
            small_g[n][i if len(small_g[n]) == depth else i // 2] = g.reshape(SMALL_FULL_SHAPES[n])

    def finish_reduce(started, group, after, tag):
        for (n, l), g in zip(group, _reduce_finish(started, place, after, tag)):
            big_out[n] = adamw_layer(weights[n], mom1[n], mom2[n], l, g, big_out[n], name=f"adamw_{n}{l}")

    half_layers = 2 * depth
    swapping, exchanging, tie = None, [], None
    for k in range(half_layers):
        i = depth - 1 - k // 2
        if k % 2 == 0:
            dh, big_g, sg = _ffn_bwd(i, dh, w_ffn[i], small, saved_ffn[i], tie)
            group, tag = _ffn_weights(i), f"f{i}"
        else:
            dh, big_g, sg = _mixer_bwd(i, dh, mem0, w_mix[i], small, saved_mix[i], tie)
            group, tag = _mixer_weights(i), f"m{i}"
        keep_small(i, sg)
        started_now, swap_now, tokens = [], None, []
        if k < half_layers - 2:
            *swap_now, token = swap_start(big_g, name=f"swap_start_{tag}")
            swap_now = (swap_now, group, tag)
        else:
            started, token = _exchange_begin(big_g, swap_halves(big_g, name=f"swap_halves_{tag}"), place, tag)
            started_now.append((started, group, tag))
        tokens.append(token)
        if swapping is not None:
            swap_args, old_group, old_tag = swapping
            grads, got = swap_wait(*swap_args, dh, name=f"swap_wait_{old_tag}")
            started, token = _exchange_begin(grads, got, place, old_tag)
            started_now.append((started, old_group, old_tag))
            tokens.append(token)
        tie = sum(tokens[1:], tokens[0])
        for started, old_group, old_tag in exchanging:
            finish_reduce(started, old_group, dh, old_tag)
        swapping, exchanging = swap_now, started_now
    for started, old_group, old_tag in exchanging:
        finish_reduce(started, old_group, big_out["ffn_w_down"][0], old_tag)

    full_g = {n: (d_final.reshape(-1) if n == "final_norm_g" else jnp.stack(small_g[n])) for n in names}
    shapes = [full_g[n].shape for n in names]
    summed = dict(zip(names, _unpack_rows(small_allreduce(_pack_rows([full_g[n] for n in names], 8)), shapes)))
    for n in SMALL_SHARDED:
        width = weights[n].shape[-1]
        summed[n] = lax.dynamic_slice_in_dim(summed[n], chip * width, width, axis=summed[n].ndim - 1)
    own_shapes = [weights[n].shape for n in names]
    pack = lambda d: _pack_rows([d[n] for n in names], 128)
    small_out = [_unpack_rows(b, own_shapes) for b in adamw_flat(pack(weights), pack(summed), pack(mom1), pack(mom2))]
    outs = {}
    for k, n in enumerate(names):
        outs[n] = (summed[n], small_out[0][k], small_out[1][k], small_out[2][k])
    outs.update(big_out)
    return (loss, dh[None], *[outs[n][0] for n in WEIGHT_ORDER], *[outs[n][1] for n in WEIGHT_ORDER],
            *[outs[n][2] for n in WEIGHT_ORDER], *[outs[n][3] for n in WEIGHT_ORDER])
```

```python
import functools
import math

import numpy as np
import jax
import jax.numpy as jnp
from jax import lax
from jax.experimental import pallas as pl
from jax.experimental.pallas import tpu as pltpu

F32 = jnp.float32
BF16 = jnp.bfloat16
MESH = pl.DeviceIdType.MESH

D_MODEL = 2048
SEQ = 2048
DEPTH = 4
EPS = 1e-6
NEG = -1e30
HEAD = 128
A_PATTERNS = ((128, 1), (512, 4), (2048, 16))
A_QKV_W = 1536
A_OUT_W = 512
A_IN = 5120
QBLK = 128
N_SIDE = 64
CHUNK = 128
B_GROUPS = 12
B_W = 1536
B_IN = 3584
MEM_LEN = 256
MEM_HEADS = 4
MEM_W = 512
FF = 5632
ADAM_LR, ADAM_B1, ADAM_B2, ADAM_EPS, ADAM_WD, ADAM_STEP = 0.001, 0.9, 0.999, 1e-08, 0.01, 10
N_CHIPS = 4

LANES = 128
V7X_VMEM_LIMIT = 56 * 1024 * 1024


def _cp(*sem):
    return pltpu.CompilerParams(dimension_semantics=sem, vmem_limit_bytes=V7X_VMEM_LIMIT)


def _pick(dim, prefs):
    for p in prefs:
        if dim % p == 0:
            return p
    raise ValueError(f"no tile for {dim} in {prefs}")


def _gelu_parts(x):
    cdf = 0.5 * (1.0 + lax.erf(x * (1.0 / math.sqrt(2.0))))
    pdf = jnp.exp(-0.5 * x * x) * (1.0 / math.sqrt(2.0 * math.pi))
    return x * cdf, cdf + x * pdf


def _gelu(x):
    return 0.5 * x * (1.0 + lax.erf(x * (1.0 / math.sqrt(2.0))))


TM_PREFS = (1024, 512, 256, 128)
TN_PREFS = (1408, 1280, 1024, 896, 512, 256, 128)
TK_PREFS = (2816, 2048, 1408, 1280, 1024, 896, 512, 256, 128)


def _mm_body(nk, dims, has_res):
    def body(*refs):
        if has_res:
            a_ref, b_ref, r_ref, o_ref = refs[:4]
        else:
            a_ref, b_ref, o_ref = refs[:3]
            r_ref = None
        part = lax.dot_general(a_ref[...].astype(BF16), b_ref[...].astype(BF16), dims,
                               preferred_element_type=F32)
        if nk == 1:
            if has_res:
                part = part + r_ref[...]
            o_ref[...] = part.astype(o_ref.dtype)
            return
        acc_ref = refs[-1]
        k = pl.program_id(2)

        @pl.when(k == 0)
        def _():
            acc_ref[...] = part

        @pl.when(k > 0)
        def _():
            acc_ref[...] += part

        @pl.when(k == nk - 1)
        def _():
            tot = acc_ref[...]
            if has_res:
                tot = tot + r_ref[...]
            o_ref[...] = tot.astype(o_ref.dtype)
    return body


def mm_nn(a, w, out_dtype, res=None, name="mm_nn"):
    m, kw = a.shape
    ns_, kw2, nsz = w.shape
    assert kw == kw2
    n = ns_ * nsz
    tm, tn, tk = _pick(m, TM_PREFS), _pick(nsz, TN_PREFS), _pick(kw, TK_PREFS)
    nb, nk = nsz // tn, kw // tk
    in_specs = [pl.BlockSpec((tm, tk), lambda i, j, k: (i, k)),
                pl.BlockSpec((None, tk, tn), lambda i, j, k: (j // nb, k, j % nb))]
    args = [a, w]
    if res is not None:
        in_specs.append(pl.BlockSpec((tm, tn), lambda i, j, k: (i, j)))
        args.append(res)
    return pl.pallas_call(
        _mm_body(nk, (((1,), (0,)), ((), ())), res is not None),
        out_shape=jax.ShapeDtypeStruct((m, n), out_dtype),
        grid=(m // tm, n // tn, nk), in_specs=in_specs,
        out_specs=pl.BlockSpec((tm, tn), lambda i, j, k: (i, j)),
        scratch_shapes=[pltpu.VMEM((tm, tn), F32)] if nk > 1 else [],
        compiler_params=_cp("parallel", "parallel", "arbitrary"), name=name)(*args)


def mm_nt(g, w, out_dtype, name="mm_nt", after=None):
    m, n = g.shape
    ns_, kw, nsz = w.shape
    assert n == ns_ * nsz
    tm, tn, tk = _pick(m, TM_PREFS), _pick(kw, TN_PREFS), _pick(nsz, TK_PREFS)
    nb, nk = nsz // tk, n // tk
    body = _mm_body(nk, (((1,), (1,)), ((), ())), False)
    tied = [] if after is None else [after]
    return pl.pallas_call(
        (lambda g_ref, w_ref, *rest: body(g_ref, w_ref, *rest[len(tied):])),
        out_shape=jax.ShapeDtypeStruct((m, kw), out_dtype),
        grid=(m // tm, kw // tn, nk),
        in_specs=[pl.BlockSpec((tm, tk), lambda i, j, k: (i, k)),
                  pl.BlockSpec((None, tn, tk), lambda i, j, k: (k // nb, j, k % nb))] + [ANY] * len(tied),
        out_specs=pl.BlockSpec((tm, tn), lambda i, j, k: (i, j)),
        scratch_shapes=[pltpu.VMEM((tm, tn), F32)] if nk > 1 else [],
        compiler_params=_cp("parallel", "parallel", "arbitrary"), name=name)(g, w, *tied)


def mm_tn(a, g, n_shards, out_dtype, name="mm_tn"):
    t, kw = a.shape
    t2, n = g.shape
    assert t == t2
    nsz = n // n_shards
    tm, tn, tk = _pick(kw, TM_PREFS), _pick(nsz, TN_PREFS), _pick(t, TK_PREFS)
    nb, nk = nsz // tn, t // tk
    return pl.pallas_call(
        _mm_body(nk, (((0,), (0,)), ((), ())), False),
        out_shape=jax.ShapeDtypeStruct((n_shards, kw, nsz), out_dtype),
        grid=(kw // tm, n // tn, nk),
        in_specs=[pl.BlockSpec((tk, tm), lambda i, j, k: (k, i)),
                  pl.BlockSpec((tk, tn), lambda i, j, k: (k, j))],
        out_specs=pl.BlockSpec((None, tm, tn), lambda i, j, k: (j // nb, i, j % nb)),
        scratch_shapes=[pltpu.VMEM((tm, tn), F32)] if nk > 1 else [],
        compiler_params=_cp("parallel", "parallel", "arbitrary"), name=name)(a, g)


ROW_TILE = 256


def _rms_stats(x):
    r = lax.rsqrt(jnp.mean(x * x, axis=-1, keepdims=True) + EPS)
    return r, x * r


def _rms_back(xh, r, g, dh):
    u = dh * g
    return r * (u - xh * jnp.mean(u * xh, axis=-1, keepdims=True))


def rms_fwd(x, g, out_dtype, name="rms_fwd", after=None):
    rows, d = x.shape
    tr = _pick(rows, (ROW_TILE, 128))
    tied = [] if after is None else [after]

    def body(x_ref, g_ref, *rest):
        o_ref = rest[-1]
        _, xh = _rms_stats(x_ref[...])
        o_ref[...] = (xh * g_ref[...]).astype(o_ref.dtype)

    return pl.pallas_call(
        body, out_shape=jax.ShapeDtypeStruct((rows, d), out_dtype), grid=(rows // tr,),
        in_specs=[pl.BlockSpec((tr, d), lambda i: (i, 0)), pl.BlockSpec((1, d), lambda i: (0, 0))] + [ANY] * len(tied),
        out_specs=pl.BlockSpec((tr, d), lambda i: (i, 0)),
        compiler_params=_cp("parallel"), name=name)(x, g.reshape(1, d), *tied)


def rms_bwd(x, g, dh, dres=None, name="rms_bwd"):
    rows, d = x.shape
    tr = _pick(rows, (ROW_TILE, 128))
    has_res = dres is not None

    def body(*refs):
        if has_res:
            x_ref, g_ref, dh_ref, dres_ref, dx_ref, dg_ref = refs
        else:
            x_ref, g_ref, dh_ref, dx_ref, dg_ref = refs
        r, xh = _rms_stats(x_ref[...])
        dh_ = dh_ref[...].astype(F32)
        part = jnp.sum(dh_ * xh, axis=0, keepdims=True)

        @pl.when(pl.program_id(0) == 0)
        def _():
            dg_ref[...] = part

        @pl.when(pl.program_id(0) > 0)
        def _():
            dg_ref[...] += part

        dx = _rms_back(xh, r, g_ref[...], dh_)
        if has_res:
            dx = dx + dres_ref[...]
        dx_ref[...] = dx

    row_spec = pl.BlockSpec((tr, d), lambda i: (i, 0))
    vec_spec = pl.BlockSpec((1, d), lambda i: (0, 0))
    args = [x, g.reshape(1, d), dh] + ([dres] if has_res else [])
    return pl.pallas_call(
        body, out_shape=(jax.ShapeDtypeStruct((rows, d), F32), jax.ShapeDtypeStruct((1, d), F32)),
        grid=(rows // tr,), in_specs=[row_spec, vec_spec, row_spec] + ([row_spec] if has_res else []),
        out_specs=(row_spec, vec_spec), compiler_params=_cp("arbitrary"), name=name)(*args)


def final_loss(x, g, target, name="final_loss"):
    rows, d = x.shape
    tr = _pick(rows, (ROW_TILE, 128))

    def body(x_ref, g_ref, t_ref, loss_ref, dx_ref, dg_ref):
        r, xh = _rms_stats(x_ref[...])
        gain = g_ref[...]
        err = xh * gain - t_ref[...]
        sq = jnp.sum(jnp.sum(err * err, axis=1, keepdims=True), axis=0, keepdims=True) * (0.5 / d)
        dy = err * (1.0 / d)
        part = jnp.sum(dy * xh, axis=0, keepdims=True)

        @pl.when(pl.program_id(0) == 0)
        def _():
            dg_ref[...] = part
            loss_ref[...] = jnp.broadcast_to(sq, loss_ref.shape)

        @pl.when(pl.program_id(0) > 0)
        def _():
            dg_ref[...] += part
            loss_ref[...] += jnp.broadcast_to(sq, loss_ref.shape)

        dx_ref[...] = _rms_back(xh, r, gain, dy)

    row_spec = pl.BlockSpec((tr, d), lambda i: (i, 0))
    vec_spec = pl.BlockSpec((1, d), lambda i: (0, 0))
    return pl.pallas_call(
        body, out_shape=(jax.ShapeDtypeStruct((1, LANES), F32), jax.ShapeDtypeStruct((rows, d), F32),
                         jax.ShapeDtypeStruct((1, d), F32)),
        grid=(rows // tr,), in_specs=[row_spec, vec_spec, row_spec],
        out_specs=(pl.BlockSpec((1, LANES), lambda i: (0, 0)), row_spec, vec_spec),
        compiler_params=_cp("arbitrary"), name=name)(x, g.reshape(1, d), target)


def _alibi_slopes():
    return (2.0 ** (-8.0 * (np.arange(12) + 1) / 12)).astype(np.float32)


def _band_scores(q, k, q0, start, wk, slope):
    s = lax.dot_general(q, k, (((1,), (1,)), ((), ())), preferred_element_type=F32) * (HEAD ** -0.5)
    qpos = q0 + lax.broadcasted_iota(jnp.int32, (QBLK, wk), 0)
    kpos = start + lax.broadcasted_iota(jnp.int32, (QBLK, wk), 1)
    rel = jnp.abs(qpos - kpos)
    return jnp.where(rel <= N_SIDE, s - slope * rel.astype(F32), NEG)


def _attn_geometry(seq, dilation):
    length = seq // dilation
    return length, length // QBLK, min(2 * QBLK, length)


def _attn_window(n, length, wk):
    q0 = pl.multiple_of(n * QBLK, QBLK)
    start = pl.multiple_of(jnp.clip(n * QBLK - N_SIDE, 0, length - wk), N_SIDE)
    return q0, start


def attn_fwd(proj, group, name):
    seq = proj.shape[0]
    dilation = A_PATTERNS[group][1]
    length, nblk, wk = _attn_geometry(seq, dilation)
    cols = A_IN // HEAD
    pv = proj.reshape(length, dilation * A_IN)

    def body(slope_ref, q_ref, k_ref, v_ref, o_ref, lse_ref):
        slope = slope_ref[group * 4 + pl.program_id(1)] * float(dilation)

        def blk(n, carry):
            q0, start = _attn_window(n, length, wk)
            q = q_ref[pl.ds(q0, QBLK), :].astype(BF16)
            k = k_ref[pl.ds(start, wk), :].astype(BF16)
            v = v_ref[pl.ds(start, wk), :].astype(BF16)
            s = _band_scores(q, k, q0, start, wk, slope)
            m = jnp.max(s, axis=-1, keepdims=True)
            p = jnp.exp(s - m)
            l = jnp.sum(p, axis=-1, keepdims=True)
            o = jnp.dot(p.astype(BF16), v, preferred_element_type=F32) / l
            o_ref[pl.ds(q0, QBLK), :] = o
            lse_ref[pl.ds(q0, QBLK), :] = jnp.broadcast_to(m + jnp.log(l), (QBLK, HEAD))
            return carry

        lax.fori_loop(0, nblk, blk, 0)

    def part(p):
        return pl.BlockSpec((length, HEAD), lambda r, h: (0, r * cols + p * 12 + group * 4 + h))

    out_spec = pl.BlockSpec((length, HEAD), lambda r, h: (0, r * 4 + h))
    o, lse = pl.pallas_call(
        body, out_shape=(jax.ShapeDtypeStruct((length, dilation * A_OUT_W), F32),) * 2,
        grid=(dilation, 4),
        in_specs=[pl.BlockSpec(memory_space=pltpu.SMEM), part(0), part(1), part(2)],
        out_specs=(out_spec, out_spec), compiler_params=_cp("parallel", "parallel"), name=name,
    )(jnp.asarray(_alibi_slopes()), pv, pv, pv)
    return o.reshape(seq, A_OUT_W), lse.reshape(seq, A_OUT_W)


def attn_combine(os_, lses, name="attn_combine"):
    seq = os_[0].shape[0]
    tr = ROW_TILE

    def body(o0, o1, o2, l0, l1, l2, c_ref, lse_ref):
        a, b, c = l0[...], l1[...], l2[...]
        m = jnp.maximum(jnp.maximum(a, b), c)
        ea, eb, ec = jnp.exp(a - m), jnp.exp(b - m), jnp.exp(c - m)
        den = ea + eb + ec
        c_ref[...] = (ea * o0[...] + eb * o1[...] + ec * o2[...]) / den
        lse_ref[...] = m + jnp.log(den)

    spec = pl.BlockSpec((tr, A_OUT_W), lambda i: (i, 0))
    return pl.pallas_call(
        body, out_shape=(jax.ShapeDtypeStruct((seq, A_OUT_W), F32),) * 2, grid=(seq // tr,),
        in_specs=[spec] * 6, out_specs=(spec, spec), compiler_params=_cp("parallel"), name=name)(*os_, *lses)


def attn_bwd(proj, dcat, comb, lse, group, name):
    seq = proj.shape[0]
    dilation = A_PATTERNS[group][1]
    length, nblk, wk = _attn_geometry(seq, dilation)
    cols = A_IN // HEAD
    pv = proj.reshape(length, dilation * A_IN)
    view = lambda a: a.reshape(length, dilation * A_OUT_W)
    scale = HEAD ** -0.5

    def body(slope_ref, q_ref, k_ref, v_ref, do_ref, c_ref, lse_ref, dq_ref, dk_ref, dv_ref, dk_acc, dv_acc):
        slope = slope_ref[group * 4 + pl.program_id(1)] * float(dilation)
        dk_acc[...] = jnp.zeros_like(dk_acc)
        dv_acc[...] = jnp.zeros_like(dv_acc)

        def blk(n, carry):
            q0, start = _attn_window(n, length, wk)
            rows = pl.ds(q0, QBLK)
            keys = pl.ds(start, wk)
            q = q_ref[rows, :].astype(BF16)
            k = k_ref[keys, :].astype(BF16)
            v = v_ref[keys, :].astype(BF16)
            do = do_ref[rows, :]
            s = _band_scores(q, k, q0, start, wk, slope)
            p = jnp.exp(s - lse_ref[rows, :][:, :1])
            delta = jnp.sum(do * c_ref[rows, :], axis=-1, keepdims=True)
            do16 = do.astype(BF16)
            dp = lax.dot_general(do16, v, (((1,), (1,)), ((), ())), preferred_element_type=F32)
            ds = (p * (dp - delta) * scale).astype(BF16)
            p16 = p.astype(BF16)
            dq_ref[rows, :] = jnp.dot(ds, k, preferred_element_type=F32).astype(dq_ref.dtype)
            dk_acc[keys, :] += lax.dot_general(ds, q, (((0,), (0,)), ((), ())), preferred_element_type=F32)
            dv_acc[keys, :] += lax.dot_general(p16, do16, (((0,), (0,)), ((), ())), preferred_element_type=F32)
            return carry

        lax.fori_loop(0, nblk, blk, 0)
        dk_ref[...] = dk_acc[...].astype(dk_ref.dtype)
        dv_ref[...] = dv_acc[...].astype(dv_ref.dtype)

    def part(p):
        return pl.BlockSpec((length, HEAD), lambda r, h: (0, r * cols + p * 12 + group * 4 + h))

    hs = pl.BlockSpec((length, HEAD), lambda r, h: (0, r * 4 + h))
    do_cols = dcat.shape[1] // HEAD
    do_spec = pl.BlockSpec((length, HEAD), lambda r, h: (0, r * do_cols + h))
    outs = pl.pallas_call(
        body, out_shape=(jax.ShapeDtypeStruct((length, dilation * A_OUT_W), BF16),) * 3,
        grid=(dilation, 4),
        in_specs=[pl.BlockSpec(memory_space=pltpu.SMEM), part(0), part(1), part(2), do_spec, hs, hs],
        out_specs=(hs, hs, hs),
        scratch_shapes=[pltpu.VMEM((length, HEAD), F32), pltpu.VMEM((length, HEAD), F32)],
        compiler_params=_cp("parallel", "parallel"), name=name,
    )(jnp.asarray(_alibi_slopes()), pv, pv, pv, dcat.reshape(length, dilation * dcat.shape[1]), view(comb), view(lse))
    return tuple(o.reshape(seq, A_OUT_W) for o in outs)


MEM_ROW_TILE = 512


def _mem_probs(q, k):
    s = lax.dot_general(q, k, (((1,), (1,)), ((), ())), preferred_element_type=F32) * (HEAD ** -0.5)
    p = jnp.exp(s - jnp.max(s, axis=-1, keepdims=True))
    return p / jnp.sum(p, axis=-1, keepdims=True)


def mem_fwd(proj, q_col, kv, name="mem_fwd"):
    seq = proj.shape[0]
    qb = q_col // HEAD

    def body(q_ref, k_ref, v_ref, o_ref):
        p = _mem_probs(q_ref[...].astype(BF16), k_ref[...].astype(BF16))
        o_ref[...] = jnp.dot(p.astype(BF16), v_ref[...].astype(BF16), preferred_element_type=F32).astype(o_ref.dtype)

    return pl.pallas_call(
        body, out_shape=jax.ShapeDtypeStruct((seq, MEM_W), BF16), grid=(MEM_HEADS, seq // MEM_ROW_TILE),
        in_specs=[pl.BlockSpec((MEM_ROW_TILE, HEAD), lambda h, i: (i, qb + h)),
                  pl.BlockSpec((MEM_LEN, HEAD), lambda h, i: (0, h)),
                  pl.BlockSpec((MEM_LEN, HEAD), lambda h, i: (0, MEM_HEADS + h))],
        out_specs=pl.BlockSpec((MEM_ROW_TILE, HEAD), lambda h, i: (i, h)),
        compiler_params=_cp("parallel", "parallel"), name=name)(proj, kv, kv)


def mem_bwd(proj, q_col, kv, dcat, do_col, name="mem_bwd"):
    seq = proj.shape[0]
    qb, ob = q_col // HEAD, do_col // HEAD
    scale = HEAD ** -0.5

    def body(q_ref, k_ref, v_ref, do_ref, dq_ref, dk_ref, dv_ref):
        q = q_ref[...].astype(BF16)
        k = k_ref[...].astype(BF16)
        v = v_ref[...].astype(BF16)
        do = do_ref[...].astype(BF16)
        p = _mem_probs(q, k)
        dp = lax.dot_general(do, v, (((1,), (1,)), ((), ())), preferred_element_type=F32)
        ds = (p * (dp - jnp.sum(dp * p, axis=-1, keepdims=True)) * scale).astype(BF16)
        dq_ref[...] = jnp.dot(ds, k, preferred_element_type=F32).astype(dq_ref.dtype)
        dk = lax.dot_general(ds, q, (((0,), (0,)), ((), ())), preferred_element_type=F32)
        dv = lax.dot_general(p.astype(BF16), do, (((0,), (0,)), ((), ())), preferred_element_type=F32)

        @pl.when(pl.program_id(1) == 0)
        def _():
            dk_ref[...] = dk
            dv_ref[...] = dv

        @pl.when(pl.program_id(1) > 0)
        def _():
            dk_ref[...] += dk
            dv_ref[...] += dv

    dq, dk, dv = pl.pallas_call(
        body, out_shape=(jax.ShapeDtypeStruct((seq, MEM_W), BF16), jax.ShapeDtypeStruct((MEM_LEN, MEM_W), F32),
                         jax.ShapeDtypeStruct((MEM_LEN, MEM_W), F32)),
        grid=(MEM_HEADS, seq // MEM_ROW_TILE),
        in_specs=[pl.BlockSpec((MEM_ROW_TILE, HEAD), lambda h, i: (i, qb + h)),
                  pl.BlockSpec((MEM_LEN, HEAD), lambda h, i: (0, h)),
                  pl.BlockSpec((MEM_LEN, HEAD), lambda h, i: (0, MEM_HEADS + h)),
                  pl.BlockSpec((MEM_ROW_TILE, HEAD), lambda h, i: (i, ob + h))],
        out_specs=(pl.BlockSpec((MEM_ROW_TILE, HEAD), lambda h, i: (i, h)),
                   pl.BlockSpec((MEM_LEN, HEAD), lambda h, i: (0, h)),
                   pl.BlockSpec((MEM_LEN, HEAD), lambda h, i: (0, h))),
        compiler_params=_cp("parallel", "arbitrary"), name=name)(proj, kv, kv, dcat)
    return dq, jnp.concatenate([dk, dv], axis=1)


def _sgu_front(x, gain):
    uv, duv = _gelu_parts(x)
    u, v = uv[:, :B_W], uv[:, B_W:]
    r, vh = _rms_stats(v)
    return u, duv, r, vh, vh * gain


def sgu_fwd(proj, gain, w_s, bias_b, name="sgu_fwd"):
    seq = proj.shape[0]

    def body(x_ref, gain_ref, ws_ref, bias_ref, o_ref):
        u, _, _, _, vn = _sgu_front(x_ref[...], gain_ref[...])
        for g in range(B_GROUPS):
            cs = slice(g * CHUNK, (g + 1) * CHUNK)
            mixed = jnp.dot(ws_ref[g].astype(BF16), vn[:, cs].astype(BF16), preferred_element_type=F32) + bias_ref[g]
            o_ref[:, cs] = (u[:, cs] * mixed).astype(o_ref.dtype)

    full = lambda shape: pl.BlockSpec(shape, lambda c: (0,) * len(shape))
    return pl.pallas_call(
        body, out_shape=jax.ShapeDtypeStruct((seq, B_W), BF16), grid=(seq // CHUNK,),
        in_specs=[pl.BlockSpec((CHUNK, 2 * B_W), lambda c: (c, 0)), full((1, B_W)),
                  full((B_GROUPS, CHUNK, CHUNK)), full((B_GROUPS, CHUNK, CHUNK))],
        out_specs=pl.BlockSpec((CHUNK, B_W), lambda c: (c, 0)),
        compiler_params=_cp("parallel"), name=name)(proj, gain.reshape(1, B_W), w_s, bias_b)


def sgu_bwd(proj, gain, w_s, w_s_t, bias_b, dcat, name="sgu_bwd"):
    seq = proj.shape[0]

    def body(x_ref, gain_ref, ws_ref, wst_ref, bias_ref, do_ref, dx_ref, dws_ref, dmix_ref, dgain_ref, dvn_ref):
        first = pl.program_id(0) == 0
        gain = gain_ref[...]
        u, duv, r, vh, vn = _sgu_front(x_ref[...], gain)
        do = do_ref[...]
        for g in range(B_GROUPS):
            cs = slice(g * CHUNK, (g + 1) * CHUNK)
            vg = vn[:, cs].astype(BF16)
            mixed = jnp.dot(ws_ref[g].astype(BF16), vg, preferred_element_type=F32) + bias_ref[g]
            dx_ref[:, cs] = (do[:, cs] * mixed * duv[:, cs]).astype(dx_ref.dtype)
            dmixed = do[:, cs] * u[:, cs]
            dm16 = dmixed.astype(BF16)
            dws = lax.dot_general(dm16, vg, (((1,), (1,)), ((), ())), preferred_element_type=F32)
            dvn_ref[:, cs] = jnp.dot(wst_ref[g].astype(BF16), dm16, preferred_element_type=F32)

            @pl.when(first)
            def _():
                dws_ref[g] = dws
                dmix_ref[g] = dmixed

            @pl.when(jnp.logical_not(first))
            def _():
                dws_ref[g] += dws
                dmix_ref[g] += dmixed

        dvn = dvn_ref[...]
        dgain = jnp.sum(dvn * vh, axis=0, keepdims=True)

        @pl.when(first)
        def _():
            dgain_ref[...] = dgain

        @pl.when(jnp.logical_not(first))
        def _():
            dgain_ref[...] += dgain

        dv = _rms_back(vh, r, gain, dvn)
        dx_ref[:, B_W:] = (dv * duv[:, B_W:]).astype(dx_ref.dtype)

    full = lambda shape: pl.BlockSpec(shape, lambda c: (0,) * len(shape))
    mats = full((B_GROUPS, CHUNK, CHUNK))
    return pl.pallas_call(
        body, out_shape=(jax.ShapeDtypeStruct((seq, 2 * B_W), BF16), jax.ShapeDtypeStruct((B_GROUPS, CHUNK, CHUNK), F32),
                         jax.ShapeDtypeStruct((B_GROUPS, CHUNK, CHUNK), F32), jax.ShapeDtypeStruct((1, B_W), F32)),
        grid=(seq // CHUNK,),
        in_specs=[pl.BlockSpec((CHUNK, 2 * B_W), lambda c: (c, 0)), full((1, B_W)), mats, mats, mats,
                  pl.BlockSpec((CHUNK, B_W), lambda c: (c, 0))],
        out_specs=(pl.BlockSpec((CHUNK, 2 * B_W), lambda c: (c, 0)), mats, mats, full((1, B_W))),
        scratch_shapes=[pltpu.VMEM((CHUNK, B_W), F32)],
        compiler_params=_cp("arbitrary"), name=name)(proj, gain.reshape(1, B_W), w_s, w_s_t, bias_b, dcat)


FFN_COLS = 128
FFN_ROWS = 32
SUBLANES = 8


def _window(ref, r0, first, last):
    cols = ref.shape[1]
    pad = jnp.zeros((SUBLANES, cols), F32)
    if first:
        return jnp.concatenate([pad, ref[pl.ds(0, FFN_ROWS + SUBLANES), :]], axis=0)
    if last:
        return jnp.concatenate([ref[pl.ds(r0 - SUBLANES, FFN_ROWS + SUBLANES), :], pad], axis=0)
    return ref[pl.ds(pl.multiple_of(r0 - SUBLANES, SUBLANES), FFN_ROWS + 2 * SUBLANES), :]


def _taps(win):
    mid = slice(SUBLANES, SUBLANES + FFN_ROWS)
    return pltpu.roll(win, 1, 0)[mid], win[mid], pltpu.roll(win, win.shape[0] - 1, 0)[mid]


def _row_steps(seq, step, carry):
    n = seq // FFN_ROWS
    carry = step(0, True, False, carry)
    carry = lax.fori_loop(1, n - 1, lambda i, c: step(pl.multiple_of(i * FFN_ROWS, FFN_ROWS), False, False, c), carry)
    return step(seq - FFN_ROWS, False, True, carry)


def _conv3(taps, w, b):
    prev, cur, nxt = taps
    return prev * w[0:1] + cur * w[1:2] + nxt * w[2:3] + b


def _fold(x):
    return jnp.sum(x.reshape(FFN_ROWS // SUBLANES, SUBLANES, x.shape[1]), axis=0)


FFN_FWD_COLS = 256


def _taps_whole(a):
    n = a.shape[0]
    rows = lax.broadcasted_iota(jnp.int32, a.shape, 0)
    return (jnp.where(rows == 0, 0.0, pltpu.roll(a, 1, 0)), a, jnp.where(rows == n - 1, 0.0, pltpu.roll(a, n - 1, 0)))


def ffn_act_fwd(a, conv_w, conv_b, name="ffn_act_fwd"):
    seq = a.shape[0]
    nb = FF // FFN_FWD_COLS

    def body(ag_ref, av_ref, wg_ref, wv_ref, bg_ref, bv_ref, o_ref):
        gate = _conv3(_taps_whole(ag_ref[...]), wg_ref[...], bg_ref[...])
        val = _conv3(_taps_whole(av_ref[...]), wv_ref[...], bv_ref[...])
        o_ref[...] = (_gelu(gate) * val).astype(o_ref.dtype)

    col = lambda rows, off: pl.BlockSpec((rows, FFN_FWD_COLS), lambda j: (0, j + off))
    cb = conv_b.reshape(1, 2 * FF)
    return pl.pallas_call(
        body, out_shape=jax.ShapeDtypeStruct((seq, FF), BF16), grid=(nb,),
        in_specs=[col(seq, 0), col(seq, nb), col(3, 0), col(3, nb), col(1, 0), col(1, nb)],
        out_specs=col(seq, 0), compiler_params=_cp("parallel"), name=name)(a, a, conv_w, conv_w, cb, cb)


def ffn_act_bwd(a, conv_w, conv_b, dact, name="ffn_act_bwd"):
    seq = a.shape[0]
    nb = FF // FFN_COLS

    def body(ag_ref, av_ref, wg_ref, wv_ref, bg_ref, bv_ref, d_ref, dag_ref, dav_ref, dwg_ref, dwv_ref, dbg_ref, dbv_ref,
             dcg_ref, dcv_ref):
        wg, wv, bg, bv = wg_ref[...], wv_ref[...], bg_ref[...], bv_ref[...]

        def conv_grads(r0, first, last, sums):
            g_taps = _taps(_window(ag_ref, r0, first, last))
            v_taps = _taps(_window(av_ref, r0, first, last))
            act, dact_dgate = _gelu_parts(_conv3(g_taps, wg, bg))
            d = d_ref[pl.ds(r0, FFN_ROWS), :].astype(F32)
            dcg = d * _conv3(v_taps, wv, bv) * dact_dgate
            dcv = d * act
            dcg_ref[pl.ds(r0, FFN_ROWS), :] = dcg
            dcv_ref[pl.ds(r0, FFN_ROWS), :] = dcv
            new = [_fold(dcg)] + [_fold(dcg * t) for t in g_taps] + [_fold(dcv)] + [_fold(dcv * t) for t in v_taps]
            return tuple(s + n for s, n in zip(sums, new))

        zero = jnp.zeros((SUBLANES, FFN_COLS), F32)
        sums = _row_steps(seq, conv_grads, (zero,) * 8)
        total = [jnp.sum(s, axis=0, keepdims=True) for s in sums]
        dbg_ref[...] = total[0]
        dbv_ref[...] = total[4]
        for k in range(3):
            dwg_ref[k:k + 1, :] = total[1 + k]
            dwv_ref[k:k + 1, :] = total[5 + k]

        def conv_transpose(r0, first, last, carry):
            for dc_ref, w, da_ref in ((dcg_ref, wg, dag_ref), (dcv_ref, wv, dav_ref)):
                prev, cur, nxt = _taps(_window(dc_ref, r0, first, last))
                da_ref[pl.ds(r0, FFN_ROWS), :] = (nxt * w[0:1] + cur * w[1:2] + prev * w[2:3]).astype(da_ref.dtype)
            return carry

        _row_steps(seq, conv_transpose, 0)

    col = lambda rows, off: pl.BlockSpec((rows, FFN_COLS), lambda j: (0, j + off))
    cb = conv_b.reshape(1, 2 * FF)
    dag, dav, dwg, dwv, dbg, dbv = pl.pallas_call(
        body, out_shape=(jax.ShapeDtypeStruct((seq, FF), BF16),) * 2 + (jax.ShapeDtypeStruct((3, FF), F32),) * 2
        + (jax.ShapeDtypeStruct((1, FF), F32),) * 2, grid=(nb,),
        in_specs=[col(seq, 0), col(seq, nb), col(3, 0), col(3, nb), col(1, 0), col(1, nb), col(seq, 0)],
        out_specs=(col(seq, 0), col(seq, 0), col(3, 0), col(3, 0), col(1, 0), col(1, 0)),
        scratch_shapes=[pltpu.VMEM((seq, FFN_COLS), F32), pltpu.VMEM((seq, FFN_COLS), F32)],
        compiler_params=_cp("parallel"), name=name)(a, a, conv_w, conv_w, cb, cb, dact)
    cat = lambda p, q: jnp.concatenate([p, q], axis=1)
    return cat(dag, dav), cat(dwg, dwv), cat(dbg, dbv)


def _adam_math(w, g, m, v):
    m = ADAM_B1 * m + (1.0 - ADAM_B1) * g
    v = ADAM_B2 * v + (1.0 - ADAM_B2) * (g * g)
    m_hat = m / (1.0 - ADAM_B1 ** ADAM_STEP)
    v_hat = v / (1.0 - ADAM_B2 ** ADAM_STEP)
    return -ADAM_LR * (m_hat / (jnp.sqrt(v_hat) + ADAM_EPS) + ADAM_WD * w), m, v


def _row_tile(rows, cols):
    return _pick(rows, (256, 128, 64)) if cols <= 1024 else _pick(rows, (128, 64))


def adamw_layer(w_all, m_all, v_all, layer, g, prev, name):
    n, rows, cols = w_all.shape
    tr = _row_tile(rows, cols)

    def body(w_ref, m_ref, v_ref, g_ref, *rest):
        go_ref, d_ref, mo_ref, vo_ref = rest[-4:]
        g_ = g_ref[...]
        d, m_, v_ = _adam_math(w_ref[...], g_, m_ref[...], v_ref[...])
        go_ref[...] = g_
        d_ref[...] = d
        mo_ref[...] = m_
        vo_ref[...] = v_

    lay = pl.BlockSpec((None, tr, cols), lambda i: (layer, i, 0))
    in_specs = [lay, lay, lay, pl.BlockSpec((tr, cols), lambda i: (i, 0))]
    args = [w_all, m_all, v_all, g]
    aliases = {}
    if prev is not None:
        in_specs += [pl.BlockSpec(memory_space=pl.ANY)] * 4
        args += list(prev)
        aliases = {4 + k: k for k in range(4)}
    return pl.pallas_call(
        body, out_shape=(jax.ShapeDtypeStruct(w_all.shape, F32),) * 4, grid=(rows // tr,),
        in_specs=in_specs, out_specs=(lay,) * 4, input_output_aliases=aliases,
        compiler_params=_cp("parallel"), name=name)(*args)


def adamw_flat(w, g, m, v, name="adamw_small"):
    rows, cols = w.shape
    tr = _pick(rows, (128, 8))

    def body(w_ref, g_ref, m_ref, v_ref, d_ref, mo_ref, vo_ref):
        d_ref[...], mo_ref[...], vo_ref[...] = _adam_math(w_ref[...], g_ref[...], m_ref[...], v_ref[...])

    spec = pl.BlockSpec((tr, cols), lambda i: (i, 0))
    return pl.pallas_call(
        body, out_shape=(jax.ShapeDtypeStruct(w.shape, F32),) * 3, grid=(rows // tr,),
        in_specs=[spec] * 4, out_specs=(spec,) * 3, compiler_params=_cp("parallel"), name=name)(w, g, m, v)


def pair_sum(dw, got, core, name):
    _, rows, cols = dw.shape
    half = rows // 2
    tr = _row_tile(half, cols)
    nrb = half // tr

    def body(c_ref, a_ref, b_ref, o_ref):
        o_ref[...] = (a_ref[...].astype(F32) + b_ref[...].astype(F32)).astype(o_ref.dtype)

    return pl.pallas_call(
        body, out_shape=jax.ShapeDtypeStruct((N_CHIPS, half, cols), BF16),
        grid_spec=pltpu.PrefetchScalarGridSpec(
            num_scalar_prefetch=1, grid=(N_CHIPS, nrb),
            in_specs=[pl.BlockSpec((None, tr, cols), lambda s, i, c_ref: (s, c_ref[0] * nrb + i, 0)),
                      pl.BlockSpec((None, tr, cols), lambda s, i, c_ref: (s, i, 0))],
            out_specs=pl.BlockSpec((None, tr, cols), lambda s, i, c_ref: (s, i, 0))),
        compiler_params=_cp("parallel", "parallel"), name=name)(core, dw, got)


def chip_sum(own, parts, place, name):
    _, half, cols = parts.shape
    tr = _row_tile(half, cols)
    nrb = half // tr

    def body(p_ref, own_ref, a_ref, b_ref, c_ref, o_ref):
        o_ref[...] = ((own_ref[...].astype(F32) + a_ref[...].astype(F32)) + b_ref[...].astype(F32)) + c_ref[...].astype(F32)

    def slot(k):
        return pl.BlockSpec((None, tr, cols), lambda i, p: (jnp.bitwise_xor(p[0], k), i, 0))

    return pl.pallas_call(
        body, out_shape=jax.ShapeDtypeStruct((2 * half, cols), F32),
        grid_spec=pltpu.PrefetchScalarGridSpec(
            num_scalar_prefetch=1, grid=(nrb,), in_specs=[slot(0), slot(1), slot(2), slot(3)],
            out_specs=pl.BlockSpec((tr, cols), lambda i, p: (p[1] * nrb + i, 0))),
        compiler_params=_cp("parallel"), name=name)(place, own, parts, parts, parts)


def cast_to_slot(w_all, layer, place, name, after=None):
    _, rows, cols = w_all.shape
    tr = _row_tile(rows, cols)
    tied = [] if after is None else [after]

    def body(p_ref, w_ref, *rest):
        o_ref = rest[-1]
        o_ref[...] = w_ref[...].astype(o_ref.dtype)

    return pl.pallas_call(
        body, out_shape=jax.ShapeDtypeStruct((N_CHIPS, rows, cols), BF16),
        grid_spec=pltpu.PrefetchScalarGridSpec(
            num_scalar_prefetch=1, grid=(rows // tr,),
            in_specs=[pl.BlockSpec((None, tr, cols), lambda i, p: (layer, i, 0))] + [ANY] * len(tied),
            out_specs=pl.BlockSpec((None, tr, cols), lambda i, p: (p[0], i, 0))),
        compiler_params=_cp("parallel"), name=name)(place, w_all, *tied)


ANY = pl.BlockSpec(memory_space=pl.ANY)


def _place():
    x, y, c = lax.axis_index("x"), lax.axis_index("y"), lax.axis_index("c")
    others = [(1 - x, y), (x, 1 - y), (1 - x, 1 - y)]
    return x, y, c, 2 * x + y, others


def _remote(src, dst, send_sem, recv_sem, dev):
    return pltpu.make_async_remote_copy(src_ref=src, dst_ref=dst, send_sem=send_sem, recv_sem=recv_sem,
                                        device_id=dev, device_id_type=MESH)


HBM = pl.BlockSpec(memory_space=pltpu.HBM)
SEM = pl.BlockSpec(memory_space=pltpu.SEMAPHORE)
EFFECT = pltpu.SideEffectType.DATAFLOW_SIDE_EFFECTING
TOKEN = jax.ShapeDtypeStruct((8, LANES), F32)


def _in_hbm(a):
    return pltpu.with_memory_space_constraint(a, pltpu.HBM)


def _gather_copies(bufs, send_sems, recv_sems):
    x, y, c, me, others = _place()
    out = []
    for w, buf in enumerate(bufs):
        half = buf.shape[1] // 2
        mine = pl.ds(c * half, half)
        for k, (ox, oy) in enumerate(others):
            sems = send_sems.at[3 * w + k], recv_sems.at[3 * w + k]
            out.append((_remote(buf.at[me, mine], buf.at[me, mine], *sems, (ox, oy, c)),
                        _remote(buf.at[me, mine], buf.at[2 * ox + oy, mine], *sems, (ox, oy, c))))
    return out


def _forward_copies(bufs, send_sems, recv_sems):
    x, y, c, me, others = _place()
    out = []
    for w, buf in enumerate(bufs):
        half = buf.shape[1] // 2
        mine, theirs = pl.ds(c * half, half), pl.ds((1 - c) * half, half)
        for k, (ox, oy) in enumerate(others):
            sems = send_sems.at[3 * w + k], recv_sems.at[3 * w + k]
            slot = 2 * ox + oy
            out.append((_remote(buf.at[slot, mine], buf.at[slot, mine], *sems, (x, y, 1 - c)),
                        _remote(buf.at[slot, mine], buf.at[slot, theirs], *sems, (x, y, 1 - c))))
    return out


def _join_copies(grads, send_sems, recv_sems):
    x, y, c, _, _ = _place()
    out = []
    for w, g in enumerate(grads):
        half = g.shape[0] // 2
        mine, theirs = pl.ds(c * half, half), pl.ds((1 - c) * half, half)
        sems = send_sems.at[w], recv_sems.at[w]
        out.append((_remote(g.at[mine], g.at[mine], *sems, (x, y, 1 - c)), _remote(g.at[mine], g.at[theirs], *sems, (x, y, 1 - c))))
    return out


IN_PLACE = dict(gather=(_gather_copies, 3), forward=(_forward_copies, 3), join=(_join_copies, 1))


def copies_start(kind, bufs, name):
    n = len(bufs)
    copies, per_buf = IN_PLACE[kind]

    def body(*refs):
        ins, (send_sems, recv_sems), token = refs[:n], refs[n:n + 2], refs[-1]
        for sent, _ in copies(ins, send_sems, recv_sems):
            sent.start()
        token[...] = jnp.zeros_like(token)

    outs = pl.pallas_call(
        body, name=name,
        out_shape=(pltpu.SemaphoreType.DMA((per_buf * n,)), pltpu.SemaphoreType.DMA((per_buf * n,)),
                   *[pltpu.HBM(b.shape, b.dtype) for b in bufs], TOKEN),
        in_specs=[HBM] * n, out_specs=(SEM, SEM, *[HBM] * n, VM),
        input_output_aliases={w: 2 + w for w in range(n)},
        compiler_params=pltpu.CompilerParams(has_side_effects=EFFECT))(*[_in_hbm(b) for b in bufs])
    return outs[0], outs[1], list(outs[2:2 + n]), outs[-1]


def copies_wait(kind, send_sems, recv_sems, bufs, after, name):
    n = len(bufs)
    copies, _ = IN_PLACE[kind]

    def body(*refs):
        ins, (send_ref, recv_ref) = refs[:n], refs[n:n + 2]
        for sent, landed in copies(ins, send_ref, recv_ref):
            sent.wait_send()
            landed.wait_recv()

    return list(pl.pallas_call(
        body, name=name, out_shape=tuple(pltpu.HBM(b.shape, b.dtype) for b in bufs),
        in_specs=[HBM] * n + [SEM, SEM, ANY], out_specs=(HBM,) * n,
        input_output_aliases={w: w for w in range(n)},
        compiler_params=pltpu.CompilerParams(has_side_effects=EFFECT))(*bufs, send_sems, recv_sems, after))


def swap_halves(grads, name):
    n = len(grads)

    def body(*refs):
        ins, outs = refs[:n], refs[n:2 * n]
        send_sems, recv_sems = refs[2 * n:]
        x, y, c, _, _ = _place()
        copies = []
        for w in range(n):
            half = ins[w].shape[1] // 2
            cp = _remote(ins[w].at[:, pl.ds((1 - c) * half, half)], outs[w], send_sems.at[w], recv_sems.at[w], (x, y, 1 - c))
            cp.start()
            copies.append(cp)
        for cp in copies:
            cp.wait()

    return pl.pallas_call(
        body, out_shape=tuple(jax.ShapeDtypeStruct((N_CHIPS, g.shape[1] // 2, g.shape[2]), g.dtype) for g in grads),
        in_specs=[ANY] * n, out_specs=(ANY,) * n,
        scratch_shapes=[pltpu.SemaphoreType.DMA((n,)), pltpu.SemaphoreType.DMA((n,))], name=name)(*grads)


def _swap_copies(grads, lands, send_sems, recv_sems):
    x, y, c, _, _ = _place()
    out = []
    for w, (g, land) in enumerate(zip(grads, lands)):
        half = g.shape[1] // 2
        out.append(_remote(g.at[:, pl.ds((1 - c) * half, half)], land, send_sems.at[w], recv_sems.at[w], (x, y, 1 - c)))
    return out


def swap_start(grads, name):
    n = len(grads)

    def body(*refs):
        ins, lands, (send_sems, recv_sems), token = refs[:n], refs[n:2 * n], refs[2 * n:2 * n + 2], refs[-1]
        for cp in _swap_copies(ins, lands, send_sems, recv_sems):
            cp.start()
        token[...] = jnp.zeros_like(token)

    shapes = [(N_CHIPS, g.shape[1] // 2, g.shape[2]) for g in grads]
    zones = [_in_hbm(lax.empty(s, g.dtype)) for s, g in zip(shapes, grads)]
    outs = pl.pallas_call(
        body, name=name,
        out_shape=(pltpu.SemaphoreType.DMA((n,)), pltpu.SemaphoreType.DMA((n,)),
                   *[pltpu.HBM(g.shape, g.dtype) for g in grads], *[pltpu.HBM(s, g.dtype) for s, g in zip(shapes, grads)],
                   TOKEN),
        in_specs=[HBM] * (2 * n), out_specs=(SEM, SEM, *[HBM] * (2 * n), VM),
        input_output_aliases={w: 2 + w for w in range(2 * n)},
        compiler_params=pltpu.CompilerParams(has_side_effects=EFFECT))(*[_in_hbm(g) for g in grads], *zones)
    return outs[0], outs[1], list(outs[2:2 + n]), list(outs[2 + n:2 + 2 * n]), outs[-1]


def swap_wait(send_sems, recv_sems, grads, lands, after, name):
    n = len(grads)

    def body(*refs):
        ins, zones, (send_ref, recv_ref) = refs[:n], refs[n:2 * n], refs[2 * n:2 * n + 2]
        for cp in _swap_copies(ins, zones, send_ref, recv_ref):
            cp.wait_send()
            cp.wait_recv()

    outs = pl.pallas_call(
        body, name=name, out_shape=tuple(pltpu.HBM(a.shape, a.dtype) for a in list(grads) + list(lands)),
        in_specs=[HBM] * (2 * n) + [SEM, SEM, ANY], out_specs=(HBM,) * (2 * n),
        input_output_aliases={w: w for w in range(2 * n)},
        compiler_params=pltpu.CompilerParams(has_side_effects=EFFECT))(*grads, *lands, send_sems, recv_sems, after)
    return list(outs[:n]), list(outs[n:])


def _exchange_copies(sums, lands, send_sems, recv_sems):
    x, y, c, me, others = _place()
    out = []
    for w, (src, land) in enumerate(zip(sums, lands)):
        for k, (ox, oy) in enumerate(others):
            sems = send_sems.at[3 * w + k], recv_sems.at[3 * w + k]
            out.append((_remote(src.at[2 * ox + oy], land.at[me], *sems, (ox, oy, c)),
                        _remote(src.at[2 * ox + oy], land.at[2 * ox + oy], *sems, (ox, oy, c))))
    return out


def exchange_start(sums, name):
    n = len(sums)

    def body(*refs):
        ins, lands, (send_sems, recv_sems), token = refs[:n], refs[n:2 * n], refs[2 * n:2 * n + 2], refs[-1]
        for sent, _ in _exchange_copies(ins, lands, send_sems, recv_sems):
            sent.start()
        token[...] = jnp.zeros_like(token)

    zones = [_in_hbm(lax.empty(s.shape, s.dtype)) for s in sums]
    outs = pl.pallas_call(
        body, name=name,
        out_shape=(pltpu.SemaphoreType.DMA((3 * n,)), pltpu.SemaphoreType.DMA((3 * n,)),
                   *[pltpu.HBM(s.shape, s.dtype) for s in sums] * 2, TOKEN),
        in_specs=[HBM] * (2 * n), out_specs=(SEM, SEM, *[HBM] * (2 * n), VM),
        input_output_aliases={w: 2 + w for w in range(2 * n)},
        compiler_params=pltpu.CompilerParams(has_side_effects=EFFECT))(*[_in_hbm(s) for s in sums], *zones)
    return outs[0], outs[1], list(outs[2:2 + n]), list(outs[2 + n:2 + 2 * n]), outs[-1]


def exchange_wait(send_sems, recv_sems, sums, lands, after, name):
    n = len(sums)

    def body(*refs):
        ins, zones, (send_ref, recv_ref) = refs[:n], refs[n:2 * n], refs[2 * n:2 * n + 2]
        for sent, landed in _exchange_copies(ins, zones, send_ref, recv_ref):
            sent.wait_send()
            landed.wait_recv()

    outs = pl.pallas_call(
        body, name=name, out_shape=tuple(pltpu.HBM(s.shape, s.dtype) for s in sums) * 2,
        in_specs=[HBM] * (2 * n) + [SEM, SEM, ANY], out_specs=(HBM,) * (2 * n),
        input_output_aliases={w: w for w in range(2 * n)},
        compiler_params=pltpu.CompilerParams(has_side_effects=EFFECT))(*sums, *lands, send_sems, recv_sems, after)
    return list(outs[:n]), list(outs[n:])


VM = pl.BlockSpec(memory_space=pltpu.VMEM)


def small_allgather(buf, name="small_allgather"):
    def body(in_ref, out_ref, send_sems, recv_sems):
        x, y, c, me, others = _place()
        out_ref[me] = in_ref[...]
        copies = []
        for k, (ox, oy) in enumerate(others):
            cp = _remote(in_ref, out_ref.at[me], send_sems.at[k], recv_sems.at[k], (ox, oy, c))
            cp.start()
            copies.append(cp)
        for k, (ox, oy) in enumerate(others):
            landed = out_ref.at[2 * ox + oy]
            _remote(landed, landed, send_sems.at[k], recv_sems.at[k], (ox, oy, c)).wait_recv()
        for cp in copies:
            cp.wait_send()

    return pl.pallas_call(
        body, out_shape=jax.ShapeDtypeStruct((N_CHIPS,) + buf.shape, buf.dtype), in_specs=[VM], out_specs=VM,
        scratch_shapes=[pltpu.SemaphoreType.DMA((3,)), pltpu.SemaphoreType.DMA((3,))],
        compiler_params=pltpu.CompilerParams(vmem_limit_bytes=V7X_VMEM_LIMIT), name=name)(buf)


def small_allreduce(buf, after, name="small_allreduce"):
    def body(in_ref, after_ref, out_ref, sib_ref, slot_ref, send_sems, recv_sems):
        x, y, c, me, others = _place()
        cp = _remote(in_ref, sib_ref, send_sems.at[3], recv_sems.at[3], (x, y, 1 - c))
        cp.start()
        cp.wait()
        slot_ref[me] = in_ref[...] + sib_ref[...]
        copies = []
        for k, (ox, oy) in enumerate(others):
            cp = _remote(slot_ref.at[me], slot_ref.at[me], send_sems.at[k], recv_sems.at[k], (ox, oy, c))
            cp.start()
            copies.append(cp)
        for k, (ox, oy) in enumerate(others):
            landed = slot_ref.at[2 * ox + oy]
            _remote(landed, landed, send_sems.at[k], recv_sems.at[k], (ox, oy, c)).wait_recv()
        for cp in copies:
            cp.wait_send()
        out_ref[...] = ((slot_ref[0] + slot_ref[1]) + slot_ref[2]) + slot_ref[3]

    return pl.pallas_call(
        body, out_shape=jax.ShapeDtypeStruct(buf.shape, buf.dtype), in_specs=[VM, ANY], out_specs=VM,
        scratch_shapes=[pltpu.VMEM(buf.shape, buf.dtype), pltpu.VMEM((N_CHIPS,) + buf.shape, buf.dtype),
                        pltpu.SemaphoreType.DMA((4,)), pltpu.SemaphoreType.DMA((4,))],
        compiler_params=pltpu.CompilerParams(vmem_limit_bytes=V7X_VMEM_LIMIT), name=name)(buf, after)


def _pack_rows(arrays, row_multiple):
    flat = jnp.concatenate([a.reshape(-1) for a in arrays])
    rows = -(-flat.shape[0] // (LANES * row_multiple)) * row_multiple
    return jnp.pad(flat, (0, rows * LANES - flat.shape[0])).reshape(rows, LANES)


def _unpack_rows(buf, shapes):
    flat = buf.reshape(-1)
    out, at = [], 0
    for s in shapes:
        n = math.prod(s)
        out.append(flat[at:at + n].reshape(s))
        at += n
    return out


def _mixer_weights(i):
    j = i // 2
    mixer = "a" if i % 2 == 0 else "b"
    return [("w_mem_kv", i), (mixer + "_w_in", j), (mixer + "_w_out", j)]


def _ffn_weights(i):
    return [("ffn_w_up", i), ("ffn_w_down", i)]


def _mixer_fwd(i, x, mem, w, small, after):
    is_a = i % 2 == 0
    j = i // 2
    wkv, win, wout = w
    wkv = wkv.reshape(1, D_MODEL, 2 * MEM_W)
    h1 = rms_fwd(x, small["mix_norm_g"][i], BF16, name=f"mix_norm{i}", after=after)
    mem_n = rms_fwd(mem, small["mem_norm_g"][i], BF16, name=f"mem_norm{i}")
    kv = mm_nn(mem_n, wkv, F32, name=f"mem_kv{i}")
    proj = mm_nn(h1, win, F32, name=f"in_proj{i}")
    saved = dict(x0=x, h1=h1, mem_n=mem_n, kv=kv, proj=proj)
    if is_a:
        outs, lses = zip(*[attn_fwd(proj, g, name=f"attn_fwd{i}_{g}") for g in range(3)])
        comb, lse = attn_combine(outs, lses, name=f"attn_combine{i}")
        mem_out = mem_fwd(proj, 3 * A_QKV_W, kv, name=f"mem_fwd{i}")
        cat = jnp.concatenate([comb.astype(BF16), mem_out], axis=1)
        saved.update(comb=comb, lse=lse)
    else:
        wout = wout.reshape(1, B_W + MEM_W, D_MODEL)
        tok = sgu_fwd(proj, small["b_v_norm_g"][j], small["b_w_s"][j], small["bias_b"][j], name=f"sgu_fwd{i}")
        mem_out = mem_fwd(proj, 2 * B_W, kv, name=f"mem_fwd{i}")
        cat = jnp.concatenate([tok, mem_out], axis=1)
    x1 = mm_nn(cat, wout, F32, res=x, name=f"out_proj{i}")
    saved.update(cat=cat)
    return x1, saved


def _ffn_fwd(i, x1, w, small, after):
    wup, wdn = w
    h2 = rms_fwd(x1, small["ffn_norm_g"][i], BF16, name=f"ffn_norm{i}", after=after)
    a = mm_nn(h2, wup, F32, name=f"ffn_up{i}")
    act = ffn_act_fwd(a, small["ffn_conv_w"][i], small["ffn_conv_b"][i], name=f"ffn_act{i}")
    x2 = mm_nn(act, wdn.reshape(1, FF, D_MODEL), F32, res=x1, name=f"ffn_down{i}")
    return x2, dict(x1=x1, h2=h2, a=a, act=act)


def _ffn_bwd(i, dx2, w, small, sv, after):
    wup, wdn = w
    sg = {}
    dact = mm_nt(dx2, wdn.reshape(1, FF, D_MODEL), F32, name=f"d_act{i}", after=after)
    d_wdn = mm_tn(sv["act"], dx2, 1, BF16, name=f"d_wdown{i}").reshape(N_CHIPS, FF // N_CHIPS, D_MODEL)
    da, sg["ffn_conv_w"], sg["ffn_conv_b"] = ffn_act_bwd(sv["a"], small["ffn_conv_w"][i], small["ffn_conv_b"][i], dact,
                                                          name=f"ffn_act_bwd{i}")
    d_wup = mm_tn(sv["h2"], da, N_CHIPS, BF16, name=f"d_wup{i}")
    dh2 = mm_nt(da, wup, F32, name=f"d_h2_{i}")
    dx1, sg["ffn_norm_g"] = rms_bwd(sv["x1"], small["ffn_norm_g"][i], dh2, dres=dx2, name=f"ffn_norm_bwd{i}")
    return dx1, [d_wup, d_wdn], sg


def _mixer_bwd(i, dx1, mem, w, small, sv, after):
    is_a = i % 2 == 0
    j = i // 2
    wkv, win, wout = w
    wkv = wkv.reshape(1, D_MODEL, 2 * MEM_W)
    sg = {}
    proj, kv = sv["proj"], sv["kv"]
    if is_a:
        dcat = mm_nt(dx1, wout, F32, name=f"d_cat{i}", after=after)
        d_wout = mm_tn(sv["cat"], dx1, N_CHIPS, BF16, name=f"d_wout{i}")
        dqm, dkv = mem_bwd(proj, 3 * A_QKV_W, kv, dcat, A_OUT_W, name=f"mem_bwd{i}")
        parts = [attn_bwd(proj, dcat, sv["comb"], sv["lse"], g, name=f"attn_bwd{i}_{g}") for g in range(3)]
        dproj = jnp.concatenate([parts[g][p] for p in range(3) for g in range(3)] + [dqm], axis=1)
    else:
        dcat = mm_nt(dx1, wout.reshape(1, B_W + MEM_W, D_MODEL), F32, name=f"d_cat{i}", after=after)
        d_wout = mm_tn(sv["cat"], dx1, 1, BF16, name=f"d_wout{i}").reshape(N_CHIPS, (B_W + MEM_W) // N_CHIPS, D_MODEL)
        dqm, dkv = mem_bwd(proj, 2 * B_W, kv, dcat, B_W, name=f"mem_bwd{i}")
        w_s = small["b_w_s"][j]
        duv, sg["b_w_s"], dmix, sg["b_v_norm_g"] = sgu_bwd(proj, small["b_v_norm_g"][j], w_s, jnp.swapaxes(w_s, 1, 2),
                                                           small["bias_b"][j], dcat, name=f"sgu_bwd{i}")
        sg["b_s_bias"] = jnp.sum(dmix, axis=-1)
        dproj = jnp.concatenate([duv, dqm], axis=1)
    d_wkv = mm_tn(sv["mem_n"], dkv, 1, BF16, name=f"d_wkv{i}").reshape(N_CHIPS, D_MODEL // N_CHIPS, 2 * MEM_W)
    dmem_n = mm_nt(dkv, wkv, F32, name=f"d_mem_n{i}")
    _, sg["mem_norm_g"] = rms_bwd(mem, small["mem_norm_g"][i], dmem_n, name=f"mem_norm_bwd{i}")
    d_win = mm_tn(sv["h1"], dproj, N_CHIPS, BF16, name=f"d_win{i}")
    dh1 = mm_nt(dproj, win, F32, name=f"d_h1_{i}")
    dx0, sg["mix_norm_g"] = rms_bwd(sv["x0"], small["mix_norm_g"][i], dh1, dres=dx1, name=f"mix_norm_bwd{i}")
    return dx0, [d_wkv, d_win, d_wout], sg


def _exchange_begin(grads, got, place, tag):
    sums = [pair_sum(g, o, place[1:], name=f"pair_sum{tag}_{k}") for k, (g, o) in enumerate(zip(grads, got))]
    send_sems, recv_sems, sums, lands, token = exchange_start(sums, name=f"exchange_start{tag}")
    return (send_sems, recv_sems, sums, lands), token


def _reduce_finish(started, place, after, tag):
    sums, parts = exchange_wait(*started, after, name=f"exchange_wait{tag}")
    halves = [chip_sum(s, p, place, name=f"chip_sum{tag}_{k}") for k, (s, p) in enumerate(zip(sums, parts))]
    return copies_start("join", halves, name=f"join_start_{tag}")


SMALL_SHARDED = ("b_v_norm_g", "ffn_conv_w")
SMALL_FULL_SHAPES = dict(mix_norm_g=(D_MODEL,), ffn_norm_g=(D_MODEL,), mem_norm_g=(D_MODEL,), b_v_norm_g=(B_W,),
                         b_w_s=(B_GROUPS, CHUNK, CHUNK), b_s_bias=(B_GROUPS, CHUNK), ffn_conv_w=(3, 2 * FF),
                         ffn_conv_b=(2 * FF,))
BIG = ("w_mem_kv", "a_w_in", "a_w_out", "b_w_in", "b_w_out", "ffn_w_up", "ffn_w_down")
WEIGHT_ORDER = ("mix_norm_g", "ffn_norm_g", "mem_norm_g", "w_mem_kv", "a_w_in", "a_w_out", "b_w_in", "b_v_norm_g", "b_w_s",
                "b_s_bias", "b_w_out", "ffn_w_up", "ffn_conv_w", "ffn_conv_b", "ffn_w_down", "final_norm_g")


def kernel(x, mem, mix_norm_g, ffn_norm_g, mem_norm_g, w_mem_kv, a_w_in, a_w_out, b_w_in, b_v_norm_g, b_w_s, b_s_bias, b_w_out, ffn_w_up, ffn_conv_w, ffn_conv_b, ffn_w_down, final_norm_g, loss_target, m_mix_norm_g, m_ffn_norm_g, m_mem_norm_g, m_w_mem_kv, m_a_w_in, m_a_w_out, m_b_w_in, m_b_v_norm_g, m_b_w_s, m_b_s_bias, m_b_w_out, m_ffn_w_up, m_ffn_conv_w, m_ffn_conv_b, m_ffn_w_down, m_final_norm_g, v_mix_norm_g, v_ffn_norm_g, v_mem_norm_g, v_w_mem_kv, v_a_w_in, v_a_w_out, v_b_w_in, v_b_v_norm_g, v_b_w_s, v_b_s_bias, v_b_w_out, v_ffn_w_up, v_ffn_conv_w, v_ffn_conv_b, v_ffn_w_down, v_final_norm_g):
    weights = dict(mix_norm_g=mix_norm_g, ffn_norm_g=ffn_norm_g, mem_norm_g=mem_norm_g, w_mem_kv=w_mem_kv, a_w_in=a_w_in,
                   a_w_out=a_w_out, b_w_in=b_w_in, b_v_norm_g=b_v_norm_g, b_w_s=b_w_s, b_s_bias=b_s_bias, b_w_out=b_w_out,
                   ffn_w_up=ffn_w_up, ffn_conv_w=ffn_conv_w, ffn_conv_b=ffn_conv_b, ffn_w_down=ffn_w_down,
                   final_norm_g=final_norm_g)
    mom1 = dict(mix_norm_g=m_mix_norm_g, ffn_norm_g=m_ffn_norm_g, mem_norm_g=m_mem_norm_g, w_mem_kv=m_w_mem_kv,
                a_w_in=m_a_w_in, a_w_out=m_a_w_out, b_w_in=m_b_w_in, b_v_norm_g=m_b_v_norm_g, b_w_s=m_b_w_s,
                b_s_bias=m_b_s_bias, b_w_out=m_b_w_out, ffn_w_up=m_ffn_w_up, ffn_conv_w=m_ffn_conv_w,
                ffn_conv_b=m_ffn_conv_b, ffn_w_down=m_ffn_w_down, final_norm_g=m_final_norm_g)
    mom2 = dict(mix_norm_g=v_mix_norm_g, ffn_norm_g=v_ffn_norm_g, mem_norm_g=v_mem_norm_g, w_mem_kv=v_w_mem_kv,
                a_w_in=v_a_w_in, a_w_out=v_a_w_out, b_w_in=v_b_w_in, b_v_norm_g=v_b_v_norm_g, b_w_s=v_b_w_s,
                b_s_bias=v_b_s_bias, b_w_out=v_b_w_out, ffn_w_up=v_ffn_w_up, ffn_conv_w=v_ffn_conv_w,
                ffn_conv_b=v_ffn_conv_b, ffn_w_down=v_ffn_w_down, final_norm_g=v_final_norm_g)
    chip = 2 * lax.axis_index("x") + lax.axis_index("y")
    place = jnp.stack([chip, lax.axis_index("c")]).astype(jnp.int32)
    x0, mem0, target = x[0], mem[0], loss_target[0]
    depth = DEPTH

    n_cw, n_vg = ffn_conv_w.size, b_v_norm_g.size
    gathered = small_allgather(_pack_rows([ffn_conv_w, b_v_norm_g], 8)).reshape(N_CHIPS, -1)
    conv_w_full = gathered[:, :n_cw].reshape(N_CHIPS, DEPTH, 3, 2 * FF // N_CHIPS).transpose(1, 2, 0, 3).reshape(DEPTH, 3, 2 * FF)
    vgain_full = gathered[:, n_cw:n_cw + n_vg].reshape(N_CHIPS, 2, B_W // N_CHIPS).transpose(1, 0, 2).reshape(2, B_W)
    small = dict(mix_norm_g=mix_norm_g, ffn_norm_g=ffn_norm_g, mem_norm_g=mem_norm_g, b_w_s=b_w_s, ffn_conv_b=ffn_conv_b,
                 ffn_conv_w=conv_w_full, b_v_norm_g=vgain_full,
                 bias_b=jnp.broadcast_to(b_s_bias[..., None], b_s_bias.shape + (CHUNK,)))

    half_layers = 2 * depth
    groups = [(_ffn_weights if b % 2 else _mixer_weights)(b // 2) for b in range(half_layers)]
    tags = [("f" if b % 2 else "m") + str(b // 2) for b in range(half_layers)]

    def start_gather(b, after):
        bufs = [cast_to_slot(weights[n], l, place, name=f"cast_{n}{l}", after=after) for n, l in groups[b]]
        return copies_start("gather", bufs, name=f"gather_start_{tags[b]}")

    def landed(b, after):
        send_sems, recv_sems, bufs, _ = over_ici.pop(b)
        bufs = copies_wait("gather", send_sems, recv_sems, bufs, after, name=f"gather_wait_{tags[b]}")
        return copies_start("forward", bufs, name=f"forward_start_{tags[b]}")

    over_ici, tie = {}, gathered
    for b in range(3):
        over_ici[b] = start_gather(b, tie)
        tie = over_ici[b][3]
    to_sibling = landed(0, tie)
    tie = to_sibling[3]
    w_half, saved_half, h = [], [], x0
    for b in range(half_layers):
        w_half.append(copies_wait("forward", *to_sibling[:3], tie if b == 0 else h, name=f"forward_wait_{tags[b]}"))
        tokens = []
        if b + 1 < half_layers:
            to_sibling = landed(b + 1, w_half[b][0])
            tokens.append(to_sibling[3])
        if b + 3 < half_layers:
            over_ici[b + 3] = start_gather(b + 3, w_half[b][0])
            tokens.append(over_ici[b + 3][3])
        tie = sum(tokens[1:], tokens[0]) if tokens else None
        if b % 2 == 0:
            h, sv = _mixer_fwd(b // 2, h, mem0, w_half[b], small, tie)
        else:
            h, sv = _ffn_fwd(b // 2, h, w_half[b], small, tie)
        saved_half.append(sv)
    w_mix, w_ffn, saved_mix, saved_ffn = w_half[0::2], w_half[1::2], saved_half[0::2], saved_half[1::2]
    loss_row, dh, d_final = final_loss(h, final_norm_g, target)
    loss = lax.psum(loss_row[0, 0], ("x", "y", "c"))

    names = [n for n in WEIGHT_ORDER if n not in BIG]
    small_g = {n: [None] * weights[n].shape[0] for n in names if n != "final_norm_g"}
    big_out = {n: None for n in BIG}

    def keep_small(i, sg):
        for n, g in sg.items():
            small_g[n][i if len(small_g[n]) == depth else i // 2] = g.reshape(SMALL_FULL_SHAPES[n])

    joining = []

    def update(after):
        (send_sems, recv_sems, halves, _), group, tag = joining.pop()
        for (n, l), g in zip(group, copies_wait("join", send_sems, recv_sems, halves, after, name=f"join_wait_{tag}")):
            big_out[n] = adamw_layer(weights[n], mom1[n], mom2[n], l, g, big_out[n], name=f"adamw_{n}{l}")

    def finish_reduce(started, group, after, tag):
        join = _reduce_finish(started, place, after, tag)
        if joining:
            update(join[3])
        joining.append((join, group, tag))

    half_layers = 2 * depth
    swapping, exchanging, tie = None, [], None
    for k in range(half_layers):
        i = depth - 1 - k // 2
        if k % 2 == 0:
            dh, big_g, sg = _ffn_bwd(i, dh, w_ffn[i], small, saved_ffn[i], tie)
            group, tag = _ffn_weights(i), f"f{i}"
        else:
            dh, big_g, sg = _mixer_bwd(i, dh, mem0, w_mix[i], small, saved_mix[i], tie)
            group, tag = _mixer_weights(i), f"m{i}"
        keep_small(i, sg)
        started_now, swap_now, tokens = [], None, []
        if k < half_layers - 2:
            *swap_now, token = swap_start(big_g, name=f"swap_start_{tag}")
            swap_now = (swap_now, group, tag)
        else:
            started, token = _exchange_begin(big_g, swap_halves(big_g, name=f"swap_halves_{tag}"), place, tag)
            started_now.append((started, group, tag))
        tokens.append(token)
        if swapping is not None:
            swap_args, old_group, old_tag = swapping
            grads, got = swap_wait(*swap_args, dh, name=f"swap_wait_{old_tag}")
            started, token = _exchange_begin(grads, got, place, old_tag)
            started_now.append((started, old_group, old_tag))
            tokens.append(token)
        tie = sum(tokens[1:], tokens[0])
        for started, old_group, old_tag in exchanging:
            finish_reduce(started, old_group, dh, old_tag)
        swapping, exchanging = swap_now, started_now
    for started, old_group, old_tag in exchanging:
        finish_reduce(started, old_group, big_out["ffn_w_down"][0], old_tag)

    full_g = {n: (d_final.reshape(-1) if n == "final_norm_g" else jnp.stack(small_g[n])) for n in names}
    shapes = [full_g[n].shape for n in names]
    reduced = small_allreduce(_pack_rows([full_g[n] for n in names], 8), joining[0][0][3])
    update(reduced)
    summed = dict(zip(names, _unpack_rows(reduced, shapes)))
    for n in SMALL_SHARDED:
        width = weights[n].shape[-1]
        summed[n] = lax.dynamic_slice_in_dim(summed[n], chip * width, width, axis=summed[n].ndim - 1)
    own_shapes = [weights[n].shape for n in names]
    pack = lambda d: _pack_rows([d[n] for n in names], 128)
    small_out = [_unpack_rows(b, own_shapes) for b in adamw_flat(pack(weights), pack(summed), pack(mom1), pack(mom2))]
    outs = {}
    for k, n in enumerate(names):
        outs[n] = (summed[n], small_out[0][k], small_out[1][k], small_out[2][k])
    outs.update(big_out)
    return (loss, dh[None], *[outs[n][0] for n in WEIGHT_ORDER], *[outs[n][1] for n in WEIGHT_ORDER],
            *[outs[n][2] for n in WEIGHT_ORDER], *[outs[n][3] for n in WEIGHT_ORDER])
```

```python
import functools
import math

import numpy as np
import jax
import jax.numpy as jnp
from jax import lax
from jax.experimental import pallas as pl
from jax.experimental.pallas import tpu as pltpu

F32 = jnp.float32
BF16 = jnp.bfloat16
MESH = pl.DeviceIdType.MESH

D_MODEL = 2048
SEQ = 2048
DEPTH = 4
EPS = 1e-6
NEG = -1e30
HEAD = 128
A_PATTERNS = ((128, 1), (512, 4), (2048, 16))
A_QKV_W = 1536
A_OUT_W = 512
A_IN = 5120
QBLK = 128
N_SIDE = 64
CHUNK = 128
B_GROUPS = 12
B_W = 1536
B_IN = 3584
MEM_LEN = 256
MEM_HEADS = 4
MEM_W = 512
FF = 5632
ADAM_LR, ADAM_B1, ADAM_B2, ADAM_EPS, ADAM_WD, ADAM_STEP = 0.001, 0.9, 0.999, 1e-08, 0.01, 10
N_CHIPS = 4

LANES = 128
V7X_VMEM_LIMIT = 56 * 1024 * 1024


def _cp(*sem):
    return pltpu.CompilerParams(dimension_semantics=sem, vmem_limit_bytes=V7X_VMEM_LIMIT)


def _pick(dim, prefs):
    for p in prefs:
        if dim % p == 0:
            return p
    raise ValueError(f"no tile for {dim} in {prefs}")


def _gelu_parts(x):
    cdf = 0.5 * (1.0 + lax.erf(x * (1.0 / math.sqrt(2.0))))
    pdf = jnp.exp(-0.5 * x * x) * (1.0 / math.sqrt(2.0 * math.pi))
    return x * cdf, cdf + x * pdf


def _gelu(x):
    return 0.5 * x * (1.0 + lax.erf(x * (1.0 / math.sqrt(2.0))))


TM_PREFS = (1024, 512, 256, 128)
TN_PREFS = (1408, 1280, 1024, 896, 512, 256, 128)
TK_PREFS = (2816, 2048, 1408, 1280, 1024, 896, 512, 256, 128)


def _mm_body(nk, dims, has_res):
    def body(*refs):
        if has_res:
            a_ref, b_ref, r_ref, o_ref = refs[:4]
        else:
            a_ref, b_ref, o_ref = refs[:3]
            r_ref = None
        part = lax.dot_general(a_ref[...].astype(BF16), b_ref[...].astype(BF16), dims,
                               preferred_element_type=F32)
        if nk == 1:
            if has_res:
                part = part + r_ref[...]
            o_ref[...] = part.astype(o_ref.dtype)
            return
        acc_ref = refs[-1]
        k = pl.program_id(2)

        @pl.when(k == 0)
        def _():
            acc_ref[...] = part

        @pl.when(k > 0)
        def _():
            acc_ref[...] += part

        @pl.when(k == nk - 1)
        def _():
            tot = acc_ref[...]
            if has_res:
                tot = tot + r_ref[...]
            o_ref[...] = tot.astype(o_ref.dtype)
    return body


def mm_nn(a, w, out_dtype, res=None, name="mm_nn"):
    m, kw = a.shape
    ns_, kw2, nsz = w.shape
    assert kw == kw2
    n = ns_ * nsz
    tm, tn, tk = _pick(m, TM_PREFS), _pick(nsz, TN_PREFS), _pick(kw, TK_PREFS)
    nb, nk = nsz // tn, kw // tk
    in_specs = [pl.BlockSpec((tm, tk), lambda i, j, k: (i, k)),
                pl.BlockSpec((None, tk, tn), lambda i, j, k: (j // nb, k, j % nb))]
    args = [a, w]
    if res is not None:
        in_specs.append(pl.BlockSpec((tm, tn), lambda i, j, k: (i, j)))
        args.append(res)
    return pl.pallas_call(
        _mm_body(nk, (((1,), (0,)), ((), ())), res is not None),
        out_shape=jax.ShapeDtypeStruct((m, n), out_dtype),
        grid=(m // tm, n // tn, nk), in_specs=in_specs,
        out_specs=pl.BlockSpec((tm, tn), lambda i, j, k: (i, j)),
        scratch_shapes=[pltpu.VMEM((tm, tn), F32)] if nk > 1 else [],
        compiler_params=_cp("parallel", "parallel", "arbitrary"), name=name)(*args)


def mm_nt(g, w, out_dtype, name="mm_nt", after=None):
    m, n = g.shape
    ns_, kw, nsz = w.shape
    assert n == ns_ * nsz
    tm, tn, tk = _pick(m, TM_PREFS), _pick(kw, TN_PREFS), _pick(nsz, TK_PREFS)
    nb, nk = nsz // tk, n // tk
    body = _mm_body(nk, (((1,), (1,)), ((), ())), False)
    tied = [] if after is None else [after]
    return pl.pallas_call(
        (lambda g_ref, w_ref, *rest: body(g_ref, w_ref, *rest[len(tied):])),
        out_shape=jax.ShapeDtypeStruct((m, kw), out_dtype),
        grid=(m // tm, kw // tn, nk),
        in_specs=[pl.BlockSpec((tm, tk), lambda i, j, k: (i, k)),
                  pl.BlockSpec((None, tn, tk), lambda i, j, k: (k // nb, j, k % nb))] + [ANY] * len(tied),
        out_specs=pl.BlockSpec((tm, tn), lambda i, j, k: (i, j)),
        scratch_shapes=[pltpu.VMEM((tm, tn), F32)] if nk > 1 else [],
        compiler_params=_cp("parallel", "parallel", "arbitrary"), name=name)(g, w, *tied)


def mm_tn(a, g, n_shards, out_dtype, name="mm_tn"):
    t, kw = a.shape
    t2, n = g.shape
    assert t == t2
    nsz = n // n_shards
    tm, tn, tk = _pick(kw, TM_PREFS), _pick(nsz, TN_PREFS), _pick(t, TK_PREFS)
    nb, nk = nsz // tn, t // tk
    return pl.pallas_call(
        _mm_body(nk, (((0,), (0,)), ((), ())), False),
        out_shape=jax.ShapeDtypeStruct((n_shards, kw, nsz), out_dtype),
        grid=(kw // tm, n // tn, nk),
        in_specs=[pl.BlockSpec((tk, tm), lambda i, j, k: (k, i)),
                  pl.BlockSpec((tk, tn), lambda i, j, k: (k, j))],
        out_specs=pl.BlockSpec((None, tm, tn), lambda i, j, k: (j // nb, i, j % nb)),
        scratch_shapes=[pltpu.VMEM((tm, tn), F32)] if nk > 1 else [],
        compiler_params=_cp("parallel", "parallel", "arbitrary"), name=name)(a, g)


ROW_TILE = 256


def _rms_stats(x):
    r = lax.rsqrt(jnp.mean(x * x, axis=-1, keepdims=True) + EPS)
    return r, x * r


def _rms_back(xh, r, g, dh):
    u = dh * g
    return r * (u - xh * jnp.mean(u * xh, axis=-1, keepdims=True))


def rms_fwd(x, g, out_dtype, name="rms_fwd", after=None):
    rows, d = x.shape
    tr = _pick(rows, (ROW_TILE, 128))
    tied = [] if after is None else [after]

    def body(x_ref, g_ref, *rest):
        o_ref = rest[-1]
        _, xh = _rms_stats(x_ref[...])
        o_ref[...] = (xh * g_ref[...]).astype(o_ref.dtype)

    return pl.pallas_call(
        body, out_shape=jax.ShapeDtypeStruct((rows, d), out_dtype), grid=(rows // tr,),
        in_specs=[pl.BlockSpec((tr, d), lambda i: (i, 0)), pl.BlockSpec((1, d), lambda i: (0, 0))] + [ANY] * len(tied),
        out_specs=pl.BlockSpec((tr, d), lambda i: (i, 0)),
        compiler_params=_cp("parallel"), name=name)(x, g.reshape(1, d), *tied)


def rms_bwd(x, g, dh, dres=None, name="rms_bwd"):
    rows, d = x.shape
    tr = _pick(rows, (ROW_TILE, 128))
    has_res = dres is not None

    def body(*refs):
        if has_res:
            x_ref, g_ref, dh_ref, dres_ref, dx_ref, dg_ref = refs
        else:
            x_ref, g_ref, dh_ref, dx_ref, dg_ref = refs
        r, xh = _rms_stats(x_ref[...])
        dh_ = dh_ref[...].astype(F32)
        part = jnp.sum(dh_ * xh, axis=0, keepdims=True)

        @pl.when(pl.program_id(0) == 0)
        def _():
            dg_ref[...] = part

        @pl.when(pl.program_id(0) > 0)
        def _():
            dg_ref[...] += part

        dx = _rms_back(xh, r, g_ref[...], dh_)
        if has_res:
            dx = dx + dres_ref[...]
        dx_ref[...] = dx

    row_spec = pl.BlockSpec((tr, d), lambda i: (i, 0))
    vec_spec = pl.BlockSpec((1, d), lambda i: (0, 0))
    args = [x, g.reshape(1, d), dh] + ([dres] if has_res else [])
    return pl.pallas_call(
        body, out_shape=(jax.ShapeDtypeStruct((rows, d), F32), jax.ShapeDtypeStruct((1, d), F32)),
        grid=(rows // tr,), in_specs=[row_spec, vec_spec, row_spec] + ([row_spec] if has_res else []),
        out_specs=(row_spec, vec_spec), compiler_params=_cp("arbitrary"), name=name)(*args)


def final_loss(x, g, target, name="final_loss"):
    rows, d = x.shape
    tr = _pick(rows, (ROW_TILE, 128))

    def body(x_ref, g_ref, t_ref, loss_ref, dx_ref, dg_ref):
        r, xh = _rms_stats(x_ref[...])
        gain = g_ref[...]
        err = xh * gain - t_ref[...]
        sq = jnp.sum(jnp.sum(err * err, axis=1, keepdims=True), axis=0, keepdims=True) * (0.5 / d)
        dy = err * (1.0 / d)
        part = jnp.sum(dy * xh, axis=0, keepdims=True)

        @pl.when(pl.program_id(0) == 0)
        def _():
            dg_ref[...] = part
            loss_ref[...] = jnp.broadcast_to(sq, loss_ref.shape)

        @pl.when(pl.program_id(0) > 0)
        def _():
            dg_ref[...] += part
            loss_ref[...] += jnp.broadcast_to(sq, loss_ref.shape)

        dx_ref[...] = _rms_back(xh, r, gain, dy)

    row_spec = pl.BlockSpec((tr, d), lambda i: (i, 0))
    vec_spec = pl.BlockSpec((1, d), lambda i: (0, 0))
    return pl.pallas_call(
        body, out_shape=(jax.ShapeDtypeStruct((1, LANES), F32), jax.ShapeDtypeStruct((rows, d), F32),
                         jax.ShapeDtypeStruct((1, d), F32)),
        grid=(rows // tr,), in_specs=[row_spec, vec_spec, row_spec],
        out_specs=(pl.BlockSpec((1, LANES), lambda i: (0, 0)), row_spec, vec_spec),
        compiler_params=_cp("arbitrary"), name=name)(x, g.reshape(1, d), target)


def _alibi_slopes():
    return (2.0 ** (-8.0 * (np.arange(12) + 1) / 12)).astype(np.float32)


def _band_scores(q, k, q0, start, wk, slope):
    s = lax.dot_general(q, k, (((1,), (1,)), ((), ())), preferred_element_type=F32) * (HEAD ** -0.5)
    qpos = q0 + lax.broadcasted_iota(jnp.int32, (QBLK, wk), 0)
    kpos = start + lax.broadcasted_iota(jnp.int32, (QBLK, wk), 1)
    rel = jnp.abs(qpos - kpos)
    return jnp.where(rel <= N_SIDE, s - slope * rel.astype(F32), NEG)


def _attn_geometry(seq, dilation):
    length = seq // dilation
    return length, length // QBLK, min(2 * QBLK, length)


def _attn_window(n, length, wk):
    q0 = pl.multiple_of(n * QBLK, QBLK)
    start = pl.multiple_of(jnp.clip(n * QBLK - N_SIDE, 0, length - wk), N_SIDE)
    return q0, start


def attn_fwd(proj, group, name):
    seq = proj.shape[0]
    dilation = A_PATTERNS[group][1]
    length, nblk, wk = _attn_geometry(seq, dilation)
    cols = A_IN // HEAD
    pv = proj.reshape(length, dilation * A_IN)

    def body(slope_ref, q_ref, k_ref, v_ref, o_ref, lse_ref):
        slope = slope_ref[group * 4 + pl.program_id(1)] * float(dilation)

        def blk(n, carry):
            q0, start = _attn_window(n, length, wk)
            q = q_ref[pl.ds(q0, QBLK), :].astype(BF16)
            k = k_ref[pl.ds(start, wk), :].astype(BF16)
            v = v_ref[pl.ds(start, wk), :].astype(BF16)
            s = _band_scores(q, k, q0, start, wk, slope)
            m = jnp.max(s, axis=-1, keepdims=True)
            p = jnp.exp(s - m)
            l = jnp.sum(p, axis=-1, keepdims=True)
            o = jnp.dot(p.astype(BF16), v, preferred_element_type=F32) / l
            o_ref[pl.ds(q0, QBLK), :] = o
            lse_ref[pl.ds(q0, QBLK), :] = jnp.broadcast_to(m + jnp.log(l), (QBLK, HEAD))
            return carry

        lax.fori_loop(0, nblk, blk, 0)

    def part(p):
        return pl.BlockSpec((length, HEAD), lambda r, h: (0, r * cols + p * 12 + group * 4 + h))

    out_spec = pl.BlockSpec((length, HEAD), lambda r, h: (0, r * 4 + h))
    o, lse = pl.pallas_call(
        body, out_shape=(jax.ShapeDtypeStruct((length, dilation * A_OUT_W), F32),) * 2,
        grid=(dilation, 4),
        in_specs=[pl.BlockSpec(memory_space=pltpu.SMEM), part(0), part(1), part(2)],
        out_specs=(out_spec, out_spec), compiler_params=_cp("parallel", "parallel"), name=name,
    )(jnp.asarray(_alibi_slopes()), pv, pv, pv)
    return o.reshape(seq, A_OUT_W), lse.reshape(seq, A_OUT_W)


def attn_combine(os_, lses, name="attn_combine"):
    seq = os_[0].shape[0]
    tr = ROW_TILE

    def body(o0, o1, o2, l0, l1, l2, c_ref, lse_ref):
        a, b, c = l0[...], l1[...], l2[...]
        m = jnp.maximum(jnp.maximum(a, b), c)
        ea, eb, ec = jnp.exp(a - m), jnp.exp(b - m), jnp.exp(c - m)
        den = ea + eb + ec
        c_ref[...] = (ea * o0[...] + eb * o1[...] + ec * o2[...]) / den
        lse_ref[...] = m + jnp.log(den)

    spec = pl.BlockSpec((tr, A_OUT_W), lambda i: (i, 0))
    return pl.pallas_call(
        body, out_shape=(jax.ShapeDtypeStruct((seq, A_OUT_W), F32),) * 2, grid=(seq // tr,),
        in_specs=[spec] * 6, out_specs=(spec, spec), compiler_params=_cp("parallel"), name=name)(*os_, *lses)


def attn_bwd(proj, dcat, comb, lse, group, name):
    seq = proj.shape[0]
    dilation = A_PATTERNS[group][1]
    length, nblk, wk = _attn_geometry(seq, dilation)
    cols = A_IN // HEAD
    pv = proj.reshape(length, dilation * A_IN)
    view = lambda a: a.reshape(length, dilation * A_OUT_W)
    scale = HEAD ** -0.5

    def body(slope_ref, q_ref, k_ref, v_ref, do_ref, c_ref, lse_ref, dq_ref, dk_ref, dv_ref, dk_acc, dv_acc):
        slope = slope_ref[group * 4 + pl.program_id(1)] * float(dilation)
        dk_acc[...] = jnp.zeros_like(dk_acc)
        dv_acc[...] = jnp.zeros_like(dv_acc)

        def blk(n, carry):
            q0, start = _attn_window(n, length, wk)
            rows = pl.ds(q0, QBLK)
            keys = pl.ds(start, wk)
            q = q_ref[rows, :].astype(BF16)
            k = k_ref[keys, :].astype(BF16)
            v = v_ref[keys, :].astype(BF16)
            do = do_ref[rows, :]
            s = _band_scores(q, k, q0, start, wk, slope)
            p = jnp.exp(s - lse_ref[rows, :][:, :1])
            delta = jnp.sum(do * c_ref[rows, :], axis=-1, keepdims=True)
            do16 = do.astype(BF16)
            dp = lax.dot_general(do16, v, (((1,), (1,)), ((), ())), preferred_element_type=F32)
            ds = (p * (dp - delta) * scale).astype(BF16)
            p16 = p.astype(BF16)
            dq_ref[rows, :] = jnp.dot(ds, k, preferred_element_type=F32).astype(dq_ref.dtype)
            dk_acc[keys, :] += lax.dot_general(ds, q, (((0,), (0,)), ((), ())), preferred_element_type=F32)
            dv_acc[keys, :] += lax.dot_general(p16, do16, (((0,), (0,)), ((), ())), preferred_element_type=F32)
            return carry

        lax.fori_loop(0, nblk, blk, 0)
        dk_ref[...] = dk_acc[...].astype(dk_ref.dtype)
        dv_ref[...] = dv_acc[...].astype(dv_ref.dtype)

    def part(p):
        return pl.BlockSpec((length, HEAD), lambda r, h: (0, r * cols + p * 12 + group * 4 + h))

    hs = pl.BlockSpec((length, HEAD), lambda r, h: (0, r * 4 + h))
    do_cols = dcat.shape[1] // HEAD
    do_spec = pl.BlockSpec((length, HEAD), lambda r, h: (0, r * do_cols + h))
    outs = pl.pallas_call(
        body, out_shape=(jax.ShapeDtypeStruct((length, dilation * A_OUT_W), BF16),) * 3,
        grid=(dilation, 4),
        in_specs=[pl.BlockSpec(memory_space=pltpu.SMEM), part(0), part(1), part(2), do_spec, hs, hs],
        out_specs=(hs, hs, hs),
        scratch_shapes=[pltpu.VMEM((length, HEAD), F32), pltpu.VMEM((length, HEAD), F32)],
        compiler_params=_cp("parallel", "parallel"), name=name,
    )(jnp.asarray(_alibi_slopes()), pv, pv, pv, dcat.reshape(length, dilation * dcat.shape[1]), view(comb), view(lse))
    return tuple(o.reshape(seq, A_OUT_W) for o in outs)


MEM_ROW_TILE = 512


def _mem_probs(q, k):
    s = lax.dot_general(q, k, (((1,), (1,)), ((), ())), preferred_element_type=F32) * (HEAD ** -0.5)
    p = jnp.exp(s - jnp.max(s, axis=-1, keepdims=True))
    return p / jnp.sum(p, axis=-1, keepdims=True)


def mem_fwd(proj, q_col, kv, name="mem_fwd"):
    seq = proj.shape[0]
    qb = q_col // HEAD

    def body(q_ref, k_ref, v_ref, o_ref):
        p = _mem_probs(q_ref[...].astype(BF16), k_ref[...].astype(BF16))
        o_ref[...] = jnp.dot(p.astype(BF16), v_ref[...].astype(BF16), preferred_element_type=F32).astype(o_ref.dtype)

    return pl.pallas_call(
        body, out_shape=jax.ShapeDtypeStruct((seq, MEM_W), BF16), grid=(MEM_HEADS, seq // MEM_ROW_TILE),
        in_specs=[pl.BlockSpec((MEM_ROW_TILE, HEAD), lambda h, i: (i, qb + h)),
                  pl.BlockSpec((MEM_LEN, HEAD), lambda h, i: (0, h)),
                  pl.BlockSpec((MEM_LEN, HEAD), lambda h, i: (0, MEM_HEADS + h))],
        out_specs=pl.BlockSpec((MEM_ROW_TILE, HEAD), lambda h, i: (i, h)),
        compiler_params=_cp("parallel", "parallel"), name=name)(proj, kv, kv)


def mem_bwd(proj, q_col, kv, dcat, do_col, name="mem_bwd"):
    seq = proj.shape[0]
    qb, ob = q_col // HEAD, do_col // HEAD
    scale = HEAD ** -0.5

    def body(q_ref, k_ref, v_ref, do_ref, dq_ref, dk_ref, dv_ref):
        q = q_ref[...].astype(BF16)
        k = k_ref[...].astype(BF16)
        v = v_ref[...].astype(BF16)
        do = do_ref[...].astype(BF16)
        p = _mem_probs(q, k)
        dp = lax.dot_general(do, v, (((1,), (1,)), ((), ())), preferred_element_type=F32)
        ds = (p * (dp - jnp.sum(dp * p, axis=-1, keepdims=True)) * scale).astype(BF16)
        dq_ref[...] = jnp.dot(ds, k, preferred_element_type=F32).astype(dq_ref.dtype)
        dk = lax.dot_general(ds, q, (((0,), (0,)), ((), ())), preferred_element_type=F32)
        dv = lax.dot_general(p.astype(BF16), do, (((0,), (0,)), ((), ())), preferred_element_type=F32)

        @pl.when(pl.program_id(1) == 0)
        def _():
            dk_ref[...] = dk
            dv_ref[...] = dv

        @pl.when(pl.program_id(1) > 0)
        def _():
            dk_ref[...] += dk
            dv_ref[...] += dv

    dq, dk, dv = pl.pallas_call(
        body, out_shape=(jax.ShapeDtypeStruct((seq, MEM_W), BF16), jax.ShapeDtypeStruct((MEM_LEN, MEM_W), F32),
                         jax.ShapeDtypeStruct((MEM_LEN, MEM_W), F32)),
        grid=(MEM_HEADS, seq // MEM_ROW_TILE),
        in_specs=[pl.BlockSpec((MEM_ROW_TILE, HEAD), lambda h, i: (i, qb + h)),
                  pl.BlockSpec((MEM_LEN, HEAD), lambda h, i: (0, h)),
                  pl.BlockSpec((MEM_LEN, HEAD), lambda h, i: (0, MEM_HEADS + h)),
                  pl.BlockSpec((MEM_ROW_TILE, HEAD), lambda h, i: (i, ob + h))],
        out_specs=(pl.BlockSpec((MEM_ROW_TILE, HEAD), lambda h, i: (i, h)),
                   pl.BlockSpec((MEM_LEN, HEAD), lambda h, i: (0, h)),
                   pl.BlockSpec((MEM_LEN, HEAD), lambda h, i: (0, h))),
        compiler_params=_cp("parallel", "arbitrary"), name=name)(proj, kv, kv, dcat)
    return dq, jnp.concatenate([dk, dv], axis=1)


def _sgu_front(x, gain):
    uv, duv = _gelu_parts(x)
    u, v = uv[:, :B_W], uv[:, B_W:]
    r, vh = _rms_stats(v)
    return u, duv, r, vh, vh * gain


def sgu_fwd(proj, gain, w_s, bias_b, name="sgu_fwd"):
    seq = proj.shape[0]

    def body(x_ref, gain_ref, ws_ref, bias_ref, o_ref):
        u, _, _, _, vn = _sgu_front(x_ref[...], gain_ref[...])
        for g in range(B_GROUPS):
            cs = slice(g * CHUNK, (g + 1) * CHUNK)
            mixed = jnp.dot(ws_ref[g].astype(BF16), vn[:, cs].astype(BF16), preferred_element_type=F32) + bias_ref[g]
            o_ref[:, cs] = (u[:, cs] * mixed).astype(o_ref.dtype)

    full = lambda shape: pl.BlockSpec(shape, lambda c: (0,) * len(shape))
    return pl.pallas_call(
        body, out_shape=jax.ShapeDtypeStruct((seq, B_W), BF16), grid=(seq // CHUNK,),
        in_specs=[pl.BlockSpec((CHUNK, 2 * B_W), lambda c: (c, 0)), full((1, B_W)),
                  full((B_GROUPS, CHUNK, CHUNK)), full((B_GROUPS, CHUNK, CHUNK))],
        out_specs=pl.BlockSpec((CHUNK, B_W), lambda c: (c, 0)),
        compiler_params=_cp("parallel"), name=name)(proj, gain.reshape(1, B_W), w_s, bias_b)


def sgu_bwd(proj, gain, w_s, w_s_t, bias_b, dcat, name="sgu_bwd"):
    seq = proj.shape[0]

    def body(x_ref, gain_ref, ws_ref, wst_ref, bias_ref, do_ref, dx_ref, dws_ref, dmix_ref, dgain_ref, dvn_ref):
        first = pl.program_id(0) == 0
        gain = gain_ref[...]
        u, duv, r, vh, vn = _sgu_front(x_ref[...], gain)
        do = do_ref[...]
        for g in range(B_GROUPS):
            cs = slice(g * CHUNK, (g + 1) * CHUNK)
            vg = vn[:, cs].astype(BF16)
            mixed = jnp.dot(ws_ref[g].astype(BF16), vg, preferred_element_type=F32) + bias_ref[g]
            dx_ref[:, cs] = (do[:, cs] * mixed * duv[:, cs]).astype(dx_ref.dtype)
            dmixed = do[:, cs] * u[:, cs]
            dm16 = dmixed.astype(BF16)
            dws = lax.dot_general(dm16, vg, (((1,), (1,)), ((), ())), preferred_element_type=F32)
            dvn_ref[:, cs] = jnp.dot(wst_ref[g].astype(BF16), dm16, preferred_element_type=F32)

            @pl.when(first)
            def _():
                dws_ref[g] = dws
                dmix_ref[g] = dmixed

            @pl.when(jnp.logical_not(first))
            def _():
                dws_ref[g] += dws
                dmix_ref[g] += dmixed

        dvn = dvn_ref[...]
        dgain = jnp.sum(dvn * vh, axis=0, keepdims=True)

        @pl.when(first)
        def _():
            dgain_ref[...] = dgain

        @pl.when(jnp.logical_not(first))
        def _():
            dgain_ref[...] += dgain

        dv = _rms_back(vh, r, gain, dvn)
        dx_ref[:, B_W:] = (dv * duv[:, B_W:]).astype(dx_ref.dtype)

    full = lambda shape: pl.BlockSpec(shape, lambda c: (0,) * len(shape))
    mats = full((B_GROUPS, CHUNK, CHUNK))
    return pl.pallas_call(
        body, out_shape=(jax.ShapeDtypeStruct((seq, 2 * B_W), BF16), jax.ShapeDtypeStruct((B_GROUPS, CHUNK, CHUNK), F32),
                         jax.ShapeDtypeStruct((B_GROUPS, CHUNK, CHUNK), F32), jax.ShapeDtypeStruct((1, B_W), F32)),
        grid=(seq // CHUNK,),
        in_specs=[pl.BlockSpec((CHUNK, 2 * B_W), lambda c: (c, 0)), full((1, B_W)), mats, mats, mats,
                  pl.BlockSpec((CHUNK, B_W), lambda c: (c, 0))],
        out_specs=(pl.BlockSpec((CHUNK, 2 * B_W), lambda c: (c, 0)), mats, mats, full((1, B_W))),
        scratch_shapes=[pltpu.VMEM((CHUNK, B_W), F32)],
        compiler_params=_cp("arbitrary"), name=name)(proj, gain.reshape(1, B_W), w_s, w_s_t, bias_b, dcat)


FFN_COLS = 128
FFN_ROWS = 32
SUBLANES = 8


def _window(ref, r0, first, last):
    cols = ref.shape[1]
    pad = jnp.zeros((SUBLANES, cols), F32)
    if first:
        return jnp.concatenate([pad, ref[pl.ds(0, FFN_ROWS + SUBLANES), :]], axis=0)
    if last:
        return jnp.concatenate([ref[pl.ds(r0 - SUBLANES, FFN_ROWS + SUBLANES), :], pad], axis=0)
    return ref[pl.ds(pl.multiple_of(r0 - SUBLANES, SUBLANES), FFN_ROWS + 2 * SUBLANES), :]


def _taps(win):
    mid = slice(SUBLANES, SUBLANES + FFN_ROWS)
    return pltpu.roll(win, 1, 0)[mid], win[mid], pltpu.roll(win, win.shape[0] - 1, 0)[mid]


def _row_steps(seq, step, carry):
    n = seq // FFN_ROWS
    carry = step(0, True, False, carry)
    carry = lax.fori_loop(1, n - 1, lambda i, c: step(pl.multiple_of(i * FFN_ROWS, FFN_ROWS), False, False, c), carry)
    return step(seq - FFN_ROWS, False, True, carry)


def _conv3(taps, w, b):
    prev, cur, nxt = taps
    return prev * w[0:1] + cur * w[1:2] + nxt * w[2:3] + b


def _fold(x):
    return jnp.sum(x.reshape(FFN_ROWS // SUBLANES, SUBLANES, x.shape[1]), axis=0)


FFN_FWD_COLS = 256


def _taps_whole(a):
    n = a.shape[0]
    rows = lax.broadcasted_iota(jnp.int32, a.shape, 0)
    return (jnp.where(rows == 0, 0.0, pltpu.roll(a, 1, 0)), a, jnp.where(rows == n - 1, 0.0, pltpu.roll(a, n - 1, 0)))


def ffn_act_fwd(a, conv_w, conv_b, name="ffn_act_fwd"):
    seq = a.shape[0]
    nb = FF // FFN_FWD_COLS

    def body(ag_ref, av_ref, wg_ref, wv_ref, bg_ref, bv_ref, o_ref):
        gate = _conv3(_taps_whole(ag_ref[...]), wg_ref[...], bg_ref[...])
        val = _conv3(_taps_whole(av_ref[...]), wv_ref[...], bv_ref[...])
        o_ref[...] = (_gelu(gate) * val).astype(o_ref.dtype)

    col = lambda rows, off: pl.BlockSpec((rows, FFN_FWD_COLS), lambda j: (0, j + off))
    cb = conv_b.reshape(1, 2 * FF)
    return pl.pallas_call(
        body, out_shape=jax.ShapeDtypeStruct((seq, FF), BF16), grid=(nb,),
        in_specs=[col(seq, 0), col(seq, nb), col(3, 0), col(3, nb), col(1, 0), col(1, nb)],
        out_specs=col(seq, 0), compiler_params=_cp("parallel"), name=name)(a, a, conv_w, conv_w, cb, cb)


def ffn_act_bwd(a, conv_w, conv_b, dact, name="ffn_act_bwd"):
    seq = a.shape[0]
    nb = FF // FFN_COLS

    def body(ag_ref, av_ref, wg_ref, wv_ref, bg_ref, bv_ref, d_ref, dag_ref, dav_ref, dwg_ref, dwv_ref, dbg_ref, dbv_ref,
             dcg_ref, dcv_ref):
        wg, wv, bg, bv = wg_ref[...], wv_ref[...], bg_ref[...], bv_ref[...]

        def conv_grads(r0, first, last, sums):
            g_taps = _taps(_window(ag_ref, r0, first, last))
            v_taps = _taps(_window(av_ref, r0, first, last))
            act, dact_dgate = _gelu_parts(_conv3(g_taps, wg, bg))
            d = d_ref[pl.ds(r0, FFN_ROWS), :].astype(F32)
            dcg = d * _conv3(v_taps, wv, bv) * dact_dgate
            dcv = d * act
            dcg_ref[pl.ds(r0, FFN_ROWS), :] = dcg
            dcv_ref[pl.ds(r0, FFN_ROWS), :] = dcv
            new = [_fold(dcg)] + [_fold(dcg * t) for t in g_taps] + [_fold(dcv)] + [_fold(dcv * t) for t in v_taps]
            return tuple(s + n for s, n in zip(sums, new))

        zero = jnp.zeros((SUBLANES, FFN_COLS), F32)
        sums = _row_steps(seq, conv_grads, (zero,) * 8)
        total = [jnp.sum(s, axis=0, keepdims=True) for s in sums]
        dbg_ref[...] = total[0]
        dbv_ref[...] = total[4]
        for k in range(3):
            dwg_ref[k:k + 1, :] = total[1 + k]
            dwv_ref[k:k + 1, :] = total[5 + k]

        def conv_transpose(r0, first, last, carry):
            for dc_ref, w, da_ref in ((dcg_ref, wg, dag_ref), (dcv_ref, wv, dav_ref)):
                prev, cur, nxt = _taps(_window(dc_ref, r0, first, last))
                da_ref[pl.ds(r0, FFN_ROWS), :] = (nxt * w[0:1] + cur * w[1:2] + prev * w[2:3]).astype(da_ref.dtype)
            return carry

        _row_steps(seq, conv_transpose, 0)

    col = lambda rows, off: pl.BlockSpec((rows, FFN_COLS), lambda j: (0, j + off))
    cb = conv_b.reshape(1, 2 * FF)
    dag, dav, dwg, dwv, dbg, dbv = pl.pallas_call(
        body, out_shape=(jax.ShapeDtypeStruct((seq, FF), BF16),) * 2 + (jax.ShapeDtypeStruct((3, FF), F32),) * 2
        + (jax.ShapeDtypeStruct((1, FF), F32),) * 2, grid=(nb,),
        in_specs=[col(seq, 0), col(seq, nb), col(3, 0), col(3, nb), col(1, 0), col(1, nb), col(seq, 0)],
        out_specs=(col(seq, 0), col(seq, 0), col(3, 0), col(3, 0), col(1, 0), col(1, 0)),
        scratch_shapes=[pltpu.VMEM((seq, FFN_COLS), F32), pltpu.VMEM((seq, FFN_COLS), F32)],
        compiler_params=_cp("parallel"), name=name)(a, a, conv_w, conv_w, cb, cb, dact)
    cat = lambda p, q: jnp.concatenate([p, q], axis=1)
    return cat(dag, dav), cat(dwg, dwv), cat(dbg, dbv)


def _adam_math(w, g, m, v):
    m = ADAM_B1 * m + (1.0 - ADAM_B1) * g
    v = ADAM_B2 * v + (1.0 - ADAM_B2) * (g * g)
    m_hat = m / (1.0 - ADAM_B1 ** ADAM_STEP)
    v_hat = v / (1.0 - ADAM_B2 ** ADAM_STEP)
    return -ADAM_LR * (m_hat / (jnp.sqrt(v_hat) + ADAM_EPS) + ADAM_WD * w), m, v


def _row_tile(rows, cols):
    return _pick(rows, (256, 128, 64)) if cols <= 1024 else _pick(rows, (128, 64))


BF16_ROWS = 16
STREAM_BLOCK_BYTES = 3 * 1024 * 1024


def _stream_rows(rows, cols, itemsize):
    fits = [r for r in range(BF16_ROWS, rows + 1, BF16_ROWS) if rows % r == 0 and r * cols * itemsize <= STREAM_BLOCK_BYTES]
    return max(fits)


def adamw_layer(w_all, m_all, v_all, layer, g, prev, name):
    n, rows, cols = w_all.shape
    tr = _row_tile(rows, cols)

    def body(w_ref, m_ref, v_ref, g_ref, *rest):
        go_ref, d_ref, mo_ref, vo_ref = rest[-4:]
        g_ = g_ref[...]
        d, m_, v_ = _adam_math(w_ref[...], g_, m_ref[...], v_ref[...])
        go_ref[...] = g_
        d_ref[...] = d
        mo_ref[...] = m_
        vo_ref[...] = v_

    lay = pl.BlockSpec((None, tr, cols), lambda i: (layer, i, 0))
    in_specs = [lay, lay, lay, pl.BlockSpec((tr, cols), lambda i: (i, 0))]
    args = [w_all, m_all, v_all, g]
    aliases = {}
    if prev is not None:
        in_specs += [pl.BlockSpec(memory_space=pl.ANY)] * 4
        args += list(prev)
        aliases = {4 + k: k for k in range(4)}
    return pl.pallas_call(
        body, out_shape=(jax.ShapeDtypeStruct(w_all.shape, F32),) * 4, grid=(rows // tr,),
        in_specs=in_specs, out_specs=(lay,) * 4, input_output_aliases=aliases,
        compiler_params=_cp("parallel"), name=name)(*args)


def adamw_flat(w, g, m, v, name="adamw_small"):
    rows, cols = w.shape
    tr = _pick(rows, (128, 8))

    def body(w_ref, g_ref, m_ref, v_ref, d_ref, mo_ref, vo_ref):
        d_ref[...], mo_ref[...], vo_ref[...] = _adam_math(w_ref[...], g_ref[...], m_ref[...], v_ref[...])

    spec = pl.BlockSpec((tr, cols), lambda i: (i, 0))
    return pl.pallas_call(
        body, out_shape=(jax.ShapeDtypeStruct(w.shape, F32),) * 3, grid=(rows // tr,),
        in_specs=[spec] * 4, out_specs=(spec,) * 3, compiler_params=_cp("parallel"), name=name)(w, g, m, v)


def pair_sum(dw, got, core, name):
    _, rows, cols = dw.shape
    half = rows // 2
    tr = _stream_rows(half, cols, 2)
    nrb = half // tr

    def body(c_ref, a_ref, b_ref, o_ref):
        o_ref[...] = (a_ref[...].astype(F32) + b_ref[...].astype(F32)).astype(o_ref.dtype)

    return pl.pallas_call(
        body, out_shape=jax.ShapeDtypeStruct((N_CHIPS, half, cols), BF16),
        grid_spec=pltpu.PrefetchScalarGridSpec(
            num_scalar_prefetch=1, grid=(N_CHIPS, nrb),
            in_specs=[pl.BlockSpec((None, tr, cols), lambda s, i, c_ref: (s, c_ref[0] * nrb + i, 0)),
                      pl.BlockSpec((None, tr, cols), lambda s, i, c_ref: (s, i, 0))],
            out_specs=pl.BlockSpec((None, tr, cols), lambda s, i, c_ref: (s, i, 0))),
        compiler_params=_cp("parallel", "parallel"), name=name)(core, dw, got)


def chip_sum(own, parts, place, name):
    _, half, cols = parts.shape
    tr = _stream_rows(half, cols, 4)
    nrb = half // tr

    def body(p_ref, own_ref, a_ref, b_ref, c_ref, o_ref):
        o_ref[...] = ((own_ref[...].astype(F32) + a_ref[...].astype(F32)) + b_ref[...].astype(F32)) + c_ref[...].astype(F32)

    def slot(k):
        return pl.BlockSpec((None, tr, cols), lambda i, p: (jnp.bitwise_xor(p[0], k), i, 0))

    return pl.pallas_call(
        body, out_shape=jax.ShapeDtypeStruct((2 * half, cols), F32),
        grid_spec=pltpu.PrefetchScalarGridSpec(
            num_scalar_prefetch=1, grid=(nrb,), in_specs=[slot(0), slot(1), slot(2), slot(3)],
            out_specs=pl.BlockSpec((tr, cols), lambda i, p: (p[1] * nrb + i, 0))),
        compiler_params=_cp("parallel"), name=name)(place, own, parts, parts, parts)


def cast_to_slot(w_all, layer, place, name, after=None):
    _, rows, cols = w_all.shape
    tr = _stream_rows(rows, cols, 4)
    tied = [] if after is None else [after]

    def body(p_ref, w_ref, *rest):
        o_ref = rest[-1]
        o_ref[...] = w_ref[...].astype(o_ref.dtype)

    return pl.pallas_call(
        body, out_shape=jax.ShapeDtypeStruct((N_CHIPS, rows, cols), BF16),
        grid_spec=pltpu.PrefetchScalarGridSpec(
            num_scalar_prefetch=1, grid=(rows // tr,),
            in_specs=[pl.BlockSpec((None, tr, cols), lambda i, p: (layer, i, 0))] + [ANY] * len(tied),
            out_specs=pl.BlockSpec((None, tr, cols), lambda i, p: (p[0], i, 0))),
        compiler_params=_cp("parallel"), name=name)(place, w_all, *tied)


ANY = pl.BlockSpec(memory_space=pl.ANY)


def _place():
    x, y, c = lax.axis_index("x"), lax.axis_index("y"), lax.axis_index("c")
    others = [(1 - x, y), (x, 1 - y), (1 - x, 1 - y)]
    return x, y, c, 2 * x + y, others


def _remote(src, dst, send_sem, recv_sem, dev):
    return pltpu.make_async_remote_copy(src_ref=src, dst_ref=dst, send_sem=send_sem, recv_sem=recv_sem,
                                        device_id=dev, device_id_type=MESH)


HBM = pl.BlockSpec(memory_space=pltpu.HBM)
SEM = pl.BlockSpec(memory_space=pltpu.SEMAPHORE)
EFFECT = pltpu.SideEffectType.DATAFLOW_SIDE_EFFECTING
TOKEN = jax.ShapeDtypeStruct((8, LANES), F32)


def _in_hbm(a):
    return pltpu.with_memory_space_constraint(a, pltpu.HBM)


def _gather_copies(bufs, send_sems, recv_sems):
    x, y, c, me, others = _place()
    out = []
    for w, buf in enumerate(bufs):
        half = buf.shape[1] // 2
        mine = pl.ds(c * half, half)
        for k, (ox, oy) in enumerate(others):
            sems = send_sems.at[3 * w + k], recv_sems.at[3 * w + k]
            out.append((_remote(buf.at[me, mine], buf.at[me, mine], *sems, (ox, oy, c)),
                        _remote(buf.at[me, mine], buf.at[2 * ox + oy, mine], *sems, (ox, oy, c))))
    return out


def _forward_copies(bufs, send_sems, recv_sems):
    x, y, c, me, others = _place()
    out = []
    for w, buf in enumerate(bufs):
        half = buf.shape[1] // 2
        mine, theirs = pl.ds(c * half, half), pl.ds((1 - c) * half, half)
        for k, (ox, oy) in enumerate(others):
            sems = send_sems.at[3 * w + k], recv_sems.at[3 * w + k]
            slot = 2 * ox + oy
            out.append((_remote(buf.at[slot, mine], buf.at[slot, mine], *sems, (x, y, 1 - c)),
                        _remote(buf.at[slot, mine], buf.at[slot, theirs], *sems, (x, y, 1 - c))))
    return out


def _join_copies(grads, send_sems, recv_sems):
    x, y, c, _, _ = _place()
    out = []
    for w, g in enumerate(grads):
        half = g.shape[0] // 2
        mine, theirs = pl.ds(c * half, half), pl.ds((1 - c) * half, half)
        sems = send_sems.at[w], recv_sems.at[w]
        out.append((_remote(g.at[mine], g.at[mine], *sems, (x, y, 1 - c)), _remote(g.at[mine], g.at[theirs], *sems, (x, y, 1 - c))))
    return out


IN_PLACE = dict(gather=(_gather_copies, 3), forward=(_forward_copies, 3), join=(_join_copies, 1))


def copies_start(kind, bufs, name):
    n = len(bufs)
    copies, per_buf = IN_PLACE[kind]

    def body(*refs):
        ins, (send_sems, recv_sems), token = refs[:n], refs[n:n + 2], refs[-1]
        for sent, _ in copies(ins, send_sems, recv_sems):
            sent.start()
        token[...] = jnp.zeros_like(token)

    outs = pl.pallas_call(
        body, name=name,
        out_shape=(pltpu.SemaphoreType.DMA((per_buf * n,)), pltpu.SemaphoreType.DMA((per_buf * n,)),
                   *[pltpu.HBM(b.shape, b.dtype) for b in bufs], TOKEN),
        in_specs=[HBM] * n, out_specs=(SEM, SEM, *[HBM] * n, VM),
        input_output_aliases={w: 2 + w for w in range(n)},
        compiler_params=pltpu.CompilerParams(has_side_effects=EFFECT))(*[_in_hbm(b) for b in bufs])
    return outs[0], outs[1], list(outs[2:2 + n]), outs[-1]


def copies_wait(kind, send_sems, recv_sems, bufs, after, name):
    n = len(bufs)
    copies, _ = IN_PLACE[kind]

    def body(*refs):
        ins, (send_ref, recv_ref) = refs[:n], refs[n:n + 2]
        for sent, landed in copies(ins, send_ref, recv_ref):
            sent.wait_send()
            landed.wait_recv()

    return list(pl.pallas_call(
        body, name=name, out_shape=tuple(pltpu.HBM(b.shape, b.dtype) for b in bufs),
        in_specs=[HBM] * n + [SEM, SEM, ANY], out_specs=(HBM,) * n,
        input_output_aliases={w: w for w in range(n)},
        compiler_params=pltpu.CompilerParams(has_side_effects=EFFECT))(*bufs, send_sems, recv_sems, after))


def swap_halves(grads, name):
    n = len(grads)

    def body(*refs):
        ins, outs = refs[:n], refs[n:2 * n]
        send_sems, recv_sems = refs[2 * n:]
        x, y, c, _, _ = _place()
        copies = []
        for w in range(n):
            half = ins[w].shape[1] // 2
            cp = _remote(ins[w].at[:, pl.ds((1 - c) * half, half)], outs[w], send_sems.at[w], recv_sems.at[w], (x, y, 1 - c))
            cp.start()
            copies.append(cp)
        for cp in copies:
            cp.wait()

    return pl.pallas_call(
        body, out_shape=tuple(jax.ShapeDtypeStruct((N_CHIPS, g.shape[1] // 2, g.shape[2]), g.dtype) for g in grads),
        in_specs=[ANY] * n, out_specs=(ANY,) * n,
        scratch_shapes=[pltpu.SemaphoreType.DMA((n,)), pltpu.SemaphoreType.DMA((n,))], name=name)(*grads)


def _swap_copies(grads, lands, send_sems, recv_sems):
    x, y, c, _, _ = _place()
    out = []
    for w, (g, land) in enumerate(zip(grads, lands)):
        half = g.shape[1] // 2
        out.append(_remote(g.at[:, pl.ds((1 - c) * half, half)], land, send_sems.at[w], recv_sems.at[w], (x, y, 1 - c)))
    return out


def swap_start(grads, name):
    n = len(grads)

    def body(*refs):
        ins, lands, (send_sems, recv_sems), token = refs[:n], refs[n:2 * n], refs[2 * n:2 * n + 2], refs[-1]
        for cp in _swap_copies(ins, lands, send_sems, recv_sems):
            cp.start()
        token[...] = jnp.zeros_like(token)

    shapes = [(N_CHIPS, g.shape[1] // 2, g.shape[2]) for g in grads]
    zones = [_in_hbm(lax.empty(s, g.dtype)) for s, g in zip(shapes, grads)]
    outs = pl.pallas_call(
        body, name=name,
        out_shape=(pltpu.SemaphoreType.DMA((n,)), pltpu.SemaphoreType.DMA((n,)),
                   *[pltpu.HBM(g.shape, g.dtype) for g in grads], *[pltpu.HBM(s, g.dtype) for s, g in zip(shapes, grads)],
                   TOKEN),
        in_specs=[HBM] * (2 * n), out_specs=(SEM, SEM, *[HBM] * (2 * n), VM),
        input_output_aliases={w: 2 + w for w in range(2 * n)},
        compiler_params=pltpu.CompilerParams(has_side_effects=EFFECT))(*[_in_hbm(g) for g in grads], *zones)
    return outs[0], outs[1], list(outs[2:2 + n]), list(outs[2 + n:2 + 2 * n]), outs[-1]


def swap_wait(send_sems, recv_sems, grads, lands, after, name):
    n = len(grads)

    def body(*refs):
        ins, zones, (send_ref, recv_ref) = refs[:n], refs[n:2 * n], refs[2 * n:2 * n + 2]
        for cp in _swap_copies(ins, zones, send_ref, recv_ref):
            cp.wait_send()
            cp.wait_recv()

    outs = pl.pallas_call(
        body, name=name, out_shape=tuple(pltpu.HBM(a.shape, a.dtype) for a in list(grads) + list(lands)),
        in_specs=[HBM] * (2 * n) + [SEM, SEM, ANY], out_specs=(HBM,) * (2 * n),
        input_output_aliases={w: w for w in range(2 * n)},
        compiler_params=pltpu.CompilerParams(has_side_effects=EFFECT))(*grads, *lands, send_sems, recv_sems, after)
    return list(outs[:n]), list(outs[n:])


def _exchange_copies(sums, lands, send_sems, recv_sems):
    x, y, c, me, others = _place()
    out = []
    for w, (src, land) in enumerate(zip(sums, lands)):
        for k, (ox, oy) in enumerate(others):
            sems = send_sems.at[3 * w + k], recv_sems.at[3 * w + k]
            out.append((_remote(src.at[2 * ox + oy], land.at[me], *sems, (ox, oy, c)),
                        _remote(src.at[2 * ox + oy], land.at[2 * ox + oy], *sems, (ox, oy, c))))
    return out


def exchange_start(sums, name):
    n = len(sums)

    def body(*refs):
        ins, lands, (send_sems, recv_sems), token = refs[:n], refs[n:2 * n], refs[2 * n:2 * n + 2], refs[-1]
        for sent, _ in _exchange_copies(ins, lands, send_sems, recv_sems):
            sent.start()
        token[...] = jnp.zeros_like(token)

    zones = [_in_hbm(lax.empty(s.shape, s.dtype)) for s in sums]
    outs = pl.pallas_call(
        body, name=name,
        out_shape=(pltpu.SemaphoreType.DMA((3 * n,)), pltpu.SemaphoreType.DMA((3 * n,)),
                   *[pltpu.HBM(s.shape, s.dtype) for s in sums] * 2, TOKEN),
        in_specs=[HBM] * (2 * n), out_specs=(SEM, SEM, *[HBM] * (2 * n), VM),
        input_output_aliases={w: 2 + w for w in range(2 * n)},
        compiler_params=pltpu.CompilerParams(has_side_effects=EFFECT))(*[_in_hbm(s) for s in sums], *zones)
    return outs[0], outs[1], list(outs[2:2 + n]), list(outs[2 + n:2 + 2 * n]), outs[-1]


def exchange_wait(send_sems, recv_sems, sums, lands, after, name):
    n = len(sums)

    def body(*refs):
        ins, zones, (send_ref, recv_ref) = refs[:n], refs[n:2 * n], refs[2 * n:2 * n + 2]
        for sent, landed in _exchange_copies(ins, zones, send_ref, recv_ref):
            sent.wait_send()
            landed.wait_recv()

    outs = pl.pallas_call(
        body, name=name, out_shape=tuple(pltpu.HBM(s.shape, s.dtype) for s in sums) * 2,
        in_specs=[HBM] * (2 * n) + [SEM, SEM, ANY], out_specs=(HBM,) * (2 * n),
        input_output_aliases={w: w for w in range(2 * n)},
        compiler_params=pltpu.CompilerParams(has_side_effects=EFFECT))(*sums, *lands, send_sems, recv_sems, after)
    return list(outs[:n]), list(outs[n:])


VM = pl.BlockSpec(memory_space=pltpu.VMEM)


def small_allgather(buf, name="small_allgather"):
    def body(in_ref, out_ref, send_sems, recv_sems):
        x, y, c, me, others = _place()
        out_ref[me] = in_ref[...]
        copies = []
        for k, (ox, oy) in enumerate(others):
            cp = _remote(in_ref, out_ref.at[me], send_sems.at[k], recv_sems.at[k], (ox, oy, c))
            cp.start()
            copies.append(cp)
        for k, (ox, oy) in enumerate(others):
            landed = out_ref.at[2 * ox + oy]
            _remote(landed, landed, send_sems.at[k], recv_sems.at[k], (ox, oy, c)).wait_recv()
        for cp in copies:
            cp.wait_send()

    return pl.pallas_call(
        body, out_shape=jax.ShapeDtypeStruct((N_CHIPS,) + buf.shape, buf.dtype), in_specs=[VM], out_specs=VM,
        scratch_shapes=[pltpu.SemaphoreType.DMA((3,)), pltpu.SemaphoreType.DMA((3,))],
        compiler_params=pltpu.CompilerParams(vmem_limit_bytes=V7X_VMEM_LIMIT), name=name)(buf)


def small_allreduce(buf, after, name="small_allreduce"):
    def body(in_ref, after_ref, out_ref, sib_ref, slot_ref, send_sems, recv_sems):
        x, y, c, me, others = _place()
        cp = _remote(in_ref, sib_ref, send_sems.at[3], recv_sems.at[3], (x, y, 1 - c))
        cp.start()
        cp.wait()
        slot_ref[me] = in_ref[...] + sib_ref[...]
        copies = []
        for k, (ox, oy) in enumerate(others):
            cp = _remote(slot_ref.at[me], slot_ref.at[me], send_sems.at[k], recv_sems.at[k], (ox, oy, c))
            cp.start()
            copies.append(cp)
        for k, (ox, oy) in enumerate(others):
            landed = slot_ref.at[2 * ox + oy]
            _remote(landed, landed, send_sems.at[k], recv_sems.at[k], (ox, oy, c)).wait_recv()
        for cp in copies:
            cp.wait_send()
        out_ref[...] = ((slot_ref[0] + slot_ref[1]) + slot_ref[2]) + slot_ref[3]

    return pl.pallas_call(
        body, out_shape=jax.ShapeDtypeStruct(buf.shape, buf.dtype), in_specs=[VM, ANY], out_specs=VM,
        scratch_shapes=[pltpu.VMEM(buf.shape, buf.dtype), pltpu.VMEM((N_CHIPS,) + buf.shape, buf.dtype),
                        pltpu.SemaphoreType.DMA((4,)), pltpu.SemaphoreType.DMA((4,))],
        compiler_params=pltpu.CompilerParams(vmem_limit_bytes=V7X_VMEM_LIMIT), name=name)(buf, after)


def _pack_rows(arrays, row_multiple):
    flat = jnp.concatenate([a.reshape(-1) for a in arrays])
    rows = -(-flat.shape[0] // (LANES * row_multiple)) * row_multiple
    return jnp.pad(flat, (0, rows * LANES - flat.shape[0])).reshape(rows, LANES)


def _unpack_rows(buf, shapes):
    flat = buf.reshape(-1)
    out, at = [], 0
    for s in shapes:
        n = math.prod(s)
        out.append(flat[at:at + n].reshape(s))
        at += n
    return out


def _mixer_weights(i):
    j = i // 2
    mixer = "a" if i % 2 == 0 else "b"
    return [("w_mem_kv", i), (mixer + "_w_in", j), (mixer + "_w_out", j)]


def _ffn_weights(i):
    return [("ffn_w_up", i), ("ffn_w_down", i)]


def _mixer_fwd(i, x, mem, w, small, after):
    is_a = i % 2 == 0
    j = i // 2
    wkv, win, wout = w
    wkv = wkv.reshape(1, D_MODEL, 2 * MEM_W)
    h1 = rms_fwd(x, small["mix_norm_g"][i], BF16, name=f"mix_norm{i}", after=after)
    mem_n = rms_fwd(mem, small["mem_norm_g"][i], BF16, name=f"mem_norm{i}")
    kv = mm_nn(mem_n, wkv, F32, name=f"mem_kv{i}")
    proj = mm_nn(h1, win, F32, name=f"in_proj{i}")
    saved = dict(x0=x, h1=h1, mem_n=mem_n, kv=kv, proj=proj)
    if is_a:
        outs, lses = zip(*[attn_fwd(proj, g, name=f"attn_fwd{i}_{g}") for g in range(3)])
        comb, lse = attn_combine(outs, lses, name=f"attn_combine{i}")
        mem_out = mem_fwd(proj, 3 * A_QKV_W, kv, name=f"mem_fwd{i}")
        cat = jnp.concatenate([comb.astype(BF16), mem_out], axis=1)
        saved.update(comb=comb, lse=lse)
    else:
        wout = wout.reshape(1, B_W + MEM_W, D_MODEL)
        tok = sgu_fwd(proj, small["b_v_norm_g"][j], small["b_w_s"][j], small["bias_b"][j], name=f"sgu_fwd{i}")
        mem_out = mem_fwd(proj, 2 * B_W, kv, name=f"mem_fwd{i}")
        cat = jnp.concatenate([tok, mem_out], axis=1)
    x1 = mm_nn(cat, wout, F32, res=x, name=f"out_proj{i}")
    saved.update(cat=cat)
    return x1, saved


def _ffn_fwd(i, x1, w, small, after):
    wup, wdn = w
    h2 = rms_fwd(x1, small["ffn_norm_g"][i], BF16, name=f"ffn_norm{i}", after=after)
    a = mm_nn(h2, wup, F32, name=f"ffn_up{i}")
    act = ffn_act_fwd(a, small["ffn_conv_w"][i], small["ffn_conv_b"][i], name=f"ffn_act{i}")
    x2 = mm_nn(act, wdn.reshape(1, FF, D_MODEL), F32, res=x1, name=f"ffn_down{i}")
    return x2, dict(x1=x1, h2=h2, a=a, act=act)


def _ffn_bwd(i, dx2, w, small, sv, after):
    wup, wdn = w
    sg = {}
    dact = mm_nt(dx2, wdn.reshape(1, FF, D_MODEL), F32, name=f"d_act{i}", after=after)
    d_wdn = mm_tn(sv["act"], dx2, 1, BF16, name=f"d_wdown{i}").reshape(N_CHIPS, FF // N_CHIPS, D_MODEL)
    da, sg["ffn_conv_w"], sg["ffn_conv_b"] = ffn_act_bwd(sv["a"], small["ffn_conv_w"][i], small["ffn_conv_b"][i], dact,
                                                          name=f"ffn_act_bwd{i}")
    d_wup = mm_tn(sv["h2"], da, N_CHIPS, BF16, name=f"d_wup{i}")
    dh2 = mm_nt(da, wup, F32, name=f"d_h2_{i}")
    dx1, sg["ffn_norm_g"] = rms_bwd(sv["x1"], small["ffn_norm_g"][i], dh2, dres=dx2, name=f"ffn_norm_bwd{i}")
    return dx1, [d_wup, d_wdn], sg


def _mixer_bwd(i, dx1, mem, w, small, sv, after):
    is_a = i % 2 == 0
    j = i // 2
    wkv, win, wout = w
    wkv = wkv.reshape(1, D_MODEL, 2 * MEM_W)
    sg = {}
    proj, kv = sv["proj"], sv["kv"]
    if is_a:
        dcat = mm_nt(dx1, wout, F32, name=f"d_cat{i}", after=after)
        d_wout = mm_tn(sv["cat"], dx1, N_CHIPS, BF16, name=f"d_wout{i}")
        dqm, dkv = mem_bwd(proj, 3 * A_QKV_W, kv, dcat, A_OUT_W, name=f"mem_bwd{i}")
        parts = [attn_bwd(proj, dcat, sv["comb"], sv["lse"], g, name=f"attn_bwd{i}_{g}") for g in range(3)]
        dproj = jnp.concatenate([parts[g][p] for p in range(3) for g in range(3)] + [dqm], axis=1)
    else:
        dcat = mm_nt(dx1, wout.reshape(1, B_W + MEM_W, D_MODEL), F32, name=f"d_cat{i}", after=after)
        d_wout = mm_tn(sv["cat"], dx1, 1, BF16, name=f"d_wout{i}").reshape(N_CHIPS, (B_W + MEM_W) // N_CHIPS, D_MODEL)
        dqm, dkv = mem_bwd(proj, 2 * B_W, kv, dcat, B_W, name=f"mem_bwd{i}")
        w_s = small["b_w_s"][j]
        duv, sg["b_w_s"], dmix, sg["b_v_norm_g"] = sgu_bwd(proj, small["b_v_norm_g"][j], w_s, jnp.swapaxes(w_s, 1, 2),
                                                           small["bias_b"][j], dcat, name=f"sgu_bwd{i}")
        sg["b_s_bias"] = jnp.sum(dmix, axis=-1)
        dproj = jnp.concatenate([duv, dqm], axis=1)
    d_wkv = mm_tn(sv["mem_n"], dkv, 1, BF16, name=f"d_wkv{i}").reshape(N_CHIPS, D_MODEL // N_CHIPS, 2 * MEM_W)
    dmem_n = mm_nt(dkv, wkv, F32, name=f"d_mem_n{i}")
    _, sg["mem_norm_g"] = rms_bwd(mem, small["mem_norm_g"][i], dmem_n, name=f"mem_norm_bwd{i}")
    d_win = mm_tn(sv["h1"], dproj, N_CHIPS, BF16, name=f"d_win{i}")
    dh1 = mm_nt(dproj, win, F32, name=f"d_h1_{i}")
    dx0, sg["mix_norm_g"] = rms_bwd(sv["x0"], small["mix_norm_g"][i], dh1, dres=dx1, name=f"mix_norm_bwd{i}")
    return dx0, [d_wkv, d_win, d_wout], sg


def _exchange_begin(grads, got, place, tag):
    sums = [pair_sum(g, o, place[1:], name=f"pair_sum{tag}_{k}") for k, (g, o) in enumerate(zip(grads, got))]
    send_sems, recv_sems, sums, lands, token = exchange_start(sums, name=f"exchange_start{tag}")
    return (send_sems, recv_sems, sums, lands), token


def _reduce_finish(started, place, after, tag):
    sums, parts = exchange_wait(*started, after, name=f"exchange_wait{tag}")
    halves = [chip_sum(s, p, place, name=f"chip_sum{tag}_{k}") for k, (s, p) in enumerate(zip(sums, parts))]
    return copies_start("join", halves, name=f"join_start_{tag}")


SMALL_SHARDED = ("b_v_norm_g", "ffn_conv_w")
SMALL_FULL_SHAPES = dict(mix_norm_g=(D_MODEL,), ffn_norm_g=(D_MODEL,), mem_norm_g=(D_MODEL,), b_v_norm_g=(B_W,),
                         b_w_s=(B_GROUPS, CHUNK, CHUNK), b_s_bias=(B_GROUPS, CHUNK), ffn_conv_w=(3, 2 * FF),
                         ffn_conv_b=(2 * FF,))
BIG = ("w_mem_kv", "a_w_in", "a_w_out", "b_w_in", "b_w_out", "ffn_w_up", "ffn_w_down")
WEIGHT_ORDER = ("mix_norm_g", "ffn_norm_g", "mem_norm_g", "w_mem_kv", "a_w_in", "a_w_out", "b_w_in", "b_v_norm_g", "b_w_s",
                "b_s_bias", "b_w_out", "ffn_w_up", "ffn_conv_w", "ffn_conv_b", "ffn_w_down", "final_norm_g")


def kernel(x, mem, mix_norm_g, ffn_norm_g, mem_norm_g, w_mem_kv, a_w_in, a_w_out, b_w_in, b_v_norm_g, b_w_s, b_s_bias, b_w_out, ffn_w_up, ffn_conv_w, ffn_conv_b, ffn_w_down, final_norm_g, loss_target, m_mix_norm_g, m_ffn_norm_g, m_mem_norm_g, m_w_mem_kv, m_a_w_in, m_a_w_out, m_b_w_in, m_b_v_norm_g, m_b_w_s, m_b_s_bias, m_b_w_out, m_ffn_w_up, m_ffn_conv_w, m_ffn_conv_b, m_ffn_w_down, m_final_norm_g, v_mix_norm_g, v_ffn_norm_g, v_mem_norm_g, v_w_mem_kv, v_a_w_in, v_a_w_out, v_b_w_in, v_b_v_norm_g, v_b_w_s, v_b_s_bias, v_b_w_out, v_ffn_w_up, v_ffn_conv_w, v_ffn_conv_b, v_ffn_w_down, v_final_norm_g):
    weights = dict(mix_norm_g=mix_norm_g, ffn_norm_g=ffn_norm_g, mem_norm_g=mem_norm_g, w_mem_kv=w_mem_kv, a_w_in=a_w_in,
                   a_w_out=a_w_out, b_w_in=b_w_in, b_v_norm_g=b_v_norm_g, b_w_s=b_w_s, b_s_bias=b_s_bias, b_w_out=b_w_out,
                   ffn_w_up=ffn_w_up, ffn_conv_w=ffn_conv_w, ffn_conv_b=ffn_conv_b, ffn_w_down=ffn_w_down,
                   final_norm_g=final_norm_g)
    mom1 = dict(mix_norm_g=m_mix_norm_g, ffn_norm_g=m_ffn_norm_g, mem_norm_g=m_mem_norm_g, w_mem_kv=m_w_mem_kv,
                a_w_in=m_a_w_in, a_w_out=m_a_w_out, b_w_in=m_b_w_in, b_v_norm_g=m_b_v_norm_g, b_w_s=m_b_w_s,
                b_s_bias=m_b_s_bias, b_w_out=m_b_w_out, ffn_w_up=m_ffn_w_up, ffn_conv_w=m_ffn_conv_w,
                ffn_conv_b=m_ffn_conv_b, ffn_w_down=m_ffn_w_down, final_norm_g=m_final_norm_g)
    mom2 = dict(mix_norm_g=v_mix_norm_g, ffn_norm_g=v_ffn_norm_g, mem_norm_g=v_mem_norm_g, w_mem_kv=v_w_mem_kv,
                a_w_in=v_a_w_in, a_w_out=v_a_w_out, b_w_in=v_b_w_in, b_v_norm_g=v_b_v_norm_g, b_w_s=v_b_w_s,
                b_s_bias=v_b_s_bias, b_w_out=v_b_w_out, ffn_w_up=v_ffn_w_up, ffn_conv_w=v_ffn_conv_w,
                ffn_conv_b=v_ffn_conv_b, ffn_w_down=v_ffn_w_down, final_norm_g=v_final_norm_g)
    chip = 2 * lax.axis_index("x") + lax.axis_index("y")
    place = jnp.stack([chip, lax.axis_index("c")]).astype(jnp.int32)
    x0, mem0, target = x[0], mem[0], loss_target[0]
    depth = DEPTH

    n_cw, n_vg = ffn_conv_w.size, b_v_norm_g.size
    gathered = small_allgather(_pack_rows([ffn_conv_w, b_v_norm_g], 8)).reshape(N_CHIPS, -1)
    conv_w_full = gathered[:, :n_cw].reshape(N_CHIPS, DEPTH, 3, 2 * FF // N_CHIPS).transpose(1, 2, 0, 3).reshape(DEPTH, 3, 2 * FF)
    vgain_full = gathered[:, n_cw:n_cw + n_vg].reshape(N_CHIPS, 2, B_W // N_CHIPS).transpose(1, 0, 2).reshape(2, B_W)
    small = dict(mix_norm_g=mix_norm_g, ffn_norm_g=ffn_norm_g, mem_norm_g=mem_norm_g, b_w_s=b_w_s, ffn_conv_b=ffn_conv_b,
                 ffn_conv_w=conv_w_full, b_v_norm_g=vgain_full,
                 bias_b=jnp.broadcast_to(b_s_bias[..., None], b_s_bias.shape + (CHUNK,)))

    half_layers = 2 * depth
    groups = [(_ffn_weights if b % 2 else _mixer_weights)(b // 2) for b in range(half_layers)]
    tags = [("f" if b % 2 else "m") + str(b // 2) for b in range(half_layers)]

    def start_gather(b, after):
        bufs = [cast_to_slot(weights[n], l, place, name=f"cast_{n}{l}", after=after) for n, l in groups[b]]
        return copies_start("gather", bufs, name=f"gather_start_{tags[b]}")

    def landed(b, after):
        send_sems, recv_sems, bufs, _ = over_ici.pop(b)
        bufs = copies_wait("gather", send_sems, recv_sems, bufs, after, name=f"gather_wait_{tags[b]}")
        return copies_start("forward", bufs, name=f"forward_start_{tags[b]}")

    ahead_from = 4
    over_ici, tie = {}, gathered
    for b in range(3):
        over_ici[b] = start_gather(b, tie)
        tie = over_ici[b][3]
    w_half, saved_half, h = [], [], x0
    for b in range(half_layers):
        behind = tie if b == 0 else h
        if b < ahead_from:
            to_sibling = landed(b, behind)
            behind = to_sibling[3]
        w_half.append(copies_wait("forward", *to_sibling[:3], behind, name=f"forward_wait_{tags[b]}"))
        tokens = []
        if ahead_from <= b + 1 < half_layers:
            to_sibling = landed(b + 1, w_half[b][0])
            tokens.append(to_sibling[3])
        if b + 3 < half_layers:
            over_ici[b + 3] = start_gather(b + 3, w_half[b][0])
            tokens.append(over_ici[b + 3][3])
        tie = sum(tokens[1:], tokens[0]) if tokens else None
        if b % 2 == 0:
            h, sv = _mixer_fwd(b // 2, h, mem0, w_half[b], small, tie)
        else:
            h, sv = _ffn_fwd(b // 2, h, w_half[b], small, tie)
        saved_half.append(sv)
    w_mix, w_ffn, saved_mix, saved_ffn = w_half[0::2], w_half[1::2], saved_half[0::2], saved_half[1::2]
    loss_row, dh, d_final = final_loss(h, final_norm_g, target)
    loss = lax.psum(loss_row[0, 0], ("x", "y", "c"))

    names = [n for n in WEIGHT_ORDER if n not in BIG]
    small_g = {n: [None] * weights[n].shape[0] for n in names if n != "final_norm_g"}
    big_out = {n: None for n in BIG}

    def keep_small(i, sg):
        for n, g in sg.items():
            small_g[n][i if len(small_g[n]) == depth else i // 2] = g.reshape(SMALL_FULL_SHAPES[n])

    joining = []

    def update(after):
        (send_sems, recv_sems, halves, _), group, tag = joining.pop()
        for (n, l), g in zip(group, copies_wait("join", send_sems, recv_sems, halves, after, name=f"join_wait_{tag}")):
            big_out[n] = adamw_layer(weights[n], mom1[n], mom2[n], l, g, big_out[n], name=f"adamw_{n}{l}")

    def finish_reduce(started, group, after, tag):
        join = _reduce_finish(started, place, after, tag)
        if joining:
            update(join[3])
        joining.append((join, group, tag))

    half_layers = 2 * depth
    swapping, exchanging, tie = None, [], None
    for k in range(half_layers):
        i = depth - 1 - k // 2
        if k % 2 == 0:
            dh, big_g, sg = _ffn_bwd(i, dh, w_ffn[i], small, saved_ffn[i], tie)
            group, tag = _ffn_weights(i), f"f{i}"
        else:
            dh, big_g, sg = _mixer_bwd(i, dh, mem0, w_mix[i], small, saved_mix[i], tie)
            group, tag = _mixer_weights(i), f"m{i}"
        keep_small(i, sg)
        started_now, swap_now, tokens = [], None, []
        if k < half_layers - 2:
            *swap_now, token = swap_start(big_g, name=f"swap_start_{tag}")
            swap_now = (swap_now, group, tag)
        else:
            started, token = _exchange_begin(big_g, swap_halves(big_g, name=f"swap_halves_{tag}"), place, tag)
            started_now.append((started, group, tag))
        tokens.append(token)
        if swapping is not None:
            swap_args, old_group, old_tag = swapping
            grads, got = swap_wait(*swap_args, dh, name=f"swap_wait_{old_tag}")
            started, token = _exchange_begin(grads, got, place, old_tag)
            started_now.append((started, old_group, old_tag))
            tokens.append(token)
        tie = sum(tokens[1:], tokens[0])
        for started, old_group, old_tag in exchanging:
            finish_reduce(started, old_group, dh, old_tag)
        swapping, exchanging = swap_now, started_now
    for started, old_group, old_tag in exchanging:
        finish_reduce(started, old_group, big_out["ffn_w_down"][0], old_tag)

    full_g = {n: (d_final.reshape(-1) if n == "final_norm_g" else jnp.stack(small_g[n])) for n in names}
    shapes = [full_g[n].shape for n in names]
    reduced = small_allreduce(_pack_rows([full_g[n] for n in names], 8), joining[0][0][3])
    update(reduced)
    summed = dict(zip(names, _unpack_rows(reduced, shapes)))
    for n in SMALL_SHARDED:
        width = weights[n].shape[-1]
        summed[n] = lax.dynamic_slice_in_dim(summed[n], chip * width, width, axis=summed[n].ndim - 1)
    own_shapes = [weights[n].shape for n in names]
    pack = lambda d: _pack_rows([d[n] for n in names], 128)
    small_out = [_unpack_rows(b, own_shapes) for b in adamw_flat(pack(weights), pack(summed), pack(mom1), pack(mom2))]
    outs = {}
    for k, n in enumerate(names):
        outs[n] = (summed[n], small_out[0][k], small_out[1][k], small_out[2][k])
    outs.update(big_out)
    return (loss, dh[None], *[outs[n][0] for n in WEIGHT_ORDER], *[outs[n][1] for n in WEIGHT_ORDER],
            *[outs[n][2] for n in WEIGHT_ORDER], *[outs[n][3] for n in WEIGHT_ORDER])
```

```python
import functools
import math

import numpy as np
import jax
import jax.numpy as jnp
from jax import lax
from jax.experimental import pallas as pl
from jax.experimental.pallas import tpu as pltpu

F32 = jnp.float32
BF16 = jnp.bfloat16
MESH = pl.DeviceIdType.MESH

D_MODEL = 2048
SEQ = 2048
DEPTH = 4
EPS = 1e-6
NEG = -1e30
HEAD = 128
A_PATTERNS = ((128, 1), (512, 4), (2048, 16))
A_QKV_W = 1536
A_OUT_W = 512
A_IN = 5120
QBLK = 128
N_SIDE = 64
CHUNK = 128
B_GROUPS = 12
B_W = 1536
B_IN = 3584
MEM_LEN = 256
MEM_HEADS = 4
MEM_W = 512
FF = 5632
ADAM_LR, ADAM_B1, ADAM_B2, ADAM_EPS, ADAM_WD, ADAM_STEP = 0.001, 0.9, 0.999, 1e-08, 0.01, 10
N_CHIPS = 4

LANES = 128
V7X_VMEM_LIMIT = 56 * 1024 * 1024


def _cp(*sem):
    return pltpu.CompilerParams(dimension_semantics=sem, vmem_limit_bytes=V7X_VMEM_LIMIT)


def _pick(dim, prefs):
    for p in prefs:
        if dim % p == 0:
            return p
    raise ValueError(f"no tile for {dim} in {prefs}")


def _gelu_parts(x):
    cdf = 0.5 * (1.0 + lax.erf(x * (1.0 / math.sqrt(2.0))))
    pdf = jnp.exp(-0.5 * x * x) * (1.0 / math.sqrt(2.0 * math.pi))
    return x * cdf, cdf + x * pdf


def _gelu(x):
    return 0.5 * x * (1.0 + lax.erf(x * (1.0 / math.sqrt(2.0))))


TM_PREFS = (1024, 512, 256, 128)
TN_PREFS = (1408, 1280, 1024, 896, 512, 256, 128)
TK_PREFS = (2816, 2048, 1408, 1280, 1024, 896, 512, 256, 128)


def _mm_body(nk, dims, has_res):
    def body(*refs):
        if has_res:
            a_ref, b_ref, r_ref, o_ref = refs[:4]
        else:
            a_ref, b_ref, o_ref = refs[:3]
            r_ref = None
        part = lax.dot_general(a_ref[...].astype(BF16), b_ref[...].astype(BF16), dims,
                               preferred_element_type=F32)
        if nk == 1:
            if has_res:
                part = part + r_ref[...]
            o_ref[...] = part.astype(o_ref.dtype)
            return
        acc_ref = refs[-1]
        k = pl.program_id(2)

        @pl.when(k == 0)
        def _():
            acc_ref[...] = part

        @pl.when(k > 0)
        def _():
            acc_ref[...] += part

        @pl.when(k == nk - 1)
        def _():
            tot = acc_ref[...]
            if has_res:
                tot = tot + r_ref[...]
            o_ref[...] = tot.astype(o_ref.dtype)
    return body


def mm_nn(a, w, out_dtype, res=None, name="mm_nn"):
    m, kw = a.shape
    ns_, kw2, nsz = w.shape
    assert kw == kw2
    n = ns_ * nsz
    tm, tn, tk = _pick(m, TM_PREFS), _pick(nsz, TN_PREFS), _pick(kw, TK_PREFS)
    nb, nk = nsz // tn, kw // tk
    in_specs = [pl.BlockSpec((tm, tk), lambda i, j, k: (i, k)),
                pl.BlockSpec((None, tk, tn), lambda i, j, k: (j // nb, k, j % nb))]
    args = [a, w]
    if res is not None:
        in_specs.append(pl.BlockSpec((tm, tn), lambda i, j, k: (i, j)))
        args.append(res)
    return pl.pallas_call(
        _mm_body(nk, (((1,), (0,)), ((), ())), res is not None),
        out_shape=jax.ShapeDtypeStruct((m, n), out_dtype),
        grid=(m // tm, n // tn, nk), in_specs=in_specs,
        out_specs=pl.BlockSpec((tm, tn), lambda i, j, k: (i, j)),
        scratch_shapes=[pltpu.VMEM((tm, tn), F32)] if nk > 1 else [],
        compiler_params=_cp("parallel", "parallel", "arbitrary"), name=name)(*args)


def mm_nt(g, w, out_dtype, name="mm_nt", after=None):
    m, n = g.shape
    ns_, kw, nsz = w.shape
    assert n == ns_ * nsz
    tm, tn, tk = _pick(m, TM_PREFS), _pick(kw, TN_PREFS), _pick(nsz, TK_PREFS)
    nb, nk = nsz // tk, n // tk
    body = _mm_body(nk, (((1,), (1,)), ((), ())), False)
    tied = [] if after is None else [after]
    return pl.pallas_call(
        (lambda g_ref, w_ref, *rest: body(g_ref, w_ref, *rest[len(tied):])),
        out_shape=jax.ShapeDtypeStruct((m, kw), out_dtype),
        grid=(m // tm, kw // tn, nk),
        in_specs=[pl.BlockSpec((tm, tk), lambda i, j, k: (i, k)),
                  pl.BlockSpec((None, tn, tk), lambda i, j, k: (k // nb, j, k % nb))] + [ANY] * len(tied),
        out_specs=pl.BlockSpec((tm, tn), lambda i, j, k: (i, j)),
        scratch_shapes=[pltpu.VMEM((tm, tn), F32)] if nk > 1 else [],
        compiler_params=_cp("parallel", "parallel", "arbitrary"), name=name)(g, w, *tied)


def mm_tn(a, g, n_shards, out_dtype, name="mm_tn"):
    t, kw = a.shape
    t2, n = g.shape
    assert t == t2
    nsz = n // n_shards
    tm, tn, tk = _pick(kw, TM_PREFS), _pick(nsz, TN_PREFS), _pick(t, TK_PREFS)
    nb, nk = nsz // tn, t // tk
    return pl.pallas_call(
        _mm_body(nk, (((0,), (0,)), ((), ())), False),
        out_shape=jax.ShapeDtypeStruct((n_shards, kw, nsz), out_dtype),
        grid=(kw // tm, n // tn, nk),
        in_specs=[pl.BlockSpec((tk, tm), lambda i, j, k: (k, i)),
                  pl.BlockSpec((tk, tn), lambda i, j, k: (k, j))],
        out_specs=pl.BlockSpec((None, tm, tn), lambda i, j, k: (j // nb, i, j % nb)),
        scratch_shapes=[pltpu.VMEM((tm, tn), F32)] if nk > 1 else [],
        compiler_params=_cp("parallel", "parallel", "arbitrary"), name=name)(a, g)


ROW_TILE = 256


def _rms_stats(x):
    r = lax.rsqrt(jnp.mean(x * x, axis=-1, keepdims=True) + EPS)
    return r, x * r


def _rms_back(xh, r, g, dh):
    u = dh * g
    return r * (u - xh * jnp.mean(u * xh, axis=-1, keepdims=True))


def rms_fwd(x, g, out_dtype, name="rms_fwd", after=None):
    rows, d = x.shape
    tr = _pick(rows, (ROW_TILE, 128))
    tied = [] if after is None else [after]

    def body(x_ref, g_ref, *rest):
        o_ref = rest[-1]
        _, xh = _rms_stats(x_ref[...])
        o_ref[...] = (xh * g_ref[...]).astype(o_ref.dtype)

    return pl.pallas_call(
        body, out_shape=jax.ShapeDtypeStruct((rows, d), out_dtype), grid=(rows // tr,),
        in_specs=[pl.BlockSpec((tr, d), lambda i: (i, 0)), pl.BlockSpec((1, d), lambda i: (0, 0))] + [ANY] * len(tied),
        out_specs=pl.BlockSpec((tr, d), lambda i: (i, 0)),
        compiler_params=_cp("parallel"), name=name)(x, g.reshape(1, d), *tied)


def rms_bwd(x, g, dh, dres=None, name="rms_bwd"):
    rows, d = x.shape
    tr = _pick(rows, (ROW_TILE, 128))
    has_res = dres is not None

    def body(*refs):
        if has_res:
            x_ref, g_ref, dh_ref, dres_ref, dx_ref, dg_ref = refs
        else:
            x_ref, g_ref, dh_ref, dx_ref, dg_ref = refs
        r, xh = _rms_stats(x_ref[...])
        dh_ = dh_ref[...].astype(F32)
        part = jnp.sum(dh_ * xh, axis=0, keepdims=True)

        @pl.when(pl.program_id(0) == 0)
        def _():
            dg_ref[...] = part

        @pl.when(pl.program_id(0) > 0)
        def _():
            dg_ref[...] += part

        dx = _rms_back(xh, r, g_ref[...], dh_)
        if has_res:
            dx = dx + dres_ref[...]
        dx_ref[...] = dx

    row_spec = pl.BlockSpec((tr, d), lambda i: (i, 0))
    vec_spec = pl.BlockSpec((1, d), lambda i: (0, 0))
    args = [x, g.reshape(1, d), dh] + ([dres] if has_res else [])
    return pl.pallas_call(
        body, out_shape=(jax.ShapeDtypeStruct((rows, d), F32), jax.ShapeDtypeStruct((1, d), F32)),
        grid=(rows // tr,), in_specs=[row_spec, vec_spec, row_spec] + ([row_spec] if has_res else []),
        out_specs=(row_spec, vec_spec), compiler_params=_cp("arbitrary"), name=name)(*args)


def final_loss(x, g, target, name="final_loss"):
    rows, d = x.shape
    tr = _pick(rows, (ROW_TILE, 128))

    def body(x_ref, g_ref, t_ref, loss_ref, dx_ref, dg_ref):
        r, xh = _rms_stats(x_ref[...])
        gain = g_ref[...]
        err = xh * gain - t_ref[...]
        sq = jnp.sum(jnp.sum(err * err, axis=1, keepdims=True), axis=0, keepdims=True) * (0.5 / d)
        dy = err * (1.0 / d)
        part = jnp.sum(dy * xh, axis=0, keepdims=True)

        @pl.when(pl.program_id(0) == 0)
        def _():
            dg_ref[...] = part
            loss_ref[...] = jnp.broadcast_to(sq, loss_ref.shape)

        @pl.when(pl.program_id(0) > 0)
        def _():
            dg_ref[...] += part
            loss_ref[...] += jnp.broadcast_to(sq, loss_ref.shape)

        dx_ref[...] = _rms_back(xh, r, gain, dy)

    row_spec = pl.BlockSpec((tr, d), lambda i: (i, 0))
    vec_spec = pl.BlockSpec((1, d), lambda i: (0, 0))
    return pl.pallas_call(
        body, out_shape=(jax.ShapeDtypeStruct((1, LANES), F32), jax.ShapeDtypeStruct((rows, d), F32),
                         jax.ShapeDtypeStruct((1, d), F32)),
        grid=(rows // tr,), in_specs=[row_spec, vec_spec, row_spec],
        out_specs=(pl.BlockSpec((1, LANES), lambda i: (0, 0)), row_spec, vec_spec),
        compiler_params=_cp("arbitrary"), name=name)(x, g.reshape(1, d), target)


def _alibi_slopes():
    return (2.0 ** (-8.0 * (np.arange(12) + 1) / 12)).astype(np.float32)


def _band_scores(q, k, q0, start, wk, slope):
    s = lax.dot_general(q, k, (((1,), (1,)), ((), ())), preferred_element_type=F32) * (HEAD ** -0.5)
    qpos = q0 + lax.broadcasted_iota(jnp.int32, (QBLK, wk), 0)
    kpos = start + lax.broadcasted_iota(jnp.int32, (QBLK, wk), 1)
    rel = jnp.abs(qpos - kpos)
    return jnp.where(rel <= N_SIDE, s - slope * rel.astype(F32), NEG)


def _attn_geometry(seq, dilation):
    length = seq // dilation
    return length, length // QBLK, min(2 * QBLK, length)


def _attn_window(n, length, wk):
    q0 = pl.multiple_of(n * QBLK, QBLK)
    start = pl.multiple_of(jnp.clip(n * QBLK - N_SIDE, 0, length - wk), N_SIDE)
    return q0, start


def _class_in(refs, scratch, r, dilation, length):
    if dilation == 1:
        return refs
    for ref, buf in zip(refs, scratch):
        buf[...] = ref[pl.ds(r, length, stride=dilation), :]
    return scratch


def _class_out(refs, scratch, r, dilation, length):
    if dilation > 1:
        for ref, buf in zip(refs, scratch):
            ref[pl.ds(r, length, stride=dilation), :] = buf[...]


def attn_fwd(proj, group, name):
    seq = proj.shape[0]
    dilation = A_PATTERNS[group][1]
    length, nblk, wk = _attn_geometry(seq, dilation)

    def body(slope_ref, q_ref, k_ref, v_ref, o_ref, lse_ref, *scratch):
        slope = slope_ref[group * 4 + pl.program_id(0)] * float(dilation)
        for r in range(dilation):
            q_c, k_c, v_c = _class_in((q_ref, k_ref, v_ref), scratch[:3], r, dilation, length)
            o_c, lse_c = (o_ref, lse_ref) if dilation == 1 else scratch[3:]

            def blk(n, carry):
                q0, start = _attn_window(n, length, wk)
                q = q_c[pl.ds(q0, QBLK), :].astype(BF16)
                k = k_c[pl.ds(start, wk), :].astype(BF16)
                v = v_c[pl.ds(start, wk), :].astype(BF16)
                s = _band_scores(q, k, q0, start, wk, slope)
                m = jnp.max(s, axis=-1, keepdims=True)
                p = jnp.exp(s - m)
                l = jnp.sum(p, axis=-1, keepdims=True)
                o = jnp.dot(p.astype(BF16), v, preferred_element_type=F32) / l
                o_c[pl.ds(q0, QBLK), :] = o
                lse_c[pl.ds(q0, QBLK), :] = jnp.broadcast_to(m + jnp.log(l), (QBLK, HEAD))
                return carry

            lax.fori_loop(0, nblk, blk, 0)
            _class_out((o_ref, lse_ref), scratch[3:], r, dilation, length)

    def part(p):
        return pl.BlockSpec((seq, HEAD), lambda h: (0, p * 12 + group * 4 + h))

    out_spec = pl.BlockSpec((seq, HEAD), lambda h: (0, h))
    return pl.pallas_call(
        body, out_shape=(jax.ShapeDtypeStruct((seq, A_OUT_W), F32),) * 2, grid=(4,),
        in_specs=[pl.BlockSpec(memory_space=pltpu.SMEM), part(0), part(1), part(2)],
        out_specs=(out_spec, out_spec),
        scratch_shapes=[pltpu.VMEM((length, HEAD), F32)] * (5 if dilation > 1 else 0),
        compiler_params=_cp("parallel"), name=name)(jnp.asarray(_alibi_slopes()), proj, proj, proj)


def attn_combine(os_, lses, name="attn_combine"):
    seq = os_[0].shape[0]
    tr = ROW_TILE

    def body(o0, o1, o2, l0, l1, l2, c_ref, lse_ref):
        a, b, c = l0[...], l1[...], l2[...]
        m = jnp.maximum(jnp.maximum(a, b), c)
        ea, eb, ec = jnp.exp(a - m), jnp.exp(b - m), jnp.exp(c - m)
        den = ea + eb + ec
        c_ref[...] = (ea * o0[...] + eb * o1[...] + ec * o2[...]) / den
        lse_ref[...] = m + jnp.log(den)

    spec = pl.BlockSpec((tr, A_OUT_W), lambda i: (i, 0))
    return pl.pallas_call(
        body, out_shape=(jax.ShapeDtypeStruct((seq, A_OUT_W), F32),) * 2, grid=(seq // tr,),
        in_specs=[spec] * 6, out_specs=(spec, spec), compiler_params=_cp("parallel"), name=name)(*os_, *lses)


def attn_bwd(proj, dcat, comb, lse, group, name):
    seq = proj.shape[0]
    dilation = A_PATTERNS[group][1]
    length, nblk, wk = _attn_geometry(seq, dilation)
    scale = HEAD ** -0.5

    def body(slope_ref, q_ref, k_ref, v_ref, do_ref, c_ref, lse_ref, dq_ref, dk_ref, dv_ref, *scratch):
        slope = slope_ref[group * 4 + pl.program_id(0)] * float(dilation)
        for r in range(dilation):
            q_c, k_c, v_c, do_c, c_c, lse_c = _class_in((q_ref, k_ref, v_ref, do_ref, c_ref, lse_ref), scratch[:6], r,
                                                        dilation, length)
            dq_c, dk_c, dv_c = (dq_ref, dk_ref, dv_ref) if dilation == 1 else scratch[6:]
            dk_c[...] = jnp.zeros_like(dk_c)
            dv_c[...] = jnp.zeros_like(dv_c)

            def blk(n, carry):
                q0, start = _attn_window(n, length, wk)
                rows = pl.ds(q0, QBLK)
                keys = pl.ds(start, wk)
                q = q_c[rows, :].astype(BF16)
                k = k_c[keys, :].astype(BF16)
                v = v_c[keys, :].astype(BF16)
                do = do_c[rows, :]
                s = _band_scores(q, k, q0, start, wk, slope)
                p = jnp.exp(s - lse_c[rows, :][:, :1])
                delta = jnp.sum(do * c_c[rows, :], axis=-1, keepdims=True)
                do16 = do.astype(BF16)
                dp = lax.dot_general(do16, v, (((1,), (1,)), ((), ())), preferred_element_type=F32)
                ds = (p * (dp - delta) * scale).astype(BF16)
                p16 = p.astype(BF16)
                dq_c[rows, :] = jnp.dot(ds, k, preferred_element_type=F32)
                dk_c[keys, :] += lax.dot_general(ds, q, (((0,), (0,)), ((), ())), preferred_element_type=F32)
                dv_c[keys, :] += lax.dot_general(p16, do16, (((0,), (0,)), ((), ())), preferred_element_type=F32)
                return carry

            lax.fori_loop(0, nblk, blk, 0)
            _class_out((dq_ref, dk_ref, dv_ref), scratch[6:], r, dilation, length)

    def part(p):
        return pl.BlockSpec((seq, HEAD), lambda h: (0, p * 12 + group * 4 + h))

    hs = pl.BlockSpec((seq, HEAD), lambda h: (0, h))
    return pl.pallas_call(
        body, out_shape=(jax.ShapeDtypeStruct((seq, A_OUT_W), F32),) * 3, grid=(4,),
        in_specs=[pl.BlockSpec(memory_space=pltpu.SMEM), part(0), part(1), part(2), hs, hs, hs],
        out_specs=(hs, hs, hs),
        scratch_shapes=[pltpu.VMEM((length, HEAD), F32)] * (9 if dilation > 1 else 0),
        compiler_params=_cp("parallel"), name=name,
    )(jnp.asarray(_alibi_slopes()), proj, proj, proj, dcat, comb, lse)


MEM_ROW_TILE = 512


def _mem_probs(q, k):
    s = lax.dot_general(q, k, (((1,), (1,)), ((), ())), preferred_element_type=F32) * (HEAD ** -0.5)
    p = jnp.exp(s - jnp.max(s, axis=-1, keepdims=True))
    return p / jnp.sum(p, axis=-1, keepdims=True)


def mem_fwd(proj, q_col, kv, name="mem_fwd"):
    seq = proj.shape[0]
    qb = q_col // HEAD

    def body(q_ref, k_ref, v_ref, o_ref):
        p = _mem_probs(q_ref[...].astype(BF16), k_ref[...].astype(BF16))
        o_ref[...] = jnp.dot(p.astype(BF16), v_ref[...].astype(BF16), preferred_element_type=F32).astype(o_ref.dtype)

    return pl.pallas_call(
        body, out_shape=jax.ShapeDtypeStruct((seq, MEM_W), BF16), grid=(MEM_HEADS, seq // MEM_ROW_TILE),
        in_specs=[pl.BlockSpec((MEM_ROW_TILE, HEAD), lambda h, i: (i, qb + h)),
                  pl.BlockSpec((MEM_LEN, HEAD), lambda h, i: (0, h)),
                  pl.BlockSpec((MEM_LEN, HEAD), lambda h, i: (0, MEM_HEADS + h))],
        out_specs=pl.BlockSpec((MEM_ROW_TILE, HEAD), lambda h, i: (i, h)),
        compiler_params=_cp("parallel", "parallel"), name=name)(proj, kv, kv)


def mem_bwd(proj, q_col, kv, dcat, do_col, name="mem_bwd"):
    seq = proj.shape[0]
    qb, ob = q_col // HEAD, do_col // HEAD
    scale = HEAD ** -0.5

    def body(q_ref, k_ref, v_ref, do_ref, dq_ref, dk_ref, dv_ref):
        q = q_ref[...].astype(BF16)
        k = k_ref[...].astype(BF16)
        v = v_ref[...].astype(BF16)
        do = do_ref[...].astype(BF16)
        p = _mem_probs(q, k)
        dp = lax.dot_general(do, v, (((1,), (1,)), ((), ())), preferred_element_type=F32)
        ds = (p * (dp - jnp.sum(dp * p, axis=-1, keepdims=True)) * scale).astype(BF16)
        dq_ref[...] = jnp.dot(ds, k, preferred_element_type=F32).astype(dq_ref.dtype)
        dk = lax.dot_general(ds, q, (((0,), (0,)), ((), ())), preferred_element_type=F32)
        dv = lax.dot_general(p.astype(BF16), do, (((0,), (0,)), ((), ())), preferred_element_type=F32)

        @pl.when(pl.program_id(1) == 0)
        def _():
            dk_ref[...] = dk
            dv_ref[...] = dv

        @pl.when(pl.program_id(1) > 0)
        def _():
            dk_ref[...] += dk
            dv_ref[...] += dv

    dq, dk, dv = pl.pallas_call(
        body, out_shape=(jax.ShapeDtypeStruct((seq, MEM_W), BF16), jax.ShapeDtypeStruct((MEM_LEN, MEM_W), F32),
                         jax.ShapeDtypeStruct((MEM_LEN, MEM_W), F32)),
        grid=(MEM_HEADS, seq // MEM_ROW_TILE),
        in_specs=[pl.BlockSpec((MEM_ROW_TILE, HEAD), lambda h, i: (i, qb + h)),
                  pl.BlockSpec((MEM_LEN, HEAD), lambda h, i: (0, h)),
                  pl.BlockSpec((MEM_LEN, HEAD), lambda h, i: (0, MEM_HEADS + h)),
                  pl.BlockSpec((MEM_ROW_TILE, HEAD), lambda h, i: (i, ob + h))],
        out_specs=(pl.BlockSpec((MEM_ROW_TILE, HEAD), lambda h, i: (i, h)),
                   pl.BlockSpec((MEM_LEN, HEAD), lambda h, i: (0, h)),
                   pl.BlockSpec((MEM_LEN, HEAD), lambda h, i: (0, h))),
        compiler_params=_cp("parallel", "arbitrary"), name=name)(proj, kv, kv, dcat)
    return dq, jnp.concatenate([dk, dv], axis=1)


def _sgu_front(x, gain):
    uv, duv = _gelu_parts(x)
    u, v = uv[:, :B_W], uv[:, B_W:]
    r, vh = _rms_stats(v)
    return u, duv, r, vh, vh * gain


def sgu_fwd(proj, gain, w_s, bias_b, name="sgu_fwd"):
    seq = proj.shape[0]

    def body(x_ref, gain_ref, ws_ref, bias_ref, o_ref):
        u, _, _, _, vn = _sgu_front(x_ref[...], gain_ref[...])
        for g in range(B_GROUPS):
            cs = slice(g * CHUNK, (g + 1) * CHUNK)
            mixed = jnp.dot(ws_ref[g].astype(BF16), vn[:, cs].astype(BF16), preferred_element_type=F32) + bias_ref[g]
            o_ref[:, cs] = (u[:, cs] * mixed).astype(o_ref.dtype)

    full = lambda shape: pl.BlockSpec(shape, lambda c: (0,) * len(shape))
    return pl.pallas_call(
        body, out_shape=jax.ShapeDtypeStruct((seq, B_W), BF16), grid=(seq // CHUNK,),
        in_specs=[pl.BlockSpec((CHUNK, 2 * B_W), lambda c: (c, 0)), full((1, B_W)),
                  full((B_GROUPS, CHUNK, CHUNK)), full((B_GROUPS, CHUNK, CHUNK))],
        out_specs=pl.BlockSpec((CHUNK, B_W), lambda c: (c, 0)),
        compiler_params=_cp("parallel"), name=name)(proj, gain.reshape(1, B_W), w_s, bias_b)


def sgu_bwd(proj, gain, w_s, w_s_t, bias_b, dcat, name="sgu_bwd"):
    seq = proj.shape[0]

    def body(x_ref, gain_ref, ws_ref, wst_ref, bias_ref, do_ref, dx_ref, dws_ref, dmix_ref, dgain_ref, dvn_ref):
        first = pl.program_id(0) == 0
        gain = gain_ref[...]
        u, duv, r, vh, vn = _sgu_front(x_ref[...], gain)
        do = do_ref[...]
        for g in range(B_GROUPS):
            cs = slice(g * CHUNK, (g + 1) * CHUNK)
            vg = vn[:, cs].astype(BF16)
            mixed = jnp.dot(ws_ref[g].astype(BF16), vg, preferred_element_type=F32) + bias_ref[g]
            dx_ref[:, cs] = (do[:, cs] * mixed * duv[:, cs]).astype(dx_ref.dtype)
            dmixed = do[:, cs] * u[:, cs]
            dm16 = dmixed.astype(BF16)
            dws = lax.dot_general(dm16, vg, (((1,), (1,)), ((), ())), preferred_element_type=F32)
            dvn_ref[:, cs] = jnp.dot(wst_ref[g].astype(BF16), dm16, preferred_element_type=F32)

            @pl.when(first)
            def _():
                dws_ref[g] = dws
                dmix_ref[g] = dmixed

            @pl.when(jnp.logical_not(first))
            def _():
                dws_ref[g] += dws
                dmix_ref[g] += dmixed

        dvn = dvn_ref[...]
        dgain = jnp.sum(dvn * vh, axis=0, keepdims=True)

        @pl.when(first)
        def _():
            dgain_ref[...] = dgain

        @pl.when(jnp.logical_not(first))
        def _():
            dgain_ref[...] += dgain

        dv = _rms_back(vh, r, gain, dvn)
        dx_ref[:, B_W:] = (dv * duv[:, B_W:]).astype(dx_ref.dtype)

    full = lambda shape: pl.BlockSpec(shape, lambda c: (0,) * len(shape))
    mats = full((B_GROUPS, CHUNK, CHUNK))
    return pl.pallas_call(
        body, out_shape=(jax.ShapeDtypeStruct((seq, 2 * B_W), BF16), jax.ShapeDtypeStruct((B_GROUPS, CHUNK, CHUNK), F32),
                         jax.ShapeDtypeStruct((B_GROUPS, CHUNK, CHUNK), F32), jax.ShapeDtypeStruct((1, B_W), F32)),
        grid=(seq // CHUNK,),
        in_specs=[pl.BlockSpec((CHUNK, 2 * B_W), lambda c: (c, 0)), full((1, B_W)), mats, mats, mats,
                  pl.BlockSpec((CHUNK, B_W), lambda c: (c, 0))],
        out_specs=(pl.BlockSpec((CHUNK, 2 * B_W), lambda c: (c, 0)), mats, mats, full((1, B_W))),
        scratch_shapes=[pltpu.VMEM((CHUNK, B_W), F32)],
        compiler_params=_cp("arbitrary"), name=name)(proj, gain.reshape(1, B_W), w_s, w_s_t, bias_b, dcat)


FFN_COLS = 128
FFN_ROWS = 32
SUBLANES = 8


def _window(ref, r0, first, last):
    cols = ref.shape[1]
    pad = jnp.zeros((SUBLANES, cols), F32)
    if first:
        return jnp.concatenate([pad, ref[pl.ds(0, FFN_ROWS + SUBLANES), :]], axis=0)
    if last:
        return jnp.concatenate([ref[pl.ds(r0 - SUBLANES, FFN_ROWS + SUBLANES), :], pad], axis=0)
    return ref[pl.ds(pl.multiple_of(r0 - SUBLANES, SUBLANES), FFN_ROWS + 2 * SUBLANES), :]


def _taps(win):
    mid = slice(SUBLANES, SUBLANES + FFN_ROWS)
    return pltpu.roll(win, 1, 0)[mid], win[mid], pltpu.roll(win, win.shape[0] - 1, 0)[mid]


def _row_steps(seq, step, carry):
    n = seq // FFN_ROWS
    carry = step(0, True, False, carry)
    carry = lax.fori_loop(1, n - 1, lambda i, c: step(pl.multiple_of(i * FFN_ROWS, FFN_ROWS), False, False, c), carry)
    return step(seq - FFN_ROWS, False, True, carry)


def _conv3(taps, w, b):
    prev, cur, nxt = taps
    return prev * w[0:1] + cur * w[1:2] + nxt * w[2:3] + b


def _fold(x):
    return jnp.sum(x.reshape(FFN_ROWS // SUBLANES, SUBLANES, x.shape[1]), axis=0)


FFN_FWD_COLS = 256


def _taps_whole(a):
    n = a.shape[0]
    rows = lax.broadcasted_iota(jnp.int32, a.shape, 0)
    return (jnp.where(rows == 0, 0.0, pltpu.roll(a, 1, 0)), a, jnp.where(rows == n - 1, 0.0, pltpu.roll(a, n - 1, 0)))


def ffn_act_fwd(a, conv_w, conv_b, name="ffn_act_fwd"):
    seq = a.shape[0]
    nb = FF // FFN_FWD_COLS

    def body(ag_ref, av_ref, wg_ref, wv_ref, bg_ref, bv_ref, o_ref):
        gate = _conv3(_taps_whole(ag_ref[...]), wg_ref[...], bg_ref[...])
        val = _conv3(_taps_whole(av_ref[...]), wv_ref[...], bv_ref[...])
        o_ref[...] = (_gelu(gate) * val).astype(o_ref.dtype)

    col = lambda rows, off: pl.BlockSpec((rows, FFN_FWD_COLS), lambda j: (0, j + off))
    cb = conv_b.reshape(1, 2 * FF)
    return pl.pallas_call(
        body, out_shape=jax.ShapeDtypeStruct((seq, FF), BF16), grid=(nb,),
        in_specs=[col(seq, 0), col(seq, nb), col(3, 0), col(3, nb), col(1, 0), col(1, nb)],
        out_specs=col(seq, 0), compiler_params=_cp("parallel"), name=name)(a, a, conv_w, conv_w, cb, cb)


def ffn_act_bwd(a, conv_w, conv_b, dact, name="ffn_act_bwd"):
    seq = a.shape[0]
    nb = FF // FFN_COLS

    def body(ag_ref, av_ref, wg_ref, wv_ref, bg_ref, bv_ref, d_ref, dag_ref, dav_ref, dwg_ref, dwv_ref, dbg_ref, dbv_ref,
             dcg_ref, dcv_ref):
        wg, wv, bg, bv = wg_ref[...], wv_ref[...], bg_ref[...], bv_ref[...]

        def conv_grads(r0, first, last, sums):
            g_taps = _taps(_window(ag_ref, r0, first, last))
            v_taps = _taps(_window(av_ref, r0, first, last))
            act, dact_dgate = _gelu_parts(_conv3(g_taps, wg, bg))
            d = d_ref[pl.ds(r0, FFN_ROWS), :].astype(F32)
            dcg = d * _conv3(v_taps, wv, bv) * dact_dgate
            dcv = d * act
            dcg_ref[pl.ds(r0, FFN_ROWS), :] = dcg
            dcv_ref[pl.ds(r0, FFN_ROWS), :] = dcv
            new = [_fold(dcg)] + [_fold(dcg * t) for t in g_taps] + [_fold(dcv)] + [_fold(dcv * t) for t in v_taps]
            return tuple(s + n for s, n in zip(sums, new))

        zero = jnp.zeros((SUBLANES, FFN_COLS), F32)
        sums = _row_steps(seq, conv_grads, (zero,) * 8)
        total = [jnp.sum(s, axis=0, keepdims=True) for s in sums]
        dbg_ref[...] = total[0]
        dbv_ref[...] = total[4]
        for k in range(3):
            dwg_ref[k:k + 1, :] = total[1 + k]
            dwv_ref[k:k + 1, :] = total[5 + k]

        def conv_transpose(r0, first, last, carry):
            for dc_ref, w, da_ref in ((dcg_ref, wg, dag_ref), (dcv_ref, wv, dav_ref)):
                prev, cur, nxt = _taps(_window(dc_ref, r0, first, last))
                da_ref[pl.ds(r0, FFN_ROWS), :] = (nxt * w[0:1] + cur * w[1:2] + prev * w[2:3]).astype(da_ref.dtype)
            return carry

        _row_steps(seq, conv_transpose, 0)

    col = lambda rows, off: pl.BlockSpec((rows, FFN_COLS), lambda j: (0, j + off))
    cb = conv_b.reshape(1, 2 * FF)
    dag, dav, dwg, dwv, dbg, dbv = pl.pallas_call(
        body, out_shape=(jax.ShapeDtypeStruct((seq, FF), BF16),) * 2 + (jax.ShapeDtypeStruct((3, FF), F32),) * 2
        + (jax.ShapeDtypeStruct((1, FF), F32),) * 2, grid=(nb,),
        in_specs=[col(seq, 0), col(seq, nb), col(3, 0), col(3, nb), col(1, 0), col(1, nb), col(seq, 0)],
        out_specs=(col(seq, 0), col(seq, 0), col(3, 0), col(3, 0), col(1, 0), col(1, 0)),
        scratch_shapes=[pltpu.VMEM((seq, FFN_COLS), F32), pltpu.VMEM((seq, FFN_COLS), F32)],
        compiler_params=_cp("parallel"), name=name)(a, a, conv_w, conv_w, cb, cb, dact)
    cat = lambda p, q: jnp.concatenate([p, q], axis=1)
    return cat(dag, dav), cat(dwg, dwv), cat(dbg, dbv)


def _adam_math(w, g, m, v):
    m = ADAM_B1 * m + (1.0 - ADAM_B1) * g
    v = ADAM_B2 * v + (1.0 - ADAM_B2) * (g * g)
    m_hat = m / (1.0 - ADAM_B1 ** ADAM_STEP)
    v_hat = v / (1.0 - ADAM_B2 ** ADAM_STEP)
    return -ADAM_LR * (m_hat / (jnp.sqrt(v_hat) + ADAM_EPS) + ADAM_WD * w), m, v


def _row_tile(rows, cols):
    return _pick(rows, (256, 128, 64)) if cols <= 1024 else _pick(rows, (128, 64))


BF16_ROWS = 16
STREAM_BLOCK_BYTES = 3 * 1024 * 1024


def _stream_rows(rows, cols, itemsize):
    fits = [r for r in range(BF16_ROWS, rows + 1, BF16_ROWS) if rows % r == 0 and r * cols * itemsize <= STREAM_BLOCK_BYTES]
    return max(fits)


def adamw_layer(w_all, m_all, v_all, layer, g, prev, name):
    n, rows, cols = w_all.shape
    tr = _row_tile(rows, cols)

    def body(w_ref, m_ref, v_ref, g_ref, *rest):
        go_ref, d_ref, mo_ref, vo_ref = rest[-4:]
        g_ = g_ref[...]
        d, m_, v_ = _adam_math(w_ref[...], g_, m_ref[...], v_ref[...])
        go_ref[...] = g_
        d_ref[...] = d
        mo_ref[...] = m_
        vo_ref[...] = v_

    lay = pl.BlockSpec((None, tr, cols), lambda i: (layer, i, 0))
    in_specs = [lay, lay, lay, pl.BlockSpec((tr, cols), lambda i: (i, 0))]
    args = [w_all, m_all, v_all, g]
    aliases = {}
    if prev is not None:
        in_specs += [pl.BlockSpec(memory_space=pl.ANY)] * 4
        args += list(prev)
        aliases = {4 + k: k for k in range(4)}
    return pl.pallas_call(
        body, out_shape=(jax.ShapeDtypeStruct(w_all.shape, F32),) * 4, grid=(rows // tr,),
        in_specs=in_specs, out_specs=(lay,) * 4, input_output_aliases=aliases,
        compiler_params=_cp("parallel"), name=name)(*args)


def adamw_flat(w, g, m, v, name="adamw_small"):
    rows, cols = w.shape
    tr = _pick(rows, (128, 8))

    def body(w_ref, g_ref, m_ref, v_ref, d_ref, mo_ref, vo_ref):
        d_ref[...], mo_ref[...], vo_ref[...] = _adam_math(w_ref[...], g_ref[...], m_ref[...], v_ref[...])

    spec = pl.BlockSpec((tr, cols), lambda i: (i, 0))
    return pl.pallas_call(
        body, out_shape=(jax.ShapeDtypeStruct(w.shape, F32),) * 3, grid=(rows // tr,),
        in_specs=[spec] * 4, out_specs=(spec,) * 3, compiler_params=_cp("parallel"), name=name)(w, g, m, v)


def pair_sum(dw, got, core, name):
    _, rows, cols = dw.shape
    half = rows // 2
    tr = _stream_rows(half, cols, 2)
    nrb = half // tr

    def body(c_ref, a_ref, b_ref, o_ref):
        o_ref[...] = (a_ref[...].astype(F32) + b_ref[...].astype(F32)).astype(o_ref.dtype)

    return pl.pallas_call(
        body, out_shape=jax.ShapeDtypeStruct((N_CHIPS, half, cols), BF16),
        grid_spec=pltpu.PrefetchScalarGridSpec(
            num_scalar_prefetch=1, grid=(N_CHIPS, nrb),
            in_specs=[pl.BlockSpec((None, tr, cols), lambda s, i, c_ref: (s, c_ref[0] * nrb + i, 0)),
                      pl.BlockSpec((None, tr, cols), lambda s, i, c_ref: (s, i, 0))],
            out_specs=pl.BlockSpec((None, tr, cols), lambda s, i, c_ref: (s, i, 0))),
        compiler_params=_cp("parallel", "parallel"), name=name)(core, dw, got)


def chip_sum(own, parts, place, name):
    _, half, cols = parts.shape
    tr = _stream_rows(half, cols, 4)
    nrb = half // tr

    def body(p_ref, own_ref, a_ref, b_ref, c_ref, o_ref):
        o_ref[...] = ((own_ref[...].astype(F32) + a_ref[...].astype(F32)) + b_ref[...].astype(F32)) + c_ref[...].astype(F32)

    def slot(k):
        return pl.BlockSpec((None, tr, cols), lambda i, p: (jnp.bitwise_xor(p[0], k), i, 0))

    return pl.pallas_call(
        body, out_shape=jax.ShapeDtypeStruct((2 * half, cols), F32),
        grid_spec=pltpu.PrefetchScalarGridSpec(
            num_scalar_prefetch=1, grid=(nrb,), in_specs=[slot(0), slot(1), slot(2), slot(3)],
            out_specs=pl.BlockSpec((tr, cols), lambda i, p: (p[1] * nrb + i, 0))),
        compiler_params=_cp("parallel"), name=name)(place, own, parts, parts, parts)


def cast_to_slot(w_all, layer, place, name, after=None):
    _, rows, cols = w_all.shape
    tr = _stream_rows(rows, cols, 4)
    tied = [] if after is None else [after]

    def body(p_ref, w_ref, *rest):
        o_ref = rest[-1]
        o_ref[...] = w_ref[...].astype(o_ref.dtype)

    return pl.pallas_call(
        body, out_shape=jax.ShapeDtypeStruct((N_CHIPS, rows, cols), BF16),
        grid_spec=pltpu.PrefetchScalarGridSpec(
            num_scalar_prefetch=1, grid=(rows // tr,),
            in_specs=[pl.BlockSpec((None, tr, cols), lambda i, p: (layer, i, 0))] + [ANY] * len(tied),
            out_specs=pl.BlockSpec((None, tr, cols), lambda i, p: (p[0], i, 0))),
        compiler_params=_cp("parallel"), name=name)(place, w_all, *tied)


ANY = pl.BlockSpec(memory_space=pl.ANY)


def _place():
    x, y, c = lax.axis_index("x"), lax.axis_index("y"), lax.axis_index("c")
    others = [(1 - x, y), (x, 1 - y), (1 - x, 1 - y)]
    return x, y, c, 2 * x + y, others


def _remote(src, dst, send_sem, recv_sem, dev):
    return pltpu.make_async_remote_copy(src_ref=src, dst_ref=dst, send_sem=send_sem, recv_sem=recv_sem,
                                        device_id=dev, device_id_type=MESH)


HBM = pl.BlockSpec(memory_space=pltpu.HBM)
SEM = pl.BlockSpec(memory_space=pltpu.SEMAPHORE)
EFFECT = pltpu.SideEffectType.DATAFLOW_SIDE_EFFECTING
TOKEN = jax.ShapeDtypeStruct((8, LANES), F32)


def _in_hbm(a):
    return pltpu.with_memory_space_constraint(a, pltpu.HBM)


def _gather_copies(bufs, send_sems, recv_sems):
    x, y, c, me, others = _place()
    out = []
    for w, buf in enumerate(bufs):
        half = buf.shape[1] // 2
        mine = pl.ds(c * half, half)
        for k, (ox, oy) in enumerate(others):
            sems = send_sems.at[3 * w + k], recv_sems.at[3 * w + k]
            out.append((_remote(buf.at[me, mine], buf.at[me, mine], *sems, (ox, oy, c)),
                        _remote(buf.at[me, mine], buf.at[2 * ox + oy, mine], *sems, (ox, oy, c))))
    return out


def _forward_copies(bufs, send_sems, recv_sems):
    x, y, c, me, others = _place()
    out = []
    for w, buf in enumerate(bufs):
        half = buf.shape[1] // 2
        mine, theirs = pl.ds(c * half, half), pl.ds((1 - c) * half, half)
        for k, (ox, oy) in enumerate(others):
            sems = send_sems.at[3 * w + k], recv_sems.at[3 * w + k]
            slot = 2 * ox + oy
            out.append((_remote(buf.at[slot, mine], buf.at[slot, mine], *sems, (x, y, 1 - c)),
                        _remote(buf.at[slot, mine], buf.at[slot, theirs], *sems, (x, y, 1 - c))))
    return out


def _join_copies(grads, send_sems, recv_sems):
    x, y, c, _, _ = _place()
    out = []
    for w, g in enumerate(grads):
        half = g.shape[0] // 2
        mine, theirs = pl.ds(c * half, half), pl.ds((1 - c) * half, half)
        sems = send_sems.at[w], recv_sems.at[w]
        out.append((_remote(g.at[mine], g.at[mine], *sems, (x, y, 1 - c)), _remote(g.at[mine], g.at[theirs], *sems, (x, y, 1 - c))))
    return out


IN_PLACE = dict(gather=(_gather_copies, 3), forward=(_forward_copies, 3), join=(_join_copies, 1))


def copies_start(kind, bufs, name):
    n = len(bufs)
    copies, per_buf = IN_PLACE[kind]

    def body(*refs):
        ins, (send_sems, recv_sems), token = refs[:n], refs[n:n + 2], refs[-1]
        for sent, _ in copies(ins, send_sems, recv_sems):
            sent.start()
        token[...] = jnp.zeros_like(token)

    outs = pl.pallas_call(
        body, name=name,
        out_shape=(pltpu.SemaphoreType.DMA((per_buf * n,)), pltpu.SemaphoreType.DMA((per_buf * n,)),
                   *[pltpu.HBM(b.shape, b.dtype) for b in bufs], TOKEN),
        in_specs=[HBM] * n, out_specs=(SEM, SEM, *[HBM] * n, VM),
        input_output_aliases={w: 2 + w for w in range(n)},
        compiler_params=pltpu.CompilerParams(has_side_effects=EFFECT))(*[_in_hbm(b) for b in bufs])
    return outs[0], outs[1], list(outs[2:2 + n]), outs[-1]


def copies_wait(kind, send_sems, recv_sems, bufs, after, name):
    n = len(bufs)
    copies, _ = IN_PLACE[kind]

    def body(*refs):
        ins, (send_ref, recv_ref) = refs[:n], refs[n:n + 2]
        for sent, landed in copies(ins, send_ref, recv_ref):
            sent.wait_send()
            landed.wait_recv()

    return list(pl.pallas_call(
        body, name=name, out_shape=tuple(pltpu.HBM(b.shape, b.dtype) for b in bufs),
        in_specs=[HBM] * n + [SEM, SEM, ANY], out_specs=(HBM,) * n,
        input_output_aliases={w: w for w in range(n)},
        compiler_params=pltpu.CompilerParams(has_side_effects=EFFECT))(*bufs, send_sems, recv_sems, after))


def swap_halves(grads, name):
    n = len(grads)

    def body(*refs):
        ins, outs = refs[:n], refs[n:2 * n]
        send_sems, recv_sems = refs[2 * n:]
        x, y, c, _, _ = _place()
        copies = []
        for w in range(n):
            half = ins[w].shape[1] // 2
            cp = _remote(ins[w].at[:, pl.ds((1 - c) * half, half)], outs[w], send_sems.at[w], recv_sems.at[w], (x, y, 1 - c))
            cp.start()
            copies.append(cp)
        for cp in copies:
            cp.wait()

    return pl.pallas_call(
        body, out_shape=tuple(jax.ShapeDtypeStruct((N_CHIPS, g.shape[1] // 2, g.shape[2]), g.dtype) for g in grads),
        in_specs=[ANY] * n, out_specs=(ANY,) * n,
        scratch_shapes=[pltpu.SemaphoreType.DMA((n,)), pltpu.SemaphoreType.DMA((n,))], name=name)(*grads)


def _swap_copies(grads, lands, send_sems, recv_sems):
    x, y, c, _, _ = _place()
    out = []
    for w, (g, land) in enumerate(zip(grads, lands)):
        half = g.shape[1] // 2
        out.append(_remote(g.at[:, pl.ds((1 - c) * half, half)], land, send_sems.at[w], recv_sems.at[w], (x, y, 1 - c)))
    return out


def swap_start(grads, name):
    n = len(grads)

    def body(*refs):
        ins, lands, (send_sems, recv_sems), token = refs[:n], refs[n:2 * n], refs[2 * n:2 * n + 2], refs[-1]
        for cp in _swap_copies(ins, lands, send_sems, recv_sems):
            cp.start()
        token[...] = jnp.zeros_like(token)

    shapes = [(N_CHIPS, g.shape[1] // 2, g.shape[2]) for g in grads]
    zones = [_in_hbm(lax.empty(s, g.dtype)) for s, g in zip(shapes, grads)]
    outs = pl.pallas_call(
        body, name=name,
        out_shape=(pltpu.SemaphoreType.DMA((n,)), pltpu.SemaphoreType.DMA((n,)),
                   *[pltpu.HBM(g.shape, g.dtype) for g in grads], *[pltpu.HBM(s, g.dtype) for s, g in zip(shapes, grads)],
                   TOKEN),
        in_specs=[HBM] * (2 * n), out_specs=(SEM, SEM, *[HBM] * (2 * n), VM),
        input_output_aliases={w: 2 + w for w in range(2 * n)},
        compiler_params=pltpu.CompilerParams(has_side_effects=EFFECT))(*[_in_hbm(g) for g in grads], *zones)
    return outs[0], outs[1], list(outs[2:2 + n]), list(outs[2 + n:2 + 2 * n]), outs[-1]


def swap_wait(send_sems, recv_sems, grads, lands, after, name):
    n = len(grads)

    def body(*refs):
        ins, zones, (send_ref, recv_ref) = refs[:n], refs[n:2 * n], refs[2 * n:2 * n + 2]
        for cp in _swap_copies(ins, zones, send_ref, recv_ref):
            cp.wait_send()
            cp.wait_recv()

    outs = pl.pallas_call(
        body, name=name, out_shape=tuple(pltpu.HBM(a.shape, a.dtype) for a in list(grads) + list(lands)),
        in_specs=[HBM] * (2 * n) + [SEM, SEM, ANY], out_specs=(HBM,) * (2 * n),
        input_output_aliases={w: w for w in range(2 * n)},
        compiler_params=pltpu.CompilerParams(has_side_effects=EFFECT))(*grads, *lands, send_sems, recv_sems, after)
    return list(outs[:n]), list(outs[n:])


def _exchange_copies(sums, lands, send_sems, recv_sems):
    x, y, c, me, others = _place()
    out = []
    for w, (src, land) in enumerate(zip(sums, lands)):
        for k, (ox, oy) in enumerate(others):
            sems = send_sems.at[3 * w + k], recv_sems.at[3 * w + k]
            out.append((_remote(src.at[2 * ox + oy], land.at[me], *sems, (ox, oy, c)),
                        _remote(src.at[2 * ox + oy], land.at[2 * ox + oy], *sems, (ox, oy, c))))
    return out


def exchange_start(sums, name):
    n = len(sums)

    def body(*refs):
        ins, lands, (send_sems, recv_sems), token = refs[:n], refs[n:2 * n], refs[2 * n:2 * n + 2], refs[-1]
        for sent, _ in _exchange_copies(ins, lands, send_sems, recv_sems):
            sent.start()
        token[...] = jnp.zeros_like(token)

    zones = [_in_hbm(lax.empty(s.shape, s.dtype)) for s in sums]
    outs = pl.pallas_call(
        body, name=name,
        out_shape=(pltpu.SemaphoreType.DMA((3 * n,)), pltpu.SemaphoreType.DMA((3 * n,)),
                   *[pltpu.HBM(s.shape, s.dtype) for s in sums] * 2, TOKEN),
        in_specs=[HBM] * (2 * n), out_specs=(SEM, SEM, *[HBM] * (2 * n), VM),
        input_output_aliases={w: 2 + w for w in range(2 * n)},
        compiler_params=pltpu.CompilerParams(has_side_effects=EFFECT))(*[_in_hbm(s) for s in sums], *zones)
    return outs[0], outs[1], list(outs[2:2 + n]), list(outs[2 + n:2 + 2 * n]), outs[-1]


def exchange_wait(send_sems, recv_sems, sums, lands, after, name):
    n = len(sums)

    def body(*refs):
        ins, zones, (send_ref, recv_ref) = refs[:n], refs[n:2 * n], refs[2 * n:2 * n + 2]
        for sent, landed in _exchange_copies(ins, zones, send_ref, recv_ref):
            sent.wait_send()
            landed.wait_recv()

    outs = pl.pallas_call(
        body, name=name, out_shape=tuple(pltpu.HBM(s.shape, s.dtype) for s in sums) * 2,
        in_specs=[HBM] * (2 * n) + [SEM, SEM, ANY], out_specs=(HBM,) * (2 * n),
        input_output_aliases={w: w for w in range(2 * n)},
        compiler_params=pltpu.CompilerParams(has_side_effects=EFFECT))(*sums, *lands, send_sems, recv_sems, after)
    return list(outs[:n]), list(outs[n:])


VM = pl.BlockSpec(memory_space=pltpu.VMEM)


def small_allgather(buf, name="small_allgather"):
    def body(in_ref, out_ref, send_sems, recv_sems):
        x, y, c, me, others = _place()
        out_ref[me] = in_ref[...]
        copies = []
        for k, (ox, oy) in enumerate(others):
            cp = _remote(in_ref, out_ref.at[me], send_sems.at[k], recv_sems.at[k], (ox, oy, c))
            cp.start()
            copies.append(cp)
        for k, (ox, oy) in enumerate(others):
            landed = out_ref.at[2 * ox + oy]
            _remote(landed, landed, send_sems.at[k], recv_sems.at[k], (ox, oy, c)).wait_recv()
        for cp in copies:
            cp.wait_send()

    return pl.pallas_call(
        body, out_shape=jax.ShapeDtypeStruct((N_CHIPS,) + buf.shape, buf.dtype), in_specs=[VM], out_specs=VM,
        scratch_shapes=[pltpu.SemaphoreType.DMA((3,)), pltpu.SemaphoreType.DMA((3,))],
        compiler_params=pltpu.CompilerParams(vmem_limit_bytes=V7X_VMEM_LIMIT), name=name)(buf)


def small_allreduce(buf, after, name="small_allreduce"):
    def body(in_ref, after_ref, out_ref, sib_ref, slot_ref, send_sems, recv_sems):
        x, y, c, me, others = _place()
        cp = _remote(in_ref, sib_ref, send_sems.at[3], recv_sems.at[3], (x, y, 1 - c))
        cp.start()
        cp.wait()
        slot_ref[me] = in_ref[...] + sib_ref[...]
        copies = []
        for k, (ox, oy) in enumerate(others):
            cp = _remote(slot_ref.at[me], slot_ref.at[me], send_sems.at[k], recv_sems.at[k], (ox, oy, c))
            cp.start()
            copies.append(cp)
        for k, (ox, oy) in enumerate(others):
            landed = slot_ref.at[2 * ox + oy]
            _remote(landed, landed, send_sems.at[k], recv_sems.at[k], (ox, oy, c)).wait_recv()
        for cp in copies:
            cp.wait_send()
        out_ref[...] = ((slot_ref[0] + slot_ref[1]) + slot_ref[2]) + slot_ref[3]

    return pl.pallas_call(
        body, out_shape=jax.ShapeDtypeStruct(buf.shape, buf.dtype), in_specs=[VM, ANY], out_specs=VM,
        scratch_shapes=[pltpu.VMEM(buf.shape, buf.dtype), pltpu.VMEM((N_CHIPS,) + buf.shape, buf.dtype),
                        pltpu.SemaphoreType.DMA((4,)), pltpu.SemaphoreType.DMA((4,))],
        compiler_params=pltpu.CompilerParams(vmem_limit_bytes=V7X_VMEM_LIMIT), name=name)(buf, after)


def _pack_rows(arrays, row_multiple):
    flat = jnp.concatenate([a.reshape(-1) for a in arrays])
    rows = -(-flat.shape[0] // (LANES * row_multiple)) * row_multiple
    return jnp.pad(flat, (0, rows * LANES - flat.shape[0])).reshape(rows, LANES)


def _unpack_rows(buf, shapes):
    flat = buf.reshape(-1)
    out, at = [], 0
    for s in shapes:
        n = math.prod(s)
        out.append(flat[at:at + n].reshape(s))
        at += n
    return out


def _mixer_weights(i):
    j = i // 2
    mixer = "a" if i % 2 == 0 else "b"
    return [("w_mem_kv", i), (mixer + "_w_in", j), (mixer + "_w_out", j)]


def _ffn_weights(i):
    return [("ffn_w_up", i), ("ffn_w_down", i)]


def _mixer_fwd(i, x, mem, w, small, after):
    is_a = i % 2 == 0
    j = i // 2
    wkv, win, wout = w
    wkv = wkv.reshape(1, D_MODEL, 2 * MEM_W)
    h1 = rms_fwd(x, small["mix_norm_g"][i], BF16, name=f"mix_norm{i}", after=after)
    mem_n = rms_fwd(mem, small["mem_norm_g"][i], BF16, name=f"mem_norm{i}")
    kv = mm_nn(mem_n, wkv, F32, name=f"mem_kv{i}")
    proj = mm_nn(h1, win, F32, name=f"in_proj{i}")
    saved = dict(x0=x, h1=h1, mem_n=mem_n, kv=kv, proj=proj)
    if is_a:
        outs, lses = zip(*[attn_fwd(proj, g, name=f"attn_fwd{i}_{g}") for g in range(3)])
        comb, lse = attn_combine(outs, lses, name=f"attn_combine{i}")
        mem_out = mem_fwd(proj, 3 * A_QKV_W, kv, name=f"mem_fwd{i}")
        cat = jnp.concatenate([comb.astype(BF16), mem_out], axis=1)
        saved.update(comb=comb, lse=lse)
    else:
        wout = wout.reshape(1, B_W + MEM_W, D_MODEL)
        tok = sgu_fwd(proj, small["b_v_norm_g"][j], small["b_w_s"][j], small["bias_b"][j], name=f"sgu_fwd{i}")
        mem_out = mem_fwd(proj, 2 * B_W, kv, name=f"mem_fwd{i}")
        cat = jnp.concatenate([tok, mem_out], axis=1)
    x1 = mm_nn(cat, wout, F32, res=x, name=f"out_proj{i}")
    saved.update(cat=cat)
    return x1, saved


def _ffn_fwd(i, x1, w, small, after):
    wup, wdn = w
    h2 = rms_fwd(x1, small["ffn_norm_g"][i], BF16, name=f"ffn_norm{i}", after=after)
    a = mm_nn(h2, wup, F32, name=f"ffn_up{i}")
    act = ffn_act_fwd(a, small["ffn_conv_w"][i], small["ffn_conv_b"][i], name=f"ffn_act{i}")
    x2 = mm_nn(act, wdn.reshape(1, FF, D_MODEL), F32, res=x1, name=f"ffn_down{i}")
    return x2, dict(x1=x1, h2=h2, a=a, act=act)


def _ffn_bwd(i, dx2, w, small, sv, after):
    wup, wdn = w
    sg = {}
    dact = mm_nt(dx2, wdn.reshape(1, FF, D_MODEL), F32, name=f"d_act{i}", after=after)
    d_wdn = mm_tn(sv["act"], dx2, 1, BF16, name=f"d_wdown{i}").reshape(N_CHIPS, FF // N_CHIPS, D_MODEL)
    da, sg["ffn_conv_w"], sg["ffn_conv_b"] = ffn_act_bwd(sv["a"], small["ffn_conv_w"][i], small["ffn_conv_b"][i], dact,
                                                          name=f"ffn_act_bwd{i}")
    d_wup = mm_tn(sv["h2"], da, N_CHIPS, BF16, name=f"d_wup{i}")
    dh2 = mm_nt(da, wup, F32, name=f"d_h2_{i}")
    dx1, sg["ffn_norm_g"] = rms_bwd(sv["x1"], small["ffn_norm_g"][i], dh2, dres=dx2, name=f"ffn_norm_bwd{i}")
    return dx1, [d_wup, d_wdn], sg


def _mixer_bwd(i, dx1, mem, w, small, sv, after):
    is_a = i % 2 == 0
    j = i // 2
    wkv, win, wout = w
    wkv = wkv.reshape(1, D_MODEL, 2 * MEM_W)
    sg = {}
    proj, kv = sv["proj"], sv["kv"]
    if is_a:
        dcat = mm_nt(dx1, wout, F32, name=f"d_cat{i}", after=after)
        d_wout = mm_tn(sv["cat"], dx1, N_CHIPS, BF16, name=f"d_wout{i}")
        dqm, dkv = mem_bwd(proj, 3 * A_QKV_W, kv, dcat, A_OUT_W, name=f"mem_bwd{i}")
        parts = [attn_bwd(proj, dcat, sv["comb"], sv["lse"], g, name=f"attn_bwd{i}_{g}") for g in range(3)]
        dproj = jnp.concatenate([parts[g][p].astype(BF16) for p in range(3) for g in range(3)] + [dqm], axis=1)
    else:
        dcat = mm_nt(dx1, wout.reshape(1, B_W + MEM_W, D_MODEL), F32, name=f"d_cat{i}", after=after)
        d_wout = mm_tn(sv["cat"], dx1, 1, BF16, name=f"d_wout{i}").reshape(N_CHIPS, (B_W + MEM_W) // N_CHIPS, D_MODEL)
        dqm, dkv = mem_bwd(proj, 2 * B_W, kv, dcat, B_W, name=f"mem_bwd{i}")
        w_s = small["b_w_s"][j]
        duv, sg["b_w_s"], dmix, sg["b_v_norm_g"] = sgu_bwd(proj, small["b_v_norm_g"][j], w_s, jnp.swapaxes(w_s, 1, 2),
                                                           small["bias_b"][j], dcat, name=f"sgu_bwd{i}")
        sg["b_s_bias"] = jnp.sum(dmix, axis=-1)
        dproj = jnp.concatenate([duv, dqm], axis=1)
    d_wkv = mm_tn(sv["mem_n"], dkv, 1, BF16, name=f"d_wkv{i}").reshape(N_CHIPS, D_MODEL // N_CHIPS, 2 * MEM_W)
    dmem_n = mm_nt(dkv, wkv, F32, name=f"d_mem_n{i}")
    _, sg["mem_norm_g"] = rms_bwd(mem, small["mem_norm_g"][i], dmem_n, name=f"mem_norm_bwd{i}")
    d_win = mm_tn(sv["h1"], dproj, N_CHIPS, BF16, name=f"d_win{i}")
    dh1 = mm_nt(dproj, win, F32, name=f"d_h1_{i}")
    dx0, sg["mix_norm_g"] = rms_bwd(sv["x0"], small["mix_norm_g"][i], dh1, dres=dx1, name=f"mix_norm_bwd{i}")
    return dx0, [d_wkv, d_win, d_wout], sg


def _exchange_begin(grads, got, place, tag):
    sums = [pair_sum(g, o, place[1:], name=f"pair_sum{tag}_{k}") for k, (g, o) in enumerate(zip(grads, got))]
    send_sems, recv_sems, sums, lands, token = exchange_start(sums, name=f"exchange_start{tag}")
    return (send_sems, recv_sems, sums, lands), token


def _reduce_finish(started, place, after, tag):
    sums, parts = exchange_wait(*started, after, name=f"exchange_wait{tag}")
    halves = [chip_sum(s, p, place, name=f"chip_sum{tag}_{k}") for k, (s, p) in enumerate(zip(sums, parts))]
    return copies_start("join", halves, name=f"join_start_{tag}")


SMALL_SHARDED = ("b_v_norm_g", "ffn_conv_w")
SMALL_FULL_SHAPES = dict(mix_norm_g=(D_MODEL,), ffn_norm_g=(D_MODEL,), mem_norm_g=(D_MODEL,), b_v_norm_g=(B_W,),
                         b_w_s=(B_GROUPS, CHUNK, CHUNK), b_s_bias=(B_GROUPS, CHUNK), ffn_conv_w=(3, 2 * FF),
                         ffn_conv_b=(2 * FF,))
BIG = ("w_mem_kv", "a_w_in", "a_w_out", "b_w_in", "b_w_out", "ffn_w_up", "ffn_w_down")
WEIGHT_ORDER = ("mix_norm_g", "ffn_norm_g", "mem_norm_g", "w_mem_kv", "a_w_in", "a_w_out", "b_w_in", "b_v_norm_g", "b_w_s",
                "b_s_bias", "b_w_out", "ffn_w_up", "ffn_conv_w", "ffn_conv_b", "ffn_w_down", "final_norm_g")


def kernel(x, mem, mix_norm_g, ffn_norm_g, mem_norm_g, w_mem_kv, a_w_in, a_w_out, b_w_in, b_v_norm_g, b_w_s, b_s_bias, b_w_out, ffn_w_up, ffn_conv_w, ffn_conv_b, ffn_w_down, final_norm_g, loss_target, m_mix_norm_g, m_ffn_norm_g, m_mem_norm_g, m_w_mem_kv, m_a_w_in, m_a_w_out, m_b_w_in, m_b_v_norm_g, m_b_w_s, m_b_s_bias, m_b_w_out, m_ffn_w_up, m_ffn_conv_w, m_ffn_conv_b, m_ffn_w_down, m_final_norm_g, v_mix_norm_g, v_ffn_norm_g, v_mem_norm_g, v_w_mem_kv, v_a_w_in, v_a_w_out, v_b_w_in, v_b_v_norm_g, v_b_w_s, v_b_s_bias, v_b_w_out, v_ffn_w_up, v_ffn_conv_w, v_ffn_conv_b, v_ffn_w_down, v_final_norm_g):
    weights = dict(mix_norm_g=mix_norm_g, ffn_norm_g=ffn_norm_g, mem_norm_g=mem_norm_g, w_mem_kv=w_mem_kv, a_w_in=a_w_in,
                   a_w_out=a_w_out, b_w_in=b_w_in, b_v_norm_g=b_v_norm_g, b_w_s=b_w_s, b_s_bias=b_s_bias, b_w_out=b_w_out,
                   ffn_w_up=ffn_w_up, ffn_conv_w=ffn_conv_w, ffn_conv_b=ffn_conv_b, ffn_w_down=ffn_w_down,
                   final_norm_g=final_norm_g)
    mom1 = dict(mix_norm_g=m_mix_norm_g, ffn_norm_g=m_ffn_norm_g, mem_norm_g=m_mem_norm_g, w_mem_kv=m_w_mem_kv,
                a_w_in=m_a_w_in, a_w_out=m_a_w_out, b_w_in=m_b_w_in, b_v_norm_g=m_b_v_norm_g, b_w_s=m_b_w_s,
                b_s_bias=m_b_s_bias, b_w_out=m_b_w_out, ffn_w_up=m_ffn_w_up, ffn_conv_w=m_ffn_conv_w,
                ffn_conv_b=m_ffn_conv_b, ffn_w_down=m_ffn_w_down, final_norm_g=m_final_norm_g)
    mom2 = dict(mix_norm_g=v_mix_norm_g, ffn_norm_g=v_ffn_norm_g, mem_norm_g=v_mem_norm_g, w_mem_kv=v_w_mem_kv,
                a_w_in=v_a_w_in, a_w_out=v_a_w_out, b_w_in=v_b_w_in, b_v_norm_g=v_b_v_norm_g, b_w_s=v_b_w_s,
                b_s_bias=v_b_s_bias, b_w_out=v_b_w_out, ffn_w_up=v_ffn_w_up, ffn_conv_w=v_ffn_conv_w,
                ffn_conv_b=v_ffn_conv_b, ffn_w_down=v_ffn_w_down, final_norm_g=v_final_norm_g)
    chip = 2 * lax.axis_index("x") + lax.axis_index("y")
    place = jnp.stack([chip, lax.axis_index("c")]).astype(jnp.int32)
    x0, mem0, target = x[0], mem[0], loss_target[0]
    depth = DEPTH

    n_cw, n_vg = ffn_conv_w.size, b_v_norm_g.size
    gathered = small_allgather(_pack_rows([ffn_conv_w, b_v_norm_g], 8)).reshape(N_CHIPS, -1)
    conv_w_full = gathered[:, :n_cw].reshape(N_CHIPS, DEPTH, 3, 2 * FF // N_CHIPS).transpose(1, 2, 0, 3).reshape(DEPTH, 3, 2 * FF)
    vgain_full = gathered[:, n_cw:n_cw + n_vg].reshape(N_CHIPS, 2, B_W // N_CHIPS).transpose(1, 0, 2).reshape(2, B_W)
    small = dict(mix_norm_g=mix_norm_g, ffn_norm_g=ffn_norm_g, mem_norm_g=mem_norm_g, b_w_s=b_w_s, ffn_conv_b=ffn_conv_b,
                 ffn_conv_w=conv_w_full, b_v_norm_g=vgain_full,
                 bias_b=jnp.broadcast_to(b_s_bias[..., None], b_s_bias.shape + (CHUNK,)))

    half_layers = 2 * depth
    groups = [(_ffn_weights if b % 2 else _mixer_weights)(b // 2) for b in range(half_layers)]
    tags = [("f" if b % 2 else "m") + str(b // 2) for b in range(half_layers)]

    def start_gather(b, after):
        bufs = [cast_to_slot(weights[n], l, place, name=f"cast_{n}{l}", after=after) for n, l in groups[b]]
        return copies_start("gather", bufs, name=f"gather_start_{tags[b]}")

    def landed(b, after):
        send_sems, recv_sems, bufs, _ = over_ici.pop(b)
        bufs = copies_wait("gather", send_sems, recv_sems, bufs, after, name=f"gather_wait_{tags[b]}")
        return copies_start("forward", bufs, name=f"forward_start_{tags[b]}")

    ahead_from = 4
    over_ici, tie = {}, gathered
    for b in range(3):
        over_ici[b] = start_gather(b, tie)
        tie = over_ici[b][3]
    w_half, saved_half, h = [], [], x0
    for b in range(half_layers):
        behind = tie if b == 0 else h
        if b < ahead_from:
            to_sibling = landed(b, behind)
            behind = to_sibling[3]
        w_half.append(copies_wait("forward", *to_sibling[:3], behind, name=f"forward_wait_{tags[b]}"))
        tokens = []
        if ahead_from <= b + 1 < half_layers:
            to_sibling = landed(b + 1, w_half[b][0])
            tokens.append(to_sibling[3])
        if b + 3 < half_layers:
            over_ici[b + 3] = start_gather(b + 3, w_half[b][0])
            tokens.append(over_ici[b + 3][3])
        tie = sum(tokens[1:], tokens[0]) if tokens else None
        if b % 2 == 0:
            h, sv = _mixer_fwd(b // 2, h, mem0, w_half[b], small, tie)
        else:
            h, sv = _ffn_fwd(b // 2, h, w_half[b], small, tie)
        saved_half.append(sv)
    w_mix, w_ffn, saved_mix, saved_ffn = w_half[0::2], w_half[1::2], saved_half[0::2], saved_half[1::2]
    loss_row, dh, d_final = final_loss(h, final_norm_g, target)
    loss = lax.psum(loss_row[0, 0], ("x", "y", "c"))

    names = [n for n in WEIGHT_ORDER if n not in BIG]
    small_g = {n: [None] * weights[n].shape[0] for n in names if n != "final_norm_g"}
    big_out = {n: None for n in BIG}

    def keep_small(i, sg):
        for n, g in sg.items():
            small_g[n][i if len(small_g[n]) == depth else i // 2] = g.reshape(SMALL_FULL_SHAPES[n])

    joining = []

    def update(after):
        (send_sems, recv_sems, halves, _), group, tag = joining.pop()
        for (n, l), g in zip(group, copies_wait("join", send_sems, recv_sems, halves, after, name=f"join_wait_{tag}")):
            big_out[n] = adamw_layer(weights[n], mom1[n], mom2[n], l, g, big_out[n], name=f"adamw_{n}{l}")

    def finish_reduce(started, group, after, tag):
        join = _reduce_finish(started, place, after, tag)
        if joining:
            update(join[3])
        joining.append((join, group, tag))

    half_layers = 2 * depth
    swapping, exchanging, tie = None, [], None
    for k in range(half_layers):
        i = depth - 1 - k // 2
        if k % 2 == 0:
            dh, big_g, sg = _ffn_bwd(i, dh, w_ffn[i], small, saved_ffn[i], tie)
            group, tag = _ffn_weights(i), f"f{i}"
        else:
            dh, big_g, sg = _mixer_bwd(i, dh, mem0, w_mix[i], small, saved_mix[i], tie)
            group, tag = _mixer_weights(i), f"m{i}"
        keep_small(i, sg)
        started_now, swap_now, tokens = [], None, []
        if k < half_layers - 2:
            *swap_now, token = swap_start(big_g, name=f"swap_start_{tag}")
            swap_now = (swap_now, group, tag)
        else:
            started, token = _exchange_begin(big_g, swap_halves(big_g, name=f"swap_halves_{tag}"), place, tag)
            started_now.append((started, group, tag))
        tokens.append(token)
        if swapping is not None:
            swap_args, old_group, old_tag = swapping
            grads, got = swap_wait(*swap_args, dh, name=f"swap_wait_{old_tag}")
            started, token = _exchange_begin(grads, got, place, old_tag)
            started_now.append((started, old_group, old_tag))
            tokens.append(token)
        tie = sum(tokens[1:], tokens[0])
        for started, old_group, old_tag in exchanging:
            finish_reduce(started, old_group, dh, old_tag)
        swapping, exchanging = swap_now, started_now
    for started, old_group, old_tag in exchanging:
        finish_reduce(started, old_group, big_out["ffn_w_down"][0], old_tag)

    full_g = {n: (d_final.reshape(-1) if n == "final_norm_g" else jnp.stack(small_g[n])) for n in names}
    shapes = [full_g[n].shape for n in names]
    reduced = small_allreduce(_pack_rows([full_g[n] for n in names], 8), joining[0][0][3])
    update(reduced)
    summed = dict(zip(names, _unpack_rows(reduced, shapes)))
    for n in SMALL_SHARDED:
        width = weights[n].shape[-1]
        summed[n] = lax.dynamic_slice_in_dim(summed[n], chip * width, width, axis=summed[n].ndim - 1)
    own_shapes = [weights[n].shape for n in names]
    pack = lambda d: _pack_rows([d[n] for n in names], 128)
    small_out = [_unpack_rows(b, own_shapes) for b in adamw_flat(pack(weights), pack(summed), pack(mom1), pack(mom2))]
    outs = {}
    for k, n in enumerate(names):
        outs[n] = (summed[n], small_out[0][k], small_out[1][k], small_out[2][k])
    outs.update(big_out)
    return (loss, dh[None], *[outs[n][0] for n in WEIGHT_ORDER], *[outs[n][1] for n in WEIGHT_ORDER],
            *[outs[n][2] for n in WEIGHT_ORDER], *[outs[n][3] for n in WEIGHT_ORDER])
```

```python
import functools
import math

import numpy as np
import jax
import jax.numpy as jnp
from jax import lax
from jax.experimental import pallas as pl
from jax.experimental.pallas import tpu as pltpu

F32 = jnp.float32
BF16 = jnp.bfloat16
MESH = pl.DeviceIdType.MESH

D_MODEL = 2048
SEQ = 2048
DEPTH = 4
EPS = 1e-6
NEG = -1e30
HEAD = 128
A_PATTERNS = ((128, 1), (512, 4), (2048, 16))
A_QKV_W = 1536
A_OUT_W = 512
A_IN = 5120
QBLK = 128
N_SIDE = 64
CHUNK = 128
B_GROUPS = 12
B_W = 1536
B_IN = 3584
MEM_LEN = 256
MEM_HEADS = 4
MEM_W = 512
FF = 5632
ADAM_LR, ADAM_B1, ADAM_B2, ADAM_EPS, ADAM_WD, ADAM_STEP = 0.001, 0.9, 0.999, 1e-08, 0.01, 10
N_CHIPS = 4

LANES = 128
V7X_VMEM_LIMIT = 56 * 1024 * 1024


def _cp(*sem):
    return pltpu.CompilerParams(dimension_semantics=sem, vmem_limit_bytes=V7X_VMEM_LIMIT)


def _pick(dim, prefs):
    for p in prefs:
        if dim % p == 0:
            return p
    raise ValueError(f"no tile for {dim} in {prefs}")


def _gelu_parts(x):
    cdf = 0.5 * (1.0 + lax.erf(x * (1.0 / math.sqrt(2.0))))
    pdf = jnp.exp(-0.5 * x * x) * (1.0 / math.sqrt(2.0 * math.pi))
    return x * cdf, cdf + x * pdf


def _gelu(x):
    return 0.5 * x * (1.0 + lax.erf(x * (1.0 / math.sqrt(2.0))))


TM_PREFS = (1024, 512, 256, 128)
TN_PREFS = (1408, 1280, 1024, 896, 512, 256, 128)
TK_PREFS = (2816, 2048, 1408, 1280, 1024, 896, 512, 256, 128)


def _mm_body(nk, dims, has_res):
    def body(*refs):
        if has_res:
            a_ref, b_ref, r_ref, o_ref = refs[:4]
        else:
            a_ref, b_ref, o_ref = refs[:3]
            r_ref = None
        part = lax.dot_general(a_ref[...].astype(BF16), b_ref[...].astype(BF16), dims,
                               preferred_element_type=F32)
        if nk == 1:
            if has_res:
                part = part + r_ref[...]
            o_ref[...] = part.astype(o_ref.dtype)
            return
        acc_ref = refs[-1]
        k = pl.program_id(2)

        @pl.when(k == 0)
        def _():
            acc_ref[...] = part

        @pl.when(k > 0)
        def _():
            acc_ref[...] += part

        @pl.when(k == nk - 1)
        def _():
            tot = acc_ref[...]
            if has_res:
                tot = tot + r_ref[...]
            o_ref[...] = tot.astype(o_ref.dtype)
    return body


def mm_nn(a, w, out_dtype, res=None, name="mm_nn"):
    m, kw = a.shape
    ns_, kw2, nsz = w.shape
    assert kw == kw2
    n = ns_ * nsz
    tm, tn, tk = _pick(m, TM_PREFS), _pick(nsz, TN_PREFS), _pick(kw, TK_PREFS)
    nb, nk = nsz // tn, kw // tk
    in_specs = [pl.BlockSpec((tm, tk), lambda i, j, k: (i, k)),
                pl.BlockSpec((None, tk, tn), lambda i, j, k: (j // nb, k, j % nb))]
    args = [a, w]
    if res is not None:
        in_specs.append(pl.BlockSpec((tm, tn), lambda i, j, k: (i, j)))
        args.append(res)
    return pl.pallas_call(
        _mm_body(nk, (((1,), (0,)), ((), ())), res is not None),
        out_shape=jax.ShapeDtypeStruct((m, n), out_dtype),
        grid=(m // tm, n // tn, nk), in_specs=in_specs,
        out_specs=pl.BlockSpec((tm, tn), lambda i, j, k: (i, j)),
        scratch_shapes=[pltpu.VMEM((tm, tn), F32)] if nk > 1 else [],
        compiler_params=_cp("parallel", "parallel", "arbitrary"), name=name)(*args)


def mm_nt(g, w, out_dtype, name="mm_nt", after=None):
    m, n = g.shape
    ns_, kw, nsz = w.shape
    assert n == ns_ * nsz
    tm, tn, tk = _pick(m, TM_PREFS), _pick(kw, TN_PREFS), _pick(nsz, TK_PREFS)
    nb, nk = nsz // tk, n // tk
    body = _mm_body(nk, (((1,), (1,)), ((), ())), False)
    tied = [] if after is None else [after]
    return pl.pallas_call(
        (lambda g_ref, w_ref, *rest: body(g_ref, w_ref, *rest[len(tied):])),
        out_shape=jax.ShapeDtypeStruct((m, kw), out_dtype),
        grid=(m // tm, kw // tn, nk),
        in_specs=[pl.BlockSpec((tm, tk), lambda i, j, k: (i, k)),
                  pl.BlockSpec((None, tn, tk), lambda i, j, k: (k // nb, j, k % nb))] + [ANY] * len(tied),
        out_specs=pl.BlockSpec((tm, tn), lambda i, j, k: (i, j)),
        scratch_shapes=[pltpu.VMEM((tm, tn), F32)] if nk > 1 else [],
        compiler_params=_cp("parallel", "parallel", "arbitrary"), name=name)(g, w, *tied)


def mm_tn(a, g, n_shards, out_dtype, name="mm_tn"):
    t, kw = a.shape
    t2, n = g.shape
    assert t == t2
    nsz = n // n_shards
    tm, tn, tk = _pick(kw, TM_PREFS), _pick(nsz, TN_PREFS), _pick(t, TK_PREFS)
    nb, nk = nsz // tn, t // tk
    return pl.pallas_call(
        _mm_body(nk, (((0,), (0,)), ((), ())), False),
        out_shape=jax.ShapeDtypeStruct((n_shards, kw, nsz), out_dtype),
        grid=(kw // tm, n // tn, nk),
        in_specs=[pl.BlockSpec((tk, tm), lambda i, j, k: (k, i)),
                  pl.BlockSpec((tk, tn), lambda i, j, k: (k, j))],
        out_specs=pl.BlockSpec((None, tm, tn), lambda i, j, k: (j // nb, i, j % nb)),
        scratch_shapes=[pltpu.VMEM((tm, tn), F32)] if nk > 1 else [],
        compiler_params=_cp("parallel", "parallel", "arbitrary"), name=name)(a, g)


ROW_TILE = 256


def _rms_stats(x):
    r = lax.rsqrt(jnp.mean(x * x, axis=-1, keepdims=True) + EPS)
    return r, x * r


def _rms_back(xh, r, g, dh):
    u = dh * g
    return r * (u - xh * jnp.mean(u * xh, axis=-1, keepdims=True))


def rms_fwd(x, g, out_dtype, name="rms_fwd", after=None):
    rows, d = x.shape
    tr = _pick(rows, (ROW_TILE, 128))
    tied = [] if after is None else [after]

    def body(x_ref, g_ref, *rest):
        o_ref = rest[-1]
        _, xh = _rms_stats(x_ref[...])
        o_ref[...] = (xh * g_ref[...]).astype(o_ref.dtype)

    return pl.pallas_call(
        body, out_shape=jax.ShapeDtypeStruct((rows, d), out_dtype), grid=(rows // tr,),
        in_specs=[pl.BlockSpec((tr, d), lambda i: (i, 0)), pl.BlockSpec((1, d), lambda i: (0, 0))] + [ANY] * len(tied),
        out_specs=pl.BlockSpec((tr, d), lambda i: (i, 0)),
        compiler_params=_cp("parallel"), name=name)(x, g.reshape(1, d), *tied)


def rms_bwd(x, g, dh, dres=None, name="rms_bwd"):
    rows, d = x.shape
    tr = _pick(rows, (ROW_TILE, 128))
    has_res = dres is not None

    def body(*refs):
        if has_res:
            x_ref, g_ref, dh_ref, dres_ref, dx_ref, dx16_ref, dg_ref = refs
        else:
            x_ref, g_ref, dh_ref, dx_ref, dx16_ref, dg_ref = refs
        r, xh = _rms_stats(x_ref[...])
        dh_ = dh_ref[...].astype(F32)
        part = jnp.sum(dh_ * xh, axis=0, keepdims=True)

        @pl.when(pl.program_id(0) == 0)
        def _():
            dg_ref[...] = part

        @pl.when(pl.program_id(0) > 0)
        def _():
            dg_ref[...] += part

        dx = _rms_back(xh, r, g_ref[...], dh_)
        if has_res:
            dx = dx + dres_ref[...]
        dx_ref[...] = dx
        dx16_ref[...] = dx.astype(BF16)

    row_spec = pl.BlockSpec((tr, d), lambda i: (i, 0))
    vec_spec = pl.BlockSpec((1, d), lambda i: (0, 0))
    args = [x, g.reshape(1, d), dh] + ([dres] if has_res else [])
    return pl.pallas_call(
        body, out_shape=(jax.ShapeDtypeStruct((rows, d), F32), jax.ShapeDtypeStruct((rows, d), BF16),
                         jax.ShapeDtypeStruct((1, d), F32)),
        grid=(rows // tr,), in_specs=[row_spec, vec_spec, row_spec] + ([row_spec] if has_res else []),
        out_specs=(row_spec, row_spec, vec_spec), compiler_params=_cp("arbitrary"), name=name)(*args)


def final_loss(x, g, target, name="final_loss"):
    rows, d = x.shape
    tr = _pick(rows, (ROW_TILE, 128))

    def body(x_ref, g_ref, t_ref, loss_ref, dx_ref, dx16_ref, dg_ref):
        r, xh = _rms_stats(x_ref[...])
        gain = g_ref[...]
        err = xh * gain - t_ref[...]
        sq = jnp.sum(jnp.sum(err * err, axis=1, keepdims=True), axis=0, keepdims=True) * (0.5 / d)
        dy = err * (1.0 / d)
        part = jnp.sum(dy * xh, axis=0, keepdims=True)

        @pl.when(pl.program_id(0) == 0)
        def _():
            dg_ref[...] = part
            loss_ref[...] = jnp.broadcast_to(sq, loss_ref.shape)

        @pl.when(pl.program_id(0) > 0)
        def _():
            dg_ref[...] += part
            loss_ref[...] += jnp.broadcast_to(sq, loss_ref.shape)

        dx = _rms_back(xh, r, gain, dy)
        dx_ref[...] = dx
        dx16_ref[...] = dx.astype(BF16)

    row_spec = pl.BlockSpec((tr, d), lambda i: (i, 0))
    vec_spec = pl.BlockSpec((1, d), lambda i: (0, 0))
    return pl.pallas_call(
        body, out_shape=(jax.ShapeDtypeStruct((1, LANES), F32), jax.ShapeDtypeStruct((rows, d), F32),
                         jax.ShapeDtypeStruct((rows, d), BF16), jax.ShapeDtypeStruct((1, d), F32)),
        grid=(rows // tr,), in_specs=[row_spec, vec_spec, row_spec],
        out_specs=(pl.BlockSpec((1, LANES), lambda i: (0, 0)), row_spec, row_spec, vec_spec),
        compiler_params=_cp("arbitrary"), name=name)(x, g.reshape(1, d), target)


def _alibi_slopes():
    return (2.0 ** (-8.0 * (np.arange(12) + 1) / 12)).astype(np.float32)


def _band_scores(q, k, q0, start, wk, slope):
    s = lax.dot_general(q, k, (((1,), (1,)), ((), ())), preferred_element_type=F32) * (HEAD ** -0.5)
    qpos = q0 + lax.broadcasted_iota(jnp.int32, (QBLK, wk), 0)
    kpos = start + lax.broadcasted_iota(jnp.int32, (QBLK, wk), 1)
    rel = jnp.abs(qpos - kpos)
    return jnp.where(rel <= N_SIDE, s - slope * rel.astype(F32), NEG)


def _attn_geometry(seq, dilation):
    length = seq // dilation
    return length, length // QBLK, min(2 * QBLK, length)


def _attn_window(n, length, wk):
    q0 = pl.multiple_of(n * QBLK, QBLK)
    start = pl.multiple_of(jnp.clip(n * QBLK - N_SIDE, 0, length - wk), N_SIDE)
    return q0, start


def _class_in(refs, scratch, r, dilation, length):
    if dilation == 1:
        return refs
    for ref, buf in zip(refs, scratch):
        buf[...] = ref[pl.ds(r, length, stride=dilation), :]
    return scratch


def _class_out(refs, scratch, r, dilation, length):
    if dilation > 1:
        for ref, buf in zip(refs, scratch):
            ref[pl.ds(r, length, stride=dilation), :] = buf[...]


def attn_fwd(proj, group, name):
    seq = proj.shape[0]
    dilation = A_PATTERNS[group][1]
    length, nblk, wk = _attn_geometry(seq, dilation)

    def body(slope_ref, q_ref, k_ref, v_ref, o_ref, lse_ref, *scratch):
        slope = slope_ref[group * 4 + pl.program_id(0)] * float(dilation)
        for r in range(dilation):
            q_c, k_c, v_c = _class_in((q_ref, k_ref, v_ref), scratch[:3], r, dilation, length)
            o_c, lse_c = (o_ref, lse_ref) if dilation == 1 else scratch[3:]

            def blk(n, carry):
                q0, start = _attn_window(n, length, wk)
                q = q_c[pl.ds(q0, QBLK), :].astype(BF16)
                k = k_c[pl.ds(start, wk), :].astype(BF16)
                v = v_c[pl.ds(start, wk), :].astype(BF16)
                s = _band_scores(q, k, q0, start, wk, slope)
                m = jnp.max(s, axis=-1, keepdims=True)
                p = jnp.exp(s - m)
                l = jnp.sum(p, axis=-1, keepdims=True)
                o = jnp.dot(p.astype(BF16), v, preferred_element_type=F32) / l
                o_c[pl.ds(q0, QBLK), :] = o
                lse_c[pl.ds(q0, QBLK), :] = jnp.broadcast_to(m + jnp.log(l), (QBLK, HEAD))
                return carry

            lax.fori_loop(0, nblk, blk, 0)
            _class_out((o_ref, lse_ref), scratch[3:], r, dilation, length)

    def part(p):
        return pl.BlockSpec((seq, HEAD), lambda h: (0, p * 12 + group * 4 + h))

    out_spec = pl.BlockSpec((seq, HEAD), lambda h: (0, h))
    return pl.pallas_call(
        body, out_shape=(jax.ShapeDtypeStruct((seq, A_OUT_W), F32),) * 2, grid=(4,),
        in_specs=[pl.BlockSpec(memory_space=pltpu.SMEM), part(0), part(1), part(2)],
        out_specs=(out_spec, out_spec),
        scratch_shapes=[pltpu.VMEM((length, HEAD), F32)] * (5 if dilation > 1 else 0),
        compiler_params=_cp("parallel"), name=name)(jnp.asarray(_alibi_slopes()), proj, proj, proj)


def attn_combine(os_, lses, name="attn_combine"):
    seq = os_[0].shape[0]
    tr = ROW_TILE

    def body(o0, o1, o2, l0, l1, l2, c_ref, lse_ref):
        a, b, c = l0[...], l1[...], l2[...]
        m = jnp.maximum(jnp.maximum(a, b), c)
        ea, eb, ec = jnp.exp(a - m), jnp.exp(b - m), jnp.exp(c - m)
        den = ea + eb + ec
        c_ref[...] = (ea * o0[...] + eb * o1[...] + ec * o2[...]) / den
        lse_ref[...] = m + jnp.log(den)

    spec = pl.BlockSpec((tr, A_OUT_W), lambda i: (i, 0))
    return pl.pallas_call(
        body, out_shape=(jax.ShapeDtypeStruct((seq, A_OUT_W), F32),) * 2, grid=(seq // tr,),
        in_specs=[spec] * 6, out_specs=(spec, spec), compiler_params=_cp("parallel"), name=name)(*os_, *lses)


def attn_bwd(proj, dcat, comb, lse, group, name):
    seq = proj.shape[0]
    dilation = A_PATTERNS[group][1]
    length, nblk, wk = _attn_geometry(seq, dilation)
    scale = HEAD ** -0.5

    def body(slope_ref, q_ref, k_ref, v_ref, do_ref, c_ref, lse_ref, dq_ref, dk_ref, dv_ref, *scratch):
        slope = slope_ref[group * 4 + pl.program_id(0)] * float(dilation)
        for r in range(dilation):
            q_c, k_c, v_c, do_c, c_c, lse_c = _class_in((q_ref, k_ref, v_ref, do_ref, c_ref, lse_ref), scratch[:6], r,
                                                        dilation, length)
            dq_c, dk_c, dv_c = (dq_ref, dk_ref, dv_ref) if dilation == 1 else scratch[6:]
            dk_c[...] = jnp.zeros_like(dk_c)
            dv_c[...] = jnp.zeros_like(dv_c)

            def blk(n, carry):
                q0, start = _attn_window(n, length, wk)
                rows = pl.ds(q0, QBLK)
                keys = pl.ds(start, wk)
                q = q_c[rows, :].astype(BF16)
                k = k_c[keys, :].astype(BF16)
                v = v_c[keys, :].astype(BF16)
                do = do_c[rows, :]
                s = _band_scores(q, k, q0, start, wk, slope)
                p = jnp.exp(s - lse_c[rows, :][:, :1])
                delta = jnp.sum(do * c_c[rows, :], axis=-1, keepdims=True)
                do16 = do.astype(BF16)
                dp = lax.dot_general(do16, v, (((1,), (1,)), ((), ())), preferred_element_type=F32)
                ds = (p * (dp - delta) * scale).astype(BF16)
                p16 = p.astype(BF16)
                dq_c[rows, :] = jnp.dot(ds, k, preferred_element_type=F32)
                dk_c[keys, :] += lax.dot_general(ds, q, (((0,), (0,)), ((), ())), preferred_element_type=F32)
                dv_c[keys, :] += lax.dot_general(p16, do16, (((0,), (0,)), ((), ())), preferred_element_type=F32)
                return carry

            lax.fori_loop(0, nblk, blk, 0)
            _class_out((dq_ref, dk_ref, dv_ref), scratch[6:], r, dilation, length)

    def part(p):
        return pl.BlockSpec((seq, HEAD), lambda h: (0, p * 12 + group * 4 + h))

    hs = pl.BlockSpec((seq, HEAD), lambda h: (0, h))
    return pl.pallas_call(
        body, out_shape=(jax.ShapeDtypeStruct((seq, A_OUT_W), F32),) * 3, grid=(4,),
        in_specs=[pl.BlockSpec(memory_space=pltpu.SMEM), part(0), part(1), part(2), hs, hs, hs],
        out_specs=(hs, hs, hs),
        scratch_shapes=[pltpu.VMEM((length, HEAD), F32)] * (9 if dilation > 1 else 0),
        compiler_params=_cp("parallel"), name=name,
    )(jnp.asarray(_alibi_slopes()), proj, proj, proj, dcat, comb, lse)


MEM_ROW_TILE = 512


def _mem_probs(q, k):
    s = lax.dot_general(q, k, (((1,), (1,)), ((), ())), preferred_element_type=F32) * (HEAD ** -0.5)
    p = jnp.exp(s - jnp.max(s, axis=-1, keepdims=True))
    return p / jnp.sum(p, axis=-1, keepdims=True)


def mem_fwd(proj, q_col, kv, name="mem_fwd"):
    seq = proj.shape[0]
    qb = q_col // HEAD

    def body(q_ref, k_ref, v_ref, o_ref):
        p = _mem_probs(q_ref[...].astype(BF16), k_ref[...].astype(BF16))
        o_ref[...] = jnp.dot(p.astype(BF16), v_ref[...].astype(BF16), preferred_element_type=F32).astype(o_ref.dtype)

    return pl.pallas_call(
        body, out_shape=jax.ShapeDtypeStruct((seq, MEM_W), BF16), grid=(MEM_HEADS, seq // MEM_ROW_TILE),
        in_specs=[pl.BlockSpec((MEM_ROW_TILE, HEAD), lambda h, i: (i, qb + h)),
                  pl.BlockSpec((MEM_LEN, HEAD), lambda h, i: (0, h)),
                  pl.BlockSpec((MEM_LEN, HEAD), lambda h, i: (0, MEM_HEADS + h))],
        out_specs=pl.BlockSpec((MEM_ROW_TILE, HEAD), lambda h, i: (i, h)),
        compiler_params=_cp("parallel", "parallel"), name=name)(proj, kv, kv)


def mem_bwd(proj, q_col, kv, dcat, do_col, name="mem_bwd"):
    seq = proj.shape[0]
    qb, ob = q_col // HEAD, do_col // HEAD
    scale = HEAD ** -0.5

    def body(q_ref, k_ref, v_ref, do_ref, dq_ref, dk_ref, dv_ref):
        q = q_ref[...].astype(BF16)
        k = k_ref[...].astype(BF16)
        v = v_ref[...].astype(BF16)
        do = do_ref[...].astype(BF16)
        p = _mem_probs(q, k)
        dp = lax.dot_general(do, v, (((1,), (1,)), ((), ())), preferred_element_type=F32)
        ds = (p * (dp - jnp.sum(dp * p, axis=-1, keepdims=True)) * scale).astype(BF16)
        dq_ref[...] = jnp.dot(ds, k, preferred_element_type=F32).astype(dq_ref.dtype)
        dk = lax.dot_general(ds, q, (((0,), (0,)), ((), ())), preferred_element_type=F32)
        dv = lax.dot_general(p.astype(BF16), do, (((0,), (0,)), ((), ())), preferred_element_type=F32)

        @pl.when(pl.program_id(1) == 0)
        def _():
            dk_ref[...] = dk
            dv_ref[...] = dv

        @pl.when(pl.program_id(1) > 0)
        def _():
            dk_ref[...] += dk
            dv_ref[...] += dv

    dq, dk, dv = pl.pallas_call(
        body, out_shape=(jax.ShapeDtypeStruct((seq, MEM_W), BF16), jax.ShapeDtypeStruct((MEM_LEN, MEM_W), F32),
                         jax.ShapeDtypeStruct((MEM_LEN, MEM_W), F32)),
        grid=(MEM_HEADS, seq // MEM_ROW_TILE),
        in_specs=[pl.BlockSpec((MEM_ROW_TILE, HEAD), lambda h, i: (i, qb + h)),
                  pl.BlockSpec((MEM_LEN, HEAD), lambda h, i: (0, h)),
                  pl.BlockSpec((MEM_LEN, HEAD), lambda h, i: (0, MEM_HEADS + h)),
                  pl.BlockSpec((MEM_ROW_TILE, HEAD), lambda h, i: (i, ob + h))],
        out_specs=(pl.BlockSpec((MEM_ROW_TILE, HEAD), lambda h, i: (i, h)),
                   pl.BlockSpec((MEM_LEN, HEAD), lambda h, i: (0, h)),
                   pl.BlockSpec((MEM_LEN, HEAD), lambda h, i: (0, h))),
        compiler_params=_cp("parallel", "arbitrary"), name=name)(proj, kv, kv, dcat)
    return dq, jnp.concatenate([dk, dv], axis=1)


def _sgu_front(x, gain):
    uv, duv = _gelu_parts(x)
    u, v = uv[:, :B_W], uv[:, B_W:]
    r, vh = _rms_stats(v)
    return u, duv, r, vh, vh * gain


def sgu_fwd(proj, gain, w_s, bias_b, name="sgu_fwd"):
    seq = proj.shape[0]

    def body(x_ref, gain_ref, ws_ref, bias_ref, o_ref):
        u, _, _, _, vn = _sgu_front(x_ref[...], gain_ref[...])
        for g in range(B_GROUPS):
            cs = slice(g * CHUNK, (g + 1) * CHUNK)
            mixed = jnp.dot(ws_ref[g].astype(BF16), vn[:, cs].astype(BF16), preferred_element_type=F32) + bias_ref[g]
            o_ref[:, cs] = (u[:, cs] * mixed).astype(o_ref.dtype)

    full = lambda shape: pl.BlockSpec(shape, lambda c: (0,) * len(shape))
    return pl.pallas_call(
        body, out_shape=jax.ShapeDtypeStruct((seq, B_W), BF16), grid=(seq // CHUNK,),
        in_specs=[pl.BlockSpec((CHUNK, 2 * B_W), lambda c: (c, 0)), full((1, B_W)),
                  full((B_GROUPS, CHUNK, CHUNK)), full((B_GROUPS, CHUNK, CHUNK))],
        out_specs=pl.BlockSpec((CHUNK, B_W), lambda c: (c, 0)),
        compiler_params=_cp("parallel"), name=name)(proj, gain.reshape(1, B_W), w_s, bias_b)


def sgu_bwd(proj, gain, w_s, w_s_t, bias_b, dcat, name="sgu_bwd"):
    seq = proj.shape[0]

    def body(x_ref, gain_ref, ws_ref, wst_ref, bias_ref, do_ref, dx_ref, dws_ref, dmix_ref, dgain_ref, dvn_ref):
        first = pl.program_id(0) == 0
        gain = gain_ref[...]
        u, duv, r, vh, vn = _sgu_front(x_ref[...], gain)
        do = do_ref[...]
        for g in range(B_GROUPS):
            cs = slice(g * CHUNK, (g + 1) * CHUNK)
            vg = vn[:, cs].astype(BF16)
            mixed = jnp.dot(ws_ref[g].astype(BF16), vg, preferred_element_type=F32) + bias_ref[g]
            dx_ref[:, cs] = (do[:, cs] * mixed * duv[:, cs]).astype(dx_ref.dtype)
            dmixed = do[:, cs] * u[:, cs]
            dm16 = dmixed.astype(BF16)
            dws = lax.dot_general(dm16, vg, (((1,), (1,)), ((), ())), preferred_element_type=F32)
            dvn_ref[:, cs] = jnp.dot(wst_ref[g].astype(BF16), dm16, preferred_element_type=F32)

            @pl.when(first)
            def _():
                dws_ref[g] = dws
                dmix_ref[g] = dmixed

            @pl.when(jnp.logical_not(first))
            def _():
                dws_ref[g] += dws
                dmix_ref[g] += dmixed

        dvn = dvn_ref[...]
        dgain = jnp.sum(dvn * vh, axis=0, keepdims=True)

        @pl.when(first)
        def _():
            dgain_ref[...] = dgain

        @pl.when(jnp.logical_not(first))
        def _():
            dgain_ref[...] += dgain

        dv = _rms_back(vh, r, gain, dvn)
        dx_ref[:, B_W:] = (dv * duv[:, B_W:]).astype(dx_ref.dtype)

    full = lambda shape: pl.BlockSpec(shape, lambda c: (0,) * len(shape))
    mats = full((B_GROUPS, CHUNK, CHUNK))
    return pl.pallas_call(
        body, out_shape=(jax.ShapeDtypeStruct((seq, 2 * B_W), BF16), jax.ShapeDtypeStruct((B_GROUPS, CHUNK, CHUNK), F32),
                         jax.ShapeDtypeStruct((B_GROUPS, CHUNK, CHUNK), F32), jax.ShapeDtypeStruct((1, B_W), F32)),
        grid=(seq // CHUNK,),
        in_specs=[pl.BlockSpec((CHUNK, 2 * B_W), lambda c: (c, 0)), full((1, B_W)), mats, mats, mats,
                  pl.BlockSpec((CHUNK, B_W), lambda c: (c, 0))],
        out_specs=(pl.BlockSpec((CHUNK, 2 * B_W), lambda c: (c, 0)), mats, mats, full((1, B_W))),
        scratch_shapes=[pltpu.VMEM((CHUNK, B_W), F32)],
        compiler_params=_cp("arbitrary"), name=name)(proj, gain.reshape(1, B_W), w_s, w_s_t, bias_b, dcat)


FFN_COLS = 128
FFN_ROWS = 32
SUBLANES = 8


def _window(ref, r0, first, last):
    cols = ref.shape[1]
    pad = jnp.zeros((SUBLANES, cols), F32)
    if first:
        return jnp.concatenate([pad, ref[pl.ds(0, FFN_ROWS + SUBLANES), :]], axis=0)
    if last:
        return jnp.concatenate([ref[pl.ds(r0 - SUBLANES, FFN_ROWS + SUBLANES), :], pad], axis=0)
    return ref[pl.ds(pl.multiple_of(r0 - SUBLANES, SUBLANES), FFN_ROWS + 2 * SUBLANES), :]


def _taps(win):
    mid = slice(SUBLANES, SUBLANES + FFN_ROWS)
    return pltpu.roll(win, 1, 0)[mid], win[mid], pltpu.roll(win, win.shape[0] - 1, 0)[mid]


def _row_steps(seq, step, carry):
    n = seq // FFN_ROWS
    carry = step(0, True, False, carry)
    carry = lax.fori_loop(1, n - 1, lambda i, c: step(pl.multiple_of(i * FFN_ROWS, FFN_ROWS), False, False, c), carry)
    return step(seq - FFN_ROWS, False, True, carry)


def _conv3(taps, w, b):
    prev, cur, nxt = taps
    return prev * w[0:1] + cur * w[1:2] + nxt * w[2:3] + b


def _fold(x):
    return jnp.sum(x.reshape(FFN_ROWS // SUBLANES, SUBLANES, x.shape[1]), axis=0)


FFN_FWD_COLS = 256


def _taps_whole(a):
    n = a.shape[0]
    rows = lax.broadcasted_iota(jnp.int32, a.shape, 0)
    return (jnp.where(rows == 0, 0.0, pltpu.roll(a, 1, 0)), a, jnp.where(rows == n - 1, 0.0, pltpu.roll(a, n - 1, 0)))


def ffn_act_fwd(a, conv_w, conv_b, name="ffn_act_fwd"):
    seq = a.shape[0]
    nb = FF // FFN_FWD_COLS

    def body(ag_ref, av_ref, wg_ref, wv_ref, bg_ref, bv_ref, o_ref):
        gate = _conv3(_taps_whole(ag_ref[...]), wg_ref[...], bg_ref[...])
        val = _conv3(_taps_whole(av_ref[...]), wv_ref[...], bv_ref[...])
        o_ref[...] = (_gelu(gate) * val).astype(o_ref.dtype)

    col = lambda rows, off: pl.BlockSpec((rows, FFN_FWD_COLS), lambda j: (0, j + off))
    cb = conv_b.reshape(1, 2 * FF)
    return pl.pallas_call(
        body, out_shape=jax.ShapeDtypeStruct((seq, FF), BF16), grid=(nb,),
        in_specs=[col(seq, 0), col(seq, nb), col(3, 0), col(3, nb), col(1, 0), col(1, nb)],
        out_specs=col(seq, 0), compiler_params=_cp("parallel"), name=name)(a, a, conv_w, conv_w, cb, cb)


def ffn_act_bwd(a, conv_w, conv_b, dact, name="ffn_act_bwd"):
    seq = a.shape[0]
    nb = FF // FFN_COLS

    def body(ag_ref, av_ref, wg_ref, wv_ref, bg_ref, bv_ref, d_ref, dag_ref, dav_ref, dwg_ref, dwv_ref, dbg_ref, dbv_ref,
             dcg_ref, dcv_ref):
        wg, wv, bg, bv = wg_ref[...], wv_ref[...], bg_ref[...], bv_ref[...]

        def conv_grads(r0, first, last, sums):
            g_taps = _taps(_window(ag_ref, r0, first, last))
            v_taps = _taps(_window(av_ref, r0, first, last))
            act, dact_dgate = _gelu_parts(_conv3(g_taps, wg, bg))
            d = d_ref[pl.ds(r0, FFN_ROWS), :].astype(F32)
            dcg = d * _conv3(v_taps, wv, bv) * dact_dgate
            dcv = d * act
            dcg_ref[pl.ds(r0, FFN_ROWS), :] = dcg
            dcv_ref[pl.ds(r0, FFN_ROWS), :] = dcv
            new = [_fold(dcg)] + [_fold(dcg * t) for t in g_taps] + [_fold(dcv)] + [_fold(dcv * t) for t in v_taps]
            return tuple(s + n for s, n in zip(sums, new))

        zero = jnp.zeros((SUBLANES, FFN_COLS), F32)
        sums = _row_steps(seq, conv_grads, (zero,) * 8)
        total = [jnp.sum(s, axis=0, keepdims=True) for s in sums]
        dbg_ref[...] = total[0]
        dbv_ref[...] = total[4]
        for k in range(3):
            dwg_ref[k:k + 1, :] = total[1 + k]
            dwv_ref[k:k + 1, :] = total[5 + k]

        def conv_transpose(r0, first, last, carry):
            for dc_ref, w, da_ref in ((dcg_ref, wg, dag_ref), (dcv_ref, wv, dav_ref)):
                prev, cur, nxt = _taps(_window(dc_ref, r0, first, last))
                da_ref[pl.ds(r0, FFN_ROWS), :] = (nxt * w[0:1] + cur * w[1:2] + prev * w[2:3]).astype(da_ref.dtype)
            return carry

        _row_steps(seq, conv_transpose, 0)

    col = lambda rows, off: pl.BlockSpec((rows, FFN_COLS), lambda j: (0, j + off))
    cb = conv_b.reshape(1, 2 * FF)
    dag, dav, dwg, dwv, dbg, dbv = pl.pallas_call(
        body, out_shape=(jax.ShapeDtypeStruct((seq, FF), BF16),) * 2 + (jax.ShapeDtypeStruct((3, FF), F32),) * 2
        + (jax.ShapeDtypeStruct((1, FF), F32),) * 2, grid=(nb,),
        in_specs=[col(seq, 0), col(seq, nb), col(3, 0), col(3, nb), col(1, 0), col(1, nb), col(seq, 0)],
        out_specs=(col(seq, 0), col(seq, 0), col(3, 0), col(3, 0), col(1, 0), col(1, 0)),
        scratch_shapes=[pltpu.VMEM((seq, FFN_COLS), F32), pltpu.VMEM((seq, FFN_COLS), F32)],
        compiler_params=_cp("parallel"), name=name)(a, a, conv_w, conv_w, cb, cb, dact)
    cat = lambda p, q: jnp.concatenate([p, q], axis=1)
    return cat(dag, dav), cat(dwg, dwv), cat(dbg, dbv)


def _adam_math(w, g, m, v):
    m = ADAM_B1 * m + (1.0 - ADAM_B1) * g
    v = ADAM_B2 * v + (1.0 - ADAM_B2) * (g * g)
    m_hat = m / (1.0 - ADAM_B1 ** ADAM_STEP)
    v_hat = v / (1.0 - ADAM_B2 ** ADAM_STEP)
    return -ADAM_LR * (m_hat / (jnp.sqrt(v_hat) + ADAM_EPS) + ADAM_WD * w), m, v


def _row_tile(rows, cols):
    return _pick(rows, (256, 128, 64)) if cols <= 1024 else _pick(rows, (128, 64))


BF16_ROWS = 16
STREAM_BLOCK_BYTES = 3 * 1024 * 1024


def _stream_rows(rows, cols, itemsize):
    fits = [r for r in range(BF16_ROWS, rows + 1, BF16_ROWS) if rows % r == 0 and r * cols * itemsize <= STREAM_BLOCK_BYTES]
    return max(fits)


def adamw_layer(w_all, m_all, v_all, layer, g, prev, name):
    n, rows, cols = w_all.shape
    tr = _row_tile(rows, cols)

    def body(w_ref, m_ref, v_ref, g_ref, *rest):
        go_ref, d_ref, mo_ref, vo_ref = rest[-4:]
        g_ = g_ref[...]
        d, m_, v_ = _adam_math(w_ref[...], g_, m_ref[...], v_ref[...])
        go_ref[...] = g_
        d_ref[...] = d
        mo_ref[...] = m_
        vo_ref[...] = v_

    lay = pl.BlockSpec((None, tr, cols), lambda i: (layer, i, 0))
    in_specs = [lay, lay, lay, pl.BlockSpec((tr, cols), lambda i: (i, 0))]
    args = [w_all, m_all, v_all, g]
    aliases = {}
    if prev is not None:
        in_specs += [pl.BlockSpec(memory_space=pl.ANY)] * 4
        args += list(prev)
        aliases = {4 + k: k for k in range(4)}
    return pl.pallas_call(
        body, out_shape=(jax.ShapeDtypeStruct(w_all.shape, F32),) * 4, grid=(rows // tr,),
        in_specs=in_specs, out_specs=(lay,) * 4, input_output_aliases=aliases,
        compiler_params=_cp("parallel"), name=name)(*args)


def adamw_flat(w, g, m, v, name="adamw_small"):
    rows, cols = w.shape
    tr = _pick(rows, (128, 8))

    def body(w_ref, g_ref, m_ref, v_ref, d_ref, mo_ref, vo_ref):
        d_ref[...], mo_ref[...], vo_ref[...] = _adam_math(w_ref[...], g_ref[...], m_ref[...], v_ref[...])

    spec = pl.BlockSpec((tr, cols), lambda i: (i, 0))
    return pl.pallas_call(
        body, out_shape=(jax.ShapeDtypeStruct(w.shape, F32),) * 3, grid=(rows // tr,),
        in_specs=[spec] * 4, out_specs=(spec,) * 3, compiler_params=_cp("parallel"), name=name)(w, g, m, v)


def pair_sum(dw, got, core, name):
    _, rows, cols = dw.shape
    half = rows // 2
    tr = _stream_rows(half, cols, 2)
    nrb = half // tr

    def body(c_ref, a_ref, b_ref, o_ref):
        o_ref[...] = (a_ref[...].astype(F32) + b_ref[...].astype(F32)).astype(o_ref.dtype)

    return pl.pallas_call(
        body, out_shape=jax.ShapeDtypeStruct((N_CHIPS, half, cols), BF16),
        grid_spec=pltpu.PrefetchScalarGridSpec(
            num_scalar_prefetch=1, grid=(N_CHIPS, nrb),
            in_specs=[pl.BlockSpec((None, tr, cols), lambda s, i, c_ref: (s, c_ref[0] * nrb + i, 0)),
                      pl.BlockSpec((None, tr, cols), lambda s, i, c_ref: (s, i, 0))],
            out_specs=pl.BlockSpec((None, tr, cols), lambda s, i, c_ref: (s, i, 0))),
        compiler_params=_cp("parallel", "parallel"), name=name)(core, dw, got)


def chip_sum(own, parts, place, name):
    _, half, cols = parts.shape
    tr = _stream_rows(half, cols, 4)
    nrb = half // tr

    def body(p_ref, own_ref, a_ref, b_ref, c_ref, o_ref):
        o_ref[...] = ((own_ref[...].astype(F32) + a_ref[...].astype(F32)) + b_ref[...].astype(F32)) + c_ref[...].astype(F32)

    def slot(k):
        return pl.BlockSpec((None, tr, cols), lambda i, p: (jnp.bitwise_xor(p[0], k), i, 0))

    return pl.pallas_call(
        body, out_shape=jax.ShapeDtypeStruct((2 * half, cols), F32),
        grid_spec=pltpu.PrefetchScalarGridSpec(
            num_scalar_prefetch=1, grid=(nrb,), in_specs=[slot(0), slot(1), slot(2), slot(3)],
            out_specs=pl.BlockSpec((tr, cols), lambda i, p: (p[1] * nrb + i, 0))),
        compiler_params=_cp("parallel"), name=name)(place, own, parts, parts, parts)


def cast_to_slot(w_all, layer, place, name, after=None):
    _, rows, cols = w_all.shape
    tr = _stream_rows(rows, cols, 4)
    tied = [] if after is None else [after]

    def body(p_ref, w_ref, *rest):
        o_ref = rest[-1]
        o_ref[...] = w_ref[...].astype(o_ref.dtype)

    return pl.pallas_call(
        body, out_shape=jax.ShapeDtypeStruct((N_CHIPS, rows, cols), BF16),
        grid_spec=pltpu.PrefetchScalarGridSpec(
            num_scalar_prefetch=1, grid=(rows // tr,),
            in_specs=[pl.BlockSpec((None, tr, cols), lambda i, p: (layer, i, 0))] + [ANY] * len(tied),
            out_specs=pl.BlockSpec((None, tr, cols), lambda i, p: (p[0], i, 0))),
        compiler_params=_cp("parallel"), name=name)(place, w_all, *tied)


ANY = pl.BlockSpec(memory_space=pl.ANY)


def _place():
    x, y, c = lax.axis_index("x"), lax.axis_index("y"), lax.axis_index("c")
    others = [(1 - x, y), (x, 1 - y), (1 - x, 1 - y)]
    return x, y, c, 2 * x + y, others


def _remote(src, dst, send_sem, recv_sem, dev):
    return pltpu.make_async_remote_copy(src_ref=src, dst_ref=dst, send_sem=send_sem, recv_sem=recv_sem,
                                        device_id=dev, device_id_type=MESH)


HBM = pl.BlockSpec(memory_space=pltpu.HBM)
SEM = pl.BlockSpec(memory_space=pltpu.SEMAPHORE)
EFFECT = pltpu.SideEffectType.DATAFLOW_SIDE_EFFECTING
TOKEN = jax.ShapeDtypeStruct((8, LANES), F32)


def _in_hbm(a):
    return pltpu.with_memory_space_constraint(a, pltpu.HBM)


def _gather_copies(bufs, send_sems, recv_sems):
    x, y, c, me, others = _place()
    out = []
    for w, buf in enumerate(bufs):
        half = buf.shape[1] // 2
        mine = pl.ds(c * half, half)
        for k, (ox, oy) in enumerate(others):
            sems = send_sems.at[3 * w + k], recv_sems.at[3 * w + k]
            out.append((_remote(buf.at[me, mine], buf.at[me, mine], *sems, (ox, oy, c)),
                        _remote(buf.at[me, mine], buf.at[2 * ox + oy, mine], *sems, (ox, oy, c))))
    return out


def _forward_copies(bufs, send_sems, recv_sems):
    x, y, c, me, others = _place()
    out = []
    for w, buf in enumerate(bufs):
        half = buf.shape[1] // 2
        mine, theirs = pl.ds(c * half, half), pl.ds((1 - c) * half, half)
        for k, (ox, oy) in enumerate(others):
            sems = send_sems.at[3 * w + k], recv_sems.at[3 * w + k]
            slot = 2 * ox + oy
            out.append((_remote(buf.at[slot, mine], buf.at[slot, mine], *sems, (x, y, 1 - c)),
                        _remote(buf.at[slot, mine], buf.at[slot, theirs], *sems, (x, y, 1 - c))))
    return out


def _join_copies(grads, send_sems, recv_sems):
    x, y, c, _, _ = _place()
    out = []
    for w, g in enumerate(grads):
        half = g.shape[0] // 2
        mine, theirs = pl.ds(c * half, half), pl.ds((1 - c) * half, half)
        sems = send_sems.at[w], recv_sems.at[w]
        out.append((_remote(g.at[mine], g.at[mine], *sems, (x, y, 1 - c)), _remote(g.at[mine], g.at[theirs], *sems, (x, y, 1 - c))))
    return out


def _slot_copies(bufs, send_sems, recv_sems):
    x, y, c, me, others = _place()
    out = []
    for w, buf in enumerate(bufs):
        for k, (ox, oy) in enumerate(others):
            sems = send_sems.at[3 * w + k], recv_sems.at[3 * w + k]
            out.append((_remote(buf.at[me], buf.at[me], *sems, (ox, oy, c)),
                        _remote(buf.at[me], buf.at[2 * ox + oy], *sems, (ox, oy, c))))
    return out


IN_PLACE = dict(gather=(_gather_copies, 3), forward=(_forward_copies, 3), join=(_join_copies, 1), slots=(_slot_copies, 3))


def copies_start(kind, bufs, name):
    n = len(bufs)
    copies, per_buf = IN_PLACE[kind]

    def body(*refs):
        ins, (send_sems, recv_sems), token = refs[:n], refs[n:n + 2], refs[-1]
        for sent, _ in copies(ins, send_sems, recv_sems):
            sent.start()
        token[...] = jnp.zeros_like(token)

    outs = pl.pallas_call(
        body, name=name,
        out_shape=(pltpu.SemaphoreType.DMA((per_buf * n,)), pltpu.SemaphoreType.DMA((per_buf * n,)),
                   *[pltpu.HBM(b.shape, b.dtype) for b in bufs], TOKEN),
        in_specs=[HBM] * n, out_specs=(SEM, SEM, *[HBM] * n, VM),
        input_output_aliases={w: 2 + w for w in range(n)},
        compiler_params=pltpu.CompilerParams(has_side_effects=EFFECT))(*[_in_hbm(b) for b in bufs])
    return outs[0], outs[1], list(outs[2:2 + n]), outs[-1]


def copies_wait(kind, send_sems, recv_sems, bufs, after, name):
    n = len(bufs)
    copies, _ = IN_PLACE[kind]

    def body(*refs):
        ins, (send_ref, recv_ref) = refs[:n], refs[n:n + 2]
        for sent, landed in copies(ins, send_ref, recv_ref):
            sent.wait_send()
            landed.wait_recv()

    return list(pl.pallas_call(
        body, name=name, out_shape=tuple(pltpu.HBM(b.shape, b.dtype) for b in bufs),
        in_specs=[HBM] * n + [SEM, SEM, ANY], out_specs=(HBM,) * n,
        input_output_aliases={w: w for w in range(n)},
        compiler_params=pltpu.CompilerParams(has_side_effects=EFFECT))(*bufs, send_sems, recv_sems, after))


def swap_halves(grads, name):
    n = len(grads)

    def body(*refs):
        ins, outs = refs[:n], refs[n:2 * n]
        send_sems, recv_sems = refs[2 * n:]
        x, y, c, _, _ = _place()
        copies = []
        for w in range(n):
            half = ins[w].shape[1] // 2
            cp = _remote(ins[w].at[:, pl.ds((1 - c) * half, half)], outs[w], send_sems.at[w], recv_sems.at[w], (x, y, 1 - c))
            cp.start()
            copies.append(cp)
        for cp in copies:
            cp.wait()

    return pl.pallas_call(
        body, out_shape=tuple(jax.ShapeDtypeStruct((N_CHIPS, g.shape[1] // 2, g.shape[2]), g.dtype) for g in grads),
        in_specs=[ANY] * n, out_specs=(ANY,) * n,
        scratch_shapes=[pltpu.SemaphoreType.DMA((n,)), pltpu.SemaphoreType.DMA((n,))], name=name)(*grads)


def _swap_copies(grads, lands, send_sems, recv_sems):
    x, y, c, _, _ = _place()
    out = []
    for w, (g, land) in enumerate(zip(grads, lands)):
        half = g.shape[1] // 2
        out.append(_remote(g.at[:, pl.ds((1 - c) * half, half)], land, send_sems.at[w], recv_sems.at[w], (x, y, 1 - c)))
    return out


def swap_start(grads, name):
    n = len(grads)

    def body(*refs):
        ins, lands, (send_sems, recv_sems), token = refs[:n], refs[n:2 * n], refs[2 * n:2 * n + 2], refs[-1]
        for cp in _swap_copies(ins, lands, send_sems, recv_sems):
            cp.start()
        token[...] = jnp.zeros_like(token)

    shapes = [(N_CHIPS, g.shape[1] // 2, g.shape[2]) for g in grads]
    zones = [_in_hbm(lax.empty(s, g.dtype)) for s, g in zip(shapes, grads)]
    outs = pl.pallas_call(
        body, name=name,
        out_shape=(pltpu.SemaphoreType.DMA((n,)), pltpu.SemaphoreType.DMA((n,)),
                   *[pltpu.HBM(g.shape, g.dtype) for g in grads], *[pltpu.HBM(s, g.dtype) for s, g in zip(shapes, grads)],
                   TOKEN),
        in_specs=[HBM] * (2 * n), out_specs=(SEM, SEM, *[HBM] * (2 * n), VM),
        input_output_aliases={w: 2 + w for w in range(2 * n)},
        compiler_params=pltpu.CompilerParams(has_side_effects=EFFECT))(*[_in_hbm(g) for g in grads], *zones)
    return outs[0], outs[1], list(outs[2:2 + n]), list(outs[2 + n:2 + 2 * n]), outs[-1]


def swap_wait(send_sems, recv_sems, grads, lands, after, name):
    n = len(grads)

    def body(*refs):
        ins, zones, (send_ref, recv_ref) = refs[:n], refs[n:2 * n], refs[2 * n:2 * n + 2]
        for cp in _swap_copies(ins, zones, send_ref, recv_ref):
            cp.wait_send()
            cp.wait_recv()

    outs = pl.pallas_call(
        body, name=name, out_shape=tuple(pltpu.HBM(a.shape, a.dtype) for a in list(grads) + list(lands)),
        in_specs=[HBM] * (2 * n) + [SEM, SEM, ANY], out_specs=(HBM,) * (2 * n),
        input_output_aliases={w: w for w in range(2 * n)},
        compiler_params=pltpu.CompilerParams(has_side_effects=EFFECT))(*grads, *lands, send_sems, recv_sems, after)
    return list(outs[:n]), list(outs[n:])


def _exchange_copies(sums, lands, send_sems, recv_sems):
    x, y, c, me, others = _place()
    out = []
    for w, (src, land) in enumerate(zip(sums, lands)):
        for k, (ox, oy) in enumerate(others):
            sems = send_sems.at[3 * w + k], recv_sems.at[3 * w + k]
            out.append((_remote(src.at[2 * ox + oy], land.at[me], *sems, (ox, oy, c)),
                        _remote(src.at[2 * ox + oy], land.at[2 * ox + oy], *sems, (ox, oy, c))))
    return out


def exchange_start(sums, name):
    n = len(sums)

    def body(*refs):
        ins, lands, (send_sems, recv_sems), token = refs[:n], refs[n:2 * n], refs[2 * n:2 * n + 2], refs[-1]
        for sent, _ in _exchange_copies(ins, lands, send_sems, recv_sems):
            sent.start()
        token[...] = jnp.zeros_like(token)

    zones = [_in_hbm(lax.empty(s.shape, s.dtype)) for s in sums]
    outs = pl.pallas_call(
        body, name=name,
        out_shape=(pltpu.SemaphoreType.DMA((3 * n,)), pltpu.SemaphoreType.DMA((3 * n,)),
                   *[pltpu.HBM(s.shape, s.dtype) for s in sums] * 2, TOKEN),
        in_specs=[HBM] * (2 * n), out_specs=(SEM, SEM, *[HBM] * (2 * n), VM),
        input_output_aliases={w: 2 + w for w in range(2 * n)},
        compiler_params=pltpu.CompilerParams(has_side_effects=EFFECT))(*[_in_hbm(s) for s in sums], *zones)
    return outs[0], outs[1], list(outs[2:2 + n]), list(outs[2 + n:2 + 2 * n]), outs[-1]


def exchange_wait(send_sems, recv_sems, sums, lands, after, name):
    n = len(sums)

    def body(*refs):
        ins, zones, (send_ref, recv_ref) = refs[:n], refs[n:2 * n], refs[2 * n:2 * n + 2]
        for sent, landed in _exchange_copies(ins, zones, send_ref, recv_ref):
            sent.wait_send()
            landed.wait_recv()

    outs = pl.pallas_call(
        body, name=name, out_shape=tuple(pltpu.HBM(s.shape, s.dtype) for s in sums) * 2,
        in_specs=[HBM] * (2 * n) + [SEM, SEM, ANY], out_specs=(HBM,) * (2 * n),
        input_output_aliases={w: w for w in range(2 * n)},
        compiler_params=pltpu.CompilerParams(has_side_effects=EFFECT))(*sums, *lands, send_sems, recv_sems, after)
    return list(outs[:n]), list(outs[n:])


VM = pl.BlockSpec(memory_space=pltpu.VMEM)


def small_allgather(buf, name="small_allgather"):
    def body(in_ref, out_ref, send_sems, recv_sems):
        x, y, c, me, others = _place()
        out_ref[me] = in_ref[...]
        copies = []
        for k, (ox, oy) in enumerate(others):
            cp = _remote(in_ref, out_ref.at[me], send_sems.at[k], recv_sems.at[k], (ox, oy, c))
            cp.start()
            copies.append(cp)
        for k, (ox, oy) in enumerate(others):
            landed = out_ref.at[2 * ox + oy]
            _remote(landed, landed, send_sems.at[k], recv_sems.at[k], (ox, oy, c)).wait_recv()
        for cp in copies:
            cp.wait_send()

    return pl.pallas_call(
        body, out_shape=jax.ShapeDtypeStruct((N_CHIPS,) + buf.shape, buf.dtype), in_specs=[VM], out_specs=VM,
        scratch_shapes=[pltpu.SemaphoreType.DMA((3,)), pltpu.SemaphoreType.DMA((3,))],
        compiler_params=pltpu.CompilerParams(vmem_limit_bytes=V7X_VMEM_LIMIT), name=name)(buf)


def pair_small(buf, after, name="pair_small"):
    def body(in_ref, after_ref, out_ref, sib_ref, send_sem, recv_sem):
        x, y, c, me, _ = _place()
        cp = _remote(in_ref, sib_ref, send_sem, recv_sem, (x, y, 1 - c))
        cp.start()
        cp.wait()
        out_ref[me] = in_ref[...] + sib_ref[...]

    return pl.pallas_call(
        body, out_shape=jax.ShapeDtypeStruct((N_CHIPS,) + buf.shape, buf.dtype), in_specs=[VM, ANY], out_specs=VM,
        scratch_shapes=[pltpu.VMEM(buf.shape, buf.dtype), pltpu.SemaphoreType.DMA, pltpu.SemaphoreType.DMA],
        compiler_params=pltpu.CompilerParams(vmem_limit_bytes=V7X_VMEM_LIMIT), name=name)(buf, after)


def sum_slots(slots, name="sum_slots"):
    _, rows, cols = slots.shape
    tr = _pick(rows, (128, 8))

    def body(s_ref, o_ref):
        o_ref[...] = ((s_ref[0] + s_ref[1]) + s_ref[2]) + s_ref[3]

    return pl.pallas_call(
        body, out_shape=jax.ShapeDtypeStruct((rows, cols), slots.dtype), grid=(rows // tr,),
        in_specs=[pl.BlockSpec((N_CHIPS, tr, cols), lambda i: (0, i, 0))],
        out_specs=pl.BlockSpec((tr, cols), lambda i: (i, 0)), compiler_params=_cp("parallel"), name=name)(slots)


def _pack_rows(arrays, row_multiple):
    flat = jnp.concatenate([a.reshape(-1) for a in arrays])
    rows = -(-flat.shape[0] // (LANES * row_multiple)) * row_multiple
    return jnp.pad(flat, (0, rows * LANES - flat.shape[0])).reshape(rows, LANES)


def _unpack_rows(buf, shapes):
    flat = buf.reshape(-1)
    out, at = [], 0
    for s in shapes:
        n = math.prod(s)
        out.append(flat[at:at + n].reshape(s))
        at += n
    return out


def _mixer_weights(i):
    j = i // 2
    mixer = "a" if i % 2 == 0 else "b"
    return [("w_mem_kv", i), (mixer + "_w_in", j), (mixer + "_w_out", j)]


def _ffn_weights(i):
    return [("ffn_w_up", i), ("ffn_w_down", i)]


def _mixer_fwd(i, x, mem, w, small, after):
    is_a = i % 2 == 0
    j = i // 2
    wkv, win, wout = w
    wkv = wkv.reshape(1, D_MODEL, 2 * MEM_W)
    h1 = rms_fwd(x, small["mix_norm_g"][i], BF16, name=f"mix_norm{i}", after=after)
    mem_n = rms_fwd(mem, small["mem_norm_g"][i], BF16, name=f"mem_norm{i}")
    kv = mm_nn(mem_n, wkv, F32, name=f"mem_kv{i}")
    proj = mm_nn(h1, win, F32, name=f"in_proj{i}")
    saved = dict(x0=x, h1=h1, mem_n=mem_n, kv=kv, proj=proj)
    if is_a:
        outs, lses = zip(*[attn_fwd(proj, g, name=f"attn_fwd{i}_{g}") for g in range(3)])
        comb, lse = attn_combine(outs, lses, name=f"attn_combine{i}")
        mem_out = mem_fwd(proj, 3 * A_QKV_W, kv, name=f"mem_fwd{i}")
        cat = jnp.concatenate([comb.astype(BF16), mem_out], axis=1)
        saved.update(comb=comb, lse=lse)
    else:
        wout = wout.reshape(1, B_W + MEM_W, D_MODEL)
        tok = sgu_fwd(proj, small["b_v_norm_g"][j], small["b_w_s"][j], small["bias_b"][j], name=f"sgu_fwd{i}")
        mem_out = mem_fwd(proj, 2 * B_W, kv, name=f"mem_fwd{i}")
        cat = jnp.concatenate([tok, mem_out], axis=1)
    x1 = mm_nn(cat, wout, F32, res=x, name=f"out_proj{i}")
    saved.update(cat=cat)
    return x1, saved


def _ffn_fwd(i, x1, w, small, after):
    wup, wdn = w
    h2 = rms_fwd(x1, small["ffn_norm_g"][i], BF16, name=f"ffn_norm{i}", after=after)
    a = mm_nn(h2, wup, F32, name=f"ffn_up{i}")
    act = ffn_act_fwd(a, small["ffn_conv_w"][i], small["ffn_conv_b"][i], name=f"ffn_act{i}")
    x2 = mm_nn(act, wdn.reshape(1, FF, D_MODEL), F32, res=x1, name=f"ffn_down{i}")
    return x2, dict(x1=x1, h2=h2, a=a, act=act)


def _ffn_bwd(i, dx2, w, small, sv, after):
    wup, wdn = w
    dx2, dx2_16 = dx2
    sg = {}
    dact = mm_nt(dx2_16, wdn.reshape(1, FF, D_MODEL), F32, name=f"d_act{i}", after=after)
    d_wdn = mm_tn(sv["act"], dx2_16, 1, BF16, name=f"d_wdown{i}").reshape(N_CHIPS, FF // N_CHIPS, D_MODEL)
    da, sg["ffn_conv_w"], sg["ffn_conv_b"] = ffn_act_bwd(sv["a"], small["ffn_conv_w"][i], small["ffn_conv_b"][i], dact,
                                                          name=f"ffn_act_bwd{i}")
    d_wup = mm_tn(sv["h2"], da, N_CHIPS, BF16, name=f"d_wup{i}")
    dh2 = mm_nt(da, wup, F32, name=f"d_h2_{i}")
    dx1, dx1_16, sg["ffn_norm_g"] = rms_bwd(sv["x1"], small["ffn_norm_g"][i], dh2, dres=dx2, name=f"ffn_norm_bwd{i}")
    return (dx1, dx1_16), [d_wup, d_wdn], sg


def _mixer_bwd(i, dx1, mem, w, small, sv, after):
    is_a = i % 2 == 0
    j = i // 2
    wkv, win, wout = w
    wkv = wkv.reshape(1, D_MODEL, 2 * MEM_W)
    dx1_32, dx1 = dx1
    sg = {}
    proj, kv = sv["proj"], sv["kv"]
    if is_a:
        dcat = mm_nt(dx1, wout, F32, name=f"d_cat{i}", after=after)
        d_wout = mm_tn(sv["cat"], dx1, N_CHIPS, BF16, name=f"d_wout{i}")
        dqm, dkv = mem_bwd(proj, 3 * A_QKV_W, kv, dcat, A_OUT_W, name=f"mem_bwd{i}")
        parts = [attn_bwd(proj, dcat, sv["comb"], sv["lse"], g, name=f"attn_bwd{i}_{g}") for g in range(3)]
        dproj = jnp.concatenate([parts[g][p].astype(BF16) for p in range(3) for g in range(3)] + [dqm], axis=1)
    else:
        dcat = mm_nt(dx1, wout.reshape(1, B_W + MEM_W, D_MODEL), F32, name=f"d_cat{i}", after=after)
        d_wout = mm_tn(sv["cat"], dx1, 1, BF16, name=f"d_wout{i}").reshape(N_CHIPS, (B_W + MEM_W) // N_CHIPS, D_MODEL)
        dqm, dkv = mem_bwd(proj, 2 * B_W, kv, dcat, B_W, name=f"mem_bwd{i}")
        w_s = small["b_w_s"][j]
        duv, sg["b_w_s"], dmix, sg["b_v_norm_g"] = sgu_bwd(proj, small["b_v_norm_g"][j], w_s, jnp.swapaxes(w_s, 1, 2),
                                                           small["bias_b"][j], dcat, name=f"sgu_bwd{i}")
        sg["b_s_bias"] = jnp.sum(dmix, axis=-1)
        dproj = jnp.concatenate([duv, dqm], axis=1)
    d_wkv = mm_tn(sv["mem_n"], dkv, 1, BF16, name=f"d_wkv{i}").reshape(N_CHIPS, D_MODEL // N_CHIPS, 2 * MEM_W)
    dmem_n = mm_nt(dkv, wkv, F32, name=f"d_mem_n{i}")
    _, _, sg["mem_norm_g"] = rms_bwd(mem, small["mem_norm_g"][i], dmem_n, name=f"mem_norm_bwd{i}")
    d_win = mm_tn(sv["h1"], dproj, N_CHIPS, BF16, name=f"d_win{i}")
    dh1 = mm_nt(dproj, win, F32, name=f"d_h1_{i}")
    dx0, dx0_16, sg["mix_norm_g"] = rms_bwd(sv["x0"], small["mix_norm_g"][i], dh1, dres=dx1_32, name=f"mix_norm_bwd{i}")
    return (dx0, dx0_16), [d_wkv, d_win, d_wout], sg


def _exchange_begin(grads, got, place, tag):
    sums = [pair_sum(g, o, place[1:], name=f"pair_sum{tag}_{k}") for k, (g, o) in enumerate(zip(grads, got))]
    send_sems, recv_sems, sums, lands, token = exchange_start(sums, name=f"exchange_start{tag}")
    return (send_sems, recv_sems, sums, lands), token


def _reduce_finish(started, place, after, tag):
    sums, parts = exchange_wait(*started, after, name=f"exchange_wait{tag}")
    halves = [chip_sum(s, p, place, name=f"chip_sum{tag}_{k}") for k, (s, p) in enumerate(zip(sums, parts))]
    return copies_start("join", halves, name=f"join_start_{tag}")


SMALL_SHARDED = ("b_v_norm_g", "ffn_conv_w")
SMALL_FULL_SHAPES = dict(mix_norm_g=(D_MODEL,), ffn_norm_g=(D_MODEL,), mem_norm_g=(D_MODEL,), b_v_norm_g=(B_W,),
                         b_w_s=(B_GROUPS, CHUNK, CHUNK), b_s_bias=(B_GROUPS, CHUNK), ffn_conv_w=(3, 2 * FF),
                         ffn_conv_b=(2 * FF,))
BIG = ("w_mem_kv", "a_w_in", "a_w_out", "b_w_in", "b_w_out", "ffn_w_up", "ffn_w_down")
WEIGHT_ORDER = ("mix_norm_g", "ffn_norm_g", "mem_norm_g", "w_mem_kv", "a_w_in", "a_w_out", "b_w_in", "b_v_norm_g", "b_w_s",
                "b_s_bias", "b_w_out", "ffn_w_up", "ffn_conv_w", "ffn_conv_b", "ffn_w_down", "final_norm_g")


def kernel(x, mem, mix_norm_g, ffn_norm_g, mem_norm_g, w_mem_kv, a_w_in, a_w_out, b_w_in, b_v_norm_g, b_w_s, b_s_bias, b_w_out, ffn_w_up, ffn_conv_w, ffn_conv_b, ffn_w_down, final_norm_g, loss_target, m_mix_norm_g, m_ffn_norm_g, m_mem_norm_g, m_w_mem_kv, m_a_w_in, m_a_w_out, m_b_w_in, m_b_v_norm_g, m_b_w_s, m_b_s_bias, m_b_w_out, m_ffn_w_up, m_ffn_conv_w, m_ffn_conv_b, m_ffn_w_down, m_final_norm_g, v_mix_norm_g, v_ffn_norm_g, v_mem_norm_g, v_w_mem_kv, v_a_w_in, v_a_w_out, v_b_w_in, v_b_v_norm_g, v_b_w_s, v_b_s_bias, v_b_w_out, v_ffn_w_up, v_ffn_conv_w, v_ffn_conv_b, v_ffn_w_down, v_final_norm_g):
    weights = dict(mix_norm_g=mix_norm_g, ffn_norm_g=ffn_norm_g, mem_norm_g=mem_norm_g, w_mem_kv=w_mem_kv, a_w_in=a_w_in,
                   a_w_out=a_w_out, b_w_in=b_w_in, b_v_norm_g=b_v_norm_g, b_w_s=b_w_s, b_s_bias=b_s_bias, b_w_out=b_w_out,
                   ffn_w_up=ffn_w_up, ffn_conv_w=ffn_conv_w, ffn_conv_b=ffn_conv_b, ffn_w_down=ffn_w_down,
                   final_norm_g=final_norm_g)
    mom1 = dict(mix_norm_g=m_mix_norm_g, ffn_norm_g=m_ffn_norm_g, mem_norm_g=m_mem_norm_g, w_mem_kv=m_w_mem_kv,
                a_w_in=m_a_w_in, a_w_out=m_a_w_out, b_w_in=m_b_w_in, b_v_norm_g=m_b_v_norm_g, b_w_s=m_b_w_s,
                b_s_bias=m_b_s_bias, b_w_out=m_b_w_out, ffn_w_up=m_ffn_w_up, ffn_conv_w=m_ffn_conv_w,
                ffn_conv_b=m_ffn_conv_b, ffn_w_down=m_ffn_w_down, final_norm_g=m_final_norm_g)
    mom2 = dict(mix_norm_g=v_mix_norm_g, ffn_norm_g=v_ffn_norm_g, mem_norm_g=v_mem_norm_g, w_mem_kv=v_w_mem_kv,
                a_w_in=v_a_w_in, a_w_out=v_a_w_out, b_w_in=v_b_w_in, b_v_norm_g=v_b_v_norm_g, b_w_s=v_b_w_s,
                b_s_bias=v_b_s_bias, b_w_out=v_b_w_out, ffn_w_up=v_ffn_w_up, ffn_conv_w=v_ffn_conv_w,
                ffn_conv_b=v_ffn_conv_b, ffn_w_down=v_ffn_w_down, final_norm_g=v_final_norm_g)
    chip = 2 * lax.axis_index("x") + lax.axis_index("y")
    place = jnp.stack([chip, lax.axis_index("c")]).astype(jnp.int32)
    x0, mem0, target = x[0], mem[0], loss_target[0]
    depth = DEPTH

    n_cw, n_vg = ffn_conv_w.size, b_v_norm_g.size
    gathered = small_allgather(_pack_rows([ffn_conv_w, b_v_norm_g], 8)).reshape(N_CHIPS, -1)
    conv_w_full = gathered[:, :n_cw].reshape(N_CHIPS, DEPTH, 3, 2 * FF // N_CHIPS).transpose(1, 2, 0, 3).reshape(DEPTH, 3, 2 * FF)
    vgain_full = gathered[:, n_cw:n_cw + n_vg].reshape(N_CHIPS, 2, B_W // N_CHIPS).transpose(1, 0, 2).reshape(2, B_W)
    small = dict(mix_norm_g=mix_norm_g, ffn_norm_g=ffn_norm_g, mem_norm_g=mem_norm_g, b_w_s=b_w_s, ffn_conv_b=ffn_conv_b,
                 ffn_conv_w=conv_w_full, b_v_norm_g=vgain_full,
                 bias_b=jnp.broadcast_to(b_s_bias[..., None], b_s_bias.shape + (CHUNK,)))

    half_layers = 2 * depth
    groups = [(_ffn_weights if b % 2 else _mixer_weights)(b // 2) for b in range(half_layers)]
    tags = [("f" if b % 2 else "m") + str(b // 2) for b in range(half_layers)]

    def start_gather(b, after):
        bufs = [cast_to_slot(weights[n], l, place, name=f"cast_{n}{l}", after=after) for n, l in groups[b]]
        return copies_start("gather", bufs, name=f"gather_start_{tags[b]}")

    def landed(b, after):
        send_sems, recv_sems, bufs, _ = over_ici.pop(b)
        bufs = copies_wait("gather", send_sems, recv_sems, bufs, after, name=f"gather_wait_{tags[b]}")
        return copies_start("forward", bufs, name=f"forward_start_{tags[b]}")

    ahead_from = 4
    over_ici, tie = {}, gathered
    for b in range(3):
        over_ici[b] = start_gather(b, tie)
        tie = over_ici[b][3]
    w_half, saved_half, h = [], [], x0
    for b in range(half_layers):
        behind = tie if b == 0 else h
        if b < ahead_from:
            to_sibling = landed(b, behind)
            behind = to_sibling[3]
        w_half.append(copies_wait("forward", *to_sibling[:3], behind, name=f"forward_wait_{tags[b]}"))
        tokens = []
        if ahead_from <= b + 1 < half_layers:
            to_sibling = landed(b + 1, w_half[b][0])
            tokens.append(to_sibling[3])
        if b + 3 < half_layers:
            over_ici[b + 3] = start_gather(b + 3, w_half[b][0])
            tokens.append(over_ici[b + 3][3])
        tie = sum(tokens[1:], tokens[0]) if tokens else None
        if b % 2 == 0:
            h, sv = _mixer_fwd(b // 2, h, mem0, w_half[b], small, tie)
        else:
            h, sv = _ffn_fwd(b // 2, h, w_half[b], small, tie)
        saved_half.append(sv)
    w_mix, w_ffn, saved_mix, saved_ffn = w_half[0::2], w_half[1::2], saved_half[0::2], saved_half[1::2]
    loss_row, *dh, d_final = final_loss(h, final_norm_g, target)
    loss = lax.psum(loss_row[0, 0], ("x", "y", "c"))

    names = [n for n in WEIGHT_ORDER if n not in BIG]
    small_g = {n: [None] * weights[n].shape[0] for n in names if n != "final_norm_g"}
    big_out = {n: None for n in BIG}

    def keep_small(i, sg):
        for n, g in sg.items():
            small_g[n][i if len(small_g[n]) == depth else i // 2] = g.reshape(SMALL_FULL_SHAPES[n])

    joining = []

    def update(after):
        (send_sems, recv_sems, halves, _), group, tag = joining.pop()
        for (n, l), g in zip(group, copies_wait("join", send_sems, recv_sems, halves, after, name=f"join_wait_{tag}")):
            big_out[n] = adamw_layer(weights[n], mom1[n], mom2[n], l, g, big_out[n], name=f"adamw_{n}{l}")

    def finish_reduce(started, group, after, tag):
        join = _reduce_finish(started, place, after, tag)
        if joining:
            update(join[3])
        joining.append((join, group, tag))

    half_layers = 2 * depth
    swapping, exchanging, tie = None, [], None
    for k in range(half_layers):
        i = depth - 1 - k // 2
        if k % 2 == 0:
            dh, big_g, sg = _ffn_bwd(i, dh, w_ffn[i], small, saved_ffn[i], tie)
            group, tag = _ffn_weights(i), f"f{i}"
        else:
            dh, big_g, sg = _mixer_bwd(i, dh, mem0, w_mix[i], small, saved_mix[i], tie)
            group, tag = _mixer_weights(i), f"m{i}"
        keep_small(i, sg)
        started_now, swap_now, tokens = [], None, []
        if k < half_layers - 2:
            *swap_now, token = swap_start(big_g, name=f"swap_start_{tag}")
            swap_now = (swap_now, group, tag)
        else:
            started, token = _exchange_begin(big_g, swap_halves(big_g, name=f"swap_halves_{tag}"), place, tag)
            started_now.append((started, group, tag))
        tokens.append(token)
        if swapping is not None:
            swap_args, old_group, old_tag = swapping
            grads, got = swap_wait(*swap_args, dh[0], name=f"swap_wait_{old_tag}")
            started, token = _exchange_begin(grads, got, place, old_tag)
            started_now.append((started, old_group, old_tag))
            tokens.append(token)
        tie = sum(tokens[1:], tokens[0])
        for started, old_group, old_tag in exchanging:
            finish_reduce(started, old_group, dh[0], old_tag)
        swapping, exchanging = swap_now, started_now
    for started, old_group, old_tag in exchanging:
        finish_reduce(started, old_group, big_out["ffn_w_down"][0], old_tag)

    full_g = {n: (d_final.reshape(-1) if n == "final_norm_g" else jnp.stack(small_g[n])) for n in names}
    shapes = [full_g[n].shape for n in names]
    flying_small = copies_start("slots", [pair_small(_pack_rows([full_g[n] for n in names], 8), tie)], name="small_start")
    update(flying_small[3])
    reduced = sum_slots(copies_wait("slots", *flying_small[:3], big_out["w_mem_kv"][0], name="small_wait")[0])
    summed = dict(zip(names, _unpack_rows(reduced, shapes)))
    for n in SMALL_SHARDED:
        width = weights[n].shape[-1]
        summed[n] = lax.dynamic_slice_in_dim(summed[n], chip * width, width, axis=summed[n].ndim - 1)
    own_shapes = [weights[n].shape for n in names]
    pack = lambda d: _pack_rows([d[n] for n in names], 128)
    small_out = [_unpack_rows(b, own_shapes) for b in adamw_flat(pack(weights), pack(summed), pack(mom1), pack(mom2))]
    outs = {}
    for k, n in enumerate(names):
        outs[n] = (summed[n], small_out[0][k], small_out[1][k], small_out[2][k])
    outs.update(big_out)
    return (loss, dh[0][None], *[outs[n][0] for n in WEIGHT_ORDER], *[outs[n][1] for n in WEIGHT_ORDER],
            *[outs[n][2] for n in WEIGHT_ORDER], *[outs[n][3] for n in WEIGHT_ORDER])
```

```python
import functools
import math

import numpy as np
import jax
import jax.numpy as jnp
from jax import lax
from jax.experimental import pallas as pl
from jax.experimental.pallas import tpu as pltpu

F32 = jnp.float32
BF16 = jnp.bfloat16
MESH = pl.DeviceIdType.MESH

D_MODEL = 2048
SEQ = 2048
DEPTH = 4
EPS = 1e-6
NEG = -1e30
HEAD = 128
A_PATTERNS = ((128, 1), (512, 4), (2048, 16))
A_QKV_W = 1536
A_OUT_W = 512
A_IN = 5120
QBLK = 128
N_SIDE = 64
CHUNK = 128
B_GROUPS = 12
B_W = 1536
B_IN = 3584
MEM_LEN = 256
MEM_HEADS = 4
MEM_W = 512
FF = 5632
ADAM_LR, ADAM_B1, ADAM_B2, ADAM_EPS, ADAM_WD, ADAM_STEP = 0.001, 0.9, 0.999, 1e-08, 0.01, 10
N_CHIPS = 4

LANES = 128
V7X_VMEM_LIMIT = 56 * 1024 * 1024


def _cp(*sem):
    return pltpu.CompilerParams(dimension_semantics=sem, vmem_limit_bytes=V7X_VMEM_LIMIT)


def _pick(dim, prefs):
    for p in prefs:
        if dim % p == 0:
            return p
    raise ValueError(f"no tile for {dim} in {prefs}")


def _gelu_parts(x):
    cdf = 0.5 * (1.0 + lax.erf(x * (1.0 / math.sqrt(2.0))))
    pdf = jnp.exp(-0.5 * x * x) * (1.0 / math.sqrt(2.0 * math.pi))
    return x * cdf, cdf + x * pdf


def _gelu(x):
    return 0.5 * x * (1.0 + lax.erf(x * (1.0 / math.sqrt(2.0))))


TM_PREFS = (1024, 512, 256, 128)
TN_PREFS = (1408, 1280, 1024, 896, 512, 256, 128)
TK_PREFS = (2816, 2048, 1408, 1280, 1024, 896, 512, 256, 128)


def _mm_body(nk, dims, has_res):
    def body(*refs):
        if has_res:
            a_ref, b_ref, r_ref, o_ref = refs[:4]
        else:
            a_ref, b_ref, o_ref = refs[:3]
            r_ref = None
        part = lax.dot_general(a_ref[...].astype(BF16), b_ref[...].astype(BF16), dims,
                               preferred_element_type=F32)
        if nk == 1:
            if has_res:
                part = part + r_ref[...]
            o_ref[...] = part.astype(o_ref.dtype)
            return
        acc_ref = refs[-1]
        k = pl.program_id(2)

        @pl.when(k == 0)
        def _():
            acc_ref[...] = part

        @pl.when(k > 0)
        def _():
            acc_ref[...] += part

        @pl.when(k == nk - 1)
        def _():
            tot = acc_ref[...]
            if has_res:
                tot = tot + r_ref[...]
            o_ref[...] = tot.astype(o_ref.dtype)
    return body


def mm_nn(a, w, out_dtype, res=None, name="mm_nn"):
    m, kw = a.shape
    ns_, kw2, nsz = w.shape
    assert kw == kw2
    n = ns_ * nsz
    tm, tn, tk = _pick(m, TM_PREFS), _pick(nsz, TN_PREFS), _pick(kw, TK_PREFS)
    nb, nk = nsz // tn, kw // tk
    in_specs = [pl.BlockSpec((tm, tk), lambda i, j, k: (i, k)),
                pl.BlockSpec((None, tk, tn), lambda i, j, k: (j // nb, k, j % nb))]
    args = [a, w]
    if res is not None:
        in_specs.append(pl.BlockSpec((tm, tn), lambda i, j, k: (i, j)))
        args.append(res)
    return pl.pallas_call(
        _mm_body(nk, (((1,), (0,)), ((), ())), res is not None),
        out_shape=jax.ShapeDtypeStruct((m, n), out_dtype),
        grid=(m // tm, n // tn, nk), in_specs=in_specs,
        out_specs=pl.BlockSpec((tm, tn), lambda i, j, k: (i, j)),
        scratch_shapes=[pltpu.VMEM((tm, tn), F32)] if nk > 1 else [],
        compiler_params=_cp("parallel", "parallel", "arbitrary"), name=name)(*args)


def mm_nt(g, w, out_dtype, name="mm_nt", after=None):
    m, n = g.shape
    ns_, kw, nsz = w.shape
    assert n == ns_ * nsz
    tm, tn, tk = _pick(m, TM_PREFS), _pick(kw, TN_PREFS), _pick(nsz, TK_PREFS)
    nb, nk = nsz // tk, n // tk
    body = _mm_body(nk, (((1,), (1,)), ((), ())), False)
    tied = [] if after is None else [after]
    return pl.pallas_call(
        (lambda g_ref, w_ref, *rest: body(g_ref, w_ref, *rest[len(tied):])),
        out_shape=jax.ShapeDtypeStruct((m, kw), out_dtype),
        grid=(m // tm, kw // tn, nk),
        in_specs=[pl.BlockSpec((tm, tk), lambda i, j, k: (i, k)),
                  pl.BlockSpec((None, tn, tk), lambda i, j, k: (k // nb, j, k % nb))] + [ANY] * len(tied),
        out_specs=pl.BlockSpec((tm, tn), lambda i, j, k: (i, j)),
        scratch_shapes=[pltpu.VMEM((tm, tn), F32)] if nk > 1 else [],
        compiler_params=_cp("parallel", "parallel", "arbitrary"), name=name)(g, w, *tied)


def mm_tn(a, g, n_shards, out_dtype, name="mm_tn"):
    t, kw = a.shape
    t2, n = g.shape
    assert t == t2
    nsz = n // n_shards
    tm, tn, tk = _pick(kw, TM_PREFS), _pick(nsz, TN_PREFS), _pick(t, TK_PREFS)
    nb, nk = nsz // tn, t // tk
    return pl.pallas_call(
        _mm_body(nk, (((0,), (0,)), ((), ())), False),
        out_shape=jax.ShapeDtypeStruct((n_shards, kw, nsz), out_dtype),
        grid=(kw // tm, n // tn, nk),
        in_specs=[pl.BlockSpec((tk, tm), lambda i, j, k: (k, i)),
                  pl.BlockSpec((tk, tn), lambda i, j, k: (k, j))],
        out_specs=pl.BlockSpec((None, tm, tn), lambda i, j, k: (j // nb, i, j % nb)),
        scratch_shapes=[pltpu.VMEM((tm, tn), F32)] if nk > 1 else [],
        compiler_params=_cp("parallel", "parallel", "arbitrary"), name=name)(a, g)


ROW_TILE = 256


def _rms_stats(x):
    r = lax.rsqrt(jnp.mean(x * x, axis=-1, keepdims=True) + EPS)
    return r, x * r


def _rms_back(xh, r, g, dh):
    u = dh * g
    return r * (u - xh * jnp.mean(u * xh, axis=-1, keepdims=True))


def rms_fwd(x, g, out_dtype, name="rms_fwd", after=None):
    rows, d = x.shape
    tr = _pick(rows, (ROW_TILE, 128))
    tied = [] if after is None else [after]

    def body(x_ref, g_ref, *rest):
        o_ref = rest[-1]
        _, xh = _rms_stats(x_ref[...])
        o_ref[...] = (xh * g_ref[...]).astype(o_ref.dtype)

    return pl.pallas_call(
        body, out_shape=jax.ShapeDtypeStruct((rows, d), out_dtype), grid=(rows // tr,),
        in_specs=[pl.BlockSpec((tr, d), lambda i: (i, 0)), pl.BlockSpec((1, d), lambda i: (0, 0))] + [ANY] * len(tied),
        out_specs=pl.BlockSpec((tr, d), lambda i: (i, 0)),
        compiler_params=_cp("parallel"), name=name)(x, g.reshape(1, d), *tied)


def rms_bwd(x, g, dh, dres=None, name="rms_bwd"):
    rows, d = x.shape
    tr = _pick(rows, (ROW_TILE, 128))
    has_res = dres is not None

    def body(*refs):
        if has_res:
            x_ref, g_ref, dh_ref, dres_ref, dx_ref, dx16_ref, dg_ref = refs
        else:
            x_ref, g_ref, dh_ref, dx_ref, dx16_ref, dg_ref = refs
        r, xh = _rms_stats(x_ref[...])
        dh_ = dh_ref[...].astype(F32)
        part = jnp.sum(dh_ * xh, axis=0, keepdims=True)

        @pl.when(pl.program_id(0) == 0)
        def _():
            dg_ref[...] = part

        @pl.when(pl.program_id(0) > 0)
        def _():
            dg_ref[...] += part

        dx = _rms_back(xh, r, g_ref[...], dh_)
        if has_res:
            dx = dx + dres_ref[...]
        dx_ref[...] = dx
        dx16_ref[...] = dx.astype(BF16)

    row_spec = pl.BlockSpec((tr, d), lambda i: (i, 0))
    vec_spec = pl.BlockSpec((1, d), lambda i: (0, 0))
    args = [x, g.reshape(1, d), dh] + ([dres] if has_res else [])
    return pl.pallas_call(
        body, out_shape=(jax.ShapeDtypeStruct((rows, d), F32), jax.ShapeDtypeStruct((rows, d), BF16),
                         jax.ShapeDtypeStruct((1, d), F32)),
        grid=(rows // tr,), in_specs=[row_spec, vec_spec, row_spec] + ([row_spec] if has_res else []),
        out_specs=(row_spec, row_spec, vec_spec), compiler_params=_cp("arbitrary"), name=name)(*args)


def final_loss(x, g, target, name="final_loss"):
    rows, d = x.shape
    tr = _pick(rows, (ROW_TILE, 128))

    def body(x_ref, g_ref, t_ref, loss_ref, dx_ref, dx16_ref, dg_ref):
        r, xh = _rms_stats(x_ref[...])
        gain = g_ref[...]
        err = xh * gain - t_ref[...]
        sq = jnp.sum(jnp.sum(err * err, axis=1, keepdims=True), axis=0, keepdims=True) * (0.5 / d)
        dy = err * (1.0 / d)
        part = jnp.sum(dy * xh, axis=0, keepdims=True)

        @pl.when(pl.program_id(0) == 0)
        def _():
            dg_ref[...] = part
            loss_ref[...] = jnp.broadcast_to(sq, loss_ref.shape)

        @pl.when(pl.program_id(0) > 0)
        def _():
            dg_ref[...] += part
            loss_ref[...] += jnp.broadcast_to(sq, loss_ref.shape)

        dx = _rms_back(xh, r, gain, dy)
        dx_ref[...] = dx
        dx16_ref[...] = dx.astype(BF16)

    row_spec = pl.BlockSpec((tr, d), lambda i: (i, 0))
    vec_spec = pl.BlockSpec((1, d), lambda i: (0, 0))
    return pl.pallas_call(
        body, out_shape=(jax.ShapeDtypeStruct((1, LANES), F32), jax.ShapeDtypeStruct((rows, d), F32),
                         jax.ShapeDtypeStruct((rows, d), BF16), jax.ShapeDtypeStruct((1, d), F32)),
        grid=(rows // tr,), in_specs=[row_spec, vec_spec, row_spec],
        out_specs=(pl.BlockSpec((1, LANES), lambda i: (0, 0)), row_spec, row_spec, vec_spec),
        compiler_params=_cp("arbitrary"), name=name)(x, g.reshape(1, d), target)


def _alibi_slopes():
    return (2.0 ** (-8.0 * (np.arange(12) + 1) / 12)).astype(np.float32)


def _band_scores(q, k, q0, start, wk, slope):
    s = lax.dot_general(q, k, (((1,), (1,)), ((), ())), preferred_element_type=F32) * (HEAD ** -0.5)
    qpos = q0 + lax.broadcasted_iota(jnp.int32, (QBLK, wk), 0)
    kpos = start + lax.broadcasted_iota(jnp.int32, (QBLK, wk), 1)
    rel = jnp.abs(qpos - kpos)
    return jnp.where(rel <= N_SIDE, s - slope * rel.astype(F32), NEG)


def _attn_geometry(seq, dilation):
    length = seq // dilation
    return length, length // QBLK, min(2 * QBLK, length)


def _attn_window(n, length, wk):
    q0 = pl.multiple_of(n * QBLK, QBLK)
    start = pl.multiple_of(jnp.clip(n * QBLK - N_SIDE, 0, length - wk), N_SIDE)
    return q0, start


def _class_in(refs, scratch, r, dilation, length):
    if dilation == 1:
        return refs
    for ref, buf in zip(refs, scratch):
        buf[...] = ref[pl.ds(r, length, stride=dilation), :]
    return scratch


def _class_out(refs, scratch, r, dilation, length):
    if dilation > 1:
        for ref, buf in zip(refs, scratch):
            ref[pl.ds(r, length, stride=dilation), :] = buf[...]


def attn_fwd(proj, group, name):
    seq = proj.shape[0]
    dilation = A_PATTERNS[group][1]
    length, nblk, wk = _attn_geometry(seq, dilation)

    def body(slope_ref, q_ref, k_ref, v_ref, o_ref, lse_ref, *scratch):
        slope = slope_ref[group * 4 + pl.program_id(0)] * float(dilation)
        for r in range(dilation):
            q_c, k_c, v_c = _class_in((q_ref, k_ref, v_ref), scratch[:3], r, dilation, length)
            o_c, lse_c = (o_ref, lse_ref) if dilation == 1 else scratch[3:]

            def blk(n, carry):
                q0, start = _attn_window(n, length, wk)
                q = q_c[pl.ds(q0, QBLK), :].astype(BF16)
                k = k_c[pl.ds(start, wk), :].astype(BF16)
                v = v_c[pl.ds(start, wk), :].astype(BF16)
                s = _band_scores(q, k, q0, start, wk, slope)
                m = jnp.max(s, axis=-1, keepdims=True)
                p = jnp.exp(s - m)
                l = jnp.sum(p, axis=-1, keepdims=True)
                o = jnp.dot(p.astype(BF16), v, preferred_element_type=F32) / l
                o_c[pl.ds(q0, QBLK), :] = o
                lse_c[pl.ds(q0, QBLK), :] = jnp.broadcast_to(m + jnp.log(l), (QBLK, HEAD))
                return carry

            lax.fori_loop(0, nblk, blk, 0)
            _class_out((o_ref, lse_ref), scratch[3:], r, dilation, length)

    def part(p):
        return pl.BlockSpec((seq, HEAD), lambda h: (0, p * 12 + group * 4 + h))

    out_spec = pl.BlockSpec((seq, HEAD), lambda h: (0, h))
    return pl.pallas_call(
        body, out_shape=(jax.ShapeDtypeStruct((seq, A_OUT_W), F32),) * 2, grid=(4,),
        in_specs=[pl.BlockSpec(memory_space=pltpu.SMEM), part(0), part(1), part(2)],
        out_specs=(out_spec, out_spec),
        scratch_shapes=[pltpu.VMEM((length, HEAD), F32)] * (5 if dilation > 1 else 0),
        compiler_params=_cp("parallel"), name=name)(jnp.asarray(_alibi_slopes()), proj, proj, proj)


def attn_combine(os_, lses, name="attn_combine"):
    seq = os_[0].shape[0]
    tr = ROW_TILE

    def body(o0, o1, o2, l0, l1, l2, c_ref, lse_ref):
        a, b, c = l0[...], l1[...], l2[...]
        m = jnp.maximum(jnp.maximum(a, b), c)
        ea, eb, ec = jnp.exp(a - m), jnp.exp(b - m), jnp.exp(c - m)
        den = ea + eb + ec
        c_ref[...] = (ea * o0[...] + eb * o1[...] + ec * o2[...]) / den
        lse_ref[...] = m + jnp.log(den)

    spec = pl.BlockSpec((tr, A_OUT_W), lambda i: (i, 0))
    return pl.pallas_call(
        body, out_shape=(jax.ShapeDtypeStruct((seq, A_OUT_W), F32),) * 2, grid=(seq // tr,),
        in_specs=[spec] * 6, out_specs=(spec, spec), compiler_params=_cp("parallel"), name=name)(*os_, *lses)


def attn_bwd(proj, dcat, comb, lse, group, name):
    seq = proj.shape[0]
    dilation = A_PATTERNS[group][1]
    length, nblk, wk = _attn_geometry(seq, dilation)
    scale = HEAD ** -0.5

    def body(slope_ref, q_ref, k_ref, v_ref, do_ref, c_ref, lse_ref, dq_ref, dk_ref, dv_ref, *scratch):
        slope = slope_ref[group * 4 + pl.program_id(0)] * float(dilation)
        for r in range(dilation):
            q_c, k_c, v_c, do_c, c_c, lse_c = _class_in((q_ref, k_ref, v_ref, do_ref, c_ref, lse_ref), scratch[:6], r,
                                                        dilation, length)
            dq_c, dk_c, dv_c = (dq_ref, dk_ref, dv_ref) if dilation == 1 else scratch[6:]
            dk_c[...] = jnp.zeros_like(dk_c)
            dv_c[...] = jnp.zeros_like(dv_c)

            def blk(n, carry):
                q0, start = _attn_window(n, length, wk)
                rows = pl.ds(q0, QBLK)
                keys = pl.ds(start, wk)
                q = q_c[rows, :].astype(BF16)
                k = k_c[keys, :].astype(BF16)
                v = v_c[keys, :].astype(BF16)
                do = do_c[rows, :]
                s = _band_scores(q, k, q0, start, wk, slope)
                p = jnp.exp(s - lse_c[rows, :][:, :1])
                delta = jnp.sum(do * c_c[rows, :], axis=-1, keepdims=True)
                do16 = do.astype(BF16)
                dp = lax.dot_general(do16, v, (((1,), (1,)), ((), ())), preferred_element_type=F32)
                ds = (p * (dp - delta) * scale).astype(BF16)
                p16 = p.astype(BF16)
                dq_c[rows, :] = jnp.dot(ds, k, preferred_element_type=F32)
                dk_c[keys, :] += lax.dot_general(ds, q, (((0,), (0,)), ((), ())), preferred_element_type=F32)
                dv_c[keys, :] += lax.dot_general(p16, do16, (((0,), (0,)), ((), ())), preferred_element_type=F32)
                return carry

            lax.fori_loop(0, nblk, blk, 0)
            _class_out((dq_ref, dk_ref, dv_ref), scratch[6:], r, dilation, length)

    def part(p):
        return pl.BlockSpec((seq, HEAD), lambda h: (0, p * 12 + group * 4 + h))

    hs = pl.BlockSpec((seq, HEAD), lambda h: (0, h))
    return pl.pallas_call(
        body, out_shape=(jax.ShapeDtypeStruct((seq, A_OUT_W), F32),) * 3, grid=(4,),
        in_specs=[pl.BlockSpec(memory_space=pltpu.SMEM), part(0), part(1), part(2), hs, hs, hs],
        out_specs=(hs, hs, hs),
        scratch_shapes=[pltpu.VMEM((length, HEAD), F32)] * (9 if dilation > 1 else 0),
        compiler_params=_cp("parallel"), name=name,
    )(jnp.asarray(_alibi_slopes()), proj, proj, proj, dcat, comb, lse)


MEM_ROW_TILE = 512


def _mem_probs(q, k):
    s = lax.dot_general(q, k, (((1,), (1,)), ((), ())), preferred_element_type=F32) * (HEAD ** -0.5)
    p = jnp.exp(s - jnp.max(s, axis=-1, keepdims=True))
    return p / jnp.sum(p, axis=-1, keepdims=True)


def mem_fwd(proj, q_col, kv, name="mem_fwd"):
    seq = proj.shape[0]
    qb = q_col // HEAD

    def body(q_ref, k_ref, v_ref, o_ref):
        p = _mem_probs(q_ref[...].astype(BF16), k_ref[...].astype(BF16))
        o_ref[...] = jnp.dot(p.astype(BF16), v_ref[...].astype(BF16), preferred_element_type=F32).astype(o_ref.dtype)

    return pl.pallas_call(
        body, out_shape=jax.ShapeDtypeStruct((seq, MEM_W), BF16), grid=(MEM_HEADS, seq // MEM_ROW_TILE),
        in_specs=[pl.BlockSpec((MEM_ROW_TILE, HEAD), lambda h, i: (i, qb + h)),
                  pl.BlockSpec((MEM_LEN, HEAD), lambda h, i: (0, h)),
                  pl.BlockSpec((MEM_LEN, HEAD), lambda h, i: (0, MEM_HEADS + h))],
        out_specs=pl.BlockSpec((MEM_ROW_TILE, HEAD), lambda h, i: (i, h)),
        compiler_params=_cp("parallel", "parallel"), name=name)(proj, kv, kv)


def mem_bwd(proj, q_col, kv, dcat, do_col, name="mem_bwd"):
    seq = proj.shape[0]
    qb, ob = q_col // HEAD, do_col // HEAD
    scale = HEAD ** -0.5

    def body(q_ref, k_ref, v_ref, do_ref, dq_ref, dk_ref, dv_ref):
        q = q_ref[...].astype(BF16)
        k = k_ref[...].astype(BF16)
        v = v_ref[...].astype(BF16)
        do = do_ref[...].astype(BF16)
        p = _mem_probs(q, k)
        dp = lax.dot_general(do, v, (((1,), (1,)), ((), ())), preferred_element_type=F32)
        ds = (p * (dp - jnp.sum(dp * p, axis=-1, keepdims=True)) * scale).astype(BF16)
        dq_ref[...] = jnp.dot(ds, k, preferred_element_type=F32).astype(dq_ref.dtype)
        dk = lax.dot_general(ds, q, (((0,), (0,)), ((), ())), preferred_element_type=F32)
        dv = lax.dot_general(p.astype(BF16), do, (((0,), (0,)), ((), ())), preferred_element_type=F32)

        @pl.when(pl.program_id(1) == 0)
        def _():
            dk_ref[...] = dk
            dv_ref[...] = dv

        @pl.when(pl.program_id(1) > 0)
        def _():
            dk_ref[...] += dk
            dv_ref[...] += dv

    dq, dk, dv = pl.pallas_call(
        body, out_shape=(jax.ShapeDtypeStruct((seq, MEM_W), BF16), jax.ShapeDtypeStruct((MEM_LEN, MEM_W), F32),
                         jax.ShapeDtypeStruct((MEM_LEN, MEM_W), F32)),
        grid=(MEM_HEADS, seq // MEM_ROW_TILE),
        in_specs=[pl.BlockSpec((MEM_ROW_TILE, HEAD), lambda h, i: (i, qb + h)),
                  pl.BlockSpec((MEM_LEN, HEAD), lambda h, i: (0, h)),
                  pl.BlockSpec((MEM_LEN, HEAD), lambda h, i: (0, MEM_HEADS + h)),
                  pl.BlockSpec((MEM_ROW_TILE, HEAD), lambda h, i: (i, ob + h))],
        out_specs=(pl.BlockSpec((MEM_ROW_TILE, HEAD), lambda h, i: (i, h)),
                   pl.BlockSpec((MEM_LEN, HEAD), lambda h, i: (0, h)),
                   pl.BlockSpec((MEM_LEN, HEAD), lambda h, i: (0, h))),
        compiler_params=_cp("parallel", "arbitrary"), name=name)(proj, kv, kv, dcat)
    return dq, jnp.concatenate([dk, dv], axis=1)


def _sgu_front(x, gain):
    uv, duv = _gelu_parts(x)
    u, v = uv[:, :B_W], uv[:, B_W:]
    r, vh = _rms_stats(v)
    return u, duv, r, vh, vh * gain


def sgu_fwd(proj, gain, w_s, bias_b, name="sgu_fwd"):
    seq = proj.shape[0]

    def body(x_ref, gain_ref, ws_ref, bias_ref, o_ref):
        u, _, _, _, vn = _sgu_front(x_ref[...], gain_ref[...])
        for g in range(B_GROUPS):
            cs = slice(g * CHUNK, (g + 1) * CHUNK)
            mixed = jnp.dot(ws_ref[g].astype(BF16), vn[:, cs].astype(BF16), preferred_element_type=F32) + bias_ref[g]
            o_ref[:, cs] = (u[:, cs] * mixed).astype(o_ref.dtype)

    full = lambda shape: pl.BlockSpec(shape, lambda c: (0,) * len(shape))
    return pl.pallas_call(
        body, out_shape=jax.ShapeDtypeStruct((seq, B_W), BF16), grid=(seq // CHUNK,),
        in_specs=[pl.BlockSpec((CHUNK, 2 * B_W), lambda c: (c, 0)), full((1, B_W)),
                  full((B_GROUPS, CHUNK, CHUNK)), full((B_GROUPS, CHUNK, CHUNK))],
        out_specs=pl.BlockSpec((CHUNK, B_W), lambda c: (c, 0)),
        compiler_params=_cp("parallel"), name=name)(proj, gain.reshape(1, B_W), w_s, bias_b)


def sgu_bwd(proj, gain, w_s, w_s_t, bias_b, dcat, name="sgu_bwd"):
    seq = proj.shape[0]

    def body(x_ref, gain_ref, ws_ref, wst_ref, bias_ref, do_ref, dx_ref, dws_ref, dmix_ref, dgain_ref, dvn_ref):
        first = pl.program_id(0) == 0
        gain = gain_ref[...]
        u, duv, r, vh, vn = _sgu_front(x_ref[...], gain)
        do = do_ref[...]
        for g in range(B_GROUPS):
            cs = slice(g * CHUNK, (g + 1) * CHUNK)
            vg = vn[:, cs].astype(BF16)
            mixed = jnp.dot(ws_ref[g].astype(BF16), vg, preferred_element_type=F32) + bias_ref[g]
            dx_ref[:, cs] = (do[:, cs] * mixed * duv[:, cs]).astype(dx_ref.dtype)
            dmixed = do[:, cs] * u[:, cs]
            dm16 = dmixed.astype(BF16)
            dws = lax.dot_general(dm16, vg, (((1,), (1,)), ((), ())), preferred_element_type=F32)
            dvn_ref[:, cs] = jnp.dot(wst_ref[g].astype(BF16), dm16, preferred_element_type=F32)

            @pl.when(first)
            def _():
                dws_ref[g] = dws
                dmix_ref[g] = dmixed

            @pl.when(jnp.logical_not(first))
            def _():
                dws_ref[g] += dws
                dmix_ref[g] += dmixed

        dvn = dvn_ref[...]
        dgain = jnp.sum(dvn * vh, axis=0, keepdims=True)

        @pl.when(first)
        def _():
            dgain_ref[...] = dgain

        @pl.when(jnp.logical_not(first))
        def _():
            dgain_ref[...] += dgain

        dv = _rms_back(vh, r, gain, dvn)
        dx_ref[:, B_W:] = (dv * duv[:, B_W:]).astype(dx_ref.dtype)

    full = lambda shape: pl.BlockSpec(shape, lambda c: (0,) * len(shape))
    mats = full((B_GROUPS, CHUNK, CHUNK))
    return pl.pallas_call(
        body, out_shape=(jax.ShapeDtypeStruct((seq, 2 * B_W), BF16), jax.ShapeDtypeStruct((B_GROUPS, CHUNK, CHUNK), F32),
                         jax.ShapeDtypeStruct((B_GROUPS, CHUNK, CHUNK), F32), jax.ShapeDtypeStruct((1, B_W), F32)),
        grid=(seq // CHUNK,),
        in_specs=[pl.BlockSpec((CHUNK, 2 * B_W), lambda c: (c, 0)), full((1, B_W)), mats, mats, mats,
                  pl.BlockSpec((CHUNK, B_W), lambda c: (c, 0))],
        out_specs=(pl.BlockSpec((CHUNK, 2 * B_W), lambda c: (c, 0)), mats, mats, full((1, B_W))),
        scratch_shapes=[pltpu.VMEM((CHUNK, B_W), F32)],
        compiler_params=_cp("arbitrary"), name=name)(proj, gain.reshape(1, B_W), w_s, w_s_t, bias_b, dcat)


FFN_COLS = 128
FFN_ROWS = 32
SUBLANES = 8


def _window(ref, r0, first, last):
    cols = ref.shape[1]
    pad = jnp.zeros((SUBLANES, cols), F32)
    if first:
        return jnp.concatenate([pad, ref[pl.ds(0, FFN_ROWS + SUBLANES), :]], axis=0)
    if last:
        return jnp.concatenate([ref[pl.ds(r0 - SUBLANES, FFN_ROWS + SUBLANES), :], pad], axis=0)
    return ref[pl.ds(pl.multiple_of(r0 - SUBLANES, SUBLANES), FFN_ROWS + 2 * SUBLANES), :]


def _taps(win):
    mid = slice(SUBLANES, SUBLANES + FFN_ROWS)
    return pltpu.roll(win, 1, 0)[mid], win[mid], pltpu.roll(win, win.shape[0] - 1, 0)[mid]


def _row_steps(seq, step, carry):
    n = seq // FFN_ROWS
    carry = step(0, True, False, carry)
    carry = lax.fori_loop(1, n - 1, lambda i, c: step(pl.multiple_of(i * FFN_ROWS, FFN_ROWS), False, False, c), carry)
    return step(seq - FFN_ROWS, False, True, carry)


def _conv3(taps, w, b):
    prev, cur, nxt = taps
    return prev * w[0:1] + cur * w[1:2] + nxt * w[2:3] + b


def _fold(x):
    return jnp.sum(x.reshape(FFN_ROWS // SUBLANES, SUBLANES, x.shape[1]), axis=0)


FFN_FWD_COLS = 256


def _taps_whole(a):
    n = a.shape[0]
    rows = lax.broadcasted_iota(jnp.int32, a.shape, 0)
    return (jnp.where(rows == 0, 0.0, pltpu.roll(a, 1, 0)), a, jnp.where(rows == n - 1, 0.0, pltpu.roll(a, n - 1, 0)))


def ffn_act_fwd(a, conv_w, conv_b, name="ffn_act_fwd"):
    seq = a.shape[0]
    nb = FF // FFN_FWD_COLS

    def body(ag_ref, av_ref, wg_ref, wv_ref, bg_ref, bv_ref, o_ref):
        gate = _conv3(_taps_whole(ag_ref[...]), wg_ref[...], bg_ref[...])
        val = _conv3(_taps_whole(av_ref[...]), wv_ref[...], bv_ref[...])
        o_ref[...] = (_gelu(gate) * val).astype(o_ref.dtype)

    col = lambda rows, off: pl.BlockSpec((rows, FFN_FWD_COLS), lambda j: (0, j + off))
    cb = conv_b.reshape(1, 2 * FF)
    return pl.pallas_call(
        body, out_shape=jax.ShapeDtypeStruct((seq, FF), BF16), grid=(nb,),
        in_specs=[col(seq, 0), col(seq, nb), col(3, 0), col(3, nb), col(1, 0), col(1, nb)],
        out_specs=col(seq, 0), compiler_params=_cp("parallel"), name=name)(a, a, conv_w, conv_w, cb, cb)


def ffn_act_bwd(a, conv_w, conv_b, dact, name="ffn_act_bwd"):
    seq = a.shape[0]
    nb = FF // FFN_COLS

    def body(ag_ref, av_ref, wg_ref, wv_ref, bg_ref, bv_ref, d_ref, dag_ref, dav_ref, dwg_ref, dwv_ref, dbg_ref, dbv_ref,
             dcg_ref, dcv_ref):
        wg, wv, bg, bv = wg_ref[...], wv_ref[...], bg_ref[...], bv_ref[...]

        def conv_grads(r0, first, last, sums):
            g_taps = _taps(_window(ag_ref, r0, first, last))
            v_taps = _taps(_window(av_ref, r0, first, last))
            act, dact_dgate = _gelu_parts(_conv3(g_taps, wg, bg))
            d = d_ref[pl.ds(r0, FFN_ROWS), :].astype(F32)
            dcg = d * _conv3(v_taps, wv, bv) * dact_dgate
            dcv = d * act
            dcg_ref[pl.ds(r0, FFN_ROWS), :] = dcg
            dcv_ref[pl.ds(r0, FFN_ROWS), :] = dcv
            new = [_fold(dcg)] + [_fold(dcg * t) for t in g_taps] + [_fold(dcv)] + [_fold(dcv * t) for t in v_taps]
            return tuple(s + n for s, n in zip(sums, new))

        zero = jnp.zeros((SUBLANES, FFN_COLS), F32)
        sums = _row_steps(seq, conv_grads, (zero,) * 8)
        total = [jnp.sum(s, axis=0, keepdims=True) for s in sums]
        dbg_ref[...] = total[0]
        dbv_ref[...] = total[4]
        for k in range(3):
            dwg_ref[k:k + 1, :] = total[1 + k]
            dwv_ref[k:k + 1, :] = total[5 + k]

        def conv_transpose(r0, first, last, carry):
            for dc_ref, w, da_ref in ((dcg_ref, wg, dag_ref), (dcv_ref, wv, dav_ref)):
                prev, cur, nxt = _taps(_window(dc_ref, r0, first, last))
                da_ref[pl.ds(r0, FFN_ROWS), :] = (nxt * w[0:1] + cur * w[1:2] + prev * w[2:3]).astype(da_ref.dtype)
            return carry

        _row_steps(seq, conv_transpose, 0)

    col = lambda rows, off: pl.BlockSpec((rows, FFN_COLS), lambda j: (0, j + off))
    cb = conv_b.reshape(1, 2 * FF)
    dag, dav, dwg, dwv, dbg, dbv = pl.pallas_call(
        body, out_shape=(jax.ShapeDtypeStruct((seq, FF), BF16),) * 2 + (jax.ShapeDtypeStruct((3, FF), F32),) * 2
        + (jax.ShapeDtypeStruct((1, FF), F32),) * 2, grid=(nb,),
        in_specs=[col(seq, 0), col(seq, nb), col(3, 0), col(3, nb), col(1, 0), col(1, nb), col(seq, 0)],
        out_specs=(col(seq, 0), col(seq, 0), col(3, 0), col(3, 0), col(1, 0), col(1, 0)),
        scratch_shapes=[pltpu.VMEM((seq, FFN_COLS), F32), pltpu.VMEM((seq, FFN_COLS), F32)],
        compiler_params=_cp("parallel"), name=name)(a, a, conv_w, conv_w, cb, cb, dact)
    cat = lambda p, q: jnp.concatenate([p, q], axis=1)
    return cat(dag, dav), cat(dwg, dwv), cat(dbg, dbv)


def _adam_math(w, g, m, v):
    m = ADAM_B1 * m + (1.0 - ADAM_B1) * g
    v = ADAM_B2 * v + (1.0 - ADAM_B2) * (g * g)
    m_hat = m / (1.0 - ADAM_B1 ** ADAM_STEP)
    v_hat = v / (1.0 - ADAM_B2 ** ADAM_STEP)
    return -ADAM_LR * (m_hat / (jnp.sqrt(v_hat) + ADAM_EPS) + ADAM_WD * w), m, v


def _row_tile(rows, cols):
    return _pick(rows, (256, 128, 64)) if cols <= 1024 else _pick(rows, (128, 64))


BF16_ROWS = 16
STREAM_BLOCK_BYTES = 3 * 1024 * 1024


def _stream_rows(rows, cols, itemsize):
    fits = [r for r in range(BF16_ROWS, rows + 1, BF16_ROWS) if rows % r == 0 and r * cols * itemsize <= STREAM_BLOCK_BYTES]
    return max(fits)


def adamw_layer(w_all, m_all, v_all, layer, g, prev, name):
    n, rows, cols = w_all.shape
    tr = _row_tile(rows, cols)

    def body(w_ref, m_ref, v_ref, g_ref, *rest):
        go_ref, d_ref, mo_ref, vo_ref = rest[-4:]
        g_ = g_ref[...]
        d, m_, v_ = _adam_math(w_ref[...], g_, m_ref[...], v_ref[...])
        go_ref[...] = g_
        d_ref[...] = d
        mo_ref[...] = m_
        vo_ref[...] = v_

    lay = pl.BlockSpec((None, tr, cols), lambda i: (layer, i, 0))
    in_specs = [lay, lay, lay, pl.BlockSpec((tr, cols), lambda i: (i, 0))]
    args = [w_all, m_all, v_all, g]
    aliases = {}
    if prev is not None:
        in_specs += [pl.BlockSpec(memory_space=pl.ANY)] * 4
        args += list(prev)
        aliases = {4 + k: k for k in range(4)}
    return pl.pallas_call(
        body, out_shape=(jax.ShapeDtypeStruct(w_all.shape, F32),) * 4, grid=(rows // tr,),
        in_specs=in_specs, out_specs=(lay,) * 4, input_output_aliases=aliases,
        compiler_params=_cp("parallel"), name=name)(*args)


def adamw_flat(w, g, m, v, name="adamw_small"):
    rows, cols = w.shape
    tr = _pick(rows, (128, 8))

    def body(w_ref, g_ref, m_ref, v_ref, d_ref, mo_ref, vo_ref):
        d_ref[...], mo_ref[...], vo_ref[...] = _adam_math(w_ref[...], g_ref[...], m_ref[...], v_ref[...])

    spec = pl.BlockSpec((tr, cols), lambda i: (i, 0))
    return pl.pallas_call(
        body, out_shape=(jax.ShapeDtypeStruct(w.shape, F32),) * 3, grid=(rows // tr,),
        in_specs=[spec] * 4, out_specs=(spec,) * 3, compiler_params=_cp("parallel"), name=name)(w, g, m, v)


def pair_sum(dw, got, core, name):
    _, rows, cols = dw.shape
    half = rows // 2
    tr = _stream_rows(half, cols, 2)
    nrb = half // tr

    def body(c_ref, a_ref, b_ref, o_ref):
        o_ref[...] = (a_ref[...].astype(F32) + b_ref[...].astype(F32)).astype(o_ref.dtype)

    return pl.pallas_call(
        body, out_shape=jax.ShapeDtypeStruct((N_CHIPS, half, cols), BF16),
        grid_spec=pltpu.PrefetchScalarGridSpec(
            num_scalar_prefetch=1, grid=(N_CHIPS, nrb),
            in_specs=[pl.BlockSpec((None, tr, cols), lambda s, i, c_ref: (s, c_ref[0] * nrb + i, 0)),
                      pl.BlockSpec((None, tr, cols), lambda s, i, c_ref: (s, i, 0))],
            out_specs=pl.BlockSpec((None, tr, cols), lambda s, i, c_ref: (s, i, 0))),
        compiler_params=_cp("parallel", "parallel"), name=name)(core, dw, got)


def chip_sum(own, parts, place, name):
    _, half, cols = parts.shape
    tr = _stream_rows(half, cols, 4)
    nrb = half // tr

    def body(p_ref, own_ref, a_ref, b_ref, c_ref, o_ref):
        o_ref[...] = ((own_ref[...].astype(F32) + a_ref[...].astype(F32)) + b_ref[...].astype(F32)) + c_ref[...].astype(F32)

    def slot(k):
        return pl.BlockSpec((None, tr, cols), lambda i, p: (jnp.bitwise_xor(p[0], k), i, 0))

    return pl.pallas_call(
        body, out_shape=jax.ShapeDtypeStruct((2 * half, cols), F32),
        grid_spec=pltpu.PrefetchScalarGridSpec(
            num_scalar_prefetch=1, grid=(nrb,), in_specs=[slot(0), slot(1), slot(2), slot(3)],
            out_specs=pl.BlockSpec((tr, cols), lambda i, p: (p[1] * nrb + i, 0))),
        compiler_params=_cp("parallel"), name=name)(place, own, parts, parts, parts)


def cast_to_slot(w_all, layer, place, name, after=None):
    _, rows, cols = w_all.shape
    tr = _stream_rows(rows, cols, 4)
    tied = [] if after is None else [after]

    def body(p_ref, w_ref, *rest):
        o_ref = rest[-1]
        o_ref[...] = w_ref[...].astype(o_ref.dtype)

    return pl.pallas_call(
        body, out_shape=jax.ShapeDtypeStruct((N_CHIPS, rows, cols), BF16),
        grid_spec=pltpu.PrefetchScalarGridSpec(
            num_scalar_prefetch=1, grid=(rows // tr,),
            in_specs=[pl.BlockSpec((None, tr, cols), lambda i, p: (layer, i, 0))] + [ANY] * len(tied),
            out_specs=pl.BlockSpec((None, tr, cols), lambda i, p: (p[0], i, 0))),
        compiler_params=_cp("parallel"), name=name)(place, w_all, *tied)


ANY = pl.BlockSpec(memory_space=pl.ANY)


def _place():
    x, y, c = lax.axis_index("x"), lax.axis_index("y"), lax.axis_index("c")
    others = [(1 - x, y), (x, 1 - y), (1 - x, 1 - y)]
    return x, y, c, 2 * x + y, others


def _remote(src, dst, send_sem, recv_sem, dev):
    return pltpu.make_async_remote_copy(src_ref=src, dst_ref=dst, send_sem=send_sem, recv_sem=recv_sem,
                                        device_id=dev, device_id_type=MESH)


HBM = pl.BlockSpec(memory_space=pltpu.HBM)
SEM = pl.BlockSpec(memory_space=pltpu.SEMAPHORE)
EFFECT = pltpu.SideEffectType.DATAFLOW_SIDE_EFFECTING
TOKEN = jax.ShapeDtypeStruct((8, LANES), F32)


def _in_hbm(a):
    return pltpu.with_memory_space_constraint(a, pltpu.HBM)


def _gather_copies(bufs, send_sems, recv_sems):
    x, y, c, me, others = _place()
    out = []
    for w, buf in enumerate(bufs):
        half = buf.shape[1] // 2
        mine = pl.ds(c * half, half)
        for k, (ox, oy) in enumerate(others):
            sems = send_sems.at[3 * w + k], recv_sems.at[3 * w + k]
            out.append((_remote(buf.at[me, mine], buf.at[me, mine], *sems, (ox, oy, c)),
                        _remote(buf.at[me, mine], buf.at[2 * ox + oy, mine], *sems, (ox, oy, c))))
    return out


def _forward_copies(bufs, send_sems, recv_sems):
    x, y, c, me, others = _place()
    out = []
    for w, buf in enumerate(bufs):
        half = buf.shape[1] // 2
        mine, theirs = pl.ds(c * half, half), pl.ds((1 - c) * half, half)
        for k, (ox, oy) in enumerate(others):
            sems = send_sems.at[3 * w + k], recv_sems.at[3 * w + k]
            slot = 2 * ox + oy
            out.append((_remote(buf.at[slot, mine], buf.at[slot, mine], *sems, (x, y, 1 - c)),
                        _remote(buf.at[slot, mine], buf.at[slot, theirs], *sems, (x, y, 1 - c))))
    return out


def _join_copies(grads, send_sems, recv_sems):
    x, y, c, _, _ = _place()
    out = []
    for w, g in enumerate(grads):
        half = g.shape[0] // 2
        mine, theirs = pl.ds(c * half, half), pl.ds((1 - c) * half, half)
        sems = send_sems.at[w], recv_sems.at[w]
        out.append((_remote(g.at[mine], g.at[mine], *sems, (x, y, 1 - c)), _remote(g.at[mine], g.at[theirs], *sems, (x, y, 1 - c))))
    return out


def _slot_copies(bufs, send_sems, recv_sems):
    x, y, c, me, others = _place()
    out = []
    for w, buf in enumerate(bufs):
        for k, (ox, oy) in enumerate(others):
            sems = send_sems.at[3 * w + k], recv_sems.at[3 * w + k]
            out.append((_remote(buf.at[me], buf.at[me], *sems, (ox, oy, c)),
                        _remote(buf.at[me], buf.at[2 * ox + oy], *sems, (ox, oy, c))))
    return out


def _ring_copies(stage):
    def copies(bufs, send_sems, recv_sems):
        x, y, c, me, _ = _place()
        x_nb, y_nb = (1 - x, y, c), (x, 1 - y, c)
        slot_x, slot_y, slot_d = 2 * (1 - x) + y, 2 * x + (1 - y), 2 * (1 - x) + (1 - y)
        out = []
        for w, buf in enumerate(bufs):
            half = buf.shape[1] // 2
            sems = [(send_sems.at[2 * w + k], recv_sems.at[2 * w + k]) for k in range(2)]
            if stage == 1:
                mine = pl.ds(c * half, half)
                out += [(_remote(buf.at[me, mine], buf.at[me, mine], *sems[0], x_nb),
                         _remote(buf.at[me, mine], buf.at[slot_x, mine], *sems[0], x_nb)),
                        (_remote(buf.at[me, mine], buf.at[me, mine], *sems[1], y_nb),
                         _remote(buf.at[me, mine], buf.at[slot_y, mine], *sems[1], y_nb))]
            else:
                first = pl.ds(c * half, half // 2)
                second = pl.ds(c * half + half // 2, half // 2)
                out += [(_remote(buf.at[slot_x, first], buf.at[slot_x, first], *sems[0], y_nb),
                         _remote(buf.at[slot_x, first], buf.at[slot_d, first], *sems[0], y_nb)),
                        (_remote(buf.at[slot_y, second], buf.at[slot_y, second], *sems[1], x_nb),
                         _remote(buf.at[slot_y, second], buf.at[slot_d, second], *sems[1], x_nb))]
        return out
    return copies


IN_PLACE = dict(gather=(_gather_copies, 3), forward=(_forward_copies, 3), join=(_join_copies, 1), slots=(_slot_copies, 3),
                ring1=(_ring_copies(1), 2), ring2=(_ring_copies(2), 2))


def copies_start(kind, bufs, name):
    n = len(bufs)
    copies, per_buf = IN_PLACE[kind]

    def body(*refs):
        ins, (send_sems, recv_sems), token = refs[:n], refs[n:n + 2], refs[-1]
        for sent, _ in copies(ins, send_sems, recv_sems):
            sent.start()
        token[...] = jnp.zeros_like(token)

    outs = pl.pallas_call(
        body, name=name,
        out_shape=(pltpu.SemaphoreType.DMA((per_buf * n,)), pltpu.SemaphoreType.DMA((per_buf * n,)),
                   *[pltpu.HBM(b.shape, b.dtype) for b in bufs], TOKEN),
        in_specs=[HBM] * n, out_specs=(SEM, SEM, *[HBM] * n, VM),
        input_output_aliases={w: 2 + w for w in range(n)},
        compiler_params=pltpu.CompilerParams(has_side_effects=EFFECT))(*[_in_hbm(b) for b in bufs])
    return outs[0], outs[1], list(outs[2:2 + n]), outs[-1]


def copies_wait(kind, send_sems, recv_sems, bufs, after, name):
    n = len(bufs)
    copies, _ = IN_PLACE[kind]

    def body(*refs):
        ins, (send_ref, recv_ref) = refs[:n], refs[n:n + 2]
        for sent, landed in copies(ins, send_ref, recv_ref):
            sent.wait_send()
            landed.wait_recv()

    return list(pl.pallas_call(
        body, name=name, out_shape=tuple(pltpu.HBM(b.shape, b.dtype) for b in bufs),
        in_specs=[HBM] * n + [SEM, SEM, ANY], out_specs=(HBM,) * n,
        input_output_aliases={w: w for w in range(n)},
        compiler_params=pltpu.CompilerParams(has_side_effects=EFFECT))(*bufs, send_sems, recv_sems, after))


def swap_halves(grads, name):
    n = len(grads)

    def body(*refs):
        ins, outs = refs[:n], refs[n:2 * n]
        send_sems, recv_sems = refs[2 * n:]
        x, y, c, _, _ = _place()
        copies = []
        for w in range(n):
            half = ins[w].shape[1] // 2
            cp = _remote(ins[w].at[:, pl.ds((1 - c) * half, half)], outs[w], send_sems.at[w], recv_sems.at[w], (x, y, 1 - c))
            cp.start()
            copies.append(cp)
        for cp in copies:
            cp.wait()

    return pl.pallas_call(
        body, out_shape=tuple(jax.ShapeDtypeStruct((N_CHIPS, g.shape[1] // 2, g.shape[2]), g.dtype) for g in grads),
        in_specs=[ANY] * n, out_specs=(ANY,) * n,
        scratch_shapes=[pltpu.SemaphoreType.DMA((n,)), pltpu.SemaphoreType.DMA((n,))], name=name)(*grads)


def _swap_copies(grads, lands, send_sems, recv_sems):
    x, y, c, _, _ = _place()
    out = []
    for w, (g, land) in enumerate(zip(grads, lands)):
        half = g.shape[1] // 2
        out.append(_remote(g.at[:, pl.ds((1 - c) * half, half)], land, send_sems.at[w], recv_sems.at[w], (x, y, 1 - c)))
    return out


def swap_start(grads, name):
    n = len(grads)

    def body(*refs):
        ins, lands, (send_sems, recv_sems), token = refs[:n], refs[n:2 * n], refs[2 * n:2 * n + 2], refs[-1]
        for cp in _swap_copies(ins, lands, send_sems, recv_sems):
            cp.start()
        token[...] = jnp.zeros_like(token)

    shapes = [(N_CHIPS, g.shape[1] // 2, g.shape[2]) for g in grads]
    zones = [_in_hbm(lax.empty(s, g.dtype)) for s, g in zip(shapes, grads)]
    outs = pl.pallas_call(
        body, name=name,
        out_shape=(pltpu.SemaphoreType.DMA((n,)), pltpu.SemaphoreType.DMA((n,)),
                   *[pltpu.HBM(g.shape, g.dtype) for g in grads], *[pltpu.HBM(s, g.dtype) for s, g in zip(shapes, grads)],
                   TOKEN),
        in_specs=[HBM] * (2 * n), out_specs=(SEM, SEM, *[HBM] * (2 * n), VM),
        input_output_aliases={w: 2 + w for w in range(2 * n)},
        compiler_params=pltpu.CompilerParams(has_side_effects=EFFECT))(*[_in_hbm(g) for g in grads], *zones)
    return outs[0], outs[1], list(outs[2:2 + n]), list(outs[2 + n:2 + 2 * n]), outs[-1]


def swap_wait(send_sems, recv_sems, grads, lands, after, name):
    n = len(grads)

    def body(*refs):
        ins, zones, (send_ref, recv_ref) = refs[:n], refs[n:2 * n], refs[2 * n:2 * n + 2]
        for cp in _swap_copies(ins, zones, send_ref, recv_ref):
            cp.wait_send()
            cp.wait_recv()

    outs = pl.pallas_call(
        body, name=name, out_shape=tuple(pltpu.HBM(a.shape, a.dtype) for a in list(grads) + list(lands)),
        in_specs=[HBM] * (2 * n) + [SEM, SEM, ANY], out_specs=(HBM,) * (2 * n),
        input_output_aliases={w: w for w in range(2 * n)},
        compiler_params=pltpu.CompilerParams(has_side_effects=EFFECT))(*grads, *lands, send_sems, recv_sems, after)
    return list(outs[:n]), list(outs[n:])


def _exchange_copies(sums, lands, send_sems, recv_sems):
    x, y, c, me, others = _place()
    out = []
    for w, (src, land) in enumerate(zip(sums, lands)):
        for k, (ox, oy) in enumerate(others):
            sems = send_sems.at[3 * w + k], recv_sems.at[3 * w + k]
            out.append((_remote(src.at[2 * ox + oy], land.at[me], *sems, (ox, oy, c)),
                        _remote(src.at[2 * ox + oy], land.at[2 * ox + oy], *sems, (ox, oy, c))))
    return out


def exchange_start(sums, name):
    n = len(sums)

    def body(*refs):
        ins, lands, (send_sems, recv_sems), token = refs[:n], refs[n:2 * n], refs[2 * n:2 * n + 2], refs[-1]
        for sent, _ in _exchange_copies(ins, lands, send_sems, recv_sems):
            sent.start()
        token[...] = jnp.zeros_like(token)

    zones = [_in_hbm(lax.empty(s.shape, s.dtype)) for s in sums]
    outs = pl.pallas_call(
        body, name=name,
        out_shape=(pltpu.SemaphoreType.DMA((3 * n,)), pltpu.SemaphoreType.DMA((3 * n,)),
                   *[pltpu.HBM(s.shape, s.dtype) for s in sums] * 2, TOKEN),
        in_specs=[HBM] * (2 * n), out_specs=(SEM, SEM, *[HBM] * (2 * n), VM),
        input_output_aliases={w: 2 + w for w in range(2 * n)},
        compiler_params=pltpu.CompilerParams(has_side_effects=EFFECT))(*[_in_hbm(s) for s in sums], *zones)
    return outs[0], outs[1], list(outs[2:2 + n]), list(outs[2 + n:2 + 2 * n]), outs[-1]


def exchange_wait(send_sems, recv_sems, sums, lands, after, name):
    n = len(sums)

    def body(*refs):
        ins, zones, (send_ref, recv_ref) = refs[:n], refs[n:2 * n], refs[2 * n:2 * n + 2]
        for sent, landed in _exchange_copies(ins, zones, send_ref, recv_ref):
            sent.wait_send()
            landed.wait_recv()

    outs = pl.pallas_call(
        body, name=name, out_shape=tuple(pltpu.HBM(s.shape, s.dtype) for s in sums) * 2,
        in_specs=[HBM] * (2 * n) + [SEM, SEM, ANY], out_specs=(HBM,) * (2 * n),
        input_output_aliases={w: w for w in range(2 * n)},
        compiler_params=pltpu.CompilerParams(has_side_effects=EFFECT))(*sums, *lands, send_sems, recv_sems, after)
    return list(outs[:n]), list(outs[n:])


VM = pl.BlockSpec(memory_space=pltpu.VMEM)


def small_allgather(buf, name="small_allgather"):
    def body(in_ref, out_ref, send_sems, recv_sems):
        x, y, c, me, others = _place()
        out_ref[me] = in_ref[...]
        copies = []
        for k, (ox, oy) in enumerate(others):
            cp = _remote(in_ref, out_ref.at[me], send_sems.at[k], recv_sems.at[k], (ox, oy, c))
            cp.start()
            copies.append(cp)
        for k, (ox, oy) in enumerate(others):
            landed = out_ref.at[2 * ox + oy]
            _remote(landed, landed, send_sems.at[k], recv_sems.at[k], (ox, oy, c)).wait_recv()
        for cp in copies:
            cp.wait_send()

    return pl.pallas_call(
        body, out_shape=jax.ShapeDtypeStruct((N_CHIPS,) + buf.shape, buf.dtype), in_specs=[VM], out_specs=VM,
        scratch_shapes=[pltpu.SemaphoreType.DMA((3,)), pltpu.SemaphoreType.DMA((3,))],
        compiler_params=pltpu.CompilerParams(vmem_limit_bytes=V7X_VMEM_LIMIT), name=name)(buf)


def pair_small(buf, after, name="pair_small"):
    def body(in_ref, after_ref, out_ref, sib_ref, send_sem, recv_sem):
        x, y, c, me, _ = _place()
        cp = _remote(in_ref, sib_ref, send_sem, recv_sem, (x, y, 1 - c))
        cp.start()
        cp.wait()
        out_ref[me] = in_ref[...] + sib_ref[...]

    return pl.pallas_call(
        body, out_shape=jax.ShapeDtypeStruct((N_CHIPS,) + buf.shape, buf.dtype), in_specs=[VM, ANY], out_specs=VM,
        scratch_shapes=[pltpu.VMEM(buf.shape, buf.dtype), pltpu.SemaphoreType.DMA, pltpu.SemaphoreType.DMA],
        compiler_params=pltpu.CompilerParams(vmem_limit_bytes=V7X_VMEM_LIMIT), name=name)(buf, after)


def sum_slots(slots, name="sum_slots"):
    _, rows, cols = slots.shape
    tr = _pick(rows, (128, 8))

    def body(s_ref, o_ref):
        o_ref[...] = ((s_ref[0] + s_ref[1]) + s_ref[2]) + s_ref[3]

    return pl.pallas_call(
        body, out_shape=jax.ShapeDtypeStruct((rows, cols), slots.dtype), grid=(rows // tr,),
        in_specs=[pl.BlockSpec((N_CHIPS, tr, cols), lambda i: (0, i, 0))],
        out_specs=pl.BlockSpec((tr, cols), lambda i: (i, 0)), compiler_params=_cp("parallel"), name=name)(slots)


def _pack_rows(arrays, row_multiple):
    flat = jnp.concatenate([a.reshape(-1) for a in arrays])
    rows = -(-flat.shape[0] // (LANES * row_multiple)) * row_multiple
    return jnp.pad(flat, (0, rows * LANES - flat.shape[0])).reshape(rows, LANES)


def _unpack_rows(buf, shapes):
    flat = buf.reshape(-1)
    out, at = [], 0
    for s in shapes:
        n = math.prod(s)
        out.append(flat[at:at + n].reshape(s))
        at += n
    return out


def _mixer_weights(i):
    j = i // 2
    mixer = "a" if i % 2 == 0 else "b"
    return [("w_mem_kv", i), (mixer + "_w_in", j), (mixer + "_w_out", j)]


def _ffn_weights(i):
    return [("ffn_w_up", i), ("ffn_w_down", i)]


def _mixer_fwd(i, x, mem, w, small, after):
    is_a = i % 2 == 0
    j = i // 2
    wkv, win, wout = w
    wkv = wkv.reshape(1, D_MODEL, 2 * MEM_W)
    h1 = rms_fwd(x, small["mix_norm_g"][i], BF16, name=f"mix_norm{i}", after=after)
    mem_n = rms_fwd(mem, small["mem_norm_g"][i], BF16, name=f"mem_norm{i}")
    kv = mm_nn(mem_n, wkv, F32, name=f"mem_kv{i}")
    proj = mm_nn(h1, win, F32, name=f"in_proj{i}")
    saved = dict(x0=x, h1=h1, mem_n=mem_n, kv=kv, proj=proj)
    if is_a:
        outs, lses = zip(*[attn_fwd(proj, g, name=f"attn_fwd{i}_{g}") for g in range(3)])
        comb, lse = attn_combine(outs, lses, name=f"attn_combine{i}")
        mem_out = mem_fwd(proj, 3 * A_QKV_W, kv, name=f"mem_fwd{i}")
        cat = jnp.concatenate([comb.astype(BF16), mem_out], axis=1)
        saved.update(comb=comb, lse=lse)
    else:
        wout = wout.reshape(1, B_W + MEM_W, D_MODEL)
        tok = sgu_fwd(proj, small["b_v_norm_g"][j], small["b_w_s"][j], small["bias_b"][j], name=f"sgu_fwd{i}")
        mem_out = mem_fwd(proj, 2 * B_W, kv, name=f"mem_fwd{i}")
        cat = jnp.concatenate([tok, mem_out], axis=1)
    x1 = mm_nn(cat, wout, F32, res=x, name=f"out_proj{i}")
    saved.update(cat=cat)
    return x1, saved


def _ffn_fwd(i, x1, w, small, after):
    wup, wdn = w
    h2 = rms_fwd(x1, small["ffn_norm_g"][i], BF16, name=f"ffn_norm{i}", after=after)
    a = mm_nn(h2, wup, F32, name=f"ffn_up{i}")
    act = ffn_act_fwd(a, small["ffn_conv_w"][i], small["ffn_conv_b"][i], name=f"ffn_act{i}")
    x2 = mm_nn(act, wdn.reshape(1, FF, D_MODEL), F32, res=x1, name=f"ffn_down{i}")
    return x2, dict(x1=x1, h2=h2, a=a, act=act)


def _ffn_bwd(i, dx2, w, small, sv, after):
    wup, wdn = w
    dx2, dx2_16 = dx2
    sg = {}
    dact = mm_nt(dx2_16, wdn.reshape(1, FF, D_MODEL), F32, name=f"d_act{i}", after=after)
    d_wdn = mm_tn(sv["act"], dx2_16, 1, BF16, name=f"d_wdown{i}").reshape(N_CHIPS, FF // N_CHIPS, D_MODEL)
    da, sg["ffn_conv_w"], sg["ffn_conv_b"] = ffn_act_bwd(sv["a"], small["ffn_conv_w"][i], small["ffn_conv_b"][i], dact,
                                                          name=f"ffn_act_bwd{i}")
    d_wup = mm_tn(sv["h2"], da, N_CHIPS, BF16, name=f"d_wup{i}")
    dh2 = mm_nt(da, wup, F32, name=f"d_h2_{i}")
    dx1, dx1_16, sg["ffn_norm_g"] = rms_bwd(sv["x1"], small["ffn_norm_g"][i], dh2, dres=dx2, name=f"ffn_norm_bwd{i}")
    return (dx1, dx1_16), [d_wup, d_wdn], sg


def _mixer_bwd(i, dx1, mem, w, small, sv, after):
    is_a = i % 2 == 0
    j = i // 2
    wkv, win, wout = w
    wkv = wkv.reshape(1, D_MODEL, 2 * MEM_W)
    dx1_32, dx1 = dx1
    sg = {}
    proj, kv = sv["proj"], sv["kv"]
    if is_a:
        dcat = mm_nt(dx1, wout, F32, name=f"d_cat{i}", after=after)
        d_wout = mm_tn(sv["cat"], dx1, N_CHIPS, BF16, name=f"d_wout{i}")
        dqm, dkv = mem_bwd(proj, 3 * A_QKV_W, kv, dcat, A_OUT_W, name=f"mem_bwd{i}")
        parts = [attn_bwd(proj, dcat, sv["comb"], sv["lse"], g, name=f"attn_bwd{i}_{g}") for g in range(3)]
        dproj = jnp.concatenate([parts[g][p].astype(BF16) for p in range(3) for g in range(3)] + [dqm], axis=1)
    else:
        dcat = mm_nt(dx1, wout.reshape(1, B_W + MEM_W, D_MODEL), F32, name=f"d_cat{i}", after=after)
        d_wout = mm_tn(sv["cat"], dx1, 1, BF16, name=f"d_wout{i}").reshape(N_CHIPS, (B_W + MEM_W) // N_CHIPS, D_MODEL)
        dqm, dkv = mem_bwd(proj, 2 * B_W, kv, dcat, B_W, name=f"mem_bwd{i}")
        w_s = small["b_w_s"][j]
        duv, sg["b_w_s"], dmix, sg["b_v_norm_g"] = sgu_bwd(proj, small["b_v_norm_g"][j], w_s, jnp.swapaxes(w_s, 1, 2),
                                                           small["bias_b"][j], dcat, name=f"sgu_bwd{i}")
        sg["b_s_bias"] = jnp.sum(dmix, axis=-1)
        dproj = jnp.concatenate([duv, dqm], axis=1)
    d_wkv = mm_tn(sv["mem_n"], dkv, 1, BF16, name=f"d_wkv{i}").reshape(N_CHIPS, D_MODEL // N_CHIPS, 2 * MEM_W)
    dmem_n = mm_nt(dkv, wkv, F32, name=f"d_mem_n{i}")
    _, _, sg["mem_norm_g"] = rms_bwd(mem, small["mem_norm_g"][i], dmem_n, name=f"mem_norm_bwd{i}")
    d_win = mm_tn(sv["h1"], dproj, N_CHIPS, BF16, name=f"d_win{i}")
    dh1 = mm_nt(dproj, win, F32, name=f"d_h1_{i}")
    dx0, dx0_16, sg["mix_norm_g"] = rms_bwd(sv["x0"], small["mix_norm_g"][i], dh1, dres=dx1_32, name=f"mix_norm_bwd{i}")
    return (dx0, dx0_16), [d_wkv, d_win, d_wout], sg


def _exchange_begin(grads, got, place, tag):
    sums = [pair_sum(g, o, place[1:], name=f"pair_sum{tag}_{k}") for k, (g, o) in enumerate(zip(grads, got))]
    send_sems, recv_sems, sums, lands, token = exchange_start(sums, name=f"exchange_start{tag}")
    return (send_sems, recv_sems, sums, lands), token


def _reduce_finish(started, place, after, tag):
    sums, parts = exchange_wait(*started, after, name=f"exchange_wait{tag}")
    halves = [chip_sum(s, p, place, name=f"chip_sum{tag}_{k}") for k, (s, p) in enumerate(zip(sums, parts))]
    return copies_start("join", halves, name=f"join_start_{tag}")


SMALL_SHARDED = ("b_v_norm_g", "ffn_conv_w")
SMALL_FULL_SHAPES = dict(mix_norm_g=(D_MODEL,), ffn_norm_g=(D_MODEL,), mem_norm_g=(D_MODEL,), b_v_norm_g=(B_W,),
                         b_w_s=(B_GROUPS, CHUNK, CHUNK), b_s_bias=(B_GROUPS, CHUNK), ffn_conv_w=(3, 2 * FF),
                         ffn_conv_b=(2 * FF,))
BIG = ("w_mem_kv", "a_w_in", "a_w_out", "b_w_in", "b_w_out", "ffn_w_up", "ffn_w_down")
WEIGHT_ORDER = ("mix_norm_g", "ffn_norm_g", "mem_norm_g", "w_mem_kv", "a_w_in", "a_w_out", "b_w_in", "b_v_norm_g", "b_w_s",
                "b_s_bias", "b_w_out", "ffn_w_up", "ffn_conv_w", "ffn_conv_b", "ffn_w_down", "final_norm_g")


def kernel(x, mem, mix_norm_g, ffn_norm_g, mem_norm_g, w_mem_kv, a_w_in, a_w_out, b_w_in, b_v_norm_g, b_w_s, b_s_bias, b_w_out, ffn_w_up, ffn_conv_w, ffn_conv_b, ffn_w_down, final_norm_g, loss_target, m_mix_norm_g, m_ffn_norm_g, m_mem_norm_g, m_w_mem_kv, m_a_w_in, m_a_w_out, m_b_w_in, m_b_v_norm_g, m_b_w_s, m_b_s_bias, m_b_w_out, m_ffn_w_up, m_ffn_conv_w, m_ffn_conv_b, m_ffn_w_down, m_final_norm_g, v_mix_norm_g, v_ffn_norm_g, v_mem_norm_g, v_w_mem_kv, v_a_w_in, v_a_w_out, v_b_w_in, v_b_v_norm_g, v_b_w_s, v_b_s_bias, v_b_w_out, v_ffn_w_up, v_ffn_conv_w, v_ffn_conv_b, v_ffn_w_down, v_final_norm_g):
    weights = dict(mix_norm_g=mix_norm_g, ffn_norm_g=ffn_norm_g, mem_norm_g=mem_norm_g, w_mem_kv=w_mem_kv, a_w_in=a_w_in,
                   a_w_out=a_w_out, b_w_in=b_w_in, b_v_norm_g=b_v_norm_g, b_w_s=b_w_s, b_s_bias=b_s_bias, b_w_out=b_w_out,
                   ffn_w_up=ffn_w_up, ffn_conv_w=ffn_conv_w, ffn_conv_b=ffn_conv_b, ffn_w_down=ffn_w_down,
                   final_norm_g=final_norm_g)
    mom1 = dict(mix_norm_g=m_mix_norm_g, ffn_norm_g=m_ffn_norm_g, mem_norm_g=m_mem_norm_g, w_mem_kv=m_w_mem_kv,
                a_w_in=m_a_w_in, a_w_out=m_a_w_out, b_w_in=m_b_w_in, b_v_norm_g=m_b_v_norm_g, b_w_s=m_b_w_s,
                b_s_bias=m_b_s_bias, b_w_out=m_b_w_out, ffn_w_up=m_ffn_w_up, ffn_conv_w=m_ffn_conv_w,
                ffn_conv_b=m_ffn_conv_b, ffn_w_down=m_ffn_w_down, final_norm_g=m_final_norm_g)
    mom2 = dict(mix_norm_g=v_mix_norm_g, ffn_norm_g=v_ffn_norm_g, mem_norm_g=v_mem_norm_g, w_mem_kv=v_w_mem_kv,
                a_w_in=v_a_w_in, a_w_out=v_a_w_out, b_w_in=v_b_w_in, b_v_norm_g=v_b_v_norm_g, b_w_s=v_b_w_s,
                b_s_bias=v_b_s_bias, b_w_out=v_b_w_out, ffn_w_up=v_ffn_w_up, ffn_conv_w=v_ffn_conv_w,
                ffn_conv_b=v_ffn_conv_b, ffn_w_down=v_ffn_w_down, final_norm_g=v_final_norm_g)
    chip = 2 * lax.axis_index("x") + lax.axis_index("y")
    place = jnp.stack([chip, lax.axis_index("c")]).astype(jnp.int32)
    x0, mem0, target = x[0], mem[0], loss_target[0]
    depth = DEPTH

    n_cw, n_vg = ffn_conv_w.size, b_v_norm_g.size
    gathered = small_allgather(_pack_rows([ffn_conv_w, b_v_norm_g], 8)).reshape(N_CHIPS, -1)
    conv_w_full = gathered[:, :n_cw].reshape(N_CHIPS, DEPTH, 3, 2 * FF // N_CHIPS).transpose(1, 2, 0, 3).reshape(DEPTH, 3, 2 * FF)
    vgain_full = gathered[:, n_cw:n_cw + n_vg].reshape(N_CHIPS, 2, B_W // N_CHIPS).transpose(1, 0, 2).reshape(2, B_W)
    small = dict(mix_norm_g=mix_norm_g, ffn_norm_g=ffn_norm_g, mem_norm_g=mem_norm_g, b_w_s=b_w_s, ffn_conv_b=ffn_conv_b,
                 ffn_conv_w=conv_w_full, b_v_norm_g=vgain_full,
                 bias_b=jnp.broadcast_to(b_s_bias[..., None], b_s_bias.shape + (CHUNK,)))

    half_layers = 2 * depth
    groups = [(_ffn_weights if b % 2 else _mixer_weights)(b // 2) for b in range(half_layers)]
    tags = [("f" if b % 2 else "m") + str(b // 2) for b in range(half_layers)]

    ring_from, early_from = 3, 6
    kind = lambda b: "gather" if b < ring_from else "ring1"

    def start_gather(b, after):
        bufs = [cast_to_slot(weights[n], l, place, name=f"cast_{n}{l}", after=after) for n, l in groups[b]]
        return copies_start(kind(b), bufs, name=f"{kind(b)}_start_{tags[b]}")

    def next_stage(b, now, new, after):
        send_sems, recv_sems, bufs, _ = flying.pop(b)
        bufs = copies_wait(now, send_sems, recv_sems, bufs, after, name=f"{now}_wait_{tags[b]}")
        flying[b] = copies_start(new, bufs, name=f"{new}_start_{tags[b]}")
        return flying[b][3]

    flying, tie = {}, gathered
    for b in range(3):
        flying[b] = start_gather(b, tie)
        tie = flying[b][3]
    w_half, saved_half, h = [], [], x0
    for b in range(half_layers):
        behind = tie if b == 0 else h
        if b < early_from:
            behind = next_stage(b, "gather" if b < ring_from else "ring2", "forward", behind)
        w_half.append(copies_wait("forward", *flying.pop(b)[:3], behind, name=f"forward_wait_{tags[b]}"))
        ready, tokens = w_half[b][0], []
        if early_from <= b + 1 < half_layers:
            tokens.append(next_stage(b + 1, "ring2", "forward", ready))
        for g in (b + 1, b + 2):
            if g < half_layers and ((g == b + 1 and ring_from <= g < early_from) or (g == b + 2 and g >= early_from)):
                tokens.append(next_stage(g, "ring1", "ring2", ready))
        if b + 3 < half_layers:
            flying[b + 3] = start_gather(b + 3, sum(tokens[1:], tokens[0]) if tokens else ready)
            tokens.append(flying[b + 3][3])
        tie = sum(tokens[1:], tokens[0]) if tokens else None
        if b % 2 == 0:
            h, sv = _mixer_fwd(b // 2, h, mem0, w_half[b], small, tie)
        else:
            h, sv = _ffn_fwd(b // 2, h, w_half[b], small, tie)
        saved_half.append(sv)
    w_mix, w_ffn, saved_mix, saved_ffn = w_half[0::2], w_half[1::2], saved_half[0::2], saved_half[1::2]
    loss_row, *dh, d_final = final_loss(h, final_norm_g, target)
    loss = lax.psum(loss_row[0, 0], ("x", "y", "c"))

    names = [n for n in WEIGHT_ORDER if n not in BIG]
    small_g = {n: [None] * weights[n].shape[0] for n in names if n != "final_norm_g"}
    big_out = {n: None for n in BIG}

    def keep_small(i, sg):
        for n, g in sg.items():
            small_g[n][i if len(small_g[n]) == depth else i // 2] = g.reshape(SMALL_FULL_SHAPES[n])

    joining = []

    def update(after):
        (send_sems, recv_sems, halves, _), group, tag = joining.pop()
        for (n, l), g in zip(group, copies_wait("join", send_sems, recv_sems, halves, after, name=f"join_wait_{tag}")):
            big_out[n] = adamw_layer(weights[n], mom1[n], mom2[n], l, g, big_out[n], name=f"adamw_{n}{l}")

    def finish_reduce(started, group, after, tag):
        join = _reduce_finish(started, place, after, tag)
        if joining:
            update(join[3])
        joining.append((join, group, tag))

    half_layers = 2 * depth
    swapping, exchanging, tie = None, [], None
    for k in range(half_layers):
        i = depth - 1 - k // 2
        if k % 2 == 0:
            dh, big_g, sg = _ffn_bwd(i, dh, w_ffn[i], small, saved_ffn[i], tie)
            group, tag = _ffn_weights(i), f"f{i}"
        else:
            dh, big_g, sg = _mixer_bwd(i, dh, mem0, w_mix[i], small, saved_mix[i], tie)
            group, tag = _mixer_weights(i), f"m{i}"
        keep_small(i, sg)
        started_now, swap_now, tokens = [], None, []
        if k < half_layers - 2:
            *swap_now, token = swap_start(big_g, name=f"swap_start_{tag}")
            swap_now = (swap_now, group, tag)
        else:
            started, token = _exchange_begin(big_g, swap_halves(big_g, name=f"swap_halves_{tag}"), place, tag)
            started_now.append((started, group, tag))
        tokens.append(token)
        if swapping is not None:
            swap_args, old_group, old_tag = swapping
            grads, got = swap_wait(*swap_args, dh[0], name=f"swap_wait_{old_tag}")
            started, token = _exchange_begin(grads, got, place, old_tag)
            started_now.append((started, old_group, old_tag))
            tokens.append(token)
        tie = sum(tokens[1:], tokens[0])
        for started, old_group, old_tag in exchanging:
            finish_reduce(started, old_group, dh[0], old_tag)
        swapping, exchanging = swap_now, started_now
    for started, old_group, old_tag in exchanging:
        finish_reduce(started, old_group, big_out["ffn_w_down"][0], old_tag)

    full_g = {n: (d_final.reshape(-1) if n == "final_norm_g" else jnp.stack(small_g[n])) for n in names}
    shapes = [full_g[n].shape for n in names]
    flying_small = copies_start("slots", [pair_small(_pack_rows([full_g[n] for n in names], 8), tie)], name="small_start")
    update(flying_small[3])
    reduced = sum_slots(copies_wait("slots", *flying_small[:3], big_out["w_mem_kv"][0], name="small_wait")[0])
    summed = dict(zip(names, _unpack_rows(reduced, shapes)))
    for n in SMALL_SHARDED:
        width = weights[n].shape[-1]
        summed[n] = lax.dynamic_slice_in_dim(summed[n], chip * width, width, axis=summed[n].ndim - 1)
    own_shapes = [weights[n].shape for n in names]
    pack = lambda d: _pack_rows([d[n] for n in names], 128)
    small_out = [_unpack_rows(b, own_shapes) for b in adamw_flat(pack(weights), pack(summed), pack(mom1), pack(mom2))]
    outs = {}
    for k, n in enumerate(names):
        outs[n] = (summed[n], small_out[0][k], small_out[1][k], small_out[2][k])
    outs.update(big_out)
    return (loss, dh[0][None], *[outs[n][0] for n in WEIGHT_ORDER], *[outs[n][1] for n in WEIGHT_ORDER],
            *[outs[n][2] for n in WEIGHT_ORDER], *[outs[n][3] for n in WEIGHT_ORDER])
```

```python
import math

import numpy as np
import jax
import jax.numpy as jnp
from jax import lax
from jax.experimental import pallas as pl
from jax.experimental.pallas import tpu as pltpu

F32 = jnp.float32
BF16 = jnp.bfloat16
MESH = pl.DeviceIdType.MESH

D_MODEL = 2048
SEQ = 2048
DEPTH = 4
EPS = 1e-6
NEG = -1e30
HEAD = 128
A_PATTERNS = ((128, 1), (512, 4), (2048, 16))
A_QKV_W = 1536
A_OUT_W = 512
A_IN = 5120
QBLK = 128
N_SIDE = 64
CHUNK = 128
B_GROUPS = 12
B_W = 1536
B_IN = 3584
MEM_LEN = 256
MEM_HEADS = 4
MEM_W = 512
FF = 5632
ADAM_LR, ADAM_B1, ADAM_B2, ADAM_EPS, ADAM_WD, ADAM_STEP = 0.001, 0.9, 0.999, 1e-08, 0.01, 10
N_CHIPS = 4

LANES = 128
V7X_VMEM_LIMIT = 56 * 1024 * 1024


def _cp(*sem):
    return pltpu.CompilerParams(dimension_semantics=sem, vmem_limit_bytes=V7X_VMEM_LIMIT)


def _pick(dim, prefs):
    for p in prefs:
        if dim % p == 0:
            return p
    raise ValueError(f"no tile for {dim} in {prefs}")


def _gelu_parts(x):
    cdf = 0.5 * (1.0 + lax.erf(x * (1.0 / math.sqrt(2.0))))
    pdf = jnp.exp(-0.5 * x * x) * (1.0 / math.sqrt(2.0 * math.pi))
    return x * cdf, cdf + x * pdf


def _gelu(x):
    return 0.5 * x * (1.0 + lax.erf(x * (1.0 / math.sqrt(2.0))))


TM_PREFS = (1024, 512, 256, 128)
TN_PREFS = (1408, 1280, 1024, 896, 512, 256, 128)
TK_PREFS = (2816, 2048, 1408, 1280, 1024, 896, 512, 256, 128)


def _mm_body(nk, dims, has_res, halves=None):
    def body(*refs):
        refs = list(refs)
        ops = []
        for operand in (0, 1):
            if halves is not None and halves[0] == operand:
                first, second = refs.pop(0), refs.pop(0)
                ops.append(jnp.where(pl.program_id(halves[1]) < halves[2], first[...], second[...]))
            else:
                ops.append(refs.pop(0)[...])
        r_ref = refs.pop(0) if has_res else None
        o_ref = refs.pop(0)
        part = lax.dot_general(ops[0].astype(BF16), ops[1].astype(BF16), dims, preferred_element_type=F32)
        if nk == 1:
            if has_res:
                part = part + r_ref[...]
            o_ref[...] = part.astype(o_ref.dtype)
            return
        acc_ref = refs[-1]
        k = pl.program_id(2)

        @pl.when(k == 0)
        def _():
            acc_ref[...] = part

        @pl.when(k > 0)
        def _():
            acc_ref[...] += part

        @pl.when(k == nk - 1)
        def _():
            tot = acc_ref[...]
            if has_res:
                tot = tot + r_ref[...]
            o_ref[...] = tot.astype(o_ref.dtype)
    return body


def mm_nn(a, w, out_dtype, res=None, name="mm_nn"):
    m, kw = a.shape
    ns_, kw2, nsz = w.shape
    assert kw == kw2
    n = ns_ * nsz
    tm, tn, tk = _pick(m, TM_PREFS), _pick(nsz, TN_PREFS), _pick(kw, TK_PREFS)
    nb, nk = nsz // tn, kw // tk
    in_specs = [pl.BlockSpec((tm, tk), lambda i, j, k: (i, k)),
                pl.BlockSpec((None, tk, tn), lambda i, j, k: (j // nb, k, j % nb))]
    args = [a, w]
    if res is not None:
        in_specs.append(pl.BlockSpec((tm, tn), lambda i, j, k: (i, j)))
        args.append(res)
    return pl.pallas_call(
        _mm_body(nk, (((1,), (0,)), ((), ())), res is not None),
        out_shape=jax.ShapeDtypeStruct((m, n), out_dtype),
        grid=(m // tm, n // tn, nk), in_specs=in_specs,
        out_specs=pl.BlockSpec((tm, tn), lambda i, j, k: (i, j)),
        scratch_shapes=[pltpu.VMEM((tm, tn), F32)] if nk > 1 else [],
        compiler_params=_cp("parallel", "parallel", "arbitrary"), name=name)(*args)


def mm_nt(g, w, out_dtype, name="mm_nt", after=None):
    parts = list(g) if isinstance(g, (tuple, list)) else [g]
    m, n = parts[0].shape[0], sum(p.shape[1] for p in parts)
    ns_, kw, nsz = w.shape
    assert n == ns_ * nsz
    tm, tn, tk = _pick(m, TM_PREFS), _pick(kw, TN_PREFS), _pick(nsz, TK_PREFS)
    nb, nk = nsz // tk, n // tk
    n_first = parts[0].shape[1] // tk
    if len(parts) == 1:
        g_specs = [pl.BlockSpec((tm, tk), lambda i, j, k: (i, k))]
    else:
        g_specs = [pl.BlockSpec((tm, tk), lambda i, j, k: (i, jnp.minimum(k, n_first - 1))),
                   pl.BlockSpec((tm, tk), lambda i, j, k: (i, jnp.maximum(k - n_first, 0)))]
    body = _mm_body(nk, (((1,), (1,)), ((), ())), False, None if len(parts) == 1 else (0, 2, n_first))
    tied = [] if after is None else [after]
    n_in = len(parts) + 1
    return pl.pallas_call(
        (lambda *refs: body(*refs[:n_in], *refs[n_in + len(tied):])),
        out_shape=jax.ShapeDtypeStruct((m, kw), out_dtype),
        grid=(m // tm, kw // tn, nk),
        in_specs=g_specs + [pl.BlockSpec((None, tn, tk), lambda i, j, k: (k // nb, j, k % nb))] + [ANY] * len(tied),
        out_specs=pl.BlockSpec((tm, tn), lambda i, j, k: (i, j)),
        scratch_shapes=[pltpu.VMEM((tm, tn), F32)] if nk > 1 else [],
        compiler_params=_cp("parallel", "parallel", "arbitrary"), name=name)(*parts, w, *tied)


def mm_tn(a, g, n_shards, out_dtype, name="mm_tn"):
    parts = list(g) if isinstance(g, (tuple, list)) else [g]
    t, kw = a.shape
    n = sum(p.shape[1] for p in parts)
    assert t == parts[0].shape[0]
    nsz = n // n_shards
    tm, tn, tk = _pick(kw, TM_PREFS), _pick(nsz, TN_PREFS), _pick(t, TK_PREFS)
    nb, nk = nsz // tn, t // tk
    n_first = parts[0].shape[1] // tn
    if len(parts) == 1:
        g_specs = [pl.BlockSpec((tk, tn), lambda i, j, k: (k, j))]
    else:
        g_specs = [pl.BlockSpec((tk, tn), lambda i, j, k: (k, jnp.minimum(j, n_first - 1))),
                   pl.BlockSpec((tk, tn), lambda i, j, k: (k, jnp.maximum(j - n_first, 0)))]
    return pl.pallas_call(
        _mm_body(nk, (((0,), (0,)), ((), ())), False, None if len(parts) == 1 else (1, 1, n_first)),
        out_shape=jax.ShapeDtypeStruct((n_shards, kw, nsz), out_dtype),
        grid=(kw // tm, n // tn, nk),
        in_specs=[pl.BlockSpec((tk, tm), lambda i, j, k: (k, i))] + g_specs,
        out_specs=pl.BlockSpec((None, tm, tn), lambda i, j, k: (j // nb, i, j % nb)),
        scratch_shapes=[pltpu.VMEM((tm, tn), F32)] if nk > 1 else [],
        compiler_params=_cp("parallel", "parallel", "arbitrary"), name=name)(a, *parts)


ROW_TILE = 256


def _rms_stats(x):
    r = lax.rsqrt(jnp.mean(x * x, axis=-1, keepdims=True) + EPS)
    return r, x * r


def _rms_back(xh, r, g, dh):
    u = dh * g
    return r * (u - xh * jnp.mean(u * xh, axis=-1, keepdims=True))


def rms_fwd(x, g, out_dtype, name="rms_fwd", after=None):
    rows, d = x.shape
    tr = _pick(rows, (ROW_TILE, 128))
    tied = [] if after is None else [after]

    def body(x_ref, g_ref, *rest):
        o_ref = rest[-1]
        _, xh = _rms_stats(x_ref[...])
        o_ref[...] = (xh * g_ref[...]).astype(o_ref.dtype)

    return pl.pallas_call(
        body, out_shape=jax.ShapeDtypeStruct((rows, d), out_dtype), grid=(rows // tr,),
        in_specs=[pl.BlockSpec((tr, d), lambda i: (i, 0)), pl.BlockSpec((1, d), lambda i: (0, 0))] + [ANY] * len(tied),
        out_specs=pl.BlockSpec((tr, d), lambda i: (i, 0)),
        compiler_params=_cp("parallel"), name=name)(x, g.reshape(1, d), *tied)


def rms_bwd(x, g, dh, dres=None, name="rms_bwd"):
    rows, d = x.shape
    tr = _pick(rows, (ROW_TILE, 128))
    has_res = dres is not None

    def body(*refs):
        if has_res:
            x_ref, g_ref, dh_ref, dres_ref, dx_ref, dx16_ref, dg_ref = refs
        else:
            x_ref, g_ref, dh_ref, dx_ref, dx16_ref, dg_ref = refs
        r, xh = _rms_stats(x_ref[...])
        dh_ = dh_ref[...].astype(F32)
        part = jnp.sum(dh_ * xh, axis=0, keepdims=True)

        @pl.when(pl.program_id(0) == 0)
        def _():
            dg_ref[...] = part

        @pl.when(pl.program_id(0) > 0)
        def _():
            dg_ref[...] += part

        dx = _rms_back(xh, r, g_ref[...], dh_)
        if has_res:
            dx = dx + dres_ref[...]
        dx_ref[...] = dx
        dx16_ref[...] = dx.astype(BF16)

    row_spec = pl.BlockSpec((tr, d), lambda i: (i, 0))
    vec_spec = pl.BlockSpec((1, d), lambda i: (0, 0))
    args = [x, g.reshape(1, d), dh] + ([dres] if has_res else [])
    return pl.pallas_call(
        body, out_shape=(jax.ShapeDtypeStruct((rows, d), F32), jax.ShapeDtypeStruct((rows, d), BF16),
                         jax.ShapeDtypeStruct((1, d), F32)),
        grid=(rows // tr,), in_specs=[row_spec, vec_spec, row_spec] + ([row_spec] if has_res else []),
        out_specs=(row_spec, row_spec, vec_spec), compiler_params=_cp("arbitrary"), name=name)(*args)


def final_loss(x, g, target, name="final_loss"):
    rows, d = x.shape
    tr = _pick(rows, (ROW_TILE, 128))

    def body(x_ref, g_ref, t_ref, loss_ref, dx_ref, dx16_ref, dg_ref):
        r, xh = _rms_stats(x_ref[...])
        gain = g_ref[...]
        err = xh * gain - t_ref[...]
        sq = jnp.sum(jnp.sum(err * err, axis=1, keepdims=True), axis=0, keepdims=True) * (0.5 / d)
        dy = err * (1.0 / d)
        part = jnp.sum(dy * xh, axis=0, keepdims=True)

        @pl.when(pl.program_id(0) == 0)
        def _():
            dg_ref[...] = part
            loss_ref[...] = jnp.broadcast_to(sq, loss_ref.shape)

        @pl.when(pl.program_id(0) > 0)
        def _():
            dg_ref[...] += part
            loss_ref[...] += jnp.broadcast_to(sq, loss_ref.shape)

        dx = _rms_back(xh, r, gain, dy)
        dx_ref[...] = dx
        dx16_ref[...] = dx.astype(BF16)

    row_spec = pl.BlockSpec((tr, d), lambda i: (i, 0))
    vec_spec = pl.BlockSpec((1, d), lambda i: (0, 0))
    return pl.pallas_call(
        body, out_shape=(jax.ShapeDtypeStruct((1, LANES), F32), jax.ShapeDtypeStruct((rows, d), F32),
                         jax.ShapeDtypeStruct((rows, d), BF16), jax.ShapeDtypeStruct((1, d), F32)),
        grid=(rows // tr,), in_specs=[row_spec, vec_spec, row_spec],
        out_specs=(pl.BlockSpec((1, LANES), lambda i: (0, 0)), row_spec, row_spec, vec_spec),
        compiler_params=_cp("arbitrary"), name=name)(x, g.reshape(1, d), target)


def _alibi_slopes():
    return (2.0 ** (-8.0 * (np.arange(12) + 1) / 12)).astype(np.float32)


def _band_scores(q, k, q0, start, wk, slope):
    s = lax.dot_general(q, k, (((1,), (1,)), ((), ())), preferred_element_type=F32) * (HEAD ** -0.5)
    qpos = q0 + lax.broadcasted_iota(jnp.int32, (QBLK, wk), 0)
    kpos = start + lax.broadcasted_iota(jnp.int32, (QBLK, wk), 1)
    rel = jnp.abs(qpos - kpos)
    return jnp.where(rel <= N_SIDE, s - slope * rel.astype(F32), NEG)


def _attn_geometry(seq, dilation):
    length = seq // dilation
    return length, length // QBLK, min(2 * QBLK, length)


def _attn_window(n, length, wk):
    q0 = pl.multiple_of(n * QBLK, QBLK)
    start = pl.multiple_of(jnp.clip(n * QBLK - N_SIDE, 0, length - wk), N_SIDE)
    return q0, start


def _class_in(refs, scratch, r, dilation, length):
    if dilation == 1:
        return refs
    for ref, buf in zip(refs, scratch):
        buf[...] = ref[pl.ds(r, length, stride=dilation), :]
    return scratch


def _class_out(refs, scratch, r, dilation, length):
    if dilation > 1:
        for ref, buf in zip(refs, scratch):
            ref[pl.ds(r, length, stride=dilation), :] = buf[...]


def attn_fwd(proj, group, name):
    seq = proj.shape[0]
    dilation = A_PATTERNS[group][1]
    length, nblk, wk = _attn_geometry(seq, dilation)

    def body(slope_ref, q_ref, k_ref, v_ref, o_ref, lse_ref, *scratch):
        slope = slope_ref[group * 4 + pl.program_id(0)] * float(dilation)
        for r in range(dilation):
            q_c, k_c, v_c = _class_in((q_ref, k_ref, v_ref), scratch[:3], r, dilation, length)
            o_c, lse_c = (o_ref, lse_ref) if dilation == 1 else scratch[3:]

            def blk(n, carry):
                q0, start = _attn_window(n, length, wk)
                q = q_c[pl.ds(q0, QBLK), :].astype(BF16)
                k = k_c[pl.ds(start, wk), :].astype(BF16)
                v = v_c[pl.ds(start, wk), :].astype(BF16)
                s = _band_scores(q, k, q0, start, wk, slope)
                m = jnp.max(s, axis=-1, keepdims=True)
                p = jnp.exp(s - m)
                l = jnp.sum(p, axis=-1, keepdims=True)
                o = jnp.dot(p.astype(BF16), v, preferred_element_type=F32) / l
                o_c[pl.ds(q0, QBLK), :] = o
                lse_c[pl.ds(q0, QBLK), :] = jnp.broadcast_to(m + jnp.log(l), (QBLK, HEAD))
                return carry

            lax.fori_loop(0, nblk, blk, 0)
            _class_out((o_ref, lse_ref), scratch[3:], r, dilation, length)

    def part(p):
        return pl.BlockSpec((seq, HEAD), lambda h: (0, p * 12 + group * 4 + h))

    out_spec = pl.BlockSpec((seq, HEAD), lambda h: (0, h))
    return pl.pallas_call(
        body, out_shape=(jax.ShapeDtypeStruct((seq, A_OUT_W), F32),) * 2, grid=(4,),
        in_specs=[pl.BlockSpec(memory_space=pltpu.SMEM), part(0), part(1), part(2)],
        out_specs=(out_spec, out_spec),
        scratch_shapes=[pltpu.VMEM((length, HEAD), F32)] * (5 if dilation > 1 else 0),
        compiler_params=_cp("parallel"), name=name)(jnp.asarray(_alibi_slopes()), proj, proj, proj)


def attn_combine(os_, lses, name="attn_combine"):
    seq = os_[0].shape[0]
    tr = ROW_TILE

    def body(o0, o1, o2, l0, l1, l2, c_ref, lse_ref):
        a, b, c = l0[...], l1[...], l2[...]
        m = jnp.maximum(jnp.maximum(a, b), c)
        ea, eb, ec = jnp.exp(a - m), jnp.exp(b - m), jnp.exp(c - m)
        den = ea + eb + ec
        c_ref[...] = (ea * o0[...] + eb * o1[...] + ec * o2[...]) / den
        lse_ref[...] = m + jnp.log(den)

    spec = pl.BlockSpec((tr, A_OUT_W), lambda i: (i, 0))
    return pl.pallas_call(
        body, out_shape=(jax.ShapeDtypeStruct((seq, A_OUT_W), F32),) * 2, grid=(seq // tr,),
        in_specs=[spec] * 6, out_specs=(spec, spec), compiler_params=_cp("parallel"), name=name)(*os_, *lses)


def attn_bwd(proj, dcat, comb, lse, group, name):
    seq = proj.shape[0]
    dilation = A_PATTERNS[group][1]
    length, nblk, wk = _attn_geometry(seq, dilation)
    scale = HEAD ** -0.5

    def body(slope_ref, q_ref, k_ref, v_ref, do_ref, c_ref, lse_ref, dq_ref, dk_ref, dv_ref, *scratch):
        slope = slope_ref[group * 4 + pl.program_id(0)] * float(dilation)
        for r in range(dilation):
            q_c, k_c, v_c, do_c, c_c, lse_c = _class_in((q_ref, k_ref, v_ref, do_ref, c_ref, lse_ref), scratch[:6], r,
                                                        dilation, length)
            dq_c, dk_c, dv_c = (dq_ref, dk_ref, dv_ref) if dilation == 1 else scratch[6:]
            dk_c[...] = jnp.zeros_like(dk_c)
            dv_c[...] = jnp.zeros_like(dv_c)

            def blk(n, carry):
                q0, start = _attn_window(n, length, wk)
                rows = pl.ds(q0, QBLK)
                keys = pl.ds(start, wk)
                q = q_c[rows, :].astype(BF16)
                k = k_c[keys, :].astype(BF16)
                v = v_c[keys, :].astype(BF16)
                do = do_c[rows, :]
                s = _band_scores(q, k, q0, start, wk, slope)
                p = jnp.exp(s - lse_c[rows, :][:, :1])
                delta = jnp.sum(do * c_c[rows, :], axis=-1, keepdims=True)
                do16 = do.astype(BF16)
                dp = lax.dot_general(do16, v, (((1,), (1,)), ((), ())), preferred_element_type=F32)
                ds = (p * (dp - delta) * scale).astype(BF16)
                p16 = p.astype(BF16)
                dq_c[rows, :] = jnp.dot(ds, k, preferred_element_type=F32)
                dk_c[keys, :] += lax.dot_general(ds, q, (((0,), (0,)), ((), ())), preferred_element_type=F32)
                dv_c[keys, :] += lax.dot_general(p16, do16, (((0,), (0,)), ((), ())), preferred_element_type=F32)
                return carry

            lax.fori_loop(0, nblk, blk, 0)
            _class_out((dq_ref, dk_ref, dv_ref), scratch[6:], r, dilation, length)

    def part(p):
        return pl.BlockSpec((seq, HEAD), lambda h: (0, p * 12 + group * 4 + h))

    hs = pl.BlockSpec((seq, HEAD), lambda h: (0, h))
    return pl.pallas_call(
        body, out_shape=(jax.ShapeDtypeStruct((seq, A_OUT_W), F32),) * 3, grid=(4,),
        in_specs=[pl.BlockSpec(memory_space=pltpu.SMEM), part(0), part(1), part(2), hs, hs, hs],
        out_specs=(hs, hs, hs),
        scratch_shapes=[pltpu.VMEM((length, HEAD), F32)] * (9 if dilation > 1 else 0),
        compiler_params=_cp("parallel"), name=name,
    )(jnp.asarray(_alibi_slopes()), proj, proj, proj, dcat, comb, lse)


MEM_ROW_TILE = 512


def _mem_probs(q, k):
    s = lax.dot_general(q, k, (((1,), (1,)), ((), ())), preferred_element_type=F32) * (HEAD ** -0.5)
    p = jnp.exp(s - jnp.max(s, axis=-1, keepdims=True))
    return p / jnp.sum(p, axis=-1, keepdims=True)


def mem_fwd(proj, q_col, kv, name="mem_fwd"):
    seq = proj.shape[0]
    qb = q_col // HEAD

    def body(q_ref, k_ref, v_ref, o_ref):
        p = _mem_probs(q_ref[...].astype(BF16), k_ref[...].astype(BF16))
        o_ref[...] = jnp.dot(p.astype(BF16), v_ref[...].astype(BF16), preferred_element_type=F32).astype(o_ref.dtype)

    return pl.pallas_call(
        body, out_shape=jax.ShapeDtypeStruct((seq, MEM_W), BF16), grid=(MEM_HEADS, seq // MEM_ROW_TILE),
        in_specs=[pl.BlockSpec((MEM_ROW_TILE, HEAD), lambda h, i: (i, qb + h)),
                  pl.BlockSpec((MEM_LEN, HEAD), lambda h, i: (0, h)),
                  pl.BlockSpec((MEM_LEN, HEAD), lambda h, i: (0, MEM_HEADS + h))],
        out_specs=pl.BlockSpec((MEM_ROW_TILE, HEAD), lambda h, i: (i, h)),
        compiler_params=_cp("parallel", "parallel"), name=name)(proj, kv, kv)


def mem_bwd(proj, q_col, kv, dcat, do_col, name="mem_bwd"):
    seq = proj.shape[0]
    qb, ob = q_col // HEAD, do_col // HEAD
    scale = HEAD ** -0.5

    def body(q_ref, k_ref, v_ref, do_ref, dq_ref, dk_ref, dv_ref):
        q = q_ref[...].astype(BF16)
        k = k_ref[...].astype(BF16)
        v = v_ref[...].astype(BF16)
        do = do_ref[...].astype(BF16)
        p = _mem_probs(q, k)
        dp = lax.dot_general(do, v, (((1,), (1,)), ((), ())), preferred_element_type=F32)
        ds = (p * (dp - jnp.sum(dp * p, axis=-1, keepdims=True)) * scale).astype(BF16)
        dq_ref[...] = jnp.dot(ds, k, preferred_element_type=F32).astype(dq_ref.dtype)
        dk = lax.dot_general(ds, q, (((0,), (0,)), ((), ())), preferred_element_type=F32)
        dv = lax.dot_general(p.astype(BF16), do, (((0,), (0,)), ((), ())), preferred_element_type=F32)

        @pl.when(pl.program_id(1) == 0)
        def _():
            dk_ref[...] = dk
            dv_ref[...] = dv

        @pl.when(pl.program_id(1) > 0)
        def _():
            dk_ref[...] += dk
            dv_ref[...] += dv

    dq, dk, dv = pl.pallas_call(
        body, out_shape=(jax.ShapeDtypeStruct((seq, MEM_W), BF16), jax.ShapeDtypeStruct((MEM_LEN, MEM_W), F32),
                         jax.ShapeDtypeStruct((MEM_LEN, MEM_W), F32)),
        grid=(MEM_HEADS, seq // MEM_ROW_TILE),
        in_specs=[pl.BlockSpec((MEM_ROW_TILE, HEAD), lambda h, i: (i, qb + h)),
                  pl.BlockSpec((MEM_LEN, HEAD), lambda h, i: (0, h)),
                  pl.BlockSpec((MEM_LEN, HEAD), lambda h, i: (0, MEM_HEADS + h)),
                  pl.BlockSpec((MEM_ROW_TILE, HEAD), lambda h, i: (i, ob + h))],
        out_specs=(pl.BlockSpec((MEM_ROW_TILE, HEAD), lambda h, i: (i, h)),
                   pl.BlockSpec((MEM_LEN, HEAD), lambda h, i: (0, h)),
                   pl.BlockSpec((MEM_LEN, HEAD), lambda h, i: (0, h))),
        compiler_params=_cp("parallel", "arbitrary"), name=name)(proj, kv, kv, dcat)
    return dq, jnp.concatenate([dk, dv], axis=1)


def _sgu_front(x, gain):
    uv, duv = _gelu_parts(x)
    u, v = uv[:, :B_W], uv[:, B_W:]
    r, vh = _rms_stats(v)
    return u, duv, r, vh, vh * gain


def sgu_fwd(proj, gain, w_s, bias_b, name="sgu_fwd"):
    seq = proj.shape[0]

    def body(x_ref, gain_ref, ws_ref, bias_ref, o_ref):
        u, _, _, _, vn = _sgu_front(x_ref[...], gain_ref[...])
        for g in range(B_GROUPS):
            cs = slice(g * CHUNK, (g + 1) * CHUNK)
            mixed = jnp.dot(ws_ref[g].astype(BF16), vn[:, cs].astype(BF16), preferred_element_type=F32) + bias_ref[g]
            o_ref[:, cs] = (u[:, cs] * mixed).astype(o_ref.dtype)

    full = lambda shape: pl.BlockSpec(shape, lambda c: (0,) * len(shape))
    return pl.pallas_call(
        body, out_shape=jax.ShapeDtypeStruct((seq, B_W), BF16), grid=(seq // CHUNK,),
        in_specs=[pl.BlockSpec((CHUNK, 2 * B_W), lambda c: (c, 0)), full((1, B_W)),
                  full((B_GROUPS, CHUNK, CHUNK)), full((B_GROUPS, CHUNK, CHUNK))],
        out_specs=pl.BlockSpec((CHUNK, B_W), lambda c: (c, 0)),
        compiler_params=_cp("parallel"), name=name)(proj, gain.reshape(1, B_W), w_s, bias_b)


def sgu_bwd(proj, gain, w_s, w_s_t, bias_b, dcat, name="sgu_bwd"):
    seq = proj.shape[0]

    def body(x_ref, gain_ref, ws_ref, wst_ref, bias_ref, do_ref, dx_ref, dws_ref, dmix_ref, dgain_ref, dvn_ref):
        first = pl.program_id(0) == 0
        gain = gain_ref[...]
        u, duv, r, vh, vn = _sgu_front(x_ref[...], gain)
        do = do_ref[...]
        for g in range(B_GROUPS):
            cs = slice(g * CHUNK, (g + 1) * CHUNK)
            vg = vn[:, cs].astype(BF16)
            mixed = jnp.dot(ws_ref[g].astype(BF16), vg, preferred_element_type=F32) + bias_ref[g]
            dx_ref[:, cs] = (do[:, cs] * mixed * duv[:, cs]).astype(dx_ref.dtype)
            dmixed = do[:, cs] * u[:, cs]
            dm16 = dmixed.astype(BF16)
            dws = lax.dot_general(dm16, vg, (((1,), (1,)), ((), ())), preferred_element_type=F32)
            dvn_ref[:, cs] = jnp.dot(wst_ref[g].astype(BF16), dm16, preferred_element_type=F32)

            @pl.when(first)
            def _():
                dws_ref[g] = dws
                dmix_ref[g] = dmixed

            @pl.when(jnp.logical_not(first))
            def _():
                dws_ref[g] += dws
                dmix_ref[g] += dmixed

        dvn = dvn_ref[...]
        dgain = jnp.sum(dvn * vh, axis=0, keepdims=True)

        @pl.when(first)
        def _():
            dgain_ref[...] = dgain

        @pl.when(jnp.logical_not(first))
        def _():
            dgain_ref[...] += dgain

        dv = _rms_back(vh, r, gain, dvn)
        dx_ref[:, B_W:] = (dv * duv[:, B_W:]).astype(dx_ref.dtype)

    full = lambda shape: pl.BlockSpec(shape, lambda c: (0,) * len(shape))
    mats = full((B_GROUPS, CHUNK, CHUNK))
    return pl.pallas_call(
        body, out_shape=(jax.ShapeDtypeStruct((seq, 2 * B_W), BF16), jax.ShapeDtypeStruct((B_GROUPS, CHUNK, CHUNK), F32),
                         jax.ShapeDtypeStruct((B_GROUPS, CHUNK, CHUNK), F32), jax.ShapeDtypeStruct((1, B_W), F32)),
        grid=(seq // CHUNK,),
        in_specs=[pl.BlockSpec((CHUNK, 2 * B_W), lambda c: (c, 0)), full((1, B_W)), mats, mats, mats,
                  pl.BlockSpec((CHUNK, B_W), lambda c: (c, 0))],
        out_specs=(pl.BlockSpec((CHUNK, 2 * B_W), lambda c: (c, 0)), mats, mats, full((1, B_W))),
        scratch_shapes=[pltpu.VMEM((CHUNK, B_W), F32)],
        compiler_params=_cp("arbitrary"), name=name)(proj, gain.reshape(1, B_W), w_s, w_s_t, bias_b, dcat)


FFN_COLS = 128
FFN_ROWS = 32
SUBLANES = 8


def _window(ref, r0, first, last):
    cols = ref.shape[1]
    pad = jnp.zeros((SUBLANES, cols), F32)
    if first:
        return jnp.concatenate([pad, ref[pl.ds(0, FFN_ROWS + SUBLANES), :]], axis=0)
    if last:
        return jnp.concatenate([ref[pl.ds(r0 - SUBLANES, FFN_ROWS + SUBLANES), :], pad], axis=0)
    return ref[pl.ds(pl.multiple_of(r0 - SUBLANES, SUBLANES), FFN_ROWS + 2 * SUBLANES), :]


def _taps(win):
    mid = slice(SUBLANES, SUBLANES + FFN_ROWS)
    return pltpu.roll(win, 1, 0)[mid], win[mid], pltpu.roll(win, win.shape[0] - 1, 0)[mid]


def _row_steps(seq, step, carry):
    n = seq // FFN_ROWS
    carry = step(0, True, False, carry)
    carry = lax.fori_loop(1, n - 1, lambda i, c: step(pl.multiple_of(i * FFN_ROWS, FFN_ROWS), False, False, c), carry)
    return step(seq - FFN_ROWS, False, True, carry)


def _conv3(taps, w, b):
    prev, cur, nxt = taps
    return prev * w[0:1] + cur * w[1:2] + nxt * w[2:3] + b


def _fold(x):
    return jnp.sum(x.reshape(FFN_ROWS // SUBLANES, SUBLANES, x.shape[1]), axis=0)


FFN_FWD_COLS = 256


def _taps_whole(a):
    n = a.shape[0]
    rows = lax.broadcasted_iota(jnp.int32, a.shape, 0)
    return (jnp.where(rows == 0, 0.0, pltpu.roll(a, 1, 0)), a, jnp.where(rows == n - 1, 0.0, pltpu.roll(a, n - 1, 0)))


def ffn_act_fwd(a, conv_w, conv_b, name="ffn_act_fwd"):
    seq = a.shape[0]
    nb = FF // FFN_FWD_COLS

    def body(ag_ref, av_ref, wg_ref, wv_ref, bg_ref, bv_ref, o_ref):
        gate = _conv3(_taps_whole(ag_ref[...]), wg_ref[...], bg_ref[...])
        val = _conv3(_taps_whole(av_ref[...]), wv_ref[...], bv_ref[...])
        o_ref[...] = (_gelu(gate) * val).astype(o_ref.dtype)

    col = lambda rows, off: pl.BlockSpec((rows, FFN_FWD_COLS), lambda j: (0, j + off))
    cb = conv_b.reshape(1, 2 * FF)
    return pl.pallas_call(
        body, out_shape=jax.ShapeDtypeStruct((seq, FF), BF16), grid=(nb,),
        in_specs=[col(seq, 0), col(seq, nb), col(3, 0), col(3, nb), col(1, 0), col(1, nb)],
        out_specs=col(seq, 0), compiler_params=_cp("parallel"), name=name)(a, a, conv_w, conv_w, cb, cb)


def ffn_act_bwd(a, conv_w, conv_b, dact, name="ffn_act_bwd"):
    seq = a.shape[0]
    nb = FF // FFN_COLS

    def body(ag_ref, av_ref, wg_ref, wv_ref, bg_ref, bv_ref, d_ref, dag_ref, dav_ref, dwg_ref, dwv_ref, dbg_ref, dbv_ref,
             dcg_ref, dcv_ref):
        wg, wv, bg, bv = wg_ref[...], wv_ref[...], bg_ref[...], bv_ref[...]

        def conv_grads(r0, first, last, sums):
            g_taps = _taps(_window(ag_ref, r0, first, last))
            v_taps = _taps(_window(av_ref, r0, first, last))
            act, dact_dgate = _gelu_parts(_conv3(g_taps, wg, bg))
            d = d_ref[pl.ds(r0, FFN_ROWS), :].astype(F32)
            dcg = d * _conv3(v_taps, wv, bv) * dact_dgate
            dcv = d * act
            dcg_ref[pl.ds(r0, FFN_ROWS), :] = dcg
            dcv_ref[pl.ds(r0, FFN_ROWS), :] = dcv
            new = [_fold(dcg)] + [_fold(dcg * t) for t in g_taps] + [_fold(dcv)] + [_fold(dcv * t) for t in v_taps]
            return tuple(s + n for s, n in zip(sums, new))

        zero = jnp.zeros((SUBLANES, FFN_COLS), F32)
        sums = _row_steps(seq, conv_grads, (zero,) * 8)
        total = [jnp.sum(s, axis=0, keepdims=True) for s in sums]
        dbg_ref[...] = total[0]
        dbv_ref[...] = total[4]
        for k in range(3):
            dwg_ref[k:k + 1, :] = total[1 + k]
            dwv_ref[k:k + 1, :] = total[5 + k]

        def conv_transpose(r0, first, last, carry):
            for dc_ref, w, da_ref in ((dcg_ref, wg, dag_ref), (dcv_ref, wv, dav_ref)):
                prev, cur, nxt = _taps(_window(dc_ref, r0, first, last))
                da_ref[pl.ds(r0, FFN_ROWS), :] = (nxt * w[0:1] + cur * w[1:2] + prev * w[2:3]).astype(da_ref.dtype)
            return carry

        _row_steps(seq, conv_transpose, 0)

    col = lambda rows, off: pl.BlockSpec((rows, FFN_COLS), lambda j: (0, j + off))
    cb = conv_b.reshape(1, 2 * FF)
    dag, dav, dwg, dwv, dbg, dbv = pl.pallas_call(
        body, out_shape=(jax.ShapeDtypeStruct((seq, FF), BF16),) * 2 + (jax.ShapeDtypeStruct((3, FF), F32),) * 2
        + (jax.ShapeDtypeStruct((1, FF), F32),) * 2, grid=(nb,),
        in_specs=[col(seq, 0), col(seq, nb), col(3, 0), col(3, nb), col(1, 0), col(1, nb), col(seq, 0)],
        out_specs=(col(seq, 0), col(seq, 0), col(3, 0), col(3, 0), col(1, 0), col(1, 0)),
        scratch_shapes=[pltpu.VMEM((seq, FFN_COLS), F32), pltpu.VMEM((seq, FFN_COLS), F32)],
        compiler_params=_cp("parallel"), name=name)(a, a, conv_w, conv_w, cb, cb, dact)
    cat = lambda p, q: jnp.concatenate([p, q], axis=1)
    return (dag, dav), cat(dwg, dwv), cat(dbg, dbv)


def _adam_math(w, g, m, v):
    m = ADAM_B1 * m + (1.0 - ADAM_B1) * g
    v = ADAM_B2 * v + (1.0 - ADAM_B2) * (g * g)
    m_hat = m / (1.0 - ADAM_B1 ** ADAM_STEP)
    v_hat = v / (1.0 - ADAM_B2 ** ADAM_STEP)
    return -ADAM_LR * (m_hat / (jnp.sqrt(v_hat) + ADAM_EPS) + ADAM_WD * w), m, v


def _row_tile(rows, cols):
    return _pick(rows, (256, 128, 64)) if cols <= 1024 else _pick(rows, (128, 64))


BF16_ROWS = 16
STREAM_BLOCK_BYTES = 3 * 1024 * 1024


def _stream_rows(rows, cols, itemsize):
    fits = [r for r in range(BF16_ROWS, rows + 1, BF16_ROWS) if rows % r == 0 and r * cols * itemsize <= STREAM_BLOCK_BYTES]
    return max(fits)


def adamw_layer(w_all, m_all, v_all, layer, g, prev, name):
    n, rows, cols = w_all.shape
    tr = _row_tile(rows, cols)

    def body(w_ref, m_ref, v_ref, g_ref, *rest):
        go_ref, d_ref, mo_ref, vo_ref = rest[-4:]
        g_ = g_ref[...]
        d, m_, v_ = _adam_math(w_ref[...], g_, m_ref[...], v_ref[...])
        go_ref[...] = g_
        d_ref[...] = d
        mo_ref[...] = m_
        vo_ref[...] = v_

    lay = pl.BlockSpec((None, tr, cols), lambda i: (layer, i, 0))
    in_specs = [lay, lay, lay, pl.BlockSpec((tr, cols), lambda i: (i, 0))]
    args = [w_all, m_all, v_all, g]
    aliases = {}
    if prev is not None:
        in_specs += [pl.BlockSpec(memory_space=pl.ANY)] * 4
        args += list(prev)
        aliases = {4 + k: k for k in range(4)}
    return pl.pallas_call(
        body, out_shape=(jax.ShapeDtypeStruct(w_all.shape, F32),) * 4, grid=(rows // tr,),
        in_specs=in_specs, out_specs=(lay,) * 4, input_output_aliases=aliases,
        compiler_params=_cp("parallel"), name=name)(*args)


def adamw_flat(w, g, m, v, name="adamw_small"):
    rows, cols = w.shape
    tr = _pick(rows, (128, 8))

    def body(w_ref, g_ref, m_ref, v_ref, d_ref, mo_ref, vo_ref):
        d_ref[...], mo_ref[...], vo_ref[...] = _adam_math(w_ref[...], g_ref[...], m_ref[...], v_ref[...])

    spec = pl.BlockSpec((tr, cols), lambda i: (i, 0))
    return pl.pallas_call(
        body, out_shape=(jax.ShapeDtypeStruct(w.shape, F32),) * 3, grid=(rows // tr,),
        in_specs=[spec] * 4, out_specs=(spec,) * 3, compiler_params=_cp("parallel"), name=name)(w, g, m, v)


def pair_sum(dw, got, core, name):
    _, rows, cols = dw.shape
    half = rows // 2
    tr = _stream_rows(half, cols, 2)
    nrb = half // tr

    def body(c_ref, a_ref, b_ref, o_ref):
        o_ref[...] = (a_ref[...].astype(F32) + b_ref[...].astype(F32)).astype(o_ref.dtype)

    return pl.pallas_call(
        body, out_shape=jax.ShapeDtypeStruct((N_CHIPS, half, cols), BF16),
        grid_spec=pltpu.PrefetchScalarGridSpec(
            num_scalar_prefetch=1, grid=(N_CHIPS, nrb),
            in_specs=[pl.BlockSpec((None, tr, cols), lambda s, i, c_ref: (s, c_ref[0] * nrb + i, 0)),
                      pl.BlockSpec((None, tr, cols), lambda s, i, c_ref: (s, i, 0))],
            out_specs=pl.BlockSpec((None, tr, cols), lambda s, i, c_ref: (s, i, 0))),
        compiler_params=_cp("parallel", "parallel"), name=name)(core, dw, got)


def chip_sum(own, parts, place, name):
    _, half, cols = parts.shape
    tr = _stream_rows(half, cols, 4)
    nrb = half // tr

    def body(p_ref, own_ref, a_ref, b_ref, c_ref, o_ref):
        o_ref[...] = ((own_ref[...].astype(F32) + a_ref[...].astype(F32)) + b_ref[...].astype(F32)) + c_ref[...].astype(F32)

    def slot(k):
        return pl.BlockSpec((None, tr, cols), lambda i, p: (jnp.bitwise_xor(p[0], k), i, 0))

    return pl.pallas_call(
        body, out_shape=jax.ShapeDtypeStruct((2 * half, cols), F32),
        grid_spec=pltpu.PrefetchScalarGridSpec(
            num_scalar_prefetch=1, grid=(nrb,), in_specs=[slot(0), slot(1), slot(2), slot(3)],
            out_specs=pl.BlockSpec((tr, cols), lambda i, p: (p[1] * nrb + i, 0))),
        compiler_params=_cp("parallel"), name=name)(place, own, parts, parts, parts)


def cast_to_slot(w_all, layer, place, name, after=None):
    _, rows, cols = w_all.shape
    tr = _stream_rows(rows, cols, 4)
    tied = [] if after is None else [after]

    def body(p_ref, w_ref, *rest):
        o_ref = rest[-1]
        o_ref[...] = w_ref[...].astype(o_ref.dtype)

    return pl.pallas_call(
        body, out_shape=jax.ShapeDtypeStruct((N_CHIPS, rows, cols), BF16),
        grid_spec=pltpu.PrefetchScalarGridSpec(
            num_scalar_prefetch=1, grid=(rows // tr,),
            in_specs=[pl.BlockSpec((None, tr, cols), lambda i, p: (layer, i, 0))] + [ANY] * len(tied),
            out_specs=pl.BlockSpec((None, tr, cols), lambda i, p: (p[0], i, 0))),
        compiler_params=_cp("parallel"), name=name)(place, w_all, *tied)


ANY = pl.BlockSpec(memory_space=pl.ANY)


def _place():
    x, y, c = lax.axis_index("x"), lax.axis_index("y"), lax.axis_index("c")
    others = [(1 - x, y), (x, 1 - y), (1 - x, 1 - y)]
    return x, y, c, 2 * x + y, others


def _remote(src, dst, send_sem, recv_sem, dev):
    return pltpu.make_async_remote_copy(src_ref=src, dst_ref=dst, send_sem=send_sem, recv_sem=recv_sem,
                                        device_id=dev, device_id_type=MESH)


HBM = pl.BlockSpec(memory_space=pltpu.HBM)
SEM = pl.BlockSpec(memory_space=pltpu.SEMAPHORE)
EFFECT = pltpu.SideEffectType.DATAFLOW_SIDE_EFFECTING
TOKEN = jax.ShapeDtypeStruct((8, LANES), F32)


def _in_hbm(a):
    return pltpu.with_memory_space_constraint(a, pltpu.HBM)


def _gather_copies(bufs, send_sems, recv_sems):
    x, y, c, me, others = _place()
    out = []
    for w, buf in enumerate(bufs):
        half = buf.shape[1] // 2
        mine = pl.ds(c * half, half)
        for k, (ox, oy) in enumerate(others):
            sems = send_sems.at[3 * w + k], recv_sems.at[3 * w + k]
            out.append((_remote(buf.at[me, mine], buf.at[me, mine], *sems, (ox, oy, c)),
                        _remote(buf.at[me, mine], buf.at[2 * ox + oy, mine], *sems, (ox, oy, c))))
    return out


def _forward_copies(bufs, send_sems, recv_sems):
    x, y, c, me, others = _place()
    out = []
    for w, buf in enumerate(bufs):
        half = buf.shape[1] // 2
        mine, theirs = pl.ds(c * half, half), pl.ds((1 - c) * half, half)
        for k, (ox, oy) in enumerate(others):
            sems = send_sems.at[3 * w + k], recv_sems.at[3 * w + k]
            slot = 2 * ox + oy
            out.append((_remote(buf.at[slot, mine], buf.at[slot, mine], *sems, (x, y, 1 - c)),
                        _remote(buf.at[slot, mine], buf.at[slot, theirs], *sems, (x, y, 1 - c))))
    return out


def _join_copies(grads, send_sems, recv_sems):
    x, y, c, _, _ = _place()
    out = []
    for w, g in enumerate(grads):
        half = g.shape[0] // 2
        mine, theirs = pl.ds(c * half, half), pl.ds((1 - c) * half, half)
        sems = send_sems.at[w], recv_sems.at[w]
        out.append((_remote(g.at[mine], g.at[mine], *sems, (x, y, 1 - c)), _remote(g.at[mine], g.at[theirs], *sems, (x, y, 1 - c))))
    return out


def _slot_copies(bufs, send_sems, recv_sems):
    x, y, c, me, others = _place()
    out = []
    for w, buf in enumerate(bufs):
        for k, (ox, oy) in enumerate(others):
            sems = send_sems.at[3 * w + k], recv_sems.at[3 * w + k]
            out.append((_remote(buf.at[me], buf.at[me], *sems, (ox, oy, c)),
                        _remote(buf.at[me], buf.at[2 * ox + oy], *sems, (ox, oy, c))))
    return out


def _ring_copies(stage):
    def copies(bufs, send_sems, recv_sems):
        x, y, c, me, _ = _place()
        x_nb, y_nb = (1 - x, y, c), (x, 1 - y, c)
        slot_x, slot_y, slot_d = 2 * (1 - x) + y, 2 * x + (1 - y), 2 * (1 - x) + (1 - y)
        out = []
        for w, buf in enumerate(bufs):
            half = buf.shape[1] // 2
            sems = [(send_sems.at[2 * w + k], recv_sems.at[2 * w + k]) for k in range(2)]
            if stage == 1:
                mine = pl.ds(c * half, half)
                out += [(_remote(buf.at[me, mine], buf.at[me, mine], *sems[0], x_nb),
                         _remote(buf.at[me, mine], buf.at[slot_x, mine], *sems[0], x_nb)),
                        (_remote(buf.at[me, mine], buf.at[me, mine], *sems[1], y_nb),
                         _remote(buf.at[me, mine], buf.at[slot_y, mine], *sems[1], y_nb))]
            else:
                first = pl.ds(c * half, half // 2)
                second = pl.ds(c * half + half // 2, half // 2)
                out += [(_remote(buf.at[slot_x, first], buf.at[slot_x, first], *sems[0], y_nb),
                         _remote(buf.at[slot_x, first], buf.at[slot_d, first], *sems[0], y_nb)),
                        (_remote(buf.at[slot_y, second], buf.at[slot_y, second], *sems[1], x_nb),
                         _remote(buf.at[slot_y, second], buf.at[slot_d, second], *sems[1], x_nb))]
        return out
    return copies


IN_PLACE = dict(gather=(_gather_copies, 3), forward=(_forward_copies, 3), join=(_join_copies, 1), slots=(_slot_copies, 3),
                ring1=(_ring_copies(1), 2), ring2=(_ring_copies(2), 2))


def copies_start(kind, bufs, name):
    n = len(bufs)
    copies, per_buf = IN_PLACE[kind]

    def body(*refs):
        ins, (send_sems, recv_sems), token = refs[:n], refs[n:n + 2], refs[-1]
        for sent, _ in copies(ins, send_sems, recv_sems):
            sent.start()
        token[...] = jnp.zeros_like(token)

    outs = pl.pallas_call(
        body, name=name,
        out_shape=(pltpu.SemaphoreType.DMA((per_buf * n,)), pltpu.SemaphoreType.DMA((per_buf * n,)),
                   *[pltpu.HBM(b.shape, b.dtype) for b in bufs], TOKEN),
        in_specs=[HBM] * n, out_specs=(SEM, SEM, *[HBM] * n, VM),
        input_output_aliases={w: 2 + w for w in range(n)},
        compiler_params=pltpu.CompilerParams(has_side_effects=EFFECT))(*[_in_hbm(b) for b in bufs])
    return outs[0], outs[1], list(outs[2:2 + n]), outs[-1]


def copies_wait(kind, send_sems, recv_sems, bufs, after, name):
    n = len(bufs)
    copies, _ = IN_PLACE[kind]

    def body(*refs):
        ins, (send_ref, recv_ref) = refs[:n], refs[n:n + 2]
        for sent, landed in copies(ins, send_ref, recv_ref):
            sent.wait_send()
            landed.wait_recv()

    return list(pl.pallas_call(
        body, name=name, out_shape=tuple(pltpu.HBM(b.shape, b.dtype) for b in bufs),
        in_specs=[HBM] * n + [SEM, SEM, ANY], out_specs=(HBM,) * n,
        input_output_aliases={w: w for w in range(n)},
        compiler_params=pltpu.CompilerParams(has_side_effects=EFFECT))(*bufs, send_sems, recv_sems, after))


def swap_halves(grads, name):
    n = len(grads)

    def body(*refs):
        ins, outs = refs[:n], refs[n:2 * n]
        send_sems, recv_sems = refs[2 * n:]
        x, y, c, _, _ = _place()
        copies = []
        for w in range(n):
            half = ins[w].shape[1] // 2
            cp = _remote(ins[w].at[:, pl.ds((1 - c) * half, half)], outs[w], send_sems.at[w], recv_sems.at[w], (x, y, 1 - c))
            cp.start()
            copies.append(cp)
        for cp in copies:
            cp.wait()

    return pl.pallas_call(
        body, out_shape=tuple(jax.ShapeDtypeStruct((N_CHIPS, g.shape[1] // 2, g.shape[2]), g.dtype) for g in grads),
        in_specs=[ANY] * n, out_specs=(ANY,) * n,
        scratch_shapes=[pltpu.SemaphoreType.DMA((n,)), pltpu.SemaphoreType.DMA((n,))], name=name)(*grads)


def _swap_copies(grads, lands, send_sems, recv_sems):
    x, y, c, _, _ = _place()
    out = []
    for w, (g, land) in enumerate(zip(grads, lands)):
        half = g.shape[1] // 2
        out.append(_remote(g.at[:, pl.ds((1 - c) * half, half)], land, send_sems.at[w], recv_sems.at[w], (x, y, 1 - c)))
    return out


def swap_start(grads, name):
    n = len(grads)

    def body(*refs):
        ins, lands, (send_sems, recv_sems), token = refs[:n], refs[n:2 * n], refs[2 * n:2 * n + 2], refs[-1]
        for cp in _swap_copies(ins, lands, send_sems, recv_sems):
            cp.start()
        token[...] = jnp.zeros_like(token)

    shapes = [(N_CHIPS, g.shape[1] // 2, g.shape[2]) for g in grads]
    zones = [_in_hbm(lax.empty(s, g.dtype)) for s, g in zip(shapes, grads)]
    outs = pl.pallas_call(
        body, name=name,
        out_shape=(pltpu.SemaphoreType.DMA((n,)), pltpu.SemaphoreType.DMA((n,)),
                   *[pltpu.HBM(g.shape, g.dtype) for g in grads], *[pltpu.HBM(s, g.dtype) for s, g in zip(shapes, grads)],
                   TOKEN),
        in_specs=[HBM] * (2 * n), out_specs=(SEM, SEM, *[HBM] * (2 * n), VM),
        input_output_aliases={w: 2 + w for w in range(2 * n)},
        compiler_params=pltpu.CompilerParams(has_side_effects=EFFECT))(*[_in_hbm(g) for g in grads], *zones)
    return outs[0], outs[1], list(outs[2:2 + n]), list(outs[2 + n:2 + 2 * n]), outs[-1]


def swap_wait(send_sems, recv_sems, grads, lands, after, name):
    n = len(grads)

    def body(*refs):
        ins, zones, (send_ref, recv_ref) = refs[:n], refs[n:2 * n], refs[2 * n:2 * n + 2]
        for cp in _swap_copies(ins, zones, send_ref, recv_ref):
            cp.wait_send()
            cp.wait_recv()

    outs = pl.pallas_call(
        body, name=name, out_shape=tuple(pltpu.HBM(a.shape, a.dtype) for a in list(grads) + list(lands)),
        in_specs=[HBM] * (2 * n) + [SEM, SEM, ANY], out_specs=(HBM,) * (2 * n),
        input_output_aliases={w: w for w in range(2 * n)},
        compiler_params=pltpu.CompilerParams(has_side_effects=EFFECT))(*grads, *lands, send_sems, recv_sems, after)
    return list(outs[:n]), list(outs[n:])


def _exchange_copies(sums, lands, send_sems, recv_sems):
    x, y, c, me, others = _place()
    out = []
    for w, (src, land) in enumerate(zip(sums, lands)):
        for k, (ox, oy) in enumerate(others):
            sems = send_sems.at[3 * w + k], recv_sems.at[3 * w + k]
            out.append((_remote(src.at[2 * ox + oy], land.at[me], *sems, (ox, oy, c)),
                        _remote(src.at[2 * ox + oy], land.at[2 * ox + oy], *sems, (ox, oy, c))))
    return out


def exchange_start(sums, name):
    n = len(sums)

    def body(*refs):
        ins, lands, (send_sems, recv_sems), token = refs[:n], refs[n:2 * n], refs[2 * n:2 * n + 2], refs[-1]
        for sent, _ in _exchange_copies(ins, lands, send_sems, recv_sems):
            sent.start()
        token[...] = jnp.zeros_like(token)

    zones = [_in_hbm(lax.empty(s.shape, s.dtype)) for s in sums]
    outs = pl.pallas_call(
        body, name=name,
        out_shape=(pltpu.SemaphoreType.DMA((3 * n,)), pltpu.SemaphoreType.DMA((3 * n,)),
                   *[pltpu.HBM(s.shape, s.dtype) for s in sums] * 2, TOKEN),
        in_specs=[HBM] * (2 * n), out_specs=(SEM, SEM, *[HBM] * (2 * n), VM),
        input_output_aliases={w: 2 + w for w in range(2 * n)},
        compiler_params=pltpu.CompilerParams(has_side_effects=EFFECT))(*[_in_hbm(s) for s in sums], *zones)
    return outs[0], outs[1], list(outs[2:2 + n]), list(outs[2 + n:2 + 2 * n]), outs[-1]


def exchange_wait(send_sems, recv_sems, sums, lands, after, name):
    n = len(sums)

    def body(*refs):
        ins, zones, (send_ref, recv_ref) = refs[:n], refs[n:2 * n], refs[2 * n:2 * n + 2]
        for sent, landed in _exchange_copies(ins, zones, send_ref, recv_ref):
            sent.wait_send()
            landed.wait_recv()

    outs = pl.pallas_call(
        body, name=name, out_shape=tuple(pltpu.HBM(s.shape, s.dtype) for s in sums) * 2,
        in_specs=[HBM] * (2 * n) + [SEM, SEM, ANY], out_specs=(HBM,) * (2 * n),
        input_output_aliases={w: w for w in range(2 * n)},
        compiler_params=pltpu.CompilerParams(has_side_effects=EFFECT))(*sums, *lands, send_sems, recv_sems, after)
    return list(outs[:n]), list(outs[n:])


VM = pl.BlockSpec(memory_space=pltpu.VMEM)


def small_allgather(buf, name="small_allgather"):
    def body(in_ref, out_ref, send_sems, recv_sems):
        x, y, c, me, others = _place()
        out_ref[me] = in_ref[...]
        copies = []
        for k, (ox, oy) in enumerate(others):
            cp = _remote(in_ref, out_ref.at[me], send_sems.at[k], recv_sems.at[k], (ox, oy, c))
            cp.start()
            copies.append(cp)
        for k, (ox, oy) in enumerate(others):
            landed = out_ref.at[2 * ox + oy]
            _remote(landed, landed, send_sems.at[k], recv_sems.at[k], (ox, oy, c)).wait_recv()
        for cp in copies:
            cp.wait_send()

    return pl.pallas_call(
        body, out_shape=jax.ShapeDtypeStruct((N_CHIPS,) + buf.shape, buf.dtype), in_specs=[VM], out_specs=VM,
        scratch_shapes=[pltpu.SemaphoreType.DMA((3,)), pltpu.SemaphoreType.DMA((3,))],
        compiler_params=pltpu.CompilerParams(vmem_limit_bytes=V7X_VMEM_LIMIT), name=name)(buf)


def pair_small(buf, after, name="pair_small"):
    def body(in_ref, after_ref, out_ref, sib_ref, send_sem, recv_sem):
        x, y, c, me, _ = _place()
        cp = _remote(in_ref, sib_ref, send_sem, recv_sem, (x, y, 1 - c))
        cp.start()
        cp.wait()
        out_ref[me] = in_ref[...] + sib_ref[...]

    return pl.pallas_call(
        body, out_shape=jax.ShapeDtypeStruct((N_CHIPS,) + buf.shape, buf.dtype), in_specs=[VM, ANY], out_specs=VM,
        scratch_shapes=[pltpu.VMEM(buf.shape, buf.dtype), pltpu.SemaphoreType.DMA, pltpu.SemaphoreType.DMA],
        compiler_params=pltpu.CompilerParams(vmem_limit_bytes=V7X_VMEM_LIMIT), name=name)(buf, after)


def sum_slots(slots, name="sum_slots"):
    _, rows, cols = slots.shape
    tr = _pick(rows, (128, 8))

    def body(s_ref, o_ref):
        o_ref[...] = ((s_ref[0] + s_ref[1]) + s_ref[2]) + s_ref[3]

    return pl.pallas_call(
        body, out_shape=jax.ShapeDtypeStruct((rows, cols), slots.dtype), grid=(rows // tr,),
        in_specs=[pl.BlockSpec((N_CHIPS, tr, cols), lambda i: (0, i, 0))],
        out_specs=pl.BlockSpec((tr, cols), lambda i: (i, 0)), compiler_params=_cp("parallel"), name=name)(slots)


def _pack_rows(arrays, row_multiple):
    flat = jnp.concatenate([a.reshape(-1) for a in arrays])
    rows = -(-flat.shape[0] // (LANES * row_multiple)) * row_multiple
    return jnp.pad(flat, (0, rows * LANES - flat.shape[0])).reshape(rows, LANES)


def _unpack_rows(buf, shapes):
    flat = buf.reshape(-1)
    out, at = [], 0
    for s in shapes:
        n = math.prod(s)
        out.append(flat[at:at + n].reshape(s))
        at += n
    return out


def _mixer_weights(i):
    j = i // 2
    mixer = "a" if i % 2 == 0 else "b"
    return [("w_mem_kv", i), (mixer + "_w_in", j), (mixer + "_w_out", j)]


def _ffn_weights(i):
    return [("ffn_w_up", i), ("ffn_w_down", i)]


def _mixer_fwd(i, x, mem, w, small, after):
    is_a = i % 2 == 0
    j = i // 2
    wkv, win, wout = w
    wkv = wkv.reshape(1, D_MODEL, 2 * MEM_W)
    h1 = rms_fwd(x, small["mix_norm_g"][i], BF16, name=f"mix_norm{i}", after=after)
    mem_n = rms_fwd(mem, small["mem_norm_g"][i], BF16, name=f"mem_norm{i}")
    kv = mm_nn(mem_n, wkv, F32, name=f"mem_kv{i}")
    proj = mm_nn(h1, win, F32, name=f"in_proj{i}")
    saved = dict(x0=x, h1=h1, mem_n=mem_n, kv=kv, proj=proj)
    if is_a:
        outs, lses = zip(*[attn_fwd(proj, g, name=f"attn_fwd{i}_{g}") for g in range(3)])
        comb, lse = attn_combine(outs, lses, name=f"attn_combine{i}")
        mem_out = mem_fwd(proj, 3 * A_QKV_W, kv, name=f"mem_fwd{i}")
        cat = jnp.concatenate([comb.astype(BF16), mem_out], axis=1)
        saved.update(comb=comb, lse=lse)
    else:
        wout = wout.reshape(1, B_W + MEM_W, D_MODEL)
        tok = sgu_fwd(proj, small["b_v_norm_g"][j], small["b_w_s"][j], small["bias_b"][j], name=f"sgu_fwd{i}")
        mem_out = mem_fwd(proj, 2 * B_W, kv, name=f"mem_fwd{i}")
        cat = jnp.concatenate([tok, mem_out], axis=1)
    x1 = mm_nn(cat, wout, F32, res=x, name=f"out_proj{i}")
    saved.update(cat=cat)
    return x1, saved


def _ffn_fwd(i, x1, w, small, after):
    wup, wdn = w
    h2 = rms_fwd(x1, small["ffn_norm_g"][i], BF16, name=f"ffn_norm{i}", after=after)
    a = mm_nn(h2, wup, F32, name=f"ffn_up{i}")
    act = ffn_act_fwd(a, small["ffn_conv_w"][i], small["ffn_conv_b"][i], name=f"ffn_act{i}")
    x2 = mm_nn(act, wdn.reshape(1, FF, D_MODEL), F32, res=x1, name=f"ffn_down{i}")
    return x2, dict(x1=x1, h2=h2, a=a, act=act)


def _ffn_bwd(i, dx2, w, small, sv, after):
    wup, wdn = w
    dx2, dx2_16 = dx2
    sg = {}
    dact = mm_nt(dx2_16, wdn.reshape(1, FF, D_MODEL), F32, name=f"d_act{i}", after=after)
    d_wdn = mm_tn(sv["act"], dx2_16, 1, BF16, name=f"d_wdown{i}").reshape(N_CHIPS, FF // N_CHIPS, D_MODEL)
    da, sg["ffn_conv_w"], sg["ffn_conv_b"] = ffn_act_bwd(sv["a"], small["ffn_conv_w"][i], small["ffn_conv_b"][i], dact,
                                                          name=f"ffn_act_bwd{i}")
    d_wup = mm_tn(sv["h2"], da, N_CHIPS, BF16, name=f"d_wup{i}")
    dh2 = mm_nt(da, wup, F32, name=f"d_h2_{i}")
    dx1, dx1_16, sg["ffn_norm_g"] = rms_bwd(sv["x1"], small["ffn_norm_g"][i], dh2, dres=dx2, name=f"ffn_norm_bwd{i}")
    return (dx1, dx1_16), [d_wup, d_wdn], sg


def _mixer_bwd(i, dx1, mem, w, small, sv, after):
    is_a = i % 2 == 0
    j = i // 2
    wkv, win, wout = w
    wkv = wkv.reshape(1, D_MODEL, 2 * MEM_W)
    dx1_32, dx1 = dx1
    sg = {}
    proj, kv = sv["proj"], sv["kv"]
    if is_a:
        dcat = mm_nt(dx1, wout, F32, name=f"d_cat{i}", after=after)
        d_wout = mm_tn(sv["cat"], dx1, N_CHIPS, BF16, name=f"d_wout{i}")
        dqm, dkv = mem_bwd(proj, 3 * A_QKV_W, kv, dcat, A_OUT_W, name=f"mem_bwd{i}")
        parts = [attn_bwd(proj, dcat, sv["comb"], sv["lse"], g, name=f"attn_bwd{i}_{g}") for g in range(3)]
        dproj = jnp.concatenate([parts[g][p].astype(BF16) for p in range(3) for g in range(3)] + [dqm], axis=1)
    else:
        dcat = mm_nt(dx1, wout.reshape(1, B_W + MEM_W, D_MODEL), F32, name=f"d_cat{i}", after=after)
        d_wout = mm_tn(sv["cat"], dx1, 1, BF16, name=f"d_wout{i}").reshape(N_CHIPS, (B_W + MEM_W) // N_CHIPS, D_MODEL)
        dqm, dkv = mem_bwd(proj, 2 * B_W, kv, dcat, B_W, name=f"mem_bwd{i}")
        w_s = small["b_w_s"][j]
        duv, sg["b_w_s"], dmix, sg["b_v_norm_g"] = sgu_bwd(proj, small["b_v_norm_g"][j], w_s, jnp.swapaxes(w_s, 1, 2),
                                                           small["bias_b"][j], dcat, name=f"sgu_bwd{i}")
        sg["b_s_bias"] = jnp.sum(dmix, axis=-1)
        dproj = jnp.concatenate([duv, dqm], axis=1)
    d_wkv = mm_tn(sv["mem_n"], dkv, 1, BF16, name=f"d_wkv{i}").reshape(N_CHIPS, D_MODEL // N_CHIPS, 2 * MEM_W)
    dmem_n = mm_nt(dkv, wkv, F32, name=f"d_mem_n{i}")
    _, _, sg["mem_norm_g"] = rms_bwd(mem, small["mem_norm_g"][i], dmem_n, name=f"mem_norm_bwd{i}")
    d_win = mm_tn(sv["h1"], dproj, N_CHIPS, BF16, name=f"d_win{i}")
    dh1 = mm_nt(dproj, win, F32, name=f"d_h1_{i}")
    dx0, dx0_16, sg["mix_norm_g"] = rms_bwd(sv["x0"], small["mix_norm_g"][i], dh1, dres=dx1_32, name=f"mix_norm_bwd{i}")
    return (dx0, dx0_16), [d_wkv, d_win, d_wout], sg


def _exchange_begin(grads, got, place, tag):
    sums = [pair_sum(g, o, place[1:], name=f"pair_sum{tag}_{k}") for k, (g, o) in enumerate(zip(grads, got))]
    send_sems, recv_sems, sums, lands, token = exchange_start(sums, name=f"exchange_start{tag}")
    return (send_sems, recv_sems, sums, lands), token


def _reduce_finish(started, place, after, tag):
    sums, parts = exchange_wait(*started, after, name=f"exchange_wait{tag}")
    halves = [chip_sum(s, p, place, name=f"chip_sum{tag}_{k}") for k, (s, p) in enumerate(zip(sums, parts))]
    return copies_start("join", halves, name=f"join_start_{tag}")


SMALL_SHARDED = ("b_v_norm_g", "ffn_conv_w")
SMALL_FULL_SHAPES = dict(mix_norm_g=(D_MODEL,), ffn_norm_g=(D_MODEL,), mem_norm_g=(D_MODEL,), b_v_norm_g=(B_W,),
                         b_w_s=(B_GROUPS, CHUNK, CHUNK), b_s_bias=(B_GROUPS, CHUNK), ffn_conv_w=(3, 2 * FF),
                         ffn_conv_b=(2 * FF,))
BIG = ("w_mem_kv", "a_w_in", "a_w_out", "b_w_in", "b_w_out", "ffn_w_up", "ffn_w_down")
WEIGHT_ORDER = ("mix_norm_g", "ffn_norm_g", "mem_norm_g", "w_mem_kv", "a_w_in", "a_w_out", "b_w_in", "b_v_norm_g", "b_w_s",
                "b_s_bias", "b_w_out", "ffn_w_up", "ffn_conv_w", "ffn_conv_b", "ffn_w_down", "final_norm_g")


def kernel(x, mem, mix_norm_g, ffn_norm_g, mem_norm_g, w_mem_kv, a_w_in, a_w_out, b_w_in, b_v_norm_g, b_w_s, b_s_bias, b_w_out, ffn_w_up, ffn_conv_w, ffn_conv_b, ffn_w_down, final_norm_g, loss_target, m_mix_norm_g, m_ffn_norm_g, m_mem_norm_g, m_w_mem_kv, m_a_w_in, m_a_w_out, m_b_w_in, m_b_v_norm_g, m_b_w_s, m_b_s_bias, m_b_w_out, m_ffn_w_up, m_ffn_conv_w, m_ffn_conv_b, m_ffn_w_down, m_final_norm_g, v_mix_norm_g, v_ffn_norm_g, v_mem_norm_g, v_w_mem_kv, v_a_w_in, v_a_w_out, v_b_w_in, v_b_v_norm_g, v_b_w_s, v_b_s_bias, v_b_w_out, v_ffn_w_up, v_ffn_conv_w, v_ffn_conv_b, v_ffn_w_down, v_final_norm_g):
    weights = dict(mix_norm_g=mix_norm_g, ffn_norm_g=ffn_norm_g, mem_norm_g=mem_norm_g, w_mem_kv=w_mem_kv, a_w_in=a_w_in,
                   a_w_out=a_w_out, b_w_in=b_w_in, b_v_norm_g=b_v_norm_g, b_w_s=b_w_s, b_s_bias=b_s_bias, b_w_out=b_w_out,
                   ffn_w_up=ffn_w_up, ffn_conv_w=ffn_conv_w, ffn_conv_b=ffn_conv_b, ffn_w_down=ffn_w_down,
                   final_norm_g=final_norm_g)
    mom1 = dict(mix_norm_g=m_mix_norm_g, ffn_norm_g=m_ffn_norm_g, mem_norm_g=m_mem_norm_g, w_mem_kv=m_w_mem_kv,
                a_w_in=m_a_w_in, a_w_out=m_a_w_out, b_w_in=m_b_w_in, b_v_norm_g=m_b_v_norm_g, b_w_s=m_b_w_s,
                b_s_bias=m_b_s_bias, b_w_out=m_b_w_out, ffn_w_up=m_ffn_w_up, ffn_conv_w=m_ffn_conv_w,
                ffn_conv_b=m_ffn_conv_b, ffn_w_down=m_ffn_w_down, final_norm_g=m_final_norm_g)
    mom2 = dict(mix_norm_g=v_mix_norm_g, ffn_norm_g=v_ffn_norm_g, mem_norm_g=v_mem_norm_g, w_mem_kv=v_w_mem_kv,
                a_w_in=v_a_w_in, a_w_out=v_a_w_out, b_w_in=v_b_w_in, b_v_norm_g=v_b_v_norm_g, b_w_s=v_b_w_s,
                b_s_bias=v_b_s_bias, b_w_out=v_b_w_out, ffn_w_up=v_ffn_w_up, ffn_conv_w=v_ffn_conv_w,
                ffn_conv_b=v_ffn_conv_b, ffn_w_down=v_ffn_w_down, final_norm_g=v_final_norm_g)
    chip = 2 * lax.axis_index("x") + lax.axis_index("y")
    place = jnp.stack([chip, lax.axis_index("c")]).astype(jnp.int32)
    x0, mem0, target = x[0], mem[0], loss_target[0]
    depth = DEPTH

    n_cw, n_vg = ffn_conv_w.size, b_v_norm_g.size
    gathered = small_allgather(_pack_rows([ffn_conv_w, b_v_norm_g], 8)).reshape(N_CHIPS, -1)
    conv_w_full = gathered[:, :n_cw].reshape(N_CHIPS, DEPTH, 3, 2 * FF // N_CHIPS).transpose(1, 2, 0, 3).reshape(DEPTH, 3, 2 * FF)
    vgain_full = gathered[:, n_cw:n_cw + n_vg].reshape(N_CHIPS, 2, B_W // N_CHIPS).transpose(1, 0, 2).reshape(2, B_W)
    small = dict(mix_norm_g=mix_norm_g, ffn_norm_g=ffn_norm_g, mem_norm_g=mem_norm_g, b_w_s=b_w_s, ffn_conv_b=ffn_conv_b,
                 ffn_conv_w=conv_w_full, b_v_norm_g=vgain_full,
                 bias_b=jnp.broadcast_to(b_s_bias[..., None], b_s_bias.shape + (CHUNK,)))

    half_layers = 2 * depth
    groups = [(_ffn_weights if b % 2 else _mixer_weights)(b // 2) for b in range(half_layers)]
    tags = [("f" if b % 2 else "m") + str(b // 2) for b in range(half_layers)]

    ring_from, early_from = 3, 6
    kind = lambda b: "gather" if b < ring_from else "ring1"

    def start_gather(b, after):
        bufs = [cast_to_slot(weights[n], l, place, name=f"cast_{n}{l}", after=after) for n, l in groups[b]]
        return copies_start(kind(b), bufs, name=f"{kind(b)}_start_{tags[b]}")

    def next_stage(b, now, new, after):
        send_sems, recv_sems, bufs, _ = flying.pop(b)
        bufs = copies_wait(now, send_sems, recv_sems, bufs, after, name=f"{now}_wait_{tags[b]}")
        flying[b] = copies_start(new, bufs, name=f"{new}_start_{tags[b]}")
        return flying[b][3]

    flying, tie = {}, gathered
    for b in range(3):
        flying[b] = start_gather(b, tie)
        tie = flying[b][3]
    w_half, saved_half, h = [], [], x0
    for b in range(half_layers):
        behind = tie if b == 0 else h
        if b < early_from:
            behind = next_stage(b, "gather" if b < ring_from else "ring2", "forward", behind)
        w_half.append(copies_wait("forward", *flying.pop(b)[:3], behind, name=f"forward_wait_{tags[b]}"))
        ready, tokens = w_half[b][0], []
        if early_from <= b + 1 < half_layers:
            tokens.append(next_stage(b + 1, "ring2", "forward", ready))
        for g in (b + 1, b + 2):
            if g < half_layers and ((g == b + 1 and ring_from <= g < early_from) or (g == b + 2 and g >= early_from)):
                tokens.append(next_stage(g, "ring1", "ring2", ready))
        if b + 3 < half_layers:
            flying[b + 3] = start_gather(b + 3, sum(tokens[1:], tokens[0]) if tokens else ready)
            tokens.append(flying[b + 3][3])
        tie = sum(tokens[1:], tokens[0]) if tokens else None
        if b % 2 == 0:
            h, sv = _mixer_fwd(b // 2, h, mem0, w_half[b], small, tie)
        else:
            h, sv = _ffn_fwd(b // 2, h, w_half[b], small, tie)
        saved_half.append(sv)
    w_mix, w_ffn, saved_mix, saved_ffn = w_half[0::2], w_half[1::2], saved_half[0::2], saved_half[1::2]
    loss_row, *dh, d_final = final_loss(h, final_norm_g, target)
    loss = lax.psum(loss_row[0, 0], ("x", "y", "c"))

    names = [n for n in WEIGHT_ORDER if n not in BIG]
    small_g = {n: [None] * weights[n].shape[0] for n in names if n != "final_norm_g"}
    big_out = {n: None for n in BIG}

    def keep_small(i, sg):
        for n, g in sg.items():
            small_g[n][i if len(small_g[n]) == depth else i // 2] = g.reshape(SMALL_FULL_SHAPES[n])

    joining = []

    def update(after):
        (send_sems, recv_sems, halves, _), group, tag = joining.pop()
        for (n, l), g in zip(group, copies_wait("join", send_sems, recv_sems, halves, after, name=f"join_wait_{tag}")):
            big_out[n] = adamw_layer(weights[n], mom1[n], mom2[n], l, g, big_out[n], name=f"adamw_{n}{l}")

    def finish_reduce(started, group, after, tag):
        join = _reduce_finish(started, place, after, tag)
        if joining:
            update(join[3])
        joining.append((join, group, tag))

    half_layers = 2 * depth
    swapping, exchanging, tie = None, [], None
    for k in range(half_layers):
        i = depth - 1 - k // 2
        if k % 2 == 0:
            dh, big_g, sg = _ffn_bwd(i, dh, w_ffn[i], small, saved_ffn[i], tie)
            group, tag = _ffn_weights(i), f"f{i}"
        else:
            dh, big_g, sg = _mixer_bwd(i, dh, mem0, w_mix[i], small, saved_mix[i], tie)
            group, tag = _mixer_weights(i), f"m{i}"
        keep_small(i, sg)
        started_now, swap_now, tokens = [], None, []
        if k < half_layers - 2:
            *swap_now, token = swap_start(big_g, name=f"swap_start_{tag}")
            swap_now = (swap_now, group, tag)
        else:
            started, token = _exchange_begin(big_g, swap_halves(big_g, name=f"swap_halves_{tag}"), place, tag)
            started_now.append((started, group, tag))
        tokens.append(token)
        if swapping is not None:
            swap_args, old_group, old_tag = swapping
            grads, got = swap_wait(*swap_args, dh[0], name=f"swap_wait_{old_tag}")
            started, token = _exchange_begin(grads, got, place, old_tag)
            started_now.append((started, old_group, old_tag))
            tokens.append(token)
        tie = sum(tokens[1:], tokens[0])
        for started, old_group, old_tag in exchanging:
            finish_reduce(started, old_group, dh[0], old_tag)
        swapping, exchanging = swap_now, started_now
    for started, old_group, old_tag in exchanging:
        finish_reduce(started, old_group, big_out["ffn_w_down"][0], old_tag)

    full_g = {n: (d_final.reshape(-1) if n == "final_norm_g" else jnp.stack(small_g[n])) for n in names}
    shapes = [full_g[n].shape for n in names]
    flying_small = copies_start("slots", [pair_small(_pack_rows([full_g[n] for n in names], 8), tie)], name="small_start")
    update(flying_small[3])
    reduced = sum_slots(copies_wait("slots", *flying_small[:3], big_out["w_mem_kv"][0], name="small_wait")[0])
    summed = dict(zip(names, _unpack_rows(reduced, shapes)))
    for n in SMALL_SHARDED:
        width = weights[n].shape[-1]
        summed[n] = lax.dynamic_slice_in_dim(summed[n], chip * width, width, axis=summed[n].ndim - 1)
    own_shapes = [weights[n].shape for n in names]
    pack = lambda d: _pack_rows([d[n] for n in names], 128)
    small_out = [_unpack_rows(b, own_shapes) for b in adamw_flat(pack(weights), pack(summed), pack(mom1), pack(mom2))]
    outs = {}
    for k, n in enumerate(names):
        outs[n] = (summed[n], small_out[0][k], small_out[1][k], small_out[2][k])
    outs.update(big_out)
    return (loss, dh[0][None], *[outs[n][0] for n in WEIGHT_ORDER], *[outs[n][1] for n in WEIGHT_ORDER],
            *[outs[n][2] for n in WEIGHT_ORDER], *[outs[n][3] for n in WEIGHT_ORDER])
```

```python
import math

import numpy as np
import jax
import jax.numpy as jnp
from jax import lax
from jax.experimental import pallas as pl
from jax.experimental.pallas import tpu as pltpu

F32 = jnp.float32
BF16 = jnp.bfloat16
MESH = pl.DeviceIdType.MESH

D_MODEL = 2048
SEQ = 2048
DEPTH = 4
EPS = 1e-6
NEG = -1e30
HEAD = 128
A_PATTERNS = ((128, 1), (512, 4), (2048, 16))
A_QKV_W = 1536
A_OUT_W = 512
A_IN = 5120
QBLK = 128
N_SIDE = 64
CHUNK = 128
B_GROUPS = 12
B_W = 1536
B_IN = 3584
MEM_LEN = 256
MEM_HEADS = 4
MEM_W = 512
FF = 5632
ADAM_LR, ADAM_B1, ADAM_B2, ADAM_EPS, ADAM_WD, ADAM_STEP = 0.001, 0.9, 0.999, 1e-08, 0.01, 10
N_CHIPS = 4

LANES = 128
V7X_VMEM_LIMIT = 56 * 1024 * 1024


def _cp(*sem):
    return pltpu.CompilerParams(dimension_semantics=sem, vmem_limit_bytes=V7X_VMEM_LIMIT)


def _pick(dim, prefs):
    for p in prefs:
        if dim % p == 0:
            return p
    raise ValueError(f"no tile for {dim} in {prefs}")


def _gelu_parts(x):
    cdf = 0.5 * (1.0 + lax.erf(x * (1.0 / math.sqrt(2.0))))
    pdf = jnp.exp(-0.5 * x * x) * (1.0 / math.sqrt(2.0 * math.pi))
    return x * cdf, cdf + x * pdf


def _gelu(x):
    return 0.5 * x * (1.0 + lax.erf(x * (1.0 / math.sqrt(2.0))))


TM_PREFS = (1024, 512, 256, 128)
TN_PREFS = (1408, 1280, 1024, 896, 512, 256, 128)
TK_PREFS = (2816, 2048, 1408, 1280, 1024, 896, 512, 256, 128)


def _mm_body(nk, dims, has_res, halves=None):
    def body(*refs):
        refs = list(refs)
        ops = []
        for operand in (0, 1):
            if halves is not None and halves[0] == operand:
                first, second = refs.pop(0), refs.pop(0)
                ops.append(jnp.where(pl.program_id(halves[1]) < halves[2], first[...], second[...]))
            else:
                ops.append(refs.pop(0)[...])
        r_ref = refs.pop(0) if has_res else None
        o_ref = refs.pop(0)
        part = lax.dot_general(ops[0].astype(BF16), ops[1].astype(BF16), dims, preferred_element_type=F32)
        if nk == 1:
            if has_res:
                part = part + r_ref[...]
            o_ref[...] = part.astype(o_ref.dtype)
            return
        acc_ref = refs[-1]
        k = pl.program_id(2)

        @pl.when(k == 0)
        def _():
            acc_ref[...] = part

        @pl.when(k > 0)
        def _():
            acc_ref[...] += part

        @pl.when(k == nk - 1)
        def _():
            tot = acc_ref[...]
            if has_res:
                tot = tot + r_ref[...]
            o_ref[...] = tot.astype(o_ref.dtype)
    return body


def mm_nn(a, w, out_dtype, res=None, name="mm_nn"):
    m, kw = a.shape
    ns_, kw2, nsz = w.shape
    assert kw == kw2
    n = ns_ * nsz
    tm, tn, tk = _pick(m, TM_PREFS), _pick(nsz, TN_PREFS), _pick(kw, TK_PREFS)
    nb, nk = nsz // tn, kw // tk
    in_specs = [pl.BlockSpec((tm, tk), lambda i, j, k: (i, k)),
                pl.BlockSpec((None, tk, tn), lambda i, j, k: (j // nb, k, j % nb))]
    args = [a, w]
    if res is not None:
        in_specs.append(pl.BlockSpec((tm, tn), lambda i, j, k: (i, j)))
        args.append(res)
    return pl.pallas_call(
        _mm_body(nk, (((1,), (0,)), ((), ())), res is not None),
        out_shape=jax.ShapeDtypeStruct((m, n), out_dtype),
        grid=(m // tm, n // tn, nk), in_specs=in_specs,
        out_specs=pl.BlockSpec((tm, tn), lambda i, j, k: (i, j)),
        scratch_shapes=[pltpu.VMEM((tm, tn), F32)] if nk > 1 else [],
        compiler_params=_cp("parallel", "parallel", "arbitrary"), name=name)(*args)


def mm_nt(g, w, out_dtype, name="mm_nt", after=None):
    parts = list(g) if isinstance(g, (tuple, list)) else [g]
    m, n = parts[0].shape[0], sum(p.shape[1] for p in parts)
    ns_, kw, nsz = w.shape
    assert n == ns_ * nsz
    tm, tn, tk = _pick(m, TM_PREFS), _pick(kw, TN_PREFS), _pick(nsz, TK_PREFS)
    nb, nk = nsz // tk, n // tk
    n_first = parts[0].shape[1] // tk
    if len(parts) == 1:
        g_specs = [pl.BlockSpec((tm, tk), lambda i, j, k: (i, k))]
    else:
        g_specs = [pl.BlockSpec((tm, tk), lambda i, j, k: (i, jnp.minimum(k, n_first - 1))),
                   pl.BlockSpec((tm, tk), lambda i, j, k: (i, jnp.maximum(k - n_first, 0)))]
    body = _mm_body(nk, (((1,), (1,)), ((), ())), False, None if len(parts) == 1 else (0, 2, n_first))
    tied = [] if after is None else [after]
    n_in = len(parts) + 1
    return pl.pallas_call(
        (lambda *refs: body(*refs[:n_in], *refs[n_in + len(tied):])),
        out_shape=jax.ShapeDtypeStruct((m, kw), out_dtype),
        grid=(m // tm, kw // tn, nk),
        in_specs=g_specs + [pl.BlockSpec((None, tn, tk), lambda i, j, k: (k // nb, j, k % nb))] + [ANY] * len(tied),
        out_specs=pl.BlockSpec((tm, tn), lambda i, j, k: (i, j)),
        scratch_shapes=[pltpu.VMEM((tm, tn), F32)] if nk > 1 else [],
        compiler_params=_cp("parallel", "parallel", "arbitrary"), name=name)(*parts, w, *tied)


def mm_tn(a, g, n_shards, out_dtype, name="mm_tn"):
    parts = list(g) if isinstance(g, (tuple, list)) else [g]
    t, kw = a.shape
    n = sum(p.shape[1] for p in parts)
    assert t == parts[0].shape[0]
    nsz = n // n_shards
    tm, tn, tk = _pick(kw, TM_PREFS), _pick(nsz, TN_PREFS), _pick(t, TK_PREFS)
    nb, nk = nsz // tn, t // tk
    n_first = parts[0].shape[1] // tn
    if len(parts) == 1:
        g_specs = [pl.BlockSpec((tk, tn), lambda i, j, k: (k, j))]
    else:
        g_specs = [pl.BlockSpec((tk, tn), lambda i, j, k: (k, jnp.minimum(j, n_first - 1))),
                   pl.BlockSpec((tk, tn), lambda i, j, k: (k, jnp.maximum(j - n_first, 0)))]
    return pl.pallas_call(
        _mm_body(nk, (((0,), (0,)), ((), ())), False, None if len(parts) == 1 else (1, 1, n_first)),
        out_shape=jax.ShapeDtypeStruct((n_shards, kw, nsz), out_dtype),
        grid=(kw // tm, n // tn, nk),
        in_specs=[pl.BlockSpec((tk, tm), lambda i, j, k: (k, i))] + g_specs,
        out_specs=pl.BlockSpec((None, tm, tn), lambda i, j, k: (j // nb, i, j % nb)),
        scratch_shapes=[pltpu.VMEM((tm, tn), F32)] if nk > 1 else [],
        compiler_params=_cp("parallel", "parallel", "arbitrary"), name=name)(a, *parts)


ROW_TILE = 256


def _rms_stats(x):
    r = lax.rsqrt(jnp.mean(x * x, axis=-1, keepdims=True) + EPS)
    return r, x * r


def _rms_back(xh, r, g, dh):
    u = dh * g
    return r * (u - xh * jnp.mean(u * xh, axis=-1, keepdims=True))


def rms_fwd(x, g, out_dtype, name="rms_fwd", after=None):
    rows, d = x.shape
    tr = _pick(rows, (ROW_TILE, 128))
    tied = [] if after is None else [after]

    def body(x_ref, g_ref, *rest):
        o_ref = rest[-1]
        _, xh = _rms_stats(x_ref[...])
        o_ref[...] = (xh * g_ref[...]).astype(o_ref.dtype)

    return pl.pallas_call(
        body, out_shape=jax.ShapeDtypeStruct((rows, d), out_dtype), grid=(rows // tr,),
        in_specs=[pl.BlockSpec((tr, d), lambda i: (i, 0)), pl.BlockSpec((1, d), lambda i: (0, 0))] + [ANY] * len(tied),
        out_specs=pl.BlockSpec((tr, d), lambda i: (i, 0)),
        compiler_params=_cp("parallel"), name=name)(x, g.reshape(1, d), *tied)


def rms_bwd(x, g, dh, dres=None, name="rms_bwd"):
    rows, d = x.shape
    tr = _pick(rows, (ROW_TILE, 128))
    has_res = dres is not None

    def body(*refs):
        if has_res:
            x_ref, g_ref, dh_ref, dres_ref, dx_ref, dx16_ref, dg_ref = refs
        else:
            x_ref, g_ref, dh_ref, dx_ref, dx16_ref, dg_ref = refs
        r, xh = _rms_stats(x_ref[...])
        dh_ = dh_ref[...].astype(F32)
        part = jnp.sum(dh_ * xh, axis=0, keepdims=True)

        @pl.when(pl.program_id(0) == 0)
        def _():
            dg_ref[...] = part

        @pl.when(pl.program_id(0) > 0)
        def _():
            dg_ref[...] += part

        dx = _rms_back(xh, r, g_ref[...], dh_)
        if has_res:
            dx = dx + dres_ref[...]
        dx_ref[...] = dx
        dx16_ref[...] = dx.astype(BF16)

    row_spec = pl.BlockSpec((tr, d), lambda i: (i, 0))
    vec_spec = pl.BlockSpec((1, d), lambda i: (0, 0))
    args = [x, g.reshape(1, d), dh] + ([dres] if has_res else [])
    return pl.pallas_call(
        body, out_shape=(jax.ShapeDtypeStruct((rows, d), F32), jax.ShapeDtypeStruct((rows, d), BF16),
                         jax.ShapeDtypeStruct((1, d), F32)),
        grid=(rows // tr,), in_specs=[row_spec, vec_spec, row_spec] + ([row_spec] if has_res else []),
        out_specs=(row_spec, row_spec, vec_spec), compiler_params=_cp("arbitrary"), name=name)(*args)


def final_loss(x, g, target, name="final_loss"):
    rows, d = x.shape
    tr = _pick(rows, (ROW_TILE, 128))

    def body(x_ref, g_ref, t_ref, loss_ref, dx_ref, dx16_ref, dg_ref):
        r, xh = _rms_stats(x_ref[...])
        gain = g_ref[...]
        err = xh * gain - t_ref[...]
        sq = jnp.sum(jnp.sum(err * err, axis=1, keepdims=True), axis=0, keepdims=True) * (0.5 / d)
        dy = err * (1.0 / d)
        part = jnp.sum(dy * xh, axis=0, keepdims=True)

        @pl.when(pl.program_id(0) == 0)
        def _():
            dg_ref[...] = part
            loss_ref[...] = jnp.broadcast_to(sq, loss_ref.shape)

        @pl.when(pl.program_id(0) > 0)
        def _():
            dg_ref[...] += part
            loss_ref[...] += jnp.broadcast_to(sq, loss_ref.shape)

        dx = _rms_back(xh, r, gain, dy)
        dx_ref[...] = dx
        dx16_ref[...] = dx.astype(BF16)

    row_spec = pl.BlockSpec((tr, d), lambda i: (i, 0))
    vec_spec = pl.BlockSpec((1, d), lambda i: (0, 0))
    return pl.pallas_call(
        body, out_shape=(jax.ShapeDtypeStruct((1, LANES), F32), jax.ShapeDtypeStruct((rows, d), F32),
                         jax.ShapeDtypeStruct((rows, d), BF16), jax.ShapeDtypeStruct((1, d), F32)),
        grid=(rows // tr,), in_specs=[row_spec, vec_spec, row_spec],
        out_specs=(pl.BlockSpec((1, LANES), lambda i: (0, 0)), row_spec, row_spec, vec_spec),
        compiler_params=_cp("arbitrary"), name=name)(x, g.reshape(1, d), target)


def _alibi_slopes():
    return (2.0 ** (-8.0 * (np.arange(12) + 1) / 12)).astype(np.float32)


def _band_scores(q, k, q0, start, wk, slope):
    s = lax.dot_general(q, k, (((1,), (1,)), ((), ())), preferred_element_type=F32) * (HEAD ** -0.5)
    qpos = q0 + lax.broadcasted_iota(jnp.int32, (QBLK, wk), 0)
    kpos = start + lax.broadcasted_iota(jnp.int32, (QBLK, wk), 1)
    rel = jnp.abs(qpos - kpos)
    return jnp.where(rel <= N_SIDE, s - slope * rel.astype(F32), NEG)


def _attn_geometry(seq, dilation):
    length = seq // dilation
    return length, length // QBLK, min(2 * QBLK, length)


def _attn_window(n, length, wk):
    q0 = pl.multiple_of(n * QBLK, QBLK)
    start = pl.multiple_of(jnp.clip(n * QBLK - N_SIDE, 0, length - wk), N_SIDE)
    return q0, start


def _class_in(refs, scratch, r, dilation, length):
    if dilation == 1:
        return refs
    for ref, buf in zip(refs, scratch):
        buf[...] = ref[pl.ds(r, length, stride=dilation), :]
    return scratch


def _class_out(refs, scratch, r, dilation, length):
    if dilation > 1:
        for ref, buf in zip(refs, scratch):
            ref[pl.ds(r, length, stride=dilation), :] = buf[...]


def attn_fwd(proj, group, name):
    seq = proj.shape[0]
    dilation = A_PATTERNS[group][1]
    length, nblk, wk = _attn_geometry(seq, dilation)

    def body(slope_ref, q_ref, k_ref, v_ref, o_ref, lse_ref, *scratch):
        slope = slope_ref[group * 4 + pl.program_id(0)] * float(dilation)
        for r in range(dilation):
            q_c, k_c, v_c = _class_in((q_ref, k_ref, v_ref), scratch[:3], r, dilation, length)
            o_c, lse_c = (o_ref, lse_ref) if dilation == 1 else scratch[3:]

            def blk(n, carry):
                q0, start = _attn_window(n, length, wk)
                q = q_c[pl.ds(q0, QBLK), :].astype(BF16)
                k = k_c[pl.ds(start, wk), :].astype(BF16)
                v = v_c[pl.ds(start, wk), :].astype(BF16)
                s = _band_scores(q, k, q0, start, wk, slope)
                m = jnp.max(s, axis=-1, keepdims=True)
                p = jnp.exp(s - m)
                l = jnp.sum(p, axis=-1, keepdims=True)
                o = jnp.dot(p.astype(BF16), v, preferred_element_type=F32) / l
                o_c[pl.ds(q0, QBLK), :] = o
                lse_c[pl.ds(q0, QBLK), :] = jnp.broadcast_to(m + jnp.log(l), (QBLK, HEAD))
                return carry

            lax.fori_loop(0, nblk, blk, 0)
            _class_out((o_ref, lse_ref), scratch[3:], r, dilation, length)

    def part(p):
        return pl.BlockSpec((seq, HEAD), lambda h: (0, p * 12 + group * 4 + h))

    out_spec = pl.BlockSpec((seq, HEAD), lambda h: (0, h))
    return pl.pallas_call(
        body, out_shape=(jax.ShapeDtypeStruct((seq, A_OUT_W), F32),) * 2, grid=(4,),
        in_specs=[pl.BlockSpec(memory_space=pltpu.SMEM), part(0), part(1), part(2)],
        out_specs=(out_spec, out_spec),
        scratch_shapes=[pltpu.VMEM((length, HEAD), F32)] * (5 if dilation > 1 else 0),
        compiler_params=_cp("parallel"), name=name)(jnp.asarray(_alibi_slopes()), proj, proj, proj)


def attn_combine(os_, lses, name="attn_combine"):
    seq = os_[0].shape[0]
    tr = ROW_TILE

    def body(o0, o1, o2, l0, l1, l2, c_ref, lse_ref):
        a, b, c = l0[...], l1[...], l2[...]
        m = jnp.maximum(jnp.maximum(a, b), c)
        ea, eb, ec = jnp.exp(a - m), jnp.exp(b - m), jnp.exp(c - m)
        den = ea + eb + ec
        c_ref[...] = (ea * o0[...] + eb * o1[...] + ec * o2[...]) / den
        lse_ref[...] = m + jnp.log(den)

    spec = pl.BlockSpec((tr, A_OUT_W), lambda i: (i, 0))
    return pl.pallas_call(
        body, out_shape=(jax.ShapeDtypeStruct((seq, A_OUT_W), F32),) * 2, grid=(seq // tr,),
        in_specs=[spec] * 6, out_specs=(spec, spec), compiler_params=_cp("parallel"), name=name)(*os_, *lses)


def attn_bwd(proj, dcat, comb, lse, group, name):
    seq = proj.shape[0]
    dilation = A_PATTERNS[group][1]
    length, nblk, wk = _attn_geometry(seq, dilation)
    scale = HEAD ** -0.5

    def body(slope_ref, q_ref, k_ref, v_ref, do_ref, c_ref, lse_ref, dq_ref, dk_ref, dv_ref, *scratch):
        slope = slope_ref[group * 4 + pl.program_id(0)] * float(dilation)
        for r in range(dilation):
            q_c, k_c, v_c, do_c, c_c, lse_c = _class_in((q_ref, k_ref, v_ref, do_ref, c_ref, lse_ref), scratch[:6], r,
                                                        dilation, length)
            dq_c, dk_c, dv_c = (dq_ref, dk_ref, dv_ref) if dilation == 1 else scratch[6:]
            dk_c[...] = jnp.zeros_like(dk_c)
            dv_c[...] = jnp.zeros_like(dv_c)

            def blk(n, carry):
                q0, start = _attn_window(n, length, wk)
                rows = pl.ds(q0, QBLK)
                keys = pl.ds(start, wk)
                q = q_c[rows, :].astype(BF16)
                k = k_c[keys, :].astype(BF16)
                v = v_c[keys, :].astype(BF16)
                do = do_c[rows, :]
                s = _band_scores(q, k, q0, start, wk, slope)
                p = jnp.exp(s - lse_c[rows, :][:, :1])
                delta = jnp.sum(do * c_c[rows, :], axis=-1, keepdims=True)
                do16 = do.astype(BF16)
                dp = lax.dot_general(do16, v, (((1,), (1,)), ((), ())), preferred_element_type=F32)
                ds = (p * (dp - delta) * scale).astype(BF16)
                p16 = p.astype(BF16)
                dq_c[rows, :] = jnp.dot(ds, k, preferred_element_type=F32)
                dk_c[keys, :] += lax.dot_general(ds, q, (((0,), (0,)), ((), ())), preferred_element_type=F32)
                dv_c[keys, :] += lax.dot_general(p16, do16, (((0,), (0,)), ((), ())), preferred_element_type=F32)
                return carry

            lax.fori_loop(0, nblk, blk, 0)
            _class_out((dq_ref, dk_ref, dv_ref), scratch[6:], r, dilation, length)

    def part(p):
        return pl.BlockSpec((seq, HEAD), lambda h: (0, p * 12 + group * 4 + h))

    hs = pl.BlockSpec((seq, HEAD), lambda h: (0, h))
    return pl.pallas_call(
        body, out_shape=(jax.ShapeDtypeStruct((seq, A_OUT_W), F32),) * 3, grid=(4,),
        in_specs=[pl.BlockSpec(memory_space=pltpu.SMEM), part(0), part(1), part(2), hs, hs, hs],
        out_specs=(hs, hs, hs),
        scratch_shapes=[pltpu.VMEM((length, HEAD), F32)] * (9 if dilation > 1 else 0),
        compiler_params=_cp("parallel"), name=name,
    )(jnp.asarray(_alibi_slopes()), proj, proj, proj, dcat, comb, lse)


MEM_ROW_TILE = 512


def _mem_probs(q, k):
    s = lax.dot_general(q, k, (((1,), (1,)), ((), ())), preferred_element_type=F32) * (HEAD ** -0.5)
    p = jnp.exp(s - jnp.max(s, axis=-1, keepdims=True))
    return p / jnp.sum(p, axis=-1, keepdims=True)


def mem_fwd(proj, q_col, kv, name="mem_fwd"):
    seq = proj.shape[0]
    qb = q_col // HEAD

    def body(q_ref, k_ref, v_ref, o_ref):
        p = _mem_probs(q_ref[...].astype(BF16), k_ref[...].astype(BF16))
        o_ref[...] = jnp.dot(p.astype(BF16), v_ref[...].astype(BF16), preferred_element_type=F32).astype(o_ref.dtype)

    return pl.pallas_call(
        body, out_shape=jax.ShapeDtypeStruct((seq, MEM_W), BF16), grid=(MEM_HEADS, seq // MEM_ROW_TILE),
        in_specs=[pl.BlockSpec((MEM_ROW_TILE, HEAD), lambda h, i: (i, qb + h)),
                  pl.BlockSpec((MEM_LEN, HEAD), lambda h, i: (0, h)),
                  pl.BlockSpec((MEM_LEN, HEAD), lambda h, i: (0, MEM_HEADS + h))],
        out_specs=pl.BlockSpec((MEM_ROW_TILE, HEAD), lambda h, i: (i, h)),
        compiler_params=_cp("parallel", "parallel"), name=name)(proj, kv, kv)


def mem_bwd(proj, q_col, kv, dcat, do_col, name="mem_bwd"):
    seq = proj.shape[0]
    qb, ob = q_col // HEAD, do_col // HEAD
    scale = HEAD ** -0.5

    def body(q_ref, k_ref, v_ref, do_ref, dq_ref, dk_ref, dv_ref):
        q = q_ref[...].astype(BF16)
        k = k_ref[...].astype(BF16)
        v = v_ref[...].astype(BF16)
        do = do_ref[...].astype(BF16)
        p = _mem_probs(q, k)
        dp = lax.dot_general(do, v, (((1,), (1,)), ((), ())), preferred_element_type=F32)
        ds = (p * (dp - jnp.sum(dp * p, axis=-1, keepdims=True)) * scale).astype(BF16)
        dq_ref[...] = jnp.dot(ds, k, preferred_element_type=F32).astype(dq_ref.dtype)
        dk = lax.dot_general(ds, q, (((0,), (0,)), ((), ())), preferred_element_type=F32)
        dv = lax.dot_general(p.astype(BF16), do, (((0,), (0,)), ((), ())), preferred_element_type=F32)

        @pl.when(pl.program_id(1) == 0)
        def _():
            dk_ref[...] = dk
            dv_ref[...] = dv

        @pl.when(pl.program_id(1) > 0)
        def _():
            dk_ref[...] += dk
            dv_ref[...] += dv

    dq, dk, dv = pl.pallas_call(
        body, out_shape=(jax.ShapeDtypeStruct((seq, MEM_W), BF16), jax.ShapeDtypeStruct((MEM_LEN, MEM_W), F32),
                         jax.ShapeDtypeStruct((MEM_LEN, MEM_W), F32)),
        grid=(MEM_HEADS, seq // MEM_ROW_TILE),
        in_specs=[pl.BlockSpec((MEM_ROW_TILE, HEAD), lambda h, i: (i, qb + h)),
                  pl.BlockSpec((MEM_LEN, HEAD), lambda h, i: (0, h)),
                  pl.BlockSpec((MEM_LEN, HEAD), lambda h, i: (0, MEM_HEADS + h)),
                  pl.BlockSpec((MEM_ROW_TILE, HEAD), lambda h, i: (i, ob + h))],
        out_specs=(pl.BlockSpec((MEM_ROW_TILE, HEAD), lambda h, i: (i, h)),
                   pl.BlockSpec((MEM_LEN, HEAD), lambda h, i: (0, h)),
                   pl.BlockSpec((MEM_LEN, HEAD), lambda h, i: (0, h))),
        compiler_params=_cp("parallel", "arbitrary"), name=name)(proj, kv, kv, dcat)
    return dq, jnp.concatenate([dk, dv], axis=1)


def _sgu_front(x, gain):
    uv, duv = _gelu_parts(x)
    u, v = uv[:, :B_W], uv[:, B_W:]
    r, vh = _rms_stats(v)
    return u, duv, r, vh, vh * gain


def sgu_fwd(proj, gain, w_s, bias_b, name="sgu_fwd"):
    seq = proj.shape[0]

    def body(x_ref, gain_ref, ws_ref, bias_ref, o_ref):
        u, _, _, _, vn = _sgu_front(x_ref[...], gain_ref[...])
        for g in range(B_GROUPS):
            cs = slice(g * CHUNK, (g + 1) * CHUNK)
            mixed = jnp.dot(ws_ref[g].astype(BF16), vn[:, cs].astype(BF16), preferred_element_type=F32) + bias_ref[g]
            o_ref[:, cs] = (u[:, cs] * mixed).astype(o_ref.dtype)

    full = lambda shape: pl.BlockSpec(shape, lambda c: (0,) * len(shape))
    return pl.pallas_call(
        body, out_shape=jax.ShapeDtypeStruct((seq, B_W), BF16), grid=(seq // CHUNK,),
        in_specs=[pl.BlockSpec((CHUNK, 2 * B_W), lambda c: (c, 0)), full((1, B_W)),
                  full((B_GROUPS, CHUNK, CHUNK)), full((B_GROUPS, CHUNK, CHUNK))],
        out_specs=pl.BlockSpec((CHUNK, B_W), lambda c: (c, 0)),
        compiler_params=_cp("parallel"), name=name)(proj, gain.reshape(1, B_W), w_s, bias_b)


def sgu_bwd(proj, gain, w_s, w_s_t, bias_b, dcat, name="sgu_bwd"):
    seq = proj.shape[0]

    def body(x_ref, gain_ref, ws_ref, wst_ref, bias_ref, do_ref, dx_ref, dws_ref, dmix_ref, dgain_ref, dvn_ref):
        first = pl.program_id(0) == 0
        gain = gain_ref[...]
        u, duv, r, vh, vn = _sgu_front(x_ref[...], gain)
        do = do_ref[...]
        for g in range(B_GROUPS):
            cs = slice(g * CHUNK, (g + 1) * CHUNK)
            vg = vn[:, cs].astype(BF16)
            mixed = jnp.dot(ws_ref[g].astype(BF16), vg, preferred_element_type=F32) + bias_ref[g]
            dx_ref[:, cs] = (do[:, cs] * mixed * duv[:, cs]).astype(dx_ref.dtype)
            dmixed = do[:, cs] * u[:, cs]
            dm16 = dmixed.astype(BF16)
            dws = lax.dot_general(dm16, vg, (((1,), (1,)), ((), ())), preferred_element_type=F32)
            dvn_ref[:, cs] = jnp.dot(wst_ref[g].astype(BF16), dm16, preferred_element_type=F32)

            @pl.when(first)
            def _():
                dws_ref[g] = dws
                dmix_ref[g] = dmixed

            @pl.when(jnp.logical_not(first))
            def _():
                dws_ref[g] += dws
                dmix_ref[g] += dmixed

        dvn = dvn_ref[...]
        dgain = jnp.sum(dvn * vh, axis=0, keepdims=True)

        @pl.when(first)
        def _():
            dgain_ref[...] = dgain

        @pl.when(jnp.logical_not(first))
        def _():
            dgain_ref[...] += dgain

        dv = _rms_back(vh, r, gain, dvn)
        dx_ref[:, B_W:] = (dv * duv[:, B_W:]).astype(dx_ref.dtype)

    full = lambda shape: pl.BlockSpec(shape, lambda c: (0,) * len(shape))
    mats = full((B_GROUPS, CHUNK, CHUNK))
    return pl.pallas_call(
        body, out_shape=(jax.ShapeDtypeStruct((seq, 2 * B_W), BF16), jax.ShapeDtypeStruct((B_GROUPS, CHUNK, CHUNK), F32),
                         jax.ShapeDtypeStruct((B_GROUPS, CHUNK, CHUNK), F32), jax.ShapeDtypeStruct((1, B_W), F32)),
        grid=(seq // CHUNK,),
        in_specs=[pl.BlockSpec((CHUNK, 2 * B_W), lambda c: (c, 0)), full((1, B_W)), mats, mats, mats,
                  pl.BlockSpec((CHUNK, B_W), lambda c: (c, 0))],
        out_specs=(pl.BlockSpec((CHUNK, 2 * B_W), lambda c: (c, 0)), mats, mats, full((1, B_W))),
        scratch_shapes=[pltpu.VMEM((CHUNK, B_W), F32)],
        compiler_params=_cp("arbitrary"), name=name)(proj, gain.reshape(1, B_W), w_s, w_s_t, bias_b, dcat)


FFN_COLS = 128
FFN_ROWS = 32
SUBLANES = 8


def _window(ref, r0, first, last):
    cols = ref.shape[1]
    pad = jnp.zeros((SUBLANES, cols), F32)
    if first:
        return jnp.concatenate([pad, ref[pl.ds(0, FFN_ROWS + SUBLANES), :]], axis=0)
    if last:
        return jnp.concatenate([ref[pl.ds(r0 - SUBLANES, FFN_ROWS + SUBLANES), :], pad], axis=0)
    return ref[pl.ds(pl.multiple_of(r0 - SUBLANES, SUBLANES), FFN_ROWS + 2 * SUBLANES), :]


def _taps(win):
    mid = slice(SUBLANES, SUBLANES + FFN_ROWS)
    return pltpu.roll(win, 1, 0)[mid], win[mid], pltpu.roll(win, win.shape[0] - 1, 0)[mid]


def _row_steps(seq, step, carry):
    n = seq // FFN_ROWS
    carry = step(0, True, False, carry)
    carry = lax.fori_loop(1, n - 1, lambda i, c: step(pl.multiple_of(i * FFN_ROWS, FFN_ROWS), False, False, c), carry)
    return step(seq - FFN_ROWS, False, True, carry)


def _conv3(taps, w, b):
    prev, cur, nxt = taps
    return prev * w[0:1] + cur * w[1:2] + nxt * w[2:3] + b


def _fold(x):
    return jnp.sum(x.reshape(FFN_ROWS // SUBLANES, SUBLANES, x.shape[1]), axis=0)


FFN_FWD_COLS = 256


def _taps_whole(a):
    n = a.shape[0]
    rows = lax.broadcasted_iota(jnp.int32, a.shape, 0)
    return (jnp.where(rows == 0, 0.0, pltpu.roll(a, 1, 0)), a, jnp.where(rows == n - 1, 0.0, pltpu.roll(a, n - 1, 0)))


def ffn_act_fwd(a, conv_w, conv_b, name="ffn_act_fwd"):
    seq = a.shape[0]
    nb = FF // FFN_FWD_COLS

    def body(ag_ref, av_ref, wg_ref, wv_ref, bg_ref, bv_ref, o_ref):
        gate = _conv3(_taps_whole(ag_ref[...]), wg_ref[...], bg_ref[...])
        val = _conv3(_taps_whole(av_ref[...]), wv_ref[...], bv_ref[...])
        o_ref[...] = (_gelu(gate) * val).astype(o_ref.dtype)

    col = lambda rows, off: pl.BlockSpec((rows, FFN_FWD_COLS), lambda j: (0, j + off))
    cb = conv_b.reshape(1, 2 * FF)
    return pl.pallas_call(
        body, out_shape=jax.ShapeDtypeStruct((seq, FF), BF16), grid=(nb,),
        in_specs=[col(seq, 0), col(seq, nb), col(3, 0), col(3, nb), col(1, 0), col(1, nb)],
        out_specs=col(seq, 0), compiler_params=_cp("parallel"), name=name)(a, a, conv_w, conv_w, cb, cb)


def ffn_act_bwd(a, conv_w, conv_b, dact, name="ffn_act_bwd"):
    seq = a.shape[0]
    nb = FF // FFN_COLS

    def body(ag_ref, av_ref, wg_ref, wv_ref, bg_ref, bv_ref, d_ref, dag_ref, dav_ref, dwg_ref, dwv_ref, dbg_ref, dbv_ref,
             dcg_ref, dcv_ref):
        wg, wv, bg, bv = wg_ref[...], wv_ref[...], bg_ref[...], bv_ref[...]

        def conv_grads(r0, first, last, sums):
            g_taps = _taps(_window(ag_ref, r0, first, last))
            v_taps = _taps(_window(av_ref, r0, first, last))
            act, dact_dgate = _gelu_parts(_conv3(g_taps, wg, bg))
            d = d_ref[pl.ds(r0, FFN_ROWS), :].astype(F32)
            dcg = d * _conv3(v_taps, wv, bv) * dact_dgate
            dcv = d * act
            dcg_ref[pl.ds(r0, FFN_ROWS), :] = dcg
            dcv_ref[pl.ds(r0, FFN_ROWS), :] = dcv
            new = [_fold(dcg)] + [_fold(dcg * t) for t in g_taps] + [_fold(dcv)] + [_fold(dcv * t) for t in v_taps]
            return tuple(s + n for s, n in zip(sums, new))

        zero = jnp.zeros((SUBLANES, FFN_COLS), F32)
        sums = _row_steps(seq, conv_grads, (zero,) * 8)
        total = [jnp.sum(s, axis=0, keepdims=True) for s in sums]
        dbg_ref[...] = total[0]
        dbv_ref[...] = total[4]
        for k in range(3):
            dwg_ref[k:k + 1, :] = total[1 + k]
            dwv_ref[k:k + 1, :] = total[5 + k]

        def conv_transpose(r0, first, last, carry):
            for dc_ref, w, da_ref in ((dcg_ref, wg, dag_ref), (dcv_ref, wv, dav_ref)):
                prev, cur, nxt = _taps(_window(dc_ref, r0, first, last))
                da_ref[pl.ds(r0, FFN_ROWS), :] = (nxt * w[0:1] + cur * w[1:2] + prev * w[2:3]).astype(da_ref.dtype)
            return carry

        _row_steps(seq, conv_transpose, 0)

    col = lambda rows, off: pl.BlockSpec((rows, FFN_COLS), lambda j: (0, j + off))
    cb = conv_b.reshape(1, 2 * FF)
    dag, dav, dwg, dwv, dbg, dbv = pl.pallas_call(
        body, out_shape=(jax.ShapeDtypeStruct((seq, FF), BF16),) * 2 + (jax.ShapeDtypeStruct((3, FF), F32),) * 2
        + (jax.ShapeDtypeStruct((1, FF), F32),) * 2, grid=(nb,),
        in_specs=[col(seq, 0), col(seq, nb), col(3, 0), col(3, nb), col(1, 0), col(1, nb), col(seq, 0)],
        out_specs=(col(seq, 0), col(seq, 0), col(3, 0), col(3, 0), col(1, 0), col(1, 0)),
        scratch_shapes=[pltpu.VMEM((seq, FFN_COLS), F32), pltpu.VMEM((seq, FFN_COLS), F32)],
        compiler_params=_cp("parallel"), name=name)(a, a, conv_w, conv_w, cb, cb, dact)
    cat = lambda p, q: jnp.concatenate([p, q], axis=1)
    return (dag, dav), cat(dwg, dwv), cat(dbg, dbv)


def _adam_math(w, g, m, v):
    m = ADAM_B1 * m + (1.0 - ADAM_B1) * g
    v = ADAM_B2 * v + (1.0 - ADAM_B2) * (g * g)
    m_hat = m / (1.0 - ADAM_B1 ** ADAM_STEP)
    v_hat = v / (1.0 - ADAM_B2 ** ADAM_STEP)
    return -ADAM_LR * (m_hat / (jnp.sqrt(v_hat) + ADAM_EPS) + ADAM_WD * w), m, v


def _row_tile(rows, cols):
    return _pick(rows, (256, 128, 64)) if cols <= 1024 else _pick(rows, (128, 64))


BF16_ROWS = 16
STREAM_BLOCK_BYTES = 3 * 1024 * 1024


def _stream_rows(rows, cols, itemsize):
    fits = [r for r in range(BF16_ROWS, rows + 1, BF16_ROWS) if rows % r == 0 and r * cols * itemsize <= STREAM_BLOCK_BYTES]
    return max(fits)


def adamw_layer(w_all, m_all, v_all, layer, g, prev, name):
    n, rows, cols = w_all.shape
    tr = _row_tile(rows, cols)

    def body(w_ref, m_ref, v_ref, g_ref, *rest):
        go_ref, d_ref, mo_ref, vo_ref = rest[-4:]
        g_ = g_ref[...]
        d, m_, v_ = _adam_math(w_ref[...], g_, m_ref[...], v_ref[...])
        go_ref[...] = g_
        d_ref[...] = d
        mo_ref[...] = m_
        vo_ref[...] = v_

    lay = pl.BlockSpec((None, tr, cols), lambda i: (layer, i, 0))
    in_specs = [lay, lay, lay, pl.BlockSpec((tr, cols), lambda i: (i, 0))]
    args = [w_all, m_all, v_all, g]
    aliases = {}
    if prev is not None:
        in_specs += [pl.BlockSpec(memory_space=pl.ANY)] * 4
        args += list(prev)
        aliases = {4 + k: k for k in range(4)}
    return pl.pallas_call(
        body, out_shape=(jax.ShapeDtypeStruct(w_all.shape, F32),) * 4, grid=(rows // tr,),
        in_specs=in_specs, out_specs=(lay,) * 4, input_output_aliases=aliases,
        compiler_params=_cp("parallel"), name=name)(*args)


def adamw_flat(w, g, m, v, name="adamw_small"):
    rows, cols = w.shape
    tr = _pick(rows, (128, 8))

    def body(w_ref, g_ref, m_ref, v_ref, d_ref, mo_ref, vo_ref):
        d_ref[...], mo_ref[...], vo_ref[...] = _adam_math(w_ref[...], g_ref[...], m_ref[...], v_ref[...])

    spec = pl.BlockSpec((tr, cols), lambda i: (i, 0))
    return pl.pallas_call(
        body, out_shape=(jax.ShapeDtypeStruct(w.shape, F32),) * 3, grid=(rows // tr,),
        in_specs=[spec] * 4, out_specs=(spec,) * 3, compiler_params=_cp("parallel"), name=name)(w, g, m, v)


def pair_sum(dw, got, core, name):
    _, rows, cols = dw.shape
    half = rows // 2
    tr = _stream_rows(half, cols, 2)
    nrb = half // tr

    def body(c_ref, a_ref, b_ref, o_ref):
        o_ref[...] = (a_ref[...].astype(F32) + b_ref[...].astype(F32)).astype(o_ref.dtype)

    return pl.pallas_call(
        body, out_shape=jax.ShapeDtypeStruct((N_CHIPS, half, cols), BF16),
        grid_spec=pltpu.PrefetchScalarGridSpec(
            num_scalar_prefetch=1, grid=(N_CHIPS, nrb),
            in_specs=[pl.BlockSpec((None, tr, cols), lambda s, i, c_ref: (s, c_ref[0] * nrb + i, 0)),
                      pl.BlockSpec((None, tr, cols), lambda s, i, c_ref: (s, i, 0))],
            out_specs=pl.BlockSpec((None, tr, cols), lambda s, i, c_ref: (s, i, 0))),
        compiler_params=_cp("parallel", "parallel"), name=name)(core, dw, got)


def chip_sum(own, parts, place, name):
    _, half, cols = parts.shape
    tr = _stream_rows(half, cols, 4)
    nrb = half // tr

    def body(p_ref, own_ref, a_ref, b_ref, c_ref, o_ref):
        o_ref[...] = ((own_ref[...].astype(F32) + a_ref[...].astype(F32)) + b_ref[...].astype(F32)) + c_ref[...].astype(F32)

    def slot(k):
        return pl.BlockSpec((None, tr, cols), lambda i, p: (jnp.bitwise_xor(p[0], k), i, 0))

    return pl.pallas_call(
        body, out_shape=jax.ShapeDtypeStruct((2 * half, cols), F32),
        grid_spec=pltpu.PrefetchScalarGridSpec(
            num_scalar_prefetch=1, grid=(nrb,), in_specs=[slot(0), slot(1), slot(2), slot(3)],
            out_specs=pl.BlockSpec((tr, cols), lambda i, p: (p[1] * nrb + i, 0))),
        compiler_params=_cp("parallel"), name=name)(place, own, parts, parts, parts)


def cast_to_slot(w_all, layer, place, name, after=None):
    _, rows, cols = w_all.shape
    tr = _stream_rows(rows, cols, 4)
    tied = [] if after is None else [after]

    def body(p_ref, w_ref, *rest):
        o_ref, token = rest[-2:]
        o_ref[...] = w_ref[...].astype(o_ref.dtype)
        token[...] = jnp.zeros_like(token)

    return pl.pallas_call(
        body, out_shape=(jax.ShapeDtypeStruct((N_CHIPS, rows, cols), BF16), TOKEN),
        grid_spec=pltpu.PrefetchScalarGridSpec(
            num_scalar_prefetch=1, grid=(rows // tr,),
            in_specs=[pl.BlockSpec((None, tr, cols), lambda i, p: (layer, i, 0))] + [ANY] * len(tied),
            out_specs=(pl.BlockSpec((None, tr, cols), lambda i, p: (p[0], i, 0)),
                       pl.BlockSpec(TOKEN.shape, lambda i, p: (0, 0)))),
        compiler_params=_cp("arbitrary"), name=name)(place, w_all, *tied)


ANY = pl.BlockSpec(memory_space=pl.ANY)


def _place():
    x, y, c = lax.axis_index("x"), lax.axis_index("y"), lax.axis_index("c")
    others = [(1 - x, y), (x, 1 - y), (1 - x, 1 - y)]
    return x, y, c, 2 * x + y, others


def _remote(src, dst, send_sem, recv_sem, dev):
    return pltpu.make_async_remote_copy(src_ref=src, dst_ref=dst, send_sem=send_sem, recv_sem=recv_sem,
                                        device_id=dev, device_id_type=MESH)


HBM = pl.BlockSpec(memory_space=pltpu.HBM)
SEM = pl.BlockSpec(memory_space=pltpu.SEMAPHORE)
EFFECT = pltpu.SideEffectType.DATAFLOW_SIDE_EFFECTING
TOKEN = jax.ShapeDtypeStruct((8, LANES), F32)


def _in_hbm(a):
    return pltpu.with_memory_space_constraint(a, pltpu.HBM)


def _gather_copies(bufs, send_sems, recv_sems):
    x, y, c, me, others = _place()
    out = []
    for w, buf in enumerate(bufs):
        half = buf.shape[1] // 2
        mine = pl.ds(c * half, half)
        for k, (ox, oy) in enumerate(others):
            sems = send_sems.at[3 * w + k], recv_sems.at[3 * w + k]
            out.append((_remote(buf.at[me, mine], buf.at[me, mine], *sems, (ox, oy, c)),
                        _remote(buf.at[me, mine], buf.at[2 * ox + oy, mine], *sems, (ox, oy, c))))
    return out


def _forward_copies(bufs, send_sems, recv_sems):
    x, y, c, me, others = _place()
    out = []
    for w, buf in enumerate(bufs):
        half = buf.shape[1] // 2
        mine, theirs = pl.ds(c * half, half), pl.ds((1 - c) * half, half)
        for k, (ox, oy) in enumerate(others):
            sems = send_sems.at[3 * w + k], recv_sems.at[3 * w + k]
            slot = 2 * ox + oy
            out.append((_remote(buf.at[slot, mine], buf.at[slot, mine], *sems, (x, y, 1 - c)),
                        _remote(buf.at[slot, mine], buf.at[slot, theirs], *sems, (x, y, 1 - c))))
    return out


def _join_copies(grads, send_sems, recv_sems):
    x, y, c, _, _ = _place()
    out = []
    for w, g in enumerate(grads):
        half = g.shape[0] // 2
        mine, theirs = pl.ds(c * half, half), pl.ds((1 - c) * half, half)
        sems = send_sems.at[w], recv_sems.at[w]
        out.append((_remote(g.at[mine], g.at[mine], *sems, (x, y, 1 - c)), _remote(g.at[mine], g.at[theirs], *sems, (x, y, 1 - c))))
    return out


def _slot_copies(bufs, send_sems, recv_sems):
    x, y, c, me, others = _place()
    out = []
    for w, buf in enumerate(bufs):
        for k, (ox, oy) in enumerate(others):
            sems = send_sems.at[3 * w + k], recv_sems.at[3 * w + k]
            out.append((_remote(buf.at[me], buf.at[me], *sems, (ox, oy, c)),
                        _remote(buf.at[me], buf.at[2 * ox + oy], *sems, (ox, oy, c))))
    return out


def _ring_copies(stage):
    def copies(bufs, send_sems, recv_sems):
        x, y, c, me, _ = _place()
        x_nb, y_nb = (1 - x, y, c), (x, 1 - y, c)
        slot_x, slot_y, slot_d = 2 * (1 - x) + y, 2 * x + (1 - y), 2 * (1 - x) + (1 - y)
        out = []
        for w, buf in enumerate(bufs):
            half = buf.shape[1] // 2
            sems = [(send_sems.at[2 * w + k], recv_sems.at[2 * w + k]) for k in range(2)]
            if stage == 1:
                mine = pl.ds(c * half, half)
                out += [(_remote(buf.at[me, mine], buf.at[me, mine], *sems[0], x_nb),
                         _remote(buf.at[me, mine], buf.at[slot_x, mine], *sems[0], x_nb)),
                        (_remote(buf.at[me, mine], buf.at[me, mine], *sems[1], y_nb),
                         _remote(buf.at[me, mine], buf.at[slot_y, mine], *sems[1], y_nb))]
            else:
                first = pl.ds(c * half, half // 2)
                second = pl.ds(c * half + half // 2, half // 2)
                out += [(_remote(buf.at[slot_x, first], buf.at[slot_x, first], *sems[0], y_nb),
                         _remote(buf.at[slot_x, first], buf.at[slot_d, first], *sems[0], y_nb)),
                        (_remote(buf.at[slot_y, second], buf.at[slot_y, second], *sems[1], x_nb),
                         _remote(buf.at[slot_y, second], buf.at[slot_d, second], *sems[1], x_nb))]
        return out
    return copies


IN_PLACE = dict(gather=(_gather_copies, 3), forward=(_forward_copies, 3), join=(_join_copies, 1), slots=(_slot_copies, 3),
                ring1=(_ring_copies(1), 2), ring2=(_ring_copies(2), 2))


def copies_start(kind, bufs, name, after=None):
    n = len(bufs)
    copies, per_buf = IN_PLACE[kind]
    tied = [] if after is None else [after]

    def body(*refs):
        ins, (send_sems, recv_sems), token = refs[:n], refs[n + len(tied):n + len(tied) + 2], refs[-1]
        for sent, _ in copies(ins, send_sems, recv_sems):
            sent.start()
        token[...] = jnp.zeros_like(token)

    outs = pl.pallas_call(
        body, name=name,
        out_shape=(pltpu.SemaphoreType.DMA((per_buf * n,)), pltpu.SemaphoreType.DMA((per_buf * n,)),
                   *[pltpu.HBM(b.shape, b.dtype) for b in bufs], TOKEN),
        in_specs=[HBM] * n + [ANY] * len(tied), out_specs=(SEM, SEM, *[HBM] * n, VM),
        input_output_aliases={w: 2 + w for w in range(n)},
        compiler_params=pltpu.CompilerParams(has_side_effects=EFFECT))(*[_in_hbm(b) for b in bufs], *tied)
    return outs[0], outs[1], list(outs[2:2 + n]), outs[-1]


def copies_wait(kind, send_sems, recv_sems, bufs, after, name):
    n = len(bufs)
    copies, _ = IN_PLACE[kind]

    def body(*refs):
        ins, (send_ref, recv_ref) = refs[:n], refs[n:n + 2]
        for sent, landed in copies(ins, send_ref, recv_ref):
            sent.wait_send()
            landed.wait_recv()

    return list(pl.pallas_call(
        body, name=name, out_shape=tuple(pltpu.HBM(b.shape, b.dtype) for b in bufs),
        in_specs=[HBM] * n + [SEM, SEM, ANY], out_specs=(HBM,) * n,
        input_output_aliases={w: w for w in range(n)},
        compiler_params=pltpu.CompilerParams(has_side_effects=EFFECT))(*bufs, send_sems, recv_sems, after))


def swap_halves(grads, name):
    n = len(grads)

    def body(*refs):
        ins, outs = refs[:n], refs[n:2 * n]
        send_sems, recv_sems = refs[2 * n:]
        x, y, c, _, _ = _place()
        copies = []
        for w in range(n):
            half = ins[w].shape[1] // 2
            cp = _remote(ins[w].at[:, pl.ds((1 - c) * half, half)], outs[w], send_sems.at[w], recv_sems.at[w], (x, y, 1 - c))
            cp.start()
            copies.append(cp)
        for cp in copies:
            cp.wait()

    return pl.pallas_call(
        body, out_shape=tuple(jax.ShapeDtypeStruct((N_CHIPS, g.shape[1] // 2, g.shape[2]), g.dtype) for g in grads),
        in_specs=[ANY] * n, out_specs=(ANY,) * n,
        scratch_shapes=[pltpu.SemaphoreType.DMA((n,)), pltpu.SemaphoreType.DMA((n,))], name=name)(*grads)


def _swap_copies(grads, lands, send_sems, recv_sems):
    x, y, c, _, _ = _place()
    out = []
    for w, (g, land) in enumerate(zip(grads, lands)):
        half = g.shape[1] // 2
        out.append(_remote(g.at[:, pl.ds((1 - c) * half, half)], land, send_sems.at[w], recv_sems.at[w], (x, y, 1 - c)))
    return out


def swap_start(grads, name):
    n = len(grads)

    def body(*refs):
        ins, lands, (send_sems, recv_sems), token = refs[:n], refs[n:2 * n], refs[2 * n:2 * n + 2], refs[-1]
        for cp in _swap_copies(ins, lands, send_sems, recv_sems):
            cp.start()
        token[...] = jnp.zeros_like(token)

    shapes = [(N_CHIPS, g.shape[1] // 2, g.shape[2]) for g in grads]
    zones = [_in_hbm(lax.empty(s, g.dtype)) for s, g in zip(shapes, grads)]
    outs = pl.pallas_call(
        body, name=name,
        out_shape=(pltpu.SemaphoreType.DMA((n,)), pltpu.SemaphoreType.DMA((n,)),
                   *[pltpu.HBM(g.shape, g.dtype) for g in grads], *[pltpu.HBM(s, g.dtype) for s, g in zip(shapes, grads)],
                   TOKEN),
        in_specs=[HBM] * (2 * n), out_specs=(SEM, SEM, *[HBM] * (2 * n), VM),
        input_output_aliases={w: 2 + w for w in range(2 * n)},
        compiler_params=pltpu.CompilerParams(has_side_effects=EFFECT))(*[_in_hbm(g) for g in grads], *zones)
    return outs[0], outs[1], list(outs[2:2 + n]), list(outs[2 + n:2 + 2 * n]), outs[-1]


def swap_wait(send_sems, recv_sems, grads, lands, after, name):
    n = len(grads)

    def body(*refs):
        ins, zones, (send_ref, recv_ref) = refs[:n], refs[n:2 * n], refs[2 * n:2 * n + 2]
        for cp in _swap_copies(ins, zones, send_ref, recv_ref):
            cp.wait_send()
            cp.wait_recv()

    outs = pl.pallas_call(
        body, name=name, out_shape=tuple(pltpu.HBM(a.shape, a.dtype) for a in list(grads) + list(lands)),
        in_specs=[HBM] * (2 * n) + [SEM, SEM, ANY], out_specs=(HBM,) * (2 * n),
        input_output_aliases={w: w for w in range(2 * n)},
        compiler_params=pltpu.CompilerParams(has_side_effects=EFFECT))(*grads, *lands, send_sems, recv_sems, after)
    return list(outs[:n]), list(outs[n:])


def _exchange_copies(sums, lands, send_sems, recv_sems):
    x, y, c, me, others = _place()
    out = []
    for w, (src, land) in enumerate(zip(sums, lands)):
        for k, (ox, oy) in enumerate(others):
            sems = send_sems.at[3 * w + k], recv_sems.at[3 * w + k]
            out.append((_remote(src.at[2 * ox + oy], land.at[me], *sems, (ox, oy, c)),
                        _remote(src.at[2 * ox + oy], land.at[2 * ox + oy], *sems, (ox, oy, c))))
    return out


def exchange_start(sums, name):
    n = len(sums)

    def body(*refs):
        ins, lands, (send_sems, recv_sems), token = refs[:n], refs[n:2 * n], refs[2 * n:2 * n + 2], refs[-1]
        for sent, _ in _exchange_copies(ins, lands, send_sems, recv_sems):
            sent.start()
        token[...] = jnp.zeros_like(token)

    zones = [_in_hbm(lax.empty(s.shape, s.dtype)) for s in sums]
    outs = pl.pallas_call(
        body, name=name,
        out_shape=(pltpu.SemaphoreType.DMA((3 * n,)), pltpu.SemaphoreType.DMA((3 * n,)),
                   *[pltpu.HBM(s.shape, s.dtype) for s in sums] * 2, TOKEN),
        in_specs=[HBM] * (2 * n), out_specs=(SEM, SEM, *[HBM] * (2 * n), VM),
        input_output_aliases={w: 2 + w for w in range(2 * n)},
        compiler_params=pltpu.CompilerParams(has_side_effects=EFFECT))(*[_in_hbm(s) for s in sums], *zones)
    return outs[0], outs[1], list(outs[2:2 + n]), list(outs[2 + n:2 + 2 * n]), outs[-1]


def exchange_wait(send_sems, recv_sems, sums, lands, after, name):
    n = len(sums)

    def body(*refs):
        ins, zones, (send_ref, recv_ref) = refs[:n], refs[n:2 * n], refs[2 * n:2 * n + 2]
        for sent, landed in _exchange_copies(ins, zones, send_ref, recv_ref):
            sent.wait_send()
            landed.wait_recv()

    outs = pl.pallas_call(
        body, name=name, out_shape=tuple(pltpu.HBM(s.shape, s.dtype) for s in sums) * 2,
        in_specs=[HBM] * (2 * n) + [SEM, SEM, ANY], out_specs=(HBM,) * (2 * n),
        input_output_aliases={w: w for w in range(2 * n)},
        compiler_params=pltpu.CompilerParams(has_side_effects=EFFECT))(*sums, *lands, send_sems, recv_sems, after)
    return list(outs[:n]), list(outs[n:])


VM = pl.BlockSpec(memory_space=pltpu.VMEM)


def small_allgather(buf, name="small_allgather"):
    def body(in_ref, out_ref, send_sems, recv_sems):
        x, y, c, me, others = _place()
        out_ref[me] = in_ref[...]
        copies = []
        for k, (ox, oy) in enumerate(others):
            cp = _remote(in_ref, out_ref.at[me], send_sems.at[k], recv_sems.at[k], (ox, oy, c))
            cp.start()
            copies.append(cp)
        for k, (ox, oy) in enumerate(others):
            landed = out_ref.at[2 * ox + oy]
            _remote(landed, landed, send_sems.at[k], recv_sems.at[k], (ox, oy, c)).wait_recv()
        for cp in copies:
            cp.wait_send()

    return pl.pallas_call(
        body, out_shape=jax.ShapeDtypeStruct((N_CHIPS,) + buf.shape, buf.dtype), in_specs=[VM], out_specs=VM,
        scratch_shapes=[pltpu.SemaphoreType.DMA((3,)), pltpu.SemaphoreType.DMA((3,))],
        compiler_params=pltpu.CompilerParams(vmem_limit_bytes=V7X_VMEM_LIMIT), name=name)(buf)


def pair_small(buf, after, name="pair_small"):
    def body(in_ref, after_ref, out_ref, sib_ref, send_sem, recv_sem):
        x, y, c, me, _ = _place()
        cp = _remote(in_ref, sib_ref, send_sem, recv_sem, (x, y, 1 - c))
        cp.start()
        cp.wait()
        out_ref[me] = in_ref[...] + sib_ref[...]

    return pl.pallas_call(
        body, out_shape=jax.ShapeDtypeStruct((N_CHIPS,) + buf.shape, buf.dtype), in_specs=[VM, ANY], out_specs=VM,
        scratch_shapes=[pltpu.VMEM(buf.shape, buf.dtype), pltpu.SemaphoreType.DMA, pltpu.SemaphoreType.DMA],
        compiler_params=pltpu.CompilerParams(vmem_limit_bytes=V7X_VMEM_LIMIT), name=name)(buf, after)


def sum_slots(slots, name="sum_slots"):
    _, rows, cols = slots.shape
    tr = _pick(rows, (128, 8))

    def body(s_ref, o_ref):
        o_ref[...] = ((s_ref[0] + s_ref[1]) + s_ref[2]) + s_ref[3]

    return pl.pallas_call(
        body, out_shape=jax.ShapeDtypeStruct((rows, cols), slots.dtype), grid=(rows // tr,),
        in_specs=[pl.BlockSpec((N_CHIPS, tr, cols), lambda i: (0, i, 0))],
        out_specs=pl.BlockSpec((tr, cols), lambda i: (i, 0)), compiler_params=_cp("parallel"), name=name)(slots)


def _pack_rows(arrays, row_multiple):
    flat = jnp.concatenate([a.reshape(-1) for a in arrays])
    rows = -(-flat.shape[0] // (LANES * row_multiple)) * row_multiple
    return jnp.pad(flat, (0, rows * LANES - flat.shape[0])).reshape(rows, LANES)


def _unpack_rows(buf, shapes):
    flat = buf.reshape(-1)
    out, at = [], 0
    for s in shapes:
        n = math.prod(s)
        out.append(flat[at:at + n].reshape(s))
        at += n
    return out


def _mixer_weights(i):
    j = i // 2
    mixer = "a" if i % 2 == 0 else "b"
    return [("w_mem_kv", i), (mixer + "_w_in", j), (mixer + "_w_out", j)]


def _ffn_weights(i):
    return [("ffn_w_up", i), ("ffn_w_down", i)]


def _mixer_fwd(i, x, mem, w, small, after):
    is_a = i % 2 == 0
    j = i // 2
    wkv, win, wout = w
    wkv = wkv.reshape(1, D_MODEL, 2 * MEM_W)
    h1 = rms_fwd(x, small["mix_norm_g"][i], BF16, name=f"mix_norm{i}", after=after)
    mem_n = rms_fwd(mem, small["mem_norm_g"][i], BF16, name=f"mem_norm{i}")
    kv = mm_nn(mem_n, wkv, F32, name=f"mem_kv{i}")
    proj = mm_nn(h1, win, F32, name=f"in_proj{i}")
    saved = dict(x0=x, h1=h1, mem_n=mem_n, kv=kv, proj=proj)
    if is_a:
        outs, lses = zip(*[attn_fwd(proj, g, name=f"attn_fwd{i}_{g}") for g in range(3)])
        comb, lse = attn_combine(outs, lses, name=f"attn_combine{i}")
        mem_out = mem_fwd(proj, 3 * A_QKV_W, kv, name=f"mem_fwd{i}")
        cat = jnp.concatenate([comb.astype(BF16), mem_out], axis=1)
        saved.update(comb=comb, lse=lse)
    else:
        wout = wout.reshape(1, B_W + MEM_W, D_MODEL)
        tok = sgu_fwd(proj, small["b_v_norm_g"][j], small["b_w_s"][j], small["bias_b"][j], name=f"sgu_fwd{i}")
        mem_out = mem_fwd(proj, 2 * B_W, kv, name=f"mem_fwd{i}")
        cat = jnp.concatenate([tok, mem_out], axis=1)
    x1 = mm_nn(cat, wout, F32, res=x, name=f"out_proj{i}")
    saved.update(cat=cat)
    return x1, saved


def _ffn_fwd(i, x1, w, small, after):
    wup, wdn = w
    h2 = rms_fwd(x1, small["ffn_norm_g"][i], BF16, name=f"ffn_norm{i}", after=after)
    a = mm_nn(h2, wup, F32, name=f"ffn_up{i}")
    act = ffn_act_fwd(a, small["ffn_conv_w"][i], small["ffn_conv_b"][i], name=f"ffn_act{i}")
    x2 = mm_nn(act, wdn.reshape(1, FF, D_MODEL), F32, res=x1, name=f"ffn_down{i}")
    return x2, dict(x1=x1, h2=h2, a=a, act=act)


def _ffn_bwd(i, dx2, w, small, sv, after):
    wup, wdn = w
    dx2, dx2_16 = dx2
    sg = {}
    dact = mm_nt(dx2_16, wdn.reshape(1, FF, D_MODEL), F32, name=f"d_act{i}", after=after)
    d_wdn = mm_tn(sv["act"], dx2_16, 1, BF16, name=f"d_wdown{i}").reshape(N_CHIPS, FF // N_CHIPS, D_MODEL)
    da, sg["ffn_conv_w"], sg["ffn_conv_b"] = ffn_act_bwd(sv["a"], small["ffn_conv_w"][i], small["ffn_conv_b"][i], dact,
                                                          name=f"ffn_act_bwd{i}")
    d_wup = mm_tn(sv["h2"], da, N_CHIPS, BF16, name=f"d_wup{i}")
    dh2 = mm_nt(da, wup, F32, name=f"d_h2_{i}")
    dx1, dx1_16, sg["ffn_norm_g"] = rms_bwd(sv["x1"], small["ffn_norm_g"][i], dh2, dres=dx2, name=f"ffn_norm_bwd{i}")
    return (dx1, dx1_16), [d_wup, d_wdn], sg


def _mixer_bwd(i, dx1, mem, w, small, sv, after):
    is_a = i % 2 == 0
    j = i // 2
    wkv, win, wout = w
    wkv = wkv.reshape(1, D_MODEL, 2 * MEM_W)
    dx1_32, dx1 = dx1
    sg = {}
    proj, kv = sv["proj"], sv["kv"]
    if is_a:
        dcat = mm_nt(dx1, wout, F32, name=f"d_cat{i}", after=after)
        d_wout = mm_tn(sv["cat"], dx1, N_CHIPS, BF16, name=f"d_wout{i}")
        dqm, dkv = mem_bwd(proj, 3 * A_QKV_W, kv, dcat, A_OUT_W, name=f"mem_bwd{i}")
        parts = [attn_bwd(proj, dcat, sv["comb"], sv["lse"], g, name=f"attn_bwd{i}_{g}") for g in range(3)]
        dproj = jnp.concatenate([parts[g][p].astype(BF16) for p in range(3) for g in range(3)] + [dqm], axis=1)
    else:
        dcat = mm_nt(dx1, wout.reshape(1, B_W + MEM_W, D_MODEL), F32, name=f"d_cat{i}", after=after)
        d_wout = mm_tn(sv["cat"], dx1, 1, BF16, name=f"d_wout{i}").reshape(N_CHIPS, (B_W + MEM_W) // N_CHIPS, D_MODEL)
        dqm, dkv = mem_bwd(proj, 2 * B_W, kv, dcat, B_W, name=f"mem_bwd{i}")
        w_s = small["b_w_s"][j]
        duv, sg["b_w_s"], dmix, sg["b_v_norm_g"] = sgu_bwd(proj, small["b_v_norm_g"][j], w_s, jnp.swapaxes(w_s, 1, 2),
                                                           small["bias_b"][j], dcat, name=f"sgu_bwd{i}")
        sg["b_s_bias"] = jnp.sum(dmix, axis=-1)
        dproj = jnp.concatenate([duv, dqm], axis=1)
    d_wkv = mm_tn(sv["mem_n"], dkv, 1, BF16, name=f"d_wkv{i}").reshape(N_CHIPS, D_MODEL // N_CHIPS, 2 * MEM_W)
    dmem_n = mm_nt(dkv, wkv, F32, name=f"d_mem_n{i}")
    _, _, sg["mem_norm_g"] = rms_bwd(mem, small["mem_norm_g"][i], dmem_n, name=f"mem_norm_bwd{i}")
    d_win = mm_tn(sv["h1"], dproj, N_CHIPS, BF16, name=f"d_win{i}")
    dh1 = mm_nt(dproj, win, F32, name=f"d_h1_{i}")
    dx0, dx0_16, sg["mix_norm_g"] = rms_bwd(sv["x0"], small["mix_norm_g"][i], dh1, dres=dx1_32, name=f"mix_norm_bwd{i}")
    return (dx0, dx0_16), [d_wkv, d_win, d_wout], sg


def _exchange_begin(grads, got, place, tag):
    sums = [pair_sum(g, o, place[1:], name=f"pair_sum{tag}_{k}") for k, (g, o) in enumerate(zip(grads, got))]
    send_sems, recv_sems, sums, lands, token = exchange_start(sums, name=f"exchange_start{tag}")
    return (send_sems, recv_sems, sums, lands), token


def _reduce_finish(started, place, after, tag):
    sums, parts = exchange_wait(*started, after, name=f"exchange_wait{tag}")
    halves = [chip_sum(s, p, place, name=f"chip_sum{tag}_{k}") for k, (s, p) in enumerate(zip(sums, parts))]
    return copies_start("join", halves, name=f"join_start_{tag}")


SMALL_SHARDED = ("b_v_norm_g", "ffn_conv_w")
SMALL_FULL_SHAPES = dict(mix_norm_g=(D_MODEL,), ffn_norm_g=(D_MODEL,), mem_norm_g=(D_MODEL,), b_v_norm_g=(B_W,),
                         b_w_s=(B_GROUPS, CHUNK, CHUNK), b_s_bias=(B_GROUPS, CHUNK), ffn_conv_w=(3, 2 * FF),
                         ffn_conv_b=(2 * FF,))
BIG = ("w_mem_kv", "a_w_in", "a_w_out", "b_w_in", "b_w_out", "ffn_w_up", "ffn_w_down")
WEIGHT_ORDER = ("mix_norm_g", "ffn_norm_g", "mem_norm_g", "w_mem_kv", "a_w_in", "a_w_out", "b_w_in", "b_v_norm_g", "b_w_s",
                "b_s_bias", "b_w_out", "ffn_w_up", "ffn_conv_w", "ffn_conv_b", "ffn_w_down", "final_norm_g")


def kernel(x, mem, mix_norm_g, ffn_norm_g, mem_norm_g, w_mem_kv, a_w_in, a_w_out, b_w_in, b_v_norm_g, b_w_s, b_s_bias, b_w_out, ffn_w_up, ffn_conv_w, ffn_conv_b, ffn_w_down, final_norm_g, loss_target, m_mix_norm_g, m_ffn_norm_g, m_mem_norm_g, m_w_mem_kv, m_a_w_in, m_a_w_out, m_b_w_in, m_b_v_norm_g, m_b_w_s, m_b_s_bias, m_b_w_out, m_ffn_w_up, m_ffn_conv_w, m_ffn_conv_b, m_ffn_w_down, m_final_norm_g, v_mix_norm_g, v_ffn_norm_g, v_mem_norm_g, v_w_mem_kv, v_a_w_in, v_a_w_out, v_b_w_in, v_b_v_norm_g, v_b_w_s, v_b_s_bias, v_b_w_out, v_ffn_w_up, v_ffn_conv_w, v_ffn_conv_b, v_ffn_w_down, v_final_norm_g):
    weights = dict(mix_norm_g=mix_norm_g, ffn_norm_g=ffn_norm_g, mem_norm_g=mem_norm_g, w_mem_kv=w_mem_kv, a_w_in=a_w_in,
                   a_w_out=a_w_out, b_w_in=b_w_in, b_v_norm_g=b_v_norm_g, b_w_s=b_w_s, b_s_bias=b_s_bias, b_w_out=b_w_out,
                   ffn_w_up=ffn_w_up, ffn_conv_w=ffn_conv_w, ffn_conv_b=ffn_conv_b, ffn_w_down=ffn_w_down,
                   final_norm_g=final_norm_g)
    mom1 = dict(mix_norm_g=m_mix_norm_g, ffn_norm_g=m_ffn_norm_g, mem_norm_g=m_mem_norm_g, w_mem_kv=m_w_mem_kv,
                a_w_in=m_a_w_in, a_w_out=m_a_w_out, b_w_in=m_b_w_in, b_v_norm_g=m_b_v_norm_g, b_w_s=m_b_w_s,
                b_s_bias=m_b_s_bias, b_w_out=m_b_w_out, ffn_w_up=m_ffn_w_up, ffn_conv_w=m_ffn_conv_w,
                ffn_conv_b=m_ffn_conv_b, ffn_w_down=m_ffn_w_down, final_norm_g=m_final_norm_g)
    mom2 = dict(mix_norm_g=v_mix_norm_g, ffn_norm_g=v_ffn_norm_g, mem_norm_g=v_mem_norm_g, w_mem_kv=v_w_mem_kv,
                a_w_in=v_a_w_in, a_w_out=v_a_w_out, b_w_in=v_b_w_in, b_v_norm_g=v_b_v_norm_g, b_w_s=v_b_w_s,
                b_s_bias=v_b_s_bias, b_w_out=v_b_w_out, ffn_w_up=v_ffn_w_up, ffn_conv_w=v_ffn_conv_w,
                ffn_conv_b=v_ffn_conv_b, ffn_w_down=v_ffn_w_down, final_norm_g=v_final_norm_g)
    chip = 2 * lax.axis_index("x") + lax.axis_index("y")
    place = jnp.stack([chip, lax.axis_index("c")]).astype(jnp.int32)
    x0, mem0, target = x[0], mem[0], loss_target[0]
    depth = DEPTH

    n_cw, n_vg = ffn_conv_w.size, b_v_norm_g.size
    gathered = small_allgather(_pack_rows([ffn_conv_w, b_v_norm_g], 8)).reshape(N_CHIPS, -1)
    conv_w_full = gathered[:, :n_cw].reshape(N_CHIPS, DEPTH, 3, 2 * FF // N_CHIPS).transpose(1, 2, 0, 3).reshape(DEPTH, 3, 2 * FF)
    vgain_full = gathered[:, n_cw:n_cw + n_vg].reshape(N_CHIPS, 2, B_W // N_CHIPS).transpose(1, 0, 2).reshape(2, B_W)
    small = dict(mix_norm_g=mix_norm_g, ffn_norm_g=ffn_norm_g, mem_norm_g=mem_norm_g, b_w_s=b_w_s, ffn_conv_b=ffn_conv_b,
                 ffn_conv_w=conv_w_full, b_v_norm_g=vgain_full,
                 bias_b=jnp.broadcast_to(b_s_bias[..., None], b_s_bias.shape + (CHUNK,)))

    half_layers = 2 * depth
    groups = [(_ffn_weights if b % 2 else _mixer_weights)(b // 2) for b in range(half_layers)]
    tags = [("f" if b % 2 else "m") + str(b // 2) for b in range(half_layers)]

    ring_from, early_from = 3, 6
    kind = lambda b: "gather" if b < ring_from else "ring1"

    def cast(b, after):
        bufs = []
        for n, l in groups[b]:
            buf, after = cast_to_slot(weights[n], l, place, name=f"cast_{n}{l}", after=after)
            bufs.append(buf)
        return bufs, after

    def start_gather(b, after):
        bufs = cast_early.pop(b) if b in cast_early else cast(b, after)[0]
        return copies_start(kind(b), bufs, name=f"{kind(b)}_start_{tags[b]}", after=after)

    def next_stage(b, now, new, after):
        send_sems, recv_sems, bufs, _ = flying.pop(b)
        bufs = copies_wait(now, send_sems, recv_sems, bufs, after, name=f"{now}_wait_{tags[b]}")
        flying[b] = copies_start(new, bufs, name=f"{new}_start_{tags[b]}")
        return flying[b][3]

    flying, cast_early, tie = {}, {}, gathered
    for b in range(3):
        flying[b] = start_gather(b, tie)
        tie = flying[b][3]
    for b in range(3, half_layers):
        cast_early[b], tie = cast(b, tie)
    w_half, saved_half, h = [], [], x0
    for b in range(half_layers):
        behind = tie if b == 0 else h
        if b < early_from:
            behind = next_stage(b, "gather" if b < ring_from else "ring2", "forward", behind)
        w_half.append(copies_wait("forward", *flying.pop(b)[:3], behind, name=f"forward_wait_{tags[b]}"))
        ready, tokens = w_half[b][0], []
        if early_from <= b + 1 < half_layers:
            tokens.append(next_stage(b + 1, "ring2", "forward", ready))
        for g in (b + 1, b + 2):
            if g < half_layers and ((g == b + 1 and ring_from <= g < early_from) or (g == b + 2 and g >= early_from)):
                tokens.append(next_stage(g, "ring1", "ring2", ready))
        if b + 3 < half_layers:
            flying[b + 3] = start_gather(b + 3, sum(tokens[1:], tokens[0]) if tokens else ready)
            tokens.append(flying[b + 3][3])
        tie = sum(tokens[1:], tokens[0]) if tokens else None
        if b % 2 == 0:
            h, sv = _mixer_fwd(b // 2, h, mem0, w_half[b], small, tie)
        else:
            h, sv = _ffn_fwd(b // 2, h, w_half[b], small, tie)
        saved_half.append(sv)
    w_mix, w_ffn, saved_mix, saved_ffn = w_half[0::2], w_half[1::2], saved_half[0::2], saved_half[1::2]
    loss_row, *dh, d_final = final_loss(h, final_norm_g, target)
    loss = lax.psum(loss_row[0, 0], ("x", "y", "c"))

    names = [n for n in WEIGHT_ORDER if n not in BIG]
    small_g = {n: [None] * weights[n].shape[0] for n in names if n != "final_norm_g"}
    big_out = {n: None for n in BIG}

    def keep_small(i, sg):
        for n, g in sg.items():
            small_g[n][i if len(small_g[n]) == depth else i // 2] = g.reshape(SMALL_FULL_SHAPES[n])

    joining = []

    def update(after):
        (send_sems, recv_sems, halves, _), group, tag = joining.pop()
        for (n, l), g in zip(group, copies_wait("join", send_sems, recv_sems, halves, after, name=f"join_wait_{tag}")):
            big_out[n] = adamw_layer(weights[n], mom1[n], mom2[n], l, g, big_out[n], name=f"adamw_{n}{l}")

    def finish_reduce(started, group, after, tag):
        join = _reduce_finish(started, place, after, tag)
        if joining:
            update(join[3])
        joining.append((join, group, tag))

    half_layers = 2 * depth
    swapping, exchanging, tie = None, [], None
    for k in range(half_layers):
        i = depth - 1 - k // 2
        if k % 2 == 0:
            dh, big_g, sg = _ffn_bwd(i, dh, w_ffn[i], small, saved_ffn[i], tie)
            group, tag = _ffn_weights(i), f"f{i}"
        else:
            dh, big_g, sg = _mixer_bwd(i, dh, mem0, w_mix[i], small, saved_mix[i], tie)
            group, tag = _mixer_weights(i), f"m{i}"
        keep_small(i, sg)
        started_now, swap_now, tokens = [], None, []
        if k < half_layers - 2:
            *swap_now, token = swap_start(big_g, name=f"swap_start_{tag}")
            swap_now = (swap_now, group, tag)
        else:
            started, token = _exchange_begin(big_g, swap_halves(big_g, name=f"swap_halves_{tag}"), place, tag)
            started_now.append((started, group, tag))
        tokens.append(token)
        if swapping is not None:
            swap_args, old_group, old_tag = swapping
            grads, got = swap_wait(*swap_args, dh[0], name=f"swap_wait_{old_tag}")
            started, token = _exchange_begin(grads, got, place, old_tag)
            started_now.append((started, old_group, old_tag))
            tokens.append(token)
        tie = sum(tokens[1:], tokens[0])
        for started, old_group, old_tag in exchanging:
            finish_reduce(started, old_group, dh[0], old_tag)
        swapping, exchanging = swap_now, started_now
    for started, old_group, old_tag in exchanging:
        finish_reduce(started, old_group, big_out["ffn_w_down"][0], old_tag)

    full_g = {n: (d_final.reshape(-1) if n == "final_norm_g" else jnp.stack(small_g[n])) for n in names}
    shapes = [full_g[n].shape for n in names]
    flying_small = copies_start("slots", [pair_small(_pack_rows([full_g[n] for n in names], 8), tie)], name="small_start")
    update(flying_small[3])
    reduced = sum_slots(copies_wait("slots", *flying_small[:3], big_out["w_mem_kv"][0], name="small_wait")[0])
    summed = dict(zip(names, _unpack_rows(reduced, shapes)))
    for n in SMALL_SHARDED:
        width = weights[n].shape[-1]
        summed[n] = lax.dynamic_slice_in_dim(summed[n], chip * width, width, axis=summed[n].ndim - 1)
    own_shapes = [weights[n].shape for n in names]
    pack = lambda d: _pack_rows([d[n] for n in names], 128)
    small_out = [_unpack_rows(b, own_shapes) for b in adamw_flat(pack(weights), pack(summed), pack(mom1), pack(mom2))]
    outs = {}
    for k, n in enumerate(names):
        outs[n] = (summed[n], small_out[0][k], small_out[1][k], small_out[2][k])
    outs.update(big_out)
    return (loss, dh[0][None], *[outs[n][0] for n in WEIGHT_ORDER], *[outs[n][1] for n in WEIGHT_ORDER],
            *[outs[n][2] for n in WEIGHT_ORDER], *[outs[n][3] for n in WEIGHT_ORDER])
```

```python
import math

import numpy as np
import jax
import jax.numpy as jnp
from jax import lax
from jax.experimental import pallas as pl
from jax.experimental.pallas import tpu as pltpu

F32 = jnp.float32
BF16 = jnp.bfloat16
MESH = pl.DeviceIdType.MESH

D_MODEL = 2048
SEQ = 2048
DEPTH = 4
EPS = 1e-6
NEG = -1e30
HEAD = 128
A_PATTERNS = ((128, 1), (512, 4), (2048, 16))
A_QKV_W = 1536
A_OUT_W = 512
A_IN = 5120
QBLK = 128
N_SIDE = 64
CHUNK = 128
B_GROUPS = 12
B_W = 1536
B_IN = 3584
MEM_LEN = 256
MEM_HEADS = 4
MEM_W = 512
FF = 5632
ADAM_LR, ADAM_B1, ADAM_B2, ADAM_EPS, ADAM_WD, ADAM_STEP = 0.001, 0.9, 0.999, 1e-08, 0.01, 10
N_CHIPS = 4

LANES = 128
V7X_VMEM_LIMIT = 56 * 1024 * 1024


def _cp(*sem):
    return pltpu.CompilerParams(dimension_semantics=sem, vmem_limit_bytes=V7X_VMEM_LIMIT)


def _pick(dim, prefs):
    for p in prefs:
        if dim % p == 0:
            return p
    raise ValueError(f"no tile for {dim} in {prefs}")


def _gelu_parts(x):
    cdf = 0.5 * (1.0 + lax.erf(x * (1.0 / math.sqrt(2.0))))
    pdf = jnp.exp(-0.5 * x * x) * (1.0 / math.sqrt(2.0 * math.pi))
    return x * cdf, cdf + x * pdf


def _gelu(x):
    return 0.5 * x * (1.0 + lax.erf(x * (1.0 / math.sqrt(2.0))))


TM_PREFS = (1024, 512, 256, 128)
TN_PREFS = (1408, 1280, 1024, 896, 512, 256, 128)
TK_PREFS = (2816, 2048, 1408, 1280, 1024, 896, 512, 256, 128)


def _mm_body(nk, dims, has_res, halves=None):
    def body(*refs):
        refs = list(refs)
        ops = []
        for operand in (0, 1):
            if halves is not None and halves[0] == operand:
                first, second = refs.pop(0), refs.pop(0)
                ops.append(jnp.where(pl.program_id(halves[1]) < halves[2], first[...], second[...]))
            else:
                ops.append(refs.pop(0)[...])
        r_ref = refs.pop(0) if has_res else None
        o_ref = refs.pop(0)
        part = lax.dot_general(ops[0].astype(BF16), ops[1].astype(BF16), dims, preferred_element_type=F32)
        if nk == 1:
            if has_res:
                part = part + r_ref[...]
            o_ref[...] = part.astype(o_ref.dtype)
            return
        acc_ref = refs[-1]
        k = pl.program_id(2)

        @pl.when(k == 0)
        def _():
            acc_ref[...] = part

        @pl.when(k > 0)
        def _():
            acc_ref[...] += part

        @pl.when(k == nk - 1)
        def _():
            tot = acc_ref[...]
            if has_res:
                tot = tot + r_ref[...]
            o_ref[...] = tot.astype(o_ref.dtype)
    return body


def mm_nn(a, w, out_dtype, res=None, name="mm_nn"):
    m, kw = a.shape
    ns_, kw2, nsz = w.shape
    assert kw == kw2
    n = ns_ * nsz
    tm, tn, tk = _pick(m, TM_PREFS), _pick(nsz, TN_PREFS), _pick(kw, TK_PREFS)
    nb, nk = nsz // tn, kw // tk
    in_specs = [pl.BlockSpec((tm, tk), lambda i, j, k: (i, k)),
                pl.BlockSpec((None, tk, tn), lambda i, j, k: (j // nb, k, j % nb))]
    args = [a, w]
    if res is not None:
        in_specs.append(pl.BlockSpec((tm, tn), lambda i, j, k: (i, j)))
        args.append(res)
    return pl.pallas_call(
        _mm_body(nk, (((1,), (0,)), ((), ())), res is not None),
        out_shape=jax.ShapeDtypeStruct((m, n), out_dtype),
        grid=(m // tm, n // tn, nk), in_specs=in_specs,
        out_specs=pl.BlockSpec((tm, tn), lambda i, j, k: (i, j)),
        scratch_shapes=[pltpu.VMEM((tm, tn), F32)] if nk > 1 else [],
        compiler_params=_cp("parallel", "parallel", "arbitrary"), name=name)(*args)


def mm_nt(g, w, out_dtype, name="mm_nt", after=None):
    parts = list(g) if isinstance(g, (tuple, list)) else [g]
    m, n = parts[0].shape[0], sum(p.shape[1] for p in parts)
    ns_, kw, nsz = w.shape
    assert n == ns_ * nsz
    tm, tn, tk = _pick(m, TM_PREFS), _pick(kw, TN_PREFS), _pick(nsz, TK_PREFS)
    nb, nk = nsz // tk, n // tk
    n_first = parts[0].shape[1] // tk
    if len(parts) == 1:
        g_specs = [pl.BlockSpec((tm, tk), lambda i, j, k: (i, k))]
    else:
        g_specs = [pl.BlockSpec((tm, tk), lambda i, j, k: (i, jnp.minimum(k, n_first - 1))),
                   pl.BlockSpec((tm, tk), lambda i, j, k: (i, jnp.maximum(k - n_first, 0)))]
    body = _mm_body(nk, (((1,), (1,)), ((), ())), False, None if len(parts) == 1 else (0, 2, n_first))
    tied = [] if after is None else [after]
    n_in = len(parts) + 1
    return pl.pallas_call(
        (lambda *refs: body(*refs[:n_in], *refs[n_in + len(tied):])),
        out_shape=jax.ShapeDtypeStruct((m, kw), out_dtype),
        grid=(m // tm, kw // tn, nk),
        in_specs=g_specs + [pl.BlockSpec((None, tn, tk), lambda i, j, k: (k // nb, j, k % nb))] + [ANY] * len(tied),
        out_specs=pl.BlockSpec((tm, tn), lambda i, j, k: (i, j)),
        scratch_shapes=[pltpu.VMEM((tm, tn), F32)] if nk > 1 else [],
        compiler_params=_cp("parallel", "parallel", "arbitrary"), name=name)(*parts, w, *tied)


def mm_tn(a, g, n_shards, out_dtype, name="mm_tn"):
    parts = list(g) if isinstance(g, (tuple, list)) else [g]
    t, kw = a.shape
    n = sum(p.shape[1] for p in parts)
    assert t == parts[0].shape[0]
    nsz = n // n_shards
    tm, tn, tk = _pick(kw, TM_PREFS), _pick(nsz, TN_PREFS), _pick(t, TK_PREFS)
    nb, nk = nsz // tn, t // tk
    n_first = parts[0].shape[1] // tn
    if len(parts) == 1:
        g_specs = [pl.BlockSpec((tk, tn), lambda i, j, k: (k, j))]
    else:
        g_specs = [pl.BlockSpec((tk, tn), lambda i, j, k: (k, jnp.minimum(j, n_first - 1))),
                   pl.BlockSpec((tk, tn), lambda i, j, k: (k, jnp.maximum(j - n_first, 0)))]
    return pl.pallas_call(
        _mm_body(nk, (((0,), (0,)), ((), ())), False, None if len(parts) == 1 else (1, 1, n_first)),
        out_shape=jax.ShapeDtypeStruct((n_shards, kw, nsz), out_dtype),
        grid=(kw // tm, n // tn, nk),
        in_specs=[pl.BlockSpec((tk, tm), lambda i, j, k: (k, i))] + g_specs,
        out_specs=pl.BlockSpec((None, tm, tn), lambda i, j, k: (j // nb, i, j % nb)),
        scratch_shapes=[pltpu.VMEM((tm, tn), F32)] if nk > 1 else [],
        compiler_params=_cp("parallel", "parallel", "arbitrary"), name=name)(a, *parts)


ROW_TILE = 256


def _rms_stats(x):
    r = lax.rsqrt(jnp.mean(x * x, axis=-1, keepdims=True) + EPS)
    return r, x * r


def _rms_back(xh, r, g, dh):
    u = dh * g
    return r * (u - xh * jnp.mean(u * xh, axis=-1, keepdims=True))


def rms_fwd(x, g, out_dtype, name="rms_fwd", after=None):
    rows, d = x.shape
    tr = _pick(rows, (ROW_TILE, 128))
    tied = [] if after is None else [after]

    def body(x_ref, g_ref, *rest):
        o_ref = rest[-1]
        _, xh = _rms_stats(x_ref[...])
        o_ref[...] = (xh * g_ref[...]).astype(o_ref.dtype)

    return pl.pallas_call(
        body, out_shape=jax.ShapeDtypeStruct((rows, d), out_dtype), grid=(rows // tr,),
        in_specs=[pl.BlockSpec((tr, d), lambda i: (i, 0)), pl.BlockSpec((1, d), lambda i: (0, 0))] + [ANY] * len(tied),
        out_specs=pl.BlockSpec((tr, d), lambda i: (i, 0)),
        compiler_params=_cp("parallel"), name=name)(x, g.reshape(1, d), *tied)


def rms_bwd(x, g, dh, dres=None, name="rms_bwd"):
    rows, d = x.shape
    tr = _pick(rows, (ROW_TILE, 128))
    has_res = dres is not None

    def body(*refs):
        if has_res:
            x_ref, g_ref, dh_ref, dres_ref, dx_ref, dx16_ref, dg_ref = refs
        else:
            x_ref, g_ref, dh_ref, dx_ref, dx16_ref, dg_ref = refs
        r, xh = _rms_stats(x_ref[...])
        dh_ = dh_ref[...].astype(F32)
        part = jnp.sum(dh_ * xh, axis=0, keepdims=True)

        @pl.when(pl.program_id(0) == 0)
        def _():
            dg_ref[...] = part

        @pl.when(pl.program_id(0) > 0)
        def _():
            dg_ref[...] += part

        dx = _rms_back(xh, r, g_ref[...], dh_)
        if has_res:
            dx = dx + dres_ref[...]
        dx_ref[...] = dx
        dx16_ref[...] = dx.astype(BF16)

    row_spec = pl.BlockSpec((tr, d), lambda i: (i, 0))
    vec_spec = pl.BlockSpec((1, d), lambda i: (0, 0))
    args = [x, g.reshape(1, d), dh] + ([dres] if has_res else [])
    return pl.pallas_call(
        body, out_shape=(jax.ShapeDtypeStruct((rows, d), F32), jax.ShapeDtypeStruct((rows, d), BF16),
                         jax.ShapeDtypeStruct((1, d), F32)),
        grid=(rows // tr,), in_specs=[row_spec, vec_spec, row_spec] + ([row_spec] if has_res else []),
        out_specs=(row_spec, row_spec, vec_spec), compiler_params=_cp("arbitrary"), name=name)(*args)


def final_loss(x, g, target, name="final_loss"):
    rows, d = x.shape
    tr = _pick(rows, (ROW_TILE, 128))

    def body(x_ref, g_ref, t_ref, loss_ref, dx_ref, dx16_ref, dg_ref):
        r, xh = _rms_stats(x_ref[...])
        gain = g_ref[...]
        err = xh * gain - t_ref[...]
        sq = jnp.sum(jnp.sum(err * err, axis=1, keepdims=True), axis=0, keepdims=True) * (0.5 / d)
        dy = err * (1.0 / d)
        part = jnp.sum(dy * xh, axis=0, keepdims=True)

        @pl.when(pl.program_id(0) == 0)
        def _():
            dg_ref[...] = part
            loss_ref[...] = jnp.broadcast_to(sq, loss_ref.shape)

        @pl.when(pl.program_id(0) > 0)
        def _():
            dg_ref[...] += part
            loss_ref[...] += jnp.broadcast_to(sq, loss_ref.shape)

        dx = _rms_back(xh, r, gain, dy)
        dx_ref[...] = dx
        dx16_ref[...] = dx.astype(BF16)

    row_spec = pl.BlockSpec((tr, d), lambda i: (i, 0))
    vec_spec = pl.BlockSpec((1, d), lambda i: (0, 0))
    return pl.pallas_call(
        body, out_shape=(jax.ShapeDtypeStruct((1, LANES), F32), jax.ShapeDtypeStruct((rows, d), F32),
                         jax.ShapeDtypeStruct((rows, d), BF16), jax.ShapeDtypeStruct((1, d), F32)),
        grid=(rows // tr,), in_specs=[row_spec, vec_spec, row_spec],
        out_specs=(pl.BlockSpec((1, LANES), lambda i: (0, 0)), row_spec, row_spec, vec_spec),
        compiler_params=_cp("arbitrary"), name=name)(x, g.reshape(1, d), target)


def _alibi_slopes():
    return (2.0 ** (-8.0 * (np.arange(12) + 1) / 12)).astype(np.float32)


def _band_scores(q, k, q0, start, wk, slope):
    s = lax.dot_general(q, k, (((1,), (1,)), ((), ())), preferred_element_type=F32) * (HEAD ** -0.5)
    qpos = q0 + lax.broadcasted_iota(jnp.int32, (QBLK, wk), 0)
    kpos = start + lax.broadcasted_iota(jnp.int32, (QBLK, wk), 1)
    rel = jnp.abs(qpos - kpos)
    return jnp.where(rel <= N_SIDE, s - slope * rel.astype(F32), NEG)


def _attn_geometry(seq, dilation):
    length = seq // dilation
    return length, length // QBLK, min(2 * QBLK, length)


def _attn_window(n, length, wk):
    q0 = pl.multiple_of(n * QBLK, QBLK)
    start = pl.multiple_of(jnp.clip(n * QBLK - N_SIDE, 0, length - wk), N_SIDE)
    return q0, start


def _class_in(refs, scratch, r, dilation, length):
    if dilation == 1:
        return refs
    for ref, buf in zip(refs, scratch):
        buf[...] = ref[pl.ds(r, length, stride=dilation), :]
    return scratch


def _class_out(refs, scratch, r, dilation, length):
    if dilation > 1:
        for ref, buf in zip(refs, scratch):
            ref[pl.ds(r, length, stride=dilation), :] = buf[...]


def attn_fwd(proj, group, name):
    seq = proj.shape[0]
    dilation = A_PATTERNS[group][1]
    length, nblk, wk = _attn_geometry(seq, dilation)

    def body(slope_ref, q_ref, k_ref, v_ref, o_ref, lse_ref, *scratch):
        slope = slope_ref[group * 4 + pl.program_id(0)] * float(dilation)
        for r in range(dilation):
            q_c, k_c, v_c = _class_in((q_ref, k_ref, v_ref), scratch[:3], r, dilation, length)
            o_c, lse_c = (o_ref, lse_ref) if dilation == 1 else scratch[3:]

            def blk(n, carry):
                q0, start = _attn_window(n, length, wk)
                q = q_c[pl.ds(q0, QBLK), :].astype(BF16)
                k = k_c[pl.ds(start, wk), :].astype(BF16)
                v = v_c[pl.ds(start, wk), :].astype(BF16)
                s = _band_scores(q, k, q0, start, wk, slope)
                m = jnp.max(s, axis=-1, keepdims=True)
                p = jnp.exp(s - m)
                l = jnp.sum(p, axis=-1, keepdims=True)
                o = jnp.dot(p.astype(BF16), v, preferred_element_type=F32) / l
                o_c[pl.ds(q0, QBLK), :] = o
                lse_c[pl.ds(q0, QBLK), :] = jnp.broadcast_to(m + jnp.log(l), (QBLK, HEAD))
                return carry

            lax.fori_loop(0, nblk, blk, 0)
            _class_out((o_ref, lse_ref), scratch[3:], r, dilation, length)

    def part(p):
        return pl.BlockSpec((seq, HEAD), lambda h: (0, p * 12 + group * 4 + h))

    out_spec = pl.BlockSpec((seq, HEAD), lambda h: (0, h))
    return pl.pallas_call(
        body, out_shape=(jax.ShapeDtypeStruct((seq, A_OUT_W), F32),) * 2, grid=(4,),
        in_specs=[pl.BlockSpec(memory_space=pltpu.SMEM), part(0), part(1), part(2)],
        out_specs=(out_spec, out_spec),
        scratch_shapes=[pltpu.VMEM((length, HEAD), F32)] * (5 if dilation > 1 else 0),
        compiler_params=_cp("parallel"), name=name)(jnp.asarray(_alibi_slopes()), proj, proj, proj)


def attn_combine(os_, lses, name="attn_combine"):
    seq = os_[0].shape[0]
    tr = ROW_TILE

    def body(o0, o1, o2, l0, l1, l2, c_ref, lse_ref):
        a, b, c = l0[...], l1[...], l2[...]
        m = jnp.maximum(jnp.maximum(a, b), c)
        ea, eb, ec = jnp.exp(a - m), jnp.exp(b - m), jnp.exp(c - m)
        den = ea + eb + ec
        c_ref[...] = (ea * o0[...] + eb * o1[...] + ec * o2[...]) / den
        lse_ref[...] = m + jnp.log(den)

    spec = pl.BlockSpec((tr, A_OUT_W), lambda i: (i, 0))
    return pl.pallas_call(
        body, out_shape=(jax.ShapeDtypeStruct((seq, A_OUT_W), F32),) * 2, grid=(seq // tr,),
        in_specs=[spec] * 6, out_specs=(spec, spec), compiler_params=_cp("parallel"), name=name)(*os_, *lses)


def attn_bwd(proj, dcat, comb, lse, group, name):
    seq = proj.shape[0]
    dilation = A_PATTERNS[group][1]
    length, nblk, wk = _attn_geometry(seq, dilation)
    scale = HEAD ** -0.5

    def body(slope_ref, q_ref, k_ref, v_ref, do_ref, c_ref, lse_ref, dq_ref, dk_ref, dv_ref, *scratch):
        slope = slope_ref[group * 4 + pl.program_id(0)] * float(dilation)
        for r in range(dilation):
            q_c, k_c, v_c, do_c, c_c, lse_c = _class_in((q_ref, k_ref, v_ref, do_ref, c_ref, lse_ref), scratch[:6], r,
                                                        dilation, length)
            dq_c, dk_c, dv_c = (dq_ref, dk_ref, dv_ref) if dilation == 1 else scratch[6:]
            dk_c[...] = jnp.zeros_like(dk_c)
            dv_c[...] = jnp.zeros_like(dv_c)

            def blk(n, carry):
                q0, start = _attn_window(n, length, wk)
                rows = pl.ds(q0, QBLK)
                keys = pl.ds(start, wk)
                q = q_c[rows, :].astype(BF16)
                k = k_c[keys, :].astype(BF16)
                v = v_c[keys, :].astype(BF16)
                do = do_c[rows, :]
                s = _band_scores(q, k, q0, start, wk, slope)
                p = jnp.exp(s - lse_c[rows, :][:, :1])
                delta = jnp.sum(do * c_c[rows, :], axis=-1, keepdims=True)
                do16 = do.astype(BF16)
                dp = lax.dot_general(do16, v, (((1,), (1,)), ((), ())), preferred_element_type=F32)
                ds = (p * (dp - delta) * scale).astype(BF16)
                p16 = p.astype(BF16)
                dq_c[rows, :] = jnp.dot(ds, k, preferred_element_type=F32)
                dk_c[keys, :] += lax.dot_general(ds, q, (((0,), (0,)), ((), ())), preferred_element_type=F32)
                dv_c[keys, :] += lax.dot_general(p16, do16, (((0,), (0,)), ((), ())), preferred_element_type=F32)
                return carry

            lax.fori_loop(0, nblk, blk, 0)
            _class_out((dq_ref, dk_ref, dv_ref), scratch[6:], r, dilation, length)

    def part(p):
        return pl.BlockSpec((seq, HEAD), lambda h: (0, p * 12 + group * 4 + h))

    hs = pl.BlockSpec((seq, HEAD), lambda h: (0, h))
    return pl.pallas_call(
        body, out_shape=(jax.ShapeDtypeStruct((seq, A_OUT_W), F32),) * 3, grid=(4,),
        in_specs=[pl.BlockSpec(memory_space=pltpu.SMEM), part(0), part(1), part(2), hs, hs, hs],
        out_specs=(hs, hs, hs),
        scratch_shapes=[pltpu.VMEM((length, HEAD), F32)] * (9 if dilation > 1 else 0),
        compiler_params=_cp("parallel"), name=name,
    )(jnp.asarray(_alibi_slopes()), proj, proj, proj, dcat, comb, lse)


MEM_ROW_TILE = 512


def _mem_probs(q, k):
    s = lax.dot_general(q, k, (((1,), (1,)), ((), ())), preferred_element_type=F32) * (HEAD ** -0.5)
    p = jnp.exp(s - jnp.max(s, axis=-1, keepdims=True))
    return p / jnp.sum(p, axis=-1, keepdims=True)


def mem_fwd(proj, q_col, kv, name="mem_fwd"):
    seq = proj.shape[0]
    qb = q_col // HEAD

    def body(q_ref, k_ref, v_ref, o_ref):
        p = _mem_probs(q_ref[...].astype(BF16), k_ref[...].astype(BF16))
        o_ref[...] = jnp.dot(p.astype(BF16), v_ref[...].astype(BF16), preferred_element_type=F32).astype(o_ref.dtype)

    return pl.pallas_call(
        body, out_shape=jax.ShapeDtypeStruct((seq, MEM_W), BF16), grid=(MEM_HEADS, seq // MEM_ROW_TILE),
        in_specs=[pl.BlockSpec((MEM_ROW_TILE, HEAD), lambda h, i: (i, qb + h)),
                  pl.BlockSpec((MEM_LEN, HEAD), lambda h, i: (0, h)),
                  pl.BlockSpec((MEM_LEN, HEAD), lambda h, i: (0, MEM_HEADS + h))],
        out_specs=pl.BlockSpec((MEM_ROW_TILE, HEAD), lambda h, i: (i, h)),
        compiler_params=_cp("parallel", "parallel"), name=name)(proj, kv, kv)


def mem_bwd(proj, q_col, kv, dcat, do_col, name="mem_bwd"):
    seq = proj.shape[0]
    qb, ob = q_col // HEAD, do_col // HEAD
    scale = HEAD ** -0.5

    def body(q_ref, k_ref, v_ref, do_ref, dq_ref, dk_ref, dv_ref):
        q = q_ref[...].astype(BF16)
        k = k_ref[...].astype(BF16)
        v = v_ref[...].astype(BF16)
        do = do_ref[...].astype(BF16)
        p = _mem_probs(q, k)
        dp = lax.dot_general(do, v, (((1,), (1,)), ((), ())), preferred_element_type=F32)
        ds = (p * (dp - jnp.sum(dp * p, axis=-1, keepdims=True)) * scale).astype(BF16)
        dq_ref[...] = jnp.dot(ds, k, preferred_element_type=F32).astype(dq_ref.dtype)
        dk = lax.dot_general(ds, q, (((0,), (0,)), ((), ())), preferred_element_type=F32)
        dv = lax.dot_general(p.astype(BF16), do, (((0,), (0,)), ((), ())), preferred_element_type=F32)

        @pl.when(pl.program_id(1) == 0)
        def _():
            dk_ref[...] = dk
            dv_ref[...] = dv

        @pl.when(pl.program_id(1) > 0)
        def _():
            dk_ref[...] += dk
            dv_ref[...] += dv

    dq, dk, dv = pl.pallas_call(
        body, out_shape=(jax.ShapeDtypeStruct((seq, MEM_W), BF16), jax.ShapeDtypeStruct((MEM_LEN, MEM_W), F32),
                         jax.ShapeDtypeStruct((MEM_LEN, MEM_W), F32)),
        grid=(MEM_HEADS, seq // MEM_ROW_TILE),
        in_specs=[pl.BlockSpec((MEM_ROW_TILE, HEAD), lambda h, i: (i, qb + h)),
                  pl.BlockSpec((MEM_LEN, HEAD), lambda h, i: (0, h)),
                  pl.BlockSpec((MEM_LEN, HEAD), lambda h, i: (0, MEM_HEADS + h)),
                  pl.BlockSpec((MEM_ROW_TILE, HEAD), lambda h, i: (i, ob + h))],
        out_specs=(pl.BlockSpec((MEM_ROW_TILE, HEAD), lambda h, i: (i, h)),
                   pl.BlockSpec((MEM_LEN, HEAD), lambda h, i: (0, h)),
                   pl.BlockSpec((MEM_LEN, HEAD), lambda h, i: (0, h))),
        compiler_params=_cp("parallel", "arbitrary"), name=name)(proj, kv, kv, dcat)
    return dq, jnp.concatenate([dk, dv], axis=1)


def _sgu_front(x, gain):
    uv, duv = _gelu_parts(x)
    u, v = uv[:, :B_W], uv[:, B_W:]
    r, vh = _rms_stats(v)
    return u, duv, r, vh, vh * gain


def sgu_fwd(proj, gain, w_s, bias_b, name="sgu_fwd"):
    seq = proj.shape[0]

    def body(x_ref, gain_ref, ws_ref, bias_ref, o_ref):
        u, _, _, _, vn = _sgu_front(x_ref[...], gain_ref[...])
        for g in range(B_GROUPS):
            cs = slice(g * CHUNK, (g + 1) * CHUNK)
            mixed = jnp.dot(ws_ref[g].astype(BF16), vn[:, cs].astype(BF16), preferred_element_type=F32) + bias_ref[g]
            o_ref[:, cs] = (u[:, cs] * mixed).astype(o_ref.dtype)

    full = lambda shape: pl.BlockSpec(shape, lambda c: (0,) * len(shape))
    return pl.pallas_call(
        body, out_shape=jax.ShapeDtypeStruct((seq, B_W), BF16), grid=(seq // CHUNK,),
        in_specs=[pl.BlockSpec((CHUNK, 2 * B_W), lambda c: (c, 0)), full((1, B_W)),
                  full((B_GROUPS, CHUNK, CHUNK)), full((B_GROUPS, CHUNK, CHUNK))],
        out_specs=pl.BlockSpec((CHUNK, B_W), lambda c: (c, 0)),
        compiler_params=_cp("parallel"), name=name)(proj, gain.reshape(1, B_W), w_s, bias_b)


def sgu_bwd(proj, gain, w_s, w_s_t, bias_b, dcat, name="sgu_bwd"):
    seq = proj.shape[0]

    def body(x_ref, gain_ref, ws_ref, wst_ref, bias_ref, do_ref, dx_ref, dws_ref, dmix_ref, dgain_ref, dvn_ref):
        first = pl.program_id(0) == 0
        gain = gain_ref[...]
        u, duv, r, vh, vn = _sgu_front(x_ref[...], gain)
        do = do_ref[...]
        for g in range(B_GROUPS):
            cs = slice(g * CHUNK, (g + 1) * CHUNK)
            vg = vn[:, cs].astype(BF16)
            mixed = jnp.dot(ws_ref[g].astype(BF16), vg, preferred_element_type=F32) + bias_ref[g]
            dx_ref[:, cs] = (do[:, cs] * mixed * duv[:, cs]).astype(dx_ref.dtype)
            dmixed = do[:, cs] * u[:, cs]
            dm16 = dmixed.astype(BF16)
            dws = lax.dot_general(dm16, vg, (((1,), (1,)), ((), ())), preferred_element_type=F32)
            dvn_ref[:, cs] = jnp.dot(wst_ref[g].astype(BF16), dm16, preferred_element_type=F32)

            @pl.when(first)
            def _():
                dws_ref[g] = dws
                dmix_ref[g] = dmixed

            @pl.when(jnp.logical_not(first))
            def _():
                dws_ref[g] += dws
                dmix_ref[g] += dmixed

        dvn = dvn_ref[...]
        dgain = jnp.sum(dvn * vh, axis=0, keepdims=True)

        @pl.when(first)
        def _():
            dgain_ref[...] = dgain

        @pl.when(jnp.logical_not(first))
        def _():
            dgain_ref[...] += dgain

        dv = _rms_back(vh, r, gain, dvn)
        dx_ref[:, B_W:] = (dv * duv[:, B_W:]).astype(dx_ref.dtype)

    full = lambda shape: pl.BlockSpec(shape, lambda c: (0,) * len(shape))
    mats = full((B_GROUPS, CHUNK, CHUNK))
    return pl.pallas_call(
        body, out_shape=(jax.ShapeDtypeStruct((seq, 2 * B_W), BF16), jax.ShapeDtypeStruct((B_GROUPS, CHUNK, CHUNK), F32),
                         jax.ShapeDtypeStruct((B_GROUPS, CHUNK, CHUNK), F32), jax.ShapeDtypeStruct((1, B_W), F32)),
        grid=(seq // CHUNK,),
        in_specs=[pl.BlockSpec((CHUNK, 2 * B_W), lambda c: (c, 0)), full((1, B_W)), mats, mats, mats,
                  pl.BlockSpec((CHUNK, B_W), lambda c: (c, 0))],
        out_specs=(pl.BlockSpec((CHUNK, 2 * B_W), lambda c: (c, 0)), mats, mats, full((1, B_W))),
        scratch_shapes=[pltpu.VMEM((CHUNK, B_W), F32)],
        compiler_params=_cp("arbitrary"), name=name)(proj, gain.reshape(1, B_W), w_s, w_s_t, bias_b, dcat)


FFN_COLS = 256
FFN_ROWS = 64
SUBLANES = 8


def _window(ref, r0, first, last):
    cols = ref.shape[1]
    pad = jnp.zeros((SUBLANES, cols), F32)
    if first:
        return jnp.concatenate([pad, ref[pl.ds(0, FFN_ROWS + SUBLANES), :]], axis=0)
    if last:
        return jnp.concatenate([ref[pl.ds(r0 - SUBLANES, FFN_ROWS + SUBLANES), :], pad], axis=0)
    return ref[pl.ds(pl.multiple_of(r0 - SUBLANES, SUBLANES), FFN_ROWS + 2 * SUBLANES), :]


def _taps(win):
    mid = slice(SUBLANES, SUBLANES + FFN_ROWS)
    return pltpu.roll(win, 1, 0)[mid], win[mid], pltpu.roll(win, win.shape[0] - 1, 0)[mid]


def _row_steps(seq, step, carry):
    n = seq // FFN_ROWS
    carry = step(0, True, False, carry)
    carry = lax.fori_loop(1, n - 1, lambda i, c: step(pl.multiple_of(i * FFN_ROWS, FFN_ROWS), False, False, c), carry)
    return step(seq - FFN_ROWS, False, True, carry)


def _conv3(taps, w, b):
    prev, cur, nxt = taps
    return prev * w[0:1] + cur * w[1:2] + nxt * w[2:3] + b


def _fold(x):
    return jnp.sum(x.reshape(FFN_ROWS // SUBLANES, SUBLANES, x.shape[1]), axis=0)


FFN_FWD_COLS = 256


def _taps_whole(a):
    n = a.shape[0]
    rows = lax.broadcasted_iota(jnp.int32, a.shape, 0)
    return (jnp.where(rows == 0, 0.0, pltpu.roll(a, 1, 0)), a, jnp.where(rows == n - 1, 0.0, pltpu.roll(a, n - 1, 0)))


def ffn_act_fwd(a, conv_w, conv_b, name="ffn_act_fwd"):
    seq = a.shape[0]
    nb = FF // FFN_FWD_COLS

    def body(ag_ref, av_ref, wg_ref, wv_ref, bg_ref, bv_ref, o_ref):
        gate = _conv3(_taps_whole(ag_ref[...]), wg_ref[...], bg_ref[...])
        val = _conv3(_taps_whole(av_ref[...]), wv_ref[...], bv_ref[...])
        o_ref[...] = (_gelu(gate) * val).astype(o_ref.dtype)

    col = lambda rows, off: pl.BlockSpec((rows, FFN_FWD_COLS), lambda j: (0, j + off))
    cb = conv_b.reshape(1, 2 * FF)
    return pl.pallas_call(
        body, out_shape=jax.ShapeDtypeStruct((seq, FF), BF16), grid=(nb,),
        in_specs=[col(seq, 0), col(seq, nb), col(3, 0), col(3, nb), col(1, 0), col(1, nb)],
        out_specs=col(seq, 0), compiler_params=_cp("parallel"), name=name)(a, a, conv_w, conv_w, cb, cb)


def ffn_act_bwd(a, conv_w, conv_b, dact, name="ffn_act_bwd"):
    seq = a.shape[0]
    nb = FF // FFN_COLS

    def body(ag_ref, av_ref, wg_ref, wv_ref, bg_ref, bv_ref, d_ref, dag_ref, dav_ref, dwg_ref, dwv_ref, dbg_ref, dbv_ref,
             dcg_ref, dcv_ref):
        wg, wv, bg, bv = wg_ref[...], wv_ref[...], bg_ref[...], bv_ref[...]

        def conv_grads(r0, first, last, sums):
            g_taps = _taps(_window(ag_ref, r0, first, last))
            v_taps = _taps(_window(av_ref, r0, first, last))
            act, dact_dgate = _gelu_parts(_conv3(g_taps, wg, bg))
            d = d_ref[pl.ds(r0, FFN_ROWS), :].astype(F32)
            dcg = d * _conv3(v_taps, wv, bv) * dact_dgate
            dcv = d * act
            dcg_ref[pl.ds(r0, FFN_ROWS), :] = dcg
            dcv_ref[pl.ds(r0, FFN_ROWS), :] = dcv
            new = [_fold(dcg)] + [_fold(dcg * t) for t in g_taps] + [_fold(dcv)] + [_fold(dcv * t) for t in v_taps]
            return tuple(s + n for s, n in zip(sums, new))

        zero = jnp.zeros((SUBLANES, FFN_COLS), F32)
        sums = _row_steps(seq, conv_grads, (zero,) * 8)
        total = [jnp.sum(s, axis=0, keepdims=True) for s in sums]
        dbg_ref[...] = total[0]
        dbv_ref[...] = total[4]
        for k in range(3):
            dwg_ref[k:k + 1, :] = total[1 + k]
            dwv_ref[k:k + 1, :] = total[5 + k]

        def conv_transpose(r0, first, last, carry):
            for dc_ref, w, da_ref in ((dcg_ref, wg, dag_ref), (dcv_ref, wv, dav_ref)):
                prev, cur, nxt = _taps(_window(dc_ref, r0, first, last))
                da_ref[pl.ds(r0, FFN_ROWS), :] = (nxt * w[0:1] + cur * w[1:2] + prev * w[2:3]).astype(da_ref.dtype)
            return carry

        _row_steps(seq, conv_transpose, 0)

    col = lambda rows, off: pl.BlockSpec((rows, FFN_COLS), lambda j: (0, j + off))
    cb = conv_b.reshape(1, 2 * FF)
    dag, dav, dwg, dwv, dbg, dbv = pl.pallas_call(
        body, out_shape=(jax.ShapeDtypeStruct((seq, FF), BF16),) * 2 + (jax.ShapeDtypeStruct((3, FF), F32),) * 2
        + (jax.ShapeDtypeStruct((1, FF), F32),) * 2, grid=(nb,),
        in_specs=[col(seq, 0), col(seq, nb), col(3, 0), col(3, nb), col(1, 0), col(1, nb), col(seq, 0)],
        out_specs=(col(seq, 0), col(seq, 0), col(3, 0), col(3, 0), col(1, 0), col(1, 0)),
        scratch_shapes=[pltpu.VMEM((seq, FFN_COLS), F32), pltpu.VMEM((seq, FFN_COLS), F32)],
        compiler_params=_cp("parallel"), name=name)(a, a, conv_w, conv_w, cb, cb, dact)
    cat = lambda p, q: jnp.concatenate([p, q], axis=1)
    return (dag, dav), cat(dwg, dwv), cat(dbg, dbv)


def _adam_math(w, g, m, v):
    m = ADAM_B1 * m + (1.0 - ADAM_B1) * g
    v = ADAM_B2 * v + (1.0 - ADAM_B2) * (g * g)
    m_hat = m / (1.0 - ADAM_B1 ** ADAM_STEP)
    v_hat = v / (1.0 - ADAM_B2 ** ADAM_STEP)
    return -ADAM_LR * (m_hat / (jnp.sqrt(v_hat) + ADAM_EPS) + ADAM_WD * w), m, v


def _row_tile(rows, cols):
    return _pick(rows, (256, 128, 64)) if cols <= 1024 else _pick(rows, (128, 64))


BF16_ROWS = 16
STREAM_BLOCK_BYTES = 3 * 1024 * 1024


def _stream_rows(rows, cols, itemsize):
    fits = [r for r in range(BF16_ROWS, rows + 1, BF16_ROWS) if rows % r == 0 and r * cols * itemsize <= STREAM_BLOCK_BYTES]
    return max(fits)


def adamw_layer(w_all, m_all, v_all, layer, g, prev, name):
    n, rows, cols = w_all.shape
    tr = _row_tile(rows, cols)

    def body(w_ref, m_ref, v_ref, g_ref, *rest):
        go_ref, d_ref, mo_ref, vo_ref = rest[-4:]
        g_ = g_ref[...]
        d, m_, v_ = _adam_math(w_ref[...], g_, m_ref[...], v_ref[...])
        go_ref[...] = g_
        d_ref[...] = d
        mo_ref[...] = m_
        vo_ref[...] = v_

    lay = pl.BlockSpec((None, tr, cols), lambda i: (layer, i, 0))
    in_specs = [lay, lay, lay, pl.BlockSpec((tr, cols), lambda i: (i, 0))]
    args = [w_all, m_all, v_all, g]
    aliases = {}
    if prev is not None:
        in_specs += [pl.BlockSpec(memory_space=pl.ANY)] * 4
        args += list(prev)
        aliases = {4 + k: k for k in range(4)}
    return pl.pallas_call(
        body, out_shape=(jax.ShapeDtypeStruct(w_all.shape, F32),) * 4, grid=(rows // tr,),
        in_specs=in_specs, out_specs=(lay,) * 4, input_output_aliases=aliases,
        compiler_params=_cp("parallel"), name=name)(*args)


def adamw_flat(w, g, m, v, name="adamw_small"):
    rows, cols = w.shape
    tr = _pick(rows, (128, 8))

    def body(w_ref, g_ref, m_ref, v_ref, d_ref, mo_ref, vo_ref):
        d_ref[...], mo_ref[...], vo_ref[...] = _adam_math(w_ref[...], g_ref[...], m_ref[...], v_ref[...])

    spec = pl.BlockSpec((tr, cols), lambda i: (i, 0))
    return pl.pallas_call(
        body, out_shape=(jax.ShapeDtypeStruct(w.shape, F32),) * 3, grid=(rows // tr,),
        in_specs=[spec] * 4, out_specs=(spec,) * 3, compiler_params=_cp("parallel"), name=name)(w, g, m, v)


def pair_sum(dw, got, core, name):
    _, rows, cols = dw.shape
    half = rows // 2
    tr = _stream_rows(half, cols, 2)
    nrb = half // tr

    def body(c_ref, a_ref, b_ref, o_ref):
        o_ref[...] = (a_ref[...].astype(F32) + b_ref[...].astype(F32)).astype(o_ref.dtype)

    return pl.pallas_call(
        body, out_shape=jax.ShapeDtypeStruct((N_CHIPS, half, cols), BF16),
        grid_spec=pltpu.PrefetchScalarGridSpec(
            num_scalar_prefetch=1, grid=(N_CHIPS, nrb),
            in_specs=[pl.BlockSpec((None, tr, cols), lambda s, i, c_ref: (s, c_ref[0] * nrb + i, 0)),
                      pl.BlockSpec((None, tr, cols), lambda s, i, c_ref: (s, i, 0))],
            out_specs=pl.BlockSpec((None, tr, cols), lambda s, i, c_ref: (s, i, 0))),
        compiler_params=_cp("parallel", "parallel"), name=name)(core, dw, got)


def chip_sum(own, parts, place, name):
    _, half, cols = parts.shape
    tr = _stream_rows(half, cols, 4)
    nrb = half // tr

    def body(p_ref, own_ref, a_ref, b_ref, c_ref, o_ref):
        o_ref[...] = ((own_ref[...].astype(F32) + a_ref[...].astype(F32)) + b_ref[...].astype(F32)) + c_ref[...].astype(F32)

    def slot(k):
        return pl.BlockSpec((None, tr, cols), lambda i, p: (jnp.bitwise_xor(p[0], k), i, 0))

    return pl.pallas_call(
        body, out_shape=jax.ShapeDtypeStruct((2 * half, cols), F32),
        grid_spec=pltpu.PrefetchScalarGridSpec(
            num_scalar_prefetch=1, grid=(nrb,), in_specs=[slot(0), slot(1), slot(2), slot(3)],
            out_specs=pl.BlockSpec((tr, cols), lambda i, p: (p[1] * nrb + i, 0))),
        compiler_params=_cp("parallel"), name=name)(place, own, parts, parts, parts)


def cast_to_slot(w_all, layer, place, name, after=None):
    _, rows, cols = w_all.shape
    tr = _stream_rows(rows, cols, 4)
    tied = [] if after is None else [after]

    def body(p_ref, w_ref, *rest):
        o_ref, token = rest[-2:]
        o_ref[...] = w_ref[...].astype(o_ref.dtype)
        token[...] = jnp.zeros_like(token)

    return pl.pallas_call(
        body, out_shape=(jax.ShapeDtypeStruct((N_CHIPS, rows, cols), BF16), TOKEN),
        grid_spec=pltpu.PrefetchScalarGridSpec(
            num_scalar_prefetch=1, grid=(rows // tr,),
            in_specs=[pl.BlockSpec((None, tr, cols), lambda i, p: (layer, i, 0))] + [ANY] * len(tied),
            out_specs=(pl.BlockSpec((None, tr, cols), lambda i, p: (p[0], i, 0)),
                       pl.BlockSpec(TOKEN.shape, lambda i, p: (0, 0)))),
        compiler_params=_cp("arbitrary"), name=name)(place, w_all, *tied)


ANY = pl.BlockSpec(memory_space=pl.ANY)


def _place():
    x, y, c = lax.axis_index("x"), lax.axis_index("y"), lax.axis_index("c")
    others = [(1 - x, y), (x, 1 - y), (1 - x, 1 - y)]
    return x, y, c, 2 * x + y, others


def _remote(src, dst, send_sem, recv_sem, dev):
    return pltpu.make_async_remote_copy(src_ref=src, dst_ref=dst, send_sem=send_sem, recv_sem=recv_sem,
                                        device_id=dev, device_id_type=MESH)


HBM = pl.BlockSpec(memory_space=pltpu.HBM)
SEM = pl.BlockSpec(memory_space=pltpu.SEMAPHORE)
EFFECT = pltpu.SideEffectType.DATAFLOW_SIDE_EFFECTING
TOKEN = jax.ShapeDtypeStruct((8, LANES), F32)


def _in_hbm(a):
    return pltpu.with_memory_space_constraint(a, pltpu.HBM)


def _gather_copies(bufs, send_sems, recv_sems):
    x, y, c, me, others = _place()
    out = []
    for w, buf in enumerate(bufs):
        half = buf.shape[1] // 2
        mine = pl.ds(c * half, half)
        for k, (ox, oy) in enumerate(others):
            sems = send_sems.at[3 * w + k], recv_sems.at[3 * w + k]
            out.append((_remote(buf.at[me, mine], buf.at[me, mine], *sems, (ox, oy, c)),
                        _remote(buf.at[me, mine], buf.at[2 * ox + oy, mine], *sems, (ox, oy, c))))
    return out


def _forward_copies(bufs, send_sems, recv_sems):
    x, y, c, me, others = _place()
    out = []
    for w, buf in enumerate(bufs):
        half = buf.shape[1] // 2
        mine, theirs = pl.ds(c * half, half), pl.ds((1 - c) * half, half)
        for k, (ox, oy) in enumerate(others):
            sems = send_sems.at[3 * w + k], recv_sems.at[3 * w + k]
            slot = 2 * ox + oy
            out.append((_remote(buf.at[slot, mine], buf.at[slot, mine], *sems, (x, y, 1 - c)),
                        _remote(buf.at[slot, mine], buf.at[slot, theirs], *sems, (x, y, 1 - c))))
    return out


def _join_copies(grads, send_sems, recv_sems):
    x, y, c, _, _ = _place()
    out = []
    for w, g in enumerate(grads):
        half = g.shape[0] // 2
        mine, theirs = pl.ds(c * half, half), pl.ds((1 - c) * half, half)
        sems = send_sems.at[w], recv_sems.at[w]
        out.append((_remote(g.at[mine], g.at[mine], *sems, (x, y, 1 - c)), _remote(g.at[mine], g.at[theirs], *sems, (x, y, 1 - c))))
    return out


def _slot_copies(bufs, send_sems, recv_sems):
    x, y, c, me, others = _place()
    out = []
    for w, buf in enumerate(bufs):
        for k, (ox, oy) in enumerate(others):
            sems = send_sems.at[3 * w + k], recv_sems.at[3 * w + k]
            out.append((_remote(buf.at[me], buf.at[me], *sems, (ox, oy, c)),
                        _remote(buf.at[me], buf.at[2 * ox + oy], *sems, (ox, oy, c))))
    return out


def _ring_copies(stage):
    def copies(bufs, send_sems, recv_sems):
        x, y, c, me, _ = _place()
        x_nb, y_nb = (1 - x, y, c), (x, 1 - y, c)
        slot_x, slot_y, slot_d = 2 * (1 - x) + y, 2 * x + (1 - y), 2 * (1 - x) + (1 - y)
        out = []
        for w, buf in enumerate(bufs):
            half = buf.shape[1] // 2
            sems = [(send_sems.at[2 * w + k], recv_sems.at[2 * w + k]) for k in range(2)]
            if stage == 1:
                mine = pl.ds(c * half, half)
                out += [(_remote(buf.at[me, mine], buf.at[me, mine], *sems[0], x_nb),
                         _remote(buf.at[me, mine], buf.at[slot_x, mine], *sems[0], x_nb)),
                        (_remote(buf.at[me, mine], buf.at[me, mine], *sems[1], y_nb),
                         _remote(buf.at[me, mine], buf.at[slot_y, mine], *sems[1], y_nb))]
            else:
                first = pl.ds(c * half, half // 2)
                second = pl.ds(c * half + half // 2, half // 2)
                out += [(_remote(buf.at[slot_x, first], buf.at[slot_x, first], *sems[0], y_nb),
                         _remote(buf.at[slot_x, first], buf.at[slot_d, first], *sems[0], y_nb)),
                        (_remote(buf.at[slot_y, second], buf.at[slot_y, second], *sems[1], x_nb),
                         _remote(buf.at[slot_y, second], buf.at[slot_d, second], *sems[1], x_nb))]
        return out
    return copies


IN_PLACE = dict(gather=(_gather_copies, 3), forward=(_forward_copies, 3), join=(_join_copies, 1), slots=(_slot_copies, 3),
                ring1=(_ring_copies(1), 2), ring2=(_ring_copies(2), 2))


def copies_start(kind, bufs, name, after=None):
    n = len(bufs)
    copies, per_buf = IN_PLACE[kind]
    tied = [] if after is None else [after]

    def body(*refs):
        ins, (send_sems, recv_sems), token = refs[:n], refs[n + len(tied):n + len(tied) + 2], refs[-1]
        for sent, _ in copies(ins, send_sems, recv_sems):
            sent.start()
        token[...] = jnp.zeros_like(token)

    outs = pl.pallas_call(
        body, name=name,
        out_shape=(pltpu.SemaphoreType.DMA((per_buf * n,)), pltpu.SemaphoreType.DMA((per_buf * n,)),
                   *[pltpu.HBM(b.shape, b.dtype) for b in bufs], TOKEN),
        in_specs=[HBM] * n + [ANY] * len(tied), out_specs=(SEM, SEM, *[HBM] * n, VM),
        input_output_aliases={w: 2 + w for w in range(n)},
        compiler_params=pltpu.CompilerParams(has_side_effects=EFFECT))(*[_in_hbm(b) for b in bufs], *tied)
    return outs[0], outs[1], list(outs[2:2 + n]), outs[-1]


def copies_wait(kind, send_sems, recv_sems, bufs, after, name):
    n = len(bufs)
    copies, _ = IN_PLACE[kind]

    def body(*refs):
        ins, (send_ref, recv_ref) = refs[:n], refs[n:n + 2]
        for sent, landed in copies(ins, send_ref, recv_ref):
            sent.wait_send()
            landed.wait_recv()

    return list(pl.pallas_call(
        body, name=name, out_shape=tuple(pltpu.HBM(b.shape, b.dtype) for b in bufs),
        in_specs=[HBM] * n + [SEM, SEM, ANY], out_specs=(HBM,) * n,
        input_output_aliases={w: w for w in range(n)},
        compiler_params=pltpu.CompilerParams(has_side_effects=EFFECT))(*bufs, send_sems, recv_sems, after))


def swap_halves(grads, name):
    n = len(grads)

    def body(*refs):
        ins, outs = refs[:n], refs[n:2 * n]
        send_sems, recv_sems = refs[2 * n:]
        x, y, c, _, _ = _place()
        copies = []
        for w in range(n):
            half = ins[w].shape[1] // 2
            cp = _remote(ins[w].at[:, pl.ds((1 - c) * half, half)], outs[w], send_sems.at[w], recv_sems.at[w], (x, y, 1 - c))
            cp.start()
            copies.append(cp)
        for cp in copies:
            cp.wait()

    return pl.pallas_call(
        body, out_shape=tuple(jax.ShapeDtypeStruct((N_CHIPS, g.shape[1] // 2, g.shape[2]), g.dtype) for g in grads),
        in_specs=[ANY] * n, out_specs=(ANY,) * n,
        scratch_shapes=[pltpu.SemaphoreType.DMA((n,)), pltpu.SemaphoreType.DMA((n,))], name=name)(*grads)


def _swap_copies(grads, lands, send_sems, recv_sems):
    x, y, c, _, _ = _place()
    out = []
    for w, (g, land) in enumerate(zip(grads, lands)):
        half = g.shape[1] // 2
        out.append(_remote(g.at[:, pl.ds((1 - c) * half, half)], land, send_sems.at[w], recv_sems.at[w], (x, y, 1 - c)))
    return out


def swap_start(grads, name):
    n = len(grads)

    def body(*refs):
        ins, lands, (send_sems, recv_sems), token = refs[:n], refs[n:2 * n], refs[2 * n:2 * n + 2], refs[-1]
        for cp in _swap_copies(ins, lands, send_sems, recv_sems):
            cp.start()
        token[...] = jnp.zeros_like(token)

    shapes = [(N_CHIPS, g.shape[1] // 2, g.shape[2]) for g in grads]
    zones = [_in_hbm(lax.empty(s, g.dtype)) for s, g in zip(shapes, grads)]
    outs = pl.pallas_call(
        body, name=name,
        out_shape=(pltpu.SemaphoreType.DMA((n,)), pltpu.SemaphoreType.DMA((n,)),
                   *[pltpu.HBM(g.shape, g.dtype) for g in grads], *[pltpu.HBM(s, g.dtype) for s, g in zip(shapes, grads)],
                   TOKEN),
        in_specs=[HBM] * (2 * n), out_specs=(SEM, SEM, *[HBM] * (2 * n), VM),
        input_output_aliases={w: 2 + w for w in range(2 * n)},
        compiler_params=pltpu.CompilerParams(has_side_effects=EFFECT))(*[_in_hbm(g) for g in grads], *zones)
    return outs[0], outs[1], list(outs[2:2 + n]), list(outs[2 + n:2 + 2 * n]), outs[-1]


def swap_wait(send_sems, recv_sems, grads, lands, after, name):
    n = len(grads)

    def body(*refs):
        ins, zones, (send_ref, recv_ref) = refs[:n], refs[n:2 * n], refs[2 * n:2 * n + 2]
        for cp in _swap_copies(ins, zones, send_ref, recv_ref):
            cp.wait_send()
            cp.wait_recv()

    outs = pl.pallas_call(
        body, name=name, out_shape=tuple(pltpu.HBM(a.shape, a.dtype) for a in list(grads) + list(lands)),
        in_specs=[HBM] * (2 * n) + [SEM, SEM, ANY], out_specs=(HBM,) * (2 * n),
        input_output_aliases={w: w for w in range(2 * n)},
        compiler_params=pltpu.CompilerParams(has_side_effects=EFFECT))(*grads, *lands, send_sems, recv_sems, after)
    return list(outs[:n]), list(outs[n:])


def _exchange_copies(sums, lands, send_sems, recv_sems):
    x, y, c, me, others = _place()
    out = []
    for w, (src, land) in enumerate(zip(sums, lands)):
        for k, (ox, oy) in enumerate(others):
            sems = send_sems.at[3 * w + k], recv_sems.at[3 * w + k]
            out.append((_remote(src.at[2 * ox + oy], land.at[me], *sems, (ox, oy, c)),
                        _remote(src.at[2 * ox + oy], land.at[2 * ox + oy], *sems, (ox, oy, c))))
    return out


def exchange_start(sums, name):
    n = len(sums)

    def body(*refs):
        ins, lands, (send_sems, recv_sems), token = refs[:n], refs[n:2 * n], refs[2 * n:2 * n + 2], refs[-1]
        for sent, _ in _exchange_copies(ins, lands, send_sems, recv_sems):
            sent.start()
        token[...] = jnp.zeros_like(token)

    zones = [_in_hbm(lax.empty(s.shape, s.dtype)) for s in sums]
    outs = pl.pallas_call(
        body, name=name,
        out_shape=(pltpu.SemaphoreType.DMA((3 * n,)), pltpu.SemaphoreType.DMA((3 * n,)),
                   *[pltpu.HBM(s.shape, s.dtype) for s in sums] * 2, TOKEN),
        in_specs=[HBM] * (2 * n), out_specs=(SEM, SEM, *[HBM] * (2 * n), VM),
        input_output_aliases={w: 2 + w for w in range(2 * n)},
        compiler_params=pltpu.CompilerParams(has_side_effects=EFFECT))(*[_in_hbm(s) for s in sums], *zones)
    return outs[0], outs[1], list(outs[2:2 + n]), list(outs[2 + n:2 + 2 * n]), outs[-1]


def exchange_wait(send_sems, recv_sems, sums, lands, after, name):
    n = len(sums)

    def body(*refs):
        ins, zones, (send_ref, recv_ref) = refs[:n], refs[n:2 * n], refs[2 * n:2 * n + 2]
        for sent, landed in _exchange_copies(ins, zones, send_ref, recv_ref):
            sent.wait_send()
            landed.wait_recv()

    outs = pl.pallas_call(
        body, name=name, out_shape=tuple(pltpu.HBM(s.shape, s.dtype) for s in sums) * 2,
        in_specs=[HBM] * (2 * n) + [SEM, SEM, ANY], out_specs=(HBM,) * (2 * n),
        input_output_aliases={w: w for w in range(2 * n)},
        compiler_params=pltpu.CompilerParams(has_side_effects=EFFECT))(*sums, *lands, send_sems, recv_sems, after)
    return list(outs[:n]), list(outs[n:])


VM = pl.BlockSpec(memory_space=pltpu.VMEM)


def small_allgather(buf, name="small_allgather"):
    def body(in_ref, out_ref, send_sems, recv_sems):
        x, y, c, me, others = _place()
        out_ref[me] = in_ref[...]
        copies = []
        for k, (ox, oy) in enumerate(others):
            cp = _remote(in_ref, out_ref.at[me], send_sems.at[k], recv_sems.at[k], (ox, oy, c))
            cp.start()
            copies.append(cp)
        for k, (ox, oy) in enumerate(others):
            landed = out_ref.at[2 * ox + oy]
            _remote(landed, landed, send_sems.at[k], recv_sems.at[k], (ox, oy, c)).wait_recv()
        for cp in copies:
            cp.wait_send()

    return pl.pallas_call(
        body, out_shape=jax.ShapeDtypeStruct((N_CHIPS,) + buf.shape, buf.dtype), in_specs=[VM], out_specs=VM,
        scratch_shapes=[pltpu.SemaphoreType.DMA((3,)), pltpu.SemaphoreType.DMA((3,))],
        compiler_params=pltpu.CompilerParams(vmem_limit_bytes=V7X_VMEM_LIMIT), name=name)(buf)


def pair_small(buf, after, name="pair_small"):
    def body(in_ref, after_ref, out_ref, sib_ref, send_sem, recv_sem):
        x, y, c, me, _ = _place()
        cp = _remote(in_ref, sib_ref, send_sem, recv_sem, (x, y, 1 - c))
        cp.start()
        cp.wait()
        out_ref[me] = in_ref[...] + sib_ref[...]

    return pl.pallas_call(
        body, out_shape=jax.ShapeDtypeStruct((N_CHIPS,) + buf.shape, buf.dtype), in_specs=[VM, ANY], out_specs=VM,
        scratch_shapes=[pltpu.VMEM(buf.shape, buf.dtype), pltpu.SemaphoreType.DMA, pltpu.SemaphoreType.DMA],
        compiler_params=pltpu.CompilerParams(vmem_limit_bytes=V7X_VMEM_LIMIT), name=name)(buf, after)


def sum_slots(slots, name="sum_slots"):
    _, rows, cols = slots.shape
    tr = _pick(rows, (128, 8))

    def body(s_ref, o_ref):
        o_ref[...] = ((s_ref[0] + s_ref[1]) + s_ref[2]) + s_ref[3]

    return pl.pallas_call(
        body, out_shape=jax.ShapeDtypeStruct((rows, cols), slots.dtype), grid=(rows // tr,),
        in_specs=[pl.BlockSpec((N_CHIPS, tr, cols), lambda i: (0, i, 0))],
        out_specs=pl.BlockSpec((tr, cols), lambda i: (i, 0)), compiler_params=_cp("parallel"), name=name)(slots)


def _pack_rows(arrays, row_multiple):
    flat = jnp.concatenate([a.reshape(-1) for a in arrays])
    rows = -(-flat.shape[0] // (LANES * row_multiple)) * row_multiple
    return jnp.pad(flat, (0, rows * LANES - flat.shape[0])).reshape(rows, LANES)


def _unpack_rows(buf, shapes):
    flat = buf.reshape(-1)
    out, at = [], 0
    for s in shapes:
        n = math.prod(s)
        out.append(flat[at:at + n].reshape(s))
        at += n
    return out


def _mixer_weights(i):
    j = i // 2
    mixer = "a" if i % 2 == 0 else "b"
    return [("w_mem_kv", i), (mixer + "_w_in", j), (mixer + "_w_out", j)]


def _ffn_weights(i):
    return [("ffn_w_up", i), ("ffn_w_down", i)]


def _mixer_fwd(i, x, mem, w, small, after):
    is_a = i % 2 == 0
    j = i // 2
    wkv, win, wout = w
    wkv = wkv.reshape(1, D_MODEL, 2 * MEM_W)
    h1 = rms_fwd(x, small["mix_norm_g"][i], BF16, name=f"mix_norm{i}", after=after)
    mem_n = rms_fwd(mem, small["mem_norm_g"][i], BF16, name=f"mem_norm{i}")
    kv = mm_nn(mem_n, wkv, F32, name=f"mem_kv{i}")
    proj = mm_nn(h1, win, F32, name=f"in_proj{i}")
    saved = dict(x0=x, h1=h1, mem_n=mem_n, kv=kv, proj=proj)
    if is_a:
        outs, lses = zip(*[attn_fwd(proj, g, name=f"attn_fwd{i}_{g}") for g in range(3)])
        comb, lse = attn_combine(outs, lses, name=f"attn_combine{i}")
        mem_out = mem_fwd(proj, 3 * A_QKV_W, kv, name=f"mem_fwd{i}")
        cat = jnp.concatenate([comb.astype(BF16), mem_out], axis=1)
        saved.update(comb=comb, lse=lse)
    else:
        wout = wout.reshape(1, B_W + MEM_W, D_MODEL)
        tok = sgu_fwd(proj, small["b_v_norm_g"][j], small["b_w_s"][j], small["bias_b"][j], name=f"sgu_fwd{i}")
        mem_out = mem_fwd(proj, 2 * B_W, kv, name=f"mem_fwd{i}")
        cat = jnp.concatenate([tok, mem_out], axis=1)
    x1 = mm_nn(cat, wout, F32, res=x, name=f"out_proj{i}")
    saved.update(cat=cat)
    return x1, saved


def _ffn_fwd(i, x1, w, small, after):
    wup, wdn = w
    h2 = rms_fwd(x1, small["ffn_norm_g"][i], BF16, name=f"ffn_norm{i}", after=after)
    a = mm_nn(h2, wup, F32, name=f"ffn_up{i}")
    act = ffn_act_fwd(a, small["ffn_conv_w"][i], small["ffn_conv_b"][i], name=f"ffn_act{i}")
    x2 = mm_nn(act, wdn.reshape(1, FF, D_MODEL), F32, res=x1, name=f"ffn_down{i}")
    return x2, dict(x1=x1, h2=h2, a=a, act=act)


def _ffn_bwd(i, dx2, w, small, sv, after):
    wup, wdn = w
    dx2, dx2_16 = dx2
    sg = {}
    dact = mm_nt(dx2_16, wdn.reshape(1, FF, D_MODEL), F32, name=f"d_act{i}", after=after)
    d_wdn = mm_tn(sv["act"], dx2_16, 1, BF16, name=f"d_wdown{i}").reshape(N_CHIPS, FF // N_CHIPS, D_MODEL)
    da, sg["ffn_conv_w"], sg["ffn_conv_b"] = ffn_act_bwd(sv["a"], small["ffn_conv_w"][i], small["ffn_conv_b"][i], dact,
                                                          name=f"ffn_act_bwd{i}")
    d_wup = mm_tn(sv["h2"], da, N_CHIPS, BF16, name=f"d_wup{i}")
    dh2 = mm_nt(da, wup, F32, name=f"d_h2_{i}")
    dx1, dx1_16, sg["ffn_norm_g"] = rms_bwd(sv["x1"], small["ffn_norm_g"][i], dh2, dres=dx2, name=f"ffn_norm_bwd{i}")
    return (dx1, dx1_16), [d_wup, d_wdn], sg


def _mixer_bwd(i, dx1, mem, w, small, sv, after):
    is_a = i % 2 == 0
    j = i // 2
    wkv, win, wout = w
    wkv = wkv.reshape(1, D_MODEL, 2 * MEM_W)
    dx1_32, dx1 = dx1
    sg = {}
    proj, kv = sv["proj"], sv["kv"]
    if is_a:
        dcat = mm_nt(dx1, wout, F32, name=f"d_cat{i}", after=after)
        d_wout = mm_tn(sv["cat"], dx1, N_CHIPS, BF16, name=f"d_wout{i}")
        dqm, dkv = mem_bwd(proj, 3 * A_QKV_W, kv, dcat, A_OUT_W, name=f"mem_bwd{i}")
        parts = [attn_bwd(proj, dcat, sv["comb"], sv["lse"], g, name=f"attn_bwd{i}_{g}") for g in range(3)]
        dproj = jnp.concatenate([parts[g][p].astype(BF16) for p in range(3) for g in range(3)] + [dqm], axis=1)
    else:
        dcat = mm_nt(dx1, wout.reshape(1, B_W + MEM_W, D_MODEL), F32, name=f"d_cat{i}", after=after)
        d_wout = mm_tn(sv["cat"], dx1, 1, BF16, name=f"d_wout{i}").reshape(N_CHIPS, (B_W + MEM_W) // N_CHIPS, D_MODEL)
        dqm, dkv = mem_bwd(proj, 2 * B_W, kv, dcat, B_W, name=f"mem_bwd{i}")
        w_s = small["b_w_s"][j]
        duv, sg["b_w_s"], dmix, sg["b_v_norm_g"] = sgu_bwd(proj, small["b_v_norm_g"][j], w_s, jnp.swapaxes(w_s, 1, 2),
                                                           small["bias_b"][j], dcat, name=f"sgu_bwd{i}")
        sg["b_s_bias"] = jnp.sum(dmix, axis=-1)
        dproj = jnp.concatenate([duv, dqm], axis=1)
    d_wkv = mm_tn(sv["mem_n"], dkv, 1, BF16, name=f"d_wkv{i}").reshape(N_CHIPS, D_MODEL // N_CHIPS, 2 * MEM_W)
    dmem_n = mm_nt(dkv, wkv, F32, name=f"d_mem_n{i}")
    _, _, sg["mem_norm_g"] = rms_bwd(mem, small["mem_norm_g"][i], dmem_n, name=f"mem_norm_bwd{i}")
    d_win = mm_tn(sv["h1"], dproj, N_CHIPS, BF16, name=f"d_win{i}")
    dh1 = mm_nt(dproj, win, F32, name=f"d_h1_{i}")
    dx0, dx0_16, sg["mix_norm_g"] = rms_bwd(sv["x0"], small["mix_norm_g"][i], dh1, dres=dx1_32, name=f"mix_norm_bwd{i}")
    return (dx0, dx0_16), [d_wkv, d_win, d_wout], sg


def _exchange_begin(grads, got, place, tag):
    sums = [pair_sum(g, o, place[1:], name=f"pair_sum{tag}_{k}") for k, (g, o) in enumerate(zip(grads, got))]
    send_sems, recv_sems, sums, lands, token = exchange_start(sums, name=f"exchange_start{tag}")
    return (send_sems, recv_sems, sums, lands), token


def _reduce_finish(started, place, after, tag):
    sums, parts = exchange_wait(*started, after, name=f"exchange_wait{tag}")
    halves = [chip_sum(s, p, place, name=f"chip_sum{tag}_{k}") for k, (s, p) in enumerate(zip(sums, parts))]
    return copies_start("join", halves, name=f"join_start_{tag}")


SMALL_SHARDED = ("b_v_norm_g", "ffn_conv_w")
SMALL_FULL_SHAPES = dict(mix_norm_g=(D_MODEL,), ffn_norm_g=(D_MODEL,), mem_norm_g=(D_MODEL,), b_v_norm_g=(B_W,),
                         b_w_s=(B_GROUPS, CHUNK, CHUNK), b_s_bias=(B_GROUPS, CHUNK), ffn_conv_w=(3, 2 * FF),
                         ffn_conv_b=(2 * FF,))
BIG = ("w_mem_kv", "a_w_in", "a_w_out", "b_w_in", "b_w_out", "ffn_w_up", "ffn_w_down")
WEIGHT_ORDER = ("mix_norm_g", "ffn_norm_g", "mem_norm_g", "w_mem_kv", "a_w_in", "a_w_out", "b_w_in", "b_v_norm_g", "b_w_s",
                "b_s_bias", "b_w_out", "ffn_w_up", "ffn_conv_w", "ffn_conv_b", "ffn_w_down", "final_norm_g")


def kernel(x, mem, mix_norm_g, ffn_norm_g, mem_norm_g, w_mem_kv, a_w_in, a_w_out, b_w_in, b_v_norm_g, b_w_s, b_s_bias, b_w_out, ffn_w_up, ffn_conv_w, ffn_conv_b, ffn_w_down, final_norm_g, loss_target, m_mix_norm_g, m_ffn_norm_g, m_mem_norm_g, m_w_mem_kv, m_a_w_in, m_a_w_out, m_b_w_in, m_b_v_norm_g, m_b_w_s, m_b_s_bias, m_b_w_out, m_ffn_w_up, m_ffn_conv_w, m_ffn_conv_b, m_ffn_w_down, m_final_norm_g, v_mix_norm_g, v_ffn_norm_g, v_mem_norm_g, v_w_mem_kv, v_a_w_in, v_a_w_out, v_b_w_in, v_b_v_norm_g, v_b_w_s, v_b_s_bias, v_b_w_out, v_ffn_w_up, v_ffn_conv_w, v_ffn_conv_b, v_ffn_w_down, v_final_norm_g):
    weights = dict(mix_norm_g=mix_norm_g, ffn_norm_g=ffn_norm_g, mem_norm_g=mem_norm_g, w_mem_kv=w_mem_kv, a_w_in=a_w_in,
                   a_w_out=a_w_out, b_w_in=b_w_in, b_v_norm_g=b_v_norm_g, b_w_s=b_w_s, b_s_bias=b_s_bias, b_w_out=b_w_out,
                   ffn_w_up=ffn_w_up, ffn_conv_w=ffn_conv_w, ffn_conv_b=ffn_conv_b, ffn_w_down=ffn_w_down,
                   final_norm_g=final_norm_g)
    mom1 = dict(mix_norm_g=m_mix_norm_g, ffn_norm_g=m_ffn_norm_g, mem_norm_g=m_mem_norm_g, w_mem_kv=m_w_mem_kv,
                a_w_in=m_a_w_in, a_w_out=m_a_w_out, b_w_in=m_b_w_in, b_v_norm_g=m_b_v_norm_g, b_w_s=m_b_w_s,
                b_s_bias=m_b_s_bias, b_w_out=m_b_w_out, ffn_w_up=m_ffn_w_up, ffn_conv_w=m_ffn_conv_w,
                ffn_conv_b=m_ffn_conv_b, ffn_w_down=m_ffn_w_down, final_norm_g=m_final_norm_g)
    mom2 = dict(mix_norm_g=v_mix_norm_g, ffn_norm_g=v_ffn_norm_g, mem_norm_g=v_mem_norm_g, w_mem_kv=v_w_mem_kv,
                a_w_in=v_a_w_in, a_w_out=v_a_w_out, b_w_in=v_b_w_in, b_v_norm_g=v_b_v_norm_g, b_w_s=v_b_w_s,
                b_s_bias=v_b_s_bias, b_w_out=v_b_w_out, ffn_w_up=v_ffn_w_up, ffn_conv_w=v_ffn_conv_w,
                ffn_conv_b=v_ffn_conv_b, ffn_w_down=v_ffn_w_down, final_norm_g=v_final_norm_g)
    chip = 2 * lax.axis_index("x") + lax.axis_index("y")
    place = jnp.stack([chip, lax.axis_index("c")]).astype(jnp.int32)
    x0, mem0, target = x[0], mem[0], loss_target[0]
    depth = DEPTH

    n_cw, n_vg = ffn_conv_w.size, b_v_norm_g.size
    gathered = small_allgather(_pack_rows([ffn_conv_w, b_v_norm_g], 8)).reshape(N_CHIPS, -1)
    conv_w_full = gathered[:, :n_cw].reshape(N_CHIPS, DEPTH, 3, 2 * FF // N_CHIPS).transpose(1, 2, 0, 3).reshape(DEPTH, 3, 2 * FF)
    vgain_full = gathered[:, n_cw:n_cw + n_vg].reshape(N_CHIPS, 2, B_W // N_CHIPS).transpose(1, 0, 2).reshape(2, B_W)
    small = dict(mix_norm_g=mix_norm_g, ffn_norm_g=ffn_norm_g, mem_norm_g=mem_norm_g, b_w_s=b_w_s, ffn_conv_b=ffn_conv_b,
                 ffn_conv_w=conv_w_full, b_v_norm_g=vgain_full,
                 bias_b=jnp.broadcast_to(b_s_bias[..., None], b_s_bias.shape + (CHUNK,)))

    half_layers = 2 * depth
    groups = [(_ffn_weights if b % 2 else _mixer_weights)(b // 2) for b in range(half_layers)]
    tags = [("f" if b % 2 else "m") + str(b // 2) for b in range(half_layers)]

    ring_from, early_from = 3, 6
    kind = lambda b: "gather" if b < ring_from else "ring1"

    def cast(b, after):
        bufs = []
        for n, l in groups[b]:
            buf, after = cast_to_slot(weights[n], l, place, name=f"cast_{n}{l}", after=after)
            bufs.append(buf)
        return bufs, after

    def start_gather(b, after):
        bufs = cast_early.pop(b) if b in cast_early else cast(b, after)[0]
        return copies_start(kind(b), bufs, name=f"{kind(b)}_start_{tags[b]}", after=after)

    def next_stage(b, now, new, after):
        send_sems, recv_sems, bufs, _ = flying.pop(b)
        bufs = copies_wait(now, send_sems, recv_sems, bufs, after, name=f"{now}_wait_{tags[b]}")
        flying[b] = copies_start(new, bufs, name=f"{new}_start_{tags[b]}")
        return flying[b][3]

    flying, cast_early, tie = {}, {}, gathered
    for b in range(3):
        flying[b] = start_gather(b, tie)
        tie = flying[b][3]
    for b in range(3, half_layers):
        cast_early[b], tie = cast(b, tie)
    w_half, saved_half, h = [], [], x0
    for b in range(half_layers):
        behind = tie if b == 0 else h
        if b < early_from:
            behind = next_stage(b, "gather" if b < ring_from else "ring2", "forward", behind)
        w_half.append(copies_wait("forward", *flying.pop(b)[:3], behind, name=f"forward_wait_{tags[b]}"))
        ready, tokens = w_half[b][0], []
        if early_from <= b + 1 < half_layers:
            tokens.append(next_stage(b + 1, "ring2", "forward", ready))
        for g in (b + 1, b + 2):
            if g < half_layers and ((g == b + 1 and ring_from <= g < early_from) or (g == b + 2 and g >= early_from)):
                tokens.append(next_stage(g, "ring1", "ring2", ready))
        if b + 3 < half_layers:
            flying[b + 3] = start_gather(b + 3, sum(tokens[1:], tokens[0]) if tokens else ready)
            tokens.append(flying[b + 3][3])
        tie = sum(tokens[1:], tokens[0]) if tokens else None
        if b % 2 == 0:
            h, sv = _mixer_fwd(b // 2, h, mem0, w_half[b], small, tie)
        else:
            h, sv = _ffn_fwd(b // 2, h, w_half[b], small, tie)
        saved_half.append(sv)
    w_mix, w_ffn, saved_mix, saved_ffn = w_half[0::2], w_half[1::2], saved_half[0::2], saved_half[1::2]
    loss_row, *dh, d_final = final_loss(h, final_norm_g, target)
    loss = lax.psum(loss_row[0, 0], ("x", "y", "c"))

    names = [n for n in WEIGHT_ORDER if n not in BIG]
    small_g = {n: [None] * weights[n].shape[0] for n in names if n != "final_norm_g"}
    big_out = {n: None for n in BIG}

    def keep_small(i, sg):
        for n, g in sg.items():
            small_g[n][i if len(small_g[n]) == depth else i // 2] = g.reshape(SMALL_FULL_SHAPES[n])

    joining = []

    def update(after):
        (send_sems, recv_sems, halves, _), group, tag = joining.pop()
        for (n, l), g in zip(group, copies_wait("join", send_sems, recv_sems, halves, after, name=f"join_wait_{tag}")):
            big_out[n] = adamw_layer(weights[n], mom1[n], mom2[n], l, g, big_out[n], name=f"adamw_{n}{l}")

    def finish_reduce(started, group, after, tag):
        join = _reduce_finish(started, place, after, tag)
        if joining:
            update(join[3])
        joining.append((join, group, tag))

    half_layers = 2 * depth
    swapping, exchanging, tie = None, [], None
    for k in range(half_layers):
        i = depth - 1 - k // 2
        if k % 2 == 0:
            dh, big_g, sg = _ffn_bwd(i, dh, w_ffn[i], small, saved_ffn[i], tie)
            group, tag = _ffn_weights(i), f"f{i}"
        else:
            dh, big_g, sg = _mixer_bwd(i, dh, mem0, w_mix[i], small, saved_mix[i], tie)
            group, tag = _mixer_weights(i), f"m{i}"
        keep_small(i, sg)
        started_now, swap_now, tokens = [], None, []
        if k < half_layers - 2:
            *swap_now, token = swap_start(big_g, name=f"swap_start_{tag}")
            swap_now = (swap_now, group, tag)
        else:
            started, token = _exchange_begin(big_g, swap_halves(big_g, name=f"swap_halves_{tag}"), place, tag)
            started_now.append((started, group, tag))
        tokens.append(token)
        if swapping is not None:
            swap_args, old_group, old_tag = swapping
            grads, got = swap_wait(*swap_args, dh[0], name=f"swap_wait_{old_tag}")
            started, token = _exchange_begin(grads, got, place, old_tag)
            started_now.append((started, old_group, old_tag))
            tokens.append(token)
        tie = sum(tokens[1:], tokens[0])
        for started, old_group, old_tag in exchanging:
            finish_reduce(started, old_group, dh[0], old_tag)
        swapping, exchanging = swap_now, started_now
    for started, old_group, old_tag in exchanging:
        finish_reduce(started, old_group, big_out["ffn_w_down"][0], old_tag)

    full_g = {n: (d_final.reshape(-1) if n == "final_norm_g" else jnp.stack(small_g[n])) for n in names}
    shapes = [full_g[n].shape for n in names]
    flying_small = copies_start("slots", [pair_small(_pack_rows([full_g[n] for n in names], 8), tie)], name="small_start")
    update(flying_small[3])
    reduced = sum_slots(copies_wait("slots", *flying_small[:3], big_out["w_mem_kv"][0], name="small_wait")[0])
    summed = dict(zip(names, _unpack_rows(reduced, shapes)))
    for n in SMALL_SHARDED:
        width = weights[n].shape[-1]
        summed[n] = lax.dynamic_slice_in_dim(summed[n], chip * width, width, axis=summed[n].ndim - 1)
    own_shapes = [weights[n].shape for n in names]
    pack = lambda d: _pack_rows([d[n] for n in names], 128)
    small_out = [_unpack_rows(b, own_shapes) for b in adamw_flat(pack(weights), pack(summed), pack(mom1), pack(mom2))]
    outs = {}
    for k, n in enumerate(names):
        outs[n] = (summed[n], small_out[0][k], small_out[1][k], small_out[2][k])
    outs.update(big_out)
    return (loss, dh[0][None], *[outs[n][0] for n in WEIGHT_ORDER], *[outs[n][1] for n in WEIGHT_ORDER],
            *[outs[n][2] for n in WEIGHT_ORDER], *[outs[n][3] for n in WEIGHT_ORDER])
```

```python
import math

import numpy as np
import jax
import jax.numpy as jnp
from jax import lax
from jax.experimental import pallas as pl
from jax.experimental.pallas import tpu as pltpu

F32 = jnp.float32
BF16 = jnp.bfloat16
MESH = pl.DeviceIdType.MESH

D_MODEL = 2048
SEQ = 2048
DEPTH = 4
EPS = 1e-6
NEG = -1e30
HEAD = 128
A_PATTERNS = ((128, 1), (512, 4), (2048, 16))
A_QKV_W = 1536
A_OUT_W = 512
A_IN = 5120
QBLK = 128
N_SIDE = 64
CHUNK = 128
B_GROUPS = 12
B_W = 1536
B_IN = 3584
MEM_LEN = 256
MEM_HEADS = 4
MEM_W = 512
FF = 5632
ADAM_LR, ADAM_B1, ADAM_B2, ADAM_EPS, ADAM_WD, ADAM_STEP = 0.001, 0.9, 0.999, 1e-08, 0.01, 10
N_CHIPS = 4

LANES = 128
V7X_VMEM_LIMIT = 56 * 1024 * 1024


def _cp(*sem):
    return pltpu.CompilerParams(dimension_semantics=sem, vmem_limit_bytes=V7X_VMEM_LIMIT)


def _pick(dim, prefs):
    for p in prefs:
        if dim % p == 0:
            return p
    raise ValueError(f"no tile for {dim} in {prefs}")


def _gelu_parts(x):
    cdf = 0.5 * (1.0 + lax.erf(x * (1.0 / math.sqrt(2.0))))
    pdf = jnp.exp(-0.5 * x * x) * (1.0 / math.sqrt(2.0 * math.pi))
    return x * cdf, cdf + x * pdf


def _gelu(x):
    return 0.5 * x * (1.0 + lax.erf(x * (1.0 / math.sqrt(2.0))))


TM_PREFS = (1024, 512, 256, 128)
TN_PREFS = (1408, 1280, 1024, 896, 512, 256, 128)
TK_PREFS = (2816, 2048, 1408, 1280, 1024, 896, 512, 256, 128)


def _mm_body(nk, dims, has_res, halves=None):
    def body(*refs):
        refs = list(refs)
        ops = []
        for operand in (0, 1):
            if halves is not None and halves[0] == operand:
                first, second = refs.pop(0), refs.pop(0)
                ops.append(jnp.where(pl.program_id(halves[1]) < halves[2], first[...], second[...]))
            else:
                ops.append(refs.pop(0)[...])
        r_ref = refs.pop(0) if has_res else None
        o_ref = refs.pop(0)
        part = lax.dot_general(ops[0].astype(BF16), ops[1].astype(BF16), dims, preferred_element_type=F32)
        if nk == 1:
            if has_res:
                part = part + r_ref[...]
            o_ref[...] = part.astype(o_ref.dtype)
            return
        acc_ref = refs[-1]
        k = pl.program_id(2)

        @pl.when(k == 0)
        def _():
            acc_ref[...] = part

        @pl.when(k > 0)
        def _():
            acc_ref[...] += part

        @pl.when(k == nk - 1)
        def _():
            tot = acc_ref[...]
            if has_res:
                tot = tot + r_ref[...]
            o_ref[...] = tot.astype(o_ref.dtype)
    return body


def mm_nn(a, w, out_dtype, res=None, name="mm_nn"):
    m, kw = a.shape
    ns_, kw2, nsz = w.shape
    assert kw == kw2
    n = ns_ * nsz
    tm, tn, tk = _pick(m, TM_PREFS), _pick(nsz, TN_PREFS), _pick(kw, TK_PREFS)
    nb, nk = nsz // tn, kw // tk
    in_specs = [pl.BlockSpec((tm, tk), lambda i, j, k: (i, k)),
                pl.BlockSpec((None, tk, tn), lambda i, j, k: (j // nb, k, j % nb))]
    args = [a, w]
    if res is not None:
        in_specs.append(pl.BlockSpec((tm, tn), lambda i, j, k: (i, j)))
        args.append(res)
    return pl.pallas_call(
        _mm_body(nk, (((1,), (0,)), ((), ())), res is not None),
        out_shape=jax.ShapeDtypeStruct((m, n), out_dtype),
        grid=(m // tm, n // tn, nk), in_specs=in_specs,
        out_specs=pl.BlockSpec((tm, tn), lambda i, j, k: (i, j)),
        scratch_shapes=[pltpu.VMEM((tm, tn), F32)] if nk > 1 else [],
        compiler_params=_cp("parallel", "parallel", "arbitrary"), name=name)(*args)


def mm_nt(g, w, out_dtype, name="mm_nt", after=None):
    parts = list(g) if isinstance(g, (tuple, list)) else [g]
    m, n = parts[0].shape[0], sum(p.shape[1] for p in parts)
    ns_, kw, nsz = w.shape
    assert n == ns_ * nsz
    tm, tn, tk = _pick(m, TM_PREFS), _pick(kw, TN_PREFS), _pick(nsz, TK_PREFS)
    nb, nk = nsz // tk, n // tk
    n_first = parts[0].shape[1] // tk
    if len(parts) == 1:
        g_specs = [pl.BlockSpec((tm, tk), lambda i, j, k: (i, k))]
    else:
        g_specs = [pl.BlockSpec((tm, tk), lambda i, j, k: (i, jnp.minimum(k, n_first - 1))),
                   pl.BlockSpec((tm, tk), lambda i, j, k: (i, jnp.maximum(k - n_first, 0)))]
    body = _mm_body(nk, (((1,), (1,)), ((), ())), False, None if len(parts) == 1 else (0, 2, n_first))
    tied = [] if after is None else [after]
    n_in = len(parts) + 1
    return pl.pallas_call(
        (lambda *refs: body(*refs[:n_in], *refs[n_in + len(tied):])),
        out_shape=jax.ShapeDtypeStruct((m, kw), out_dtype),
        grid=(m // tm, kw // tn, nk),
        in_specs=g_specs + [pl.BlockSpec((None, tn, tk), lambda i, j, k: (k // nb, j, k % nb))] + [ANY] * len(tied),
        out_specs=pl.BlockSpec((tm, tn), lambda i, j, k: (i, j)),
        scratch_shapes=[pltpu.VMEM((tm, tn), F32)] if nk > 1 else [],
        compiler_params=_cp("parallel", "parallel", "arbitrary"), name=name)(*parts, w, *tied)


def mm_tn(a, g, n_shards, out_dtype, name="mm_tn"):
    parts = list(g) if isinstance(g, (tuple, list)) else [g]
    t, kw = a.shape
    n = sum(p.shape[1] for p in parts)
    assert t == parts[0].shape[0]
    nsz = n // n_shards
    tm, tn, tk = _pick(kw, TM_PREFS), _pick(nsz, TN_PREFS), _pick(t, TK_PREFS)
    nb, nk = nsz // tn, t // tk
    n_first = parts[0].shape[1] // tn
    if len(parts) == 1:
        g_specs = [pl.BlockSpec((tk, tn), lambda i, j, k: (k, j))]
    else:
        g_specs = [pl.BlockSpec((tk, tn), lambda i, j, k: (k, jnp.minimum(j, n_first - 1))),
                   pl.BlockSpec((tk, tn), lambda i, j, k: (k, jnp.maximum(j - n_first, 0)))]
    return pl.pallas_call(
        _mm_body(nk, (((0,), (0,)), ((), ())), False, None if len(parts) == 1 else (1, 1, n_first)),
        out_shape=jax.ShapeDtypeStruct((n_shards, kw, nsz), out_dtype),
        grid=(kw // tm, n // tn, nk),
        in_specs=[pl.BlockSpec((tk, tm), lambda i, j, k: (k, i))] + g_specs,
        out_specs=pl.BlockSpec((None, tm, tn), lambda i, j, k: (j // nb, i, j % nb)),
        scratch_shapes=[pltpu.VMEM((tm, tn), F32)] if nk > 1 else [],
        compiler_params=_cp("parallel", "parallel", "arbitrary"), name=name)(a, *parts)


ROW_TILE = 256


def _rms_stats(x):
    r = lax.rsqrt(jnp.mean(x * x, axis=-1, keepdims=True) + EPS)
    return r, x * r


def _rms_back(xh, r, g, dh):
    u = dh * g
    return r * (u - xh * jnp.mean(u * xh, axis=-1, keepdims=True))


def rms_fwd(x, g, out_dtype, name="rms_fwd", after=None):
    rows, d = x.shape
    tr = _pick(rows, (ROW_TILE, 128))
    tied = [] if after is None else [after]

    def body(x_ref, g_ref, *rest):
        o_ref = rest[-1]
        _, xh = _rms_stats(x_ref[...])
        o_ref[...] = (xh * g_ref[...]).astype(o_ref.dtype)

    return pl.pallas_call(
        body, out_shape=jax.ShapeDtypeStruct((rows, d), out_dtype), grid=(rows // tr,),
        in_specs=[pl.BlockSpec((tr, d), lambda i: (i, 0)), pl.BlockSpec((1, d), lambda i: (0, 0))] + [ANY] * len(tied),
        out_specs=pl.BlockSpec((tr, d), lambda i: (i, 0)),
        compiler_params=_cp("parallel"), name=name)(x, g.reshape(1, d), *tied)


def rms_bwd(x, g, dh, dres=None, name="rms_bwd"):
    rows, d = x.shape
    tr = _pick(rows, (ROW_TILE, 128))
    has_res = dres is not None

    def body(*refs):
        if has_res:
            x_ref, g_ref, dh_ref, dres_ref, dx_ref, dx16_ref, dg_ref = refs
        else:
            x_ref, g_ref, dh_ref, dx_ref, dx16_ref, dg_ref = refs
        r, xh = _rms_stats(x_ref[...])
        dh_ = dh_ref[...].astype(F32)
        part = jnp.sum(dh_ * xh, axis=0, keepdims=True)

        @pl.when(pl.program_id(0) == 0)
        def _():
            dg_ref[...] = part

        @pl.when(pl.program_id(0) > 0)
        def _():
            dg_ref[...] += part

        dx = _rms_back(xh, r, g_ref[...], dh_)
        if has_res:
            dx = dx + dres_ref[...]
        dx_ref[...] = dx
        dx16_ref[...] = dx.astype(BF16)

    row_spec = pl.BlockSpec((tr, d), lambda i: (i, 0))
    vec_spec = pl.BlockSpec((1, d), lambda i: (0, 0))
    args = [x, g.reshape(1, d), dh] + ([dres] if has_res else [])
    return pl.pallas_call(
        body, out_shape=(jax.ShapeDtypeStruct((rows, d), F32), jax.ShapeDtypeStruct((rows, d), BF16),
                         jax.ShapeDtypeStruct((1, d), F32)),
        grid=(rows // tr,), in_specs=[row_spec, vec_spec, row_spec] + ([row_spec] if has_res else []),
        out_specs=(row_spec, row_spec, vec_spec), compiler_params=_cp("arbitrary"), name=name)(*args)


def final_loss(x, g, target, name="final_loss"):
    rows, d = x.shape
    tr = _pick(rows, (ROW_TILE, 128))

    def body(x_ref, g_ref, t_ref, loss_ref, dx_ref, dx16_ref, dg_ref):
        r, xh = _rms_stats(x_ref[...])
        gain = g_ref[...]
        err = xh * gain - t_ref[...]
        sq = jnp.sum(jnp.sum(err * err, axis=1, keepdims=True), axis=0, keepdims=True) * (0.5 / d)
        dy = err * (1.0 / d)
        part = jnp.sum(dy * xh, axis=0, keepdims=True)

        @pl.when(pl.program_id(0) == 0)
        def _():
            dg_ref[...] = part
            loss_ref[...] = jnp.broadcast_to(sq, loss_ref.shape)

        @pl.when(pl.program_id(0) > 0)
        def _():
            dg_ref[...] += part
            loss_ref[...] += jnp.broadcast_to(sq, loss_ref.shape)

        dx = _rms_back(xh, r, gain, dy)
        dx_ref[...] = dx
        dx16_ref[...] = dx.astype(BF16)

    row_spec = pl.BlockSpec((tr, d), lambda i: (i, 0))
    vec_spec = pl.BlockSpec((1, d), lambda i: (0, 0))
    return pl.pallas_call(
        body, out_shape=(jax.ShapeDtypeStruct((1, LANES), F32), jax.ShapeDtypeStruct((rows, d), F32),
                         jax.ShapeDtypeStruct((rows, d), BF16), jax.ShapeDtypeStruct((1, d), F32)),
        grid=(rows // tr,), in_specs=[row_spec, vec_spec, row_spec],
        out_specs=(pl.BlockSpec((1, LANES), lambda i: (0, 0)), row_spec, row_spec, vec_spec),
        compiler_params=_cp("arbitrary"), name=name)(x, g.reshape(1, d), target)


def _alibi_slopes():
    return (2.0 ** (-8.0 * (np.arange(12) + 1) / 12)).astype(np.float32)


def _band_scores(q, k, q0, start, wk, slope):
    s = lax.dot_general(q, k, (((1,), (1,)), ((), ())), preferred_element_type=F32) * (HEAD ** -0.5)
    qpos = q0 + lax.broadcasted_iota(jnp.int32, (QBLK, wk), 0)
    kpos = start + lax.broadcasted_iota(jnp.int32, (QBLK, wk), 1)
    rel = jnp.abs(qpos - kpos)
    return jnp.where(rel <= N_SIDE, s - slope * rel.astype(F32), NEG)


def _attn_geometry(seq, dilation):
    length = seq // dilation
    return length, length // QBLK, min(2 * QBLK, length)


def _attn_window(n, length, wk):
    q0 = pl.multiple_of(n * QBLK, QBLK)
    start = pl.multiple_of(jnp.clip(n * QBLK - N_SIDE, 0, length - wk), N_SIDE)
    return q0, start


def _class_in(refs, scratch, r, dilation, length):
    if dilation == 1:
        return refs
    for ref, buf in zip(refs, scratch):
        buf[...] = ref[pl.ds(r, length, stride=dilation), :]
    return scratch


def _class_out(refs, scratch, r, dilation, length):
    if dilation > 1:
        for ref, buf in zip(refs, scratch):
            ref[pl.ds(r, length, stride=dilation), :] = buf[...]


def attn_fwd(proj, group, name):
    seq = proj.shape[0]
    dilation = A_PATTERNS[group][1]
    length, nblk, wk = _attn_geometry(seq, dilation)

    def body(slope_ref, q_ref, k_ref, v_ref, o_ref, lse_ref, *scratch):
        slope = slope_ref[group * 4 + pl.program_id(0)] * float(dilation)
        for r in range(dilation):
            q_c, k_c, v_c = _class_in((q_ref, k_ref, v_ref), scratch[:3], r, dilation, length)
            o_c, lse_c = (o_ref, lse_ref) if dilation == 1 else scratch[3:]

            def blk(n, carry):
                q0, start = _attn_window(n, length, wk)
                q = q_c[pl.ds(q0, QBLK), :].astype(BF16)
                k = k_c[pl.ds(start, wk), :].astype(BF16)
                v = v_c[pl.ds(start, wk), :].astype(BF16)
                s = _band_scores(q, k, q0, start, wk, slope)
                m = jnp.max(s, axis=-1, keepdims=True)
                p = jnp.exp(s - m)
                l = jnp.sum(p, axis=-1, keepdims=True)
                o = jnp.dot(p.astype(BF16), v, preferred_element_type=F32) / l
                o_c[pl.ds(q0, QBLK), :] = o
                lse_c[pl.ds(q0, QBLK), :] = jnp.broadcast_to(m + jnp.log(l), (QBLK, HEAD))
                return carry

            lax.fori_loop(0, nblk, blk, 0)
            _class_out((o_ref, lse_ref), scratch[3:], r, dilation, length)

    def part(p):
        return pl.BlockSpec((seq, HEAD), lambda h: (0, p * 12 + group * 4 + h))

    out_spec = pl.BlockSpec((seq, HEAD), lambda h: (0, h))
    return pl.pallas_call(
        body, out_shape=(jax.ShapeDtypeStruct((seq, A_OUT_W), F32),) * 2, grid=(4,),
        in_specs=[pl.BlockSpec(memory_space=pltpu.SMEM), part(0), part(1), part(2)],
        out_specs=(out_spec, out_spec),
        scratch_shapes=[pltpu.VMEM((length, HEAD), F32)] * (5 if dilation > 1 else 0),
        compiler_params=_cp("parallel"), name=name)(jnp.asarray(_alibi_slopes()), proj, proj, proj)


def attn_combine(os_, lses, name="attn_combine"):
    seq = os_[0].shape[0]
    tr = ROW_TILE

    def body(o0, o1, o2, l0, l1, l2, c_ref, lse_ref):
        a, b, c = l0[...], l1[...], l2[...]
        m = jnp.maximum(jnp.maximum(a, b), c)
        ea, eb, ec = jnp.exp(a - m), jnp.exp(b - m), jnp.exp(c - m)
        den = ea + eb + ec
        c_ref[...] = (ea * o0[...] + eb * o1[...] + ec * o2[...]) / den
        lse_ref[...] = m + jnp.log(den)

    spec = pl.BlockSpec((tr, A_OUT_W), lambda i: (i, 0))
    return pl.pallas_call(
        body, out_shape=(jax.ShapeDtypeStruct((seq, A_OUT_W), F32),) * 2, grid=(seq // tr,),
        in_specs=[spec] * 6, out_specs=(spec, spec), compiler_params=_cp("parallel"), name=name)(*os_, *lses)


def attn_bwd(proj, dcat, comb, lse, group, name):
    seq = proj.shape[0]
    dilation = A_PATTERNS[group][1]
    length, nblk, wk = _attn_geometry(seq, dilation)
    scale = HEAD ** -0.5

    def body(slope_ref, q_ref, k_ref, v_ref, do_ref, c_ref, lse_ref, dq_ref, dk_ref, dv_ref, *scratch):
        slope = slope_ref[group * 4 + pl.program_id(0)] * float(dilation)
        for r in range(dilation):
            q_c, k_c, v_c, do_c, c_c, lse_c = _class_in((q_ref, k_ref, v_ref, do_ref, c_ref, lse_ref), scratch[:6], r,
                                                        dilation, length)
            dq_c, dk_c, dv_c = (dq_ref, dk_ref, dv_ref) if dilation == 1 else scratch[6:]
            dk_c[...] = jnp.zeros_like(dk_c)
            dv_c[...] = jnp.zeros_like(dv_c)

            def blk(n, carry):
                q0, start = _attn_window(n, length, wk)
                rows = pl.ds(q0, QBLK)
                keys = pl.ds(start, wk)
                q = q_c[rows, :].astype(BF16)
                k = k_c[keys, :].astype(BF16)
                v = v_c[keys, :].astype(BF16)
                do = do_c[rows, :]
                s = _band_scores(q, k, q0, start, wk, slope)
                p = jnp.exp(s - lse_c[rows, :][:, :1])
                delta = jnp.sum(do * c_c[rows, :], axis=-1, keepdims=True)
                do16 = do.astype(BF16)
                dp = lax.dot_general(do16, v, (((1,), (1,)), ((), ())), preferred_element_type=F32)
                ds = (p * (dp - delta) * scale).astype(BF16)
                p16 = p.astype(BF16)
                dq_c[rows, :] = jnp.dot(ds, k, preferred_element_type=F32)
                dk_c[keys, :] += lax.dot_general(ds, q, (((0,), (0,)), ((), ())), preferred_element_type=F32)
                dv_c[keys, :] += lax.dot_general(p16, do16, (((0,), (0,)), ((), ())), preferred_element_type=F32)
                return carry

            lax.fori_loop(0, nblk, blk, 0)
            _class_out((dq_ref, dk_ref, dv_ref), scratch[6:], r, dilation, length)

    def part(p):
        return pl.BlockSpec((seq, HEAD), lambda h: (0, p * 12 + group * 4 + h))

    hs = pl.BlockSpec((seq, HEAD), lambda h: (0, h))
    return pl.pallas_call(
        body, out_shape=(jax.ShapeDtypeStruct((seq, A_OUT_W), F32),) * 3, grid=(4,),
        in_specs=[pl.BlockSpec(memory_space=pltpu.SMEM), part(0), part(1), part(2), hs, hs, hs],
        out_specs=(hs, hs, hs),
        scratch_shapes=[pltpu.VMEM((length, HEAD), F32)] * (9 if dilation > 1 else 0),
        compiler_params=_cp("parallel"), name=name,
    )(jnp.asarray(_alibi_slopes()), proj, proj, proj, dcat, comb, lse)


MEM_ROW_TILE = 512


def _mem_probs(q, k):
    s = lax.dot_general(q, k, (((1,), (1,)), ((), ())), preferred_element_type=F32) * (HEAD ** -0.5)
    p = jnp.exp(s - jnp.max(s, axis=-1, keepdims=True))
    return p / jnp.sum(p, axis=-1, keepdims=True)


def mem_fwd(proj, q_col, kv, name="mem_fwd"):
    seq = proj.shape[0]
    qb = q_col // HEAD

    def body(q_ref, k_ref, v_ref, o_ref):
        p = _mem_probs(q_ref[...].astype(BF16), k_ref[...].astype(BF16))
        o_ref[...] = jnp.dot(p.astype(BF16), v_ref[...].astype(BF16), preferred_element_type=F32).astype(o_ref.dtype)

    return pl.pallas_call(
        body, out_shape=jax.ShapeDtypeStruct((seq, MEM_W), BF16), grid=(MEM_HEADS, seq // MEM_ROW_TILE),
        in_specs=[pl.BlockSpec((MEM_ROW_TILE, HEAD), lambda h, i: (i, qb + h)),
                  pl.BlockSpec((MEM_LEN, HEAD), lambda h, i: (0, h)),
                  pl.BlockSpec((MEM_LEN, HEAD), lambda h, i: (0, MEM_HEADS + h))],
        out_specs=pl.BlockSpec((MEM_ROW_TILE, HEAD), lambda h, i: (i, h)),
        compiler_params=_cp("parallel", "parallel"), name=name)(proj, kv, kv)


def mem_bwd(proj, q_col, kv, dcat, do_col, name="mem_bwd"):
    seq = proj.shape[0]
    qb, ob = q_col // HEAD, do_col // HEAD
    scale = HEAD ** -0.5

    def body(q_ref, k_ref, v_ref, do_ref, dq_ref, dk_ref, dv_ref):
        q = q_ref[...].astype(BF16)
        k = k_ref[...].astype(BF16)
        v = v_ref[...].astype(BF16)
        do = do_ref[...].astype(BF16)
        p = _mem_probs(q, k)
        dp = lax.dot_general(do, v, (((1,), (1,)), ((), ())), preferred_element_type=F32)
        ds = (p * (dp - jnp.sum(dp * p, axis=-1, keepdims=True)) * scale).astype(BF16)
        dq_ref[...] = jnp.dot(ds, k, preferred_element_type=F32).astype(dq_ref.dtype)
        dk = lax.dot_general(ds, q, (((0,), (0,)), ((), ())), preferred_element_type=F32)
        dv = lax.dot_general(p.astype(BF16), do, (((0,), (0,)), ((), ())), preferred_element_type=F32)

        @pl.when(pl.program_id(1) == 0)
        def _():
            dk_ref[...] = dk
            dv_ref[...] = dv

        @pl.when(pl.program_id(1) > 0)
        def _():
            dk_ref[...] += dk
            dv_ref[...] += dv

    dq, dk, dv = pl.pallas_call(
        body, out_shape=(jax.ShapeDtypeStruct((seq, MEM_W), BF16), jax.ShapeDtypeStruct((MEM_LEN, MEM_W), F32),
                         jax.ShapeDtypeStruct((MEM_LEN, MEM_W), F32)),
        grid=(MEM_HEADS, seq // MEM_ROW_TILE),
        in_specs=[pl.BlockSpec((MEM_ROW_TILE, HEAD), lambda h, i: (i, qb + h)),
                  pl.BlockSpec((MEM_LEN, HEAD), lambda h, i: (0, h)),
                  pl.BlockSpec((MEM_LEN, HEAD), lambda h, i: (0, MEM_HEADS + h)),
                  pl.BlockSpec((MEM_ROW_TILE, HEAD), lambda h, i: (i, ob + h))],
        out_specs=(pl.BlockSpec((MEM_ROW_TILE, HEAD), lambda h, i: (i, h)),
                   pl.BlockSpec((MEM_LEN, HEAD), lambda h, i: (0, h)),
                   pl.BlockSpec((MEM_LEN, HEAD), lambda h, i: (0, h))),
        compiler_params=_cp("parallel", "arbitrary"), name=name)(proj, kv, kv, dcat)
    return dq, jnp.concatenate([dk, dv], axis=1)


def _sgu_front(x, gain):
    uv, duv = _gelu_parts(x)
    u, v = uv[:, :B_W], uv[:, B_W:]
    r, vh = _rms_stats(v)
    return u, duv, r, vh, vh * gain


def sgu_fwd(proj, gain, w_s, bias_b, name="sgu_fwd"):
    seq = proj.shape[0]

    def body(x_ref, gain_ref, ws_ref, bias_ref, o_ref):
        u, _, _, _, vn = _sgu_front(x_ref[...], gain_ref[...])
        for g in range(B_GROUPS):
            cs = slice(g * CHUNK, (g + 1) * CHUNK)
            mixed = jnp.dot(ws_ref[g].astype(BF16), vn[:, cs].astype(BF16), preferred_element_type=F32) + bias_ref[g]
            o_ref[:, cs] = (u[:, cs] * mixed).astype(o_ref.dtype)

    full = lambda shape: pl.BlockSpec(shape, lambda c: (0,) * len(shape))
    return pl.pallas_call(
        body, out_shape=jax.ShapeDtypeStruct((seq, B_W), BF16), grid=(seq // CHUNK,),
        in_specs=[pl.BlockSpec((CHUNK, 2 * B_W), lambda c: (c, 0)), full((1, B_W)),
                  full((B_GROUPS, CHUNK, CHUNK)), full((B_GROUPS, CHUNK, CHUNK))],
        out_specs=pl.BlockSpec((CHUNK, B_W), lambda c: (c, 0)),
        compiler_params=_cp("parallel"), name=name)(proj, gain.reshape(1, B_W), w_s, bias_b)


def sgu_bwd(proj, gain, w_s, w_s_t, bias_b, dcat, name="sgu_bwd"):
    seq = proj.shape[0]

    def body(x_ref, gain_ref, ws_ref, wst_ref, bias_ref, do_ref, dx_ref, dws_ref, dmix_ref, dgain_ref, dvn_ref):
        first = pl.program_id(0) == 0
        gain = gain_ref[...]
        u, duv, r, vh, vn = _sgu_front(x_ref[...], gain)
        do = do_ref[...]
        for g in range(B_GROUPS):
            cs = slice(g * CHUNK, (g + 1) * CHUNK)
            vg = vn[:, cs].astype(BF16)
            mixed = jnp.dot(ws_ref[g].astype(BF16), vg, preferred_element_type=F32) + bias_ref[g]
            dx_ref[:, cs] = (do[:, cs] * mixed * duv[:, cs]).astype(dx_ref.dtype)
            dmixed = do[:, cs] * u[:, cs]
            dm16 = dmixed.astype(BF16)
            dws = lax.dot_general(dm16, vg, (((1,), (1,)), ((), ())), preferred_element_type=F32)
            dvn_ref[:, cs] = jnp.dot(wst_ref[g].astype(BF16), dm16, preferred_element_type=F32)

            @pl.when(first)
            def _():
                dws_ref[g] = dws
                dmix_ref[g] = dmixed

            @pl.when(jnp.logical_not(first))
            def _():
                dws_ref[g] += dws
                dmix_ref[g] += dmixed

        dvn = dvn_ref[...]
        dgain = jnp.sum(dvn * vh, axis=0, keepdims=True)

        @pl.when(first)
        def _():
            dgain_ref[...] = dgain

        @pl.when(jnp.logical_not(first))
        def _():
            dgain_ref[...] += dgain

        dv = _rms_back(vh, r, gain, dvn)
        dx_ref[:, B_W:] = (dv * duv[:, B_W:]).astype(dx_ref.dtype)

    full = lambda shape: pl.BlockSpec(shape, lambda c: (0,) * len(shape))
    mats = full((B_GROUPS, CHUNK, CHUNK))
    return pl.pallas_call(
        body, out_shape=(jax.ShapeDtypeStruct((seq, 2 * B_W), BF16), jax.ShapeDtypeStruct((B_GROUPS, CHUNK, CHUNK), F32),
                         jax.ShapeDtypeStruct((B_GROUPS, CHUNK, CHUNK), F32), jax.ShapeDtypeStruct((1, B_W), F32)),
        grid=(seq // CHUNK,),
        in_specs=[pl.BlockSpec((CHUNK, 2 * B_W), lambda c: (c, 0)), full((1, B_W)), mats, mats, mats,
                  pl.BlockSpec((CHUNK, B_W), lambda c: (c, 0))],
        out_specs=(pl.BlockSpec((CHUNK, 2 * B_W), lambda c: (c, 0)), mats, mats, full((1, B_W))),
        scratch_shapes=[pltpu.VMEM((CHUNK, B_W), F32)],
        compiler_params=_cp("arbitrary"), name=name)(proj, gain.reshape(1, B_W), w_s, w_s_t, bias_b, dcat)


FFN_COLS = 256
FFN_ROWS = 64
SUBLANES = 8


def _window(ref, r0, first, last):
    cols = ref.shape[1]
    pad = jnp.zeros((SUBLANES, cols), F32)
    if first:
        return jnp.concatenate([pad, ref[pl.ds(0, FFN_ROWS + SUBLANES), :]], axis=0)
    if last:
        return jnp.concatenate([ref[pl.ds(r0 - SUBLANES, FFN_ROWS + SUBLANES), :], pad], axis=0)
    return ref[pl.ds(pl.multiple_of(r0 - SUBLANES, SUBLANES), FFN_ROWS + 2 * SUBLANES), :]


def _taps(win):
    mid = slice(SUBLANES, SUBLANES + FFN_ROWS)
    return pltpu.roll(win, 1, 0)[mid], win[mid], pltpu.roll(win, win.shape[0] - 1, 0)[mid]


def _row_steps(seq, step, carry):
    n = seq // FFN_ROWS
    carry = step(0, True, False, carry)
    carry = lax.fori_loop(1, n - 1, lambda i, c: step(pl.multiple_of(i * FFN_ROWS, FFN_ROWS), False, False, c), carry)
    return step(seq - FFN_ROWS, False, True, carry)


def _conv3(taps, w, b):
    prev, cur, nxt = taps
    return prev * w[0:1] + cur * w[1:2] + nxt * w[2:3] + b


def _fold(x):
    return jnp.sum(x.reshape(FFN_ROWS // SUBLANES, SUBLANES, x.shape[1]), axis=0)


FFN_FWD_COLS = 256


def _taps_whole(a):
    n = a.shape[0]
    rows = lax.broadcasted_iota(jnp.int32, a.shape, 0)
    return (jnp.where(rows == 0, 0.0, pltpu.roll(a, 1, 0)), a, jnp.where(rows == n - 1, 0.0, pltpu.roll(a, n - 1, 0)))


def ffn_act_fwd(a, conv_w, conv_b, name="ffn_act_fwd"):
    seq = a.shape[0]
    nb = FF // FFN_FWD_COLS

    def body(ag_ref, av_ref, wg_ref, wv_ref, bg_ref, bv_ref, o_ref):
        gate = _conv3(_taps_whole(ag_ref[...]), wg_ref[...], bg_ref[...])
        val = _conv3(_taps_whole(av_ref[...]), wv_ref[...], bv_ref[...])
        o_ref[...] = (_gelu(gate) * val).astype(o_ref.dtype)

    col = lambda rows, off: pl.BlockSpec((rows, FFN_FWD_COLS), lambda j: (0, j + off))
    cb = conv_b.reshape(1, 2 * FF)
    return pl.pallas_call(
        body, out_shape=jax.ShapeDtypeStruct((seq, FF), BF16), grid=(nb,),
        in_specs=[col(seq, 0), col(seq, nb), col(3, 0), col(3, nb), col(1, 0), col(1, nb)],
        out_specs=col(seq, 0), compiler_params=_cp("parallel"), name=name)(a, a, conv_w, conv_w, cb, cb)


def ffn_act_bwd(a, conv_w, conv_b, dact, name="ffn_act_bwd"):
    seq = a.shape[0]
    nb = FF // FFN_COLS

    def body(ag_ref, av_ref, wg_ref, wv_ref, bg_ref, bv_ref, d_ref, dag_ref, dav_ref, dwg_ref, dwv_ref, dbg_ref, dbv_ref,
             dcg_ref, dcv_ref):
        wg, wv, bg, bv = wg_ref[...], wv_ref[...], bg_ref[...], bv_ref[...]

        def conv_grads(r0, first, last, sums):
            g_taps = _taps(_window(ag_ref, r0, first, last))
            v_taps = _taps(_window(av_ref, r0, first, last))
            act, dact_dgate = _gelu_parts(_conv3(g_taps, wg, bg))
            d = d_ref[pl.ds(r0, FFN_ROWS), :].astype(F32)
            dcg = d * _conv3(v_taps, wv, bv) * dact_dgate
            dcv = d * act
            dcg_ref[pl.ds(r0, FFN_ROWS), :] = dcg
            dcv_ref[pl.ds(r0, FFN_ROWS), :] = dcv
            new = [_fold(dcg)] + [_fold(dcg * t) for t in g_taps] + [_fold(dcv)] + [_fold(dcv * t) for t in v_taps]
            return tuple(s + n for s, n in zip(sums, new))

        zero = jnp.zeros((SUBLANES, FFN_COLS), F32)
        sums = _row_steps(seq, conv_grads, (zero,) * 8)
        total = [jnp.sum(s, axis=0, keepdims=True) for s in sums]
        dbg_ref[...] = total[0]
        dbv_ref[...] = total[4]
        for k in range(3):
            dwg_ref[k:k + 1, :] = total[1 + k]
            dwv_ref[k:k + 1, :] = total[5 + k]

        def conv_transpose(r0, first, last, carry):
            for dc_ref, w, da_ref in ((dcg_ref, wg, dag_ref), (dcv_ref, wv, dav_ref)):
                prev, cur, nxt = _taps(_window(dc_ref, r0, first, last))
                da_ref[pl.ds(r0, FFN_ROWS), :] = (nxt * w[0:1] + cur * w[1:2] + prev * w[2:3]).astype(da_ref.dtype)
            return carry

        _row_steps(seq, conv_transpose, 0)

    col = lambda rows, off: pl.BlockSpec((rows, FFN_COLS), lambda j: (0, j + off))
    cb = conv_b.reshape(1, 2 * FF)
    dag, dav, dwg, dwv, dbg, dbv = pl.pallas_call(
        body, out_shape=(jax.ShapeDtypeStruct((seq, FF), BF16),) * 2 + (jax.ShapeDtypeStruct((3, FF), F32),) * 2
        + (jax.ShapeDtypeStruct((1, FF), F32),) * 2, grid=(nb,),
        in_specs=[col(seq, 0), col(seq, nb), col(3, 0), col(3, nb), col(1, 0), col(1, nb), col(seq, 0)],
        out_specs=(col(seq, 0), col(seq, 0), col(3, 0), col(3, 0), col(1, 0), col(1, 0)),
        scratch_shapes=[pltpu.VMEM((seq, FFN_COLS), F32), pltpu.VMEM((seq, FFN_COLS), F32)],
        compiler_params=_cp("parallel"), name=name)(a, a, conv_w, conv_w, cb, cb, dact)
    cat = lambda p, q: jnp.concatenate([p, q], axis=1)
    return (dag, dav), cat(dwg, dwv), cat(dbg, dbv)


def _adam_math(w, g, m, v):
    m = ADAM_B1 * m + (1.0 - ADAM_B1) * g
    v = ADAM_B2 * v + (1.0 - ADAM_B2) * (g * g)
    m_hat = m / (1.0 - ADAM_B1 ** ADAM_STEP)
    v_hat = v / (1.0 - ADAM_B2 ** ADAM_STEP)
    return -ADAM_LR * (m_hat / (jnp.sqrt(v_hat) + ADAM_EPS) + ADAM_WD * w), m, v


def _row_tile(rows, cols):
    return _pick(rows, (256, 128, 64)) if cols <= 1024 else _pick(rows, (128, 64))


BF16_ROWS = 16
STREAM_BLOCK_BYTES = 3 * 1024 * 1024


def _stream_rows(rows, cols, itemsize):
    fits = [r for r in range(BF16_ROWS, rows + 1, BF16_ROWS) if rows % r == 0 and r * cols * itemsize <= STREAM_BLOCK_BYTES]
    return max(fits)


def adamw_layer(w_all, m_all, v_all, layer, g, prev, name):
    n, rows, cols = w_all.shape
    tr = _row_tile(rows, cols)

    def body(w_ref, m_ref, v_ref, g_ref, *rest):
        go_ref, d_ref, mo_ref, vo_ref = rest[-4:]
        g_ = g_ref[...]
        d, m_, v_ = _adam_math(w_ref[...], g_, m_ref[...], v_ref[...])
        go_ref[...] = g_
        d_ref[...] = d
        mo_ref[...] = m_
        vo_ref[...] = v_

    lay = pl.BlockSpec((None, tr, cols), lambda i: (layer, i, 0))
    in_specs = [lay, lay, lay, pl.BlockSpec((tr, cols), lambda i: (i, 0))]
    args = [w_all, m_all, v_all, g]
    aliases = {}
    if prev is not None:
        in_specs += [pl.BlockSpec(memory_space=pl.ANY)] * 4
        args += list(prev)
        aliases = {4 + k: k for k in range(4)}
    return pl.pallas_call(
        body, out_shape=(jax.ShapeDtypeStruct(w_all.shape, F32),) * 4, grid=(rows // tr,),
        in_specs=in_specs, out_specs=(lay,) * 4, input_output_aliases=aliases,
        compiler_params=_cp("parallel"), name=name)(*args)


def adamw_flat(w, g, m, v, name="adamw_small"):
    rows, cols = w.shape
    tr = _pick(rows, (128, 8))

    def body(w_ref, g_ref, m_ref, v_ref, d_ref, mo_ref, vo_ref):
        d_ref[...], mo_ref[...], vo_ref[...] = _adam_math(w_ref[...], g_ref[...], m_ref[...], v_ref[...])

    spec = pl.BlockSpec((tr, cols), lambda i: (i, 0))
    return pl.pallas_call(
        body, out_shape=(jax.ShapeDtypeStruct(w.shape, F32),) * 3, grid=(rows // tr,),
        in_specs=[spec] * 4, out_specs=(spec,) * 3, compiler_params=_cp("parallel"), name=name)(w, g, m, v)


def pair_sum(dw, got, core, name):
    _, rows, cols = dw.shape
    half = rows // 2
    tr = _stream_rows(half, cols, 2)
    nrb = half // tr

    def body(c_ref, a_ref, b_ref, o_ref):
        o_ref[...] = (a_ref[...].astype(F32) + b_ref[...].astype(F32)).astype(o_ref.dtype)

    return pl.pallas_call(
        body, out_shape=jax.ShapeDtypeStruct((N_CHIPS, half, cols), BF16),
        grid_spec=pltpu.PrefetchScalarGridSpec(
            num_scalar_prefetch=1, grid=(N_CHIPS, nrb),
            in_specs=[pl.BlockSpec((None, tr, cols), lambda s, i, c_ref: (s, c_ref[0] * nrb + i, 0)),
                      pl.BlockSpec((None, tr, cols), lambda s, i, c_ref: (s, i, 0))],
            out_specs=pl.BlockSpec((None, tr, cols), lambda s, i, c_ref: (s, i, 0))),
        compiler_params=_cp("parallel", "parallel"), name=name)(core, dw, got)


def chip_sum(own, parts, place, name):
    _, half, cols = parts.shape
    tr = _stream_rows(half, cols, 4)
    nrb = half // tr

    def body(p_ref, own_ref, a_ref, b_ref, c_ref, o_ref):
        o_ref[...] = ((own_ref[...].astype(F32) + a_ref[...].astype(F32)) + b_ref[...].astype(F32)) + c_ref[...].astype(F32)

    def slot(k):
        return pl.BlockSpec((None, tr, cols), lambda i, p: (jnp.bitwise_xor(p[0], k), i, 0))

    return pl.pallas_call(
        body, out_shape=jax.ShapeDtypeStruct((2 * half, cols), F32),
        grid_spec=pltpu.PrefetchScalarGridSpec(
            num_scalar_prefetch=1, grid=(nrb,), in_specs=[slot(0), slot(1), slot(2), slot(3)],
            out_specs=pl.BlockSpec((tr, cols), lambda i, p: (p[1] * nrb + i, 0))),
        compiler_params=_cp("parallel"), name=name)(place, own, parts, parts, parts)


def cast_to_slot(w_all, layer, place, name, after=None):
    _, rows, cols = w_all.shape
    tr = _stream_rows(rows, cols, 4)
    tied = [] if after is None else [after]

    def body(p_ref, w_ref, *rest):
        o_ref, token = rest[-2:]
        o_ref[...] = w_ref[...].astype(o_ref.dtype)
        token[...] = jnp.zeros_like(token)

    return pl.pallas_call(
        body, out_shape=(jax.ShapeDtypeStruct((N_CHIPS, rows, cols), BF16), TOKEN),
        grid_spec=pltpu.PrefetchScalarGridSpec(
            num_scalar_prefetch=1, grid=(rows // tr,),
            in_specs=[pl.BlockSpec((None, tr, cols), lambda i, p: (layer, i, 0))] + [ANY] * len(tied),
            out_specs=(pl.BlockSpec((None, tr, cols), lambda i, p: (p[0], i, 0)),
                       pl.BlockSpec(TOKEN.shape, lambda i, p: (0, 0)))),
        compiler_params=_cp("arbitrary"), name=name)(place, w_all, *tied)


ANY = pl.BlockSpec(memory_space=pl.ANY)


def _place():
    x, y, c = lax.axis_index("x"), lax.axis_index("y"), lax.axis_index("c")
    others = [(1 - x, y), (x, 1 - y), (1 - x, 1 - y)]
    return x, y, c, 2 * x + y, others


def _remote(src, dst, send_sem, recv_sem, dev):
    return pltpu.make_async_remote_copy(src_ref=src, dst_ref=dst, send_sem=send_sem, recv_sem=recv_sem,
                                        device_id=dev, device_id_type=MESH)


HBM = pl.BlockSpec(memory_space=pltpu.HBM)
SEM = pl.BlockSpec(memory_space=pltpu.SEMAPHORE)
EFFECT = pltpu.SideEffectType.DATAFLOW_SIDE_EFFECTING
TOKEN = jax.ShapeDtypeStruct((8, LANES), F32)


def _in_hbm(a):
    return pltpu.with_memory_space_constraint(a, pltpu.HBM)


def _gather_copies(bufs, send_sems, recv_sems):
    x, y, c, me, others = _place()
    out = []
    for w, buf in enumerate(bufs):
        half = buf.shape[1] // 2
        mine = pl.ds(c * half, half)
        for k, (ox, oy) in enumerate(others):
            sems = send_sems.at[3 * w + k], recv_sems.at[3 * w + k]
            out.append((_remote(buf.at[me, mine], buf.at[me, mine], *sems, (ox, oy, c)),
                        _remote(buf.at[me, mine], buf.at[2 * ox + oy, mine], *sems, (ox, oy, c))))
    return out


def _forward_copies(bufs, send_sems, recv_sems):
    x, y, c, me, others = _place()
    out = []
    for w, buf in enumerate(bufs):
        half = buf.shape[1] // 2
        mine, theirs = pl.ds(c * half, half), pl.ds((1 - c) * half, half)
        for k, (ox, oy) in enumerate(others):
            sems = send_sems.at[3 * w + k], recv_sems.at[3 * w + k]
            slot = 2 * ox + oy
            out.append((_remote(buf.at[slot, mine], buf.at[slot, mine], *sems, (x, y, 1 - c)),
                        _remote(buf.at[slot, mine], buf.at[slot, theirs], *sems, (x, y, 1 - c))))
    return out


def _join_copies(grads, send_sems, recv_sems):
    x, y, c, _, _ = _place()
    out = []
    for w, g in enumerate(grads):
        half = g.shape[0] // 2
        mine, theirs = pl.ds(c * half, half), pl.ds((1 - c) * half, half)
        sems = send_sems.at[w], recv_sems.at[w]
        out.append((_remote(g.at[mine], g.at[mine], *sems, (x, y, 1 - c)), _remote(g.at[mine], g.at[theirs], *sems, (x, y, 1 - c))))
    return out


def _slot_copies(bufs, send_sems, recv_sems):
    x, y, c, me, others = _place()
    out = []
    for w, buf in enumerate(bufs):
        for k, (ox, oy) in enumerate(others):
            sems = send_sems.at[3 * w + k], recv_sems.at[3 * w + k]
            out.append((_remote(buf.at[me], buf.at[me], *sems, (ox, oy, c)),
                        _remote(buf.at[me], buf.at[2 * ox + oy], *sems, (ox, oy, c))))
    return out


def _ring_copies(stage):
    def copies(bufs, send_sems, recv_sems):
        x, y, c, me, _ = _place()
        x_nb, y_nb = (1 - x, y, c), (x, 1 - y, c)
        slot_x, slot_y, slot_d = 2 * (1 - x) + y, 2 * x + (1 - y), 2 * (1 - x) + (1 - y)
        out = []
        for w, buf in enumerate(bufs):
            half = buf.shape[1] // 2
            sems = [(send_sems.at[2 * w + k], recv_sems.at[2 * w + k]) for k in range(2)]
            if stage == 1:
                mine = pl.ds(c * half, half)
                out += [(_remote(buf.at[me, mine], buf.at[me, mine], *sems[0], x_nb),
                         _remote(buf.at[me, mine], buf.at[slot_x, mine], *sems[0], x_nb)),
                        (_remote(buf.at[me, mine], buf.at[me, mine], *sems[1], y_nb),
                         _remote(buf.at[me, mine], buf.at[slot_y, mine], *sems[1], y_nb))]
            else:
                first = pl.ds(c * half, half // 2)
                second = pl.ds(c * half + half // 2, half // 2)
                out += [(_remote(buf.at[slot_x, first], buf.at[slot_x, first], *sems[0], y_nb),
                         _remote(buf.at[slot_x, first], buf.at[slot_d, first], *sems[0], y_nb)),
                        (_remote(buf.at[slot_y, second], buf.at[slot_y, second], *sems[1], x_nb),
                         _remote(buf.at[slot_y, second], buf.at[slot_d, second], *sems[1], x_nb))]
        return out
    return copies


IN_PLACE = dict(gather=(_gather_copies, 3), forward=(_forward_copies, 3), join=(_join_copies, 1), slots=(_slot_copies, 3),
                ring1=(_ring_copies(1), 2), ring2=(_ring_copies(2), 2))


def copies_start(kind, bufs, name, after=None):
    n = len(bufs)
    copies, per_buf = IN_PLACE[kind]
    tied = [] if after is None else [after]

    def body(*refs):
        ins, (send_sems, recv_sems), token = refs[:n], refs[n + len(tied):n + len(tied) + 2], refs[-1]
        for sent, _ in copies(ins, send_sems, recv_sems):
            sent.start()
        token[...] = jnp.zeros_like(token)

    outs = pl.pallas_call(
        body, name=name,
        out_shape=(pltpu.SemaphoreType.DMA((per_buf * n,)), pltpu.SemaphoreType.DMA((per_buf * n,)),
                   *[pltpu.HBM(b.shape, b.dtype) for b in bufs], TOKEN),
        in_specs=[HBM] * n + [ANY] * len(tied), out_specs=(SEM, SEM, *[HBM] * n, VM),
        input_output_aliases={w: 2 + w for w in range(n)},
        compiler_params=pltpu.CompilerParams(has_side_effects=EFFECT))(*[_in_hbm(b) for b in bufs], *tied)
    return outs[0], outs[1], list(outs[2:2 + n]), outs[-1]


def copies_wait(kind, send_sems, recv_sems, bufs, after, name):
    n = len(bufs)
    copies, _ = IN_PLACE[kind]

    def body(*refs):
        ins, (send_ref, recv_ref) = refs[:n], refs[n:n + 2]
        for sent, landed in copies(ins, send_ref, recv_ref):
            sent.wait_send()
            landed.wait_recv()

    return list(pl.pallas_call(
        body, name=name, out_shape=tuple(pltpu.HBM(b.shape, b.dtype) for b in bufs),
        in_specs=[HBM] * n + [SEM, SEM, ANY], out_specs=(HBM,) * n,
        input_output_aliases={w: w for w in range(n)},
        compiler_params=pltpu.CompilerParams(has_side_effects=EFFECT))(*bufs, send_sems, recv_sems, after))


def swap_halves(grads, name):
    n = len(grads)

    def body(*refs):
        ins, outs = refs[:n], refs[n:2 * n]
        send_sems, recv_sems = refs[2 * n:]
        x, y, c, _, _ = _place()
        copies = []
        for w in range(n):
            half = ins[w].shape[1] // 2
            cp = _remote(ins[w].at[:, pl.ds((1 - c) * half, half)], outs[w], send_sems.at[w], recv_sems.at[w], (x, y, 1 - c))
            cp.start()
            copies.append(cp)
        for cp in copies:
            cp.wait()

    return pl.pallas_call(
        body, out_shape=tuple(jax.ShapeDtypeStruct((N_CHIPS, g.shape[1] // 2, g.shape[2]), g.dtype) for g in grads),
        in_specs=[ANY] * n, out_specs=(ANY,) * n,
        scratch_shapes=[pltpu.SemaphoreType.DMA((n,)), pltpu.SemaphoreType.DMA((n,))], name=name)(*grads)


def _swap_copies(grads, lands, send_sems, recv_sems):
    x, y, c, _, _ = _place()
    out = []
    for w, (g, land) in enumerate(zip(grads, lands)):
        half = g.shape[1] // 2
        out.append(_remote(g.at[:, pl.ds((1 - c) * half, half)], land, send_sems.at[w], recv_sems.at[w], (x, y, 1 - c)))
    return out


def swap_start(grads, name):
    n = len(grads)

    def body(*refs):
        ins, lands, (send_sems, recv_sems), token = refs[:n], refs[n:2 * n], refs[2 * n:2 * n + 2], refs[-1]
        for cp in _swap_copies(ins, lands, send_sems, recv_sems):
            cp.start()
        token[...] = jnp.zeros_like(token)

    shapes = [(N_CHIPS, g.shape[1] // 2, g.shape[2]) for g in grads]
    zones = [_in_hbm(lax.empty(s, g.dtype)) for s, g in zip(shapes, grads)]
    outs = pl.pallas_call(
        body, name=name,
        out_shape=(pltpu.SemaphoreType.DMA((n,)), pltpu.SemaphoreType.DMA((n,)),
                   *[pltpu.HBM(g.shape, g.dtype) for g in grads], *[pltpu.HBM(s, g.dtype) for s, g in zip(shapes, grads)],
                   TOKEN),
        in_specs=[HBM] * (2 * n), out_specs=(SEM, SEM, *[HBM] * (2 * n), VM),
        input_output_aliases={w: 2 + w for w in range(2 * n)},
        compiler_params=pltpu.CompilerParams(has_side_effects=EFFECT))(*[_in_hbm(g) for g in grads], *zones)
    return outs[0], outs[1], list(outs[2:2 + n]), list(outs[2 + n:2 + 2 * n]), outs[-1]


def swap_wait(send_sems, recv_sems, grads, lands, after, name):
    n = len(grads)

    def body(*refs):
        ins, zones, (send_ref, recv_ref) = refs[:n], refs[n:2 * n], refs[2 * n:2 * n + 2]
        for cp in _swap_copies(ins, zones, send_ref, recv_ref):
            cp.wait_send()
            cp.wait_recv()

    outs = pl.pallas_call(
        body, name=name, out_shape=tuple(pltpu.HBM(a.shape, a.dtype) for a in list(grads) + list(lands)),
        in_specs=[HBM] * (2 * n) + [SEM, SEM, ANY], out_specs=(HBM,) * (2 * n),
        input_output_aliases={w: w for w in range(2 * n)},
        compiler_params=pltpu.CompilerParams(has_side_effects=EFFECT))(*grads, *lands, send_sems, recv_sems, after)
    return list(outs[:n]), list(outs[n:])


def _exchange_copies(sums, lands, send_sems, recv_sems):
    x, y, c, me, others = _place()
    out = []
    for w, (src, land) in enumerate(zip(sums, lands)):
        for k, (ox, oy) in enumerate(others):
            sems = send_sems.at[3 * w + k], recv_sems.at[3 * w + k]
            out.append((_remote(src.at[2 * ox + oy], land.at[me], *sems, (ox, oy, c)),
                        _remote(src.at[2 * ox + oy], land.at[2 * ox + oy], *sems, (ox, oy, c))))
    return out


def exchange_start(sums, name):
    n = len(sums)

    def body(*refs):
        ins, lands, (send_sems, recv_sems), token = refs[:n], refs[n:2 * n], refs[2 * n:2 * n + 2], refs[-1]
        for sent, _ in _exchange_copies(ins, lands, send_sems, recv_sems):
            sent.start()
        token[...] = jnp.zeros_like(token)

    zones = [_in_hbm(lax.empty(s.shape, s.dtype)) for s in sums]
    outs = pl.pallas_call(
        body, name=name,
        out_shape=(pltpu.SemaphoreType.DMA((3 * n,)), pltpu.SemaphoreType.DMA((3 * n,)),
                   *[pltpu.HBM(s.shape, s.dtype) for s in sums] * 2, TOKEN),
        in_specs=[HBM] * (2 * n), out_specs=(SEM, SEM, *[HBM] * (2 * n), VM),
        input_output_aliases={w: 2 + w for w in range(2 * n)},
        compiler_params=pltpu.CompilerParams(has_side_effects=EFFECT))(*[_in_hbm(s) for s in sums], *zones)
    return outs[0], outs[1], list(outs[2:2 + n]), list(outs[2 + n:2 + 2 * n]), outs[-1]


def exchange_wait(send_sems, recv_sems, sums, lands, after, name):
    n = len(sums)

    def body(*refs):
        ins, zones, (send_ref, recv_ref) = refs[:n], refs[n:2 * n], refs[2 * n:2 * n + 2]
        for sent, landed in _exchange_copies(ins, zones, send_ref, recv_ref):
            sent.wait_send()
            landed.wait_recv()

    outs = pl.pallas_call(
        body, name=name, out_shape=tuple(pltpu.HBM(s.shape, s.dtype) for s in sums) * 2,
        in_specs=[HBM] * (2 * n) + [SEM, SEM, ANY], out_specs=(HBM,) * (2 * n),
        input_output_aliases={w: w for w in range(2 * n)},
        compiler_params=pltpu.CompilerParams(has_side_effects=EFFECT))(*sums, *lands, send_sems, recv_sems, after)
    return list(outs[:n]), list(outs[n:])


VM = pl.BlockSpec(memory_space=pltpu.VMEM)


def small_allgather(buf, name="small_allgather"):
    def body(in_ref, out_ref, send_sems, recv_sems):
        x, y, c, me, others = _place()
        out_ref[me] = in_ref[...]
        copies = []
        for k, (ox, oy) in enumerate(others):
            cp = _remote(in_ref, out_ref.at[me], send_sems.at[k], recv_sems.at[k], (ox, oy, c))
            cp.start()
            copies.append(cp)
        for k, (ox, oy) in enumerate(others):
            landed = out_ref.at[2 * ox + oy]
            _remote(landed, landed, send_sems.at[k], recv_sems.at[k], (ox, oy, c)).wait_recv()
        for cp in copies:
            cp.wait_send()

    return pl.pallas_call(
        body, out_shape=jax.ShapeDtypeStruct((N_CHIPS,) + buf.shape, buf.dtype), in_specs=[VM], out_specs=VM,
        scratch_shapes=[pltpu.SemaphoreType.DMA((3,)), pltpu.SemaphoreType.DMA((3,))],
        compiler_params=pltpu.CompilerParams(vmem_limit_bytes=V7X_VMEM_LIMIT), name=name)(buf)


def pair_small(buf, after, name="pair_small"):
    def body(in_ref, after_ref, out_ref, sib_ref, send_sem, recv_sem):
        x, y, c, me, _ = _place()
        cp = _remote(in_ref, sib_ref, send_sem, recv_sem, (x, y, 1 - c))
        cp.start()
        cp.wait()
        out_ref[me] = in_ref[...] + sib_ref[...]

    return pl.pallas_call(
        body, out_shape=jax.ShapeDtypeStruct((N_CHIPS,) + buf.shape, buf.dtype), in_specs=[VM, ANY], out_specs=VM,
        scratch_shapes=[pltpu.VMEM(buf.shape, buf.dtype), pltpu.SemaphoreType.DMA, pltpu.SemaphoreType.DMA],
        compiler_params=pltpu.CompilerParams(vmem_limit_bytes=V7X_VMEM_LIMIT), name=name)(buf, after)


def sum_slots(slots, name="sum_slots"):
    _, rows, cols = slots.shape
    tr = _pick(rows, (128, 8))

    def body(s_ref, o_ref):
        o_ref[...] = ((s_ref[0] + s_ref[1]) + s_ref[2]) + s_ref[3]

    return pl.pallas_call(
        body, out_shape=jax.ShapeDtypeStruct((rows, cols), slots.dtype), grid=(rows // tr,),
        in_specs=[pl.BlockSpec((N_CHIPS, tr, cols), lambda i: (0, i, 0))],
        out_specs=pl.BlockSpec((tr, cols), lambda i: (i, 0)), compiler_params=_cp("parallel"), name=name)(slots)


def _pack_rows(arrays, row_multiple):
    flat = jnp.concatenate([a.reshape(-1) for a in arrays])
    rows = -(-flat.shape[0] // (LANES * row_multiple)) * row_multiple
    return jnp.pad(flat, (0, rows * LANES - flat.shape[0])).reshape(rows, LANES)


def _unpack_rows(buf, shapes):
    flat = buf.reshape(-1)
    out, at = [], 0
    for s in shapes:
        n = math.prod(s)
        out.append(flat[at:at + n].reshape(s))
        at += n
    return out


def _mixer_weights(i):
    j = i // 2
    mixer = "a" if i % 2 == 0 else "b"
    return [("w_mem_kv", i), (mixer + "_w_in", j), (mixer + "_w_out", j)]


def _ffn_weights(i):
    return [("ffn_w_up", i), ("ffn_w_down", i)]


def _mixer_fwd(i, x, mem, w, small, after):
    is_a = i % 2 == 0
    j = i // 2
    wkv, win, wout = w
    wkv = wkv.reshape(1, D_MODEL, 2 * MEM_W)
    h1 = rms_fwd(x, small["mix_norm_g"][i], BF16, name=f"mix_norm{i}", after=after)
    mem_n = rms_fwd(mem, small["mem_norm_g"][i], BF16, name=f"mem_norm{i}")
    kv = mm_nn(mem_n, wkv, F32, name=f"mem_kv{i}")
    proj = mm_nn(h1, win, F32, name=f"in_proj{i}")
    saved = dict(x0=x, h1=h1, mem_n=mem_n, kv=kv, proj=proj)
    if is_a:
        outs, lses = zip(*[attn_fwd(proj, g, name=f"attn_fwd{i}_{g}") for g in range(3)])
        comb, lse = attn_combine(outs, lses, name=f"attn_combine{i}")
        mem_out = mem_fwd(proj, 3 * A_QKV_W, kv, name=f"mem_fwd{i}")
        cat = jnp.concatenate([comb.astype(BF16), mem_out], axis=1)
        saved.update(comb=comb, lse=lse)
    else:
        wout = wout.reshape(1, B_W + MEM_W, D_MODEL)
        tok = sgu_fwd(proj, small["b_v_norm_g"][j], small["b_w_s"][j], small["bias_b"][j], name=f"sgu_fwd{i}")
        mem_out = mem_fwd(proj, 2 * B_W, kv, name=f"mem_fwd{i}")
        cat = jnp.concatenate([tok, mem_out], axis=1)
    x1 = mm_nn(cat, wout, F32, res=x, name=f"out_proj{i}")
    saved.update(cat=cat)
    return x1, saved


def _ffn_fwd(i, x1, w, small, after):
    wup, wdn = w
    h2 = rms_fwd(x1, small["ffn_norm_g"][i], BF16, name=f"ffn_norm{i}", after=after)
    a = mm_nn(h2, wup, F32, name=f"ffn_up{i}")
    act = ffn_act_fwd(a, small["ffn_conv_w"][i], small["ffn_conv_b"][i], name=f"ffn_act{i}")
    if callable(wdn):
        wdn = wdn(act)
    x2 = mm_nn(act, wdn.reshape(1, FF, D_MODEL), F32, res=x1, name=f"ffn_down{i}")
    return x2, dict(x1=x1, h2=h2, a=a, act=act, w=[wup, wdn])


def _ffn_bwd(i, dx2, w, small, sv, after):
    wup, wdn = w
    dx2, dx2_16 = dx2
    sg = {}
    dact = mm_nt(dx2_16, wdn.reshape(1, FF, D_MODEL), F32, name=f"d_act{i}", after=after)
    d_wdn = mm_tn(sv["act"], dx2_16, 1, BF16, name=f"d_wdown{i}").reshape(N_CHIPS, FF // N_CHIPS, D_MODEL)
    da, sg["ffn_conv_w"], sg["ffn_conv_b"] = ffn_act_bwd(sv["a"], small["ffn_conv_w"][i], small["ffn_conv_b"][i], dact,
                                                          name=f"ffn_act_bwd{i}")
    d_wup = mm_tn(sv["h2"], da, N_CHIPS, BF16, name=f"d_wup{i}")
    dh2 = mm_nt(da, wup, F32, name=f"d_h2_{i}")
    dx1, dx1_16, sg["ffn_norm_g"] = rms_bwd(sv["x1"], small["ffn_norm_g"][i], dh2, dres=dx2, name=f"ffn_norm_bwd{i}")
    return (dx1, dx1_16), [d_wup, d_wdn], sg


def _mixer_bwd(i, dx1, mem, w, small, sv, after):
    is_a = i % 2 == 0
    j = i // 2
    wkv, win, wout = w
    wkv = wkv.reshape(1, D_MODEL, 2 * MEM_W)
    dx1_32, dx1 = dx1
    sg = {}
    proj, kv = sv["proj"], sv["kv"]
    if is_a:
        dcat = mm_nt(dx1, wout, F32, name=f"d_cat{i}", after=after)
        d_wout = mm_tn(sv["cat"], dx1, N_CHIPS, BF16, name=f"d_wout{i}")
        dqm, dkv = mem_bwd(proj, 3 * A_QKV_W, kv, dcat, A_OUT_W, name=f"mem_bwd{i}")
        parts = [attn_bwd(proj, dcat, sv["comb"], sv["lse"], g, name=f"attn_bwd{i}_{g}") for g in range(3)]
        dproj = jnp.concatenate([parts[g][p].astype(BF16) for p in range(3) for g in range(3)] + [dqm], axis=1)
    else:
        dcat = mm_nt(dx1, wout.reshape(1, B_W + MEM_W, D_MODEL), F32, name=f"d_cat{i}", after=after)
        d_wout = mm_tn(sv["cat"], dx1, 1, BF16, name=f"d_wout{i}").reshape(N_CHIPS, (B_W + MEM_W) // N_CHIPS, D_MODEL)
        dqm, dkv = mem_bwd(proj, 2 * B_W, kv, dcat, B_W, name=f"mem_bwd{i}")
        w_s = small["b_w_s"][j]
        duv, sg["b_w_s"], dmix, sg["b_v_norm_g"] = sgu_bwd(proj, small["b_v_norm_g"][j], w_s, jnp.swapaxes(w_s, 1, 2),
                                                           small["bias_b"][j], dcat, name=f"sgu_bwd{i}")
        sg["b_s_bias"] = jnp.sum(dmix, axis=-1)
        dproj = jnp.concatenate([duv, dqm], axis=1)
    d_wkv = mm_tn(sv["mem_n"], dkv, 1, BF16, name=f"d_wkv{i}").reshape(N_CHIPS, D_MODEL // N_CHIPS, 2 * MEM_W)
    dmem_n = mm_nt(dkv, wkv, F32, name=f"d_mem_n{i}")
    _, _, sg["mem_norm_g"] = rms_bwd(mem, small["mem_norm_g"][i], dmem_n, name=f"mem_norm_bwd{i}")
    d_win = mm_tn(sv["h1"], dproj, N_CHIPS, BF16, name=f"d_win{i}")
    dh1 = mm_nt(dproj, win, F32, name=f"d_h1_{i}")
    dx0, dx0_16, sg["mix_norm_g"] = rms_bwd(sv["x0"], small["mix_norm_g"][i], dh1, dres=dx1_32, name=f"mix_norm_bwd{i}")
    return (dx0, dx0_16), [d_wkv, d_win, d_wout], sg


def _exchange_begin(grads, got, place, tag):
    sums = [pair_sum(g, o, place[1:], name=f"pair_sum{tag}_{k}") for k, (g, o) in enumerate(zip(grads, got))]
    send_sems, recv_sems, sums, lands, token = exchange_start(sums, name=f"exchange_start{tag}")
    return (send_sems, recv_sems, sums, lands), token


def _reduce_finish(started, place, after, tag):
    sums, parts = exchange_wait(*started, after, name=f"exchange_wait{tag}")
    halves = [chip_sum(s, p, place, name=f"chip_sum{tag}_{k}") for k, (s, p) in enumerate(zip(sums, parts))]
    return copies_start("join", halves, name=f"join_start_{tag}")


SMALL_SHARDED = ("b_v_norm_g", "ffn_conv_w")
SMALL_FULL_SHAPES = dict(mix_norm_g=(D_MODEL,), ffn_norm_g=(D_MODEL,), mem_norm_g=(D_MODEL,), b_v_norm_g=(B_W,),
                         b_w_s=(B_GROUPS, CHUNK, CHUNK), b_s_bias=(B_GROUPS, CHUNK), ffn_conv_w=(3, 2 * FF),
                         ffn_conv_b=(2 * FF,))
BIG = ("w_mem_kv", "a_w_in", "a_w_out", "b_w_in", "b_w_out", "ffn_w_up", "ffn_w_down")
WEIGHT_ORDER = ("mix_norm_g", "ffn_norm_g", "mem_norm_g", "w_mem_kv", "a_w_in", "a_w_out", "b_w_in", "b_v_norm_g", "b_w_s",
                "b_s_bias", "b_w_out", "ffn_w_up", "ffn_conv_w", "ffn_conv_b", "ffn_w_down", "final_norm_g")


def kernel(x, mem, mix_norm_g, ffn_norm_g, mem_norm_g, w_mem_kv, a_w_in, a_w_out, b_w_in, b_v_norm_g, b_w_s, b_s_bias, b_w_out, ffn_w_up, ffn_conv_w, ffn_conv_b, ffn_w_down, final_norm_g, loss_target, m_mix_norm_g, m_ffn_norm_g, m_mem_norm_g, m_w_mem_kv, m_a_w_in, m_a_w_out, m_b_w_in, m_b_v_norm_g, m_b_w_s, m_b_s_bias, m_b_w_out, m_ffn_w_up, m_ffn_conv_w, m_ffn_conv_b, m_ffn_w_down, m_final_norm_g, v_mix_norm_g, v_ffn_norm_g, v_mem_norm_g, v_w_mem_kv, v_a_w_in, v_a_w_out, v_b_w_in, v_b_v_norm_g, v_b_w_s, v_b_s_bias, v_b_w_out, v_ffn_w_up, v_ffn_conv_w, v_ffn_conv_b, v_ffn_w_down, v_final_norm_g):
    weights = dict(mix_norm_g=mix_norm_g, ffn_norm_g=ffn_norm_g, mem_norm_g=mem_norm_g, w_mem_kv=w_mem_kv, a_w_in=a_w_in,
                   a_w_out=a_w_out, b_w_in=b_w_in, b_v_norm_g=b_v_norm_g, b_w_s=b_w_s, b_s_bias=b_s_bias, b_w_out=b_w_out,
                   ffn_w_up=ffn_w_up, ffn_conv_w=ffn_conv_w, ffn_conv_b=ffn_conv_b, ffn_w_down=ffn_w_down,
                   final_norm_g=final_norm_g)
    mom1 = dict(mix_norm_g=m_mix_norm_g, ffn_norm_g=m_ffn_norm_g, mem_norm_g=m_mem_norm_g, w_mem_kv=m_w_mem_kv,
                a_w_in=m_a_w_in, a_w_out=m_a_w_out, b_w_in=m_b_w_in, b_v_norm_g=m_b_v_norm_g, b_w_s=m_b_w_s,
                b_s_bias=m_b_s_bias, b_w_out=m_b_w_out, ffn_w_up=m_ffn_w_up, ffn_conv_w=m_ffn_conv_w,
                ffn_conv_b=m_ffn_conv_b, ffn_w_down=m_ffn_w_down, final_norm_g=m_final_norm_g)
    mom2 = dict(mix_norm_g=v_mix_norm_g, ffn_norm_g=v_ffn_norm_g, mem_norm_g=v_mem_norm_g, w_mem_kv=v_w_mem_kv,
                a_w_in=v_a_w_in, a_w_out=v_a_w_out, b_w_in=v_b_w_in, b_v_norm_g=v_b_v_norm_g, b_w_s=v_b_w_s,
                b_s_bias=v_b_s_bias, b_w_out=v_b_w_out, ffn_w_up=v_ffn_w_up, ffn_conv_w=v_ffn_conv_w,
                ffn_conv_b=v_ffn_conv_b, ffn_w_down=v_ffn_w_down, final_norm_g=v_final_norm_g)
    chip = 2 * lax.axis_index("x") + lax.axis_index("y")
    place = jnp.stack([chip, lax.axis_index("c")]).astype(jnp.int32)
    x0, mem0, target = x[0], mem[0], loss_target[0]
    depth = DEPTH

    n_cw, n_vg = ffn_conv_w.size, b_v_norm_g.size
    gathered = small_allgather(_pack_rows([ffn_conv_w, b_v_norm_g], 8)).reshape(N_CHIPS, -1)
    conv_w_full = gathered[:, :n_cw].reshape(N_CHIPS, DEPTH, 3, 2 * FF // N_CHIPS).transpose(1, 2, 0, 3).reshape(DEPTH, 3, 2 * FF)
    vgain_full = gathered[:, n_cw:n_cw + n_vg].reshape(N_CHIPS, 2, B_W // N_CHIPS).transpose(1, 0, 2).reshape(2, B_W)
    small = dict(mix_norm_g=mix_norm_g, ffn_norm_g=ffn_norm_g, mem_norm_g=mem_norm_g, b_w_s=b_w_s, ffn_conv_b=ffn_conv_b,
                 ffn_conv_w=conv_w_full, b_v_norm_g=vgain_full,
                 bias_b=jnp.broadcast_to(b_s_bias[..., None], b_s_bias.shape + (CHUNK,)))

    half_layers = 2 * depth
    groups = [(_ffn_weights if b % 2 else _mixer_weights)(b // 2) for b in range(half_layers)]
    tags = [("f" if b % 2 else "m") + str(b // 2) for b in range(half_layers)]

    ring_from, early_from = 3, 6
    kind = lambda b: "gather" if b < ring_from else "ring1"

    def cast(b, after):
        bufs = []
        for n, l in groups[b]:
            buf, after = cast_to_slot(weights[n], l, place, name=f"cast_{n}{l}", after=after)
            bufs.append(buf)
        return bufs, after

    def start_gather(b, after):
        bufs = cast_early.pop(b) if b in cast_early else cast(b, after)[0]
        return copies_start(kind(b), bufs, name=f"{kind(b)}_start_{tags[b]}", after=after)

    def next_stage(b, now, new, after):
        send_sems, recv_sems, bufs, _ = flying.pop(b)
        bufs = copies_wait(now, send_sems, recv_sems, bufs, after, name=f"{now}_wait_{tags[b]}")
        flying[b] = copies_start(new, bufs, name=f"{new}_start_{tags[b]}")
        return flying[b][3]

    flying, cast_early, tie = {}, {}, gathered
    for b in range(3):
        if b == 1:
            up_down, _ = cast(b, tie)
            flying[b] = copies_start("gather", up_down[:1], name="gather_start_f0_up", after=tie)
            flying["down"] = copies_start("gather", up_down[1:], name="gather_start_f0_down", after=flying[b][3])
            tie = flying["down"][3]
        else:
            flying[b] = start_gather(b, tie)
            tie = flying[b][3]
    for b in range(3, half_layers):
        cast_early[b], tie = cast(b, tie)

    def late_down(after):
        send_sems, recv_sems, bufs, _ = flying.pop("down")
        bufs = copies_wait("gather", send_sems, recv_sems, bufs, after, name="gather_wait_f0_down")
        send_sems, recv_sems, bufs, token = copies_start("forward", bufs, name="forward_start_f0_down")
        return copies_wait("forward", send_sems, recv_sems, bufs, token, name="forward_wait_f0_down")[0]

    w_half, saved_half, h = [], [], x0
    for b in range(half_layers):
        behind = tie if b == 0 else h
        if b < early_from:
            behind = next_stage(b, "gather" if b < ring_from else "ring2", "forward", behind)
        w_half.append(copies_wait("forward", *flying.pop(b)[:3], behind, name=f"forward_wait_{tags[b]}"))
        ready, tokens = w_half[b][0], []
        if early_from <= b + 1 < half_layers:
            tokens.append(next_stage(b + 1, "ring2", "forward", ready))
        for g in (b + 1, b + 2):
            if g < half_layers and ((g == b + 1 and ring_from <= g < early_from) or (g == b + 2 and g >= early_from)):
                tokens.append(next_stage(g, "ring1", "ring2", ready))
        if b + 3 < half_layers:
            flying[b + 3] = start_gather(b + 3, sum(tokens[1:], tokens[0]) if tokens else ready)
            tokens.append(flying[b + 3][3])
        tie = sum(tokens[1:], tokens[0]) if tokens else None
        if b % 2 == 0:
            h, sv = _mixer_fwd(b // 2, h, mem0, w_half[b], small, tie)
        else:
            h, sv = _ffn_fwd(b // 2, h, w_half[b] + [late_down] if b == 1 else w_half[b], small, tie)
            w_half[b] = sv.pop("w")
        saved_half.append(sv)
    w_mix, w_ffn, saved_mix, saved_ffn = w_half[0::2], w_half[1::2], saved_half[0::2], saved_half[1::2]
    loss_row, *dh, d_final = final_loss(h, final_norm_g, target)
    loss = lax.psum(loss_row[0, 0], ("x", "y", "c"))

    names = [n for n in WEIGHT_ORDER if n not in BIG]
    small_g = {n: [None] * weights[n].shape[0] for n in names if n != "final_norm_g"}
    big_out = {n: None for n in BIG}

    def keep_small(i, sg):
        for n, g in sg.items():
            small_g[n][i if len(small_g[n]) == depth else i // 2] = g.reshape(SMALL_FULL_SHAPES[n])

    joining = []

    def update(after):
        (send_sems, recv_sems, halves, _), group, tag = joining.pop()
        for (n, l), g in zip(group, copies_wait("join", send_sems, recv_sems, halves, after, name=f"join_wait_{tag}")):
            big_out[n] = adamw_layer(weights[n], mom1[n], mom2[n], l, g, big_out[n], name=f"adamw_{n}{l}")

    def finish_reduce(started, group, after, tag):
        join = _reduce_finish(started, place, after, tag)
        if joining:
            update(join[3])
        joining.append((join, group, tag))

    half_layers = 2 * depth
    swapping, exchanging, tie = None, [], None
    for k in range(half_layers):
        i = depth - 1 - k // 2
        if k % 2 == 0:
            dh, big_g, sg = _ffn_bwd(i, dh, w_ffn[i], small, saved_ffn[i], tie)
            group, tag = _ffn_weights(i), f"f{i}"
        else:
            dh, big_g, sg = _mixer_bwd(i, dh, mem0, w_mix[i], small, saved_mix[i], tie)
            group, tag = _mixer_weights(i), f"m{i}"
        keep_small(i, sg)
        started_now, swap_now, tokens = [], None, []
        if k < half_layers - 2:
            *swap_now, token = swap_start(big_g, name=f"swap_start_{tag}")
            swap_now = (swap_now, group, tag)
        else:
            started, token = _exchange_begin(big_g, swap_halves(big_g, name=f"swap_halves_{tag}"), place, tag)
            started_now.append((started, group, tag))
        tokens.append(token)
        if swapping is not None:
            swap_args, old_group, old_tag = swapping
            grads, got = swap_wait(*swap_args, dh[0], name=f"swap_wait_{old_tag}")
            started, token = _exchange_begin(grads, got, place, old_tag)
            started_now.append((started, old_group, old_tag))
            tokens.append(token)
        tie = sum(tokens[1:], tokens[0])
        for started, old_group, old_tag in exchanging:
            finish_reduce(started, old_group, dh[0], old_tag)
        swapping, exchanging = swap_now, started_now
    for started, old_group, old_tag in exchanging:
        finish_reduce(started, old_group, big_out["ffn_w_down"][0], old_tag)

    full_g = {n: (d_final.reshape(-1) if n == "final_norm_g" else jnp.stack(small_g[n])) for n in names}
    shapes = [full_g[n].shape for n in names]
    flying_small = copies_start("slots", [pair_small(_pack_rows([full_g[n] for n in names], 8), tie)], name="small_start")
    update(flying_small[3])
    reduced = sum_slots(copies_wait("slots", *flying_small[:3], big_out["w_mem_kv"][0], name="small_wait")[0])
    summed = dict(zip(names, _unpack_rows(reduced, shapes)))
    for n in SMALL_SHARDED:
        width = weights[n].shape[-1]
        summed[n] = lax.dynamic_slice_in_dim(summed[n], chip * width, width, axis=summed[n].ndim - 1)
    own_shapes = [weights[n].shape for n in names]
    pack = lambda d: _pack_rows([d[n] for n in names], 128)
    small_out = [_unpack_rows(b, own_shapes) for b in adamw_flat(pack(weights), pack(summed), pack(mom1), pack(mom2))]
    outs = {}
    for k, n in enumerate(names):
        outs[n] = (summed[n], small_out[0][k], small_out[1][k], small_out[2][k])
    outs.update(big_out)
    return (loss, dh[0][None], *[outs[n][0] for n in WEIGHT_ORDER], *[outs[n][1] for n in WEIGHT_ORDER],
            *[outs[n][2] for n in WEIGHT_ORDER], *[outs[n][3] for n in WEIGHT_ORDER])
```
